```python
import math
import jax, jax.numpy as jnp
from jax import lax
import numpy as np

D_MODEL = 2048
BATCH = 8
SEQ = 4096
DEPTH = 2

MIX_WIDTH = D_MODEL
ATTN_WIDTH = MIX_WIDTH // 2
SSM_WIDTH = MIX_WIDTH - ATTN_WIDTH
HEAD_DIM = 64
N_Q_HEADS = ATTN_WIDTH // HEAD_DIM
KV_RATIO = 8
N_KV_HEADS = N_Q_HEADS // KV_RATIO
KV_DIM = N_KV_HEADS * HEAD_DIM
WINDOW = 128
SSM_GROUP = 16
N_SSM_GROUPS = SSM_WIDTH // SSM_GROUP
STATE = 64
IN_COLS = ATTN_WIDTH + 2 * KV_DIM + SSM_WIDTH
D_FF = ((8 * D_MODEL // 3 + 255) // 256) * 256
CONV_WIDTH = 3
EPS = 1e-6
NEG = -1e30

kernel_name = "hybrid_swa_s5_convffn_sandwich_adaln"


def rmsnorm(x, g):
    x32 = x.astype(jnp.float32)
    y = x32 * lax.rsqrt(jnp.mean(x32 * x32, axis=-1, keepdims=True) + EPS)
    return y.astype(x.dtype) * g


def sliding_window_attention(q, k, v, sinks):
    bsz, seq = q.shape[0], q.shape[1]
    nb = seq // WINDOW
    grp = N_Q_HEADS // N_KV_HEADS
    qb = q.reshape(bsz, nb, WINDOW, N_KV_HEADS, grp, HEAD_DIM).astype(jnp.float32)

    def band(t):
        tb = t.reshape(bsz, nb, WINDOW, N_KV_HEADS, HEAD_DIM)
        prev = jnp.pad(tb, ((0, 0), (1, 0), (0, 0), (0, 0), (0, 0)))[:, :-1]
        return jnp.concatenate([prev, tb], axis=2)

    kb = band(k).astype(jnp.float32)
    vb = band(v)
    s = jnp.einsum('bnqhgd,bnkhd->bnhgqk', qb, kb) * (HEAD_DIM ** -0.5)
    qi = jnp.arange(WINDOW)[:, None]
    kj = jnp.arange(2 * WINDOW)[None, :]
    in_band = (kj > qi) & (kj <= qi + WINDOW)
    blk = jnp.arange(nb)[:, None, None]
    valid = in_band[None] & ((blk * WINDOW + kj[None] - WINDOW) >= 0)
    s = jnp.where(valid[None, :, None, None], s, NEG)
    sink = sinks.astype(jnp.float32).reshape(1, 1, N_KV_HEADS, grp, 1, 1)
    m = jnp.maximum(jnp.max(s, axis=-1, keepdims=True), sink)
    e = jnp.exp(s - m)
    p = e / (jnp.sum(e, axis=-1, keepdims=True) + jnp.exp(sink - m))
    out = jnp.einsum('bnhgqk,bnkhd->bnqhgd', p.astype(v.dtype), vb)
    return out.reshape(bsz, seq, N_Q_HEADS * HEAD_DIM)


def s5_ssm(u, lam_re, lam_im, log_step, b_re, b_im, c_re, c_im, d_skip):
    bsz, seq = u.shape[0], u.shape[1]
    dtype = u.dtype
    u4 = u.reshape(bsz, seq, N_SSM_GROUPS, SSM_GROUP).astype(jnp.float32)
    lr = lam_re.astype(jnp.float32)
    li = lam_im.astype(jnp.float32)
    dt = jnp.exp(log_step.astype(jnp.float32))[:, None]
    mag = jnp.exp(lr * dt)
    ang = li * dt
    ab_re = mag * jnp.cos(ang)
    ab_im = mag * jnp.sin(ang)
    den = lr * lr + li * li
    f_re = ((ab_re - 1.0) * lr + ab_im * li) / den
    f_im = (ab_im * lr - (ab_re - 1.0) * li) / den
    br = b_re.astype(jnp.float32)
    bi = b_im.astype(jnp.float32)
    bb_re = f_re[..., None] * br - f_im[..., None] * bi
    bb_im = f_re[..., None] * bi + f_im[..., None] * br
    bu_re = jnp.einsum('bsgh,gph->bsgp', u4, bb_re)
    bu_im = jnp.einsum('bsgh,gph->bsgp', u4, bb_im)
    a_re = jnp.broadcast_to(ab_re[None, None], (1, seq, N_SSM_GROUPS, STATE))
    a_im = jnp.broadcast_to(ab_im[None, None], (1, seq, N_SSM_GROUPS, STATE))

    def combine(e1, e2):
        a1r, a1i, b1r, b1i = e1
        a2r, a2i, b2r, b2i = e2
        return (a2r * a1r - a2i * a1i,
                a2r * a1i + a2i * a1r,
                a2r * b1r - a2i * b1i + b2r,
                a2r * b1i + a2i * b1r + b2i)

    _, _, xr, xi = lax.associative_scan(combine, (a_re, a_im, bu_re, bu_im), axis=1)
    y = (jnp.einsum('bsgp,ghp->bsgh', xr, c_re.astype(jnp.float32))
         - jnp.einsum('bsgp,ghp->bsgh', xi, c_im.astype(jnp.float32))
         + d_skip.astype(jnp.float32)[None, None] * u4)
    return y.reshape(bsz, seq, SSM_WIDTH).astype(dtype)


def causal_depthwise_conv(h, w, b):
    seq = h.shape[1]
    hp = jnp.pad(h, ((0, 0), (CONV_WIDTH - 1, 0), (0, 0)))
    out = b
    for k in range(CONV_WIDTH):
        out = out + hp[:, k:k + seq] * w[k]
    return out


def _fwd_setup_inputs(seed: int = 0) -> dict:
    key = jax.random.key(seed)
    ks = jax.random.split(key, 32)
    nrm = jax.random.normal
    G, P, H = N_SSM_GROUPS, STATE, SSM_GROUP
    return {
        "x": nrm(ks[0], (BATCH, SEQ, D_MODEL), jnp.float32),
        "c": nrm(ks[1], (BATCH, D_MODEL), jnp.float32),
        "w_ada": nrm(ks[2], (DEPTH, D_MODEL, 6 * D_MODEL), jnp.float32) * (0.5 * D_MODEL ** -0.5),
        "b_ada": nrm(ks[3], (DEPTH, 6 * D_MODEL), jnp.float32) * 0.02,
        "g_pre_mix": 1.0 + 0.1 * nrm(ks[4], (DEPTH, D_MODEL), jnp.float32),
        "g_post_mix": 1.0 + 0.1 * nrm(ks[5], (DEPTH, D_MODEL), jnp.float32),
        "w_in": nrm(ks[6], (DEPTH, D_MODEL, IN_COLS), jnp.float32) * D_MODEL ** -0.5,
        "attn_sinks": nrm(ks[7], (DEPTH, N_Q_HEADS), jnp.float32),
        "lam_re": -0.5 + 0.01 * nrm(ks[8], (DEPTH, G, P), jnp.float32),
        "lam_im": jnp.pi * jnp.arange(P, dtype=jnp.float32)[None, None, :]
                  + 0.01 * nrm(ks[9], (DEPTH, G, P), jnp.float32),
        "log_step": jax.random.uniform(ks[10], (DEPTH, G), jnp.float32,
                                       minval=math.log(1e-3), maxval=math.log(1e-1)),
        "ssm_b_re": nrm(ks[11], (DEPTH, G, P, H), jnp.float32) * (2 * H) ** -0.5,
        "ssm_b_im": nrm(ks[12], (DEPTH, G, P, H), jnp.float32) * (2 * H) ** -0.5,
        "ssm_c_re": nrm(ks[13], (DEPTH, G, H, P), jnp.float32) * 0.5,
        "ssm_c_im": nrm(ks[14], (DEPTH, G, H, P), jnp.float32) * 0.5,
        "ssm_d": nrm(ks[15], (DEPTH, G, H), jnp.float32),
        "w_glu": nrm(ks[16], (DEPTH, SSM_WIDTH, SSM_WIDTH), jnp.float32) * SSM_WIDTH ** -0.5,
        "g_attn_out": 1.0 + 0.1 * nrm(ks[17], (DEPTH, ATTN_WIDTH), jnp.float32),
        "g_ssm_out": 1.0 + 0.1 * nrm(ks[18], (DEPTH, SSM_WIDTH), jnp.float32),
        "w_out": nrm(ks[19], (DEPTH, MIX_WIDTH, D_MODEL), jnp.float32) * MIX_WIDTH ** -0.5,
        "g_pre_ffn": 1.0 + 0.1 * nrm(ks[20], (DEPTH, D_MODEL), jnp.float32),
        "g_post_ffn": 1.0 + 0.1 * nrm(ks[21], (DEPTH, D_MODEL), jnp.float32),
        "w_up": nrm(ks[22], (DEPTH, D_MODEL, 2 * D_FF), jnp.float32) * D_MODEL ** -0.5,
        "conv_w": nrm(ks[23], (DEPTH, CONV_WIDTH, 2 * D_FF), jnp.float32) * CONV_WIDTH ** -0.5,
        "conv_b": nrm(ks[24], (DEPTH, 2 * D_FF), jnp.float32) * 0.01,
        "w_down": nrm(ks[25], (DEPTH, D_FF, D_MODEL), jnp.float32) * D_FF ** -0.5,
    }


def _fwd_reference(x, c, w_ada, b_ada, g_pre_mix, g_post_mix, w_in, attn_sinks, lam_re, lam_im,
              log_step, ssm_b_re, ssm_b_im, ssm_c_re, ssm_c_im, ssm_d, w_glu, g_attn_out,
              g_ssm_out, w_out, g_pre_ffn, g_post_ffn, w_up, conv_w, conv_b, w_down):
    bsz, seq = x.shape[0], x.shape[1]
    c_act = jax.nn.silu(c)
    for l in range(DEPTH):
        ada = c_act @ w_ada[l] + b_ada[l]
        sh_m, sc_m, gt_m, sh_f, sc_f, gt_f = [t[:, None, :] for t in jnp.split(ada, 6, axis=-1)]

        h = rmsnorm(x, g_pre_mix[l]) * (1.0 + sc_m) + sh_m
        proj = h @ w_in[l]
        q = proj[..., :ATTN_WIDTH].reshape(bsz, seq, N_Q_HEADS, HEAD_DIM)
        k = proj[..., ATTN_WIDTH:ATTN_WIDTH + KV_DIM].reshape(bsz, seq, N_KV_HEADS, HEAD_DIM)
        v = proj[..., ATTN_WIDTH + KV_DIM:ATTN_WIDTH + 2 * KV_DIM].reshape(bsz, seq, N_KV_HEADS, HEAD_DIM)
        u = proj[..., ATTN_WIDTH + 2 * KV_DIM:]

        attn = sliding_window_attention(q, k, v, attn_sinks[l])
        y = s5_ssm(u, lam_re[l], lam_im[l], log_step[l], ssm_b_re[l], ssm_b_im[l],
                   ssm_c_re[l], ssm_c_im[l], ssm_d[l])
        z = jax.nn.gelu(y, approximate=True)
        ssm = z * jax.nn.sigmoid(z @ w_glu[l])

        merged = jnp.concatenate([rmsnorm(attn, g_attn_out[l]), rmsnorm(ssm, g_ssm_out[l])], axis=-1)
        mix = merged @ w_out[l]
        x = x + (1.0 + gt_m) * rmsnorm(mix, g_post_mix[l])

        h = rmsnorm(x, g_pre_ffn[l]) * (1.0 + sc_f) + sh_f
        up = causal_depthwise_conv(h @ w_up[l], conv_w[l], conv_b[l])
        val, gate = up[..., :D_FF], up[..., D_FF:]
        ff = (jax.nn.gelu(gate, approximate=True) * val) @ w_down[l]
        x = x + (1.0 + gt_f) * rmsnorm(ff, g_post_ffn[l])
    return x


import jax as _jax
import jax.numpy as _jnp

TWIN_FORMAT = 'train_step'
FWD_PARAMS = ['x', 'c', 'w_ada', 'b_ada', 'g_pre_mix', 'g_post_mix', 'w_in', 'attn_sinks', 'lam_re', 'lam_im', 'log_step', 'ssm_b_re', 'ssm_b_im', 'ssm_c_re', 'ssm_c_im', 'ssm_d', 'w_glu', 'g_attn_out', 'g_ssm_out', 'w_out', 'g_pre_ffn', 'g_post_ffn', 'w_up', 'conv_w', 'conv_b', 'w_down']
TWIN_WEIGHTS = ['w_ada', 'b_ada', 'g_pre_mix', 'g_post_mix', 'w_in', 'attn_sinks', 'lam_re', 'lam_im', 'log_step', 'ssm_b_re', 'ssm_b_im', 'ssm_c_re', 'ssm_c_im', 'ssm_d', 'w_glu', 'g_attn_out', 'g_ssm_out', 'w_out', 'g_pre_ffn', 'g_post_ffn', 'w_up', 'conv_w', 'conv_b', 'w_down']
TWIN_DIFF_INPUT = 'x'
TWIN_INPUTS = ['x', 'c', 'w_ada', 'b_ada', 'g_pre_mix', 'g_post_mix', 'w_in', 'attn_sinks', 'lam_re', 'lam_im', 'log_step', 'ssm_b_re', 'ssm_b_im', 'ssm_c_re', 'ssm_c_im', 'ssm_d', 'w_glu', 'g_attn_out', 'g_ssm_out', 'w_out', 'g_pre_ffn', 'g_post_ffn', 'w_up', 'conv_w', 'conv_b', 'w_down', 'loss_target', 'm_w_ada', 'm_b_ada', 'm_g_pre_mix', 'm_g_post_mix', 'm_w_in', 'm_attn_sinks', 'm_lam_re', 'm_lam_im', 'm_log_step', 'm_ssm_b_re', 'm_ssm_b_im', 'm_ssm_c_re', 'm_ssm_c_im', 'm_ssm_d', 'm_w_glu', 'm_g_attn_out', 'm_g_ssm_out', 'm_w_out', 'm_g_pre_ffn', 'm_g_post_ffn', 'm_w_up', 'm_conv_w', 'm_conv_b', 'm_w_down', 'v_w_ada', 'v_b_ada', 'v_g_pre_mix', 'v_g_post_mix', 'v_w_in', 'v_attn_sinks', 'v_lam_re', 'v_lam_im', 'v_log_step', 'v_ssm_b_re', 'v_ssm_b_im', 'v_ssm_c_re', 'v_ssm_c_im', 'v_ssm_d', 'v_w_glu', 'v_g_attn_out', 'v_g_ssm_out', 'v_w_out', 'v_g_pre_ffn', 'v_g_post_ffn', 'v_w_up', 'v_conv_w', 'v_conv_b', 'v_w_down']
TWIN_OUTPUTS = ['loss', 'grad_x', 'grad_w_ada', 'grad_b_ada', 'grad_g_pre_mix', 'grad_g_post_mix', 'grad_w_in', 'grad_attn_sinks', 'grad_lam_re', 'grad_lam_im', 'grad_log_step', 'grad_ssm_b_re', 'grad_ssm_b_im', 'grad_ssm_c_re', 'grad_ssm_c_im', 'grad_ssm_d', 'grad_w_glu', 'grad_g_attn_out', 'grad_g_ssm_out', 'grad_w_out', 'grad_g_pre_ffn', 'grad_g_post_ffn', 'grad_w_up', 'grad_conv_w', 'grad_conv_b', 'grad_w_down', 'delta_w_ada', 'delta_b_ada', 'delta_g_pre_mix', 'delta_g_post_mix', 'delta_w_in', 'delta_attn_sinks', 'delta_lam_re', 'delta_lam_im', 'delta_log_step', 'delta_ssm_b_re', 'delta_ssm_b_im', 'delta_ssm_c_re', 'delta_ssm_c_im', 'delta_ssm_d', 'delta_w_glu', 'delta_g_attn_out', 'delta_g_ssm_out', 'delta_w_out', 'delta_g_pre_ffn', 'delta_g_post_ffn', 'delta_w_up', 'delta_conv_w', 'delta_conv_b', 'delta_w_down', 'new_m_w_ada', 'new_m_b_ada', 'new_m_g_pre_mix', 'new_m_g_post_mix', 'new_m_w_in', 'new_m_attn_sinks', 'new_m_lam_re', 'new_m_lam_im', 'new_m_log_step', 'new_m_ssm_b_re', 'new_m_ssm_b_im', 'new_m_ssm_c_re', 'new_m_ssm_c_im', 'new_m_ssm_d', 'new_m_w_glu', 'new_m_g_attn_out', 'new_m_g_ssm_out', 'new_m_w_out', 'new_m_g_pre_ffn', 'new_m_g_post_ffn', 'new_m_w_up', 'new_m_conv_w', 'new_m_conv_b', 'new_m_w_down', 'new_v_w_ada', 'new_v_b_ada', 'new_v_g_pre_mix', 'new_v_g_post_mix', 'new_v_w_in', 'new_v_attn_sinks', 'new_v_lam_re', 'new_v_lam_im', 'new_v_log_step', 'new_v_ssm_b_re', 'new_v_ssm_b_im', 'new_v_ssm_c_re', 'new_v_ssm_c_im', 'new_v_ssm_d', 'new_v_w_glu', 'new_v_g_attn_out', 'new_v_g_ssm_out', 'new_v_w_out', 'new_v_g_pre_ffn', 'new_v_g_post_ffn', 'new_v_w_up', 'new_v_conv_w', 'new_v_conv_b', 'new_v_w_down']
TWIN_LEAF_KINDS = {'loss': 'loss', 'grad_x': 'grad_x', 'grad_w_ada': 'grad_w', 'grad_b_ada': 'grad_w', 'grad_g_pre_mix': 'grad_w', 'grad_g_post_mix': 'grad_w', 'grad_w_in': 'grad_w', 'grad_attn_sinks': 'grad_w', 'grad_lam_re': 'grad_w', 'grad_lam_im': 'grad_w', 'grad_log_step': 'grad_w', 'grad_ssm_b_re': 'grad_w', 'grad_ssm_b_im': 'grad_w', 'grad_ssm_c_re': 'grad_w', 'grad_ssm_c_im': 'grad_w', 'grad_ssm_d': 'grad_w', 'grad_w_glu': 'grad_w', 'grad_g_attn_out': 'grad_w', 'grad_g_ssm_out': 'grad_w', 'grad_w_out': 'grad_w', 'grad_g_pre_ffn': 'grad_w', 'grad_g_post_ffn': 'grad_w', 'grad_w_up': 'grad_w', 'grad_conv_w': 'grad_w', 'grad_conv_b': 'grad_w', 'grad_w_down': 'grad_w', 'delta_w_ada': 'delta_w', 'delta_b_ada': 'delta_w', 'delta_g_pre_mix': 'delta_w', 'delta_g_post_mix': 'delta_w', 'delta_w_in': 'delta_w', 'delta_attn_sinks': 'delta_w', 'delta_lam_re': 'delta_w', 'delta_lam_im': 'delta_w', 'delta_log_step': 'delta_w', 'delta_ssm_b_re': 'delta_w', 'delta_ssm_b_im': 'delta_w', 'delta_ssm_c_re': 'delta_w', 'delta_ssm_c_im': 'delta_w', 'delta_ssm_d': 'delta_w', 'delta_w_glu': 'delta_w', 'delta_g_attn_out': 'delta_w', 'delta_g_ssm_out': 'delta_w', 'delta_w_out': 'delta_w', 'delta_g_pre_ffn': 'delta_w', 'delta_g_post_ffn': 'delta_w', 'delta_w_up': 'delta_w', 'delta_conv_w': 'delta_w', 'delta_conv_b': 'delta_w', 'delta_w_down': 'delta_w', 'new_m_w_ada': 'new_m', 'new_m_b_ada': 'new_m', 'new_m_g_pre_mix': 'new_m', 'new_m_g_post_mix': 'new_m', 'new_m_w_in': 'new_m', 'new_m_attn_sinks': 'new_m', 'new_m_lam_re': 'new_m', 'new_m_lam_im': 'new_m', 'new_m_log_step': 'new_m', 'new_m_ssm_b_re': 'new_m', 'new_m_ssm_b_im': 'new_m', 'new_m_ssm_c_re': 'new_m', 'new_m_ssm_c_im': 'new_m', 'new_m_ssm_d': 'new_m', 'new_m_w_glu': 'new_m', 'new_m_g_attn_out': 'new_m', 'new_m_g_ssm_out': 'new_m', 'new_m_w_out': 'new_m', 'new_m_g_pre_ffn': 'new_m', 'new_m_g_post_ffn': 'new_m', 'new_m_w_up': 'new_m', 'new_m_conv_w': 'new_m', 'new_m_conv_b': 'new_m', 'new_m_w_down': 'new_m', 'new_v_w_ada': 'new_v', 'new_v_b_ada': 'new_v', 'new_v_g_pre_mix': 'new_v', 'new_v_g_post_mix': 'new_v', 'new_v_w_in': 'new_v', 'new_v_attn_sinks': 'new_v', 'new_v_lam_re': 'new_v', 'new_v_lam_im': 'new_v', 'new_v_log_step': 'new_v', 'new_v_ssm_b_re': 'new_v', 'new_v_ssm_b_im': 'new_v', 'new_v_ssm_c_re': 'new_v', 'new_v_ssm_c_im': 'new_v', 'new_v_ssm_d': 'new_v', 'new_v_w_glu': 'new_v', 'new_v_g_attn_out': 'new_v', 'new_v_g_ssm_out': 'new_v', 'new_v_w_out': 'new_v', 'new_v_g_pre_ffn': 'new_v', 'new_v_g_post_ffn': 'new_v', 'new_v_w_up': 'new_v', 'new_v_conv_w': 'new_v', 'new_v_conv_b': 'new_v', 'new_v_w_down': 'new_v'}


def _forward(args):
    return _fwd_reference(*[args[k] for k in FWD_PARAMS])


def _output_shape():
    out = _jax.eval_shape(lambda: _forward(_fwd_setup_inputs(0)))
    return out.shape, out.dtype

N_MICROBATCH = 1
ADAM_LR = 0.001
ADAM_B1 = 0.9
ADAM_B2 = 0.999
ADAM_EPS = 1e-08
ADAM_WD = 0.01
ADAM_STEP = 10
PER_EXAMPLE_BATCH_AXIS = {'x': 0, 'c': 0, 'loss_target': 0}
SHARED_INPUTS = []
_WEIGHT_DTYPES = {'w_ada': _jnp.float32, 'b_ada': _jnp.float32, 'g_pre_mix': _jnp.float32, 'g_post_mix': _jnp.float32, 'w_in': _jnp.float32, 'attn_sinks': _jnp.float32, 'lam_re': _jnp.float32, 'lam_im': _jnp.float32, 'log_step': _jnp.float32, 'ssm_b_re': _jnp.float32, 'ssm_b_im': _jnp.float32, 'ssm_c_re': _jnp.float32, 'ssm_c_im': _jnp.float32, 'ssm_d': _jnp.float32, 'w_glu': _jnp.float32, 'g_attn_out': _jnp.float32, 'g_ssm_out': _jnp.float32, 'w_out': _jnp.float32, 'g_pre_ffn': _jnp.float32, 'g_post_ffn': _jnp.float32, 'w_up': _jnp.float32, 'conv_w': _jnp.float32, 'conv_b': _jnp.float32, 'w_down': _jnp.float32}
MOMENT_SCALE = {'w_ada': 7.966622e+00, 'b_ada': 1.726026e+01, 'g_pre_mix': 4.223028e+00, 'g_post_mix': 2.347743e+01, 'w_in': 8.373456e+00, 'attn_sinks': 3.007964e-01, 'lam_re': 1.817264e+00, 'lam_im': 1.608317e+00, 'log_step': 5.284565e+01, 'ssm_b_re': 1.247649e+00, 'ssm_b_im': 1.129018e+00, 'ssm_c_re': 4.284029e-01, 'ssm_c_im': 4.164781e-01, 'ssm_d': 7.773227e+00, 'w_glu': 1.781361e+00, 'g_attn_out': 1.129521e+01, 'g_ssm_out': 8.596675e+00, 'w_out': 1.013329e+01, 'g_pre_ffn': 5.573542e+00, 'g_post_ffn': 1.922597e+01, 'w_up': 2.620930e+00, 'conv_w': 2.787860e+00, 'conv_b': 4.326834e+00, 'w_down': 5.142433e+00}


def _to_microbatches(a, axis):
    t = _jnp.moveaxis(a, axis, 0)
    t = t.reshape((N_MICROBATCH, t.shape[0] // N_MICROBATCH) + t.shape[1:])
    return _jnp.moveaxis(t, 1, axis + 1)


def setup_inputs(seed: int = 0) -> dict:
    inp = _fwd_setup_inputs(seed)
    key = _jax.random.fold_in(_jax.random.key(seed), 7919)
    shape, _ = _output_shape()
    out = dict(inp)
    out["loss_target"] = _jax.random.normal(_jax.random.fold_in(key, 0), shape, _jnp.float32)
    for i, name in enumerate(TWIN_WEIGHTS):
        w = inp[name].astype(_jnp.float32)
        if MOMENT_SCALE is None:
            s = _jnp.sqrt(_jnp.mean(_jnp.square(w)) + 1e-30)
        else:
            s = MOMENT_SCALE[name]
        km, kv = _jax.random.split(_jax.random.fold_in(key, i + 1))
        out[name] = w
        out["m_" + name] = s * _jax.random.normal(km, w.shape, _jnp.float32)
        out["v_" + name] = (s * s) * _jax.random.uniform(kv, w.shape, _jnp.float32, 0.5, 1.5)
    if N_MICROBATCH > 1:
        for name, axis in PER_EXAMPLE_BATCH_AXIS.items():
            out[name] = _to_microbatches(out[name], axis)
    return {'x': out['x'], 'c': out['c'], 'w_ada': out['w_ada'], 'b_ada': out['b_ada'], 'g_pre_mix': out['g_pre_mix'], 'g_post_mix': out['g_post_mix'], 'w_in': out['w_in'], 'attn_sinks': out['attn_sinks'], 'lam_re': out['lam_re'], 'lam_im': out['lam_im'], 'log_step': out['log_step'], 'ssm_b_re': out['ssm_b_re'], 'ssm_b_im': out['ssm_b_im'], 'ssm_c_re': out['ssm_c_re'], 'ssm_c_im': out['ssm_c_im'], 'ssm_d': out['ssm_d'], 'w_glu': out['w_glu'], 'g_attn_out': out['g_attn_out'], 'g_ssm_out': out['g_ssm_out'], 'w_out': out['w_out'], 'g_pre_ffn': out['g_pre_ffn'], 'g_post_ffn': out['g_post_ffn'], 'w_up': out['w_up'], 'conv_w': out['conv_w'], 'conv_b': out['conv_b'], 'w_down': out['w_down'], 'loss_target': out['loss_target'], 'm_w_ada': out['m_w_ada'], 'm_b_ada': out['m_b_ada'], 'm_g_pre_mix': out['m_g_pre_mix'], 'm_g_post_mix': out['m_g_post_mix'], 'm_w_in': out['m_w_in'], 'm_attn_sinks': out['m_attn_sinks'], 'm_lam_re': out['m_lam_re'], 'm_lam_im': out['m_lam_im'], 'm_log_step': out['m_log_step'], 'm_ssm_b_re': out['m_ssm_b_re'], 'm_ssm_b_im': out['m_ssm_b_im'], 'm_ssm_c_re': out['m_ssm_c_re'], 'm_ssm_c_im': out['m_ssm_c_im'], 'm_ssm_d': out['m_ssm_d'], 'm_w_glu': out['m_w_glu'], 'm_g_attn_out': out['m_g_attn_out'], 'm_g_ssm_out': out['m_g_ssm_out'], 'm_w_out': out['m_w_out'], 'm_g_pre_ffn': out['m_g_pre_ffn'], 'm_g_post_ffn': out['m_g_post_ffn'], 'm_w_up': out['m_w_up'], 'm_conv_w': out['m_conv_w'], 'm_conv_b': out['m_conv_b'], 'm_w_down': out['m_w_down'], 'v_w_ada': out['v_w_ada'], 'v_b_ada': out['v_b_ada'], 'v_g_pre_mix': out['v_g_pre_mix'], 'v_g_post_mix': out['v_g_post_mix'], 'v_w_in': out['v_w_in'], 'v_attn_sinks': out['v_attn_sinks'], 'v_lam_re': out['v_lam_re'], 'v_lam_im': out['v_lam_im'], 'v_log_step': out['v_log_step'], 'v_ssm_b_re': out['v_ssm_b_re'], 'v_ssm_b_im': out['v_ssm_b_im'], 'v_ssm_c_re': out['v_ssm_c_re'], 'v_ssm_c_im': out['v_ssm_c_im'], 'v_ssm_d': out['v_ssm_d'], 'v_w_glu': out['v_w_glu'], 'v_g_attn_out': out['v_g_attn_out'], 'v_g_ssm_out': out['v_g_ssm_out'], 'v_w_out': out['v_w_out'], 'v_g_pre_ffn': out['v_g_pre_ffn'], 'v_g_post_ffn': out['v_g_post_ffn'], 'v_w_up': out['v_w_up'], 'v_conv_w': out['v_conv_w'], 'v_conv_b': out['v_conv_b'], 'v_w_down': out['v_w_down']}


def _loss(weights, diff, rest, loss_target):
    with _jax.named_scope("forward"):
        args = {**rest, TWIN_DIFF_INPUT: diff, **{k: w.astype(_WEIGHT_DTYPES[k]) for k, w in weights.items()}}
        y = _forward(args)
    with _jax.named_scope("loss_head"):
        err = _jnp.square(y.astype(_jnp.float32) - loss_target)
        return 0.5 * _jnp.sum(_jnp.mean(err, axis=-1)) if err.ndim else 0.5 * err


def _adamw(w, g, m, v):
    m = ADAM_B1 * m + (1.0 - ADAM_B1) * g
    v = ADAM_B2 * v + (1.0 - ADAM_B2) * _jnp.square(g)
    m_hat = m / (1.0 - ADAM_B1 ** ADAM_STEP)
    v_hat = v / (1.0 - ADAM_B2 ** ADAM_STEP)
    delta = -ADAM_LR * (m_hat / (_jnp.sqrt(v_hat) + ADAM_EPS) + ADAM_WD * w)
    return delta, m, v


def reference(x, c, w_ada, b_ada, g_pre_mix, g_post_mix, w_in, attn_sinks, lam_re, lam_im, log_step, ssm_b_re, ssm_b_im, ssm_c_re, ssm_c_im, ssm_d, w_glu, g_attn_out, g_ssm_out, w_out, g_pre_ffn, g_post_ffn, w_up, conv_w, conv_b, w_down, loss_target, m_w_ada, m_b_ada, m_g_pre_mix, m_g_post_mix, m_w_in, m_attn_sinks, m_lam_re, m_lam_im, m_log_step, m_ssm_b_re, m_ssm_b_im, m_ssm_c_re, m_ssm_c_im, m_ssm_d, m_w_glu, m_g_attn_out, m_g_ssm_out, m_w_out, m_g_pre_ffn, m_g_post_ffn, m_w_up, m_conv_w, m_conv_b, m_w_down, v_w_ada, v_b_ada, v_g_pre_mix, v_g_post_mix, v_w_in, v_attn_sinks, v_lam_re, v_lam_im, v_log_step, v_ssm_b_re, v_ssm_b_im, v_ssm_c_re, v_ssm_c_im, v_ssm_d, v_w_glu, v_g_attn_out, v_g_ssm_out, v_w_out, v_g_pre_ffn, v_g_post_ffn, v_w_up, v_conv_w, v_conv_b, v_w_down):
    given = dict(x=x, c=c, w_ada=w_ada, b_ada=b_ada, g_pre_mix=g_pre_mix, g_post_mix=g_post_mix, w_in=w_in, attn_sinks=attn_sinks, lam_re=lam_re, lam_im=lam_im, log_step=log_step, ssm_b_re=ssm_b_re, ssm_b_im=ssm_b_im, ssm_c_re=ssm_c_re, ssm_c_im=ssm_c_im, ssm_d=ssm_d, w_glu=w_glu, g_attn_out=g_attn_out, g_ssm_out=g_ssm_out, w_out=w_out, g_pre_ffn=g_pre_ffn, g_post_ffn=g_post_ffn, w_up=w_up, conv_w=conv_w, conv_b=conv_b, w_down=w_down, loss_target=loss_target, m_w_ada=m_w_ada, m_b_ada=m_b_ada, m_g_pre_mix=m_g_pre_mix, m_g_post_mix=m_g_post_mix, m_w_in=m_w_in, m_attn_sinks=m_attn_sinks, m_lam_re=m_lam_re, m_lam_im=m_lam_im, m_log_step=m_log_step, m_ssm_b_re=m_ssm_b_re, m_ssm_b_im=m_ssm_b_im, m_ssm_c_re=m_ssm_c_re, m_ssm_c_im=m_ssm_c_im, m_ssm_d=m_ssm_d, m_w_glu=m_w_glu, m_g_attn_out=m_g_attn_out, m_g_ssm_out=m_g_ssm_out, m_w_out=m_w_out, m_g_pre_ffn=m_g_pre_ffn, m_g_post_ffn=m_g_post_ffn, m_w_up=m_w_up, m_conv_w=m_conv_w, m_conv_b=m_conv_b, m_w_down=m_w_down, v_w_ada=v_w_ada, v_b_ada=v_b_ada, v_g_pre_mix=v_g_pre_mix, v_g_post_mix=v_g_post_mix, v_w_in=v_w_in, v_attn_sinks=v_attn_sinks, v_lam_re=v_lam_re, v_lam_im=v_lam_im, v_log_step=v_log_step, v_ssm_b_re=v_ssm_b_re, v_ssm_b_im=v_ssm_b_im, v_ssm_c_re=v_ssm_c_re, v_ssm_c_im=v_ssm_c_im, v_ssm_d=v_ssm_d, v_w_glu=v_w_glu, v_g_attn_out=v_g_attn_out, v_g_ssm_out=v_g_ssm_out, v_w_out=v_w_out, v_g_pre_ffn=v_g_pre_ffn, v_g_post_ffn=v_g_post_ffn, v_w_up=v_w_up, v_conv_w=v_conv_w, v_conv_b=v_conv_b, v_w_down=v_w_down)
    weights = {n: given[n] for n in TWIN_WEIGHTS}
    shared = {n: given[n] for n in SHARED_INPUTS}
    per_example = {n: given[n] for n in ['x', 'c']}
    grad_fn = _jax.value_and_grad(_loss, argnums=(0, 1))

    def one_microbatch(ex, loss_target):
        ex = dict(ex)
        diff = ex.pop(TWIN_DIFF_INPUT)
        return grad_fn(weights, diff, {**shared, **ex}, loss_target)

    if N_MICROBATCH == 1:
        loss, (grad_w, grad_x) = one_microbatch(per_example, given["loss_target"])
    else:
        def body(carry, xs):
            loss_sum, grad_sum = carry
            l_k, (gw_k, gx_k) = one_microbatch(xs[0], xs[1])
            with _jax.named_scope("update"):
                return (loss_sum + l_k, _jax.tree.map(_jnp.add, grad_sum, gw_k)), gx_k

        init = (_jnp.zeros((), _jnp.float32), _jax.tree.map(_jnp.zeros_like, weights))
        (loss, grad_w), grad_x = _jax.lax.scan(body, init, (per_example, given["loss_target"]))
    with _jax.named_scope("update"):
        delta_w, new_m, new_v = {}, {}, {}
        for n in TWIN_WEIGHTS:
            delta_w[n], new_m[n], new_v[n] = _adamw(weights[n], grad_w[n], given["m_" + n], given["v_" + n])
    return (loss, grad_x, *[grad_w[n] for n in TWIN_WEIGHTS], *[delta_w[n] for n in TWIN_WEIGHTS],
            *[new_m[n] for n in TWIN_WEIGHTS], *[new_v[n] for n in TWIN_WEIGHTS])
```

```python
import functools
import math

import jax
import jax.numpy as jnp
from jax import lax
from jax.experimental import pallas as pl
from jax.experimental.pallas import tpu as pltpu

F32 = jnp.float32
BF16 = jnp.bfloat16

N_DEV = 8
HEAD_DIM = 64
WINDOW = 128
SSM_GROUP = 16
STATE = 64
LANES = 128
GROUPS_PER_BLOCK = LANES // SSM_GROUP
BLOCK_STATES = GROUPS_PER_BLOCK * STATE
EPS = 1e-6
NEG = -1e30
ADAM_LR, ADAM_B1, ADAM_B2, ADAM_EPS, ADAM_WD, ADAM_STEP = 0.001, 0.9, 0.999, 1e-08, 0.01, 10
VMEM_BYTES_V7X = 64 * 1024 * 1024
GELU_C = math.sqrt(2.0 / math.pi)
MESH = pl.DeviceIdType.MESH
ANY = pl.BlockSpec(memory_space=pl.ANY)


def _pick(n, pref, align):
    t = (min(pref, n) // align) * align
    while t >= align:
        if n % t == 0:
            return t
        t -= align
    return n


def _params(vmem_bytes=None):
    if vmem_bytes is None:
        return pltpu.CompilerParams()
    return pltpu.CompilerParams(vmem_limit_bytes=int(min(vmem_bytes, VMEM_BYTES_V7X - (8 << 20))))


def _gelu(x):
    return 0.5 * x * (1.0 + jnp.tanh(GELU_C * (x + 0.044715 * x * x * x)))


def _gelu_grad(x):
    th = jnp.tanh(GELU_C * (x + 0.044715 * x * x * x))
    return 0.5 * (1.0 + th) + 0.5 * x * (1.0 - th * th) * GELU_C * (1.0 + 3.0 * 0.044715 * x * x)


def _rstd(x):
    return lax.rsqrt(jnp.mean(x * x, axis=-1, keepdims=True) + EPS)


def _norm_bwd(dhat, xhat, r):
    return r * (dhat - xhat * jnp.mean(dhat * xhat, axis=-1, keepdims=True))


def _matmul(a, b, *, ta=False, tb=False, out_dtype=F32, name):
    (kdim, m) = a.shape if ta else a.shape[::-1]
    (n, k2) = b.shape if tb else b.shape[::-1]
    assert kdim == k2, (a.shape, b.shape, ta, tb)
    tm, tn, tk = _pick(m, 1024, LANES), _pick(n, 1024, LANES), _pick(kdim, 1024, LANES)
    nk = kdim // tk
    dn = (((0 if ta else 1,), (1 if tb else 0,)), ((), ()))

    def body(a_ref, b_ref, o_ref, acc_ref):
        k = pl.program_id(2)

        @pl.when(k == 0)
        def _():
            acc_ref[...] = jnp.zeros_like(acc_ref)

        acc_ref[...] += lax.dot_general(a_ref[...].astype(BF16), b_ref[...].astype(BF16), dn,
                                        preferred_element_type=F32)

        @pl.when(k == nk - 1)
        def _():
            o_ref[...] = acc_ref[...].astype(o_ref.dtype)

    a_spec = pl.BlockSpec((tk, tm), lambda i, j, k: (k, i)) if ta else pl.BlockSpec((tm, tk), lambda i, j, k: (i, k))
    b_spec = pl.BlockSpec((tn, tk), lambda i, j, k: (j, k)) if tb else pl.BlockSpec((tk, tn), lambda i, j, k: (k, j))
    vmem = (2 * (tm * tk * a.dtype.itemsize + tk * tn * b.dtype.itemsize) + tm * tn * 4
            + 2 * tm * tn * jnp.dtype(out_dtype).itemsize + 3 * tm * tn * 4 + (4 << 20))
    return pl.pallas_call(
        body, name=name, grid=(m // tm, n // tn, nk),
        in_specs=[a_spec, b_spec], out_specs=pl.BlockSpec((tm, tn), lambda i, j, k: (i, j)),
        out_shape=jax.ShapeDtypeStruct((m, n), out_dtype),
        scratch_shapes=[pltpu.VMEM((tm, tn), F32)],
        compiler_params=_params(vmem),
    )(a, b)


def _all_gather(x, name):
    def body(x_ref, out_ref, send_sems, recv_sems, local_sem):
        x_, y_, c_ = lax.axis_index("x"), lax.axis_index("y"), lax.axis_index("c")
        me, sibling = (x_, y_, c_), (x_, y_, 1 - c_)
        chips = [(1 - x_, y_), (x_, 1 - y_), (1 - x_, 1 - y_)]

        def slot(px, py, pc):
            return out_ref.at[4 * px + 2 * py + pc]

        def copy(k, block, to, src=None):
            return pltpu.make_async_remote_copy(
                src_ref=slot(*block) if src is None else src, dst_ref=slot(*block),
                send_sem=send_sems.at[k], recv_sem=recv_sems.at[k], device_id=to, device_id_type=MESH)

        mine = pltpu.make_async_copy(x_ref, slot(*me), local_sem)
        mine.start()
        first = [copy(0, me, sibling, src=x_ref)]
        first += [copy(1 + j, me, (*chip, c_), src=x_ref) for j, chip in enumerate(chips)]
        for cp in first:
            cp.start()
        passed = [copy(4 + j, (*chip, c_), sibling) for j, chip in enumerate(chips)]
        for j, chip in enumerate(chips):
            copy(1 + j, (*chip, c_), me).wait_recv()
            passed[j].start()
        copy(0, sibling, me).wait_recv()
        for j, chip in enumerate(chips):
            copy(4 + j, (*chip, 1 - c_), me).wait_recv()
        for cp in first + passed:
            cp.wait_send()
        mine.wait()

    return pl.pallas_call(
        body, name=name, out_shape=jax.ShapeDtypeStruct((N_DEV,) + x.shape, x.dtype),
        in_specs=[ANY], out_specs=ANY,
        scratch_shapes=[pltpu.SemaphoreType.DMA((7,)), pltpu.SemaphoreType.DMA((7,)), pltpu.SemaphoreType.DMA],
    )(x)


def _all_to_all(x, name):
    def body(x_ref, out_ref, send_sems, recv_sems, local_sem):
        x_, y_, c_ = lax.axis_index("x"), lax.axis_index("y"), lax.axis_index("c")
        me = 4 * x_ + 2 * y_ + c_
        mine = pltpu.make_async_copy(x_ref.at[me], out_ref.at[me], local_sem)
        mine.start()
        copies = []
        for k in range(1, N_DEV):
            fx, fy, fc = (k >> 2) & 1, (k >> 1) & 1, k & 1
            px, py, pc = (1 - x_ if fx else x_), (1 - y_ if fy else y_), (1 - c_ if fc else c_)
            peer = 4 * px + 2 * py + pc
            send = pltpu.make_async_remote_copy(
                src_ref=x_ref.at[peer], dst_ref=out_ref.at[me], send_sem=send_sems.at[k - 1],
                recv_sem=recv_sems.at[k - 1], device_id=(px, py, pc), device_id_type=MESH)
            recv = pltpu.make_async_remote_copy(
                src_ref=x_ref.at[peer], dst_ref=out_ref.at[peer], send_sem=send_sems.at[k - 1],
                recv_sem=recv_sems.at[k - 1], device_id=(px, py, pc), device_id_type=MESH)
            send.start()
            copies.append((send, recv))
        for send, recv in copies:
            recv.wait_recv()
        for send, recv in copies:
            send.wait_send()
        mine.wait()

    return pl.pallas_call(
        body, name=name, out_shape=jax.ShapeDtypeStruct(x.shape, x.dtype),
        in_specs=[ANY], out_specs=ANY,
        scratch_shapes=[pltpu.SemaphoreType.DMA((7,)), pltpu.SemaphoreType.DMA((7,)), pltpu.SemaphoreType.DMA],
    )(x)


def _sum_slots(x, name):
    _, r, c = x.shape
    tr = _pick(r, 512, 16)

    def body(x_ref, o_ref):
        acc = x_ref[0].astype(F32)
        for i in range(1, N_DEV):
            acc = acc + x_ref[i].astype(F32)
        o_ref[...] = acc

    return pl.pallas_call(
        body, name=name, grid=(r // tr,),
        in_specs=[pl.BlockSpec((N_DEV, tr, c), lambda i: (0, i, 0))],
        out_specs=pl.BlockSpec((tr, c), lambda i: (i, 0)),
        out_shape=jax.ShapeDtypeStruct((r, c), F32),
        compiler_params=_params(2 * N_DEV * tr * c * x.dtype.itemsize + 4 * tr * c * 4 + (4 << 20)),
    )(x)


def _pack(arrs, dtype, cols):
    flat = jnp.concatenate([a.astype(dtype).reshape(-1) for a in arrs])
    unit = 16 * cols
    pad = (-flat.shape[0]) % unit
    flat = jnp.pad(flat, (0, pad))
    return flat.reshape(-1, cols)


def _unpack(flat, shapes):
    out, off = [], 0
    for s in shapes:
        n = math.prod(s)
        out.append(flat[off:off + n].reshape(s))
        off += n
    return out


def _ada_fwd(c_all, w_ada, b_shard, name):
    nl, d, n = w_ada.shape
    tn = _pick(n, 512, LANES)

    def body(c_ref, w_ref, b_ref, o_ref, act_ref):
        cv = c_ref[...]
        act = cv * jax.nn.sigmoid(cv)
        act_ref[...] = act
        o_ref[...] = jnp.dot(act.astype(BF16), w_ref[...].astype(BF16), preferred_element_type=F32) + b_ref[...]

    return pl.pallas_call(
        body, name=name, grid=(nl, n // tn),
        in_specs=[pl.BlockSpec(c_all.shape, lambda l, j: (0, 0)),
                  pl.BlockSpec((None, d, tn), lambda l, j: (l, 0, j)),
                  pl.BlockSpec((None, 1, tn), lambda l, j: (l, 0, j))],
        out_specs=[pl.BlockSpec((None, c_all.shape[0], tn), lambda l, j: (l, 0, j)),
                   pl.BlockSpec(c_all.shape, lambda l, j: (0, 0))],
        out_shape=[jax.ShapeDtypeStruct((nl, c_all.shape[0], n), F32), jax.ShapeDtypeStruct(c_all.shape, F32)],
        compiler_params=_params(2 * d * tn * 4 + d * tn * 2 + (8 << 20)),
    )(c_all, w_ada, b_shard)


def _ada_wgrad(act_t, dada, name):
    d, kp = act_t.shape
    nl, _, n = dada.shape
    tm = _pick(d, 512, 8)

    def body(a_ref, g_ref, o_ref):
        o_ref[...] = jnp.dot(a_ref[...].astype(BF16), g_ref[...].astype(BF16), preferred_element_type=F32)

    return pl.pallas_call(
        body, name=name, grid=(nl, d // tm),
        in_specs=[pl.BlockSpec((tm, kp), lambda l, i: (i, 0)), pl.BlockSpec((None, kp, n), lambda l, i: (l, 0, 0))],
        out_specs=pl.BlockSpec((None, tm, n), lambda l, i: (l, i, 0)),
        out_shape=jax.ShapeDtypeStruct((nl, d, n), F32),
        compiler_params=_params(4 * tm * n * 4 + 2 * kp * n * 4 + (8 << 20)),
    )(act_t, dada)


def _row_spec(tm, d):
    return pl.BlockSpec((tm, d), lambda i: (i, 0))


def _vec_spec(d):
    return pl.BlockSpec((1, d), lambda i: (0, 0))


def _modnorm_fwd(x, g, sc, sh, name):
    s, d = x.shape
    tm = _pick(s, 256, 16)

    def body(x_ref, g_ref, sc_ref, sh_ref, o_ref):
        xv = x_ref[...]
        o_ref[...] = ((xv * _rstd(xv)) * g_ref[...] * (1.0 + sc_ref[...]) + sh_ref[...]).astype(o_ref.dtype)

    return pl.pallas_call(
        body, name=name, grid=(s // tm,),
        in_specs=[_row_spec(tm, d), _vec_spec(d), _vec_spec(d), _vec_spec(d)], out_specs=_row_spec(tm, d),
        out_shape=jax.ShapeDtypeStruct((s, d), BF16),
    )(x, g, sc, sh)


def _modnorm_bwd(dh, x, g, sc, dres, name):
    s, d = x.shape
    tm = _pick(s, 256, 8)

    def body(dh_ref, x_ref, g_ref, sc_ref, dres_ref, dx_ref, dg_ref, dsc_ref, dsh_ref):
        @pl.when(pl.program_id(0) == 0)
        def _():
            dg_ref[...] = jnp.zeros_like(dg_ref)
            dsc_ref[...] = jnp.zeros_like(dsc_ref)
            dsh_ref[...] = jnp.zeros_like(dsh_ref)

        dh_, xv, gv = dh_ref[...], x_ref[...], g_ref[...]
        r = _rstd(xv)
        xhat = xv * r
        dn = dh_ * (1.0 + sc_ref[...])
        dsh_ref[...] += jnp.sum(dh_, axis=0, keepdims=True)
        dsc_ref[...] += jnp.sum(dh_ * (xhat * gv), axis=0, keepdims=True)
        dg_ref[...] += jnp.sum(dn * xhat, axis=0, keepdims=True)
        dx_ref[...] = _norm_bwd(dn * gv, xhat, r) + dres_ref[...]

    vec = jax.ShapeDtypeStruct((1, d), F32)
    return pl.pallas_call(
        body, name=name, grid=(s // tm,),
        in_specs=[_row_spec(tm, d), _row_spec(tm, d), _vec_spec(d), _vec_spec(d), _row_spec(tm, d)],
        out_specs=[_row_spec(tm, d), _vec_spec(d), _vec_spec(d), _vec_spec(d)],
        out_shape=[jax.ShapeDtypeStruct((s, d), F32), vec, vec, vec],
    )(dh, x, g, sc, dres)


def _resnorm_fwd(x, y, g, gt, name):
    s, d = x.shape
    tm = _pick(s, 256, 8)

    def body(x_ref, y_ref, g_ref, gt_ref, o_ref):
        yv = y_ref[...]
        o_ref[...] = x_ref[...] + (1.0 + gt_ref[...]) * ((yv * _rstd(yv)) * g_ref[...])

    return pl.pallas_call(
        body, name=name, grid=(s // tm,),
        in_specs=[_row_spec(tm, d), _row_spec(tm, d), _vec_spec(d), _vec_spec(d)], out_specs=_row_spec(tm, d),
        out_shape=jax.ShapeDtypeStruct((s, d), F32),
    )(x, y, g, gt)


def _resnorm_bwd(dxo, y, g, gt, name):
    s, d = y.shape
    tm = _pick(s, 256, 16)

    def body(dxo_ref, y_ref, g_ref, gt_ref, dy_ref, dg_ref, dgt_ref):
        @pl.when(pl.program_id(0) == 0)
        def _():
            dg_ref[...] = jnp.zeros_like(dg_ref)
            dgt_ref[...] = jnp.zeros_like(dgt_ref)

        dxo_, yv, gv = dxo_ref[...], y_ref[...], g_ref[...]
        r = _rstd(yv)
        yhat = yv * r
        dn = dxo_ * (1.0 + gt_ref[...])
        dgt_ref[...] += jnp.sum(dxo_ * (yhat * gv), axis=0, keepdims=True)
        dg_ref[...] += jnp.sum(dn * yhat, axis=0, keepdims=True)
        dy_ref[...] = _norm_bwd(dn * gv, yhat, r).astype(dy_ref.dtype)

    vec = jax.ShapeDtypeStruct((1, d), F32)
    return pl.pallas_call(
        body, name=name, grid=(s // tm,),
        in_specs=[_row_spec(tm, d), _row_spec(tm, d), _vec_spec(d), _vec_spec(d)],
        out_specs=[_row_spec(tm, d), _vec_spec(d), _vec_spec(d)],
        out_shape=[jax.ShapeDtypeStruct((s, d), BF16), vec, vec],
    )(dxo, y, g, gt)


def _loss_bwd(xf, tgt, name):
    s, d = xf.shape
    tm = _pick(s, 256, 8)

    def body(x_ref, t_ref, dy_ref, l_ref):
        @pl.when(pl.program_id(0) == 0)
        def _():
            l_ref[...] = jnp.zeros_like(l_ref)

        e = x_ref[...] - t_ref[...]
        dy_ref[...] = e * (1.0 / d)
        l_ref[...] += jnp.sum(e * e) * (0.5 / d)

    return pl.pallas_call(
        body, name=name, grid=(s // tm,),
        in_specs=[_row_spec(tm, d), _row_spec(tm, d)],
        out_specs=[_row_spec(tm, d), pl.BlockSpec((8, LANES), lambda i: (0, 0))],
        out_shape=[jax.ShapeDtypeStruct((s, d), F32), jax.ShapeDtypeStruct((8, LANES), F32)],
    )(xf, tgt)


def _attn_specs(n_q, n_kv):
    aw, kvd = n_q * HEAD_DIM, n_kv * HEAD_DIM
    assert aw % kvd == 0
    kcol = aw // kvd
    q = pl.BlockSpec((WINDOW, aw), lambda n: (n, 0))
    kc = pl.BlockSpec((WINDOW, kvd), lambda n: (n, kcol))
    kp = pl.BlockSpec((WINDOW, kvd), lambda n: (jnp.maximum(n - 1, 0), kcol))
    vc = pl.BlockSpec((WINDOW, kvd), lambda n: (n, kcol + 1))
    vp = pl.BlockSpec((WINDOW, kvd), lambda n: (jnp.maximum(n - 1, 0), kcol + 1))
    return [q, kc, kp, vc, vp]


def _band_mask(n):
    qi = lax.broadcasted_iota(jnp.int32, (WINDOW, 2 * WINDOW), 0)
    kj = lax.broadcasted_iota(jnp.int32, (WINDOW, 2 * WINDOW), 1)
    return (kj > qi) & (kj <= qi + WINDOW) & ((kj >= WINDOW) | (n > 0))


_NT = (((1,), (1,)), ((), ()))
_TN = (((0,), (0,)), ((), ()))


def _attn_fwd(proj, sinks, *, n_q, n_kv, name):
    s = proj.shape[0]
    aw, grp = n_q * HEAD_DIM, n_q // n_kv

    def body(q_ref, kc_ref, kp_ref, vc_ref, vp_ref, sink_ref, o_ref, lse_ref):
        valid = _band_mask(pl.program_id(0))
        kb = jnp.concatenate([kp_ref[...], kc_ref[...]], axis=0).astype(BF16)
        vb = jnp.concatenate([vp_ref[...], vc_ref[...]], axis=0).astype(BF16)
        lse_ref[...] = jnp.zeros_like(lse_ref)
        for h in range(n_q):
            g = h // grp
            hs, gs = slice(h * HEAD_DIM, (h + 1) * HEAD_DIM), slice(g * HEAD_DIM, (g + 1) * HEAD_DIM)
            sc = lax.dot_general(q_ref[:, hs].astype(BF16), kb[:, gs], _NT, preferred_element_type=F32)
            sc = jnp.where(valid, sc * (HEAD_DIM ** -0.5), NEG)
            sink = sink_ref[:, h:h + 1]
            m = jnp.maximum(jnp.max(sc, axis=-1, keepdims=True), sink)
            e = jnp.exp(sc - m)
            den = jnp.sum(e, axis=-1, keepdims=True) + jnp.exp(sink - m)
            p = e / den
            o_ref[:, hs] = jnp.dot(p.astype(BF16), vb[:, gs], preferred_element_type=F32)
            lse_ref[:, h:h + 1] = m + jnp.log(den)

    return pl.pallas_call(
        body, name=name, grid=(s // WINDOW,),
        in_specs=_attn_specs(n_q, n_kv) + [pl.BlockSpec((1, LANES), lambda n: (0, 0))],
        out_specs=[pl.BlockSpec((WINDOW, aw), lambda n: (n, 0)), pl.BlockSpec((WINDOW, LANES), lambda n: (n, 0))],
        out_shape=[jax.ShapeDtypeStruct((s, aw), F32), jax.ShapeDtypeStruct((s, LANES), F32)],
    )(proj, proj, proj, proj, proj, sinks)


def _attn_bwd(proj, sinks, out, lse, dout, *, n_q, n_kv, name):
    s = proj.shape[0]
    aw, kvd, grp = n_q * HEAD_DIM, n_kv * HEAD_DIM, n_q // n_kv
    scale = HEAD_DIM ** -0.5

    def body(q_ref, kc_ref, kp_ref, vc_ref, vp_ref, sink_ref, o_ref, lse_ref, do_ref,
             dq_ref, dk_ref, dv_ref, dsink_ref):
        n = pl.program_id(0)

        @pl.when(n == 0)
        def _():
            dk_ref[...] = jnp.zeros_like(dk_ref)
            dv_ref[...] = jnp.zeros_like(dv_ref)
            dsink_ref[...] = jnp.zeros_like(dsink_ref)

        valid = _band_mask(n)
        kb = jnp.concatenate([kp_ref[...], kc_ref[...]], axis=0).astype(BF16)
        vb = jnp.concatenate([vp_ref[...], vc_ref[...]], axis=0).astype(BF16)
        lane = lax.broadcasted_iota(jnp.int32, (8, LANES), 1)
        dsink = jnp.zeros((8, LANES), F32)
        cur = pl.ds(pl.multiple_of(n * WINDOW, WINDOW), WINDOW)
        prev = pl.ds(pl.multiple_of(jnp.maximum(n - 1, 0) * WINDOW, WINDOW), WINDOW)
        for g in range(n_kv):
            gs = slice(g * HEAD_DIM, (g + 1) * HEAD_DIM)
            dkb = jnp.zeros((2 * WINDOW, HEAD_DIM), F32)
            dvb = jnp.zeros((2 * WINDOW, HEAD_DIM), F32)
            for h in range(g * grp, (g + 1) * grp):
                hs = slice(h * HEAD_DIM, (h + 1) * HEAD_DIM)
                qh = q_ref[:, hs].astype(BF16)
                sc = lax.dot_general(qh, kb[:, gs], _NT, preferred_element_type=F32)
                sc = jnp.where(valid, sc * scale, NEG)
                lse_h = lse_ref[:, h:h + 1]
                p = jnp.exp(sc - lse_h)
                do = do_ref[:, hs]
                dob = do.astype(BF16)
                delta = jnp.sum(do * o_ref[:, hs], axis=-1, keepdims=True)
                dp = lax.dot_general(dob, vb[:, gs], _NT, preferred_element_type=F32)
                ds = (p * (dp - delta) * scale).astype(BF16)
                dq_ref[:, hs] = jnp.dot(ds, kb[:, gs], preferred_element_type=F32)
                dkb = dkb + lax.dot_general(ds, qh, _TN, preferred_element_type=F32)
                dvb = dvb + lax.dot_general(p.astype(BF16), dob, _TN, preferred_element_type=F32)
                psink = jnp.exp(sink_ref[:, h:h + 1] - lse_h)
                dsink = dsink + jnp.where(lane == h, -jnp.sum(psink * delta), 0.0)
            dk_ref[cur, gs] += dkb[WINDOW:]
            dv_ref[cur, gs] += dvb[WINDOW:]

            @pl.when(n > 0)
            def _():
                dk_ref[prev, gs] += dkb[:WINDOW]
                dv_ref[prev, gs] += dvb[:WINDOW]

        dsink_ref[...] += dsink

    blk = pl.BlockSpec((WINDOW, aw), lambda n: (n, 0))
    kv_full = pl.BlockSpec((s, kvd), lambda n: (0, 0))
    return pl.pallas_call(
        body, name=name, grid=(s // WINDOW,),
        in_specs=_attn_specs(n_q, n_kv) + [pl.BlockSpec((1, LANES), lambda n: (0, 0)), blk,
                                           pl.BlockSpec((WINDOW, LANES), lambda n: (n, 0)), blk],
        out_specs=[blk, kv_full, kv_full, pl.BlockSpec((8, LANES), lambda n: (0, 0))],
        out_shape=[jax.ShapeDtypeStruct((s, aw), F32), jax.ShapeDtypeStruct((s, kvd), F32),
                   jax.ShapeDtypeStruct((s, kvd), F32), jax.ShapeDtypeStruct((8, LANES), F32)],
    )(proj, proj, proj, proj, proj, sinks, out, lse, dout)


def _disc(lr, li, ls):
    dt = jnp.exp(ls)
    mag = jnp.exp(lr * dt)
    ang = li * dt
    ab_re, ab_im = mag * jnp.cos(ang), mag * jnp.sin(ang)
    den = lr * lr + li * li
    f_re = ((ab_re - 1.0) * lr + ab_im * li) / den
    f_im = (ab_im * lr - (ab_re - 1.0) * li) / den
    return ab_re, ab_im, f_re, f_im


def _ssm_params_fwd(lr, li, ls, b_re, b_im, t_len, name):
    gp = lr.shape[1]
    h = b_re.shape[0]

    def body(lr_ref, li_ref, ls_ref, br_ref, bi_ref, bbr_ref, bbi_ref, tr_ref, ti_ref):
        ab_re, ab_im, f_re, f_im = _disc(lr_ref[...], li_ref[...], ls_ref[...])
        br, bi = br_ref[...], bi_ref[...]
        bbr_ref[...] = f_re * br - f_im * bi
        bbi_ref[...] = f_re * bi + f_im * br
        tr_ref[0:1, :] = ab_re
        ti_ref[0:1, :] = ab_im
        k = 1
        while k < t_len:
            pr, pi = tr_ref[k - 1:k, :], ti_ref[k - 1:k, :]
            qr, qi = tr_ref[0:k, :], ti_ref[0:k, :]
            tr_ref[k:2 * k, :] = qr * pr - qi * pi
            ti_ref[k:2 * k, :] = qr * pi + qi * pr
            k *= 2

    vec, mat, tab = (jax.ShapeDtypeStruct(sh, F32) for sh in ((1, gp), (h, gp), (t_len, gp)))
    return pl.pallas_call(body, name=name, out_shape=[mat, mat, tab, tab],
                          compiler_params=_params(8 * t_len * gp * 4 + (8 << 20)))(lr, li, ls, b_re, b_im)


def _ssm_params_bwd(lr, li, ls, b_re, b_im, dab_re, dab_im, dbb_re, dbb_im, seg, name):
    gp = lr.shape[1]
    h = b_re.shape[0]

    def body(lr_ref, li_ref, ls_ref, br_ref, bi_ref, dar_ref, dai_ref, dbbr_ref, dbbi_ref, seg_ref,
             dlr_ref, dli_ref, dls_ref, dbr_ref, dbi_ref):
        lr_, li_, ls_ = lr_ref[...], li_ref[...], ls_ref[...]
        (ab_re, ab_im, f_re, f_im), vjp = jax.vjp(_disc, lr_, li_, ls_)
        br, bi, dbbr, dbbi = br_ref[...], bi_ref[...], dbbr_ref[...], dbbi_ref[...]
        dbr_ref[...] = dbbr * f_re + dbbi * f_im
        dbi_ref[...] = dbbi * f_re - dbbr * f_im
        df_re = jnp.sum(dbbr * br + dbbi * bi, axis=0, keepdims=True)
        df_im = jnp.sum(dbbi * br - dbbr * bi, axis=0, keepdims=True)
        dlr, dli, dls = vjp((dar_ref[...], dai_ref[...], df_re, df_im))
        dlr_ref[...] = dlr
        dli_ref[...] = dli
        dls8 = jnp.broadcast_to(dls, (8, gp))
        dls_ref[...] = jnp.dot(dls8, seg_ref[...], preferred_element_type=F32, precision=lax.Precision.HIGHEST)

    vec, mat = jax.ShapeDtypeStruct((1, gp), F32), jax.ShapeDtypeStruct((h, gp), F32)
    return pl.pallas_call(body, name=name,
                          out_shape=[vec, vec, jax.ShapeDtypeStruct((8, seg.shape[1]), F32), mat, mat],
                          compiler_params=_params(24 << 20))(
        lr, li, ls, b_re, b_im, dab_re, dab_im, dbb_re, dbb_im, seg)


def _scan(xr, xi, apow_ref, t_len, reverse):
    hs = BLOCK_STATES
    row = lax.broadcasted_iota(jnp.int32, xr.shape, 0)
    d = 1
    while d < t_len:
        pr, pi = apow_ref[d - 1:d, :hs], apow_ref[d - 1:d, hs:]
        if reverse:
            pi, shift, keep = -pi, t_len - d, row < t_len - d
        else:
            shift, keep = d, row >= d
        sr = jnp.where(keep, pltpu.roll(xr, shift, 0), 0.0)
        si = jnp.where(keep, pltpu.roll(xi, shift, 0), 0.0)
        xr, xi = xr + pr * sr - pi * si, xi + pr * si + pi * sr
        d *= 2
    return xr, xi


def _ssm_chunk(s):
    return _pick(s, 256, 8)


def _ssm_fwd(proj, ucol, bbd, ccat, dskip, apow, *, name):
    s = proj.shape[0]
    nb, t_len = bbd.shape[0], apow.shape[1]
    nc = s // t_len
    hs = BLOCK_STATES

    def body(u_ref, bbd_ref, ccat_ref, d_ref, apow_ref, y_ref, z_ref, xs_ref, carry_ref):
        @pl.when(pl.program_id(1) == 0)
        def _():
            carry_ref[...] = jnp.zeros_like(carry_ref)

        xs_ref[...] = carry_ref[...]
        u = u_ref[...]
        bu = jnp.dot(u.astype(BF16), bbd_ref[...], preferred_element_type=F32)
        xr, xi = _scan(bu[:, :hs], bu[:, hs:], apow_ref, t_len, False)
        cr, ci = carry_ref[0:1, :hs], carry_ref[0:1, hs:]
        ar, ai = apow_ref[:, :hs], apow_ref[:, hs:]
        xr, xi = xr + ar * cr - ai * ci, xi + ar * ci + ai * cr
        xcat = jnp.concatenate([xr, xi], axis=1)
        carry_ref[...] = jnp.broadcast_to(xcat[t_len - 1:t_len, :], carry_ref.shape)
        y = jnp.dot(xcat.astype(BF16), ccat_ref[...], preferred_element_type=F32) + d_ref[...] * u
        y_ref[...] = y
        z_ref[...] = _gelu(y).astype(z_ref.dtype)

    return pl.pallas_call(
        body, name=name, grid=(nb, nc),
        in_specs=[pl.BlockSpec((t_len, LANES), lambda j, n: (n, ucol + j)),
                  pl.BlockSpec((None, LANES, 2 * hs), lambda j, n: (j, 0, 0)),
                  pl.BlockSpec((None, 2 * hs, LANES), lambda j, n: (j, 0, 0)),
                  pl.BlockSpec((1, LANES), lambda j, n: (0, j)),
                  pl.BlockSpec((None, t_len, 2 * hs), lambda j, n: (j, 0, 0))],
        out_specs=[pl.BlockSpec((t_len, LANES), lambda j, n: (n, j)),
                   pl.BlockSpec((t_len, LANES), lambda j, n: (n, j)),
                   pl.BlockSpec((None, None, 8, 2 * hs), lambda j, n: (j, n, 0, 0))],
        out_shape=[jax.ShapeDtypeStruct((s, nb * LANES), F32), jax.ShapeDtypeStruct((s, nb * LANES), BF16),
                   jax.ShapeDtypeStruct((nb, nc, 8, 2 * hs), F32)],
        scratch_shapes=[pltpu.VMEM((8, 2 * hs), F32)],
        compiler_params=_params(40 << 20),
    )(proj, bbd, ccat, dskip, apow)


def _ssm_bwd(proj, ucol, y, dzd, dz2, xs, bbd, ccat, dskip, apow, apow_rev, *, name):
    s = proj.shape[0]
    nb, t_len = bbd.shape[0], apow.shape[1]
    nc = s // t_len
    hs = BLOCK_STATES

    def body(u_ref, y_ref, dzd_ref, dz2_ref, xs_ref, bbd_ref, ccat_ref, d_ref, apow_ref, apowr_ref,
             du_ref, dbbd_ref, dccat_ref, dd_ref, da_ref, gcarry_ref):
        @pl.when(pl.program_id(1) == 0)
        def _():
            gcarry_ref[...] = jnp.zeros_like(gcarry_ref)
            dbbd_ref[...] = jnp.zeros_like(dbbd_ref)
            dccat_ref[...] = jnp.zeros_like(dccat_ref)
            dd_ref[...] = jnp.zeros_like(dd_ref)
            da_ref[...] = jnp.zeros_like(da_ref)

        u = u_ref[...]
        ub = u.astype(BF16)
        dy = (dzd_ref[...] + dz2_ref[...]) * _gelu_grad(y_ref[...])
        dyb = dy.astype(BF16)
        bu = jnp.dot(ub, bbd_ref[...], preferred_element_type=F32)
        xr, xi = _scan(bu[:, :hs], bu[:, hs:], apow_ref, t_len, False)
        sr, si = xs_ref[0:1, :hs], xs_ref[0:1, hs:]
        ar, ai = apow_ref[:, :hs], apow_ref[:, hs:]
        xr, xi = xr + ar * sr - ai * si, xi + ar * si + ai * sr
        dxd = lax.dot_general(dyb, ccat_ref[...], _NT, preferred_element_type=F32)
        gr, gi = _scan(dxd[:, :hs], dxd[:, hs:], apow_ref, t_len, True)
        cr, ci = gcarry_ref[0:1, :hs], gcarry_ref[0:1, hs:]
        rr, ri = apowr_ref[:, :hs], apowr_ref[:, hs:]
        gr, gi = gr + rr * cr + ri * ci, gi + rr * ci - ri * cr
        gcat = jnp.concatenate([gr, gi], axis=1)
        gcarry_ref[...] = jnp.broadcast_to(gcat[0:1, :], gcarry_ref.shape)
        gb = gcat.astype(BF16)
        du_ref[...] = lax.dot_general(gb, bbd_ref[...], _NT, preferred_element_type=F32) + d_ref[...] * dy
        dbbd_ref[...] += lax.dot_general(ub, gb, _TN, preferred_element_type=F32)
        xb = jnp.concatenate([xr, xi], axis=1).astype(BF16)
        dccat_ref[...] += lax.dot_general(xb, dyb, _TN, preferred_element_type=F32)
        dd_ref[...] += jnp.sum(dy * u, axis=0, keepdims=True)
        first = lax.broadcasted_iota(jnp.int32, xr.shape, 0) == 0
        xpr = jnp.where(first, sr, pltpu.roll(xr, 1, 0))
        xpi = jnp.where(first, si, pltpu.roll(xi, 1, 0))
        dar = jnp.sum(gr * xpr + gi * xpi, axis=0, keepdims=True)
        dai = jnp.sum(gi * xpr - gr * xpi, axis=0, keepdims=True)
        da_ref[...] += jnp.concatenate([dar, dai], axis=1)

    def rows(j, n):
        return nc - 1 - n

    chunk = pl.BlockSpec((t_len, LANES), lambda j, n: (rows(j, n), j))
    return pl.pallas_call(
        body, name=name, grid=(nb, nc),
        in_specs=[pl.BlockSpec((t_len, LANES), lambda j, n: (rows(j, n), ucol + j)), chunk, chunk, chunk,
                  pl.BlockSpec((None, None, 8, 2 * hs), lambda j, n: (j, rows(j, n), 0, 0)),
                  pl.BlockSpec((None, LANES, 2 * hs), lambda j, n: (j, 0, 0)),
                  pl.BlockSpec((None, 2 * hs, LANES), lambda j, n: (j, 0, 0)),
                  pl.BlockSpec((1, LANES), lambda j, n: (0, j)),
                  pl.BlockSpec((None, t_len, 2 * hs), lambda j, n: (j, 0, 0)),
                  pl.BlockSpec((None, t_len, 2 * hs), lambda j, n: (j, 0, 0))],
        out_specs=[chunk,
                   pl.BlockSpec((None, LANES, 2 * hs), lambda j, n: (j, 0, 0)),
                   pl.BlockSpec((None, 2 * hs, LANES), lambda j, n: (j, 0, 0)),
                   pl.BlockSpec((1, LANES), lambda j, n: (0, j)),
                   pl.BlockSpec((None, 1, 2 * hs), lambda j, n: (j, 0, 0))],
        out_shape=[jax.ShapeDtypeStruct((s, nb * LANES), F32),
                   jax.ShapeDtypeStruct((nb, LANES, 2 * hs), F32),
                   jax.ShapeDtypeStruct((nb, 2 * hs, LANES), F32),
                   jax.ShapeDtypeStruct((1, nb * LANES), F32),
                   jax.ShapeDtypeStruct((nb, 1, 2 * hs), F32)],
        scratch_shapes=[pltpu.VMEM((8, 2 * hs), F32)],
        compiler_params=_params(48 << 20),
    )(proj, y, dzd, dz2, xs, bbd, ccat, dskip, apow, apow_rev)


def _to_blocks(a):
    g, p, k = a.shape
    nb = g // GROUPS_PER_BLOCK
    eye = jnp.eye(GROUPS_PER_BLOCK, dtype=a.dtype)
    a4 = a.reshape(nb, GROUPS_PER_BLOCK, p, k)
    out = jnp.einsum("ab,jbpk->jakbp", eye, a4)
    return out.reshape(nb, GROUPS_PER_BLOCK * k, GROUPS_PER_BLOCK * p)


def _from_blocks(d, p, k):
    nb = d.shape[0]
    d5 = d.reshape(nb, GROUPS_PER_BLOCK, k, GROUPS_PER_BLOCK, p)
    idx = jnp.arange(GROUPS_PER_BLOCK)
    diag = d5[:, idx, :, idx, :]
    return jnp.transpose(diag, (1, 0, 3, 2)).reshape(nb * GROUPS_PER_BLOCK, p, k)


def _merge_fwd(attn, y, gl, g_a, g_s, name):
    s, wa = attn.shape
    ws = y.shape[1]
    tm = _pick(s, 256, 16)

    def body(a_ref, y_ref, gl_ref, ga_ref, gs_ref, o_ref):
        av = a_ref[...]
        o_ref[:, :wa] = ((av * _rstd(av)) * ga_ref[...]).astype(o_ref.dtype)
        sv = _gelu(y_ref[...]) * jax.nn.sigmoid(gl_ref[...])
        o_ref[:, wa:] = ((sv * _rstd(sv)) * gs_ref[...]).astype(o_ref.dtype)

    return pl.pallas_call(
        body, name=name, grid=(s // tm,),
        in_specs=[_row_spec(tm, wa), _row_spec(tm, ws), _row_spec(tm, ws), _vec_spec(wa), _vec_spec(ws)],
        out_specs=_row_spec(tm, wa + ws), out_shape=jax.ShapeDtypeStruct((s, wa + ws), BF16),
    )(attn, y, gl, g_a, g_s)


def _merge_bwd(dmerged, attn, y, gl, g_a, g_s, name):
    s, wa = attn.shape
    ws = y.shape[1]
    tm = _pick(s, 256, 16)

    def body(dm_ref, a_ref, y_ref, gl_ref, ga_ref, gs_ref, da_ref, dgl_ref, dzd_ref, dga_ref, dgs_ref):
        @pl.when(pl.program_id(0) == 0)
        def _():
            dga_ref[...] = jnp.zeros_like(dga_ref)
            dgs_ref[...] = jnp.zeros_like(dgs_ref)

        dan, dsn = dm_ref[:, :wa], dm_ref[:, wa:]
        av = a_ref[...]
        ra = _rstd(av)
        ahat = av * ra
        dga_ref[...] += jnp.sum(dan * ahat, axis=0, keepdims=True)
        da_ref[...] = _norm_bwd(dan * ga_ref[...], ahat, ra)
        z = _gelu(y_ref[...])
        sig = jax.nn.sigmoid(gl_ref[...])
        sv = z * sig
        rs = _rstd(sv)
        shat = sv * rs
        dgs_ref[...] += jnp.sum(dsn * shat, axis=0, keepdims=True)
        dssm = _norm_bwd(dsn * gs_ref[...], shat, rs)
        dzd_ref[...] = dssm * sig
        dgl_ref[...] = (dssm * z * sig * (1.0 - sig)).astype(dgl_ref.dtype)

    return pl.pallas_call(
        body, name=name, grid=(s // tm,),
        in_specs=[_row_spec(tm, wa + ws), _row_spec(tm, wa), _row_spec(tm, ws), _row_spec(tm, ws),
                  _vec_spec(wa), _vec_spec(ws)],
        out_specs=[_row_spec(tm, wa), _row_spec(tm, ws), _row_spec(tm, ws), _vec_spec(wa), _vec_spec(ws)],
        out_shape=[jax.ShapeDtypeStruct((s, wa), F32), jax.ShapeDtypeStruct((s, ws), BF16),
                   jax.ShapeDtypeStruct((s, ws), F32), jax.ShapeDtypeStruct((1, wa), F32),
                   jax.ShapeDtypeStruct((1, ws), F32)],
    )(dmerged, attn, y, gl, g_a, g_s)


def _shift_down(main, halo, k):
    rolled = pltpu.roll(main, k, 0)
    row = lax.broadcasted_iota(jnp.int32, main.shape, 0)
    for r in range(k):
        rolled = jnp.where(row == r, halo[8 - k + r:8 - k + r + 1, :], rolled)
    return rolled


def _shift_up(main, halo, k):
    tm = main.shape[0]
    rolled = pltpu.roll(main, tm - k, 0)
    row = lax.broadcasted_iota(jnp.int32, main.shape, 0)
    for r in range(k):
        rolled = jnp.where(row == tm - k + r, halo[r:r + 1, :], rolled)
    return rolled


def _conv(main, halo, w_ref, b_ref):
    return (b_ref[...] + w_ref[0:1, :] * _shift_down(main, halo, 2) + w_ref[1:2, :] * _shift_down(main, halo, 1)
            + w_ref[2:3, :] * main)


def _gate_tiles(s, f):
    return _pick(s, 256, 16), _pick(f, 512, LANES)


def _gate_in_specs(tm, tn, nfb, order):
    hb = tm // 8
    ij = (lambda a, b: (b, a)) if order == "ji" else (lambda a, b: (a, b))

    def main(off):
        return pl.BlockSpec((tm, tn), lambda a, b: (ij(a, b)[0], ij(a, b)[1] + off))

    def halo(off):
        return pl.BlockSpec((8, tn), lambda a, b: (jnp.maximum(ij(a, b)[0] * hb - 1, 0), ij(a, b)[1] + off))

    def vec(rows, off):
        return pl.BlockSpec((rows, tn), lambda a, b: (0, ij(a, b)[1] + off))

    return [main(0), main(nfb), halo(0), halo(nfb), vec(3, 0), vec(3, nfb), vec(1, 0), vec(1, nfb)]


def _gate_fwd(up0, conv_w, conv_b, name):
    s, f2 = up0.shape
    f = f2 // 2
    tm, tn = _gate_tiles(s, f)
    nfb = f // tn

    def body(v_ref, g_ref, vh_ref, gh_ref, wv_ref, wg_ref, bv_ref, bg_ref, o_ref):
        top = pl.program_id(0) == 0
        vh = jnp.where(top, 0.0, vh_ref[...])
        gh = jnp.where(top, 0.0, gh_ref[...])
        val = _conv(v_ref[...], vh, wv_ref, bv_ref)
        gate = _conv(g_ref[...], gh, wg_ref, bg_ref)
        o_ref[...] = (_gelu(gate) * val).astype(o_ref.dtype)

    return pl.pallas_call(
        body, name=name, grid=(s // tm, nfb),
        in_specs=_gate_in_specs(tm, tn, nfb, "ij"), out_specs=pl.BlockSpec((tm, tn), lambda i, j: (i, j)),
        out_shape=jax.ShapeDtypeStruct((s, f), BF16),
    )(up0, up0, up0, up0, conv_w, conv_w, conv_b, conv_b)


def _gate_bwd(up0, conv_w, conv_b, da, name):
    s, f2 = up0.shape
    f = f2 // 2
    tm, tn = _gate_tiles(s, f)
    nfb = f // tn

    def body(v_ref, g_ref, vh_ref, gh_ref, wv_ref, wg_ref, bv_ref, bg_ref, da_ref, dup_ref, dcb_ref, dcw_ref):
        top = pl.program_id(1) == 0

        @pl.when(top)
        def _():
            dcb_ref[...] = jnp.zeros_like(dcb_ref)
            dcw_ref[...] = jnp.zeros_like(dcw_ref)

        halos = (jnp.where(top, 0.0, vh_ref[...]), jnp.where(top, 0.0, gh_ref[...]))
        mains = (v_ref[...], g_ref[...])
        val = _conv(mains[0], halos[0], wv_ref, bv_ref)
        gate = _conv(mains[1], halos[1], wg_ref, bg_ref)
        dav = da_ref[...]
        dups = (dav * _gelu(gate), dav * val * _gelu_grad(gate))
        for half in range(2):
            dup = dups[half]
            dup_ref[half] = dup
            dcb_ref[half] += jnp.sum(dup, axis=0, keepdims=True)
            dcw_ref[half, 0:1, :] += jnp.sum(dup * _shift_down(mains[half], halos[half], 2), axis=0, keepdims=True)
            dcw_ref[half, 1:2, :] += jnp.sum(dup * _shift_down(mains[half], halos[half], 1), axis=0, keepdims=True)
            dcw_ref[half, 2:3, :] += jnp.sum(dup * mains[half], axis=0, keepdims=True)

    return pl.pallas_call(
        body, name=name, grid=(nfb, s // tm),
        in_specs=_gate_in_specs(tm, tn, nfb, "ji") + [pl.BlockSpec((tm, tn), lambda j, i: (i, j))],
        out_specs=[pl.BlockSpec((2, tm, tn), lambda j, i: (0, i, j)),
                   pl.BlockSpec((2, 1, tn), lambda j, i: (0, 0, j)),
                   pl.BlockSpec((2, 3, tn), lambda j, i: (0, 0, j))],
        out_shape=[jax.ShapeDtypeStruct((2, s, f), F32), jax.ShapeDtypeStruct((2, 1, f), F32),
                   jax.ShapeDtypeStruct((2, 3, f), F32)],
    )(up0, up0, up0, up0, conv_w, conv_w, conv_b, conv_b, da)


def _conv_bwd(dup, conv_w, name):
    _, s, f = dup.shape
    tm, tn = _gate_tiles(s, f)
    nfb, ni, hb = f // tn, s // tm, tm // 8

    def body(d_ref, dh_ref, w_ref, o_ref):
        main = d_ref[...]
        halo = jnp.where(pl.program_id(1) == ni - 1, 0.0, dh_ref[...])
        o_ref[...] = (w_ref[2:3, :] * main + w_ref[1:2, :] * _shift_up(main, halo, 1)
                      + w_ref[0:1, :] * _shift_up(main, halo, 2)).astype(o_ref.dtype)

    return pl.pallas_call(
        body, name=name, grid=(2, ni, nfb),
        in_specs=[pl.BlockSpec((None, tm, tn), lambda h, i, j: (h, i, j)),
                  pl.BlockSpec((None, 8, tn), lambda h, i, j: (h, jnp.minimum((i + 1) * hb, s // 8 - 1), j)),
                  pl.BlockSpec((3, tn), lambda h, i, j: (0, h * nfb + j))],
        out_specs=pl.BlockSpec((tm, tn), lambda h, i, j: (i, h * nfb + j)),
        out_shape=jax.ShapeDtypeStruct((s, 2 * f), BF16),
    )(dup, dup, conv_w)


def _adamw(w, g, m, v, name):
    r, c = w.shape
    tr = _pick(r, max(8, (1 << 19) // max(c, 1) // 8 * 8), 8)
    c1, c2 = 1.0 / (1.0 - ADAM_B1 ** ADAM_STEP), 1.0 / (1.0 - ADAM_B2 ** ADAM_STEP)

    def body(w_ref, g_ref, m_ref, v_ref, d_ref, nm_ref, nv_ref):
        gv = g_ref[...]
        nm = ADAM_B1 * m_ref[...] + (1.0 - ADAM_B1) * gv
        nv = ADAM_B2 * v_ref[...] + (1.0 - ADAM_B2) * (gv * gv)
        nm_ref[...] = nm
        nv_ref[...] = nv
        d_ref[...] = -ADAM_LR * ((nm * c1) / (jnp.sqrt(nv * c2) + ADAM_EPS) + ADAM_WD * w_ref[...])

    spec = pl.BlockSpec((tr, c), lambda i: (i, 0))
    out = jax.ShapeDtypeStruct((r, c), F32)
    return pl.pallas_call(body, name=name, grid=(r // tr,), in_specs=[spec] * 4, out_specs=[spec] * 3,
                          out_shape=[out] * 3, compiler_params=_params(14 * tr * c * 4 + (4 << 20)))(w, g, m, v)


def _adamw_nd(w, g, m, v, name):
    shape = w.shape
    c = shape[-1]
    outs = _adamw(w.reshape(-1, c), g.reshape(-1, c), m.reshape(-1, c), v.reshape(-1, c), name)
    return [o.reshape(shape) for o in outs]


BIG = ("w_in", "w_glu", "w_out", "w_up", "w_down")
SMALL = ("b_ada", "g_pre_mix", "g_post_mix", "attn_sinks", "lam_re", "lam_im", "log_step", "ssm_b_re", "ssm_b_im",
         "ssm_c_re", "ssm_c_im", "ssm_d", "g_attn_out", "g_ssm_out", "g_pre_ffn", "g_post_ffn", "conv_b")
ORDER = ("w_ada", "b_ada", "g_pre_mix", "g_post_mix", "w_in", "attn_sinks", "lam_re", "lam_im", "log_step",
         "ssm_b_re", "ssm_b_im", "ssm_c_re", "ssm_c_im", "ssm_d", "w_glu", "g_attn_out", "g_ssm_out", "w_out",
         "g_pre_ffn", "g_post_ffn", "w_up", "conv_w", "conv_b", "w_down")
COL_SHARDED = ("w_in", "w_up")


def _full_from_gathered(name, blocks):
    if name in COL_SHARDED:
        n, nl, a, b = blocks.shape
        return jnp.transpose(blocks, (1, 2, 0, 3)).reshape(nl, a, n * b)
    n, nl, a, b = blocks.shape
    return jnp.transpose(blocks, (1, 0, 2, 3)).reshape(nl, n * a, b)


def _blocks_from_full(name, full):
    a, b = full.shape
    if name in COL_SHARDED:
        return jnp.transpose(full.reshape(a, N_DEV, b // N_DEV), (1, 0, 2)).reshape(N_DEV, -1)
    return full.reshape(N_DEV, -1)


def kernel(x, c, w_ada, b_ada, g_pre_mix, g_post_mix, w_in, attn_sinks, lam_re, lam_im, log_step, ssm_b_re, ssm_b_im, ssm_c_re, ssm_c_im, ssm_d, w_glu, g_attn_out, g_ssm_out, w_out, g_pre_ffn, g_post_ffn, w_up, conv_w, conv_b, w_down, loss_target, m_w_ada, m_b_ada, m_g_pre_mix, m_g_post_mix, m_w_in, m_attn_sinks, m_lam_re, m_lam_im, m_log_step, m_ssm_b_re, m_ssm_b_im, m_ssm_c_re, m_ssm_c_im, m_ssm_d, m_w_glu, m_g_attn_out, m_g_ssm_out, m_w_out, m_g_pre_ffn, m_g_post_ffn, m_w_up, m_conv_w, m_conv_b, m_w_down, v_w_ada, v_b_ada, v_g_pre_mix, v_g_post_mix, v_w_in, v_attn_sinks, v_lam_re, v_lam_im, v_log_step, v_ssm_b_re, v_ssm_b_im, v_ssm_c_re, v_ssm_c_im, v_ssm_d, v_w_glu, v_g_attn_out, v_g_ssm_out, v_w_out, v_g_pre_ffn, v_g_post_ffn, v_w_up, v_conv_w, v_conv_b, v_w_down):
    env = dict(locals())
    W = {n: env[n] for n in ORDER}
    M = {n: env["m_" + n] for n in ORDER}
    V = {n: env["v_" + n] for n in ORDER}

    depth = w_ada.shape[0]
    s, d = x.shape[1], x.shape[2]
    xs0 = x.reshape(s, d)
    tgt = loss_target.reshape(s, d)
    attn_w = d // 2
    ssm_w = d - attn_w
    in_cols = w_in.shape[2] * N_DEV
    kv_dim = (in_cols - attn_w - ssm_w) // 2
    n_q, n_kv = attn_w // HEAD_DIM, kv_dim // HEAD_DIM
    n_grp = ssm_w // SSM_GROUP
    nb = ssm_w // LANES
    f = w_down.shape[1] * N_DEV
    ucol = (attn_w + 2 * kv_dim) // LANES
    t_len = _ssm_chunk(s)
    me = 4 * lax.axis_index("x") + 2 * lax.axis_index("y") + lax.axis_index("c")

    extras = [lax.bitcast_convert_type(conv_w, BF16), lax.bitcast_convert_type(c, BF16)]
    wpack = _pack([W[n] for n in BIG] + extras, BF16, 1024)
    gathered = _all_gather(wpack, "ag_weights").reshape(N_DEV, -1)
    shapes = [W[n].shape for n in BIG] + [conv_w.shape + (2,), c.shape + (2,)]
    parts = jax.vmap(lambda row: tuple(_unpack(row, shapes)))(gathered)
    full = {n: _full_from_gathered(n, p) for n, p in zip(BIG, parts[:len(BIG)])}
    conv_w_blocks = lax.bitcast_convert_type(parts[len(BIG)], F32)
    conv_w_full = jnp.transpose(conv_w_blocks, (1, 2, 0, 3)).reshape(depth, 3, 2 * f)
    c_all = lax.bitcast_convert_type(parts[len(BIG) + 1], F32).reshape(N_DEV, d)

    c_pad = jnp.pad(c_all, ((0, 16 - N_DEV), (0, 0)))
    n_ada = w_ada.shape[2]
    b_shard = lax.dynamic_slice_in_dim(b_ada, me * n_ada, n_ada, axis=1).reshape(depth, 1, n_ada)
    ada_part, c_act = _ada_fwd(c_pad, w_ada, b_shard, "ada_fwd")
    ada_all = _all_gather(ada_part.reshape(depth * 16, n_ada), "ag_ada").reshape(N_DEV, depth, 16, n_ada)
    ada_me = lax.dynamic_index_in_dim(ada_all, me, axis=2, keepdims=False)
    ada = jnp.transpose(ada_me, (1, 0, 2)).reshape(depth, 6, 1, d)

    gp = n_grp * STATE

    def hgp(a):
        return jnp.transpose(a, (2, 0, 1)).reshape(SSM_GROUP, gp)

    ssm = []
    for l in range(depth):
        lr, li = lam_re[l].reshape(1, gp), lam_im[l].reshape(1, gp)
        ls = jnp.repeat(log_step[l], STATE).reshape(1, gp)
        br, bi = hgp(ssm_b_re[l]), hgp(ssm_b_im[l])
        bbr, bbi, tab_r, tab_i = _ssm_params_fwd(lr, li, ls, br, bi, t_len, f"ssm_params_fwd{l}")
        bb_re = jnp.transpose(bbr.reshape(SSM_GROUP, n_grp, STATE), (1, 2, 0))
        bb_im = jnp.transpose(bbi.reshape(SSM_GROUP, n_grp, STATE), (1, 2, 0))
        bbd = jnp.concatenate([_to_blocks(bb_re), _to_blocks(bb_im)], axis=2).astype(BF16)
        c_re_t = jnp.transpose(ssm_c_re[l], (0, 2, 1))
        c_im_t = jnp.transpose(ssm_c_im[l], (0, 2, 1))
        ccat = jnp.concatenate([jnp.transpose(_to_blocks(c_re_t), (0, 2, 1)),
                                -jnp.transpose(_to_blocks(c_im_t), (0, 2, 1))], axis=1).astype(BF16)

        def tab(t):
            return t.reshape(t_len, nb, BLOCK_STATES)

        apow = jnp.transpose(jnp.concatenate([tab(tab_r), tab(tab_i)], axis=2), (1, 0, 2))
        ssm.append(dict(lr=lr, li=li, ls=ls, br=br, bi=bi, bbd=bbd, ccat=ccat, apow=apow,
                        apow_rev=jnp.flip(apow, axis=1), dskip=ssm_d[l].reshape(1, ssm_w)))

    sinks_pad = jnp.pad(attn_sinks, ((0, 0), (0, LANES - n_q)))

    def vec(a):
        return a.reshape(1, -1)

    saved = []
    xin = xs0
    for l in range(depth):
        sh_m, sc_m, gt_m, sh_f, sc_f, gt_f = (ada[l, i] for i in range(6))
        p = ssm[l]
        h1 = _modnorm_fwd(xin, vec(g_pre_mix[l]), sc_m, sh_m, f"modnorm_mix_fwd{l}")
        proj = _matmul(h1, full["w_in"][l], name=f"mm_in{l}")
        attn, lse = _attn_fwd(proj, sinks_pad[l:l + 1], n_q=n_q, n_kv=n_kv, name=f"attn_fwd{l}")
        y, z, xstart = _ssm_fwd(proj, ucol, p["bbd"], p["ccat"], p["dskip"], p["apow"], name=f"ssm_fwd{l}")
        gl = _matmul(z, full["w_glu"][l], name=f"mm_glu{l}")
        merged = _merge_fwd(attn, y, gl, vec(g_attn_out[l]), vec(g_ssm_out[l]), f"merge_fwd{l}")
        mix = _matmul(merged, full["w_out"][l], name=f"mm_out{l}")
        x2 = _resnorm_fwd(xin, mix, vec(g_post_mix[l]), gt_m, f"resnorm_mix_fwd{l}")
        h2 = _modnorm_fwd(x2, vec(g_pre_ffn[l]), sc_f, sh_f, f"modnorm_ffn_fwd{l}")
        up0 = _matmul(h2, full["w_up"][l], name=f"mm_up{l}")
        cw, cb = conv_w_full[l], vec(conv_b[l])
        act = _gate_fwd(up0, cw, cb, f"gate_fwd{l}")
        ff = _matmul(act, full["w_down"][l], name=f"mm_down{l}")
        x3 = _resnorm_fwd(x2, ff, vec(g_post_ffn[l]), gt_f, f"resnorm_ffn_fwd{l}")
        saved.append(dict(xin=xin, h1=h1, proj=proj, attn=attn, lse=lse, y=y, z=z, xstart=xstart, gl=gl,
                          merged=merged, mix=mix, x2=x2, h2=h2, up0=up0, act=act, ff=ff))
        xin = x3

    dxo, loss_acc = _loss_bwd(xin, tgt, "loss")
    loss = lax.psum(loss_acc[0, 0], ("x", "y", "c"))

    grads = {n: [None] * depth for n in ORDER}
    dada = [None] * depth
    big_blocks = {n: [None] * depth for n in BIG}
    seg = jnp.pad(jnp.repeat(jnp.eye(n_grp, dtype=F32), STATE, axis=0), ((0, 0), (0, (-n_grp) % LANES)))
    for l in reversed(range(depth)):
        sh_m, sc_m, gt_m, sh_f, sc_f, gt_f = (ada[l, i] for i in range(6))
        a, p = saved[l], ssm[l]
        cw, cb = conv_w_full[l], vec(conv_b[l])
        dff, dg, dgt_f = _resnorm_bwd(dxo, a["ff"], vec(g_post_ffn[l]), gt_f, f"resnorm_ffn_bwd{l}")
        grads["g_post_ffn"][l] = dg
        dact = _matmul(dff, full["w_down"][l], tb=True, name=f"mm_down_dx{l}")
        big_blocks["w_down"][l] = _matmul(a["act"], dff, ta=True, out_dtype=BF16, name=f"mm_down_dw{l}")
        dup, dcb, dcw = _gate_bwd(a["up0"], cw, cb, dact, f"gate_bwd{l}")
        grads["conv_b"][l] = dcb.reshape(1, 2 * f)
        grads["conv_w"][l] = jnp.transpose(dcw, (1, 0, 2)).reshape(3, 2 * f)
        dup0 = _conv_bwd(dup, cw, f"conv_bwd{l}")
        dh2 = _matmul(dup0, full["w_up"][l], tb=True, name=f"mm_up_dx{l}")
        big_blocks["w_up"][l] = _matmul(a["h2"], dup0, ta=True, out_dtype=BF16, name=f"mm_up_dw{l}")
        dx2, dg, dsc_f, dsh_f = _modnorm_bwd(dh2, a["x2"], vec(g_pre_ffn[l]), sc_f, dxo, f"modnorm_ffn_bwd{l}")
        grads["g_pre_ffn"][l] = dg
        dmix, dg, dgt_m = _resnorm_bwd(dx2, a["mix"], vec(g_post_mix[l]), gt_m, f"resnorm_mix_bwd{l}")
        grads["g_post_mix"][l] = dg
        dmerged = _matmul(dmix, full["w_out"][l], tb=True, name=f"mm_out_dx{l}")
        big_blocks["w_out"][l] = _matmul(a["merged"], dmix, ta=True, out_dtype=BF16, name=f"mm_out_dw{l}")
        dattn, dgl, dzd, dga, dgs = _merge_bwd(dmerged, a["attn"], a["y"], a["gl"], vec(g_attn_out[l]),
                                               vec(g_ssm_out[l]), f"merge_bwd{l}")
        grads["g_attn_out"][l], grads["g_ssm_out"][l] = dga, dgs
        dz2 = _matmul(dgl, full["w_glu"][l], tb=True, name=f"mm_glu_dx{l}")
        big_blocks["w_glu"][l] = _matmul(a["z"], dgl, ta=True, out_dtype=BF16, name=f"mm_glu_dw{l}")
        du, dbbd, dccat, dd, da = _ssm_bwd(a["proj"], ucol, a["y"], dzd, dz2, a["xstart"], p["bbd"], p["ccat"],
                                           p["dskip"], p["apow"], p["apow_rev"], name=f"ssm_bwd{l}")
        grads["ssm_d"][l] = dd
        hs = BLOCK_STATES
        dbb_re = _from_blocks(dbbd[:, :, :hs], STATE, SSM_GROUP)
        dbb_im = _from_blocks(dbbd[:, :, hs:], STATE, SSM_GROUP)
        dccat_t = jnp.transpose(dccat, (0, 2, 1))
        grads["ssm_c_re"][l] = jnp.transpose(_from_blocks(dccat_t[:, :, :hs], STATE, SSM_GROUP), (0, 2, 1))
        grads["ssm_c_im"][l] = -jnp.transpose(_from_blocks(dccat_t[:, :, hs:], STATE, SSM_GROUP), (0, 2, 1))
        dab_re, dab_im = da[:, 0, :hs].reshape(1, gp), da[:, 0, hs:].reshape(1, gp)
        dlr, dli, dls, dbr, dbi = _ssm_params_bwd(p["lr"], p["li"], p["ls"], p["br"], p["bi"], dab_re, dab_im,
                                                  hgp(dbb_re), hgp(dbb_im), seg, f"ssm_params_bwd{l}")
        grads["lam_re"][l], grads["lam_im"][l], grads["log_step"][l] = dlr, dli, dls[0, :n_grp]
        grads["ssm_b_re"][l] = jnp.transpose(dbr.reshape(SSM_GROUP, n_grp, STATE), (1, 2, 0))
        grads["ssm_b_im"][l] = jnp.transpose(dbi.reshape(SSM_GROUP, n_grp, STATE), (1, 2, 0))
        dq, dk, dv, dsink = _attn_bwd(a["proj"], sinks_pad[l:l + 1], a["attn"], a["lse"], dattn,
                                      n_q=n_q, n_kv=n_kv, name=f"attn_bwd{l}")
        grads["attn_sinks"][l] = dsink[0, :n_q]
        dproj = jnp.concatenate([dq, dk, dv, du], axis=1).astype(BF16)
        dh1 = _matmul(dproj, full["w_in"][l], tb=True, name=f"mm_in_dx{l}")
        big_blocks["w_in"][l] = _matmul(a["h1"], dproj, ta=True, out_dtype=BF16, name=f"mm_in_dw{l}")
        dxo, dg, dsc_m, dsh_m = _modnorm_bwd(dh1, a["xin"], vec(g_pre_mix[l]), sc_m, dx2, f"modnorm_mix_bwd{l}")
        grads["g_pre_mix"][l] = dg
        dada[l] = jnp.concatenate([dsh_m, dsc_m, dgt_m, dsh_f, dsc_f, dgt_f], axis=1)
    grad_x = dxo.reshape(x.shape)

    gpack = jnp.concatenate(
        [_blocks_from_full(n, big_blocks[n][l]) for n in BIG for l in range(depth)], axis=1)
    unit = 16 * 1024
    gpack = jnp.pad(gpack, ((0, 0), (0, (-gpack.shape[1]) % unit))).reshape(N_DEV, -1, 1024)
    gsum = _sum_slots(_all_to_all(gpack, "a2a_wgrads"), "sum_wgrads").reshape(-1)
    off = 0
    for n in BIG:
        per = math.prod(W[n].shape[1:])
        layers = []
        for l in range(depth):
            layers.append(gsum[off:off + per].reshape(W[n].shape[1:]))
            off += per
        grads[n] = jnp.stack(layers)

    small_order = SMALL + ("conv_w",)
    small_shapes = {n: W[n].shape for n in SMALL}
    small_shapes["conv_w"] = (depth, 3, 2 * f)
    stacked = {"b_ada": jnp.stack(dada).reshape(depth, 6 * d)}
    for n in small_order[1:]:
        stacked[n] = jnp.stack([g.reshape(small_shapes[n][1:]) for g in grads[n]])
    spack = _pack([stacked[n] for n in small_order], F32, 1024)
    sg = _all_gather(spack, "ag_small")
    ssum = _sum_slots(sg, "sum_small").reshape(-1)
    for n, g in zip(small_order, _unpack(ssum, [small_shapes[n] for n in small_order])):
        grads[n] = g
    n_cw = conv_w.shape[2]
    grads["conv_w"] = lax.dynamic_slice_in_dim(grads["conv_w"], me * n_cw, n_cw, axis=2)
    dada_all = sg.reshape(N_DEV, -1)[:, :depth * 6 * d].reshape(N_DEV, depth, 6 * d)
    dada_shard = lax.dynamic_slice_in_dim(dada_all, me * n_ada, n_ada, axis=2)
    kp = LANES
    dada_pad = jnp.pad(jnp.transpose(dada_shard, (1, 0, 2)), ((0, 0), (0, kp - N_DEV), (0, 0)))
    act_t = jnp.pad(jnp.transpose(c_act[:N_DEV]), ((0, 0), (0, kp - N_DEV)))
    grads["w_ada"] = _ada_wgrad(act_t, dada_pad, "ada_wgrad")

    delta, new_m, new_v = {}, {}, {}
    for n in ("w_ada",) + BIG + ("conv_w",):
        delta[n], new_m[n], new_v[n] = _adamw_nd(W[n], grads[n], M[n], V[n], f"adamw_{n}")
    packs = [_pack([t[n] for n in SMALL], F32, 1024) for t in (W, grads, M, V)]
    outs = _adamw(*packs, "adamw_small")
    shapes = [W[n].shape for n in SMALL]
    for tgt_d, o in zip((delta, new_m, new_v), outs):
        for n, val in zip(SMALL, _unpack(o.reshape(-1), shapes)):
            tgt_d[n] = val

    return (loss, grad_x, *[grads[n] for n in ORDER], *[delta[n] for n in ORDER],
            *[new_m[n] for n in ORDER], *[new_v[n] for n in ORDER])
```

```python
import functools
import math

import jax
import jax.numpy as jnp
from jax import lax
from jax.experimental import pallas as pl
from jax.experimental.pallas import tpu as pltpu

F32 = jnp.float32
BF16 = jnp.bfloat16

N_DEV = 8
HEAD_DIM = 64
WINDOW = 128
SSM_GROUP = 16
STATE = 64
LANES = 128
GROUPS_PER_BLOCK = LANES // SSM_GROUP
BLOCK_STATES = GROUPS_PER_BLOCK * STATE
EPS = 1e-6
NEG = -1e30
ADAM_LR, ADAM_B1, ADAM_B2, ADAM_EPS, ADAM_WD, ADAM_STEP = 0.001, 0.9, 0.999, 1e-08, 0.01, 10
VMEM_BYTES_V7X = 64 * 1024 * 1024
GELU_C = math.sqrt(2.0 / math.pi)
MESH = pl.DeviceIdType.MESH
ANY = pl.BlockSpec(memory_space=pl.ANY)


def _pick(n, pref, align):
    t = (min(pref, n) // align) * align
    while t >= align:
        if n % t == 0:
            return t
        t -= align
    return n


def _params(vmem_bytes=None):
    if vmem_bytes is None:
        return pltpu.CompilerParams()
    return pltpu.CompilerParams(vmem_limit_bytes=int(min(vmem_bytes, VMEM_BYTES_V7X - (8 << 20))))


def _gelu(x):
    return 0.5 * x * (1.0 + jnp.tanh(GELU_C * (x + 0.044715 * x * x * x)))


def _gelu_grad(x):
    th = jnp.tanh(GELU_C * (x + 0.044715 * x * x * x))
    return 0.5 * (1.0 + th) + 0.5 * x * (1.0 - th * th) * GELU_C * (1.0 + 3.0 * 0.044715 * x * x)


def _rstd(x):
    return lax.rsqrt(jnp.mean(x * x, axis=-1, keepdims=True) + EPS)


def _norm_bwd(dhat, xhat, r):
    return r * (dhat - xhat * jnp.mean(dhat * xhat, axis=-1, keepdims=True))


def _matmul(a, b, *, ta=False, tb=False, out_dtype=F32, name):
    (kdim, m) = a.shape if ta else a.shape[::-1]
    (n, k2) = b.shape if tb else b.shape[::-1]
    assert kdim == k2, (a.shape, b.shape, ta, tb)
    tm, tn, tk = _pick(m, 1024, LANES), _pick(n, 1024, LANES), _pick(kdim, 2048, LANES)
    nk = kdim // tk
    dn = (((0 if ta else 1,), (1 if tb else 0,)), ((), ()))

    def partial_product(a_ref, b_ref):
        return lax.dot_general(a_ref[...].astype(BF16), b_ref[...].astype(BF16), dn, preferred_element_type=F32)

    def body_one(a_ref, b_ref, o_ref):
        o_ref[...] = partial_product(a_ref, b_ref).astype(o_ref.dtype)

    def body_acc(a_ref, b_ref, o_ref, acc_ref):
        k = pl.program_id(2)

        @pl.when(k == 0)
        def _():
            acc_ref[...] = partial_product(a_ref, b_ref)

        @pl.when((k > 0) & (k < nk - 1))
        def _():
            acc_ref[...] += partial_product(a_ref, b_ref)

        @pl.when(k == nk - 1)
        def _():
            o_ref[...] = (acc_ref[...] + partial_product(a_ref, b_ref)).astype(o_ref.dtype)

    body = body_one if nk == 1 else body_acc
    a_spec = pl.BlockSpec((tk, tm), lambda i, j, k: (k, i)) if ta else pl.BlockSpec((tm, tk), lambda i, j, k: (i, k))
    b_spec = pl.BlockSpec((tn, tk), lambda i, j, k: (j, k)) if tb else pl.BlockSpec((tk, tn), lambda i, j, k: (k, j))
    vmem = (2 * (tm * tk * a.dtype.itemsize + tk * tn * b.dtype.itemsize) + tm * tn * 4
            + 2 * tm * tn * jnp.dtype(out_dtype).itemsize + 3 * tm * tn * 4 + (4 << 20))
    return pl.pallas_call(
        body, name=name, grid=(m // tm, n // tn, nk),
        in_specs=[a_spec, b_spec], out_specs=pl.BlockSpec((tm, tn), lambda i, j, k: (i, j)),
        out_shape=jax.ShapeDtypeStruct((m, n), out_dtype),
        scratch_shapes=[] if nk == 1 else [pltpu.VMEM((tm, tn), F32)],
        compiler_params=_params(vmem),
    )(a, b)


def _all_gather(x, name):
    def body(x_ref, out_ref, send_sems, recv_sems, local_sem):
        x_, y_, c_ = lax.axis_index("x"), lax.axis_index("y"), lax.axis_index("c")
        me, sibling = (x_, y_, c_), (x_, y_, 1 - c_)
        chips = [(1 - x_, y_), (x_, 1 - y_), (1 - x_, 1 - y_)]

        def slot(px, py, pc):
            return out_ref.at[4 * px + 2 * py + pc]

        def copy(k, block, to, src=None):
            return pltpu.make_async_remote_copy(
                src_ref=slot(*block) if src is None else src, dst_ref=slot(*block),
                send_sem=send_sems.at[k], recv_sem=recv_sems.at[k], device_id=to, device_id_type=MESH)

        mine = pltpu.make_async_copy(x_ref, slot(*me), local_sem)
        mine.start()
        first = [copy(0, me, sibling, src=x_ref)]
        first += [copy(1 + j, me, (*chip, c_), src=x_ref) for j, chip in enumerate(chips)]
        for cp in first:
            cp.start()
        passed = [copy(4 + j, (*chip, c_), sibling) for j, chip in enumerate(chips)]
        for j, chip in enumerate(chips):
            copy(1 + j, (*chip, c_), me).wait_recv()
            passed[j].start()
        copy(0, sibling, me).wait_recv()
        for j, chip in enumerate(chips):
            copy(4 + j, (*chip, 1 - c_), me).wait_recv()
        for cp in first + passed:
            cp.wait_send()
        mine.wait()

    return pl.pallas_call(
        body, name=name, out_shape=jax.ShapeDtypeStruct((N_DEV,) + x.shape, x.dtype),
        in_specs=[ANY], out_specs=ANY,
        scratch_shapes=[pltpu.SemaphoreType.DMA((7,)), pltpu.SemaphoreType.DMA((7,)), pltpu.SemaphoreType.DMA],
    )(x)


def _gather_multi(srcs, out_shapes, views, name):
    n = len(srcs)

    def body(*refs):
        src_refs, out_refs = refs[:n], refs[n:2 * n]
        send_sems, recv_sems, local_sems = refs[2 * n:]
        x_, y_, c_ = lax.axis_index("x"), lax.axis_index("y"), lax.axis_index("c")
        me, sibling = (x_, y_, c_), (x_, y_, 1 - c_)
        chips = [(1 - x_, y_), (x_, 1 - y_), (1 - x_, 1 - y_)]

        def slot(i, px, py, pc):
            return views[i](out_refs[i], 4 * px + 2 * py + pc)

        def copy(i, k, block, to, from_src=False):
            return pltpu.make_async_remote_copy(
                src_ref=src_refs[i] if from_src else slot(i, *block), dst_ref=slot(i, *block),
                send_sem=send_sems.at[7 * i + k], recv_sem=recv_sems.at[7 * i + k], device_id=to, device_id_type=MESH)

        mine = [pltpu.make_async_copy(src_refs[i], slot(i, *me), local_sems.at[i]) for i in range(n)]
        for cp in mine:
            cp.start()
        first = []
        for i in range(n):
            first.append(copy(i, 0, me, sibling, True))
            first += [copy(i, 1 + j, me, (*chip, c_), True) for j, chip in enumerate(chips)]
        for cp in first:
            cp.start()
        passed = []
        for j, chip in enumerate(chips):
            for i in range(n):
                copy(i, 1 + j, (*chip, c_), me).wait_recv()
                fwd = copy(i, 4 + j, (*chip, c_), sibling)
                fwd.start()
                passed.append(fwd)
        for i in range(n):
            copy(i, 0, sibling, me).wait_recv()
            for j, chip in enumerate(chips):
                copy(i, 4 + j, (*chip, 1 - c_), me).wait_recv()
        for cp in first + passed:
            cp.wait_send()
        for cp in mine:
            cp.wait()

    return pl.pallas_call(
        body, name=name, out_shape=[jax.ShapeDtypeStruct(s, a.dtype) for s, a in zip(out_shapes, srcs)],
        in_specs=[ANY] * n, out_specs=[ANY] * n,
        scratch_shapes=[pltpu.SemaphoreType.DMA((7 * n,)), pltpu.SemaphoreType.DMA((7 * n,)),
                        pltpu.SemaphoreType.DMA((n,))],
    )(*srcs)


N_CHIPS = 4


def _rs_sibling(parts, views, block_shapes, name):
    n = len(parts)

    def body(*refs):
        part_refs, stage_refs = refs[:n], refs[n:2 * n]
        send_sems, recv_sems = refs[2 * n:]
        x_, y_, c_ = lax.axis_index("x"), lax.axis_index("y"), lax.axis_index("c")
        sibling = (x_, y_, 1 - c_)
        copies = []
        for i in range(n):
            for q in range(N_CHIPS):
                cp = pltpu.make_async_remote_copy(
                    src_ref=views[i](part_refs[i], 2 * q + (1 - c_)), dst_ref=stage_refs[i].at[q],
                    send_sem=send_sems.at[N_CHIPS * i + q], recv_sem=recv_sems.at[N_CHIPS * i + q],
                    device_id=sibling, device_id_type=MESH)
                cp.start()
                copies.append(cp)
        for cp in copies:
            cp.wait_recv()
        for cp in copies:
            cp.wait_send()

    return pl.pallas_call(
        body, name=name,
        out_shape=[jax.ShapeDtypeStruct((N_CHIPS,) + tuple(b), p.dtype) for b, p in zip(block_shapes, parts)],
        in_specs=[ANY] * n, out_specs=[ANY] * n,
        scratch_shapes=[pltpu.SemaphoreType.DMA((N_CHIPS * n,)), pltpu.SemaphoreType.DMA((N_CHIPS * n,))],
    )(*parts)


def _rs_pairsum(core, part, stage, kind, name):
    _, r, c = stage.shape
    tr = _pick(r, 512, 16)
    nt = r // tr
    if kind == "rows":
        part_spec = pl.BlockSpec((tr, c), lambda q, i, cr: ((2 * q + cr[0]) * nt + i, 0))
    elif kind == "cols":
        part_spec = pl.BlockSpec((tr, c), lambda q, i, cr: (i, 2 * q + cr[0]))
    else:
        part_spec = pl.BlockSpec((None, tr, c), lambda q, i, cr: (2 * q + cr[0], i, 0))

    def body(core_ref, p_ref, s_ref, o_ref):
        o_ref[...] = (p_ref[...].astype(F32) + s_ref[...].astype(F32)).astype(o_ref.dtype)

    blk = pl.BlockSpec((None, tr, c), lambda q, i, cr: (q, i, 0))
    return pl.pallas_call(
        body, name=name,
        grid_spec=pltpu.PrefetchScalarGridSpec(num_scalar_prefetch=1, grid=(N_CHIPS, nt),
                                               in_specs=[part_spec, blk], out_specs=blk),
        out_shape=jax.ShapeDtypeStruct(stage.shape, BF16),
    )(core, part, stage)


def _rs_chips(sums, dsts, out_shapes, name):
    n, n_out = len(sums), len(out_shapes)

    def body(*refs):
        sum_refs, out_refs = refs[:n], refs[n:n + n_out]
        send_sems, recv_sems, local_sems = refs[n + n_out:]
        x_, y_, c_ = lax.axis_index("x"), lax.axis_index("y"), lax.axis_index("c")
        q_me = 2 * x_ + y_
        local, copies = [], []
        for i in range(n):
            o, view = dsts[i]
            cp = pltpu.make_async_copy(sum_refs[i].at[q_me], view(out_refs[o], q_me), local_sems.at[i])
            cp.start()
            local.append(cp)
            for k in range(1, N_CHIPS):
                px, py = (1 - x_ if (k >> 1) & 1 else x_), (1 - y_ if k & 1 else y_)
                q_peer = 2 * px + py
                sem = (N_CHIPS - 1) * i + k - 1
                send = pltpu.make_async_remote_copy(
                    src_ref=sum_refs[i].at[q_peer], dst_ref=view(out_refs[o], q_me), send_sem=send_sems.at[sem],
                    recv_sem=recv_sems.at[sem], device_id=(px, py, c_), device_id_type=MESH)
                recv = pltpu.make_async_remote_copy(
                    src_ref=sum_refs[i].at[q_peer], dst_ref=view(out_refs[o], q_peer), send_sem=send_sems.at[sem],
                    recv_sem=recv_sems.at[sem], device_id=(px, py, c_), device_id_type=MESH)
                send.start()
                copies.append((send, recv))
        for send, recv in copies:
            recv.wait_recv()
        for send, recv in copies:
            send.wait_send()
        for cp in local:
            cp.wait()

    m = (N_CHIPS - 1) * n
    return pl.pallas_call(
        body, name=name, out_shape=[jax.ShapeDtypeStruct(s, BF16) for s in out_shapes],
        in_specs=[ANY] * n, out_specs=[ANY] * n_out,
        scratch_shapes=[pltpu.SemaphoreType.DMA((m,)), pltpu.SemaphoreType.DMA((m,)), pltpu.SemaphoreType.DMA((n,))],
    )(*sums)


def _sum_slots(x, name):
    ns, r, c = x.shape
    tr = _pick(r, 512, 16)

    def body(x_ref, o_ref):
        acc = x_ref[0].astype(F32)
        for i in range(1, ns):
            acc = acc + x_ref[i].astype(F32)
        o_ref[...] = acc

    return pl.pallas_call(
        body, name=name, grid=(r // tr,),
        in_specs=[pl.BlockSpec((ns, tr, c), lambda i: (0, i, 0))],
        out_specs=pl.BlockSpec((tr, c), lambda i: (i, 0)),
        out_shape=jax.ShapeDtypeStruct((r, c), F32),
        compiler_params=_params(2 * ns * tr * c * x.dtype.itemsize + 4 * tr * c * 4 + (4 << 20)),
    )(x)


def _cols_from_blocks(blk, name):
    nd, nl, k, n = blk.shape
    tk = _pick(k, 256, 16)

    def body(b_ref, o_ref, wide_ref):
        for dev in range(nd):
            wide_ref[:, dev * n:(dev + 1) * n] = b_ref[dev].astype(F32)
        o_ref[...] = wide_ref[...].astype(o_ref.dtype)

    return pl.pallas_call(
        body, name=name, grid=(nl, k // tk),
        in_specs=[pl.BlockSpec((nd, None, tk, n), lambda l, i: (0, l, i, 0))],
        out_specs=pl.BlockSpec((None, tk, nd * n), lambda l, i: (l, i, 0)),
        out_shape=jax.ShapeDtypeStruct((nl, k, nd * n), BF16),
        scratch_shapes=[pltpu.VMEM((tk, nd * n), F32)],
    )(blk)


def _blocks_from_cols(full, name):
    k, n8 = full.shape
    n = n8 // N_DEV
    tk = _pick(k, 256, 16)

    def body(f_ref, o_ref):
        for dev in range(N_DEV):
            o_ref[dev] = f_ref[:, dev * n:(dev + 1) * n].astype(o_ref.dtype)

    return pl.pallas_call(
        body, name=name, grid=(k // tk,),
        in_specs=[pl.BlockSpec((tk, n8), lambda i: (i, 0))],
        out_specs=pl.BlockSpec((N_DEV, tk, n), lambda i: (0, i, 0)),
        out_shape=jax.ShapeDtypeStruct((N_DEV, k, n), BF16),
    )(full)


def _pack(arrs, dtype, cols):
    flat = jnp.concatenate([a.astype(dtype).reshape(-1) for a in arrs])
    unit = 16 * cols
    pad = (-flat.shape[0]) % unit
    flat = jnp.pad(flat, (0, pad))
    return flat.reshape(-1, cols)


def _unpack(flat, shapes):
    out, off = [], 0
    for s in shapes:
        n = math.prod(s)
        out.append(flat[off:off + n].reshape(s))
        off += n
    return out


def _ada_fwd(c_all, w_ada, b_shard, name):
    nl, d, n = w_ada.shape
    tn = _pick(n, 512, LANES)

    def body(c_ref, w_ref, b_ref, o_ref, act_ref):
        cv = c_ref[...]
        act = cv * jax.nn.sigmoid(cv)
        act_ref[...] = act
        o_ref[...] = jnp.dot(act.astype(BF16), w_ref[...].astype(BF16), preferred_element_type=F32) + b_ref[...]

    return pl.pallas_call(
        body, name=name, grid=(nl, n // tn),
        in_specs=[pl.BlockSpec(c_all.shape, lambda l, j: (0, 0)),
                  pl.BlockSpec((None, d, tn), lambda l, j: (l, 0, j)),
                  pl.BlockSpec((None, 1, tn), lambda l, j: (l, 0, j))],
        out_specs=[pl.BlockSpec((None, c_all.shape[0], tn), lambda l, j: (l, 0, j)),
                   pl.BlockSpec(c_all.shape, lambda l, j: (0, 0))],
        out_shape=[jax.ShapeDtypeStruct((nl, c_all.shape[0], n), F32), jax.ShapeDtypeStruct(c_all.shape, F32)],
        compiler_params=_params(2 * d * tn * 4 + d * tn * 2 + (8 << 20)),
    )(c_all, w_ada, b_shard)


def _ada_wgrad(act_t, dada, name):
    d, kp = act_t.shape
    nl, _, n = dada.shape
    tm = _pick(d, 512, 8)

    def body(a_ref, g_ref, o_ref):
        o_ref[...] = jnp.dot(a_ref[...].astype(BF16), g_ref[...].astype(BF16), preferred_element_type=F32)

    return pl.pallas_call(
        body, name=name, grid=(nl, d // tm),
        in_specs=[pl.BlockSpec((tm, kp), lambda l, i: (i, 0)), pl.BlockSpec((None, kp, n), lambda l, i: (l, 0, 0))],
        out_specs=pl.BlockSpec((None, tm, n), lambda l, i: (l, i, 0)),
        out_shape=jax.ShapeDtypeStruct((nl, d, n), F32),
        compiler_params=_params(4 * tm * n * 4 + 2 * kp * n * 4 + (8 << 20)),
    )(act_t, dada)


def _row_spec(tm, d):
    return pl.BlockSpec((tm, d), lambda i: (i, 0))


def _vec_spec(d):
    return pl.BlockSpec((1, d), lambda i: (0, 0))


def _modnorm_fwd(x, g, sc, sh, name):
    s, d = x.shape
    tm = _pick(s, 256, 16)

    def body(x_ref, g_ref, sc_ref, sh_ref, o_ref):
        xv = x_ref[...]
        o_ref[...] = ((xv * _rstd(xv)) * g_ref[...] * (1.0 + sc_ref[...]) + sh_ref[...]).astype(o_ref.dtype)

    return pl.pallas_call(
        body, name=name, grid=(s // tm,),
        in_specs=[_row_spec(tm, d), _vec_spec(d), _vec_spec(d), _vec_spec(d)], out_specs=_row_spec(tm, d),
        out_shape=jax.ShapeDtypeStruct((s, d), BF16),
    )(x, g, sc, sh)


def _modnorm_bwd(dh, x, g, sc, dres, name):
    s, d = x.shape
    tm = _pick(s, 256, 8)

    def body(dh_ref, x_ref, g_ref, sc_ref, dres_ref, dx_ref, dg_ref, dsc_ref, dsh_ref):
        @pl.when(pl.program_id(0) == 0)
        def _():
            dg_ref[...] = jnp.zeros_like(dg_ref)
            dsc_ref[...] = jnp.zeros_like(dsc_ref)
            dsh_ref[...] = jnp.zeros_like(dsh_ref)

        dh_, xv, gv = dh_ref[...], x_ref[...], g_ref[...]
        r = _rstd(xv)
        xhat = xv * r
        dn = dh_ * (1.0 + sc_ref[...])
        dsh_ref[...] += jnp.sum(dh_, axis=0, keepdims=True)
        dsc_ref[...] += jnp.sum(dh_ * (xhat * gv), axis=0, keepdims=True)
        dg_ref[...] += jnp.sum(dn * xhat, axis=0, keepdims=True)
        dx_ref[...] = _norm_bwd(dn * gv, xhat, r) + dres_ref[...]

    vec = jax.ShapeDtypeStruct((1, d), F32)
    return pl.pallas_call(
        body, name=name, grid=(s // tm,),
        in_specs=[_row_spec(tm, d), _row_spec(tm, d), _vec_spec(d), _vec_spec(d), _row_spec(tm, d)],
        out_specs=[_row_spec(tm, d), _vec_spec(d), _vec_spec(d), _vec_spec(d)],
        out_shape=[jax.ShapeDtypeStruct((s, d), F32), vec, vec, vec],
    )(dh, x, g, sc, dres)


def _resnorm_fwd(x, y, g, gt, name):
    s, d = x.shape
    tm = _pick(s, 256, 8)

    def body(x_ref, y_ref, g_ref, gt_ref, o_ref):
        yv = y_ref[...]
        o_ref[...] = x_ref[...] + (1.0 + gt_ref[...]) * ((yv * _rstd(yv)) * g_ref[...])

    return pl.pallas_call(
        body, name=name, grid=(s // tm,),
        in_specs=[_row_spec(tm, d), _row_spec(tm, d), _vec_spec(d), _vec_spec(d)], out_specs=_row_spec(tm, d),
        out_shape=jax.ShapeDtypeStruct((s, d), F32),
    )(x, y, g, gt)


def _resnorm_bwd(dxo, y, g, gt, name):
    s, d = y.shape
    tm = _pick(s, 256, 16)

    def body(dxo_ref, y_ref, g_ref, gt_ref, dy_ref, dg_ref, dgt_ref):
        @pl.when(pl.program_id(0) == 0)
        def _():
            dg_ref[...] = jnp.zeros_like(dg_ref)
            dgt_ref[...] = jnp.zeros_like(dgt_ref)

        dxo_, yv, gv = dxo_ref[...], y_ref[...], g_ref[...]
        r = _rstd(yv)
        yhat = yv * r
        dn = dxo_ * (1.0 + gt_ref[...])
        dgt_ref[...] += jnp.sum(dxo_ * (yhat * gv), axis=0, keepdims=True)
        dg_ref[...] += jnp.sum(dn * yhat, axis=0, keepdims=True)
        dy_ref[...] = _norm_bwd(dn * gv, yhat, r).astype(dy_ref.dtype)

    vec = jax.ShapeDtypeStruct((1, d), F32)
    return pl.pallas_call(
        body, name=name, grid=(s // tm,),
        in_specs=[_row_spec(tm, d), _row_spec(tm, d), _vec_spec(d), _vec_spec(d)],
        out_specs=[_row_spec(tm, d), _vec_spec(d), _vec_spec(d)],
        out_shape=[jax.ShapeDtypeStruct((s, d), BF16), vec, vec],
    )(dxo, y, g, gt)


def _loss_bwd(xf, tgt, name):
    s, d = xf.shape
    tm = _pick(s, 256, 8)

    def body(x_ref, t_ref, dy_ref, l_ref):
        @pl.when(pl.program_id(0) == 0)
        def _():
            l_ref[...] = jnp.zeros_like(l_ref)

        e = x_ref[...] - t_ref[...]
        dy_ref[...] = e * (1.0 / d)
        l_ref[...] += jnp.sum(e * e) * (0.5 / d)

    return pl.pallas_call(
        body, name=name, grid=(s // tm,),
        in_specs=[_row_spec(tm, d), _row_spec(tm, d)],
        out_specs=[_row_spec(tm, d), pl.BlockSpec((8, LANES), lambda i: (0, 0))],
        out_shape=[jax.ShapeDtypeStruct((s, d), F32), jax.ShapeDtypeStruct((8, LANES), F32)],
    )(xf, tgt)


def _attn_specs(n_q, n_kv):
    aw, kvd = n_q * HEAD_DIM, n_kv * HEAD_DIM
    assert aw % kvd == 0
    kcol = aw // kvd
    q = pl.BlockSpec((WINDOW, aw), lambda n: (n, 0))
    kc = pl.BlockSpec((WINDOW, kvd), lambda n: (n, kcol))
    kp = pl.BlockSpec((WINDOW, kvd), lambda n: (jnp.maximum(n - 1, 0), kcol))
    vc = pl.BlockSpec((WINDOW, kvd), lambda n: (n, kcol + 1))
    vp = pl.BlockSpec((WINDOW, kvd), lambda n: (jnp.maximum(n - 1, 0), kcol + 1))
    return [q, kc, kp, vc, vp]


def _band_mask(n, n_heads):
    qi = lax.broadcasted_iota(jnp.int32, (n_heads * WINDOW, 2 * WINDOW), 0) & (WINDOW - 1)
    kj = lax.broadcasted_iota(jnp.int32, (n_heads * WINDOW, 2 * WINDOW), 1)
    return (kj > qi) & (kj <= qi + WINDOW) & ((kj >= WINDOW) | (n > 0))


def _stack_heads(ref, heads):
    return jnp.concatenate([ref[:, h * HEAD_DIM:(h + 1) * HEAD_DIM] for h in heads], axis=0)


def _stack_sinks(ref, heads):
    return jnp.concatenate([jnp.broadcast_to(ref[:, h:h + 1], (WINDOW, 1)) for h in heads], axis=0)


_NT = (((1,), (1,)), ((), ()))
_TN = (((0,), (0,)), ((), ()))


def _attn_fwd(proj, sinks, *, n_q, n_kv, name):
    s = proj.shape[0]
    aw, grp = n_q * HEAD_DIM, n_q // n_kv

    def body(q_ref, kc_ref, kp_ref, vc_ref, vp_ref, sink_ref, o_ref, lse_ref):
        valid = _band_mask(pl.program_id(0), grp)
        kb = jnp.concatenate([kp_ref[...], kc_ref[...]], axis=0).astype(BF16)
        vb = jnp.concatenate([vp_ref[...], vc_ref[...]], axis=0).astype(BF16)
        lse_ref[...] = jnp.zeros_like(lse_ref)
        for g in range(n_kv):
            heads = range(g * grp, (g + 1) * grp)
            gs = slice(g * HEAD_DIM, (g + 1) * HEAD_DIM)
            qg = _stack_heads(q_ref, heads).astype(BF16)
            sink = _stack_sinks(sink_ref, heads)
            sc = lax.dot_general(qg, kb[:, gs], _NT, preferred_element_type=F32)
            sc = jnp.where(valid, sc * (HEAD_DIM ** -0.5), NEG)
            m = jnp.maximum(jnp.max(sc, axis=-1, keepdims=True), sink)
            e = jnp.exp(sc - m)
            den = jnp.sum(e, axis=-1, keepdims=True) + jnp.exp(sink - m)
            p = e * (1.0 / den)
            og = jnp.dot(p.astype(BF16), vb[:, gs], preferred_element_type=F32)
            lse = m + jnp.log(den)
            for i, h in enumerate(heads):
                rows = slice(i * WINDOW, (i + 1) * WINDOW)
                o_ref[:, h * HEAD_DIM:(h + 1) * HEAD_DIM] = og[rows]
                lse_ref[:, h:h + 1] = lse[rows]

    return pl.pallas_call(
        body, name=name, grid=(s // WINDOW,),
        in_specs=_attn_specs(n_q, n_kv) + [pl.BlockSpec((1, LANES), lambda n: (0, 0))],
        out_specs=[pl.BlockSpec((WINDOW, aw), lambda n: (n, 0)), pl.BlockSpec((WINDOW, LANES), lambda n: (n, 0))],
        out_shape=[jax.ShapeDtypeStruct((s, aw), F32), jax.ShapeDtypeStruct((s, LANES), F32)],
    )(proj, proj, proj, proj, proj, sinks)


def _attn_bwd(proj, sinks, out, lse, dout, *, n_q, n_kv, name):
    s = proj.shape[0]
    aw, kvd, grp = n_q * HEAD_DIM, n_kv * HEAD_DIM, n_q // n_kv
    scale = HEAD_DIM ** -0.5

    def body(q_ref, kc_ref, kp_ref, vc_ref, vp_ref, sink_ref, o_ref, lse_ref, do_ref,
             dq_ref, dk_ref, dv_ref, dsink_ref):
        n = pl.program_id(0)

        @pl.when(n == 0)
        def _():
            dk_ref[...] = jnp.zeros_like(dk_ref)
            dv_ref[...] = jnp.zeros_like(dv_ref)
            dsink_ref[...] = jnp.zeros_like(dsink_ref)

        valid = _band_mask(n, grp)
        kb = jnp.concatenate([kp_ref[...], kc_ref[...]], axis=0).astype(BF16)
        vb = jnp.concatenate([vp_ref[...], vc_ref[...]], axis=0).astype(BF16)
        lane = lax.broadcasted_iota(jnp.int32, (8, LANES), 1)
        dsink = jnp.zeros((8, LANES), F32)
        cur = pl.ds(pl.multiple_of(n * WINDOW, WINDOW), WINDOW)
        prev = pl.ds(pl.multiple_of(jnp.maximum(n - 1, 0) * WINDOW, WINDOW), WINDOW)
        for g in range(n_kv):
            heads = range(g * grp, (g + 1) * grp)
            gs = slice(g * HEAD_DIM, (g + 1) * HEAD_DIM)
            qg = _stack_heads(q_ref, heads).astype(BF16)
            do = _stack_heads(do_ref, heads)
            dob = do.astype(BF16)
            lse = jnp.concatenate([lse_ref[:, h:h + 1] for h in heads], axis=0)
            sc = lax.dot_general(qg, kb[:, gs], _NT, preferred_element_type=F32)
            sc = jnp.where(valid, sc * scale, NEG)
            p = jnp.exp(sc - lse)
            delta = jnp.sum(do * _stack_heads(o_ref, heads), axis=-1, keepdims=True)
            dp = lax.dot_general(dob, vb[:, gs], _NT, preferred_element_type=F32)
            ds = (p * (dp - delta) * scale).astype(BF16)
            dqg = jnp.dot(ds, kb[:, gs], preferred_element_type=F32)
            dkb = lax.dot_general(ds, qg, _TN, preferred_element_type=F32)
            dvb = lax.dot_general(p.astype(BF16), dob, _TN, preferred_element_type=F32)
            sink_term = jnp.exp(_stack_sinks(sink_ref, heads) - lse) * delta
            for i, h in enumerate(heads):
                rows = slice(i * WINDOW, (i + 1) * WINDOW)
                dq_ref[:, h * HEAD_DIM:(h + 1) * HEAD_DIM] = dqg[rows]
                dsink = dsink + jnp.where(lane == h, -jnp.sum(sink_term[rows]), 0.0)
            dk_ref[cur, gs] += dkb[WINDOW:]
            dv_ref[cur, gs] += dvb[WINDOW:]

            @pl.when(n > 0)
            def _():
                dk_ref[prev, gs] += dkb[:WINDOW]
                dv_ref[prev, gs] += dvb[:WINDOW]

        dsink_ref[...] += dsink

    blk = pl.BlockSpec((WINDOW, aw), lambda n: (n, 0))
    kv_full = pl.BlockSpec((s, kvd), lambda n: (0, 0))
    return pl.pallas_call(
        body, name=name, grid=(s // WINDOW,),
        in_specs=_attn_specs(n_q, n_kv) + [pl.BlockSpec((1, LANES), lambda n: (0, 0)), blk,
                                           pl.BlockSpec((WINDOW, LANES), lambda n: (n, 0)), blk],
        out_specs=[blk, kv_full, kv_full, pl.BlockSpec((8, LANES), lambda n: (0, 0))],
        out_shape=[jax.ShapeDtypeStruct((s, aw), F32), jax.ShapeDtypeStruct((s, kvd), F32),
                   jax.ShapeDtypeStruct((s, kvd), F32), jax.ShapeDtypeStruct((8, LANES), F32)],
    )(proj, proj, proj, proj, proj, sinks, out, lse, dout)


def _disc(lr, li, ls):
    dt = jnp.exp(ls)
    mag = jnp.exp(lr * dt)
    ang = li * dt
    ab_re, ab_im = mag * jnp.cos(ang), mag * jnp.sin(ang)
    den = lr * lr + li * li
    f_re = ((ab_re - 1.0) * lr + ab_im * li) / den
    f_im = (ab_im * lr - (ab_re - 1.0) * li) / den
    return ab_re, ab_im, f_re, f_im


POW_ROWS = 8


def _ssm_params_fwd(lr, li, ls, b_re, b_im, name):
    gp = lr.shape[1]
    h = b_re.shape[0]

    def body(lr_ref, li_ref, ls_ref, br_ref, bi_ref, bbr_ref, bbi_ref, tr_ref, ti_ref):
        ab_re, ab_im, f_re, f_im = _disc(lr_ref[...], li_ref[...], ls_ref[...])
        br, bi = br_ref[...], bi_ref[...]
        bbr_ref[...] = f_re * br - f_im * bi
        bbi_ref[...] = f_re * bi + f_im * br
        pr, pi = ab_re, ab_im
        for i in range(POW_ROWS):
            tr_ref[i:i + 1, :] = pr
            ti_ref[i:i + 1, :] = pi
            pr, pi = pr * pr - pi * pi, 2.0 * pr * pi

    mat, tab = jax.ShapeDtypeStruct((h, gp), F32), jax.ShapeDtypeStruct((POW_ROWS, gp), F32)
    return pl.pallas_call(body, name=name, out_shape=[mat, mat, tab, tab])(lr, li, ls, b_re, b_im)


def _ssm_params_bwd(lr, li, ls, b_re, b_im, dab_re, dab_im, dbb_re, dbb_im, seg, name):
    gp = lr.shape[1]
    h = b_re.shape[0]

    def body(lr_ref, li_ref, ls_ref, br_ref, bi_ref, dar_ref, dai_ref, dbbr_ref, dbbi_ref, seg_ref,
             dlr_ref, dli_ref, dls_ref, dbr_ref, dbi_ref):
        lr_, li_, ls_ = lr_ref[...], li_ref[...], ls_ref[...]
        (ab_re, ab_im, f_re, f_im), vjp = jax.vjp(_disc, lr_, li_, ls_)
        br, bi, dbbr, dbbi = br_ref[...], bi_ref[...], dbbr_ref[...], dbbi_ref[...]
        dbr_ref[...] = dbbr * f_re + dbbi * f_im
        dbi_ref[...] = dbbi * f_re - dbbr * f_im
        df_re = jnp.sum(dbbr * br + dbbi * bi, axis=0, keepdims=True)
        df_im = jnp.sum(dbbi * br - dbbr * bi, axis=0, keepdims=True)
        dlr, dli, dls = vjp((dar_ref[...], dai_ref[...], df_re, df_im))
        dlr_ref[...] = dlr
        dli_ref[...] = dli
        dls8 = jnp.broadcast_to(dls, (8, gp))
        dls_ref[...] = jnp.dot(dls8, seg_ref[...], preferred_element_type=F32, precision=lax.Precision.HIGHEST)

    vec, mat = jax.ShapeDtypeStruct((1, gp), F32), jax.ShapeDtypeStruct((h, gp), F32)
    return pl.pallas_call(body, name=name,
                          out_shape=[vec, vec, jax.ShapeDtypeStruct((8, seg.shape[1]), F32), mat, mat],
                          compiler_params=_params(24 << 20))(
        lr, li, ls, b_re, b_im, dab_re, dab_im, dbb_re, dbb_im, seg)


def _scan(xr, xi, apow_ref, t_len, reverse):
    hs = BLOCK_STATES
    row = lax.broadcasted_iota(jnp.int32, xr.shape, 0)
    d = 1
    while d < t_len:
        i = d.bit_length() - 1
        pr, pi = apow_ref[i:i + 1, :hs], apow_ref[i:i + 1, hs:]
        if reverse:
            pi, shift, keep = -pi, t_len - d, row < t_len - d
        else:
            shift, keep = d, row >= d
        sr = jnp.where(keep, pltpu.roll(xr, shift, 0), 0.0)
        si = jnp.where(keep, pltpu.roll(xi, shift, 0), 0.0)
        xr, xi = xr + pr * sr - pi * si, xi + pr * si + pi * sr
        d *= 2
    return xr, xi


def _ssm_chunk(s):
    t_len = _pick(s, 256, 8)
    assert t_len & (t_len - 1) == 0 and t_len <= 1 << POW_ROWS, t_len
    return t_len


def _fold_carry(br, bi, carry_ref, apow_ref, at_row, conj):
    hs = BLOCK_STATES
    cr, ci = carry_ref[0:1, :hs], carry_ref[0:1, hs:]
    ar, ai = apow_ref[0:1, :hs], apow_ref[0:1, hs:]
    if conj:
        ai = -ai
    here = lax.broadcasted_iota(jnp.int32, br.shape, 0) == at_row
    return jnp.where(here, br + (ar * cr - ai * ci), br), jnp.where(here, bi + (ar * ci + ai * cr), bi)


def _ssm_fwd(proj, ucol, bbd, ccat, dskip, apow, t_len, *, name):
    s = proj.shape[0]
    nb = bbd.shape[0]
    nc = s // t_len
    hs = BLOCK_STATES

    def body(u_ref, bbd_ref, ccat_ref, d_ref, apow_ref, y_ref, z_ref, xs_ref, carry_ref):
        @pl.when(pl.program_id(1) == 0)
        def _():
            carry_ref[...] = jnp.zeros_like(carry_ref)

        xs_ref[...] = carry_ref[...]
        u = u_ref[...]
        bu = jnp.dot(u.astype(BF16), bbd_ref[...], preferred_element_type=F32)
        br, bi = _fold_carry(bu[:, :hs], bu[:, hs:], carry_ref, apow_ref, 0, False)
        xr, xi = _scan(br, bi, apow_ref, t_len, False)
        xcat = jnp.concatenate([xr, xi], axis=1)
        carry_ref[...] = jnp.broadcast_to(xcat[t_len - 1:t_len, :], carry_ref.shape)
        y = jnp.dot(xcat.astype(BF16), ccat_ref[...], preferred_element_type=F32) + d_ref[...] * u
        y_ref[...] = y
        z_ref[...] = _gelu(y).astype(z_ref.dtype)

    return pl.pallas_call(
        body, name=name, grid=(nb, nc),
        in_specs=[pl.BlockSpec((t_len, LANES), lambda j, n: (n, ucol + j)),
                  pl.BlockSpec((None, LANES, 2 * hs), lambda j, n: (j, 0, 0)),
                  pl.BlockSpec((None, 2 * hs, LANES), lambda j, n: (j, 0, 0)),
                  pl.BlockSpec((1, LANES), lambda j, n: (0, j)),
                  pl.BlockSpec((None, POW_ROWS, 2 * hs), lambda j, n: (j, 0, 0))],
        out_specs=[pl.BlockSpec((t_len, LANES), lambda j, n: (n, j)),
                   pl.BlockSpec((t_len, LANES), lambda j, n: (n, j)),
                   pl.BlockSpec((None, None, 8, 2 * hs), lambda j, n: (j, n, 0, 0))],
        out_shape=[jax.ShapeDtypeStruct((s, nb * LANES), F32), jax.ShapeDtypeStruct((s, nb * LANES), BF16),
                   jax.ShapeDtypeStruct((nb, nc, 8, 2 * hs), F32)],
        scratch_shapes=[pltpu.VMEM((8, 2 * hs), F32)],
        compiler_params=_params(40 << 20),
    )(proj, bbd, ccat, dskip, apow)


def _ssm_bwd(proj, ucol, y, dzd, dz2, xs, bbd, ccat, dskip, apow, t_len, *, name):
    s = proj.shape[0]
    nb = bbd.shape[0]
    nc = s // t_len
    hs = BLOCK_STATES

    def body(u_ref, y_ref, dzd_ref, dz2_ref, xs_ref, bbd_ref, ccat_ref, d_ref, apow_ref,
             du_ref, dbbd_ref, dccat_ref, dd_ref, da_ref, gcarry_ref):
        @pl.when(pl.program_id(1) == 0)
        def _():
            gcarry_ref[...] = jnp.zeros_like(gcarry_ref)
            dbbd_ref[...] = jnp.zeros_like(dbbd_ref)
            dccat_ref[...] = jnp.zeros_like(dccat_ref)
            dd_ref[...] = jnp.zeros_like(dd_ref)
            da_ref[...] = jnp.zeros_like(da_ref)

        u = u_ref[...]
        ub = u.astype(BF16)
        dy = (dzd_ref[...] + dz2_ref[...]) * _gelu_grad(y_ref[...])
        dyb = dy.astype(BF16)
        bu = jnp.dot(ub, bbd_ref[...], preferred_element_type=F32)
        br, bi = _fold_carry(bu[:, :hs], bu[:, hs:], xs_ref, apow_ref, 0, False)
        xr, xi = _scan(br, bi, apow_ref, t_len, False)
        sr, si = xs_ref[0:1, :hs], xs_ref[0:1, hs:]
        dxd = lax.dot_general(dyb, ccat_ref[...], _NT, preferred_element_type=F32)
        dr, di = _fold_carry(dxd[:, :hs], dxd[:, hs:], gcarry_ref, apow_ref, t_len - 1, True)
        gr, gi = _scan(dr, di, apow_ref, t_len, True)
        gcat = jnp.concatenate([gr, gi], axis=1)
        gcarry_ref[...] = jnp.broadcast_to(gcat[0:1, :], gcarry_ref.shape)
        gb = gcat.astype(BF16)
        du_ref[...] = lax.dot_general(gb, bbd_ref[...], _NT, preferred_element_type=F32) + d_ref[...] * dy
        dbbd_ref[...] += lax.dot_general(ub, gb, _TN, preferred_element_type=F32)
        xb = jnp.concatenate([xr, xi], axis=1).astype(BF16)
        dccat_ref[...] += lax.dot_general(xb, dyb, _TN, preferred_element_type=F32)
        dd_ref[...] += jnp.sum(dy * u, axis=0, keepdims=True)
        first = lax.broadcasted_iota(jnp.int32, xr.shape, 0) == 0
        xpr = jnp.where(first, sr, pltpu.roll(xr, 1, 0))
        xpi = jnp.where(first, si, pltpu.roll(xi, 1, 0))
        dar = jnp.sum(gr * xpr + gi * xpi, axis=0, keepdims=True)
        dai = jnp.sum(gi * xpr - gr * xpi, axis=0, keepdims=True)
        da_ref[...] += jnp.concatenate([dar, dai], axis=1)

    def rows(j, n):
        return nc - 1 - n

    chunk = pl.BlockSpec((t_len, LANES), lambda j, n: (rows(j, n), j))
    return pl.pallas_call(
        body, name=name, grid=(nb, nc),
        in_specs=[pl.BlockSpec((t_len, LANES), lambda j, n: (rows(j, n), ucol + j)), chunk, chunk, chunk,
                  pl.BlockSpec((None, None, 8, 2 * hs), lambda j, n: (j, rows(j, n), 0, 0)),
                  pl.BlockSpec((None, LANES, 2 * hs), lambda j, n: (j, 0, 0)),
                  pl.BlockSpec((None, 2 * hs, LANES), lambda j, n: (j, 0, 0)),
                  pl.BlockSpec((1, LANES), lambda j, n: (0, j)),
                  pl.BlockSpec((None, POW_ROWS, 2 * hs), lambda j, n: (j, 0, 0))],
        out_specs=[chunk,
                   pl.BlockSpec((None, LANES, 2 * hs), lambda j, n: (j, 0, 0)),
                   pl.BlockSpec((None, 2 * hs, LANES), lambda j, n: (j, 0, 0)),
                   pl.BlockSpec((1, LANES), lambda j, n: (0, j)),
                   pl.BlockSpec((None, 1, 2 * hs), lambda j, n: (j, 0, 0))],
        out_shape=[jax.ShapeDtypeStruct((s, nb * LANES), F32),
                   jax.ShapeDtypeStruct((nb, LANES, 2 * hs), F32),
                   jax.ShapeDtypeStruct((nb, 2 * hs, LANES), F32),
                   jax.ShapeDtypeStruct((1, nb * LANES), F32),
                   jax.ShapeDtypeStruct((nb, 1, 2 * hs), F32)],
        scratch_shapes=[pltpu.VMEM((8, 2 * hs), F32)],
        compiler_params=_params(48 << 20),
    )(proj, y, dzd, dz2, xs, bbd, ccat, dskip, apow)


def _to_blocks(a):
    g, p, k = a.shape
    nb = g // GROUPS_PER_BLOCK
    eye = jnp.eye(GROUPS_PER_BLOCK, dtype=a.dtype)
    a4 = a.reshape(nb, GROUPS_PER_BLOCK, p, k)
    out = jnp.einsum("ab,jbpk->jakbp", eye, a4)
    return out.reshape(nb, GROUPS_PER_BLOCK * k, GROUPS_PER_BLOCK * p)


def _from_blocks(d, p, k):
    nb = d.shape[0]
    d5 = d.reshape(nb, GROUPS_PER_BLOCK, k, GROUPS_PER_BLOCK, p)
    eye = jnp.eye(GROUPS_PER_BLOCK, dtype=bool)[None, :, None, :, None]
    diag = jnp.sum(jnp.where(eye, d5, 0.0), axis=1)
    return jnp.transpose(diag, (0, 2, 3, 1)).reshape(nb * GROUPS_PER_BLOCK, p, k)


def _merge_fwd(attn, y, gl, g_a, g_s, name):
    s, wa = attn.shape
    ws = y.shape[1]
    tm = _pick(s, 256, 16)

    def body(a_ref, y_ref, gl_ref, ga_ref, gs_ref, o_ref):
        av = a_ref[...]
        o_ref[:, :wa] = ((av * _rstd(av)) * ga_ref[...]).astype(o_ref.dtype)
        sv = _gelu(y_ref[...]) * jax.nn.sigmoid(gl_ref[...])
        o_ref[:, wa:] = ((sv * _rstd(sv)) * gs_ref[...]).astype(o_ref.dtype)

    return pl.pallas_call(
        body, name=name, grid=(s // tm,),
        in_specs=[_row_spec(tm, wa), _row_spec(tm, ws), _row_spec(tm, ws), _vec_spec(wa), _vec_spec(ws)],
        out_specs=_row_spec(tm, wa + ws), out_shape=jax.ShapeDtypeStruct((s, wa + ws), BF16),
    )(attn, y, gl, g_a, g_s)


def _merge_bwd(dmerged, attn, y, gl, g_a, g_s, name):
    s, wa = attn.shape
    ws = y.shape[1]
    tm = _pick(s, 256, 16)

    def body(dm_ref, a_ref, y_ref, gl_ref, ga_ref, gs_ref, da_ref, dgl_ref, dzd_ref, dga_ref, dgs_ref):
        @pl.when(pl.program_id(0) == 0)
        def _():
            dga_ref[...] = jnp.zeros_like(dga_ref)
            dgs_ref[...] = jnp.zeros_like(dgs_ref)

        dan, dsn = dm_ref[:, :wa], dm_ref[:, wa:]
        av = a_ref[...]
        ra = _rstd(av)
        ahat = av * ra
        dga_ref[...] += jnp.sum(dan * ahat, axis=0, keepdims=True)
        da_ref[...] = _norm_bwd(dan * ga_ref[...], ahat, ra)
        z = _gelu(y_ref[...])
        sig = jax.nn.sigmoid(gl_ref[...])
        sv = z * sig
        rs = _rstd(sv)
        shat = sv * rs
        dgs_ref[...] += jnp.sum(dsn * shat, axis=0, keepdims=True)
        dssm = _norm_bwd(dsn * gs_ref[...], shat, rs)
        dzd_ref[...] = dssm * sig
        dgl_ref[...] = (dssm * z * sig * (1.0 - sig)).astype(dgl_ref.dtype)

    return pl.pallas_call(
        body, name=name, grid=(s // tm,),
        in_specs=[_row_spec(tm, wa + ws), _row_spec(tm, wa), _row_spec(tm, ws), _row_spec(tm, ws),
                  _vec_spec(wa), _vec_spec(ws)],
        out_specs=[_row_spec(tm, wa), _row_spec(tm, ws), _row_spec(tm, ws), _vec_spec(wa), _vec_spec(ws)],
        out_shape=[jax.ShapeDtypeStruct((s, wa), F32), jax.ShapeDtypeStruct((s, ws), BF16),
                   jax.ShapeDtypeStruct((s, ws), F32), jax.ShapeDtypeStruct((1, wa), F32),
                   jax.ShapeDtypeStruct((1, ws), F32)],
    )(dmerged, attn, y, gl, g_a, g_s)


def _shift_down(main, halo, k):
    rolled = pltpu.roll(main, k, 0)
    row = lax.broadcasted_iota(jnp.int32, main.shape, 0)
    for r in range(k):
        rolled = jnp.where(row == r, halo[8 - k + r:8 - k + r + 1, :], rolled)
    return rolled


def _shift_up(main, halo, k):
    tm = main.shape[0]
    rolled = pltpu.roll(main, tm - k, 0)
    row = lax.broadcasted_iota(jnp.int32, main.shape, 0)
    for r in range(k):
        rolled = jnp.where(row == tm - k + r, halo[r:r + 1, :], rolled)
    return rolled


def _conv(main, halo, w_ref, b_ref):
    return (b_ref[...] + w_ref[0:1, :] * _shift_down(main, halo, 2) + w_ref[1:2, :] * _shift_down(main, halo, 1)
            + w_ref[2:3, :] * main)


def _gate_tiles(s, f):
    return _pick(s, 256, 16), _pick(f, 512, LANES)


def _gate_in_specs(tm, tn, nfb, order):
    hb = tm // 8
    ij = (lambda a, b: (b, a)) if order == "ji" else (lambda a, b: (a, b))

    def main(off):
        return pl.BlockSpec((tm, tn), lambda a, b: (ij(a, b)[0], ij(a, b)[1] + off))

    def halo(off):
        return pl.BlockSpec((8, tn), lambda a, b: (jnp.maximum(ij(a, b)[0] * hb - 1, 0), ij(a, b)[1] + off))

    def vec(rows, off):
        return pl.BlockSpec((rows, tn), lambda a, b: (0, ij(a, b)[1] + off))

    return [main(0), main(nfb), halo(0), halo(nfb), vec(3, 0), vec(3, nfb), vec(1, 0), vec(1, nfb)]


def _gate_fwd(up0, conv_w, conv_b, name):
    s, f2 = up0.shape
    f = f2 // 2
    tm, tn = _gate_tiles(s, f)
    nfb = f // tn

    def body(v_ref, g_ref, vh_ref, gh_ref, wv_ref, wg_ref, bv_ref, bg_ref, o_ref):
        top = pl.program_id(0) == 0
        vh = jnp.where(top, 0.0, vh_ref[...])
        gh = jnp.where(top, 0.0, gh_ref[...])
        val = _conv(v_ref[...], vh, wv_ref, bv_ref)
        gate = _conv(g_ref[...], gh, wg_ref, bg_ref)
        o_ref[...] = (_gelu(gate) * val).astype(o_ref.dtype)

    return pl.pallas_call(
        body, name=name, grid=(s // tm, nfb),
        in_specs=_gate_in_specs(tm, tn, nfb, "ij"), out_specs=pl.BlockSpec((tm, tn), lambda i, j: (i, j)),
        out_shape=jax.ShapeDtypeStruct((s, f), BF16),
    )(up0, up0, up0, up0, conv_w, conv_w, conv_b, conv_b)


def _gate_bwd(up0, conv_w, conv_b, da, name):
    s, f2 = up0.shape
    f = f2 // 2
    tm, tn = _gate_tiles(s, f)
    nfb = f // tn

    def body(v_ref, g_ref, vh_ref, gh_ref, wv_ref, wg_ref, bv_ref, bg_ref, da_ref, dup_ref, dcb_ref, dcw_ref):
        top = pl.program_id(1) == 0

        @pl.when(top)
        def _():
            dcb_ref[...] = jnp.zeros_like(dcb_ref)
            dcw_ref[...] = jnp.zeros_like(dcw_ref)

        halos = (jnp.where(top, 0.0, vh_ref[...]), jnp.where(top, 0.0, gh_ref[...]))
        mains = (v_ref[...], g_ref[...])
        val = _conv(mains[0], halos[0], wv_ref, bv_ref)
        gate = _conv(mains[1], halos[1], wg_ref, bg_ref)
        dav = da_ref[...]
        dups = (dav * _gelu(gate), dav * val * _gelu_grad(gate))
        for half in range(2):
            dup = dups[half]
            dup_ref[half] = dup
            dcb_ref[half] += jnp.sum(dup, axis=0, keepdims=True)
            dcw_ref[half, 0:1, :] += jnp.sum(dup * _shift_down(mains[half], halos[half], 2), axis=0, keepdims=True)
            dcw_ref[half, 1:2, :] += jnp.sum(dup * _shift_down(mains[half], halos[half], 1), axis=0, keepdims=True)
            dcw_ref[half, 2:3, :] += jnp.sum(dup * mains[half], axis=0, keepdims=True)

    return pl.pallas_call(
        body, name=name, grid=(nfb, s // tm),
        in_specs=_gate_in_specs(tm, tn, nfb, "ji") + [pl.BlockSpec((tm, tn), lambda j, i: (i, j))],
        out_specs=[pl.BlockSpec((2, tm, tn), lambda j, i: (0, i, j)),
                   pl.BlockSpec((2, 1, tn), lambda j, i: (0, 0, j)),
                   pl.BlockSpec((2, 3, tn), lambda j, i: (0, 0, j))],
        out_shape=[jax.ShapeDtypeStruct((2, s, f), F32), jax.ShapeDtypeStruct((2, 1, f), F32),
                   jax.ShapeDtypeStruct((2, 3, f), F32)],
    )(up0, up0, up0, up0, conv_w, conv_w, conv_b, conv_b, da)


def _conv_bwd(dup, conv_w, name):
    _, s, f = dup.shape
    tm, tn = _gate_tiles(s, f)
    nfb, ni, hb = f // tn, s // tm, tm // 8

    def body(d_ref, dh_ref, w_ref, o_ref):
        main = d_ref[...]
        halo = jnp.where(pl.program_id(1) == ni - 1, 0.0, dh_ref[...])
        o_ref[...] = (w_ref[2:3, :] * main + w_ref[1:2, :] * _shift_up(main, halo, 1)
                      + w_ref[0:1, :] * _shift_up(main, halo, 2)).astype(o_ref.dtype)

    return pl.pallas_call(
        body, name=name, grid=(2, ni, nfb),
        in_specs=[pl.BlockSpec((None, tm, tn), lambda h, i, j: (h, i, j)),
                  pl.BlockSpec((None, 8, tn), lambda h, i, j: (h, jnp.minimum((i + 1) * hb, s // 8 - 1), j)),
                  pl.BlockSpec((3, tn), lambda h, i, j: (0, h * nfb + j))],
        out_specs=pl.BlockSpec((tm, tn), lambda h, i, j: (i, h * nfb + j)),
        out_shape=jax.ShapeDtypeStruct((s, 2 * f), BF16),
    )(dup, dup, conv_w)


def _adamw(w, g, m, v, name):
    r, c = w.shape
    tr = _pick(r, max(8, (1 << 19) // max(c, 1) // 8 * 8), 8)
    c1, c2 = 1.0 / (1.0 - ADAM_B1 ** ADAM_STEP), 1.0 / (1.0 - ADAM_B2 ** ADAM_STEP)

    def body(w_ref, g_ref, m_ref, v_ref, d_ref, nm_ref, nv_ref):
        gv = g_ref[...]
        nm = ADAM_B1 * m_ref[...] + (1.0 - ADAM_B1) * gv
        nv = ADAM_B2 * v_ref[...] + (1.0 - ADAM_B2) * (gv * gv)
        nm_ref[...] = nm
        nv_ref[...] = nv
        d_ref[...] = -ADAM_LR * ((nm * c1) / (jnp.sqrt(nv * c2) + ADAM_EPS) + ADAM_WD * w_ref[...])

    spec = pl.BlockSpec((tr, c), lambda i: (i, 0))
    out = jax.ShapeDtypeStruct((r, c), F32)
    return pl.pallas_call(body, name=name, grid=(r // tr,), in_specs=[spec] * 4, out_specs=[spec] * 3,
                          out_shape=[out] * 3, compiler_params=_params(14 * tr * c * 4 + (4 << 20)))(w, g, m, v)


def _adamw_nd(w, g, m, v, name):
    shape = w.shape
    c = shape[-1]
    outs = _adamw(w.reshape(-1, c), g.reshape(-1, c), m.reshape(-1, c), v.reshape(-1, c), name)
    return [o.reshape(shape) for o in outs]


BIG = ("w_in", "w_glu", "w_out", "w_up", "w_down")
SMALL = ("b_ada", "g_pre_mix", "g_post_mix", "attn_sinks", "lam_re", "lam_im", "log_step", "ssm_b_re", "ssm_b_im",
         "ssm_c_re", "ssm_c_im", "ssm_d", "g_attn_out", "g_ssm_out", "g_pre_ffn", "g_post_ffn", "conv_b")
ORDER = ("w_ada", "b_ada", "g_pre_mix", "g_post_mix", "w_in", "attn_sinks", "lam_re", "lam_im", "log_step",
         "ssm_b_re", "ssm_b_im", "ssm_c_re", "ssm_c_im", "ssm_d", "w_glu", "g_attn_out", "g_ssm_out", "w_out",
         "g_pre_ffn", "g_post_ffn", "w_up", "conv_w", "conv_b", "w_down")
COL_SHARDED = ("w_in", "w_up")


def kernel(x, c, w_ada, b_ada, g_pre_mix, g_post_mix, w_in, attn_sinks, lam_re, lam_im, log_step, ssm_b_re, ssm_b_im, ssm_c_re, ssm_c_im, ssm_d, w_glu, g_attn_out, g_ssm_out, w_out, g_pre_ffn, g_post_ffn, w_up, conv_w, conv_b, w_down, loss_target, m_w_ada, m_b_ada, m_g_pre_mix, m_g_post_mix, m_w_in, m_attn_sinks, m_lam_re, m_lam_im, m_log_step, m_ssm_b_re, m_ssm_b_im, m_ssm_c_re, m_ssm_c_im, m_ssm_d, m_w_glu, m_g_attn_out, m_g_ssm_out, m_w_out, m_g_pre_ffn, m_g_post_ffn, m_w_up, m_conv_w, m_conv_b, m_w_down, v_w_ada, v_b_ada, v_g_pre_mix, v_g_post_mix, v_w_in, v_attn_sinks, v_lam_re, v_lam_im, v_log_step, v_ssm_b_re, v_ssm_b_im, v_ssm_c_re, v_ssm_c_im, v_ssm_d, v_w_glu, v_g_attn_out, v_g_ssm_out, v_w_out, v_g_pre_ffn, v_g_post_ffn, v_w_up, v_conv_w, v_conv_b, v_w_down):
    env = dict(locals())
    W = {n: env[n] for n in ORDER}
    M = {n: env["m_" + n] for n in ORDER}
    V = {n: env["v_" + n] for n in ORDER}

    depth = w_ada.shape[0]
    s, d = x.shape[1], x.shape[2]
    xs0 = x.reshape(s, d)
    tgt = loss_target.reshape(s, d)
    attn_w = d // 2
    ssm_w = d - attn_w
    in_cols = w_in.shape[2] * N_DEV
    kv_dim = (in_cols - attn_w - ssm_w) // 2
    n_q, n_kv = attn_w // HEAD_DIM, kv_dim // HEAD_DIM
    n_grp = ssm_w // SSM_GROUP
    nb = ssm_w // LANES
    f = w_down.shape[1] * N_DEV
    ucol = (attn_w + 2 * kv_dim) // LANES
    t_len = _ssm_chunk(s)
    me = 4 * lax.axis_index("x") + 2 * lax.axis_index("y") + lax.axis_index("c")

    def at_block(ref, idx):
        return ref.at[idx]

    def at_rows(n_rows):
        return lambda ref, idx: ref.at[:, pl.ds(pl.multiple_of(idx * n_rows, 8), n_rows), :]

    def at_cols(n_cols):
        return lambda ref, idx: ref.at[:, :, pl.ds(pl.multiple_of(idx * n_cols, LANES), n_cols)]

    srcs, out_shapes, views = [], [], []
    for n in BIG:
        shp = W[n].shape
        srcs.append(W[n].astype(BF16))
        if n == "w_in":
            out_shapes.append((N_DEV,) + shp)
            views.append(at_block)
        elif n in COL_SHARDED:
            out_shapes.append(shp[:2] + (N_DEV * shp[2],))
            views.append(at_cols(shp[2]))
        else:
            out_shapes.append((shp[0], N_DEV * shp[1], shp[2]))
            views.append(at_rows(shp[1]))
    srcs += [conv_w, c]
    out_shapes += [(N_DEV,) + conv_w.shape, (N_DEV,) + c.shape]
    views += [at_block, at_block]
    gathered = _gather_multi(srcs, out_shapes, views, "ag_weights")
    full = dict(zip(BIG, gathered[:len(BIG)]))
    full["w_in"] = _cols_from_blocks(full["w_in"], "w_in_layout")
    conv_w_full = jnp.transpose(gathered[len(BIG)], (1, 2, 0, 3)).reshape(depth, 3, 2 * f)
    c_all = gathered[len(BIG) + 1].reshape(N_DEV, d)

    c_pad = jnp.pad(c_all, ((0, 16 - N_DEV), (0, 0)))
    n_ada = w_ada.shape[2]
    b_shard = lax.dynamic_slice_in_dim(b_ada, me * n_ada, n_ada, axis=1).reshape(depth, 1, n_ada)
    ada_part, c_act = _ada_fwd(c_pad, w_ada, b_shard, "ada_fwd")
    ada_all = _all_gather(ada_part.reshape(depth * 16, n_ada), "ag_ada").reshape(N_DEV, depth, 16, n_ada)
    ada_me = lax.dynamic_index_in_dim(ada_all, me, axis=2, keepdims=False)
    ada = jnp.transpose(ada_me, (1, 0, 2)).reshape(depth, 6, 1, d)

    gp = n_grp * STATE

    def hgp(a):
        return jnp.transpose(a, (2, 0, 1)).reshape(SSM_GROUP, gp)

    ssm = []
    for l in range(depth):
        lr, li = lam_re[l].reshape(1, gp), lam_im[l].reshape(1, gp)
        ls = jnp.repeat(log_step[l], STATE).reshape(1, gp)
        br, bi = hgp(ssm_b_re[l]), hgp(ssm_b_im[l])
        bbr, bbi, tab_r, tab_i = _ssm_params_fwd(lr, li, ls, br, bi, f"ssm_params_fwd{l}")
        bb_re = jnp.transpose(bbr.reshape(SSM_GROUP, n_grp, STATE), (1, 2, 0))
        bb_im = jnp.transpose(bbi.reshape(SSM_GROUP, n_grp, STATE), (1, 2, 0))
        bbd = jnp.concatenate([_to_blocks(bb_re), _to_blocks(bb_im)], axis=2).astype(BF16)
        c_re_t = jnp.transpose(ssm_c_re[l], (0, 2, 1))
        c_im_t = jnp.transpose(ssm_c_im[l], (0, 2, 1))
        ccat = jnp.concatenate([jnp.transpose(_to_blocks(c_re_t), (0, 2, 1)),
                                -jnp.transpose(_to_blocks(c_im_t), (0, 2, 1))], axis=1).astype(BF16)

        def tab(t):
            return t.reshape(POW_ROWS, nb, BLOCK_STATES)

        apow = jnp.transpose(jnp.concatenate([tab(tab_r), tab(tab_i)], axis=2), (1, 0, 2))
        ssm.append(dict(lr=lr, li=li, ls=ls, br=br, bi=bi, bbd=bbd, ccat=ccat, apow=apow,
                        dskip=ssm_d[l].reshape(1, ssm_w)))

    sinks_pad = jnp.pad(attn_sinks, ((0, 0), (0, LANES - n_q)))

    def vec(a):
        return a.reshape(1, -1)

    saved = []
    xin = xs0
    for l in range(depth):
        sh_m, sc_m, gt_m, sh_f, sc_f, gt_f = (ada[l, i] for i in range(6))
        p = ssm[l]
        h1 = _modnorm_fwd(xin, vec(g_pre_mix[l]), sc_m, sh_m, f"modnorm_mix_fwd{l}")
        proj = _matmul(h1, full["w_in"][l], name=f"mm_in{l}")
        attn, lse = _attn_fwd(proj, sinks_pad[l:l + 1], n_q=n_q, n_kv=n_kv, name=f"attn_fwd{l}")
        y, z, xstart = _ssm_fwd(proj, ucol, p["bbd"], p["ccat"], p["dskip"], p["apow"], t_len, name=f"ssm_fwd{l}")
        gl = _matmul(z, full["w_glu"][l], name=f"mm_glu{l}")
        merged = _merge_fwd(attn, y, gl, vec(g_attn_out[l]), vec(g_ssm_out[l]), f"merge_fwd{l}")
        mix = _matmul(merged, full["w_out"][l], name=f"mm_out{l}")
        x2 = _resnorm_fwd(xin, mix, vec(g_post_mix[l]), gt_m, f"resnorm_mix_fwd{l}")
        h2 = _modnorm_fwd(x2, vec(g_pre_ffn[l]), sc_f, sh_f, f"modnorm_ffn_fwd{l}")
        up0 = _matmul(h2, full["w_up"][l], name=f"mm_up{l}")
        cw, cb = conv_w_full[l], vec(conv_b[l])
        act = _gate_fwd(up0, cw, cb, f"gate_fwd{l}")
        ff = _matmul(act, full["w_down"][l], name=f"mm_down{l}")
        x3 = _resnorm_fwd(x2, ff, vec(g_post_ffn[l]), gt_f, f"resnorm_ffn_fwd{l}")
        saved.append(dict(xin=xin, h1=h1, proj=proj, attn=attn, lse=lse, y=y, z=z, xstart=xstart, gl=gl,
                          merged=merged, mix=mix, x2=x2, h2=h2, up0=up0, act=act, ff=ff))
        xin = x3

    dxo, loss_acc = _loss_bwd(xin, tgt, "loss")
    loss = lax.psum(loss_acc[0, 0], ("x", "y", "c"))

    grads = {n: [None] * depth for n in ORDER}
    dada = [None] * depth
    big_blocks = {n: [None] * depth for n in BIG}
    seg = jnp.pad(jnp.repeat(jnp.eye(n_grp, dtype=F32), STATE, axis=0), ((0, 0), (0, (-n_grp) % LANES)))
    for l in reversed(range(depth)):
        sh_m, sc_m, gt_m, sh_f, sc_f, gt_f = (ada[l, i] for i in range(6))
        a, p = saved[l], ssm[l]
        cw, cb = conv_w_full[l], vec(conv_b[l])
        dff, dg, dgt_f = _resnorm_bwd(dxo, a["ff"], vec(g_post_ffn[l]), gt_f, f"resnorm_ffn_bwd{l}")
        grads["g_post_ffn"][l] = dg
        dact = _matmul(dff, full["w_down"][l], tb=True, name=f"mm_down_dx{l}")
        big_blocks["w_down"][l] = _matmul(a["act"], dff, ta=True, out_dtype=BF16, name=f"mm_down_dw{l}")
        dup, dcb, dcw = _gate_bwd(a["up0"], cw, cb, dact, f"gate_bwd{l}")
        grads["conv_b"][l] = dcb.reshape(1, 2 * f)
        grads["conv_w"][l] = jnp.transpose(dcw, (1, 0, 2)).reshape(3, 2 * f)
        dup0 = _conv_bwd(dup, cw, f"conv_bwd{l}")
        dh2 = _matmul(dup0, full["w_up"][l], tb=True, name=f"mm_up_dx{l}")
        big_blocks["w_up"][l] = _matmul(a["h2"], dup0, ta=True, out_dtype=BF16, name=f"mm_up_dw{l}")
        dx2, dg, dsc_f, dsh_f = _modnorm_bwd(dh2, a["x2"], vec(g_pre_ffn[l]), sc_f, dxo, f"modnorm_ffn_bwd{l}")
        grads["g_pre_ffn"][l] = dg
        dmix, dg, dgt_m = _resnorm_bwd(dx2, a["mix"], vec(g_post_mix[l]), gt_m, f"resnorm_mix_bwd{l}")
        grads["g_post_mix"][l] = dg
        dmerged = _matmul(dmix, full["w_out"][l], tb=True, name=f"mm_out_dx{l}")
        big_blocks["w_out"][l] = _matmul(a["merged"], dmix, ta=True, out_dtype=BF16, name=f"mm_out_dw{l}")
        dattn, dgl, dzd, dga, dgs = _merge_bwd(dmerged, a["attn"], a["y"], a["gl"], vec(g_attn_out[l]),
                                               vec(g_ssm_out[l]), f"merge_bwd{l}")
        grads["g_attn_out"][l], grads["g_ssm_out"][l] = dga, dgs
        dz2 = _matmul(dgl, full["w_glu"][l], tb=True, name=f"mm_glu_dx{l}")
        big_blocks["w_glu"][l] = _matmul(a["z"], dgl, ta=True, out_dtype=BF16, name=f"mm_glu_dw{l}")
        du, dbbd, dccat, dd, da = _ssm_bwd(a["proj"], ucol, a["y"], dzd, dz2, a["xstart"], p["bbd"], p["ccat"],
                                           p["dskip"], p["apow"], t_len, name=f"ssm_bwd{l}")
        grads["ssm_d"][l] = dd
        hs = BLOCK_STATES
        dbb_re = _from_blocks(dbbd[:, :, :hs], STATE, SSM_GROUP)
        dbb_im = _from_blocks(dbbd[:, :, hs:], STATE, SSM_GROUP)
        dccat_t = jnp.transpose(dccat, (0, 2, 1))
        grads["ssm_c_re"][l] = jnp.transpose(_from_blocks(dccat_t[:, :, :hs], STATE, SSM_GROUP), (0, 2, 1))
        grads["ssm_c_im"][l] = -jnp.transpose(_from_blocks(dccat_t[:, :, hs:], STATE, SSM_GROUP), (0, 2, 1))
        dab_re, dab_im = da[:, 0, :hs].reshape(1, gp), da[:, 0, hs:].reshape(1, gp)
        dlr, dli, dls, dbr, dbi = _ssm_params_bwd(p["lr"], p["li"], p["ls"], p["br"], p["bi"], dab_re, dab_im,
                                                  hgp(dbb_re), hgp(dbb_im), seg, f"ssm_params_bwd{l}")
        grads["lam_re"][l], grads["lam_im"][l], grads["log_step"][l] = dlr, dli, dls[0, :n_grp]
        grads["ssm_b_re"][l] = jnp.transpose(dbr.reshape(SSM_GROUP, n_grp, STATE), (1, 2, 0))
        grads["ssm_b_im"][l] = jnp.transpose(dbi.reshape(SSM_GROUP, n_grp, STATE), (1, 2, 0))
        dq, dk, dv, dsink = _attn_bwd(a["proj"], sinks_pad[l:l + 1], a["attn"], a["lse"], dattn,
                                      n_q=n_q, n_kv=n_kv, name=f"attn_bwd{l}")
        grads["attn_sinks"][l] = dsink[0, :n_q]
        dproj = jnp.concatenate([dq, dk, dv, du], axis=1).astype(BF16)
        dh1 = _matmul(dproj, full["w_in"][l], tb=True, name=f"mm_in_dx{l}")
        big_blocks["w_in"][l] = _blocks_from_cols(_matmul(a["h1"], dproj, ta=True, name=f"mm_in_dw{l}"),
                                                  f"w_in_grad_layout{l}")
        dxo, dg, dsc_m, dsh_m = _modnorm_bwd(dh1, a["xin"], vec(g_pre_mix[l]), sc_m, dx2, f"modnorm_mix_bwd{l}")
        grads["g_pre_mix"][l] = dg
        dada[l] = jnp.concatenate([dsh_m, dsc_m, dgt_m, dsh_f, dsc_f, dgt_f], axis=1)
    grad_x = dxo.reshape(x.shape)

    def part_view(n):
        shp = W[n].shape
        if n == "w_in":
            return at_block, "blk"
        if n in COL_SHARDED:
            return (lambda ref, idx: ref.at[:, pl.ds(pl.multiple_of(idx * shp[2], LANES), shp[2])]), "cols"
        return (lambda ref, idx: ref.at[pl.ds(pl.multiple_of(idx * shp[1], 8), shp[1]), :]), "rows"

    items = [(n, l) for n in BIG for l in range(depth)]
    parts = [big_blocks[n][l] for n, l in items]
    stages = _rs_sibling(parts, [part_view(n)[0] for n, _ in items], [W[n].shape[1:] for n, _ in items],
                         "rs_sibling")
    core = lax.axis_index("c").astype(jnp.int32).reshape(1)
    sums = [_rs_pairsum(core, part, stage, part_view(n)[1], f"rs_pairsum_{n}{l}")
            for (n, l), part, stage in zip(items, parts, stages)]
    dsts = [(BIG.index(n), (lambda ref, q, l=l: ref.at[q, l])) for n, l in items]
    landed = _rs_chips(sums, dsts, [(N_CHIPS,) + W[n].shape for n in BIG], "rs_chips")
    for n, slots in zip(BIG, landed):
        shp = W[n].shape
        grads[n] = _sum_slots(slots.reshape(N_CHIPS, shp[0] * shp[1], shp[2]), f"rs_sum_{n}").reshape(shp)

    small_order = SMALL + ("conv_w",)
    small_shapes = {n: W[n].shape for n in SMALL}
    small_shapes["conv_w"] = (depth, 3, 2 * f)
    stacked = {"b_ada": jnp.stack(dada).reshape(depth, 6 * d)}
    for n in small_order[1:]:
        stacked[n] = jnp.stack([g.reshape(small_shapes[n][1:]) for g in grads[n]])
    spack = _pack([stacked[n] for n in small_order], F32, 1024)
    sg = _all_gather(spack, "ag_small")
    ssum = _sum_slots(sg, "sum_small").reshape(-1)
    for n, g in zip(small_order, _unpack(ssum, [small_shapes[n] for n in small_order])):
        grads[n] = g
    n_cw = conv_w.shape[2]
    grads["conv_w"] = lax.dynamic_slice_in_dim(grads["conv_w"], me * n_cw, n_cw, axis=2)
    dada_all = sg.reshape(N_DEV, -1)[:, :depth * 6 * d].reshape(N_DEV, depth, 6 * d)
    dada_shard = lax.dynamic_slice_in_dim(dada_all, me * n_ada, n_ada, axis=2)
    kp = LANES
    dada_pad = jnp.pad(jnp.transpose(dada_shard, (1, 0, 2)), ((0, 0), (0, kp - N_DEV), (0, 0)))
    act_t = jnp.pad(jnp.transpose(c_act[:N_DEV]), ((0, 0), (0, kp - N_DEV)))
    grads["w_ada"] = _ada_wgrad(act_t, dada_pad, "ada_wgrad")

    delta, new_m, new_v = {}, {}, {}
    for n in ("w_ada",) + BIG + ("conv_w",):
        delta[n], new_m[n], new_v[n] = _adamw_nd(W[n], grads[n], M[n], V[n], f"adamw_{n}")
    packs = [_pack([t[n] for n in SMALL], F32, 1024) for t in (W, grads, M, V)]
    outs = _adamw(*packs, "adamw_small")
    shapes = [W[n].shape for n in SMALL]
    for tgt_d, o in zip((delta, new_m, new_v), outs):
        for n, val in zip(SMALL, _unpack(o.reshape(-1), shapes)):
            tgt_d[n] = val

    return (loss, grad_x, *[grads[n] for n in ORDER], *[delta[n] for n in ORDER],
            *[new_m[n] for n in ORDER], *[new_v[n] for n in ORDER])
```

```python
import functools
import math

import jax
import jax.numpy as jnp
from jax import lax
from jax.experimental import pallas as pl
from jax.experimental.pallas import tpu as pltpu

F32 = jnp.float32
BF16 = jnp.bfloat16

N_DEV = 8
HEAD_DIM = 64
WINDOW = 128
SSM_GROUP = 16
STATE = 64
LANES = 128
GROUPS_PER_BLOCK = LANES // SSM_GROUP
BLOCK_STATES = GROUPS_PER_BLOCK * STATE
EPS = 1e-6
NEG = -1e30
ADAM_LR, ADAM_B1, ADAM_B2, ADAM_EPS, ADAM_WD, ADAM_STEP = 0.001, 0.9, 0.999, 1e-08, 0.01, 10
VMEM_BYTES_V7X = 64 * 1024 * 1024
GELU_C = math.sqrt(2.0 / math.pi)
MESH = pl.DeviceIdType.MESH
ANY = pl.BlockSpec(memory_space=pl.ANY)


def _pick(n, pref, align):
    t = (min(pref, n) // align) * align
    while t >= align:
        if n % t == 0:
            return t
        t -= align
    return n


def _params(vmem_bytes=None):
    if vmem_bytes is None:
        return pltpu.CompilerParams()
    return pltpu.CompilerParams(vmem_limit_bytes=int(min(vmem_bytes, VMEM_BYTES_V7X - (8 << 20))))


def _gelu(x):
    return 0.5 * x * (1.0 + jnp.tanh(GELU_C * (x + 0.044715 * x * x * x)))


def _gelu_grad(x):
    th = jnp.tanh(GELU_C * (x + 0.044715 * x * x * x))
    return 0.5 * (1.0 + th) + 0.5 * x * (1.0 - th * th) * GELU_C * (1.0 + 3.0 * 0.044715 * x * x)


def _rstd(x):
    return lax.rsqrt(jnp.mean(x * x, axis=-1, keepdims=True) + EPS)


def _norm_bwd(dhat, xhat, r):
    return r * (dhat - xhat * jnp.mean(dhat * xhat, axis=-1, keepdims=True))


def _matmul(a, b, *, ta=False, tb=False, out_dtype=F32, name):
    (kdim, m) = a.shape if ta else a.shape[::-1]
    (n, k2) = b.shape if tb else b.shape[::-1]
    assert kdim == k2, (a.shape, b.shape, ta, tb)
    tm, tn, tk = _pick(m, 1024, LANES), _pick(n, 1024, LANES), _pick(kdim, 2048, LANES)
    nk = kdim // tk
    dn = (((0 if ta else 1,), (1 if tb else 0,)), ((), ()))

    def partial_product(a_ref, b_ref):
        return lax.dot_general(a_ref[...].astype(BF16), b_ref[...].astype(BF16), dn, preferred_element_type=F32)

    def body_one(a_ref, b_ref, o_ref):
        o_ref[...] = partial_product(a_ref, b_ref).astype(o_ref.dtype)

    def body_acc(a_ref, b_ref, o_ref, acc_ref):
        k = pl.program_id(2)

        @pl.when(k == 0)
        def _():
            acc_ref[...] = partial_product(a_ref, b_ref)

        @pl.when((k > 0) & (k < nk - 1))
        def _():
            acc_ref[...] += partial_product(a_ref, b_ref)

        @pl.when(k == nk - 1)
        def _():
            o_ref[...] = (acc_ref[...] + partial_product(a_ref, b_ref)).astype(o_ref.dtype)

    body = body_one if nk == 1 else body_acc
    a_spec = pl.BlockSpec((tk, tm), lambda i, j, k: (k, i)) if ta else pl.BlockSpec((tm, tk), lambda i, j, k: (i, k))
    b_spec = pl.BlockSpec((tn, tk), lambda i, j, k: (j, k)) if tb else pl.BlockSpec((tk, tn), lambda i, j, k: (k, j))
    vmem = (2 * (tm * tk * a.dtype.itemsize + tk * tn * b.dtype.itemsize) + tm * tn * 4
            + 2 * tm * tn * jnp.dtype(out_dtype).itemsize + 3 * tm * tn * 4 + (4 << 20))
    return pl.pallas_call(
        body, name=name, grid=(m // tm, n // tn, nk),
        in_specs=[a_spec, b_spec], out_specs=pl.BlockSpec((tm, tn), lambda i, j, k: (i, j)),
        out_shape=jax.ShapeDtypeStruct((m, n), out_dtype),
        scratch_shapes=[] if nk == 1 else [pltpu.VMEM((tm, tn), F32)],
        compiler_params=_params(vmem),
    )(a, b)


def _all_gather(x, name):
    def body(x_ref, out_ref, send_sems, recv_sems, local_sem):
        x_, y_, c_ = lax.axis_index("x"), lax.axis_index("y"), lax.axis_index("c")
        me, sibling = (x_, y_, c_), (x_, y_, 1 - c_)
        chips = [(1 - x_, y_), (x_, 1 - y_), (1 - x_, 1 - y_)]

        def slot(px, py, pc):
            return out_ref.at[4 * px + 2 * py + pc]

        def copy(k, block, to, src=None):
            return pltpu.make_async_remote_copy(
                src_ref=slot(*block) if src is None else src, dst_ref=slot(*block),
                send_sem=send_sems.at[k], recv_sem=recv_sems.at[k], device_id=to, device_id_type=MESH)

        mine = pltpu.make_async_copy(x_ref, slot(*me), local_sem)
        mine.start()
        first = [copy(0, me, sibling, src=x_ref)]
        first += [copy(1 + j, me, (*chip, c_), src=x_ref) for j, chip in enumerate(chips)]
        for cp in first:
            cp.start()
        passed = [copy(4 + j, (*chip, c_), sibling) for j, chip in enumerate(chips)]
        for j, chip in enumerate(chips):
            copy(1 + j, (*chip, c_), me).wait_recv()
            passed[j].start()
        copy(0, sibling, me).wait_recv()
        for j, chip in enumerate(chips):
            copy(4 + j, (*chip, 1 - c_), me).wait_recv()
        for cp in first + passed:
            cp.wait_send()
        mine.wait()

    return pl.pallas_call(
        body, name=name, out_shape=jax.ShapeDtypeStruct((N_DEV,) + x.shape, x.dtype),
        in_specs=[ANY], out_specs=ANY,
        scratch_shapes=[pltpu.SemaphoreType.DMA((7,)), pltpu.SemaphoreType.DMA((7,)), pltpu.SemaphoreType.DMA],
    )(x)


def _gather_multi(srcs, out_shapes, views, name):
    n = len(srcs)

    def body(*refs):
        src_refs, out_refs = refs[:n], refs[n:2 * n]
        send_sems, recv_sems, local_sems = refs[2 * n:]
        x_, y_, c_ = lax.axis_index("x"), lax.axis_index("y"), lax.axis_index("c")
        me, sibling = (x_, y_, c_), (x_, y_, 1 - c_)
        chips = [(1 - x_, y_), (x_, 1 - y_), (1 - x_, 1 - y_)]

        def slot(i, px, py, pc):
            return views[i](out_refs[i], 4 * px + 2 * py + pc)

        def copy(i, k, block, to, from_src=False):
            return pltpu.make_async_remote_copy(
                src_ref=src_refs[i] if from_src else slot(i, *block), dst_ref=slot(i, *block),
                send_sem=send_sems.at[7 * i + k], recv_sem=recv_sems.at[7 * i + k], device_id=to, device_id_type=MESH)

        mine = [pltpu.make_async_copy(src_refs[i], slot(i, *me), local_sems.at[i]) for i in range(n)]
        for cp in mine:
            cp.start()
        first = []
        for i in range(n):
            first.append(copy(i, 0, me, sibling, True))
            first += [copy(i, 1 + j, me, (*chip, c_), True) for j, chip in enumerate(chips)]
        for cp in first:
            cp.start()
        passed = []
        for j, chip in enumerate(chips):
            for i in range(n):
                copy(i, 1 + j, (*chip, c_), me).wait_recv()
                fwd = copy(i, 4 + j, (*chip, c_), sibling)
                fwd.start()
                passed.append(fwd)
        for i in range(n):
            copy(i, 0, sibling, me).wait_recv()
            for j, chip in enumerate(chips):
                copy(i, 4 + j, (*chip, 1 - c_), me).wait_recv()
        for cp in first + passed:
            cp.wait_send()
        for cp in mine:
            cp.wait()

    return pl.pallas_call(
        body, name=name, out_shape=[jax.ShapeDtypeStruct(s, a.dtype) for s, a in zip(out_shapes, srcs)],
        in_specs=[ANY] * n, out_specs=[ANY] * n,
        scratch_shapes=[pltpu.SemaphoreType.DMA((7 * n,)), pltpu.SemaphoreType.DMA((7 * n,)),
                        pltpu.SemaphoreType.DMA((n,))],
    )(*srcs)


N_CHIPS = 4


def _rs_sibling(parts, views, block_shapes, name):
    n = len(parts)

    def body(*refs):
        part_refs, stage_refs = refs[:n], refs[n:2 * n]
        send_sems, recv_sems = refs[2 * n:]
        x_, y_, c_ = lax.axis_index("x"), lax.axis_index("y"), lax.axis_index("c")
        sibling = (x_, y_, 1 - c_)
        copies = []
        for i in range(n):
            for q in range(N_CHIPS):
                cp = pltpu.make_async_remote_copy(
                    src_ref=views[i](part_refs[i], 2 * q + (1 - c_)), dst_ref=stage_refs[i].at[q],
                    send_sem=send_sems.at[N_CHIPS * i + q], recv_sem=recv_sems.at[N_CHIPS * i + q],
                    device_id=sibling, device_id_type=MESH)
                cp.start()
                copies.append(cp)
        for cp in copies:
            cp.wait_recv()
        for cp in copies:
            cp.wait_send()

    return pl.pallas_call(
        body, name=name,
        out_shape=[jax.ShapeDtypeStruct((N_CHIPS,) + tuple(b), p.dtype) for b, p in zip(block_shapes, parts)],
        in_specs=[ANY] * n, out_specs=[ANY] * n,
        scratch_shapes=[pltpu.SemaphoreType.DMA((N_CHIPS * n,)), pltpu.SemaphoreType.DMA((N_CHIPS * n,))],
    )(*parts)


def _rs_pairsum(core, part, stage, kind, name):
    _, r, c = stage.shape
    tr = _pick(r, 512, 16)
    nt = r // tr
    if kind == "rows":
        part_spec = pl.BlockSpec((tr, c), lambda q, i, cr: ((2 * q + cr[0]) * nt + i, 0))
    elif kind == "cols":
        part_spec = pl.BlockSpec((tr, c), lambda q, i, cr: (i, 2 * q + cr[0]))
    else:
        part_spec = pl.BlockSpec((None, tr, c), lambda q, i, cr: (2 * q + cr[0], i, 0))

    def body(core_ref, p_ref, s_ref, o_ref):
        o_ref[...] = (p_ref[...].astype(F32) + s_ref[...].astype(F32)).astype(o_ref.dtype)

    blk = pl.BlockSpec((None, tr, c), lambda q, i, cr: (q, i, 0))
    return pl.pallas_call(
        body, name=name,
        grid_spec=pltpu.PrefetchScalarGridSpec(num_scalar_prefetch=1, grid=(N_CHIPS, nt),
                                               in_specs=[part_spec, blk], out_specs=blk),
        out_shape=jax.ShapeDtypeStruct(stage.shape, BF16),
    )(core, part, stage)


def _rs_chips(sums, dsts, out_shapes, name):
    n, n_out = len(sums), len(out_shapes)

    def body(*refs):
        sum_refs, out_refs = refs[:n], refs[n:n + n_out]
        send_sems, recv_sems, local_sems = refs[n + n_out:]
        x_, y_, c_ = lax.axis_index("x"), lax.axis_index("y"), lax.axis_index("c")
        q_me = 2 * x_ + y_
        local, copies = [], []
        for i in range(n):
            o, view = dsts[i]
            cp = pltpu.make_async_copy(sum_refs[i].at[q_me], view(out_refs[o], q_me), local_sems.at[i])
            cp.start()
            local.append(cp)
            for k in range(1, N_CHIPS):
                px, py = (1 - x_ if (k >> 1) & 1 else x_), (1 - y_ if k & 1 else y_)
                q_peer = 2 * px + py
                sem = (N_CHIPS - 1) * i + k - 1
                send = pltpu.make_async_remote_copy(
                    src_ref=sum_refs[i].at[q_peer], dst_ref=view(out_refs[o], q_me), send_sem=send_sems.at[sem],
                    recv_sem=recv_sems.at[sem], device_id=(px, py, c_), device_id_type=MESH)
                recv = pltpu.make_async_remote_copy(
                    src_ref=sum_refs[i].at[q_peer], dst_ref=view(out_refs[o], q_peer), send_sem=send_sems.at[sem],
                    recv_sem=recv_sems.at[sem], device_id=(px, py, c_), device_id_type=MESH)
                send.start()
                copies.append((send, recv))
        for send, recv in copies:
            recv.wait_recv()
        for send, recv in copies:
            send.wait_send()
        for cp in local:
            cp.wait()

    m = (N_CHIPS - 1) * n
    return pl.pallas_call(
        body, name=name, out_shape=[jax.ShapeDtypeStruct(s, BF16) for s in out_shapes],
        in_specs=[ANY] * n, out_specs=[ANY] * n_out,
        scratch_shapes=[pltpu.SemaphoreType.DMA((m,)), pltpu.SemaphoreType.DMA((m,)), pltpu.SemaphoreType.DMA((n,))],
    )(*sums)


HBM_SPEC = pl.BlockSpec(memory_space=pltpu.HBM)
SEM_SPEC = pl.BlockSpec(memory_space=pltpu.SEMAPHORE)
SIDE_EFFECT = pltpu.SideEffectType.DATAFLOW_SIDE_EFFECTING
N_PEERS = N_DEV - 1


def _peer(k, x_, y_, c_):
    px = 1 - x_ if (k >> 2) & 1 else x_
    py = 1 - y_ if (k >> 1) & 1 else y_
    pc = 1 - c_ if k & 1 else c_
    return (px, py, pc), 4 * px + 2 * py + pc


def _exchange_copies(src_refs, land_refs, send_sems, recv_sems, src_views, dst_views):
    x_, y_, c_ = lax.axis_index("x"), lax.axis_index("y"), lax.axis_index("c")
    me = 4 * x_ + 2 * y_ + c_
    out = []
    for i in range(len(src_refs)):
        for k in range(1, N_DEV):
            peer, idx = _peer(k, x_, y_, c_)

            def copy(dst_slot, i=i, k=k, peer=peer, idx=idx):
                return pltpu.make_async_remote_copy(
                    src_ref=src_views[i](src_refs[i], idx), dst_ref=dst_views[i](land_refs[i], dst_slot),
                    send_sem=send_sems[i].at[k - 1], recv_sem=recv_sems[i].at[k - 1], device_id=peer,
                    device_id_type=MESH)

            out.append((copy(me), copy(idx)))
    return out


def _exchange_start(srcs, lands, src_views, dst_views, name):
    n = len(srcs)

    def body(*refs):
        src_refs, land_refs = refs[:n], refs[n:2 * n]
        send_sems, recv_sems = refs[2 * n:3 * n], refs[3 * n:4 * n]
        token = refs[-1]
        for send, _ in _exchange_copies(src_refs, land_refs, send_sems, recv_sems, src_views, dst_views):
            send.start()
        token[...] = jnp.zeros_like(token)

    sems = [pltpu.SemaphoreType.DMA((N_PEERS,))] * n
    thru = [pltpu.HBM(a.shape, a.dtype) for a in list(srcs) + list(lands)]
    outs = pl.pallas_call(
        body, name=name, out_shape=sems + sems + thru + [jax.ShapeDtypeStruct((8, LANES), F32)],
        in_specs=[HBM_SPEC] * (2 * n),
        out_specs=[SEM_SPEC] * (2 * n) + [HBM_SPEC] * (2 * n) + [pl.BlockSpec(memory_space=pltpu.VMEM)],
        input_output_aliases={j: 2 * n + j for j in range(2 * n)},
        compiler_params=pltpu.CompilerParams(has_side_effects=SIDE_EFFECT),
    )(*[pltpu.with_memory_space_constraint(a, pltpu.HBM) for a in list(srcs) + list(lands)])
    per_array = [(outs[j], outs[n + j], outs[2 * n + j], outs[3 * n + j]) for j in range(n)]
    return per_array, outs[-1]


def _exchange_wait(started, after, src_views, dst_views, name):
    send_sems, recv_sems, srcs, lands = (list(t) for t in zip(*started))
    n = len(srcs)

    def body(*refs):
        src_refs, land_refs = refs[:n], refs[n:2 * n]
        send_refs, recv_refs = refs[2 * n:3 * n], refs[3 * n:4 * n]
        copies = _exchange_copies(src_refs, land_refs, send_refs, recv_refs, src_views, dst_views)
        for send, _ in copies:
            send.wait_send()
        for _, recv in copies:
            recv.wait_recv()

    thru = [pltpu.HBM(a.shape, a.dtype) for a in list(srcs) + list(lands)]
    outs = pl.pallas_call(
        body, name=name, out_shape=thru,
        in_specs=[HBM_SPEC] * (2 * n) + [SEM_SPEC] * (2 * n) + [ANY],
        out_specs=[HBM_SPEC] * (2 * n),
        input_output_aliases={j: j for j in range(2 * n)},
        compiler_params=pltpu.CompilerParams(has_side_effects=SIDE_EFFECT),
    )(*srcs, *lands, *send_sems, *recv_sems, after)
    return outs[:n], outs[n:]


def _place_own(srcs, lands, dst_views, name):
    n = len(srcs)

    def body(*refs):
        src_refs, land_refs, sems = refs[:n], refs[n:2 * n], refs[-1]
        me = 4 * lax.axis_index("x") + 2 * lax.axis_index("y") + lax.axis_index("c")
        copies = [pltpu.make_async_copy(src_refs[i], dst_views[i](land_refs[i], me), sems.at[i]) for i in range(n)]
        for cp in copies:
            cp.start()
        for cp in copies:
            cp.wait()

    return pl.pallas_call(
        body, name=name, out_shape=[jax.ShapeDtypeStruct(a.shape, a.dtype) for a in lands],
        in_specs=[ANY] * (2 * n), out_specs=[ANY] * n,
        input_output_aliases={n + j: j for j in range(n)},
        scratch_shapes=[pltpu.SemaphoreType.DMA((n,))],
    )(*srcs, *lands)


def _sum_slots_own(me, landed, part, kind, name):
    _, r, c = landed.shape
    tr = _pick(r, 512, 16)
    nt = r // tr
    if kind == "rows":
        part_spec = pl.BlockSpec((tr, c), lambda i, mr: (mr[0] * nt + i, 0))
    elif kind == "cols":
        part_spec = pl.BlockSpec((tr, c), lambda i, mr: (i, mr[0]))
    else:
        part_spec = pl.BlockSpec((None, tr, c), lambda i, mr: (mr[0], i, 0))

    def body(me_ref, x_ref, p_ref, o_ref):
        own = p_ref[...].astype(F32)
        acc = jnp.zeros_like(own)
        for i in range(N_DEV):
            acc = acc + jnp.where(me_ref[0] == i, own, x_ref[i].astype(F32))
        o_ref[...] = acc

    return pl.pallas_call(
        body, name=name,
        grid_spec=pltpu.PrefetchScalarGridSpec(
            num_scalar_prefetch=1, grid=(nt,),
            in_specs=[pl.BlockSpec((N_DEV, tr, c), lambda i, mr: (0, i, 0)), part_spec],
            out_specs=pl.BlockSpec((tr, c), lambda i, mr: (i, 0))),
        out_shape=jax.ShapeDtypeStruct((r, c), F32),
        compiler_params=_params(2 * N_DEV * tr * c * landed.dtype.itemsize + 8 * tr * c * 4 + (4 << 20)),
    )(me, landed, part)


def _sum_slots(x, name):
    ns, r, c = x.shape
    tr = _pick(r, 512, 16)

    def body(x_ref, o_ref):
        acc = x_ref[0].astype(F32)
        for i in range(1, ns):
            acc = acc + x_ref[i].astype(F32)
        o_ref[...] = acc

    return pl.pallas_call(
        body, name=name, grid=(r // tr,),
        in_specs=[pl.BlockSpec((ns, tr, c), lambda i: (0, i, 0))],
        out_specs=pl.BlockSpec((tr, c), lambda i: (i, 0)),
        out_shape=jax.ShapeDtypeStruct((r, c), F32),
        compiler_params=_params(2 * ns * tr * c * x.dtype.itemsize + 4 * tr * c * 4 + (4 << 20)),
    )(x)


def _cols_from_blocks(blk, name):
    nd, nl, k, n = blk.shape
    tk = _pick(k, 256, 16)

    def body(b_ref, o_ref, wide_ref):
        for dev in range(nd):
            wide_ref[:, dev * n:(dev + 1) * n] = b_ref[dev].astype(F32)
        o_ref[...] = wide_ref[...].astype(o_ref.dtype)

    return pl.pallas_call(
        body, name=name, grid=(nl, k // tk),
        in_specs=[pl.BlockSpec((nd, None, tk, n), lambda l, i: (0, l, i, 0))],
        out_specs=pl.BlockSpec((None, tk, nd * n), lambda l, i: (l, i, 0)),
        out_shape=jax.ShapeDtypeStruct((nl, k, nd * n), BF16),
        scratch_shapes=[pltpu.VMEM((tk, nd * n), F32)],
    )(blk)


def _blocks_from_cols(full, name):
    k, n8 = full.shape
    n = n8 // N_DEV
    tk = _pick(k, 256, 16)

    def body(f_ref, o_ref):
        for dev in range(N_DEV):
            o_ref[dev] = f_ref[:, dev * n:(dev + 1) * n].astype(o_ref.dtype)

    return pl.pallas_call(
        body, name=name, grid=(k // tk,),
        in_specs=[pl.BlockSpec((tk, n8), lambda i: (i, 0))],
        out_specs=pl.BlockSpec((N_DEV, tk, n), lambda i: (0, i, 0)),
        out_shape=jax.ShapeDtypeStruct((N_DEV, k, n), BF16),
    )(full)


def _pack(arrs, dtype, cols):
    flat = jnp.concatenate([a.astype(dtype).reshape(-1) for a in arrs])
    unit = 16 * cols
    pad = (-flat.shape[0]) % unit
    flat = jnp.pad(flat, (0, pad))
    return flat.reshape(-1, cols)


def _unpack(flat, shapes):
    out, off = [], 0
    for s in shapes:
        n = math.prod(s)
        out.append(flat[off:off + n].reshape(s))
        off += n
    return out


def _ada_fwd(c_all, w_ada, b_shard, name):
    nl, d, n = w_ada.shape
    tn = _pick(n, 512, LANES)

    def body(c_ref, w_ref, b_ref, o_ref, act_ref):
        cv = c_ref[...]
        act = cv * jax.nn.sigmoid(cv)
        act_ref[...] = act
        o_ref[...] = jnp.dot(act.astype(BF16), w_ref[...].astype(BF16), preferred_element_type=F32) + b_ref[...]

    return pl.pallas_call(
        body, name=name, grid=(nl, n // tn),
        in_specs=[pl.BlockSpec(c_all.shape, lambda l, j: (0, 0)),
                  pl.BlockSpec((None, d, tn), lambda l, j: (l, 0, j)),
                  pl.BlockSpec((None, 1, tn), lambda l, j: (l, 0, j))],
        out_specs=[pl.BlockSpec((None, c_all.shape[0], tn), lambda l, j: (l, 0, j)),
                   pl.BlockSpec(c_all.shape, lambda l, j: (0, 0))],
        out_shape=[jax.ShapeDtypeStruct((nl, c_all.shape[0], n), F32), jax.ShapeDtypeStruct(c_all.shape, F32)],
        compiler_params=_params(2 * d * tn * 4 + d * tn * 2 + (8 << 20)),
    )(c_all, w_ada, b_shard)


def _ada_wgrad(act_t, dada, name):
    d, kp = act_t.shape
    nl, _, n = dada.shape
    tm = _pick(d, 512, 8)

    def body(a_ref, g_ref, o_ref):
        o_ref[...] = jnp.dot(a_ref[...].astype(BF16), g_ref[...].astype(BF16), preferred_element_type=F32)

    return pl.pallas_call(
        body, name=name, grid=(nl, d // tm),
        in_specs=[pl.BlockSpec((tm, kp), lambda l, i: (i, 0)), pl.BlockSpec((None, kp, n), lambda l, i: (l, 0, 0))],
        out_specs=pl.BlockSpec((None, tm, n), lambda l, i: (l, i, 0)),
        out_shape=jax.ShapeDtypeStruct((nl, d, n), F32),
        compiler_params=_params(4 * tm * n * 4 + 2 * kp * n * 4 + (8 << 20)),
    )(act_t, dada)


def _row_spec(tm, d):
    return pl.BlockSpec((tm, d), lambda i: (i, 0))


def _vec_spec(d):
    return pl.BlockSpec((1, d), lambda i: (0, 0))


def _modnorm_fwd(x, g, sc, sh, name):
    s, d = x.shape
    tm = _pick(s, 256, 16)

    def body(x_ref, g_ref, sc_ref, sh_ref, o_ref):
        xv = x_ref[...]
        o_ref[...] = ((xv * _rstd(xv)) * g_ref[...] * (1.0 + sc_ref[...]) + sh_ref[...]).astype(o_ref.dtype)

    return pl.pallas_call(
        body, name=name, grid=(s // tm,),
        in_specs=[_row_spec(tm, d), _vec_spec(d), _vec_spec(d), _vec_spec(d)], out_specs=_row_spec(tm, d),
        out_shape=jax.ShapeDtypeStruct((s, d), BF16),
    )(x, g, sc, sh)


def _modnorm_bwd(dh, x, g, sc, dres, name):
    s, d = x.shape
    tm = _pick(s, 256, 8)

    def body(dh_ref, x_ref, g_ref, sc_ref, dres_ref, dx_ref, dg_ref, dsc_ref, dsh_ref):
        @pl.when(pl.program_id(0) == 0)
        def _():
            dg_ref[...] = jnp.zeros_like(dg_ref)
            dsc_ref[...] = jnp.zeros_like(dsc_ref)
            dsh_ref[...] = jnp.zeros_like(dsh_ref)

        dh_, xv, gv = dh_ref[...], x_ref[...], g_ref[...]
        r = _rstd(xv)
        xhat = xv * r
        dn = dh_ * (1.0 + sc_ref[...])
        dsh_ref[...] += jnp.sum(dh_, axis=0, keepdims=True)
        dsc_ref[...] += jnp.sum(dh_ * (xhat * gv), axis=0, keepdims=True)
        dg_ref[...] += jnp.sum(dn * xhat, axis=0, keepdims=True)
        dx_ref[...] = _norm_bwd(dn * gv, xhat, r) + dres_ref[...]

    vec = jax.ShapeDtypeStruct((1, d), F32)
    return pl.pallas_call(
        body, name=name, grid=(s // tm,),
        in_specs=[_row_spec(tm, d), _row_spec(tm, d), _vec_spec(d), _vec_spec(d), _row_spec(tm, d)],
        out_specs=[_row_spec(tm, d), _vec_spec(d), _vec_spec(d), _vec_spec(d)],
        out_shape=[jax.ShapeDtypeStruct((s, d), F32), vec, vec, vec],
    )(dh, x, g, sc, dres)


def _resnorm_fwd(x, y, g, gt, name):
    s, d = x.shape
    tm = _pick(s, 256, 8)

    def body(x_ref, y_ref, g_ref, gt_ref, o_ref):
        yv = y_ref[...]
        o_ref[...] = x_ref[...] + (1.0 + gt_ref[...]) * ((yv * _rstd(yv)) * g_ref[...])

    return pl.pallas_call(
        body, name=name, grid=(s // tm,),
        in_specs=[_row_spec(tm, d), _row_spec(tm, d), _vec_spec(d), _vec_spec(d)], out_specs=_row_spec(tm, d),
        out_shape=jax.ShapeDtypeStruct((s, d), F32),
    )(x, y, g, gt)


def _resnorm_bwd(dxo, y, g, gt, name):
    s, d = y.shape
    tm = _pick(s, 256, 16)

    def body(dxo_ref, y_ref, g_ref, gt_ref, dy_ref, dg_ref, dgt_ref):
        @pl.when(pl.program_id(0) == 0)
        def _():
            dg_ref[...] = jnp.zeros_like(dg_ref)
            dgt_ref[...] = jnp.zeros_like(dgt_ref)

        dxo_, yv, gv = dxo_ref[...], y_ref[...], g_ref[...]
        r = _rstd(yv)
        yhat = yv * r
        dn = dxo_ * (1.0 + gt_ref[...])
        dgt_ref[...] += jnp.sum(dxo_ * (yhat * gv), axis=0, keepdims=True)
        dg_ref[...] += jnp.sum(dn * yhat, axis=0, keepdims=True)
        dy_ref[...] = _norm_bwd(dn * gv, yhat, r).astype(dy_ref.dtype)

    vec = jax.ShapeDtypeStruct((1, d), F32)
    return pl.pallas_call(
        body, name=name, grid=(s // tm,),
        in_specs=[_row_spec(tm, d), _row_spec(tm, d), _vec_spec(d), _vec_spec(d)],
        out_specs=[_row_spec(tm, d), _vec_spec(d), _vec_spec(d)],
        out_shape=[jax.ShapeDtypeStruct((s, d), BF16), vec, vec],
    )(dxo, y, g, gt)


def _loss_bwd(xf, tgt, name):
    s, d = xf.shape
    tm = _pick(s, 256, 8)

    def body(x_ref, t_ref, dy_ref, l_ref):
        @pl.when(pl.program_id(0) == 0)
        def _():
            l_ref[...] = jnp.zeros_like(l_ref)

        e = x_ref[...] - t_ref[...]
        dy_ref[...] = e * (1.0 / d)
        l_ref[...] += jnp.sum(e * e) * (0.5 / d)

    return pl.pallas_call(
        body, name=name, grid=(s // tm,),
        in_specs=[_row_spec(tm, d), _row_spec(tm, d)],
        out_specs=[_row_spec(tm, d), pl.BlockSpec((8, LANES), lambda i: (0, 0))],
        out_shape=[jax.ShapeDtypeStruct((s, d), F32), jax.ShapeDtypeStruct((8, LANES), F32)],
    )(xf, tgt)


def _attn_specs(n_q, n_kv):
    aw, kvd = n_q * HEAD_DIM, n_kv * HEAD_DIM
    assert aw % kvd == 0
    kcol = aw // kvd
    q = pl.BlockSpec((WINDOW, aw), lambda n: (n, 0))
    kc = pl.BlockSpec((WINDOW, kvd), lambda n: (n, kcol))
    kp = pl.BlockSpec((WINDOW, kvd), lambda n: (jnp.maximum(n - 1, 0), kcol))
    vc = pl.BlockSpec((WINDOW, kvd), lambda n: (n, kcol + 1))
    vp = pl.BlockSpec((WINDOW, kvd), lambda n: (jnp.maximum(n - 1, 0), kcol + 1))
    return [q, kc, kp, vc, vp]


def _band_mask(n, n_heads):
    qi = lax.broadcasted_iota(jnp.int32, (n_heads * WINDOW, 2 * WINDOW), 0) & (WINDOW - 1)
    kj = lax.broadcasted_iota(jnp.int32, (n_heads * WINDOW, 2 * WINDOW), 1)
    return (kj > qi) & (kj <= qi + WINDOW) & ((kj >= WINDOW) | (n > 0))


def _stack_heads(ref, heads):
    return jnp.concatenate([ref[:, h * HEAD_DIM:(h + 1) * HEAD_DIM] for h in heads], axis=0)


def _stack_sinks(ref, heads):
    return jnp.concatenate([jnp.broadcast_to(ref[:, h:h + 1], (WINDOW, 1)) for h in heads], axis=0)


_NT = (((1,), (1,)), ((), ()))
_TN = (((0,), (0,)), ((), ()))


def _attn_fwd(proj, sinks, *, n_q, n_kv, name):
    s = proj.shape[0]
    aw, grp = n_q * HEAD_DIM, n_q // n_kv

    def body(q_ref, kc_ref, kp_ref, vc_ref, vp_ref, sink_ref, o_ref, lse_ref):
        valid = _band_mask(pl.program_id(0), grp)
        kb = jnp.concatenate([kp_ref[...], kc_ref[...]], axis=0).astype(BF16)
        vb = jnp.concatenate([vp_ref[...], vc_ref[...]], axis=0).astype(BF16)
        lse_ref[...] = jnp.zeros_like(lse_ref)
        for g in range(n_kv):
            heads = range(g * grp, (g + 1) * grp)
            gs = slice(g * HEAD_DIM, (g + 1) * HEAD_DIM)
            qg = _stack_heads(q_ref, heads).astype(BF16)
            sink = _stack_sinks(sink_ref, heads)
            sc = lax.dot_general(qg, kb[:, gs], _NT, preferred_element_type=F32)
            sc = jnp.where(valid, sc * (HEAD_DIM ** -0.5), NEG)
            m = jnp.maximum(jnp.max(sc, axis=-1, keepdims=True), sink)
            e = jnp.exp(sc - m)
            den = jnp.sum(e, axis=-1, keepdims=True) + jnp.exp(sink - m)
            p = e * (1.0 / den)
            og = jnp.dot(p.astype(BF16), vb[:, gs], preferred_element_type=F32)
            lse = m + jnp.log(den)
            for i, h in enumerate(heads):
                rows = slice(i * WINDOW, (i + 1) * WINDOW)
                o_ref[:, h * HEAD_DIM:(h + 1) * HEAD_DIM] = og[rows]
                lse_ref[:, h:h + 1] = lse[rows]

    return pl.pallas_call(
        body, name=name, grid=(s // WINDOW,),
        in_specs=_attn_specs(n_q, n_kv) + [pl.BlockSpec((1, LANES), lambda n: (0, 0))],
        out_specs=[pl.BlockSpec((WINDOW, aw), lambda n: (n, 0)), pl.BlockSpec((WINDOW, LANES), lambda n: (n, 0))],
        out_shape=[jax.ShapeDtypeStruct((s, aw), F32), jax.ShapeDtypeStruct((s, LANES), F32)],
    )(proj, proj, proj, proj, proj, sinks)


def _attn_bwd(proj, sinks, out, lse, dout, *, n_q, n_kv, name):
    s = proj.shape[0]
    aw, kvd, grp = n_q * HEAD_DIM, n_kv * HEAD_DIM, n_q // n_kv
    scale = HEAD_DIM ** -0.5

    def body(q_ref, kc_ref, kp_ref, vc_ref, vp_ref, sink_ref, o_ref, lse_ref, do_ref,
             dq_ref, dk_ref, dv_ref, dsink_ref):
        n = pl.program_id(0)

        @pl.when(n == 0)
        def _():
            dk_ref[...] = jnp.zeros_like(dk_ref)
            dv_ref[...] = jnp.zeros_like(dv_ref)
            dsink_ref[...] = jnp.zeros_like(dsink_ref)

        valid = _band_mask(n, grp)
        kb = jnp.concatenate([kp_ref[...], kc_ref[...]], axis=0).astype(BF16)
        vb = jnp.concatenate([vp_ref[...], vc_ref[...]], axis=0).astype(BF16)
        lane = lax.broadcasted_iota(jnp.int32, (8, LANES), 1)
        dsink = jnp.zeros((8, LANES), F32)
        cur = pl.ds(pl.multiple_of(n * WINDOW, WINDOW), WINDOW)
        prev = pl.ds(pl.multiple_of(jnp.maximum(n - 1, 0) * WINDOW, WINDOW), WINDOW)
        for g in range(n_kv):
            heads = range(g * grp, (g + 1) * grp)
            gs = slice(g * HEAD_DIM, (g + 1) * HEAD_DIM)
            qg = _stack_heads(q_ref, heads).astype(BF16)
            do = _stack_heads(do_ref, heads)
            dob = do.astype(BF16)
            lse = jnp.concatenate([lse_ref[:, h:h + 1] for h in heads], axis=0)
            sc = lax.dot_general(qg, kb[:, gs], _NT, preferred_element_type=F32)
            sc = jnp.where(valid, sc * scale, NEG)
            p = jnp.exp(sc - lse)
            delta = jnp.sum(do * _stack_heads(o_ref, heads), axis=-1, keepdims=True)
            dp = lax.dot_general(dob, vb[:, gs], _NT, preferred_element_type=F32)
            ds = (p * (dp - delta) * scale).astype(BF16)
            dqg = jnp.dot(ds, kb[:, gs], preferred_element_type=F32)
            dkb = lax.dot_general(ds, qg, _TN, preferred_element_type=F32)
            dvb = lax.dot_general(p.astype(BF16), dob, _TN, preferred_element_type=F32)
            sink_term = jnp.exp(_stack_sinks(sink_ref, heads) - lse) * delta
            for i, h in enumerate(heads):
                rows = slice(i * WINDOW, (i + 1) * WINDOW)
                dq_ref[:, h * HEAD_DIM:(h + 1) * HEAD_DIM] = dqg[rows]
                dsink = dsink + jnp.where(lane == h, -jnp.sum(sink_term[rows]), 0.0)
            dk_ref[cur, gs] += dkb[WINDOW:]
            dv_ref[cur, gs] += dvb[WINDOW:]

            @pl.when(n > 0)
            def _():
                dk_ref[prev, gs] += dkb[:WINDOW]
                dv_ref[prev, gs] += dvb[:WINDOW]

        dsink_ref[...] += dsink

    blk = pl.BlockSpec((WINDOW, aw), lambda n: (n, 0))
    kv_full = pl.BlockSpec((s, kvd), lambda n: (0, 0))
    return pl.pallas_call(
        body, name=name, grid=(s // WINDOW,),
        in_specs=_attn_specs(n_q, n_kv) + [pl.BlockSpec((1, LANES), lambda n: (0, 0)), blk,
                                           pl.BlockSpec((WINDOW, LANES), lambda n: (n, 0)), blk],
        out_specs=[blk, kv_full, kv_full, pl.BlockSpec((8, LANES), lambda n: (0, 0))],
        out_shape=[jax.ShapeDtypeStruct((s, aw), F32), jax.ShapeDtypeStruct((s, kvd), F32),
                   jax.ShapeDtypeStruct((s, kvd), F32), jax.ShapeDtypeStruct((8, LANES), F32)],
    )(proj, proj, proj, proj, proj, sinks, out, lse, dout)


def _disc(lr, li, ls):
    dt = jnp.exp(ls)
    mag = jnp.exp(lr * dt)
    ang = li * dt
    ab_re, ab_im = mag * jnp.cos(ang), mag * jnp.sin(ang)
    den = lr * lr + li * li
    f_re = ((ab_re - 1.0) * lr + ab_im * li) / den
    f_im = (ab_im * lr - (ab_re - 1.0) * li) / den
    return ab_re, ab_im, f_re, f_im


POW_ROWS = 8
SUB = 8
TAB_ROWS = POW_ROWS + 2 * SUB


def _ssm_params_fwd(lr, li, ls, b_re, b_im, name):
    gp = lr.shape[1]
    h = b_re.shape[0]

    def body(lr_ref, li_ref, ls_ref, br_ref, bi_ref, bbr_ref, bbi_ref, tr_ref, ti_ref):
        ab_re, ab_im, f_re, f_im = _disc(lr_ref[...], li_ref[...], ls_ref[...])
        br, bi = br_ref[...], bi_ref[...]
        bbr_ref[...] = f_re * br - f_im * bi
        bbi_ref[...] = f_re * bi + f_im * br
        pr, pi = ab_re, ab_im
        for i in range(POW_ROWS):
            tr_ref[i:i + 1, :] = pr
            ti_ref[i:i + 1, :] = pi
            pr, pi = pr * pr - pi * pi, 2.0 * pr * pi
        pr, pi = ab_re, ab_im
        for r in range(SUB):
            for row in (POW_ROWS + r, POW_ROWS + 2 * SUB - 1 - r):
                tr_ref[row:row + 1, :] = pr
                ti_ref[row:row + 1, :] = pi
            pr, pi = pr * ab_re - pi * ab_im, pr * ab_im + pi * ab_re

    mat, tab = jax.ShapeDtypeStruct((h, gp), F32), jax.ShapeDtypeStruct((TAB_ROWS, gp), F32)
    return pl.pallas_call(body, name=name, out_shape=[mat, mat, tab, tab])(lr, li, ls, b_re, b_im)


def _ssm_params_bwd(lr, li, ls, b_re, b_im, dab_re, dab_im, dbb_re, dbb_im, seg, name):
    gp = lr.shape[1]
    h = b_re.shape[0]

    def body(lr_ref, li_ref, ls_ref, br_ref, bi_ref, dar_ref, dai_ref, dbbr_ref, dbbi_ref, seg_ref,
             dlr_ref, dli_ref, dls_ref, dbr_ref, dbi_ref):
        lr_, li_, ls_ = lr_ref[...], li_ref[...], ls_ref[...]
        (ab_re, ab_im, f_re, f_im), vjp = jax.vjp(_disc, lr_, li_, ls_)
        br, bi, dbbr, dbbi = br_ref[...], bi_ref[...], dbbr_ref[...], dbbi_ref[...]
        dbr_ref[...] = dbbr * f_re + dbbi * f_im
        dbi_ref[...] = dbbi * f_re - dbbr * f_im
        df_re = jnp.sum(dbbr * br + dbbi * bi, axis=0, keepdims=True)
        df_im = jnp.sum(dbbi * br - dbbr * bi, axis=0, keepdims=True)
        dlr, dli, dls = vjp((dar_ref[...], dai_ref[...], df_re, df_im))
        dlr_ref[...] = dlr
        dli_ref[...] = dli
        dls8 = jnp.broadcast_to(dls, (8, gp))
        dls_ref[...] = jnp.dot(dls8, seg_ref[...], preferred_element_type=F32, precision=lax.Precision.HIGHEST)

    vec, mat = jax.ShapeDtypeStruct((1, gp), F32), jax.ShapeDtypeStruct((h, gp), F32)
    return pl.pallas_call(body, name=name,
                          out_shape=[vec, vec, jax.ShapeDtypeStruct((8, seg.shape[1]), F32), mat, mat],
                          compiler_params=_params(24 << 20))(
        lr, li, ls, b_re, b_im, dab_re, dab_im, dbb_re, dbb_im, seg)


def _scan_bufs(t_len):
    hs = BLOCK_STATES
    return [pltpu.VMEM((hs // LANES, t_len, LANES), F32), pltpu.VMEM((hs // LANES, t_len, LANES), F32),
            pltpu.VMEM((t_len // SUB, hs), F32), pltpu.VMEM((t_len // SUB, hs), F32)]


def _scan(xr, xi, apow_ref, bufs, t_len, reverse):
    hs = BLOCK_STATES
    n_tiles = t_len // SUB
    sr_ref, si_ref, er_ref, ei_ref = bufs

    def doubling(xr, xi, n_rows, first_pow, within):
        row = lax.broadcasted_iota(jnp.int32, xr.shape, 0) & (within - 1)
        d = 1
        while d < within:
            i = first_pow + d.bit_length() - 1
            pr, pi = apow_ref[i:i + 1, :hs], apow_ref[i:i + 1, hs:]
            if reverse:
                pi, shift, keep = -pi, n_rows - d, row < within - d
            else:
                shift, keep = d, row >= d
            sr = jnp.where(keep, pltpu.roll(xr, shift, 0), 0.0)
            si = jnp.where(keep, pltpu.roll(xi, shift, 0), 0.0)
            xr, xi = xr + pr * sr - pi * si, xi + pr * si + pi * sr
            d *= 2
        return xr, xi

    shape3 = (n_tiles, SUB, hs)
    row = lax.broadcasted_iota(jnp.int32, shape3, 1)
    xr, xi = xr.reshape(shape3), xi.reshape(shape3)
    for i, d in enumerate((1, 2, 4)):
        pr, pi = apow_ref[i:i + 1, :hs], apow_ref[i:i + 1, hs:]
        if reverse:
            pi, shift, keep = -pi, SUB - d, row < SUB - d
        else:
            shift, keep = d, row >= d
        sr = jnp.where(keep, pltpu.roll(xr, shift, 1), 0.0)
        si = jnp.where(keep, pltpu.roll(xi, shift, 1), 0.0)
        xr, xi = xr + pr * sr - pi * si, xi + pr * si + pi * sr
    xr, xi = xr.reshape(t_len, hs), xi.reshape(t_len, hs)
    chunks = [slice(c * LANES, (c + 1) * LANES) for c in range(hs // LANES)]
    for c, lanes in enumerate(chunks):
        sr_ref[c] = xr[:, lanes]
        si_ref[c] = xi[:, lanes]
    edge = pl.ds(0 if reverse else SUB - 1, n_tiles, stride=SUB)
    tr, ti = doubling(jnp.concatenate([sr_ref[c, edge, :] for c in range(len(chunks))], axis=1),
                      jnp.concatenate([si_ref[c, edge, :] for c in range(len(chunks))], axis=1), n_tiles, 3, n_tiles)
    trow = lax.broadcasted_iota(jnp.int32, tr.shape, 0)
    if reverse:
        shift, keep = n_tiles - 1, trow < n_tiles - 1
    else:
        shift, keep = 1, trow >= 1
    er_ref[...] = jnp.where(keep, pltpu.roll(tr, shift, 0), 0.0)
    ei_ref[...] = jnp.where(keep, pltpu.roll(ti, shift, 0), 0.0)
    lin = POW_ROWS + SUB if reverse else POW_ROWS
    mr, mi = apow_ref[lin:lin + SUB, :hs], apow_ref[lin:lin + SUB, hs:]
    if reverse:
        mi = -mi
    for t in range(n_tiles):
        rows = slice(t * SUB, (t + 1) * SUB)
        er, ei = er_ref[t:t + 1, :], ei_ref[t:t + 1, :]
        add_r, add_i = mr * er - mi * ei, mr * ei + mi * er
        for c, lanes in enumerate(chunks):
            sr_ref[c, rows, :] += add_r[:, lanes]
            si_ref[c, rows, :] += add_i[:, lanes]
    return (jnp.concatenate([sr_ref[c] for c in range(len(chunks))], axis=1),
            jnp.concatenate([si_ref[c] for c in range(len(chunks))], axis=1))


def _ssm_chunk(s):
    t_len = _pick(s, 256, 8)
    assert t_len & (t_len - 1) == 0 and t_len <= 1 << POW_ROWS, t_len
    return t_len


def _fold_carry(br, bi, carry_ref, apow_ref, at_row, conj):
    hs = BLOCK_STATES
    cr, ci = carry_ref[0:1, :hs], carry_ref[0:1, hs:]
    ar, ai = apow_ref[0:1, :hs], apow_ref[0:1, hs:]
    if conj:
        ai = -ai
    here = lax.broadcasted_iota(jnp.int32, br.shape, 0) == at_row
    return jnp.where(here, br + (ar * cr - ai * ci), br), jnp.where(here, bi + (ar * ci + ai * cr), bi)


def _ssm_fwd(proj, ucol, bbd, ccat, dskip, apow, t_len, *, name):
    s = proj.shape[0]
    nb = bbd.shape[0]
    nc = s // t_len
    hs = BLOCK_STATES

    def body(u_ref, bbd_ref, ccat_ref, d_ref, apow_ref, y_ref, z_ref, xs_ref, carry_ref, *bufs):
        @pl.when(pl.program_id(1) == 0)
        def _():
            carry_ref[...] = jnp.zeros_like(carry_ref)

        xs_ref[...] = carry_ref[...]
        u = u_ref[...]
        bu = jnp.dot(u.astype(BF16), bbd_ref[...], preferred_element_type=F32)
        br, bi = _fold_carry(bu[:, :hs], bu[:, hs:], carry_ref, apow_ref, 0, False)
        xr, xi = _scan(br, bi, apow_ref, bufs, t_len, False)
        xcat = jnp.concatenate([xr, xi], axis=1)
        carry_ref[...] = jnp.broadcast_to(xcat[t_len - 1:t_len, :], carry_ref.shape)
        y = jnp.dot(xcat.astype(BF16), ccat_ref[...], preferred_element_type=F32) + d_ref[...] * u
        y_ref[...] = y
        z_ref[...] = _gelu(y).astype(z_ref.dtype)

    return pl.pallas_call(
        body, name=name, grid=(nb, nc),
        in_specs=[pl.BlockSpec((t_len, LANES), lambda j, n: (n, ucol + j)),
                  pl.BlockSpec((None, LANES, 2 * hs), lambda j, n: (j, 0, 0)),
                  pl.BlockSpec((None, 2 * hs, LANES), lambda j, n: (j, 0, 0)),
                  pl.BlockSpec((1, LANES), lambda j, n: (0, j)),
                  pl.BlockSpec((None, TAB_ROWS, 2 * hs), lambda j, n: (j, 0, 0))],
        out_specs=[pl.BlockSpec((t_len, LANES), lambda j, n: (n, j)),
                   pl.BlockSpec((t_len, LANES), lambda j, n: (n, j)),
                   pl.BlockSpec((None, None, 8, 2 * hs), lambda j, n: (j, n, 0, 0))],
        out_shape=[jax.ShapeDtypeStruct((s, nb * LANES), F32), jax.ShapeDtypeStruct((s, nb * LANES), BF16),
                   jax.ShapeDtypeStruct((nb, nc, 8, 2 * hs), F32)],
        scratch_shapes=[pltpu.VMEM((8, 2 * hs), F32)] + _scan_bufs(t_len),
        compiler_params=_params(40 << 20),
    )(proj, bbd, ccat, dskip, apow)


def _ssm_bwd(proj, ucol, y, dzd, dz2, xs, bbd, ccat, dskip, apow, t_len, *, name):
    s = proj.shape[0]
    nb = bbd.shape[0]
    nc = s // t_len
    hs = BLOCK_STATES

    def body(u_ref, y_ref, dzd_ref, dz2_ref, xs_ref, bbd_ref, ccat_ref, d_ref, apow_ref,
             du_ref, dbbd_ref, dccat_ref, dd_ref, da_ref, gcarry_ref, *bufs):
        @pl.when(pl.program_id(1) == 0)
        def _():
            gcarry_ref[...] = jnp.zeros_like(gcarry_ref)
            dbbd_ref[...] = jnp.zeros_like(dbbd_ref)
            dccat_ref[...] = jnp.zeros_like(dccat_ref)
            dd_ref[...] = jnp.zeros_like(dd_ref)
            da_ref[...] = jnp.zeros_like(da_ref)

        u = u_ref[...]
        ub = u.astype(BF16)
        dy = (dzd_ref[...] + dz2_ref[...]) * _gelu_grad(y_ref[...])
        dyb = dy.astype(BF16)
        bu = jnp.dot(ub, bbd_ref[...], preferred_element_type=F32)
        br, bi = _fold_carry(bu[:, :hs], bu[:, hs:], xs_ref, apow_ref, 0, False)
        xr, xi = _scan(br, bi, apow_ref, bufs[:4], t_len, False)
        sr, si = xs_ref[0:1, :hs], xs_ref[0:1, hs:]
        dxd = lax.dot_general(dyb, ccat_ref[...], _NT, preferred_element_type=F32)
        dr, di = _fold_carry(dxd[:, :hs], dxd[:, hs:], gcarry_ref, apow_ref, t_len - 1, True)
        gr, gi = _scan(dr, di, apow_ref, bufs[4:], t_len, True)
        gcat = jnp.concatenate([gr, gi], axis=1)
        gcarry_ref[...] = jnp.broadcast_to(gcat[0:1, :], gcarry_ref.shape)
        gb = gcat.astype(BF16)
        du_ref[...] = lax.dot_general(gb, bbd_ref[...], _NT, preferred_element_type=F32) + d_ref[...] * dy
        dbbd_ref[...] += lax.dot_general(ub, gb, _TN, preferred_element_type=F32)
        xb = jnp.concatenate([xr, xi], axis=1).astype(BF16)
        dccat_ref[...] += lax.dot_general(xb, dyb, _TN, preferred_element_type=F32)
        dd_ref[...] += jnp.sum(dy * u, axis=0, keepdims=True)
        first = lax.broadcasted_iota(jnp.int32, xr.shape, 0) == 0
        xpr = jnp.where(first, sr, pltpu.roll(xr, 1, 0))
        xpi = jnp.where(first, si, pltpu.roll(xi, 1, 0))
        dar = jnp.sum(gr * xpr + gi * xpi, axis=0, keepdims=True)
        dai = jnp.sum(gi * xpr - gr * xpi, axis=0, keepdims=True)
        da_ref[...] += jnp.concatenate([dar, dai], axis=1)

    def rows(j, n):
        return nc - 1 - n

    chunk = pl.BlockSpec((t_len, LANES), lambda j, n: (rows(j, n), j))
    return pl.pallas_call(
        body, name=name, grid=(nb, nc),
        in_specs=[pl.BlockSpec((t_len, LANES), lambda j, n: (rows(j, n), ucol + j)), chunk, chunk, chunk,
                  pl.BlockSpec((None, None, 8, 2 * hs), lambda j, n: (j, rows(j, n), 0, 0)),
                  pl.BlockSpec((None, LANES, 2 * hs), lambda j, n: (j, 0, 0)),
                  pl.BlockSpec((None, 2 * hs, LANES), lambda j, n: (j, 0, 0)),
                  pl.BlockSpec((1, LANES), lambda j, n: (0, j)),
                  pl.BlockSpec((None, TAB_ROWS, 2 * hs), lambda j, n: (j, 0, 0))],
        out_specs=[chunk,
                   pl.BlockSpec((None, LANES, 2 * hs), lambda j, n: (j, 0, 0)),
                   pl.BlockSpec((None, 2 * hs, LANES), lambda j, n: (j, 0, 0)),
                   pl.BlockSpec((1, LANES), lambda j, n: (0, j)),
                   pl.BlockSpec((None, 1, 2 * hs), lambda j, n: (j, 0, 0))],
        out_shape=[jax.ShapeDtypeStruct((s, nb * LANES), F32),
                   jax.ShapeDtypeStruct((nb, LANES, 2 * hs), F32),
                   jax.ShapeDtypeStruct((nb, 2 * hs, LANES), F32),
                   jax.ShapeDtypeStruct((1, nb * LANES), F32),
                   jax.ShapeDtypeStruct((nb, 1, 2 * hs), F32)],
        scratch_shapes=[pltpu.VMEM((8, 2 * hs), F32)] + _scan_bufs(t_len) + _scan_bufs(t_len),
        compiler_params=_params(48 << 20),
    )(proj, y, dzd, dz2, xs, bbd, ccat, dskip, apow)


def _to_blocks(a):
    g, p, k = a.shape
    nb = g // GROUPS_PER_BLOCK
    eye = jnp.eye(GROUPS_PER_BLOCK, dtype=a.dtype)
    a4 = a.reshape(nb, GROUPS_PER_BLOCK, p, k)
    out = jnp.einsum("ab,jbpk->jakbp", eye, a4)
    return out.reshape(nb, GROUPS_PER_BLOCK * k, GROUPS_PER_BLOCK * p)


def _from_blocks(d, p, k):
    nb = d.shape[0]
    d5 = d.reshape(nb, GROUPS_PER_BLOCK, k, GROUPS_PER_BLOCK, p)
    eye = jnp.eye(GROUPS_PER_BLOCK, dtype=bool)[None, :, None, :, None]
    diag = jnp.sum(jnp.where(eye, d5, 0.0), axis=1)
    return jnp.transpose(diag, (0, 2, 3, 1)).reshape(nb * GROUPS_PER_BLOCK, p, k)


def _merge_fwd(attn, y, gl, g_a, g_s, name):
    s, wa = attn.shape
    ws = y.shape[1]
    tm = _pick(s, 256, 16)

    def body(a_ref, y_ref, gl_ref, ga_ref, gs_ref, o_ref):
        av = a_ref[...]
        o_ref[:, :wa] = ((av * _rstd(av)) * ga_ref[...]).astype(o_ref.dtype)
        sv = _gelu(y_ref[...]) * jax.nn.sigmoid(gl_ref[...])
        o_ref[:, wa:] = ((sv * _rstd(sv)) * gs_ref[...]).astype(o_ref.dtype)

    return pl.pallas_call(
        body, name=name, grid=(s // tm,),
        in_specs=[_row_spec(tm, wa), _row_spec(tm, ws), _row_spec(tm, ws), _vec_spec(wa), _vec_spec(ws)],
        out_specs=_row_spec(tm, wa + ws), out_shape=jax.ShapeDtypeStruct((s, wa + ws), BF16),
    )(attn, y, gl, g_a, g_s)


def _merge_bwd(dmerged, attn, y, gl, g_a, g_s, name):
    s, wa = attn.shape
    ws = y.shape[1]
    tm = _pick(s, 256, 16)

    def body(dm_ref, a_ref, y_ref, gl_ref, ga_ref, gs_ref, da_ref, dgl_ref, dzd_ref, dga_ref, dgs_ref):
        @pl.when(pl.program_id(0) == 0)
        def _():
            dga_ref[...] = jnp.zeros_like(dga_ref)
            dgs_ref[...] = jnp.zeros_like(dgs_ref)

        dan, dsn = dm_ref[:, :wa], dm_ref[:, wa:]
        av = a_ref[...]
        ra = _rstd(av)
        ahat = av * ra
        dga_ref[...] += jnp.sum(dan * ahat, axis=0, keepdims=True)
        da_ref[...] = _norm_bwd(dan * ga_ref[...], ahat, ra)
        z = _gelu(y_ref[...])
        sig = jax.nn.sigmoid(gl_ref[...])
        sv = z * sig
        rs = _rstd(sv)
        shat = sv * rs
        dgs_ref[...] += jnp.sum(dsn * shat, axis=0, keepdims=True)
        dssm = _norm_bwd(dsn * gs_ref[...], shat, rs)
        dzd_ref[...] = dssm * sig
        dgl_ref[...] = (dssm * z * sig * (1.0 - sig)).astype(dgl_ref.dtype)

    return pl.pallas_call(
        body, name=name, grid=(s // tm,),
        in_specs=[_row_spec(tm, wa + ws), _row_spec(tm, wa), _row_spec(tm, ws), _row_spec(tm, ws),
                  _vec_spec(wa), _vec_spec(ws)],
        out_specs=[_row_spec(tm, wa), _row_spec(tm, ws), _row_spec(tm, ws), _vec_spec(wa), _vec_spec(ws)],
        out_shape=[jax.ShapeDtypeStruct((s, wa), F32), jax.ShapeDtypeStruct((s, ws), BF16),
                   jax.ShapeDtypeStruct((s, ws), F32), jax.ShapeDtypeStruct((1, wa), F32),
                   jax.ShapeDtypeStruct((1, ws), F32)],
    )(dmerged, attn, y, gl, g_a, g_s)


def _shift_down(main, halo, k):
    rolled = pltpu.roll(main, k, 0)
    row = lax.broadcasted_iota(jnp.int32, main.shape, 0)
    for r in range(k):
        rolled = jnp.where(row == r, halo[8 - k + r:8 - k + r + 1, :], rolled)
    return rolled


def _shift_up(main, halo, k):
    tm = main.shape[0]
    rolled = pltpu.roll(main, tm - k, 0)
    row = lax.broadcasted_iota(jnp.int32, main.shape, 0)
    for r in range(k):
        rolled = jnp.where(row == tm - k + r, halo[r:r + 1, :], rolled)
    return rolled


def _conv(main, halo, w_ref, b_ref):
    return (b_ref[...] + w_ref[0:1, :] * _shift_down(main, halo, 2) + w_ref[1:2, :] * _shift_down(main, halo, 1)
            + w_ref[2:3, :] * main)


def _gate_tiles(s, f):
    return _pick(s, 512, 16), _pick(f, 512, LANES)


def _gate_in_specs(tm, tn, nfb, order):
    hb = tm // 8
    ij = (lambda a, b: (b, a)) if order == "ji" else (lambda a, b: (a, b))

    def main(off):
        return pl.BlockSpec((tm, tn), lambda a, b: (ij(a, b)[0], ij(a, b)[1] + off))

    def halo(off):
        return pl.BlockSpec((8, tn), lambda a, b: (jnp.maximum(ij(a, b)[0] * hb - 1, 0), ij(a, b)[1] + off))

    def vec(rows, off):
        return pl.BlockSpec((rows, tn), lambda a, b: (0, ij(a, b)[1] + off))

    return [main(0), main(nfb), halo(0), halo(nfb), vec(3, 0), vec(3, nfb), vec(1, 0), vec(1, nfb)]


def _gate_fwd(up0, conv_w, conv_b, name):
    s, f2 = up0.shape
    f = f2 // 2
    tm, tn = _gate_tiles(s, f)
    nfb = f // tn

    def body(v_ref, g_ref, vh_ref, gh_ref, wv_ref, wg_ref, bv_ref, bg_ref, o_ref):
        top = pl.program_id(0) == 0
        vh = jnp.where(top, 0.0, vh_ref[...])
        gh = jnp.where(top, 0.0, gh_ref[...])
        val = _conv(v_ref[...], vh, wv_ref, bv_ref)
        gate = _conv(g_ref[...], gh, wg_ref, bg_ref)
        o_ref[...] = (_gelu(gate) * val).astype(o_ref.dtype)

    return pl.pallas_call(
        body, name=name, grid=(s // tm, nfb),
        in_specs=_gate_in_specs(tm, tn, nfb, "ij"), out_specs=pl.BlockSpec((tm, tn), lambda i, j: (i, j)),
        out_shape=jax.ShapeDtypeStruct((s, f), BF16),
        compiler_params=_params(24 * tm * tn * 4 + (4 << 20)),
    )(up0, up0, up0, up0, conv_w, conv_w, conv_b, conv_b)


def _gate_bwd(up0, conv_w, conv_b, da, name):
    s, f2 = up0.shape
    f = f2 // 2
    tm, tn = _gate_tiles(s, f)
    nfb = f // tn

    def body(v_ref, g_ref, vh_ref, gh_ref, wv_ref, wg_ref, bv_ref, bg_ref, da_ref, dup_ref, dcb_ref, dcw_ref):
        top = pl.program_id(1) == 0

        @pl.when(top)
        def _():
            dcb_ref[...] = jnp.zeros_like(dcb_ref)
            dcw_ref[...] = jnp.zeros_like(dcw_ref)

        halos = (jnp.where(top, 0.0, vh_ref[...]), jnp.where(top, 0.0, gh_ref[...]))
        mains = (v_ref[...], g_ref[...])
        val = _conv(mains[0], halos[0], wv_ref, bv_ref)
        gate = _conv(mains[1], halos[1], wg_ref, bg_ref)
        dav = da_ref[...]
        dups = (dav * _gelu(gate), dav * val * _gelu_grad(gate))
        for half in range(2):
            dup = dups[half]
            dup_ref[half] = dup
            dcb_ref[half] += jnp.sum(dup, axis=0, keepdims=True)
            dcw_ref[half, 0:1, :] += jnp.sum(dup * _shift_down(mains[half], halos[half], 2), axis=0, keepdims=True)
            dcw_ref[half, 1:2, :] += jnp.sum(dup * _shift_down(mains[half], halos[half], 1), axis=0, keepdims=True)
            dcw_ref[half, 2:3, :] += jnp.sum(dup * mains[half], axis=0, keepdims=True)

    return pl.pallas_call(
        body, name=name, grid=(nfb, s // tm),
        in_specs=_gate_in_specs(tm, tn, nfb, "ji") + [pl.BlockSpec((tm, tn), lambda j, i: (i, j))],
        out_specs=[pl.BlockSpec((2, tm, tn), lambda j, i: (0, i, j)),
                   pl.BlockSpec((2, 1, tn), lambda j, i: (0, 0, j)),
                   pl.BlockSpec((2, 3, tn), lambda j, i: (0, 0, j))],
        out_shape=[jax.ShapeDtypeStruct((2, s, f), F32), jax.ShapeDtypeStruct((2, 1, f), F32),
                   jax.ShapeDtypeStruct((2, 3, f), F32)],
        compiler_params=_params(40 * tm * tn * 4 + (4 << 20)),
    )(up0, up0, up0, up0, conv_w, conv_w, conv_b, conv_b, da)


def _conv_bwd(dup, conv_w, name):
    _, s, f = dup.shape
    tm, tn = _pick(s, 512, 16), _pick(f, 1536, LANES)
    nfb, ni, hb = f // tn, s // tm, tm // 8

    def body(d_ref, dh_ref, w_ref, o_ref):
        main = d_ref[...]
        halo = jnp.where(pl.program_id(1) == ni - 1, 0.0, dh_ref[...])
        o_ref[...] = (w_ref[2:3, :] * main + w_ref[1:2, :] * _shift_up(main, halo, 1)
                      + w_ref[0:1, :] * _shift_up(main, halo, 2)).astype(o_ref.dtype)

    return pl.pallas_call(
        body, name=name, grid=(2, ni, nfb),
        in_specs=[pl.BlockSpec((None, tm, tn), lambda h, i, j: (h, i, j)),
                  pl.BlockSpec((None, 8, tn), lambda h, i, j: (h, jnp.minimum((i + 1) * hb, s // 8 - 1), j)),
                  pl.BlockSpec((3, tn), lambda h, i, j: (0, h * nfb + j))],
        out_specs=pl.BlockSpec((tm, tn), lambda h, i, j: (i, h * nfb + j)),
        out_shape=jax.ShapeDtypeStruct((s, 2 * f), BF16),
        compiler_params=_params(12 * tm * tn * 4 + (4 << 20)),
    )(dup, dup, conv_w)


def _adamw(w, g, m, v, name):
    r, c = w.shape
    tr = _pick(r, max(8, (1 << 19) // max(c, 1) // 8 * 8), 8)
    c1, c2 = 1.0 / (1.0 - ADAM_B1 ** ADAM_STEP), 1.0 / (1.0 - ADAM_B2 ** ADAM_STEP)

    def body(w_ref, g_ref, m_ref, v_ref, d_ref, nm_ref, nv_ref):
        gv = g_ref[...]
        nm = ADAM_B1 * m_ref[...] + (1.0 - ADAM_B1) * gv
        nv = ADAM_B2 * v_ref[...] + (1.0 - ADAM_B2) * (gv * gv)
        nm_ref[...] = nm
        nv_ref[...] = nv
        d_ref[...] = -ADAM_LR * ((nm * c1) / (jnp.sqrt(nv * c2) + ADAM_EPS) + ADAM_WD * w_ref[...])

    spec = pl.BlockSpec((tr, c), lambda i: (i, 0))
    out = jax.ShapeDtypeStruct((r, c), F32)
    return pl.pallas_call(body, name=name, grid=(r // tr,), in_specs=[spec] * 4, out_specs=[spec] * 3,
                          out_shape=[out] * 3, compiler_params=_params(14 * tr * c * 4 + (4 << 20)))(w, g, m, v)


def _adamw_nd(w, g, m, v, name):
    shape = w.shape
    c = shape[-1]
    outs = _adamw(w.reshape(-1, c), g.reshape(-1, c), m.reshape(-1, c), v.reshape(-1, c), name)
    return [o.reshape(shape) for o in outs]


BIG = ("w_in", "w_glu", "w_out", "w_up", "w_down")
SMALL = ("b_ada", "g_pre_mix", "g_post_mix", "attn_sinks", "lam_re", "lam_im", "log_step", "ssm_b_re", "ssm_b_im",
         "ssm_c_re", "ssm_c_im", "ssm_d", "g_attn_out", "g_ssm_out", "g_pre_ffn", "g_post_ffn", "conv_b")
ORDER = ("w_ada", "b_ada", "g_pre_mix", "g_post_mix", "w_in", "attn_sinks", "lam_re", "lam_im", "log_step",
         "ssm_b_re", "ssm_b_im", "ssm_c_re", "ssm_c_im", "ssm_d", "w_glu", "g_attn_out", "g_ssm_out", "w_out",
         "g_pre_ffn", "g_post_ffn", "w_up", "conv_w", "conv_b", "w_down")
COL_SHARDED = ("w_in", "w_up")


def kernel(x, c, w_ada, b_ada, g_pre_mix, g_post_mix, w_in, attn_sinks, lam_re, lam_im, log_step, ssm_b_re, ssm_b_im, ssm_c_re, ssm_c_im, ssm_d, w_glu, g_attn_out, g_ssm_out, w_out, g_pre_ffn, g_post_ffn, w_up, conv_w, conv_b, w_down, loss_target, m_w_ada, m_b_ada, m_g_pre_mix, m_g_post_mix, m_w_in, m_attn_sinks, m_lam_re, m_lam_im, m_log_step, m_ssm_b_re, m_ssm_b_im, m_ssm_c_re, m_ssm_c_im, m_ssm_d, m_w_glu, m_g_attn_out, m_g_ssm_out, m_w_out, m_g_pre_ffn, m_g_post_ffn, m_w_up, m_conv_w, m_conv_b, m_w_down, v_w_ada, v_b_ada, v_g_pre_mix, v_g_post_mix, v_w_in, v_attn_sinks, v_lam_re, v_lam_im, v_log_step, v_ssm_b_re, v_ssm_b_im, v_ssm_c_re, v_ssm_c_im, v_ssm_d, v_w_glu, v_g_attn_out, v_g_ssm_out, v_w_out, v_g_pre_ffn, v_g_post_ffn, v_w_up, v_conv_w, v_conv_b, v_w_down):
    env = dict(locals())
    W = {n: env[n] for n in ORDER}
    M = {n: env["m_" + n] for n in ORDER}
    V = {n: env["v_" + n] for n in ORDER}

    depth = w_ada.shape[0]
    s, d = x.shape[1], x.shape[2]
    xs0 = x.reshape(s, d)
    tgt = loss_target.reshape(s, d)
    attn_w = d // 2
    ssm_w = d - attn_w
    in_cols = w_in.shape[2] * N_DEV
    kv_dim = (in_cols - attn_w - ssm_w) // 2
    n_q, n_kv = attn_w // HEAD_DIM, kv_dim // HEAD_DIM
    n_grp = ssm_w // SSM_GROUP
    nb = ssm_w // LANES
    f = w_down.shape[1] * N_DEV
    ucol = (attn_w + 2 * kv_dim) // LANES
    t_len = _ssm_chunk(s)
    me = 4 * lax.axis_index("x") + 2 * lax.axis_index("y") + lax.axis_index("c")

    def at_block(ref, idx):
        return ref.at[idx]

    def at_rows(n_rows):
        return lambda ref, idx: ref.at[:, pl.ds(pl.multiple_of(idx * n_rows, 8), n_rows), :]

    def at_cols(n_cols):
        return lambda ref, idx: ref.at[:, :, pl.ds(pl.multiple_of(idx * n_cols, LANES), n_cols)]

    first = _gather_multi([w_in.astype(BF16), conv_w, c],
                          [(N_DEV,) + w_in.shape, (N_DEV,) + conv_w.shape, (N_DEV,) + c.shape],
                          [at_block, at_block, at_block], "ag_first")
    full = {"w_in": _cols_from_blocks(first[0], "w_in_layout")}
    conv_w_full = jnp.transpose(first[1], (1, 2, 0, 3)).reshape(depth, 3, 2 * f)
    c_all = first[2].reshape(N_DEV, d)

    def at_rows2(n_rows):
        return lambda ref, idx: ref.at[pl.ds(pl.multiple_of(idx * n_rows, 8), n_rows), :]

    def at_cols2(n_cols):
        return lambda ref, idx: ref.at[:, pl.ds(pl.multiple_of(idx * n_cols, LANES), n_cols)]

    def whole(ref, idx):
        return ref

    later = [(n, l) for l in range(depth) for n in BIG[1:]]
    later_srcs = [W[n][l].astype(BF16) for n, l in later]
    later_views = [at_cols2(W[n].shape[2]) if n in COL_SHARDED else at_rows2(W[n].shape[1]) for n, _ in later]
    later_shapes = [(W[n].shape[1], N_DEV * W[n].shape[2]) if n in COL_SHARDED
                    else (N_DEV * W[n].shape[1], W[n].shape[2]) for n, _ in later]
    lands = _place_own(later_srcs, [lax.empty(shp, BF16) for shp in later_shapes], later_views, "ag_place_own")
    ag_started, ag_token = _exchange_start(later_srcs, lands, [whole] * len(later), later_views, "ag_start")

    def weights_arrived(names, l, after, name):
        picks = [later.index((n, l)) for n in names]
        _, got = _exchange_wait([ag_started[i] for i in picks], after, [whole] * len(picks),
                                [later_views[i] for i in picks], name)
        return dict(zip(names, got))

    c_pad = jnp.pad(c_all, ((0, 16 - N_DEV), (0, 0)))
    n_ada = w_ada.shape[2]
    b_shard = lax.dynamic_slice_in_dim(b_ada, me * n_ada, n_ada, axis=1).reshape(depth, 1, n_ada)
    ada_part, c_act = _ada_fwd(c_pad, w_ada, b_shard, "ada_fwd")
    ada_all = _all_gather(ada_part.reshape(depth * 16, n_ada), "ag_ada").reshape(N_DEV, depth, 16, n_ada)
    ada_me = lax.dynamic_index_in_dim(ada_all, me, axis=2, keepdims=False)
    ada = jnp.transpose(ada_me, (1, 0, 2)).reshape(depth, 6, 1, d) + ag_token[0, 0]

    gp = n_grp * STATE

    def hgp(a):
        return jnp.transpose(a, (2, 0, 1)).reshape(SSM_GROUP, gp)

    ssm = []
    for l in range(depth):
        lr, li = lam_re[l].reshape(1, gp), lam_im[l].reshape(1, gp)
        ls = jnp.repeat(log_step[l], STATE).reshape(1, gp)
        br, bi = hgp(ssm_b_re[l]), hgp(ssm_b_im[l])
        bbr, bbi, tab_r, tab_i = _ssm_params_fwd(lr, li, ls, br, bi, f"ssm_params_fwd{l}")
        bb_re = jnp.transpose(bbr.reshape(SSM_GROUP, n_grp, STATE), (1, 2, 0))
        bb_im = jnp.transpose(bbi.reshape(SSM_GROUP, n_grp, STATE), (1, 2, 0))
        bbd = jnp.concatenate([_to_blocks(bb_re), _to_blocks(bb_im)], axis=2).astype(BF16)
        c_re_t = jnp.transpose(ssm_c_re[l], (0, 2, 1))
        c_im_t = jnp.transpose(ssm_c_im[l], (0, 2, 1))
        ccat = jnp.concatenate([jnp.transpose(_to_blocks(c_re_t), (0, 2, 1)),
                                -jnp.transpose(_to_blocks(c_im_t), (0, 2, 1))], axis=1).astype(BF16)

        def tab(t):
            return t.reshape(TAB_ROWS, nb, BLOCK_STATES)

        apow = jnp.transpose(jnp.concatenate([tab(tab_r), tab(tab_i)], axis=2), (1, 0, 2))
        ssm.append(dict(lr=lr, li=li, ls=ls, br=br, bi=bi, bbd=bbd, ccat=ccat, apow=apow,
                        dskip=ssm_d[l].reshape(1, ssm_w)))

    sinks_pad = jnp.pad(attn_sinks, ((0, 0), (0, LANES - n_q)))

    def vec(a):
        return a.reshape(1, -1)

    saved = []
    fw = [dict() for _ in range(depth)]
    xin = xs0
    for l in range(depth):
        sh_m, sc_m, gt_m, sh_f, sc_f, gt_f = (ada[l, i] for i in range(6))
        p = ssm[l]
        if l > 0:
            fw[l].update(weights_arrived(BIG[1:], l, xin, f"ag_wait_layer{l}"))
        h1 = _modnorm_fwd(xin, vec(g_pre_mix[l]), sc_m, sh_m, f"modnorm_mix_fwd{l}")
        proj = _matmul(h1, full["w_in"][l], name=f"mm_in{l}")
        attn, lse = _attn_fwd(proj, sinks_pad[l:l + 1], n_q=n_q, n_kv=n_kv, name=f"attn_fwd{l}")
        y, z, xstart = _ssm_fwd(proj, ucol, p["bbd"], p["ccat"], p["dskip"], p["apow"], t_len, name=f"ssm_fwd{l}")
        if l == 0:
            fw[0].update(weights_arrived(("w_glu", "w_out"), 0, z, "ag_wait_mix0"))
        gl = _matmul(z, fw[l]["w_glu"], name=f"mm_glu{l}")
        merged = _merge_fwd(attn, y, gl, vec(g_attn_out[l]), vec(g_ssm_out[l]), f"merge_fwd{l}")
        mix = _matmul(merged, fw[l]["w_out"], name=f"mm_out{l}")
        x2 = _resnorm_fwd(xin, mix, vec(g_post_mix[l]), gt_m, f"resnorm_mix_fwd{l}")
        h2 = _modnorm_fwd(x2, vec(g_pre_ffn[l]), sc_f, sh_f, f"modnorm_ffn_fwd{l}")
        if l == 0:
            fw[0].update(weights_arrived(("w_up", "w_down"), 0, h2, "ag_wait_ffn0"))
        up0 = _matmul(h2, fw[l]["w_up"], name=f"mm_up{l}")
        cw, cb = conv_w_full[l], vec(conv_b[l])
        act = _gate_fwd(up0, cw, cb, f"gate_fwd{l}")
        ff = _matmul(act, fw[l]["w_down"], name=f"mm_down{l}")
        x3 = _resnorm_fwd(x2, ff, vec(g_post_ffn[l]), gt_f, f"resnorm_ffn_fwd{l}")
        saved.append(dict(xin=xin, h1=h1, proj=proj, attn=attn, lse=lse, y=y, z=z, xstart=xstart, gl=gl,
                          merged=merged, mix=mix, x2=x2, h2=h2, up0=up0, act=act, ff=ff))
        xin = x3

    dxo, loss_acc = _loss_bwd(xin, tgt, "loss")
    loss = lax.psum(loss_acc[0, 0], ("x", "y", "c"))

    grads = {n: [None] * depth for n in ORDER}
    dada = [None] * depth
    big_blocks = {n: [None] * depth for n in BIG}
    seg = jnp.pad(jnp.repeat(jnp.eye(n_grp, dtype=F32), STATE, axis=0), ((0, 0), (0, (-n_grp) % LANES)))

    def part_view(n):
        shp = W[n].shape
        if n == "w_in":
            return at_block, "blk"
        if n in COL_SHARDED:
            return at_cols2(shp[2]), "cols"
        return at_rows2(shp[1]), "rows"

    rs_groups = []

    def send_partials(items, name):
        parts = [big_blocks[n][l] for n, l in items]
        lands = [lax.empty((N_DEV,) + W[n].shape[1:], BF16) for n, _ in items]
        started, token = _exchange_start(parts, lands, [part_view(n)[0] for n, _ in items],
                                         [at_block] * len(items), name)
        rs_groups.append((items, started, name))
        return token[0, 0]

    order = jnp.zeros((), F32)
    for l in reversed(range(depth)):
        sh_m, sc_m, gt_m, sh_f, sc_f, gt_f = (ada[l, i] for i in range(6))
        gt_f = gt_f + order
        a, p = saved[l], ssm[l]
        cw, cb = conv_w_full[l], vec(conv_b[l])
        dff, dg, dgt_f = _resnorm_bwd(dxo, a["ff"], vec(g_post_ffn[l]), gt_f, f"resnorm_ffn_bwd{l}")
        grads["g_post_ffn"][l] = dg
        dact = _matmul(dff, fw[l]["w_down"], tb=True, name=f"mm_down_dx{l}")
        big_blocks["w_down"][l] = _matmul(a["act"], dff, ta=True, out_dtype=BF16, name=f"mm_down_dw{l}")
        dup, dcb, dcw = _gate_bwd(a["up0"], cw, cb, dact, f"gate_bwd{l}")
        grads["conv_b"][l] = dcb.reshape(1, 2 * f)
        grads["conv_w"][l] = jnp.transpose(dcw, (1, 0, 2)).reshape(3, 2 * f)
        dup0 = _conv_bwd(dup, cw, f"conv_bwd{l}")
        dh2 = _matmul(dup0, fw[l]["w_up"], tb=True, name=f"mm_up_dx{l}")
        big_blocks["w_up"][l] = _matmul(a["h2"], dup0, ta=True, out_dtype=BF16, name=f"mm_up_dw{l}")
        if l == 0:
            sc_f = sc_f + send_partials([("w_down", 0), ("w_up", 0)], "rs_start_ffn0")
        dx2, dg, dsc_f, dsh_f = _modnorm_bwd(dh2, a["x2"], vec(g_pre_ffn[l]), sc_f, dxo, f"modnorm_ffn_bwd{l}")
        grads["g_pre_ffn"][l] = dg
        dmix, dg, dgt_m = _resnorm_bwd(dx2, a["mix"], vec(g_post_mix[l]), gt_m, f"resnorm_mix_bwd{l}")
        grads["g_post_mix"][l] = dg
        dmerged = _matmul(dmix, fw[l]["w_out"], tb=True, name=f"mm_out_dx{l}")
        big_blocks["w_out"][l] = _matmul(a["merged"], dmix, ta=True, out_dtype=BF16, name=f"mm_out_dw{l}")
        dattn, dgl, dzd, dga, dgs = _merge_bwd(dmerged, a["attn"], a["y"], a["gl"], vec(g_attn_out[l]),
                                               vec(g_ssm_out[l]), f"merge_bwd{l}")
        grads["g_attn_out"][l], grads["g_ssm_out"][l] = dga, dgs
        dz2 = _matmul(dgl, fw[l]["w_glu"], tb=True, name=f"mm_glu_dx{l}")
        big_blocks["w_glu"][l] = _matmul(a["z"], dgl, ta=True, out_dtype=BF16, name=f"mm_glu_dw{l}")
        du, dbbd, dccat, dd, da = _ssm_bwd(a["proj"], ucol, a["y"], dzd, dz2, a["xstart"], p["bbd"], p["ccat"],
                                           p["dskip"], p["apow"], t_len, name=f"ssm_bwd{l}")
        grads["ssm_d"][l] = dd
        hs = BLOCK_STATES
        dbb_re = _from_blocks(dbbd[:, :, :hs], STATE, SSM_GROUP)
        dbb_im = _from_blocks(dbbd[:, :, hs:], STATE, SSM_GROUP)
        dccat_t = jnp.transpose(dccat, (0, 2, 1))
        grads["ssm_c_re"][l] = jnp.transpose(_from_blocks(dccat_t[:, :, :hs], STATE, SSM_GROUP), (0, 2, 1))
        grads["ssm_c_im"][l] = -jnp.transpose(_from_blocks(dccat_t[:, :, hs:], STATE, SSM_GROUP), (0, 2, 1))
        dab_re, dab_im = da[:, 0, :hs].reshape(1, gp), da[:, 0, hs:].reshape(1, gp)
        dlr, dli, dls, dbr, dbi = _ssm_params_bwd(p["lr"], p["li"], p["ls"], p["br"], p["bi"], dab_re, dab_im,
                                                  hgp(dbb_re), hgp(dbb_im), seg, f"ssm_params_bwd{l}")
        grads["lam_re"][l], grads["lam_im"][l], grads["log_step"][l] = dlr, dli, dls[0, :n_grp]
        grads["ssm_b_re"][l] = jnp.transpose(dbr.reshape(SSM_GROUP, n_grp, STATE), (1, 2, 0))
        grads["ssm_b_im"][l] = jnp.transpose(dbi.reshape(SSM_GROUP, n_grp, STATE), (1, 2, 0))
        dq, dk, dv, dsink = _attn_bwd(a["proj"], sinks_pad[l:l + 1], a["attn"], a["lse"], dattn,
                                      n_q=n_q, n_kv=n_kv, name=f"attn_bwd{l}")
        grads["attn_sinks"][l] = dsink[0, :n_q]
        dproj = jnp.concatenate([dq, dk, dv, du], axis=1).astype(BF16)
        dh1 = _matmul(dproj, full["w_in"][l], tb=True, name=f"mm_in_dx{l}")
        big_blocks["w_in"][l] = _blocks_from_cols(_matmul(a["h1"], dproj, ta=True, name=f"mm_in_dw{l}"),
                                                  f"w_in_grad_layout{l}")
        dxo, dg, dsc_m, dsh_m = _modnorm_bwd(dh1, a["xin"], vec(g_pre_mix[l]), sc_m, dx2, f"modnorm_mix_bwd{l}")
        grads["g_pre_mix"][l] = dg
        dada[l] = jnp.concatenate([dsh_m, dsc_m, dgt_m, dsh_f, dsc_f, dgt_f], axis=1)
        if l > 0:
            order = send_partials([(n, l) for n in reversed(BIG)], f"rs_start_layer{l}")
        else:
            send_partials([("w_out", 0), ("w_glu", 0), ("w_in", 0)], "rs_start_mix0")
    grad_x = dxo.reshape(x.shape)

    me_arr = me.astype(jnp.int32).reshape(1)
    layer_grads = {n: [None] * depth for n in BIG}
    for items, started, name in rs_groups:
        mine, landed = _exchange_wait(started, dxo, [part_view(n)[0] for n, _ in items], [at_block] * len(items),
                                      name.replace("start", "wait"))
        for (n, l), part, slots in zip(items, mine, landed):
            layer_grads[n][l] = _sum_slots_own(me_arr, slots, part, part_view(n)[1], f"rs_sum_{n}{l}")
    for n in BIG:
        grads[n] = jnp.stack(layer_grads[n])

    small_order = SMALL + ("conv_w",)
    small_shapes = {n: W[n].shape for n in SMALL}
    small_shapes["conv_w"] = (depth, 3, 2 * f)
    stacked = {"b_ada": jnp.stack(dada).reshape(depth, 6 * d)}
    for n in small_order[1:]:
        stacked[n] = jnp.stack([g.reshape(small_shapes[n][1:]) for g in grads[n]])
    spack = _pack([stacked[n] for n in small_order], F32, 1024)
    sg = _all_gather(spack, "ag_small")
    ssum = _sum_slots(sg, "sum_small").reshape(-1)
    for n, g in zip(small_order, _unpack(ssum, [small_shapes[n] for n in small_order])):
        grads[n] = g
    n_cw = conv_w.shape[2]
    grads["conv_w"] = lax.dynamic_slice_in_dim(grads["conv_w"], me * n_cw, n_cw, axis=2)
    dada_all = sg.reshape(N_DEV, -1)[:, :depth * 6 * d].reshape(N_DEV, depth, 6 * d)
    dada_shard = lax.dynamic_slice_in_dim(dada_all, me * n_ada, n_ada, axis=2)
    kp = LANES
    dada_pad = jnp.pad(jnp.transpose(dada_shard, (1, 0, 2)), ((0, 0), (0, kp - N_DEV), (0, 0)))
    act_t = jnp.pad(jnp.transpose(c_act[:N_DEV]), ((0, 0), (0, kp - N_DEV)))
    grads["w_ada"] = _ada_wgrad(act_t, dada_pad, "ada_wgrad")

    delta, new_m, new_v = {}, {}, {}
    for n in ("w_ada",) + BIG + ("conv_w",):
        delta[n], new_m[n], new_v[n] = _adamw_nd(W[n], grads[n], M[n], V[n], f"adamw_{n}")
    packs = [_pack([t[n] for n in SMALL], F32, 1024) for t in (W, grads, M, V)]
    outs = _adamw(*packs, "adamw_small")
    shapes = [W[n].shape for n in SMALL]
    for tgt_d, o in zip((delta, new_m, new_v), outs):
        for n, val in zip(SMALL, _unpack(o.reshape(-1), shapes)):
            tgt_d[n] = val

    return (loss, grad_x, *[grads[n] for n in ORDER], *[delta[n] for n in ORDER],
            *[new_m[n] for n in ORDER], *[new_v[n] for n in ORDER])
```

```python
import functools
import math

import jax
import jax.numpy as jnp
from jax import lax
from jax.experimental import pallas as pl
from jax.experimental.pallas import tpu as pltpu

F32 = jnp.float32
BF16 = jnp.bfloat16

N_DEV = 8
HEAD_DIM = 64
WINDOW = 128
SSM_GROUP = 16
STATE = 64
LANES = 128
GROUPS_PER_BLOCK = LANES // SSM_GROUP
BLOCK_STATES = GROUPS_PER_BLOCK * STATE
EPS = 1e-6
NEG = -1e30
ADAM_LR, ADAM_B1, ADAM_B2, ADAM_EPS, ADAM_WD, ADAM_STEP = 0.001, 0.9, 0.999, 1e-08, 0.01, 10
VMEM_BYTES_V7X = 64 * 1024 * 1024
GELU_C = math.sqrt(2.0 / math.pi)
MESH = pl.DeviceIdType.MESH
ANY = pl.BlockSpec(memory_space=pl.ANY)


def _pick(n, pref, align):
    t = (min(pref, n) // align) * align
    while t >= align:
        if n % t == 0:
            return t
        t -= align
    return n


def _params(vmem_bytes=None):
    if vmem_bytes is None:
        return pltpu.CompilerParams()
    return pltpu.CompilerParams(vmem_limit_bytes=int(min(vmem_bytes, VMEM_BYTES_V7X - (8 << 20))))


def _gelu(x):
    return 0.5 * x * (1.0 + jnp.tanh(GELU_C * (x + 0.044715 * x * x * x)))


def _gelu_grad(x):
    th = jnp.tanh(GELU_C * (x + 0.044715 * x * x * x))
    return 0.5 * (1.0 + th) + 0.5 * x * (1.0 - th * th) * GELU_C * (1.0 + 3.0 * 0.044715 * x * x)


def _rstd(x):
    return lax.rsqrt(jnp.mean(x * x, axis=-1, keepdims=True) + EPS)


def _norm_bwd(dhat, xhat, r):
    return r * (dhat - xhat * jnp.mean(dhat * xhat, axis=-1, keepdims=True))


def _matmul(a, b, *, ta=False, tb=False, out_dtype=F32, name):
    (kdim, m) = a.shape if ta else a.shape[::-1]
    (n, k2) = b.shape if tb else b.shape[::-1]
    assert kdim == k2, (a.shape, b.shape, ta, tb)
    tm, tn, tk = _pick(m, 1024, LANES), _pick(n, 1024, LANES), _pick(kdim, 2048, LANES)
    nk = kdim // tk
    dn = (((0 if ta else 1,), (1 if tb else 0,)), ((), ()))

    def partial_product(a_ref, b_ref):
        return lax.dot_general(a_ref[...].astype(BF16), b_ref[...].astype(BF16), dn, preferred_element_type=F32)

    def body_one(a_ref, b_ref, o_ref):
        o_ref[...] = partial_product(a_ref, b_ref).astype(o_ref.dtype)

    def body_acc(a_ref, b_ref, o_ref, acc_ref):
        k = pl.program_id(2)

        @pl.when(k == 0)
        def _():
            acc_ref[...] = partial_product(a_ref, b_ref)

        @pl.when((k > 0) & (k < nk - 1))
        def _():
            acc_ref[...] += partial_product(a_ref, b_ref)

        @pl.when(k == nk - 1)
        def _():
            o_ref[...] = (acc_ref[...] + partial_product(a_ref, b_ref)).astype(o_ref.dtype)

    body = body_one if nk == 1 else body_acc
    a_spec = pl.BlockSpec((tk, tm), lambda i, j, k: (k, i)) if ta else pl.BlockSpec((tm, tk), lambda i, j, k: (i, k))
    b_spec = pl.BlockSpec((tn, tk), lambda i, j, k: (j, k)) if tb else pl.BlockSpec((tk, tn), lambda i, j, k: (k, j))
    vmem = (2 * (tm * tk * a.dtype.itemsize + tk * tn * b.dtype.itemsize) + tm * tn * 4
            + 2 * tm * tn * jnp.dtype(out_dtype).itemsize + 3 * tm * tn * 4 + (4 << 20))
    return pl.pallas_call(
        body, name=name, grid=(m // tm, n // tn, nk),
        in_specs=[a_spec, b_spec], out_specs=pl.BlockSpec((tm, tn), lambda i, j, k: (i, j)),
        out_shape=jax.ShapeDtypeStruct((m, n), out_dtype),
        scratch_shapes=[] if nk == 1 else [pltpu.VMEM((tm, tn), F32)],
        compiler_params=_params(vmem),
    )(a, b)


def _all_gather(x, name):
    def body(x_ref, out_ref, send_sems, recv_sems, local_sem):
        x_, y_, c_ = lax.axis_index("x"), lax.axis_index("y"), lax.axis_index("c")
        me, sibling = (x_, y_, c_), (x_, y_, 1 - c_)
        chips = [(1 - x_, y_), (x_, 1 - y_), (1 - x_, 1 - y_)]

        def slot(px, py, pc):
            return out_ref.at[4 * px + 2 * py + pc]

        def copy(k, block, to, src=None):
            return pltpu.make_async_remote_copy(
                src_ref=slot(*block) if src is None else src, dst_ref=slot(*block),
                send_sem=send_sems.at[k], recv_sem=recv_sems.at[k], device_id=to, device_id_type=MESH)

        mine = pltpu.make_async_copy(x_ref, slot(*me), local_sem)
        mine.start()
        first = [copy(0, me, sibling, src=x_ref)]
        first += [copy(1 + j, me, (*chip, c_), src=x_ref) for j, chip in enumerate(chips)]
        for cp in first:
            cp.start()
        passed = [copy(4 + j, (*chip, c_), sibling) for j, chip in enumerate(chips)]
        for j, chip in enumerate(chips):
            copy(1 + j, (*chip, c_), me).wait_recv()
            passed[j].start()
        copy(0, sibling, me).wait_recv()
        for j, chip in enumerate(chips):
            copy(4 + j, (*chip, 1 - c_), me).wait_recv()
        for cp in first + passed:
            cp.wait_send()
        mine.wait()

    return pl.pallas_call(
        body, name=name, out_shape=jax.ShapeDtypeStruct((N_DEV,) + x.shape, x.dtype),
        in_specs=[ANY], out_specs=ANY,
        scratch_shapes=[pltpu.SemaphoreType.DMA((7,)), pltpu.SemaphoreType.DMA((7,)), pltpu.SemaphoreType.DMA],
    )(x)


def _gather_multi(srcs, out_shapes, views, name):
    n = len(srcs)

    def body(*refs):
        src_refs, out_refs = refs[:n], refs[n:2 * n]
        send_sems, recv_sems, local_sems = refs[2 * n:]
        x_, y_, c_ = lax.axis_index("x"), lax.axis_index("y"), lax.axis_index("c")
        me, sibling = (x_, y_, c_), (x_, y_, 1 - c_)
        chips = [(1 - x_, y_), (x_, 1 - y_), (1 - x_, 1 - y_)]

        def slot(i, px, py, pc):
            return views[i](out_refs[i], 4 * px + 2 * py + pc)

        def copy(i, k, block, to, from_src=False):
            return pltpu.make_async_remote_copy(
                src_ref=src_refs[i] if from_src else slot(i, *block), dst_ref=slot(i, *block),
                send_sem=send_sems.at[7 * i + k], recv_sem=recv_sems.at[7 * i + k], device_id=to, device_id_type=MESH)

        mine = [pltpu.make_async_copy(src_refs[i], slot(i, *me), local_sems.at[i]) for i in range(n)]
        for cp in mine:
            cp.start()
        first = []
        for i in range(n):
            first.append(copy(i, 0, me, sibling, True))
            first += [copy(i, 1 + j, me, (*chip, c_), True) for j, chip in enumerate(chips)]
        for cp in first:
            cp.start()
        passed = []
        for j, chip in enumerate(chips):
            for i in range(n):
                copy(i, 1 + j, (*chip, c_), me).wait_recv()
                fwd = copy(i, 4 + j, (*chip, c_), sibling)
                fwd.start()
                passed.append(fwd)
        for i in range(n):
            copy(i, 0, sibling, me).wait_recv()
            for j, chip in enumerate(chips):
                copy(i, 4 + j, (*chip, 1 - c_), me).wait_recv()
        for cp in first + passed:
            cp.wait_send()
        for cp in mine:
            cp.wait()

    return pl.pallas_call(
        body, name=name, out_shape=[jax.ShapeDtypeStruct(s, a.dtype) for s, a in zip(out_shapes, srcs)],
        in_specs=[ANY] * n, out_specs=[ANY] * n,
        scratch_shapes=[pltpu.SemaphoreType.DMA((7 * n,)), pltpu.SemaphoreType.DMA((7 * n,)),
                        pltpu.SemaphoreType.DMA((n,))],
    )(*srcs)


HBM_SPEC = pl.BlockSpec(memory_space=pltpu.HBM)
SEM_SPEC = pl.BlockSpec(memory_space=pltpu.SEMAPHORE)
SIDE_EFFECT = pltpu.SideEffectType.DATAFLOW_SIDE_EFFECTING
N_PEERS = N_DEV - 1


def _peer(k, x_, y_, c_):
    px = 1 - x_ if (k >> 2) & 1 else x_
    py = 1 - y_ if (k >> 1) & 1 else y_
    pc = 1 - c_ if k & 1 else c_
    return (px, py, pc), 4 * px + 2 * py + pc


def _exchange_copies(src_refs, land_refs, send_sems, recv_sems, src_views, dst_views):
    x_, y_, c_ = lax.axis_index("x"), lax.axis_index("y"), lax.axis_index("c")
    me = 4 * x_ + 2 * y_ + c_
    out = []
    for i in range(len(src_refs)):
        for k in range(1, N_DEV):
            peer, idx = _peer(k, x_, y_, c_)

            def copy(dst_slot, i=i, k=k, peer=peer, idx=idx):
                return pltpu.make_async_remote_copy(
                    src_ref=src_views[i](src_refs[i], idx), dst_ref=dst_views[i](land_refs[i], dst_slot),
                    send_sem=send_sems[i].at[k - 1], recv_sem=recv_sems[i].at[k - 1], device_id=peer,
                    device_id_type=MESH)

            out.append((copy(me), copy(idx)))
    return out


def _exchange_start(srcs, lands, src_views, dst_views, name):
    n = len(srcs)

    def body(*refs):
        src_refs, land_refs = refs[:n], refs[n:2 * n]
        send_sems, recv_sems = refs[2 * n:3 * n], refs[3 * n:4 * n]
        token = refs[-1]
        for send, _ in _exchange_copies(src_refs, land_refs, send_sems, recv_sems, src_views, dst_views):
            send.start()
        token[...] = jnp.zeros_like(token)

    sems = [pltpu.SemaphoreType.DMA((N_PEERS,))] * n
    thru = [pltpu.HBM(a.shape, a.dtype) for a in list(srcs) + list(lands)]
    outs = pl.pallas_call(
        body, name=name, out_shape=sems + sems + thru + [jax.ShapeDtypeStruct((8, LANES), F32)],
        in_specs=[HBM_SPEC] * (2 * n),
        out_specs=[SEM_SPEC] * (2 * n) + [HBM_SPEC] * (2 * n) + [pl.BlockSpec(memory_space=pltpu.VMEM)],
        input_output_aliases={j: 2 * n + j for j in range(2 * n)},
        compiler_params=pltpu.CompilerParams(has_side_effects=SIDE_EFFECT),
    )(*[pltpu.with_memory_space_constraint(a, pltpu.HBM) for a in list(srcs) + list(lands)])
    per_array = [(outs[j], outs[n + j], outs[2 * n + j], outs[3 * n + j]) for j in range(n)]
    return per_array, outs[-1]


def _exchange_wait(started, after, src_views, dst_views, name):
    send_sems, recv_sems, srcs, lands = (list(t) for t in zip(*started))
    n = len(srcs)

    def body(*refs):
        src_refs, land_refs = refs[:n], refs[n:2 * n]
        send_refs, recv_refs = refs[2 * n:3 * n], refs[3 * n:4 * n]
        copies = _exchange_copies(src_refs, land_refs, send_refs, recv_refs, src_views, dst_views)
        for send, _ in copies:
            send.wait_send()
        for _, recv in copies:
            recv.wait_recv()

    thru = [pltpu.HBM(a.shape, a.dtype) for a in list(srcs) + list(lands)]
    outs = pl.pallas_call(
        body, name=name, out_shape=thru,
        in_specs=[HBM_SPEC] * (2 * n) + [SEM_SPEC] * (2 * n) + [ANY],
        out_specs=[HBM_SPEC] * (2 * n),
        input_output_aliases={j: j for j in range(2 * n)},
        compiler_params=pltpu.CompilerParams(has_side_effects=SIDE_EFFECT),
    )(*srcs, *lands, *send_sems, *recv_sems, after)
    return outs[:n], outs[n:]


def _place_own(me, src, land, kind, name):
    r, c = src.shape
    tr = _pick(r, 512, 16)
    nt = r // tr
    if kind == "rows":
        out_spec = pl.BlockSpec((tr, c), lambda i, mr: (mr[0] * nt + i, 0))
    else:
        out_spec = pl.BlockSpec((tr, c), lambda i, mr: (i, mr[0]))

    def body(me_ref, s_ref, land_ref, o_ref):
        o_ref[...] = s_ref[...]

    return pl.pallas_call(
        body, name=name,
        grid_spec=pltpu.PrefetchScalarGridSpec(
            num_scalar_prefetch=1, grid=(nt,),
            in_specs=[pl.BlockSpec((tr, c), lambda i, mr: (i, 0)), ANY], out_specs=out_spec),
        out_shape=jax.ShapeDtypeStruct(land.shape, land.dtype),
        input_output_aliases={2: 0},
    )(me, src, land)


def _sum_slots_own(me, landed, part, kind, name):
    _, r, c = landed.shape
    tr = _pick(r, 512, 16)
    nt = r // tr
    if kind == "rows":
        part_spec = pl.BlockSpec((tr, c), lambda i, mr: (mr[0] * nt + i, 0))
    elif kind == "cols":
        part_spec = pl.BlockSpec((tr, c), lambda i, mr: (i, mr[0]))
    elif kind == "blk":
        part_spec = pl.BlockSpec((None, tr, c), lambda i, mr: (mr[0], i, 0))
    else:
        part_spec = pl.BlockSpec((tr, c), lambda i, mr: (i, 0))

    def body(me_ref, x_ref, p_ref, o_ref):
        own = p_ref[...].astype(F32)
        acc = jnp.zeros_like(own)
        for i in range(N_DEV):
            acc = acc + jnp.where(me_ref[0] == i, own, x_ref[i].astype(F32))
        o_ref[...] = acc

    return pl.pallas_call(
        body, name=name,
        grid_spec=pltpu.PrefetchScalarGridSpec(
            num_scalar_prefetch=1, grid=(nt,),
            in_specs=[pl.BlockSpec((N_DEV, tr, c), lambda i, mr: (0, i, 0)), part_spec],
            out_specs=pl.BlockSpec((tr, c), lambda i, mr: (i, 0))),
        out_shape=jax.ShapeDtypeStruct((r, c), F32),
        compiler_params=_params(2 * N_DEV * tr * c * landed.dtype.itemsize + 8 * tr * c * 4 + (4 << 20)),
    )(me, landed, part)


def _cols_from_blocks(blk, name):
    nd, nl, k, n = blk.shape
    tk = _pick(k, 256, 16)

    def body(b_ref, o_ref, wide_ref):
        for dev in range(nd):
            wide_ref[:, dev * n:(dev + 1) * n] = b_ref[dev].astype(F32)
        o_ref[...] = wide_ref[...].astype(o_ref.dtype)

    return pl.pallas_call(
        body, name=name, grid=(nl, k // tk),
        in_specs=[pl.BlockSpec((nd, None, tk, n), lambda l, i: (0, l, i, 0))],
        out_specs=pl.BlockSpec((None, tk, nd * n), lambda l, i: (l, i, 0)),
        out_shape=jax.ShapeDtypeStruct((nl, k, nd * n), BF16),
        scratch_shapes=[pltpu.VMEM((tk, nd * n), F32)],
    )(blk)


def _blocks_from_cols(full, name):
    k, n8 = full.shape
    n = n8 // N_DEV
    tk = _pick(k, 256, 16)

    def body(f_ref, o_ref):
        for dev in range(N_DEV):
            o_ref[dev] = f_ref[:, dev * n:(dev + 1) * n].astype(o_ref.dtype)

    return pl.pallas_call(
        body, name=name, grid=(k // tk,),
        in_specs=[pl.BlockSpec((tk, n8), lambda i: (i, 0))],
        out_specs=pl.BlockSpec((N_DEV, tk, n), lambda i: (0, i, 0)),
        out_shape=jax.ShapeDtypeStruct((N_DEV, k, n), BF16),
    )(full)


def _pack(arrs, dtype, cols):
    flat = jnp.concatenate([a.astype(dtype).reshape(-1) for a in arrs])
    unit = 16 * cols
    pad = (-flat.shape[0]) % unit
    flat = jnp.pad(flat, (0, pad))
    return flat.reshape(-1, cols)


def _unpack(flat, shapes):
    out, off = [], 0
    for s in shapes:
        n = math.prod(s)
        out.append(flat[off:off + n].reshape(s))
        off += n
    return out


def _ada_fwd(c_all, w_ada, b_shard, name):
    nl, d, n = w_ada.shape
    tn = _pick(n, 512, LANES)

    def body(c_ref, w_ref, b_ref, o_ref, act_ref):
        cv = c_ref[...]
        act = cv * jax.nn.sigmoid(cv)
        act_ref[...] = act
        o_ref[...] = jnp.dot(act.astype(BF16), w_ref[...].astype(BF16), preferred_element_type=F32) + b_ref[...]

    return pl.pallas_call(
        body, name=name, grid=(nl, n // tn),
        in_specs=[pl.BlockSpec(c_all.shape, lambda l, j: (0, 0)),
                  pl.BlockSpec((None, d, tn), lambda l, j: (l, 0, j)),
                  pl.BlockSpec((None, 1, tn), lambda l, j: (l, 0, j))],
        out_specs=[pl.BlockSpec((None, c_all.shape[0], tn), lambda l, j: (l, 0, j)),
                   pl.BlockSpec(c_all.shape, lambda l, j: (0, 0))],
        out_shape=[jax.ShapeDtypeStruct((nl, c_all.shape[0], n), F32), jax.ShapeDtypeStruct(c_all.shape, F32)],
        compiler_params=_params(2 * d * tn * 4 + d * tn * 2 + (8 << 20)),
    )(c_all, w_ada, b_shard)


def _ada_wgrad(act_t, dada, name):
    d, kp = act_t.shape
    nl, _, n = dada.shape
    tm = _pick(d, 512, 8)

    def body(a_ref, g_ref, o_ref):
        o_ref[...] = jnp.dot(a_ref[...].astype(BF16), g_ref[...].astype(BF16), preferred_element_type=F32)

    return pl.pallas_call(
        body, name=name, grid=(nl, d // tm),
        in_specs=[pl.BlockSpec((tm, kp), lambda l, i: (i, 0)), pl.BlockSpec((None, kp, n), lambda l, i: (l, 0, 0))],
        out_specs=pl.BlockSpec((None, tm, n), lambda l, i: (l, i, 0)),
        out_shape=jax.ShapeDtypeStruct((nl, d, n), F32),
        compiler_params=_params(4 * tm * n * 4 + 2 * kp * n * 4 + (8 << 20)),
    )(act_t, dada)


def _row_spec(tm, d):
    return pl.BlockSpec((tm, d), lambda i: (i, 0))


def _vec_spec(d):
    return pl.BlockSpec((1, d), lambda i: (0, 0))


def _modnorm_fwd(x, g, sc, sh, name):
    s, d = x.shape
    tm = _pick(s, 256, 16)

    def body(x_ref, g_ref, sc_ref, sh_ref, o_ref):
        xv = x_ref[...]
        o_ref[...] = ((xv * _rstd(xv)) * g_ref[...] * (1.0 + sc_ref[...]) + sh_ref[...]).astype(o_ref.dtype)

    return pl.pallas_call(
        body, name=name, grid=(s // tm,),
        in_specs=[_row_spec(tm, d), _vec_spec(d), _vec_spec(d), _vec_spec(d)], out_specs=_row_spec(tm, d),
        out_shape=jax.ShapeDtypeStruct((s, d), BF16),
    )(x, g, sc, sh)


def _modnorm_bwd(dh, x, g, sc, dres, name):
    s, d = x.shape
    tm = _pick(s, 256, 8)

    def body(dh_ref, x_ref, g_ref, sc_ref, dres_ref, dx_ref, dg_ref, dsc_ref, dsh_ref):
        @pl.when(pl.program_id(0) == 0)
        def _():
            dg_ref[...] = jnp.zeros_like(dg_ref)
            dsc_ref[...] = jnp.zeros_like(dsc_ref)
            dsh_ref[...] = jnp.zeros_like(dsh_ref)

        dh_, xv, gv = dh_ref[...], x_ref[...], g_ref[...]
        r = _rstd(xv)
        xhat = xv * r
        dn = dh_ * (1.0 + sc_ref[...])
        dsh_ref[...] += jnp.sum(dh_, axis=0, keepdims=True)
        dsc_ref[...] += jnp.sum(dh_ * (xhat * gv), axis=0, keepdims=True)
        dg_ref[...] += jnp.sum(dn * xhat, axis=0, keepdims=True)
        dx_ref[...] = _norm_bwd(dn * gv, xhat, r) + dres_ref[...]

    vec = jax.ShapeDtypeStruct((1, d), F32)
    return pl.pallas_call(
        body, name=name, grid=(s // tm,),
        in_specs=[_row_spec(tm, d), _row_spec(tm, d), _vec_spec(d), _vec_spec(d), _row_spec(tm, d)],
        out_specs=[_row_spec(tm, d), _vec_spec(d), _vec_spec(d), _vec_spec(d)],
        out_shape=[jax.ShapeDtypeStruct((s, d), F32), vec, vec, vec],
    )(dh, x, g, sc, dres)


def _resnorm_fwd(x, y, g, gt, name):
    s, d = x.shape
    tm = _pick(s, 256, 8)

    def body(x_ref, y_ref, g_ref, gt_ref, o_ref):
        yv = y_ref[...]
        o_ref[...] = x_ref[...] + (1.0 + gt_ref[...]) * ((yv * _rstd(yv)) * g_ref[...])

    return pl.pallas_call(
        body, name=name, grid=(s // tm,),
        in_specs=[_row_spec(tm, d), _row_spec(tm, d), _vec_spec(d), _vec_spec(d)], out_specs=_row_spec(tm, d),
        out_shape=jax.ShapeDtypeStruct((s, d), F32),
    )(x, y, g, gt)


def _resnorm_bwd(dxo, y, g, gt, name):
    s, d = y.shape
    tm = _pick(s, 256, 16)

    def body(dxo_ref, y_ref, g_ref, gt_ref, dy_ref, dg_ref, dgt_ref):
        @pl.when(pl.program_id(0) == 0)
        def _():
            dg_ref[...] = jnp.zeros_like(dg_ref)
            dgt_ref[...] = jnp.zeros_like(dgt_ref)

        dxo_, yv, gv = dxo_ref[...], y_ref[...], g_ref[...]
        r = _rstd(yv)
        yhat = yv * r
        dn = dxo_ * (1.0 + gt_ref[...])
        dgt_ref[...] += jnp.sum(dxo_ * (yhat * gv), axis=0, keepdims=True)
        dg_ref[...] += jnp.sum(dn * yhat, axis=0, keepdims=True)
        dy_ref[...] = _norm_bwd(dn * gv, yhat, r).astype(dy_ref.dtype)

    vec = jax.ShapeDtypeStruct((1, d), F32)
    return pl.pallas_call(
        body, name=name, grid=(s // tm,),
        in_specs=[_row_spec(tm, d), _row_spec(tm, d), _vec_spec(d), _vec_spec(d)],
        out_specs=[_row_spec(tm, d), _vec_spec(d), _vec_spec(d)],
        out_shape=[jax.ShapeDtypeStruct((s, d), BF16), vec, vec],
    )(dxo, y, g, gt)


def _loss_bwd(xf, tgt, name):
    s, d = xf.shape
    tm = _pick(s, 256, 8)

    def body(x_ref, t_ref, dy_ref, l_ref):
        @pl.when(pl.program_id(0) == 0)
        def _():
            l_ref[...] = jnp.zeros_like(l_ref)

        e = x_ref[...] - t_ref[...]
        dy_ref[...] = e * (1.0 / d)
        l_ref[...] += jnp.sum(e * e) * (0.5 / d)

    return pl.pallas_call(
        body, name=name, grid=(s // tm,),
        in_specs=[_row_spec(tm, d), _row_spec(tm, d)],
        out_specs=[_row_spec(tm, d), pl.BlockSpec((8, LANES), lambda i: (0, 0))],
        out_shape=[jax.ShapeDtypeStruct((s, d), F32), jax.ShapeDtypeStruct((8, LANES), F32)],
    )(xf, tgt)


def _attn_specs(n_q, n_kv):
    aw, kvd = n_q * HEAD_DIM, n_kv * HEAD_DIM
    assert aw % kvd == 0
    kcol = aw // kvd
    q = pl.BlockSpec((WINDOW, aw), lambda n: (n, 0))
    kc = pl.BlockSpec((WINDOW, kvd), lambda n: (n, kcol))
    kp = pl.BlockSpec((WINDOW, kvd), lambda n: (jnp.maximum(n - 1, 0), kcol))
    vc = pl.BlockSpec((WINDOW, kvd), lambda n: (n, kcol + 1))
    vp = pl.BlockSpec((WINDOW, kvd), lambda n: (jnp.maximum(n - 1, 0), kcol + 1))
    return [q, kc, kp, vc, vp]


def _band_mask(n, n_heads):
    qi = lax.broadcasted_iota(jnp.int32, (n_heads * WINDOW, 2 * WINDOW), 0) & (WINDOW - 1)
    kj = lax.broadcasted_iota(jnp.int32, (n_heads * WINDOW, 2 * WINDOW), 1)
    return (kj > qi) & (kj <= qi + WINDOW) & ((kj >= WINDOW) | (n > 0))


def _stack_heads(ref, heads):
    return jnp.concatenate([ref[:, h * HEAD_DIM:(h + 1) * HEAD_DIM] for h in heads], axis=0)


def _stack_sinks(ref, heads):
    return jnp.concatenate([jnp.broadcast_to(ref[:, h:h + 1], (WINDOW, 1)) for h in heads], axis=0)


_NT = (((1,), (1,)), ((), ()))
_TN = (((0,), (0,)), ((), ()))


def _attn_fwd(proj, sinks, *, n_q, n_kv, name):
    s = proj.shape[0]
    aw, grp = n_q * HEAD_DIM, n_q // n_kv

    def body(q_ref, kc_ref, kp_ref, vc_ref, vp_ref, sink_ref, o_ref, lse_ref):
        valid = _band_mask(pl.program_id(0), grp)
        kb = jnp.concatenate([kp_ref[...], kc_ref[...]], axis=0).astype(BF16)
        vb = jnp.concatenate([vp_ref[...], vc_ref[...]], axis=0).astype(BF16)
        lse_ref[...] = jnp.zeros_like(lse_ref)
        for g in range(n_kv):
            heads = range(g * grp, (g + 1) * grp)
            gs = slice(g * HEAD_DIM, (g + 1) * HEAD_DIM)
            qg = _stack_heads(q_ref, heads).astype(BF16)
            sink = _stack_sinks(sink_ref, heads)
            sc = lax.dot_general(qg, kb[:, gs], _NT, preferred_element_type=F32)
            sc = jnp.where(valid, sc * (HEAD_DIM ** -0.5), NEG)
            m = jnp.maximum(jnp.max(sc, axis=-1, keepdims=True), sink)
            e = jnp.exp(sc - m)
            den = jnp.sum(e, axis=-1, keepdims=True) + jnp.exp(sink - m)
            p = e * (1.0 / den)
            og = jnp.dot(p.astype(BF16), vb[:, gs], preferred_element_type=F32)
            lse = m + jnp.log(den)
            for i, h in enumerate(heads):
                rows = slice(i * WINDOW, (i + 1) * WINDOW)
                o_ref[:, h * HEAD_DIM:(h + 1) * HEAD_DIM] = og[rows]
                lse_ref[:, h:h + 1] = lse[rows]

    return pl.pallas_call(
        body, name=name, grid=(s // WINDOW,),
        in_specs=_attn_specs(n_q, n_kv) + [pl.BlockSpec((1, LANES), lambda n: (0, 0))],
        out_specs=[pl.BlockSpec((WINDOW, aw), lambda n: (n, 0)), pl.BlockSpec((WINDOW, LANES), lambda n: (n, 0))],
        out_shape=[jax.ShapeDtypeStruct((s, aw), F32), jax.ShapeDtypeStruct((s, LANES), F32)],
    )(proj, proj, proj, proj, proj, sinks)


def _attn_bwd(proj, sinks, out, lse, dout, *, n_q, n_kv, name):
    s = proj.shape[0]
    aw, kvd, grp = n_q * HEAD_DIM, n_kv * HEAD_DIM, n_q // n_kv
    scale = HEAD_DIM ** -0.5

    def body(q_ref, kc_ref, kp_ref, vc_ref, vp_ref, sink_ref, o_ref, lse_ref, do_ref,
             dq_ref, dk_ref, dv_ref, dsink_ref):
        n = pl.program_id(0)

        @pl.when(n == 0)
        def _():
            dk_ref[...] = jnp.zeros_like(dk_ref)
            dv_ref[...] = jnp.zeros_like(dv_ref)
            dsink_ref[...] = jnp.zeros_like(dsink_ref)

        valid = _band_mask(n, grp)
        kb = jnp.concatenate([kp_ref[...], kc_ref[...]], axis=0).astype(BF16)
        vb = jnp.concatenate([vp_ref[...], vc_ref[...]], axis=0).astype(BF16)
        lane = lax.broadcasted_iota(jnp.int32, (8, LANES), 1)
        dsink = jnp.zeros((8, LANES), F32)
        cur = pl.ds(pl.multiple_of(n * WINDOW, WINDOW), WINDOW)
        prev = pl.ds(pl.multiple_of(jnp.maximum(n - 1, 0) * WINDOW, WINDOW), WINDOW)
        for g in range(n_kv):
            heads = range(g * grp, (g + 1) * grp)
            gs = slice(g * HEAD_DIM, (g + 1) * HEAD_DIM)
            qg = _stack_heads(q_ref, heads).astype(BF16)
            do = _stack_heads(do_ref, heads)
            dob = do.astype(BF16)
            lse = jnp.concatenate([lse_ref[:, h:h + 1] for h in heads], axis=0)
            sc = lax.dot_general(qg, kb[:, gs], _NT, preferred_element_type=F32)
            sc = jnp.where(valid, sc * scale, NEG)
            p = jnp.exp(sc - lse)
            delta = jnp.sum(do * _stack_heads(o_ref, heads), axis=-1, keepdims=True)
            dp = lax.dot_general(dob, vb[:, gs], _NT, preferred_element_type=F32)
            ds = (p * (dp - delta) * scale).astype(BF16)
            dqg = jnp.dot(ds, kb[:, gs], preferred_element_type=F32)
            dkb = lax.dot_general(ds, qg, _TN, preferred_element_type=F32)
            dvb = lax.dot_general(p.astype(BF16), dob, _TN, preferred_element_type=F32)
            sink_term = jnp.exp(_stack_sinks(sink_ref, heads) - lse) * delta
            for i, h in enumerate(heads):
                rows = slice(i * WINDOW, (i + 1) * WINDOW)
                dq_ref[:, h * HEAD_DIM:(h + 1) * HEAD_DIM] = dqg[rows]
                dsink = dsink + jnp.where(lane == h, -jnp.sum(sink_term[rows]), 0.0)
            dk_ref[cur, gs] += dkb[WINDOW:]
            dv_ref[cur, gs] += dvb[WINDOW:]

            @pl.when(n > 0)
            def _():
                dk_ref[prev, gs] += dkb[:WINDOW]
                dv_ref[prev, gs] += dvb[:WINDOW]

        dsink_ref[...] += dsink

    blk = pl.BlockSpec((WINDOW, aw), lambda n: (n, 0))
    kv_full = pl.BlockSpec((s, kvd), lambda n: (0, 0))
    return pl.pallas_call(
        body, name=name, grid=(s // WINDOW,),
        in_specs=_attn_specs(n_q, n_kv) + [pl.BlockSpec((1, LANES), lambda n: (0, 0)), blk,
                                           pl.BlockSpec((WINDOW, LANES), lambda n: (n, 0)), blk],
        out_specs=[blk, kv_full, kv_full, pl.BlockSpec((8, LANES), lambda n: (0, 0))],
        out_shape=[jax.ShapeDtypeStruct((s, aw), F32), jax.ShapeDtypeStruct((s, kvd), F32),
                   jax.ShapeDtypeStruct((s, kvd), F32), jax.ShapeDtypeStruct((8, LANES), F32)],
    )(proj, proj, proj, proj, proj, sinks, out, lse, dout)


def _disc(lr, li, ls):
    dt = jnp.exp(ls)
    mag = jnp.exp(lr * dt)
    ang = li * dt
    ab_re, ab_im = mag * jnp.cos(ang), mag * jnp.sin(ang)
    den = lr * lr + li * li
    f_re = ((ab_re - 1.0) * lr + ab_im * li) / den
    f_im = (ab_im * lr - (ab_re - 1.0) * li) / den
    return ab_re, ab_im, f_re, f_im


POW_ROWS = 8
SUB = 8
TAB_ROWS = POW_ROWS + 2 * SUB


def _ssm_params_fwd(lr, li, ls, b_re, b_im, name):
    gp = lr.shape[1]
    h = b_re.shape[0]

    def body(lr_ref, li_ref, ls_ref, br_ref, bi_ref, bbr_ref, bbi_ref, tr_ref, ti_ref):
        ab_re, ab_im, f_re, f_im = _disc(lr_ref[...], li_ref[...], ls_ref[...])
        br, bi = br_ref[...], bi_ref[...]
        bbr_ref[...] = f_re * br - f_im * bi
        bbi_ref[...] = f_re * bi + f_im * br
        pr, pi = ab_re, ab_im
        for i in range(POW_ROWS):
            tr_ref[i:i + 1, :] = pr
            ti_ref[i:i + 1, :] = pi
            pr, pi = pr * pr - pi * pi, 2.0 * pr * pi
        pr, pi = ab_re, ab_im
        for r in range(SUB):
            for row in (POW_ROWS + r, POW_ROWS + 2 * SUB - 1 - r):
                tr_ref[row:row + 1, :] = pr
                ti_ref[row:row + 1, :] = pi
            pr, pi = pr * ab_re - pi * ab_im, pr * ab_im + pi * ab_re

    mat, tab = jax.ShapeDtypeStruct((h, gp), F32), jax.ShapeDtypeStruct((TAB_ROWS, gp), F32)
    return pl.pallas_call(body, name=name, out_shape=[mat, mat, tab, tab])(lr, li, ls, b_re, b_im)


def _ssm_params_bwd(lr, li, ls, b_re, b_im, dab_re, dab_im, dbb_re, dbb_im, seg, name):
    gp = lr.shape[1]
    h = b_re.shape[0]

    def body(lr_ref, li_ref, ls_ref, br_ref, bi_ref, dar_ref, dai_ref, dbbr_ref, dbbi_ref, seg_ref,
             dlr_ref, dli_ref, dls_ref, dbr_ref, dbi_ref):
        lr_, li_, ls_ = lr_ref[...], li_ref[...], ls_ref[...]
        (ab_re, ab_im, f_re, f_im), vjp = jax.vjp(_disc, lr_, li_, ls_)
        br, bi, dbbr, dbbi = br_ref[...], bi_ref[...], dbbr_ref[...], dbbi_ref[...]
        dbr_ref[...] = dbbr * f_re + dbbi * f_im
        dbi_ref[...] = dbbi * f_re - dbbr * f_im
        df_re = jnp.sum(dbbr * br + dbbi * bi, axis=0, keepdims=True)
        df_im = jnp.sum(dbbi * br - dbbr * bi, axis=0, keepdims=True)
        dlr, dli, dls = vjp((dar_ref[...], dai_ref[...], df_re, df_im))
        dlr_ref[...] = dlr
        dli_ref[...] = dli
        dls8 = jnp.broadcast_to(dls, (8, gp))
        dls_ref[...] = jnp.dot(dls8, seg_ref[...], preferred_element_type=F32, precision=lax.Precision.HIGHEST)

    vec, mat = jax.ShapeDtypeStruct((1, gp), F32), jax.ShapeDtypeStruct((h, gp), F32)
    return pl.pallas_call(body, name=name,
                          out_shape=[vec, vec, jax.ShapeDtypeStruct((8, seg.shape[1]), F32), mat, mat],
                          compiler_params=_params(24 << 20))(
        lr, li, ls, b_re, b_im, dab_re, dab_im, dbb_re, dbb_im, seg)


def _scan_bufs(t_len):
    hs = BLOCK_STATES
    return [pltpu.VMEM((hs // LANES, t_len, LANES), F32), pltpu.VMEM((hs // LANES, t_len, LANES), F32),
            pltpu.VMEM((t_len // SUB, hs), F32), pltpu.VMEM((t_len // SUB, hs), F32)]


def _scan(xr, xi, apow_ref, bufs, t_len, reverse):
    hs = BLOCK_STATES
    n_tiles = t_len // SUB
    sr_ref, si_ref, er_ref, ei_ref = bufs

    def doubling(xr, xi, n_rows, first_pow, within):
        row = lax.broadcasted_iota(jnp.int32, xr.shape, 0) & (within - 1)
        d = 1
        while d < within:
            i = first_pow + d.bit_length() - 1
            pr, pi = apow_ref[i:i + 1, :hs], apow_ref[i:i + 1, hs:]
            if reverse:
                pi, shift, keep = -pi, n_rows - d, row < within - d
            else:
                shift, keep = d, row >= d
            sr = jnp.where(keep, pltpu.roll(xr, shift, 0), 0.0)
            si = jnp.where(keep, pltpu.roll(xi, shift, 0), 0.0)
            xr, xi = xr + pr * sr - pi * si, xi + pr * si + pi * sr
            d *= 2
        return xr, xi

    shape3 = (n_tiles, SUB, hs)
    row = lax.broadcasted_iota(jnp.int32, shape3, 1)
    xr, xi = xr.reshape(shape3), xi.reshape(shape3)
    for i, d in enumerate((1, 2, 4)):
        pr, pi = apow_ref[i:i + 1, :hs], apow_ref[i:i + 1, hs:]
        if reverse:
            pi, shift, keep = -pi, SUB - d, row < SUB - d
        else:
            shift, keep = d, row >= d
        sr = jnp.where(keep, pltpu.roll(xr, shift, 1), 0.0)
        si = jnp.where(keep, pltpu.roll(xi, shift, 1), 0.0)
        xr, xi = xr + pr * sr - pi * si, xi + pr * si + pi * sr
    xr, xi = xr.reshape(t_len, hs), xi.reshape(t_len, hs)
    chunks = [slice(c * LANES, (c + 1) * LANES) for c in range(hs // LANES)]
    for c, lanes in enumerate(chunks):
        sr_ref[c] = xr[:, lanes]
        si_ref[c] = xi[:, lanes]
    edge = pl.ds(0 if reverse else SUB - 1, n_tiles, stride=SUB)
    tr, ti = doubling(jnp.concatenate([sr_ref[c, edge, :] for c in range(len(chunks))], axis=1),
                      jnp.concatenate([si_ref[c, edge, :] for c in range(len(chunks))], axis=1), n_tiles, 3, n_tiles)
    trow = lax.broadcasted_iota(jnp.int32, tr.shape, 0)
    if reverse:
        shift, keep = n_tiles - 1, trow < n_tiles - 1
    else:
        shift, keep = 1, trow >= 1
    er_ref[...] = jnp.where(keep, pltpu.roll(tr, shift, 0), 0.0)
    ei_ref[...] = jnp.where(keep, pltpu.roll(ti, shift, 0), 0.0)
    lin = POW_ROWS + SUB if reverse else POW_ROWS
    mr, mi = apow_ref[lin:lin + SUB, :hs], apow_ref[lin:lin + SUB, hs:]
    if reverse:
        mi = -mi
    for t in range(n_tiles):
        rows = slice(t * SUB, (t + 1) * SUB)
        er, ei = er_ref[t:t + 1, :], ei_ref[t:t + 1, :]
        add_r, add_i = mr * er - mi * ei, mr * ei + mi * er
        for c, lanes in enumerate(chunks):
            sr_ref[c, rows, :] += add_r[:, lanes]
            si_ref[c, rows, :] += add_i[:, lanes]
    return (jnp.concatenate([sr_ref[c] for c in range(len(chunks))], axis=1),
            jnp.concatenate([si_ref[c] for c in range(len(chunks))], axis=1))


def _ssm_chunk(s):
    t_len = _pick(s, 256, 8)
    assert t_len & (t_len - 1) == 0 and t_len <= 1 << POW_ROWS, t_len
    return t_len


def _fold_carry(br, bi, carry_ref, apow_ref, at_row, conj):
    hs = BLOCK_STATES
    cr, ci = carry_ref[0:1, :hs], carry_ref[0:1, hs:]
    ar, ai = apow_ref[0:1, :hs], apow_ref[0:1, hs:]
    if conj:
        ai = -ai
    here = lax.broadcasted_iota(jnp.int32, br.shape, 0) == at_row
    return jnp.where(here, br + (ar * cr - ai * ci), br), jnp.where(here, bi + (ar * ci + ai * cr), bi)


def _ssm_fwd(proj, ucol, bbd, ccat, dskip, apow, t_len, *, name):
    s = proj.shape[0]
    nb = bbd.shape[0]
    nc = s // t_len
    hs = BLOCK_STATES

    def body(u_ref, bbd_ref, ccat_ref, d_ref, apow_ref, y_ref, z_ref, xs_ref, carry_ref, *bufs):
        @pl.when(pl.program_id(1) == 0)
        def _():
            carry_ref[...] = jnp.zeros_like(carry_ref)

        xs_ref[...] = carry_ref[...]
        u = u_ref[...]
        bu = jnp.dot(u.astype(BF16), bbd_ref[...], preferred_element_type=F32)
        br, bi = _fold_carry(bu[:, :hs], bu[:, hs:], carry_ref, apow_ref, 0, False)
        xr, xi = _scan(br, bi, apow_ref, bufs, t_len, False)
        xcat = jnp.concatenate([xr, xi], axis=1)
        carry_ref[...] = jnp.broadcast_to(xcat[t_len - 1:t_len, :], carry_ref.shape)
        y = jnp.dot(xcat.astype(BF16), ccat_ref[...], preferred_element_type=F32) + d_ref[...] * u
        y_ref[...] = y
        z_ref[...] = _gelu(y).astype(z_ref.dtype)

    return pl.pallas_call(
        body, name=name, grid=(nb, nc),
        in_specs=[pl.BlockSpec((t_len, LANES), lambda j, n: (n, ucol + j)),
                  pl.BlockSpec((None, LANES, 2 * hs), lambda j, n: (j, 0, 0)),
                  pl.BlockSpec((None, 2 * hs, LANES), lambda j, n: (j, 0, 0)),
                  pl.BlockSpec((1, LANES), lambda j, n: (0, j)),
                  pl.BlockSpec((None, TAB_ROWS, 2 * hs), lambda j, n: (j, 0, 0))],
        out_specs=[pl.BlockSpec((t_len, LANES), lambda j, n: (n, j)),
                   pl.BlockSpec((t_len, LANES), lambda j, n: (n, j)),
                   pl.BlockSpec((None, None, 8, 2 * hs), lambda j, n: (j, n, 0, 0))],
        out_shape=[jax.ShapeDtypeStruct((s, nb * LANES), F32), jax.ShapeDtypeStruct((s, nb * LANES), BF16),
                   jax.ShapeDtypeStruct((nb, nc, 8, 2 * hs), F32)],
        scratch_shapes=[pltpu.VMEM((8, 2 * hs), F32)] + _scan_bufs(t_len),
        compiler_params=_params(40 << 20),
    )(proj, bbd, ccat, dskip, apow)


def _ssm_bwd(proj, ucol, y, dzd, dz2, xs, bbd, ccat, dskip, apow, t_len, *, name):
    s = proj.shape[0]
    nb = bbd.shape[0]
    nc = s // t_len
    hs = BLOCK_STATES

    def body(u_ref, y_ref, dzd_ref, dz2_ref, xs_ref, bbd_ref, ccat_ref, d_ref, apow_ref,
             du_ref, dbbd_ref, dccat_ref, dd_ref, da_ref, gcarry_ref, *bufs):
        @pl.when(pl.program_id(1) == 0)
        def _():
            gcarry_ref[...] = jnp.zeros_like(gcarry_ref)
            dbbd_ref[...] = jnp.zeros_like(dbbd_ref)
            dccat_ref[...] = jnp.zeros_like(dccat_ref)
            dd_ref[...] = jnp.zeros_like(dd_ref)
            da_ref[...] = jnp.zeros_like(da_ref)

        u = u_ref[...]
        ub = u.astype(BF16)
        dy = (dzd_ref[...] + dz2_ref[...]) * _gelu_grad(y_ref[...])
        dyb = dy.astype(BF16)
        bu = jnp.dot(ub, bbd_ref[...], preferred_element_type=F32)
        br, bi = _fold_carry(bu[:, :hs], bu[:, hs:], xs_ref, apow_ref, 0, False)
        xr, xi = _scan(br, bi, apow_ref, bufs[:4], t_len, False)
        sr, si = xs_ref[0:1, :hs], xs_ref[0:1, hs:]
        dxd = lax.dot_general(dyb, ccat_ref[...], _NT, preferred_element_type=F32)
        dr, di = _fold_carry(dxd[:, :hs], dxd[:, hs:], gcarry_ref, apow_ref, t_len - 1, True)
        gr, gi = _scan(dr, di, apow_ref, bufs[4:], t_len, True)
        gcat = jnp.concatenate([gr, gi], axis=1)
        gcarry_ref[...] = jnp.broadcast_to(gcat[0:1, :], gcarry_ref.shape)
        gb = gcat.astype(BF16)
        du_ref[...] = lax.dot_general(gb, bbd_ref[...], _NT, preferred_element_type=F32) + d_ref[...] * dy
        dbbd_ref[...] += lax.dot_general(ub, gb, _TN, preferred_element_type=F32)
        xb = jnp.concatenate([xr, xi], axis=1).astype(BF16)
        dccat_ref[...] += lax.dot_general(xb, dyb, _TN, preferred_element_type=F32)
        dd_ref[...] += jnp.sum(dy * u, axis=0, keepdims=True)
        first = lax.broadcasted_iota(jnp.int32, xr.shape, 0) == 0
        xpr = jnp.where(first, sr, pltpu.roll(xr, 1, 0))
        xpi = jnp.where(first, si, pltpu.roll(xi, 1, 0))
        dar = jnp.sum(gr * xpr + gi * xpi, axis=0, keepdims=True)
        dai = jnp.sum(gi * xpr - gr * xpi, axis=0, keepdims=True)
        da_ref[...] += jnp.concatenate([dar, dai], axis=1)

    def rows(j, n):
        return nc - 1 - n

    chunk = pl.BlockSpec((t_len, LANES), lambda j, n: (rows(j, n), j))
    return pl.pallas_call(
        body, name=name, grid=(nb, nc),
        in_specs=[pl.BlockSpec((t_len, LANES), lambda j, n: (rows(j, n), ucol + j)), chunk, chunk, chunk,
                  pl.BlockSpec((None, None, 8, 2 * hs), lambda j, n: (j, rows(j, n), 0, 0)),
                  pl.BlockSpec((None, LANES, 2 * hs), lambda j, n: (j, 0, 0)),
                  pl.BlockSpec((None, 2 * hs, LANES), lambda j, n: (j, 0, 0)),
                  pl.BlockSpec((1, LANES), lambda j, n: (0, j)),
                  pl.BlockSpec((None, TAB_ROWS, 2 * hs), lambda j, n: (j, 0, 0))],
        out_specs=[chunk,
                   pl.BlockSpec((None, LANES, 2 * hs), lambda j, n: (j, 0, 0)),
                   pl.BlockSpec((None, 2 * hs, LANES), lambda j, n: (j, 0, 0)),
                   pl.BlockSpec((1, LANES), lambda j, n: (0, j)),
                   pl.BlockSpec((None, 1, 2 * hs), lambda j, n: (j, 0, 0))],
        out_shape=[jax.ShapeDtypeStruct((s, nb * LANES), F32),
                   jax.ShapeDtypeStruct((nb, LANES, 2 * hs), F32),
                   jax.ShapeDtypeStruct((nb, 2 * hs, LANES), F32),
                   jax.ShapeDtypeStruct((1, nb * LANES), F32),
                   jax.ShapeDtypeStruct((nb, 1, 2 * hs), F32)],
        scratch_shapes=[pltpu.VMEM((8, 2 * hs), F32)] + _scan_bufs(t_len) + _scan_bufs(t_len),
        compiler_params=_params(48 << 20),
    )(proj, y, dzd, dz2, xs, bbd, ccat, dskip, apow)


def _to_blocks(a):
    g, p, k = a.shape
    nb = g // GROUPS_PER_BLOCK
    eye = jnp.eye(GROUPS_PER_BLOCK, dtype=a.dtype)
    a4 = a.reshape(nb, GROUPS_PER_BLOCK, p, k)
    out = jnp.einsum("ab,jbpk->jakbp", eye, a4)
    return out.reshape(nb, GROUPS_PER_BLOCK * k, GROUPS_PER_BLOCK * p)


def _from_blocks(d, p, k):
    nb = d.shape[0]
    d5 = d.reshape(nb, GROUPS_PER_BLOCK, k, GROUPS_PER_BLOCK, p)
    eye = jnp.eye(GROUPS_PER_BLOCK, dtype=bool)[None, :, None, :, None]
    diag = jnp.sum(jnp.where(eye, d5, 0.0), axis=1)
    return jnp.transpose(diag, (0, 2, 3, 1)).reshape(nb * GROUPS_PER_BLOCK, p, k)


def _merge_fwd(attn, y, gl, g_a, g_s, name):
    s, wa = attn.shape
    ws = y.shape[1]
    tm = _pick(s, 256, 16)

    def body(a_ref, y_ref, gl_ref, ga_ref, gs_ref, o_ref):
        av = a_ref[...]
        o_ref[:, :wa] = ((av * _rstd(av)) * ga_ref[...]).astype(o_ref.dtype)
        sv = _gelu(y_ref[...]) * jax.nn.sigmoid(gl_ref[...])
        o_ref[:, wa:] = ((sv * _rstd(sv)) * gs_ref[...]).astype(o_ref.dtype)

    return pl.pallas_call(
        body, name=name, grid=(s // tm,),
        in_specs=[_row_spec(tm, wa), _row_spec(tm, ws), _row_spec(tm, ws), _vec_spec(wa), _vec_spec(ws)],
        out_specs=_row_spec(tm, wa + ws), out_shape=jax.ShapeDtypeStruct((s, wa + ws), BF16),
    )(attn, y, gl, g_a, g_s)


def _merge_bwd(dmerged, attn, y, gl, g_a, g_s, name):
    s, wa = attn.shape
    ws = y.shape[1]
    tm = _pick(s, 256, 16)

    def body(dm_ref, a_ref, y_ref, gl_ref, ga_ref, gs_ref, da_ref, dgl_ref, dzd_ref, dga_ref, dgs_ref):
        @pl.when(pl.program_id(0) == 0)
        def _():
            dga_ref[...] = jnp.zeros_like(dga_ref)
            dgs_ref[...] = jnp.zeros_like(dgs_ref)

        dan, dsn = dm_ref[:, :wa], dm_ref[:, wa:]
        av = a_ref[...]
        ra = _rstd(av)
        ahat = av * ra
        dga_ref[...] += jnp.sum(dan * ahat, axis=0, keepdims=True)
        da_ref[...] = _norm_bwd(dan * ga_ref[...], ahat, ra)
        z = _gelu(y_ref[...])
        sig = jax.nn.sigmoid(gl_ref[...])
        sv = z * sig
        rs = _rstd(sv)
        shat = sv * rs
        dgs_ref[...] += jnp.sum(dsn * shat, axis=0, keepdims=True)
        dssm = _norm_bwd(dsn * gs_ref[...], shat, rs)
        dzd_ref[...] = dssm * sig
        dgl_ref[...] = (dssm * z * sig * (1.0 - sig)).astype(dgl_ref.dtype)

    return pl.pallas_call(
        body, name=name, grid=(s // tm,),
        in_specs=[_row_spec(tm, wa + ws), _row_spec(tm, wa), _row_spec(tm, ws), _row_spec(tm, ws),
                  _vec_spec(wa), _vec_spec(ws)],
        out_specs=[_row_spec(tm, wa), _row_spec(tm, ws), _row_spec(tm, ws), _vec_spec(wa), _vec_spec(ws)],
        out_shape=[jax.ShapeDtypeStruct((s, wa), F32), jax.ShapeDtypeStruct((s, ws), BF16),
                   jax.ShapeDtypeStruct((s, ws), F32), jax.ShapeDtypeStruct((1, wa), F32),
                   jax.ShapeDtypeStruct((1, ws), F32)],
    )(dmerged, attn, y, gl, g_a, g_s)


def _shift_down(main, halo, k):
    rolled = pltpu.roll(main, k, 0)
    row = lax.broadcasted_iota(jnp.int32, main.shape, 0)
    for r in range(k):
        rolled = jnp.where(row == r, halo[8 - k + r:8 - k + r + 1, :], rolled)
    return rolled


def _shift_up(main, halo, k):
    tm = main.shape[0]
    rolled = pltpu.roll(main, tm - k, 0)
    row = lax.broadcasted_iota(jnp.int32, main.shape, 0)
    for r in range(k):
        rolled = jnp.where(row == tm - k + r, halo[r:r + 1, :], rolled)
    return rolled


def _conv(main, halo, w_ref, b_ref):
    return (b_ref[...] + w_ref[0:1, :] * _shift_down(main, halo, 2) + w_ref[1:2, :] * _shift_down(main, halo, 1)
            + w_ref[2:3, :] * main)


def _gate_tiles(s, f):
    return _pick(s, 512, 16), _pick(f, 512, LANES)


def _gate_in_specs(tm, tn, nfb, order):
    hb = tm // 8
    ij = (lambda a, b: (b, a)) if order == "ji" else (lambda a, b: (a, b))

    def main(off):
        return pl.BlockSpec((tm, tn), lambda a, b: (ij(a, b)[0], ij(a, b)[1] + off))

    def halo(off):
        return pl.BlockSpec((8, tn), lambda a, b: (jnp.maximum(ij(a, b)[0] * hb - 1, 0), ij(a, b)[1] + off))

    def vec(rows, off):
        return pl.BlockSpec((rows, tn), lambda a, b: (0, ij(a, b)[1] + off))

    return [main(0), main(nfb), halo(0), halo(nfb), vec(3, 0), vec(3, nfb), vec(1, 0), vec(1, nfb)]


def _gate_fwd(up0, conv_w, conv_b, name):
    s, f2 = up0.shape
    f = f2 // 2
    tm, tn = _gate_tiles(s, f)
    nfb = f // tn

    def body(v_ref, g_ref, vh_ref, gh_ref, wv_ref, wg_ref, bv_ref, bg_ref, o_ref):
        top = pl.program_id(0) == 0
        vh = jnp.where(top, 0.0, vh_ref[...])
        gh = jnp.where(top, 0.0, gh_ref[...])
        val = _conv(v_ref[...], vh, wv_ref, bv_ref)
        gate = _conv(g_ref[...], gh, wg_ref, bg_ref)
        o_ref[...] = (_gelu(gate) * val).astype(o_ref.dtype)

    return pl.pallas_call(
        body, name=name, grid=(s // tm, nfb),
        in_specs=_gate_in_specs(tm, tn, nfb, "ij"), out_specs=pl.BlockSpec((tm, tn), lambda i, j: (i, j)),
        out_shape=jax.ShapeDtypeStruct((s, f), BF16),
        compiler_params=_params(24 * tm * tn * 4 + (4 << 20)),
    )(up0, up0, up0, up0, conv_w, conv_w, conv_b, conv_b)


def _gate_bwd(up0, conv_w, conv_b, da, name):
    s, f2 = up0.shape
    f = f2 // 2
    tm, tn = _gate_tiles(s, f)
    nfb = f // tn

    def body(v_ref, g_ref, vh_ref, gh_ref, wv_ref, wg_ref, bv_ref, bg_ref, da_ref, dup_ref, dcb_ref, dcw_ref):
        top = pl.program_id(1) == 0

        @pl.when(top)
        def _():
            dcb_ref[...] = jnp.zeros_like(dcb_ref)
            dcw_ref[...] = jnp.zeros_like(dcw_ref)

        halos = (jnp.where(top, 0.0, vh_ref[...]), jnp.where(top, 0.0, gh_ref[...]))
        mains = (v_ref[...], g_ref[...])
        val = _conv(mains[0], halos[0], wv_ref, bv_ref)
        gate = _conv(mains[1], halos[1], wg_ref, bg_ref)
        dav = da_ref[...]
        dups = (dav * _gelu(gate), dav * val * _gelu_grad(gate))
        for half in range(2):
            dup = dups[half]
            dup_ref[half] = dup
            dcb_ref[half] += jnp.sum(dup, axis=0, keepdims=True)
            dcw_ref[half, 0:1, :] += jnp.sum(dup * _shift_down(mains[half], halos[half], 2), axis=0, keepdims=True)
            dcw_ref[half, 1:2, :] += jnp.sum(dup * _shift_down(mains[half], halos[half], 1), axis=0, keepdims=True)
            dcw_ref[half, 2:3, :] += jnp.sum(dup * mains[half], axis=0, keepdims=True)

    return pl.pallas_call(
        body, name=name, grid=(nfb, s // tm),
        in_specs=_gate_in_specs(tm, tn, nfb, "ji") + [pl.BlockSpec((tm, tn), lambda j, i: (i, j))],
        out_specs=[pl.BlockSpec((2, tm, tn), lambda j, i: (0, i, j)),
                   pl.BlockSpec((2, 1, tn), lambda j, i: (0, 0, j)),
                   pl.BlockSpec((2, 3, tn), lambda j, i: (0, 0, j))],
        out_shape=[jax.ShapeDtypeStruct((2, s, f), F32), jax.ShapeDtypeStruct((2, 1, f), F32),
                   jax.ShapeDtypeStruct((2, 3, f), F32)],
        compiler_params=_params(40 * tm * tn * 4 + (4 << 20)),
    )(up0, up0, up0, up0, conv_w, conv_w, conv_b, conv_b, da)


def _conv_bwd(dup, conv_w, name):
    _, s, f = dup.shape
    tm, tn = _pick(s, 512, 16), _pick(f, 1536, LANES)
    nfb, ni, hb = f // tn, s // tm, tm // 8

    def body(d_ref, dh_ref, w_ref, o_ref):
        main = d_ref[...]
        halo = jnp.where(pl.program_id(1) == ni - 1, 0.0, dh_ref[...])
        o_ref[...] = (w_ref[2:3, :] * main + w_ref[1:2, :] * _shift_up(main, halo, 1)
                      + w_ref[0:1, :] * _shift_up(main, halo, 2)).astype(o_ref.dtype)

    return pl.pallas_call(
        body, name=name, grid=(2, ni, nfb),
        in_specs=[pl.BlockSpec((None, tm, tn), lambda h, i, j: (h, i, j)),
                  pl.BlockSpec((None, 8, tn), lambda h, i, j: (h, jnp.minimum((i + 1) * hb, s // 8 - 1), j)),
                  pl.BlockSpec((3, tn), lambda h, i, j: (0, h * nfb + j))],
        out_specs=pl.BlockSpec((tm, tn), lambda h, i, j: (i, h * nfb + j)),
        out_shape=jax.ShapeDtypeStruct((s, 2 * f), BF16),
        compiler_params=_params(12 * tm * tn * 4 + (4 << 20)),
    )(dup, dup, conv_w)


def _adamw(w, g, m, v, name):
    r, c = w.shape
    tr = _pick(r, max(8, (1 << 19) // max(c, 1) // 8 * 8), 8)
    c1, c2 = 1.0 / (1.0 - ADAM_B1 ** ADAM_STEP), 1.0 / (1.0 - ADAM_B2 ** ADAM_STEP)

    def body(w_ref, g_ref, m_ref, v_ref, d_ref, nm_ref, nv_ref):
        gv = g_ref[...]
        nm = ADAM_B1 * m_ref[...] + (1.0 - ADAM_B1) * gv
        nv = ADAM_B2 * v_ref[...] + (1.0 - ADAM_B2) * (gv * gv)
        nm_ref[...] = nm
        nv_ref[...] = nv
        d_ref[...] = -ADAM_LR * ((nm * c1) / (jnp.sqrt(nv * c2) + ADAM_EPS) + ADAM_WD * w_ref[...])

    spec = pl.BlockSpec((tr, c), lambda i: (i, 0))
    out = jax.ShapeDtypeStruct((r, c), F32)
    return pl.pallas_call(body, name=name, grid=(r // tr,), in_specs=[spec] * 4, out_specs=[spec] * 3,
                          out_shape=[out] * 3, compiler_params=_params(14 * tr * c * 4 + (4 << 20)))(w, g, m, v)


def _adamw_nd(w, g, m, v, name):
    shape = w.shape
    c = shape[-1]
    outs = _adamw(w.reshape(-1, c), g.reshape(-1, c), m.reshape(-1, c), v.reshape(-1, c), name)
    return [o.reshape(shape) for o in outs]


BIG = ("w_in", "w_glu", "w_out", "w_up", "w_down")
SMALL = ("b_ada", "g_pre_mix", "g_post_mix", "attn_sinks", "lam_re", "lam_im", "log_step", "ssm_b_re", "ssm_b_im",
         "ssm_c_re", "ssm_c_im", "ssm_d", "g_attn_out", "g_ssm_out", "g_pre_ffn", "g_post_ffn", "conv_b")
ORDER = ("w_ada", "b_ada", "g_pre_mix", "g_post_mix", "w_in", "attn_sinks", "lam_re", "lam_im", "log_step",
         "ssm_b_re", "ssm_b_im", "ssm_c_re", "ssm_c_im", "ssm_d", "w_glu", "g_attn_out", "g_ssm_out", "w_out",
         "g_pre_ffn", "g_post_ffn", "w_up", "conv_w", "conv_b", "w_down")
COL_SHARDED = ("w_in", "w_up")


def kernel(x, c, w_ada, b_ada, g_pre_mix, g_post_mix, w_in, attn_sinks, lam_re, lam_im, log_step, ssm_b_re, ssm_b_im, ssm_c_re, ssm_c_im, ssm_d, w_glu, g_attn_out, g_ssm_out, w_out, g_pre_ffn, g_post_ffn, w_up, conv_w, conv_b, w_down, loss_target, m_w_ada, m_b_ada, m_g_pre_mix, m_g_post_mix, m_w_in, m_attn_sinks, m_lam_re, m_lam_im, m_log_step, m_ssm_b_re, m_ssm_b_im, m_ssm_c_re, m_ssm_c_im, m_ssm_d, m_w_glu, m_g_attn_out, m_g_ssm_out, m_w_out, m_g_pre_ffn, m_g_post_ffn, m_w_up, m_conv_w, m_conv_b, m_w_down, v_w_ada, v_b_ada, v_g_pre_mix, v_g_post_mix, v_w_in, v_attn_sinks, v_lam_re, v_lam_im, v_log_step, v_ssm_b_re, v_ssm_b_im, v_ssm_c_re, v_ssm_c_im, v_ssm_d, v_w_glu, v_g_attn_out, v_g_ssm_out, v_w_out, v_g_pre_ffn, v_g_post_ffn, v_w_up, v_conv_w, v_conv_b, v_w_down):
    env = dict(locals())
    W = {n: env[n] for n in ORDER}
    M = {n: env["m_" + n] for n in ORDER}
    V = {n: env["v_" + n] for n in ORDER}

    depth = w_ada.shape[0]
    s, d = x.shape[1], x.shape[2]
    xs0 = x.reshape(s, d)
    tgt = loss_target.reshape(s, d)
    attn_w = d // 2
    ssm_w = d - attn_w
    in_cols = w_in.shape[2] * N_DEV
    kv_dim = (in_cols - attn_w - ssm_w) // 2
    n_q, n_kv = attn_w // HEAD_DIM, kv_dim // HEAD_DIM
    n_grp = ssm_w // SSM_GROUP
    nb = ssm_w // LANES
    f = w_down.shape[1] * N_DEV
    ucol = (attn_w + 2 * kv_dim) // LANES
    t_len = _ssm_chunk(s)
    me = 4 * lax.axis_index("x") + 2 * lax.axis_index("y") + lax.axis_index("c")

    def at_block(ref, idx):
        return ref.at[idx]

    def at_rows(n_rows):
        return lambda ref, idx: ref.at[:, pl.ds(pl.multiple_of(idx * n_rows, 8), n_rows), :]

    def at_cols(n_cols):
        return lambda ref, idx: ref.at[:, :, pl.ds(pl.multiple_of(idx * n_cols, LANES), n_cols)]

    first = _gather_multi([w_in.astype(BF16), conv_w, c],
                          [(N_DEV,) + w_in.shape, (N_DEV,) + conv_w.shape, (N_DEV,) + c.shape],
                          [at_block, at_block, at_block], "ag_first")
    full = {"w_in": _cols_from_blocks(first[0], "w_in_layout")}
    conv_w_full = jnp.transpose(first[1], (1, 2, 0, 3)).reshape(depth, 3, 2 * f)
    c_all = first[2].reshape(N_DEV, d)

    def at_rows2(n_rows):
        return lambda ref, idx: ref.at[pl.ds(pl.multiple_of(idx * n_rows, 8), n_rows), :]

    def at_cols2(n_cols):
        return lambda ref, idx: ref.at[:, pl.ds(pl.multiple_of(idx * n_cols, LANES), n_cols)]

    def whole(ref, idx):
        return ref

    later = [(n, l) for l in range(depth) for n in BIG[1:]]
    later_srcs = [W[n][l].astype(BF16) for n, l in later]
    later_views = [at_cols2(W[n].shape[2]) if n in COL_SHARDED else at_rows2(W[n].shape[1]) for n, _ in later]
    later_shapes = [(W[n].shape[1], N_DEV * W[n].shape[2]) if n in COL_SHARDED
                    else (N_DEV * W[n].shape[1], W[n].shape[2]) for n, _ in later]
    me_arr = me.astype(jnp.int32).reshape(1)
    lands = [_place_own(me_arr, src, lax.empty(shp, BF16), "cols" if n in COL_SHARDED else "rows", f"ag_own_{n}{l}")
             for (n, l), src, shp in zip(later, later_srcs, later_shapes)]
    ag_started, ag_token = _exchange_start(later_srcs, lands, [whole] * len(later), later_views, "ag_start")

    def weights_arrived(names, l, after, name):
        picks = [later.index((n, l)) for n in names]
        _, got = _exchange_wait([ag_started[i] for i in picks], after, [whole] * len(picks),
                                [later_views[i] for i in picks], name)
        return dict(zip(names, got))

    c_pad = jnp.pad(c_all, ((0, 16 - N_DEV), (0, 0)))
    n_ada = w_ada.shape[2]
    b_shard = lax.dynamic_slice_in_dim(b_ada, me * n_ada, n_ada, axis=1).reshape(depth, 1, n_ada)
    ada_part, c_act = _ada_fwd(c_pad, w_ada, b_shard, "ada_fwd")
    ada_all = _all_gather(ada_part.reshape(depth * 16, n_ada), "ag_ada").reshape(N_DEV, depth, 16, n_ada)
    ada_me = lax.dynamic_index_in_dim(ada_all, me, axis=2, keepdims=False)
    ada = jnp.transpose(ada_me, (1, 0, 2)).reshape(depth, 6, 1, d) + ag_token[0, 0]

    gp = n_grp * STATE

    def hgp(a):
        return jnp.transpose(a, (2, 0, 1)).reshape(SSM_GROUP, gp)

    ssm = []
    for l in range(depth):
        lr, li = lam_re[l].reshape(1, gp), lam_im[l].reshape(1, gp)
        ls = jnp.repeat(log_step[l], STATE).reshape(1, gp)
        br, bi = hgp(ssm_b_re[l]), hgp(ssm_b_im[l])
        bbr, bbi, tab_r, tab_i = _ssm_params_fwd(lr, li, ls, br, bi, f"ssm_params_fwd{l}")
        bb_re = jnp.transpose(bbr.reshape(SSM_GROUP, n_grp, STATE), (1, 2, 0))
        bb_im = jnp.transpose(bbi.reshape(SSM_GROUP, n_grp, STATE), (1, 2, 0))
        bbd = jnp.concatenate([_to_blocks(bb_re), _to_blocks(bb_im)], axis=2).astype(BF16)
        c_re_t = jnp.transpose(ssm_c_re[l], (0, 2, 1))
        c_im_t = jnp.transpose(ssm_c_im[l], (0, 2, 1))
        ccat = jnp.concatenate([jnp.transpose(_to_blocks(c_re_t), (0, 2, 1)),
                                -jnp.transpose(_to_blocks(c_im_t), (0, 2, 1))], axis=1).astype(BF16)

        def tab(t):
            return t.reshape(TAB_ROWS, nb, BLOCK_STATES)

        apow = jnp.transpose(jnp.concatenate([tab(tab_r), tab(tab_i)], axis=2), (1, 0, 2))
        ssm.append(dict(lr=lr, li=li, ls=ls, br=br, bi=bi, bbd=bbd, ccat=ccat, apow=apow,
                        dskip=ssm_d[l].reshape(1, ssm_w)))

    sinks_pad = jnp.pad(attn_sinks, ((0, 0), (0, LANES - n_q)))

    def vec(a):
        return a.reshape(1, -1)

    saved = []
    fw = [dict() for _ in range(depth)]
    xin = xs0
    for l in range(depth):
        sh_m, sc_m, gt_m, sh_f, sc_f, gt_f = (ada[l, i] for i in range(6))
        p = ssm[l]
        if l > 0:
            fw[l].update(weights_arrived(BIG[1:], l, xin, f"ag_wait_layer{l}"))
        h1 = _modnorm_fwd(xin, vec(g_pre_mix[l]), sc_m, sh_m, f"modnorm_mix_fwd{l}")
        proj = _matmul(h1, full["w_in"][l], name=f"mm_in{l}")
        attn, lse = _attn_fwd(proj, sinks_pad[l:l + 1], n_q=n_q, n_kv=n_kv, name=f"attn_fwd{l}")
        y, z, xstart = _ssm_fwd(proj, ucol, p["bbd"], p["ccat"], p["dskip"], p["apow"], t_len, name=f"ssm_fwd{l}")
        if l == 0:
            fw[0].update(weights_arrived(("w_glu", "w_out"), 0, z, "ag_wait_mix0"))
        gl = _matmul(z, fw[l]["w_glu"], name=f"mm_glu{l}")
        merged = _merge_fwd(attn, y, gl, vec(g_attn_out[l]), vec(g_ssm_out[l]), f"merge_fwd{l}")
        mix = _matmul(merged, fw[l]["w_out"], name=f"mm_out{l}")
        x2 = _resnorm_fwd(xin, mix, vec(g_post_mix[l]), gt_m, f"resnorm_mix_fwd{l}")
        h2 = _modnorm_fwd(x2, vec(g_pre_ffn[l]), sc_f, sh_f, f"modnorm_ffn_fwd{l}")
        if l == 0:
            fw[0].update(weights_arrived(("w_up",), 0, h2, "ag_wait_up0"))
        up0 = _matmul(h2, fw[l]["w_up"], name=f"mm_up{l}")
        cw, cb = conv_w_full[l], vec(conv_b[l])
        act = _gate_fwd(up0, cw, cb, f"gate_fwd{l}")
        if l == 0:
            fw[0].update(weights_arrived(("w_down",), 0, act, "ag_wait_down0"))
        ff = _matmul(act, fw[l]["w_down"], name=f"mm_down{l}")
        x3 = _resnorm_fwd(x2, ff, vec(g_post_ffn[l]), gt_f, f"resnorm_ffn_fwd{l}")
        saved.append(dict(xin=xin, h1=h1, proj=proj, attn=attn, lse=lse, y=y, z=z, xstart=xstart, gl=gl,
                          merged=merged, mix=mix, x2=x2, h2=h2, up0=up0, act=act, ff=ff))
        xin = x3

    dxo, loss_acc = _loss_bwd(xin, tgt, "loss")
    loss = lax.psum(loss_acc[0, 0], ("x", "y", "c"))

    grads = {n: [None] * depth for n in ORDER}
    dada = [None] * depth
    big_blocks = {n: [None] * depth for n in BIG}
    seg = jnp.pad(jnp.repeat(jnp.eye(n_grp, dtype=F32), STATE, axis=0), ((0, 0), (0, (-n_grp) % LANES)))

    def part_view(n):
        shp = W[n].shape
        if n == "w_in":
            return at_block, "blk"
        if n in COL_SHARDED:
            return at_cols2(shp[2]), "cols"
        return at_rows2(shp[1]), "rows"

    rs_groups = []

    def send_partials(items, name):
        parts = [big_blocks[n][l] for n, l in items]
        lands = [lax.empty((N_DEV,) + W[n].shape[1:], BF16) for n, _ in items]
        started, token = _exchange_start(parts, lands, [part_view(n)[0] for n, _ in items],
                                         [at_block] * len(items), name)
        rs_groups.append((items, started, name))
        return token[0, 0]

    order = jnp.zeros((), F32)
    for l in reversed(range(depth)):
        sh_m, sc_m, gt_m, sh_f, sc_f, gt_f = (ada[l, i] for i in range(6))
        gt_f = gt_f + order
        a, p = saved[l], ssm[l]
        cw, cb = conv_w_full[l], vec(conv_b[l])
        dff, dg, dgt_f = _resnorm_bwd(dxo, a["ff"], vec(g_post_ffn[l]), gt_f, f"resnorm_ffn_bwd{l}")
        grads["g_post_ffn"][l] = dg
        dact = _matmul(dff, fw[l]["w_down"], tb=True, name=f"mm_down_dx{l}")
        big_blocks["w_down"][l] = _matmul(a["act"], dff, ta=True, out_dtype=BF16, name=f"mm_down_dw{l}")
        dup, dcb, dcw = _gate_bwd(a["up0"], cw, cb, dact, f"gate_bwd{l}")
        grads["conv_b"][l] = dcb.reshape(1, 2 * f)
        grads["conv_w"][l] = jnp.transpose(dcw, (1, 0, 2)).reshape(3, 2 * f)
        dup0 = _conv_bwd(dup, cw, f"conv_bwd{l}")
        dh2 = _matmul(dup0, fw[l]["w_up"], tb=True, name=f"mm_up_dx{l}")
        big_blocks["w_up"][l] = _matmul(a["h2"], dup0, ta=True, out_dtype=BF16, name=f"mm_up_dw{l}")
        if l == 0:
            sc_f = sc_f + send_partials([("w_down", 0), ("w_up", 0)], "rs_start_ffn0")
        dx2, dg, dsc_f, dsh_f = _modnorm_bwd(dh2, a["x2"], vec(g_pre_ffn[l]), sc_f, dxo, f"modnorm_ffn_bwd{l}")
        grads["g_pre_ffn"][l] = dg
        dmix, dg, dgt_m = _resnorm_bwd(dx2, a["mix"], vec(g_post_mix[l]), gt_m, f"resnorm_mix_bwd{l}")
        grads["g_post_mix"][l] = dg
        dmerged = _matmul(dmix, fw[l]["w_out"], tb=True, name=f"mm_out_dx{l}")
        big_blocks["w_out"][l] = _matmul(a["merged"], dmix, ta=True, out_dtype=BF16, name=f"mm_out_dw{l}")
        dattn, dgl, dzd, dga, dgs = _merge_bwd(dmerged, a["attn"], a["y"], a["gl"], vec(g_attn_out[l]),
                                               vec(g_ssm_out[l]), f"merge_bwd{l}")
        grads["g_attn_out"][l], grads["g_ssm_out"][l] = dga, dgs
        dz2 = _matmul(dgl, fw[l]["w_glu"], tb=True, name=f"mm_glu_dx{l}")
        big_blocks["w_glu"][l] = _matmul(a["z"], dgl, ta=True, out_dtype=BF16, name=f"mm_glu_dw{l}")
        dskip = p["dskip"]
        if l == 0:
            dskip = dskip + send_partials([("w_out", 0), ("w_glu", 0)], "rs_start_mix0")
        du, dbbd, dccat, dd, da = _ssm_bwd(a["proj"], ucol, a["y"], dzd, dz2, a["xstart"], p["bbd"], p["ccat"],
                                           dskip, p["apow"], t_len, name=f"ssm_bwd{l}")
        grads["ssm_d"][l] = dd
        hs = BLOCK_STATES
        dbb_re = _from_blocks(dbbd[:, :, :hs], STATE, SSM_GROUP)
        dbb_im = _from_blocks(dbbd[:, :, hs:], STATE, SSM_GROUP)
        dccat_t = jnp.transpose(dccat, (0, 2, 1))
        grads["ssm_c_re"][l] = jnp.transpose(_from_blocks(dccat_t[:, :, :hs], STATE, SSM_GROUP), (0, 2, 1))
        grads["ssm_c_im"][l] = -jnp.transpose(_from_blocks(dccat_t[:, :, hs:], STATE, SSM_GROUP), (0, 2, 1))
        dab_re, dab_im = da[:, 0, :hs].reshape(1, gp), da[:, 0, hs:].reshape(1, gp)
        dlr, dli, dls, dbr, dbi = _ssm_params_bwd(p["lr"], p["li"], p["ls"], p["br"], p["bi"], dab_re, dab_im,
                                                  hgp(dbb_re), hgp(dbb_im), seg, f"ssm_params_bwd{l}")
        grads["lam_re"][l], grads["lam_im"][l], grads["log_step"][l] = dlr, dli, dls[0, :n_grp]
        grads["ssm_b_re"][l] = jnp.transpose(dbr.reshape(SSM_GROUP, n_grp, STATE), (1, 2, 0))
        grads["ssm_b_im"][l] = jnp.transpose(dbi.reshape(SSM_GROUP, n_grp, STATE), (1, 2, 0))
        dq, dk, dv, dsink = _attn_bwd(a["proj"], sinks_pad[l:l + 1], a["attn"], a["lse"], dattn,
                                      n_q=n_q, n_kv=n_kv, name=f"attn_bwd{l}")
        grads["attn_sinks"][l] = dsink[0, :n_q]
        dproj = jnp.concatenate([dq, dk, dv, du], axis=1).astype(BF16)
        dh1 = _matmul(dproj, full["w_in"][l], tb=True, name=f"mm_in_dx{l}")
        big_blocks["w_in"][l] = _blocks_from_cols(_matmul(a["h1"], dproj, ta=True, name=f"mm_in_dw{l}"),
                                                  f"w_in_grad_layout{l}")
        dxo, dg, dsc_m, dsh_m = _modnorm_bwd(dh1, a["xin"], vec(g_pre_mix[l]), sc_m, dx2, f"modnorm_mix_bwd{l}")
        grads["g_pre_mix"][l] = dg
        dada[l] = jnp.concatenate([dsh_m, dsc_m, dgt_m, dsh_f, dsc_f, dgt_f], axis=1)
        if l > 0:
            order = send_partials([(n, l) for n in reversed(BIG)], f"rs_start_layer{l}")
        else:
            send_partials([("w_in", 0)], "rs_start_in0")
    grad_x = dxo.reshape(x.shape)

    small_order = SMALL + ("conv_w",)
    small_shapes = {n: W[n].shape for n in SMALL}
    small_shapes["conv_w"] = (depth, 3, 2 * f)
    stacked = {"b_ada": jnp.stack(dada).reshape(depth, 6 * d)}
    for n in small_order[1:]:
        stacked[n] = jnp.stack([g.reshape(small_shapes[n][1:]) for g in grads[n]])
    spack = _pack([stacked[n] for n in small_order], F32, 1024)
    small_started, _ = _exchange_start([spack], [lax.empty((N_DEV,) + spack.shape, F32)], [whole], [at_block],
                                       "small_start")

    delta, new_m, new_v = {}, {}, {}
    layer_grads = {n: [None] * depth for n in BIG}
    grads_done = set()
    after = dxo
    for items, started, name in rs_groups:
        mine, landed = _exchange_wait(started, after, [part_view(n)[0] for n, _ in items], [at_block] * len(items),
                                      name.replace("start", "wait"))
        for (n, l), part, slots in zip(items, mine, landed):
            layer_grads[n][l] = _sum_slots_own(me_arr, slots, part, part_view(n)[1], f"rs_sum_{n}{l}")
        for n in BIG:
            if n not in grads_done and all(g is not None for g in layer_grads[n]):
                grads[n] = jnp.stack(layer_grads[n])
                delta[n], new_m[n], new_v[n] = _adamw_nd(W[n], grads[n], M[n], V[n], f"adamw_{n}")
                grads_done.add(n)
                after = delta[n]

    mine, landed = _exchange_wait(small_started, after, [whole], [at_block], "small_wait")
    ssum = _sum_slots_own(me_arr, landed[0], mine[0], "self", "sum_small").reshape(-1)
    for n, g in zip(small_order, _unpack(ssum, [small_shapes[n] for n in small_order])):
        grads[n] = g
    n_cw = conv_w.shape[2]
    grads["conv_w"] = lax.dynamic_slice_in_dim(grads["conv_w"], me * n_cw, n_cw, axis=2)
    n_dada = depth * 6 * d
    slot = lax.broadcasted_iota(jnp.int32, (N_DEV, n_dada), 0)
    dada_all = jnp.where(slot == me, mine[0].reshape(-1)[:n_dada][None],
                         landed[0].reshape(N_DEV, -1)[:, :n_dada]).reshape(N_DEV, depth, 6 * d)
    dada_shard = lax.dynamic_slice_in_dim(dada_all, me * n_ada, n_ada, axis=2)
    kp = LANES
    dada_pad = jnp.pad(jnp.transpose(dada_shard, (1, 0, 2)), ((0, 0), (0, kp - N_DEV), (0, 0)))
    act_t = jnp.pad(jnp.transpose(c_act[:N_DEV]), ((0, 0), (0, kp - N_DEV)))
    grads["w_ada"] = _ada_wgrad(act_t, dada_pad, "ada_wgrad")

    for n in ("w_ada", "conv_w"):
        delta[n], new_m[n], new_v[n] = _adamw_nd(W[n], grads[n], M[n], V[n], f"adamw_{n}")
    packs = [_pack([t[n] for n in SMALL], F32, 1024) for t in (W, grads, M, V)]
    outs = _adamw(*packs, "adamw_small")
    shapes = [W[n].shape for n in SMALL]
    for tgt_d, o in zip((delta, new_m, new_v), outs):
        for n, val in zip(SMALL, _unpack(o.reshape(-1), shapes)):
            tgt_d[n] = val

    return (loss, grad_x, *[grads[n] for n in ORDER], *[delta[n] for n in ORDER],
            *[new_m[n] for n in ORDER], *[new_v[n] for n in ORDER])
```

```python
import functools
import math

import jax
import jax.numpy as jnp
from jax import lax
from jax.experimental import pallas as pl
from jax.experimental.pallas import tpu as pltpu

F32 = jnp.float32
BF16 = jnp.bfloat16

N_DEV = 8
HEAD_DIM = 64
WINDOW = 128
SSM_GROUP = 16
STATE = 64
LANES = 128
GROUPS_PER_BLOCK = LANES // SSM_GROUP
BLOCK_STATES = GROUPS_PER_BLOCK * STATE
EPS = 1e-6
NEG = -1e30
ADAM_LR, ADAM_B1, ADAM_B2, ADAM_EPS, ADAM_WD, ADAM_STEP = 0.001, 0.9, 0.999, 1e-08, 0.01, 10
VMEM_BYTES_V7X = 64 * 1024 * 1024
GELU_C = math.sqrt(2.0 / math.pi)
MESH = pl.DeviceIdType.MESH
ANY = pl.BlockSpec(memory_space=pl.ANY)


def _pick(n, pref, align):
    t = (min(pref, n) // align) * align
    while t >= align:
        if n % t == 0:
            return t
        t -= align
    return n


def _params(vmem_bytes=None):
    if vmem_bytes is None:
        return pltpu.CompilerParams()
    return pltpu.CompilerParams(vmem_limit_bytes=int(min(vmem_bytes, VMEM_BYTES_V7X - (8 << 20))))


def _gelu_and_grad(x):
    x2 = x * x
    half_x = 0.5 * x
    th = jnp.tanh((GELU_C * x) * (1.0 + 0.044715 * x2))
    one_th = 1.0 + th
    grad = 0.5 * one_th + (half_x * (1.0 - th * th)) * (GELU_C + (3.0 * 0.044715 * GELU_C) * x2)
    return half_x * one_th, grad


def _gelu(x):
    return _gelu_and_grad(x)[0]


def _gelu_grad(x):
    return _gelu_and_grad(x)[1]


def _rstd(x):
    return lax.rsqrt(jnp.mean(x * x, axis=-1, keepdims=True) + EPS)


def _norm_bwd(dhat, xhat, r):
    return r * (dhat - xhat * jnp.mean(dhat * xhat, axis=-1, keepdims=True))


def _matmul(a, b, *, ta=False, tb=False, out_dtype=F32, name):
    (kdim, m) = a.shape if ta else a.shape[::-1]
    (n, k2) = b.shape if tb else b.shape[::-1]
    assert kdim == k2, (a.shape, b.shape, ta, tb)
    tm, tn, tk = _pick(m, 1024, LANES), _pick(n, 1024, LANES), _pick(kdim, 2048, LANES)
    nk = kdim // tk
    dn = (((0 if ta else 1,), (1 if tb else 0,)), ((), ()))

    def partial_product(a_ref, b_ref):
        return lax.dot_general(a_ref[...].astype(BF16), b_ref[...].astype(BF16), dn, preferred_element_type=F32)

    def body_one(a_ref, b_ref, o_ref):
        o_ref[...] = partial_product(a_ref, b_ref).astype(o_ref.dtype)

    def body_acc(a_ref, b_ref, o_ref, acc_ref):
        k = pl.program_id(2)

        @pl.when(k == 0)
        def _():
            acc_ref[...] = partial_product(a_ref, b_ref)

        @pl.when((k > 0) & (k < nk - 1))
        def _():
            acc_ref[...] += partial_product(a_ref, b_ref)

        @pl.when(k == nk - 1)
        def _():
            o_ref[...] = (acc_ref[...] + partial_product(a_ref, b_ref)).astype(o_ref.dtype)

    body = body_one if nk == 1 else body_acc
    a_spec = pl.BlockSpec((tk, tm), lambda i, j, k: (k, i)) if ta else pl.BlockSpec((tm, tk), lambda i, j, k: (i, k))
    b_spec = pl.BlockSpec((tn, tk), lambda i, j, k: (j, k)) if tb else pl.BlockSpec((tk, tn), lambda i, j, k: (k, j))
    vmem = (2 * (tm * tk * a.dtype.itemsize + tk * tn * b.dtype.itemsize) + tm * tn * 4
            + 2 * tm * tn * jnp.dtype(out_dtype).itemsize + 3 * tm * tn * 4 + (4 << 20))
    return pl.pallas_call(
        body, name=name, grid=(m // tm, n // tn, nk),
        in_specs=[a_spec, b_spec], out_specs=pl.BlockSpec((tm, tn), lambda i, j, k: (i, j)),
        out_shape=jax.ShapeDtypeStruct((m, n), out_dtype),
        scratch_shapes=[] if nk == 1 else [pltpu.VMEM((tm, tn), F32)],
        compiler_params=_params(vmem),
    )(a, b)


def _all_gather(x, name):
    def body(x_ref, out_ref, send_sems, recv_sems, local_sem):
        x_, y_, c_ = lax.axis_index("x"), lax.axis_index("y"), lax.axis_index("c")
        me, sibling = (x_, y_, c_), (x_, y_, 1 - c_)
        chips = [(1 - x_, y_), (x_, 1 - y_), (1 - x_, 1 - y_)]

        def slot(px, py, pc):
            return out_ref.at[4 * px + 2 * py + pc]

        def copy(k, block, to, src=None):
            return pltpu.make_async_remote_copy(
                src_ref=slot(*block) if src is None else src, dst_ref=slot(*block),
                send_sem=send_sems.at[k], recv_sem=recv_sems.at[k], device_id=to, device_id_type=MESH)

        mine = pltpu.make_async_copy(x_ref, slot(*me), local_sem)
        mine.start()
        first = [copy(0, me, sibling, src=x_ref)]
        first += [copy(1 + j, me, (*chip, c_), src=x_ref) for j, chip in enumerate(chips)]
        for cp in first:
            cp.start()
        passed = [copy(4 + j, (*chip, c_), sibling) for j, chip in enumerate(chips)]
        for j, chip in enumerate(chips):
            copy(1 + j, (*chip, c_), me).wait_recv()
            passed[j].start()
        copy(0, sibling, me).wait_recv()
        for j, chip in enumerate(chips):
            copy(4 + j, (*chip, 1 - c_), me).wait_recv()
        for cp in first + passed:
            cp.wait_send()
        mine.wait()

    return pl.pallas_call(
        body, name=name, out_shape=jax.ShapeDtypeStruct((N_DEV,) + x.shape, x.dtype),
        in_specs=[ANY], out_specs=ANY,
        scratch_shapes=[pltpu.SemaphoreType.DMA((7,)), pltpu.SemaphoreType.DMA((7,)), pltpu.SemaphoreType.DMA],
    )(x)


def _gather_multi(srcs, out_shapes, views, name):
    n = len(srcs)

    def body(*refs):
        src_refs, out_refs = refs[:n], refs[n:2 * n]
        send_sems, recv_sems, local_sems = refs[2 * n:]
        x_, y_, c_ = lax.axis_index("x"), lax.axis_index("y"), lax.axis_index("c")
        me, sibling = (x_, y_, c_), (x_, y_, 1 - c_)
        chips = [(1 - x_, y_), (x_, 1 - y_), (1 - x_, 1 - y_)]

        def slot(i, px, py, pc):
            return views[i](out_refs[i], 4 * px + 2 * py + pc)

        def copy(i, k, block, to, from_src=False):
            return pltpu.make_async_remote_copy(
                src_ref=src_refs[i] if from_src else slot(i, *block), dst_ref=slot(i, *block),
                send_sem=send_sems.at[7 * i + k], recv_sem=recv_sems.at[7 * i + k], device_id=to, device_id_type=MESH)

        mine = [pltpu.make_async_copy(src_refs[i], slot(i, *me), local_sems.at[i]) for i in range(n)]
        for cp in mine:
            cp.start()
        first = []
        for i in range(n):
            first.append(copy(i, 0, me, sibling, True))
            first += [copy(i, 1 + j, me, (*chip, c_), True) for j, chip in enumerate(chips)]
        for cp in first:
            cp.start()
        passed = []
        for j, chip in enumerate(chips):
            for i in range(n):
                copy(i, 1 + j, (*chip, c_), me).wait_recv()
                fwd = copy(i, 4 + j, (*chip, c_), sibling)
                fwd.start()
                passed.append(fwd)
        for i in range(n):
            copy(i, 0, sibling, me).wait_recv()
            for j, chip in enumerate(chips):
                copy(i, 4 + j, (*chip, 1 - c_), me).wait_recv()
        for cp in first + passed:
            cp.wait_send()
        for cp in mine:
            cp.wait()

    return pl.pallas_call(
        body, name=name, out_shape=[jax.ShapeDtypeStruct(s, a.dtype) for s, a in zip(out_shapes, srcs)],
        in_specs=[ANY] * n, out_specs=[ANY] * n,
        scratch_shapes=[pltpu.SemaphoreType.DMA((7 * n,)), pltpu.SemaphoreType.DMA((7 * n,)),
                        pltpu.SemaphoreType.DMA((n,))],
    )(*srcs)


HBM_SPEC = pl.BlockSpec(memory_space=pltpu.HBM)
SEM_SPEC = pl.BlockSpec(memory_space=pltpu.SEMAPHORE)
SIDE_EFFECT = pltpu.SideEffectType.DATAFLOW_SIDE_EFFECTING
N_PEERS = N_DEV - 1


def _peer(k, x_, y_, c_):
    px = 1 - x_ if (k >> 2) & 1 else x_
    py = 1 - y_ if (k >> 1) & 1 else y_
    pc = 1 - c_ if k & 1 else c_
    return (px, py, pc), 4 * px + 2 * py + pc


def _exchange_copies(src_refs, land_refs, send_sems, recv_sems, src_views, dst_views):
    x_, y_, c_ = lax.axis_index("x"), lax.axis_index("y"), lax.axis_index("c")
    me = 4 * x_ + 2 * y_ + c_
    out = []
    for i in range(len(src_refs)):
        for k in range(1, N_DEV):
            peer, idx = _peer(k, x_, y_, c_)

            def copy(dst_slot, i=i, k=k, peer=peer, idx=idx):
                return pltpu.make_async_remote_copy(
                    src_ref=src_views[i](src_refs[i], idx), dst_ref=dst_views[i](land_refs[i], dst_slot),
                    send_sem=send_sems[i].at[k - 1], recv_sem=recv_sems[i].at[k - 1], device_id=peer,
                    device_id_type=MESH)

            out.append((copy(me), copy(idx)))
    return out


def _exchange_start(srcs, lands, src_views, dst_views, name):
    n = len(srcs)

    def body(*refs):
        src_refs, land_refs = refs[:n], refs[n:2 * n]
        send_sems, recv_sems = refs[2 * n:3 * n], refs[3 * n:4 * n]
        token = refs[-1]
        for send, _ in _exchange_copies(src_refs, land_refs, send_sems, recv_sems, src_views, dst_views):
            send.start()
        token[...] = jnp.zeros_like(token)

    sems = [pltpu.SemaphoreType.DMA((N_PEERS,))] * n
    thru = [pltpu.HBM(a.shape, a.dtype) for a in list(srcs) + list(lands)]
    outs = pl.pallas_call(
        body, name=name, out_shape=sems + sems + thru + [jax.ShapeDtypeStruct((8, LANES), F32)],
        in_specs=[HBM_SPEC] * (2 * n),
        out_specs=[SEM_SPEC] * (2 * n) + [HBM_SPEC] * (2 * n) + [pl.BlockSpec(memory_space=pltpu.VMEM)],
        input_output_aliases={j: 2 * n + j for j in range(2 * n)},
        compiler_params=pltpu.CompilerParams(has_side_effects=SIDE_EFFECT),
    )(*[pltpu.with_memory_space_constraint(a, pltpu.HBM) for a in list(srcs) + list(lands)])
    per_array = [(outs[j], outs[n + j], outs[2 * n + j], outs[3 * n + j]) for j in range(n)]
    return per_array, outs[-1]


def _exchange_wait(started, after, src_views, dst_views, name):
    send_sems, recv_sems, srcs, lands = (list(t) for t in zip(*started))
    n = len(srcs)
    after = list(after)

    def body(*refs):
        src_refs, land_refs = refs[:n], refs[n:2 * n]
        send_refs, recv_refs = refs[2 * n:3 * n], refs[3 * n:4 * n]
        copies = _exchange_copies(src_refs, land_refs, send_refs, recv_refs, src_views, dst_views)
        for send, _ in copies:
            send.wait_send()
        for _, recv in copies:
            recv.wait_recv()

    thru = [pltpu.HBM(a.shape, a.dtype) for a in list(srcs) + list(lands)]
    outs = pl.pallas_call(
        body, name=name, out_shape=thru,
        in_specs=[HBM_SPEC] * (2 * n) + [SEM_SPEC] * (2 * n) + [ANY] * len(after),
        out_specs=[HBM_SPEC] * (2 * n),
        input_output_aliases={j: j for j in range(2 * n)},
        compiler_params=pltpu.CompilerParams(has_side_effects=SIDE_EFFECT),
    )(*srcs, *lands, *send_sems, *recv_sems, *after)
    return outs[:n], outs[n:]


def _place_own(me, src, land, kind, name):
    r, c = src.shape
    tr = _pick(r, 512, 16)
    nt = r // tr
    if kind == "rows":
        out_spec = pl.BlockSpec((tr, c), lambda i, mr: (mr[0] * nt + i, 0))
    elif kind == "cols":
        out_spec = pl.BlockSpec((tr, c), lambda i, mr: (i, mr[0]))
    else:
        out_spec = pl.BlockSpec((None, tr, c), lambda i, mr: (mr[0], i, 0))

    def body(me_ref, s_ref, land_ref, o_ref):
        o_ref[...] = s_ref[...]

    return pl.pallas_call(
        body, name=name,
        grid_spec=pltpu.PrefetchScalarGridSpec(
            num_scalar_prefetch=1, grid=(nt,),
            in_specs=[pl.BlockSpec((tr, c), lambda i, mr: (i, 0)), ANY], out_specs=out_spec),
        out_shape=jax.ShapeDtypeStruct(land.shape, land.dtype),
        input_output_aliases={2: 0},
    )(me, src, land)


def _sum_slots_own(me, landed, part, kind, name, *, layer=0, n_layers=1, stacked=None):
    _, r, c = landed.shape
    tr = _pick(r, 512, 16)
    nt = r // tr
    if kind == "rows":
        part_spec = pl.BlockSpec((tr, c), lambda i, mr: (mr[0] * nt + i, 0))
    elif kind == "cols":
        part_spec = pl.BlockSpec((tr, c), lambda i, mr: (i, mr[0]))
    elif kind == "blk":
        part_spec = pl.BlockSpec((None, tr, c), lambda i, mr: (mr[0], i, 0))
    else:
        part_spec = pl.BlockSpec((tr, c), lambda i, mr: (i, 0))

    def body(me_ref, x_ref, p_ref, *rest):
        o_ref = rest[-1]
        own = p_ref[...].astype(F32)
        acc = jnp.where(me_ref[0] == 0, own, x_ref[0].astype(F32))
        for i in range(1, N_DEV):
            acc = acc + jnp.where(me_ref[0] == i, own, x_ref[i].astype(F32))
        o_ref[...] = acc

    operands = [me, landed, part] + ([] if stacked is None else [stacked])
    return pl.pallas_call(
        body, name=name,
        grid_spec=pltpu.PrefetchScalarGridSpec(
            num_scalar_prefetch=1, grid=(nt,),
            in_specs=[pl.BlockSpec((N_DEV, tr, c), lambda i, mr: (0, i, 0)), part_spec]
            + ([] if stacked is None else [ANY]),
            out_specs=pl.BlockSpec((None, tr, c), lambda i, mr: (layer, i, 0))),
        out_shape=jax.ShapeDtypeStruct((n_layers, r, c), F32),
        input_output_aliases={} if stacked is None else {3: 0},
        compiler_params=_params(2 * N_DEV * tr * c * landed.dtype.itemsize + 8 * tr * c * 4 + (4 << 20)),
    )(*operands)


def _cols_from_blocks(blk, name):
    nd, nl, k, n = blk.shape
    tk = _pick(k, 256, 16)

    def body(b_ref, o_ref, wide_ref):
        for dev in range(nd):
            wide_ref[:, dev * n:(dev + 1) * n] = b_ref[dev].astype(F32)
        o_ref[...] = wide_ref[...].astype(o_ref.dtype)

    return pl.pallas_call(
        body, name=name, grid=(nl, k // tk),
        in_specs=[pl.BlockSpec((nd, None, tk, n), lambda l, i: (0, l, i, 0))],
        out_specs=pl.BlockSpec((None, tk, nd * n), lambda l, i: (l, i, 0)),
        out_shape=jax.ShapeDtypeStruct((nl, k, nd * n), BF16),
        scratch_shapes=[pltpu.VMEM((tk, nd * n), F32)],
    )(blk)


def _blocks_from_cols(full, name):
    k, n8 = full.shape
    n = n8 // N_DEV
    tk = _pick(k, 256, 16)

    def body(f_ref, o_ref):
        for dev in range(N_DEV):
            o_ref[dev] = f_ref[:, dev * n:(dev + 1) * n].astype(o_ref.dtype)

    return pl.pallas_call(
        body, name=name, grid=(k // tk,),
        in_specs=[pl.BlockSpec((tk, n8), lambda i: (i, 0))],
        out_specs=pl.BlockSpec((N_DEV, tk, n), lambda i: (0, i, 0)),
        out_shape=jax.ShapeDtypeStruct((N_DEV, k, n), BF16),
    )(full)


def _pack(arrs, dtype, cols):
    flat = jnp.concatenate([a.astype(dtype).reshape(-1) for a in arrs])
    unit = 16 * cols
    pad = (-flat.shape[0]) % unit
    flat = jnp.pad(flat, (0, pad))
    return flat.reshape(-1, cols)


def _unpack(flat, shapes):
    out, off = [], 0
    for s in shapes:
        n = math.prod(s)
        out.append(flat[off:off + n].reshape(s))
        off += n
    return out


def _ada_fwd(c_all, w_ada, b_shard, name):
    nl, d, n = w_ada.shape
    tn = _pick(n, 512, LANES)

    def body(c_ref, w_ref, b_ref, o_ref, act_ref):
        cv = c_ref[...]
        act = cv * jax.nn.sigmoid(cv)
        act_ref[...] = act
        o_ref[...] = jnp.dot(act.astype(BF16), w_ref[...].astype(BF16), preferred_element_type=F32) + b_ref[...]

    return pl.pallas_call(
        body, name=name, grid=(nl, n // tn),
        in_specs=[pl.BlockSpec(c_all.shape, lambda l, j: (0, 0)),
                  pl.BlockSpec((None, d, tn), lambda l, j: (l, 0, j)),
                  pl.BlockSpec((None, 1, tn), lambda l, j: (l, 0, j))],
        out_specs=[pl.BlockSpec((None, c_all.shape[0], tn), lambda l, j: (l, 0, j)),
                   pl.BlockSpec(c_all.shape, lambda l, j: (0, 0))],
        out_shape=[jax.ShapeDtypeStruct((nl, c_all.shape[0], n), F32), jax.ShapeDtypeStruct(c_all.shape, F32)],
        compiler_params=_params(2 * d * tn * 4 + d * tn * 2 + (8 << 20)),
    )(c_all, w_ada, b_shard)


def _ada_wgrad(act_t, dada, name):
    d, kp = act_t.shape
    nl, _, n = dada.shape
    tm = _pick(d, 512, 8)

    def body(a_ref, g_ref, o_ref):
        o_ref[...] = jnp.dot(a_ref[...].astype(BF16), g_ref[...].astype(BF16), preferred_element_type=F32)

    return pl.pallas_call(
        body, name=name, grid=(nl, d // tm),
        in_specs=[pl.BlockSpec((tm, kp), lambda l, i: (i, 0)), pl.BlockSpec((None, kp, n), lambda l, i: (l, 0, 0))],
        out_specs=pl.BlockSpec((None, tm, n), lambda l, i: (l, i, 0)),
        out_shape=jax.ShapeDtypeStruct((nl, d, n), F32),
        compiler_params=_params(4 * tm * n * 4 + 2 * kp * n * 4 + (8 << 20)),
    )(act_t, dada)


def _row_spec(tm, d):
    return pl.BlockSpec((tm, d), lambda i: (i, 0))


def _vec_spec(d):
    return pl.BlockSpec((1, d), lambda i: (0, 0))


def _modnorm_fwd(x, g, sc, sh, name):
    s, d = x.shape
    tm = _pick(s, 256, 16)

    def body(x_ref, g_ref, sc_ref, sh_ref, o_ref):
        xv = x_ref[...]
        o_ref[...] = ((xv * _rstd(xv)) * g_ref[...] * (1.0 + sc_ref[...]) + sh_ref[...]).astype(o_ref.dtype)

    return pl.pallas_call(
        body, name=name, grid=(s // tm,),
        in_specs=[_row_spec(tm, d), _vec_spec(d), _vec_spec(d), _vec_spec(d)], out_specs=_row_spec(tm, d),
        out_shape=jax.ShapeDtypeStruct((s, d), BF16),
    )(x, g, sc, sh)


def _modnorm_bwd(dh, x, g, sc, dres, name):
    s, d = x.shape
    tm = _pick(s, 256, 8)

    def body(dh_ref, x_ref, g_ref, sc_ref, dres_ref, dx_ref, dg_ref, dsc_ref, dsh_ref):
        @pl.when(pl.program_id(0) == 0)
        def _():
            dg_ref[...] = jnp.zeros_like(dg_ref)
            dsc_ref[...] = jnp.zeros_like(dsc_ref)
            dsh_ref[...] = jnp.zeros_like(dsh_ref)

        dh_, xv, gv = dh_ref[...], x_ref[...], g_ref[...]
        r = _rstd(xv)
        xhat = xv * r
        dn = dh_ * (1.0 + sc_ref[...])
        dsh_ref[...] += jnp.sum(dh_, axis=0, keepdims=True)
        dsc_ref[...] += jnp.sum(dh_ * (xhat * gv), axis=0, keepdims=True)
        dg_ref[...] += jnp.sum(dn * xhat, axis=0, keepdims=True)
        dx_ref[...] = _norm_bwd(dn * gv, xhat, r) + dres_ref[...]

    vec = jax.ShapeDtypeStruct((1, d), F32)
    return pl.pallas_call(
        body, name=name, grid=(s // tm,),
        in_specs=[_row_spec(tm, d), _row_spec(tm, d), _vec_spec(d), _vec_spec(d), _row_spec(tm, d)],
        out_specs=[_row_spec(tm, d), _vec_spec(d), _vec_spec(d), _vec_spec(d)],
        out_shape=[jax.ShapeDtypeStruct((s, d), F32), vec, vec, vec],
    )(dh, x, g, sc, dres)


def _resnorm_fwd(x, y, g, gt, name):
    s, d = x.shape
    tm = _pick(s, 256, 8)

    def body(x_ref, y_ref, g_ref, gt_ref, o_ref):
        yv = y_ref[...]
        o_ref[...] = x_ref[...] + (1.0 + gt_ref[...]) * ((yv * _rstd(yv)) * g_ref[...])

    return pl.pallas_call(
        body, name=name, grid=(s // tm,),
        in_specs=[_row_spec(tm, d), _row_spec(tm, d), _vec_spec(d), _vec_spec(d)], out_specs=_row_spec(tm, d),
        out_shape=jax.ShapeDtypeStruct((s, d), F32),
    )(x, y, g, gt)


def _resnorm_bwd(dxo, y, g, gt, name):
    s, d = y.shape
    tm = _pick(s, 256, 16)

    def body(dxo_ref, y_ref, g_ref, gt_ref, dy_ref, dg_ref, dgt_ref):
        @pl.when(pl.program_id(0) == 0)
        def _():
            dg_ref[...] = jnp.zeros_like(dg_ref)
            dgt_ref[...] = jnp.zeros_like(dgt_ref)

        dxo_, yv, gv = dxo_ref[...], y_ref[...], g_ref[...]
        r = _rstd(yv)
        yhat = yv * r
        dn = dxo_ * (1.0 + gt_ref[...])
        dgt_ref[...] += jnp.sum(dxo_ * (yhat * gv), axis=0, keepdims=True)
        dg_ref[...] += jnp.sum(dn * yhat, axis=0, keepdims=True)
        dy_ref[...] = _norm_bwd(dn * gv, yhat, r).astype(dy_ref.dtype)

    vec = jax.ShapeDtypeStruct((1, d), F32)
    return pl.pallas_call(
        body, name=name, grid=(s // tm,),
        in_specs=[_row_spec(tm, d), _row_spec(tm, d), _vec_spec(d), _vec_spec(d)],
        out_specs=[_row_spec(tm, d), _vec_spec(d), _vec_spec(d)],
        out_shape=[jax.ShapeDtypeStruct((s, d), BF16), vec, vec],
    )(dxo, y, g, gt)


def _loss_bwd(xf, tgt, name):
    s, d = xf.shape
    tm = _pick(s, 256, 8)

    def body(x_ref, t_ref, dy_ref, l_ref):
        @pl.when(pl.program_id(0) == 0)
        def _():
            l_ref[...] = jnp.zeros_like(l_ref)

        e = x_ref[...] - t_ref[...]
        dy_ref[...] = e * (1.0 / d)
        l_ref[...] += jnp.sum(e * e) * (0.5 / d)

    return pl.pallas_call(
        body, name=name, grid=(s // tm,),
        in_specs=[_row_spec(tm, d), _row_spec(tm, d)],
        out_specs=[_row_spec(tm, d), pl.BlockSpec((8, LANES), lambda i: (0, 0))],
        out_shape=[jax.ShapeDtypeStruct((s, d), F32), jax.ShapeDtypeStruct((8, LANES), F32)],
    )(xf, tgt)


def _attn_specs(n_q, n_kv):
    aw, kvd = n_q * HEAD_DIM, n_kv * HEAD_DIM
    assert aw % kvd == 0
    kcol = aw // kvd
    q = pl.BlockSpec((WINDOW, aw), lambda n: (n, 0))
    kc = pl.BlockSpec((WINDOW, kvd), lambda n: (n, kcol))
    kp = pl.BlockSpec((WINDOW, kvd), lambda n: (jnp.maximum(n - 1, 0), kcol))
    vc = pl.BlockSpec((WINDOW, kvd), lambda n: (n, kcol + 1))
    vp = pl.BlockSpec((WINDOW, kvd), lambda n: (jnp.maximum(n - 1, 0), kcol + 1))
    return [q, kc, kp, vc, vp]


def _band_mask(n, n_heads):
    qi = lax.broadcasted_iota(jnp.int32, (n_heads * WINDOW, 2 * WINDOW), 0) & (WINDOW - 1)
    kj = lax.broadcasted_iota(jnp.int32, (n_heads * WINDOW, 2 * WINDOW), 1)
    return (kj > qi) & (kj <= qi + WINDOW) & ((kj >= WINDOW) | (n > 0))


def _stack_heads(ref, heads):
    return jnp.concatenate([ref[:, h * HEAD_DIM:(h + 1) * HEAD_DIM] for h in heads], axis=0)


def _stack_sinks(ref, heads):
    return jnp.concatenate([jnp.broadcast_to(ref[:, h:h + 1], (WINDOW, 1)) for h in heads], axis=0)


_NT = (((1,), (1,)), ((), ()))
_TN = (((0,), (0,)), ((), ()))


def _attn_fwd(proj, sinks, *, n_q, n_kv, name):
    s = proj.shape[0]
    aw, grp = n_q * HEAD_DIM, n_q // n_kv

    def body(q_ref, kc_ref, kp_ref, vc_ref, vp_ref, sink_ref, o_ref, lse_ref):
        valid = _band_mask(pl.program_id(0), grp)
        kb = jnp.concatenate([kp_ref[...], kc_ref[...]], axis=0).astype(BF16)
        vb = jnp.concatenate([vp_ref[...], vc_ref[...]], axis=0).astype(BF16)
        lse_ref[...] = jnp.zeros_like(lse_ref)
        for g in range(n_kv):
            heads = range(g * grp, (g + 1) * grp)
            gs = slice(g * HEAD_DIM, (g + 1) * HEAD_DIM)
            qg = _stack_heads(q_ref, heads).astype(BF16)
            sink = _stack_sinks(sink_ref, heads)
            sc = lax.dot_general(qg, kb[:, gs], _NT, preferred_element_type=F32)
            sc = jnp.where(valid, sc * (HEAD_DIM ** -0.5), NEG)
            m = jnp.maximum(jnp.max(sc, axis=-1, keepdims=True), sink)
            e = jnp.exp(sc - m)
            den = jnp.sum(e, axis=-1, keepdims=True) + jnp.exp(sink - m)
            p = e * (1.0 / den)
            og = jnp.dot(p.astype(BF16), vb[:, gs], preferred_element_type=F32)
            lse = m + jnp.log(den)
            for i, h in enumerate(heads):
                rows = slice(i * WINDOW, (i + 1) * WINDOW)
                o_ref[:, h * HEAD_DIM:(h + 1) * HEAD_DIM] = og[rows]
                lse_ref[:, h:h + 1] = lse[rows]

    return pl.pallas_call(
        body, name=name, grid=(s // WINDOW,),
        in_specs=_attn_specs(n_q, n_kv) + [pl.BlockSpec((1, LANES), lambda n: (0, 0))],
        out_specs=[pl.BlockSpec((WINDOW, aw), lambda n: (n, 0)), pl.BlockSpec((WINDOW, LANES), lambda n: (n, 0))],
        out_shape=[jax.ShapeDtypeStruct((s, aw), F32), jax.ShapeDtypeStruct((s, LANES), F32)],
    )(proj, proj, proj, proj, proj, sinks)


def _attn_bwd(proj, sinks, out, lse, dout, *, n_q, n_kv, name):
    s = proj.shape[0]
    aw, kvd, grp = n_q * HEAD_DIM, n_kv * HEAD_DIM, n_q // n_kv
    scale = HEAD_DIM ** -0.5

    def body(q_ref, kc_ref, kp_ref, vc_ref, vp_ref, sink_ref, o_ref, lse_ref, do_ref,
             dq_ref, dk_ref, dv_ref, dsink_ref):
        n = pl.program_id(0)

        @pl.when(n == 0)
        def _():
            dk_ref[...] = jnp.zeros_like(dk_ref)
            dv_ref[...] = jnp.zeros_like(dv_ref)
            dsink_ref[...] = jnp.zeros_like(dsink_ref)

        valid = _band_mask(n, grp)
        kb = jnp.concatenate([kp_ref[...], kc_ref[...]], axis=0).astype(BF16)
        vb = jnp.concatenate([vp_ref[...], vc_ref[...]], axis=0).astype(BF16)
        lane = lax.broadcasted_iota(jnp.int32, (8, LANES), 1)
        dsink = jnp.zeros((8, LANES), F32)
        cur = pl.ds(pl.multiple_of(n * WINDOW, WINDOW), WINDOW)
        prev = pl.ds(pl.multiple_of(jnp.maximum(n - 1, 0) * WINDOW, WINDOW), WINDOW)
        for g in range(n_kv):
            heads = range(g * grp, (g + 1) * grp)
            gs = slice(g * HEAD_DIM, (g + 1) * HEAD_DIM)
            qg = _stack_heads(q_ref, heads).astype(BF16)
            do = _stack_heads(do_ref, heads)
            dob = do.astype(BF16)
            lse = jnp.concatenate([lse_ref[:, h:h + 1] for h in heads], axis=0)
            sc = lax.dot_general(qg, kb[:, gs], _NT, preferred_element_type=F32)
            sc = jnp.where(valid, sc * scale, NEG)
            p = jnp.exp(sc - lse)
            delta = jnp.sum(do * _stack_heads(o_ref, heads), axis=-1, keepdims=True)
            dp = lax.dot_general(dob, vb[:, gs], _NT, preferred_element_type=F32)
            ds = (p * (dp - delta) * scale).astype(BF16)
            dqg = jnp.dot(ds, kb[:, gs], preferred_element_type=F32)
            dkb = lax.dot_general(ds, qg, _TN, preferred_element_type=F32)
            dvb = lax.dot_general(p.astype(BF16), dob, _TN, preferred_element_type=F32)
            sink_term = jnp.exp(_stack_sinks(sink_ref, heads) - lse) * delta
            for i, h in enumerate(heads):
                rows = slice(i * WINDOW, (i + 1) * WINDOW)
                dq_ref[:, h * HEAD_DIM:(h + 1) * HEAD_DIM] = dqg[rows]
                dsink = dsink + jnp.where(lane == h, -jnp.sum(sink_term[rows]), 0.0)
            dk_ref[cur, gs] += dkb[WINDOW:]
            dv_ref[cur, gs] += dvb[WINDOW:]

            @pl.when(n > 0)
            def _():
                dk_ref[prev, gs] += dkb[:WINDOW]
                dv_ref[prev, gs] += dvb[:WINDOW]

        dsink_ref[...] += dsink

    blk = pl.BlockSpec((WINDOW, aw), lambda n: (n, 0))
    kv_full = pl.BlockSpec((s, kvd), lambda n: (0, 0))
    return pl.pallas_call(
        body, name=name, grid=(s // WINDOW,),
        in_specs=_attn_specs(n_q, n_kv) + [pl.BlockSpec((1, LANES), lambda n: (0, 0)), blk,
                                           pl.BlockSpec((WINDOW, LANES), lambda n: (n, 0)), blk],
        out_specs=[blk, kv_full, kv_full, pl.BlockSpec((8, LANES), lambda n: (0, 0))],
        out_shape=[jax.ShapeDtypeStruct((s, aw), F32), jax.ShapeDtypeStruct((s, kvd), F32),
                   jax.ShapeDtypeStruct((s, kvd), F32), jax.ShapeDtypeStruct((8, LANES), F32)],
    )(proj, proj, proj, proj, proj, sinks, out, lse, dout)


def _disc(lr, li, ls):
    dt = jnp.exp(ls)
    mag = jnp.exp(lr * dt)
    ang = li * dt
    ab_re, ab_im = mag * jnp.cos(ang), mag * jnp.sin(ang)
    den = lr * lr + li * li
    f_re = ((ab_re - 1.0) * lr + ab_im * li) / den
    f_im = (ab_im * lr - (ab_re - 1.0) * li) / den
    return ab_re, ab_im, f_re, f_im


POW_ROWS = 8
SUB = 8
TAB_ROWS = POW_ROWS + 2 * SUB


def _ssm_params_fwd(lr, li, ls, b_re, b_im, name):
    gp = lr.shape[1]
    h = b_re.shape[0]

    def body(lr_ref, li_ref, ls_ref, br_ref, bi_ref, bbr_ref, bbi_ref, tr_ref, ti_ref):
        ab_re, ab_im, f_re, f_im = _disc(lr_ref[...], li_ref[...], ls_ref[...])
        br, bi = br_ref[...], bi_ref[...]
        bbr_ref[...] = f_re * br - f_im * bi
        bbi_ref[...] = f_re * bi + f_im * br
        pr, pi = ab_re, ab_im
        for i in range(POW_ROWS):
            tr_ref[i:i + 1, :] = pr
            ti_ref[i:i + 1, :] = pi
            pr, pi = pr * pr - pi * pi, 2.0 * pr * pi
        pr, pi = ab_re, ab_im
        for r in range(SUB):
            for row in (POW_ROWS + r, POW_ROWS + 2 * SUB - 1 - r):
                tr_ref[row:row + 1, :] = pr
                ti_ref[row:row + 1, :] = pi
            pr, pi = pr * ab_re - pi * ab_im, pr * ab_im + pi * ab_re

    mat, tab = jax.ShapeDtypeStruct((h, gp), F32), jax.ShapeDtypeStruct((TAB_ROWS, gp), F32)
    return pl.pallas_call(body, name=name, out_shape=[mat, mat, tab, tab])(lr, li, ls, b_re, b_im)


def _ssm_params_bwd(lr, li, ls, b_re, b_im, dab_re, dab_im, dbb_re, dbb_im, seg, name):
    gp = lr.shape[1]
    h = b_re.shape[0]

    def body(lr_ref, li_ref, ls_ref, br_ref, bi_ref, dar_ref, dai_ref, dbbr_ref, dbbi_ref, seg_ref,
             dlr_ref, dli_ref, dls_ref, dbr_ref, dbi_ref):
        lr_, li_, ls_ = lr_ref[...], li_ref[...], ls_ref[...]
        (ab_re, ab_im, f_re, f_im), vjp = jax.vjp(_disc, lr_, li_, ls_)
        br, bi, dbbr, dbbi = br_ref[...], bi_ref[...], dbbr_ref[...], dbbi_ref[...]
        dbr_ref[...] = dbbr * f_re + dbbi * f_im
        dbi_ref[...] = dbbi * f_re - dbbr * f_im
        df_re = jnp.sum(dbbr * br + dbbi * bi, axis=0, keepdims=True)
        df_im = jnp.sum(dbbi * br - dbbr * bi, axis=0, keepdims=True)
        dlr, dli, dls = vjp((dar_ref[...], dai_ref[...], df_re, df_im))
        dlr_ref[...] = dlr
        dli_ref[...] = dli
        dls8 = jnp.broadcast_to(dls, (8, gp))
        dls_ref[...] = jnp.dot(dls8, seg_ref[...], preferred_element_type=F32, precision=lax.Precision.HIGHEST)

    vec, mat = jax.ShapeDtypeStruct((1, gp), F32), jax.ShapeDtypeStruct((h, gp), F32)
    return pl.pallas_call(body, name=name,
                          out_shape=[vec, vec, jax.ShapeDtypeStruct((8, seg.shape[1]), F32), mat, mat],
                          compiler_params=_params(24 << 20))(
        lr, li, ls, b_re, b_im, dab_re, dab_im, dbb_re, dbb_im, seg)


def _scan_bufs(t_len):
    hs = BLOCK_STATES
    return [pltpu.VMEM((hs // LANES, t_len, LANES), F32), pltpu.VMEM((hs // LANES, t_len, LANES), F32),
            pltpu.VMEM((t_len // SUB, hs), F32), pltpu.VMEM((t_len // SUB, hs), F32)]


def _scan(xr, xi, apow_ref, bufs, t_len, reverse):
    hs = BLOCK_STATES
    n_tiles = t_len // SUB
    sr_ref, si_ref, er_ref, ei_ref = bufs

    def doubling(xr, xi, n_rows, first_pow, within):
        row = lax.broadcasted_iota(jnp.int32, xr.shape, 0) & (within - 1)
        d = 1
        while d < within:
            i = first_pow + d.bit_length() - 1
            pr, pi = apow_ref[i:i + 1, :hs], apow_ref[i:i + 1, hs:]
            if reverse:
                pi, shift, keep = -pi, n_rows - d, row < within - d
            else:
                shift, keep = d, row >= d
            sr = jnp.where(keep, pltpu.roll(xr, shift, 0), 0.0)
            si = jnp.where(keep, pltpu.roll(xi, shift, 0), 0.0)
            xr, xi = xr + pr * sr - pi * si, xi + pr * si + pi * sr
            d *= 2
        return xr, xi

    shape3 = (n_tiles, SUB, hs)
    row = lax.broadcasted_iota(jnp.int32, shape3, 1)
    xr, xi = xr.reshape(shape3), xi.reshape(shape3)
    for i, d in enumerate((1, 2, 4)):
        pr, pi = apow_ref[i:i + 1, :hs], apow_ref[i:i + 1, hs:]
        if reverse:
            pi, shift, keep = -pi, SUB - d, row < SUB - d
        else:
            shift, keep = d, row >= d
        sr = jnp.where(keep, pltpu.roll(xr, shift, 1), 0.0)
        si = jnp.where(keep, pltpu.roll(xi, shift, 1), 0.0)
        xr, xi = xr + pr * sr - pi * si, xi + pr * si + pi * sr
    xr, xi = xr.reshape(t_len, hs), xi.reshape(t_len, hs)
    chunks = [slice(c * LANES, (c + 1) * LANES) for c in range(hs // LANES)]
    for c, lanes in enumerate(chunks):
        sr_ref[c] = xr[:, lanes]
        si_ref[c] = xi[:, lanes]
    edge = pl.ds(0 if reverse else SUB - 1, n_tiles, stride=SUB)
    tr, ti = doubling(jnp.concatenate([sr_ref[c, edge, :] for c in range(len(chunks))], axis=1),
                      jnp.concatenate([si_ref[c, edge, :] for c in range(len(chunks))], axis=1), n_tiles, 3, n_tiles)
    trow = lax.broadcasted_iota(jnp.int32, tr.shape, 0)
    if reverse:
        shift, keep = n_tiles - 1, trow < n_tiles - 1
    else:
        shift, keep = 1, trow >= 1
    er_ref[...] = jnp.where(keep, pltpu.roll(tr, shift, 0), 0.0)
    ei_ref[...] = jnp.where(keep, pltpu.roll(ti, shift, 0), 0.0)
    lin = POW_ROWS + SUB if reverse else POW_ROWS
    mr, mi = apow_ref[lin:lin + SUB, :hs], apow_ref[lin:lin + SUB, hs:]
    if reverse:
        mi = -mi
    for t in range(n_tiles):
        rows = slice(t * SUB, (t + 1) * SUB)
        er, ei = er_ref[t:t + 1, :], ei_ref[t:t + 1, :]
        add_r, add_i = mr * er - mi * ei, mr * ei + mi * er
        for c, lanes in enumerate(chunks):
            sr_ref[c, rows, :] += add_r[:, lanes]
            si_ref[c, rows, :] += add_i[:, lanes]
    return (jnp.concatenate([sr_ref[c] for c in range(len(chunks))], axis=1),
            jnp.concatenate([si_ref[c] for c in range(len(chunks))], axis=1))


def _ssm_chunk(s):
    t_len = _pick(s, 256, 8)
    assert t_len & (t_len - 1) == 0 and t_len <= 1 << POW_ROWS, t_len
    return t_len


def _fold_carry(br, bi, carry_ref, apow_ref, at_row, conj):
    hs = BLOCK_STATES
    cr, ci = carry_ref[0:1, :hs], carry_ref[0:1, hs:]
    ar, ai = apow_ref[0:1, :hs], apow_ref[0:1, hs:]
    if conj:
        ai = -ai
    here = lax.broadcasted_iota(jnp.int32, br.shape, 0) == at_row
    return jnp.where(here, br + (ar * cr - ai * ci), br), jnp.where(here, bi + (ar * ci + ai * cr), bi)


def _ssm_fwd(proj, ucol, bbd, ccat, dskip, apow, t_len, *, name):
    s = proj.shape[0]
    nb = bbd.shape[0]
    nc = s // t_len
    hs = BLOCK_STATES

    def body(u_ref, bbd_ref, ccat_ref, d_ref, apow_ref, y_ref, z_ref, xs_ref, carry_ref, *bufs):
        @pl.when(pl.program_id(1) == 0)
        def _():
            carry_ref[...] = jnp.zeros_like(carry_ref)

        xs_ref[...] = carry_ref[...]
        u = u_ref[...]
        bu = jnp.dot(u.astype(BF16), bbd_ref[...], preferred_element_type=F32)
        br, bi = _fold_carry(bu[:, :hs], bu[:, hs:], carry_ref, apow_ref, 0, False)
        xr, xi = _scan(br, bi, apow_ref, bufs, t_len, False)
        xcat = jnp.concatenate([xr, xi], axis=1)
        carry_ref[...] = jnp.broadcast_to(xcat[t_len - 1:t_len, :], carry_ref.shape)
        y = jnp.dot(xcat.astype(BF16), ccat_ref[...], preferred_element_type=F32) + d_ref[...] * u
        y_ref[...] = y
        z_ref[...] = _gelu(y).astype(z_ref.dtype)

    return pl.pallas_call(
        body, name=name, grid=(nb, nc),
        in_specs=[pl.BlockSpec((t_len, LANES), lambda j, n: (n, ucol + j)),
                  pl.BlockSpec((None, LANES, 2 * hs), lambda j, n: (j, 0, 0)),
                  pl.BlockSpec((None, 2 * hs, LANES), lambda j, n: (j, 0, 0)),
                  pl.BlockSpec((1, LANES), lambda j, n: (0, j)),
                  pl.BlockSpec((None, TAB_ROWS, 2 * hs), lambda j, n: (j, 0, 0))],
        out_specs=[pl.BlockSpec((t_len, LANES), lambda j, n: (n, j)),
                   pl.BlockSpec((t_len, LANES), lambda j, n: (n, j)),
                   pl.BlockSpec((None, None, 8, 2 * hs), lambda j, n: (j, n, 0, 0))],
        out_shape=[jax.ShapeDtypeStruct((s, nb * LANES), F32), jax.ShapeDtypeStruct((s, nb * LANES), BF16),
                   jax.ShapeDtypeStruct((nb, nc, 8, 2 * hs), F32)],
        scratch_shapes=[pltpu.VMEM((8, 2 * hs), F32)] + _scan_bufs(t_len),
        compiler_params=_params(40 << 20),
    )(proj, bbd, ccat, dskip, apow)


def _ssm_bwd(proj, ucol, y, dzd, dz2, xs, bbd, ccat, dskip, apow, t_len, *, name):
    s = proj.shape[0]
    nb = bbd.shape[0]
    nc = s // t_len
    hs = BLOCK_STATES

    def body(u_ref, y_ref, dzd_ref, dz2_ref, xs_ref, bbd_ref, ccat_ref, d_ref, apow_ref,
             du_ref, dbbd_ref, dccat_ref, dd_ref, da_ref, gcarry_ref, *bufs):
        @pl.when(pl.program_id(1) == 0)
        def _():
            gcarry_ref[...] = jnp.zeros_like(gcarry_ref)
            dbbd_ref[...] = jnp.zeros_like(dbbd_ref)
            dccat_ref[...] = jnp.zeros_like(dccat_ref)
            dd_ref[...] = jnp.zeros_like(dd_ref)
            da_ref[...] = jnp.zeros_like(da_ref)

        u = u_ref[...]
        ub = u.astype(BF16)
        dy = (dzd_ref[...] + dz2_ref[...]) * _gelu_grad(y_ref[...])
        dyb = dy.astype(BF16)
        bu = jnp.dot(ub, bbd_ref[...], preferred_element_type=F32)
        br, bi = _fold_carry(bu[:, :hs], bu[:, hs:], xs_ref, apow_ref, 0, False)
        xr, xi = _scan(br, bi, apow_ref, bufs[:4], t_len, False)
        sr, si = xs_ref[0:1, :hs], xs_ref[0:1, hs:]
        dxd = lax.dot_general(dyb, ccat_ref[...], _NT, preferred_element_type=F32)
        dr, di = _fold_carry(dxd[:, :hs], dxd[:, hs:], gcarry_ref, apow_ref, t_len - 1, True)
        gr, gi = _scan(dr, di, apow_ref, bufs[4:], t_len, True)
        gcat = jnp.concatenate([gr, gi], axis=1)
        gcarry_ref[...] = jnp.broadcast_to(gcat[0:1, :], gcarry_ref.shape)
        gb = gcat.astype(BF16)
        du_ref[...] = lax.dot_general(gb, bbd_ref[...], _NT, preferred_element_type=F32) + d_ref[...] * dy
        dbbd_ref[...] += lax.dot_general(ub, gb, _TN, preferred_element_type=F32)
        xb = jnp.concatenate([xr, xi], axis=1).astype(BF16)
        dccat_ref[...] += lax.dot_general(xb, dyb, _TN, preferred_element_type=F32)
        dd_ref[...] += jnp.sum(dy * u, axis=0, keepdims=True)
        first = lax.broadcasted_iota(jnp.int32, xr.shape, 0) == 0
        xpr = jnp.where(first, sr, pltpu.roll(xr, 1, 0))
        xpi = jnp.where(first, si, pltpu.roll(xi, 1, 0))
        dar = jnp.sum(gr * xpr + gi * xpi, axis=0, keepdims=True)
        dai = jnp.sum(gi * xpr - gr * xpi, axis=0, keepdims=True)
        da_ref[...] += jnp.concatenate([dar, dai], axis=1)

    def rows(j, n):
        return nc - 1 - n

    chunk = pl.BlockSpec((t_len, LANES), lambda j, n: (rows(j, n), j))
    return pl.pallas_call(
        body, name=name, grid=(nb, nc),
        in_specs=[pl.BlockSpec((t_len, LANES), lambda j, n: (rows(j, n), ucol + j)), chunk, chunk, chunk,
                  pl.BlockSpec((None, None, 8, 2 * hs), lambda j, n: (j, rows(j, n), 0, 0)),
                  pl.BlockSpec((None, LANES, 2 * hs), lambda j, n: (j, 0, 0)),
                  pl.BlockSpec((None, 2 * hs, LANES), lambda j, n: (j, 0, 0)),
                  pl.BlockSpec((1, LANES), lambda j, n: (0, j)),
                  pl.BlockSpec((None, TAB_ROWS, 2 * hs), lambda j, n: (j, 0, 0))],
        out_specs=[chunk,
                   pl.BlockSpec((None, LANES, 2 * hs), lambda j, n: (j, 0, 0)),
                   pl.BlockSpec((None, 2 * hs, LANES), lambda j, n: (j, 0, 0)),
                   pl.BlockSpec((1, LANES), lambda j, n: (0, j)),
                   pl.BlockSpec((None, 1, 2 * hs), lambda j, n: (j, 0, 0))],
        out_shape=[jax.ShapeDtypeStruct((s, nb * LANES), F32),
                   jax.ShapeDtypeStruct((nb, LANES, 2 * hs), F32),
                   jax.ShapeDtypeStruct((nb, 2 * hs, LANES), F32),
                   jax.ShapeDtypeStruct((1, nb * LANES), F32),
                   jax.ShapeDtypeStruct((nb, 1, 2 * hs), F32)],
        scratch_shapes=[pltpu.VMEM((8, 2 * hs), F32)] + _scan_bufs(t_len) + _scan_bufs(t_len),
        compiler_params=_params(48 << 20),
    )(proj, y, dzd, dz2, xs, bbd, ccat, dskip, apow)


def _to_blocks(a):
    g, p, k = a.shape
    nb = g // GROUPS_PER_BLOCK
    eye = jnp.eye(GROUPS_PER_BLOCK, dtype=a.dtype)
    a4 = a.reshape(nb, GROUPS_PER_BLOCK, p, k)
    out = jnp.einsum("ab,jbpk->jakbp", eye, a4)
    return out.reshape(nb, GROUPS_PER_BLOCK * k, GROUPS_PER_BLOCK * p)


def _from_blocks(d, p, k):
    nb = d.shape[0]
    d5 = d.reshape(nb, GROUPS_PER_BLOCK, k, GROUPS_PER_BLOCK, p)
    eye = jnp.eye(GROUPS_PER_BLOCK, dtype=bool)[None, :, None, :, None]
    diag = jnp.sum(jnp.where(eye, d5, 0.0), axis=1)
    return jnp.transpose(diag, (0, 2, 3, 1)).reshape(nb * GROUPS_PER_BLOCK, p, k)


def _merge_fwd(attn, y, gl, g_a, g_s, name):
    s, wa = attn.shape
    ws = y.shape[1]
    tm = _pick(s, 256, 16)

    def body(a_ref, y_ref, gl_ref, ga_ref, gs_ref, o_ref):
        av = a_ref[...]
        o_ref[:, :wa] = ((av * _rstd(av)) * ga_ref[...]).astype(o_ref.dtype)
        sv = _gelu(y_ref[...]) * jax.nn.sigmoid(gl_ref[...])
        o_ref[:, wa:] = ((sv * _rstd(sv)) * gs_ref[...]).astype(o_ref.dtype)

    return pl.pallas_call(
        body, name=name, grid=(s // tm,),
        in_specs=[_row_spec(tm, wa), _row_spec(tm, ws), _row_spec(tm, ws), _vec_spec(wa), _vec_spec(ws)],
        out_specs=_row_spec(tm, wa + ws), out_shape=jax.ShapeDtypeStruct((s, wa + ws), BF16),
    )(attn, y, gl, g_a, g_s)


def _merge_bwd(dmerged, attn, y, gl, g_a, g_s, name):
    s, wa = attn.shape
    ws = y.shape[1]
    tm = _pick(s, 256, 16)

    def body(dm_ref, a_ref, y_ref, gl_ref, ga_ref, gs_ref, da_ref, dgl_ref, dzd_ref, dga_ref, dgs_ref):
        @pl.when(pl.program_id(0) == 0)
        def _():
            dga_ref[...] = jnp.zeros_like(dga_ref)
            dgs_ref[...] = jnp.zeros_like(dgs_ref)

        dan, dsn = dm_ref[:, :wa], dm_ref[:, wa:]
        av = a_ref[...]
        ra = _rstd(av)
        ahat = av * ra
        dga_ref[...] += jnp.sum(dan * ahat, axis=0, keepdims=True)
        da_ref[...] = _norm_bwd(dan * ga_ref[...], ahat, ra)
        z = _gelu(y_ref[...])
        sig = jax.nn.sigmoid(gl_ref[...])
        sv = z * sig
        rs = _rstd(sv)
        shat = sv * rs
        dgs_ref[...] += jnp.sum(dsn * shat, axis=0, keepdims=True)
        dssm = _norm_bwd(dsn * gs_ref[...], shat, rs)
        dzd_ref[...] = dssm * sig
        dgl_ref[...] = (dssm * z * sig * (1.0 - sig)).astype(dgl_ref.dtype)

    return pl.pallas_call(
        body, name=name, grid=(s // tm,),
        in_specs=[_row_spec(tm, wa + ws), _row_spec(tm, wa), _row_spec(tm, ws), _row_spec(tm, ws),
                  _vec_spec(wa), _vec_spec(ws)],
        out_specs=[_row_spec(tm, wa), _row_spec(tm, ws), _row_spec(tm, ws), _vec_spec(wa), _vec_spec(ws)],
        out_shape=[jax.ShapeDtypeStruct((s, wa), F32), jax.ShapeDtypeStruct((s, ws), BF16),
                   jax.ShapeDtypeStruct((s, ws), F32), jax.ShapeDtypeStruct((1, wa), F32),
                   jax.ShapeDtypeStruct((1, ws), F32)],
    )(dmerged, attn, y, gl, g_a, g_s)


def _shift_down(main, halo, k):
    rolled = pltpu.roll(main, k, 0)
    row = lax.broadcasted_iota(jnp.int32, main.shape, 0)
    for r in range(k):
        rolled = jnp.where(row == r, halo[8 - k + r:8 - k + r + 1, :], rolled)
    return rolled


def _shift_up(main, halo, k):
    tm = main.shape[0]
    rolled = pltpu.roll(main, tm - k, 0)
    row = lax.broadcasted_iota(jnp.int32, main.shape, 0)
    for r in range(k):
        rolled = jnp.where(row == tm - k + r, halo[r:r + 1, :], rolled)
    return rolled


def _conv(main, halo, w_ref, b_ref):
    return (b_ref[...] + w_ref[0:1, :] * _shift_down(main, halo, 2) + w_ref[1:2, :] * _shift_down(main, halo, 1)
            + w_ref[2:3, :] * main)


def _gate_tiles(s, f):
    return _pick(s, 512, 16), _pick(f, 512, LANES)


def _gate_in_specs(tm, tn, nfb, order):
    hb = tm // 8
    ij = (lambda a, b: (b, a)) if order == "ji" else (lambda a, b: (a, b))

    def main(off):
        return pl.BlockSpec((tm, tn), lambda a, b: (ij(a, b)[0], ij(a, b)[1] + off))

    def halo(off):
        return pl.BlockSpec((8, tn), lambda a, b: (jnp.maximum(ij(a, b)[0] * hb - 1, 0), ij(a, b)[1] + off))

    def vec(rows, off):
        return pl.BlockSpec((rows, tn), lambda a, b: (0, ij(a, b)[1] + off))

    return [main(0), main(nfb), halo(0), halo(nfb), vec(3, 0), vec(3, nfb), vec(1, 0), vec(1, nfb)]


def _gate_fwd(up0, conv_w, conv_b, name):
    s, f2 = up0.shape
    f = f2 // 2
    tm, tn = _gate_tiles(s, f)
    nfb = f // tn

    def body(v_ref, g_ref, vh_ref, gh_ref, wv_ref, wg_ref, bv_ref, bg_ref, o_ref):
        top = pl.program_id(0) == 0
        vh = jnp.where(top, 0.0, vh_ref[...])
        gh = jnp.where(top, 0.0, gh_ref[...])
        val = _conv(v_ref[...], vh, wv_ref, bv_ref)
        gate = _conv(g_ref[...], gh, wg_ref, bg_ref)
        o_ref[...] = (_gelu(gate) * val).astype(o_ref.dtype)

    return pl.pallas_call(
        body, name=name, grid=(s // tm, nfb),
        in_specs=_gate_in_specs(tm, tn, nfb, "ij"), out_specs=pl.BlockSpec((tm, tn), lambda i, j: (i, j)),
        out_shape=jax.ShapeDtypeStruct((s, f), BF16),
        compiler_params=_params(24 * tm * tn * 4 + (4 << 20)),
    )(up0, up0, up0, up0, conv_w, conv_w, conv_b, conv_b)


def _gate_bwd(up0, conv_w, conv_b, da, name):
    s, f2 = up0.shape
    f = f2 // 2
    tm, tn = _gate_tiles(s, f)
    nfb = f // tn

    def body(v_ref, g_ref, vh_ref, gh_ref, wv_ref, wg_ref, bv_ref, bg_ref, da_ref, dup_ref, dcb_ref, dcw_ref):
        top = pl.program_id(1) == 0

        @pl.when(top)
        def _():
            dcb_ref[...] = jnp.zeros_like(dcb_ref)
            dcw_ref[...] = jnp.zeros_like(dcw_ref)

        halos = (jnp.where(top, 0.0, vh_ref[...]), jnp.where(top, 0.0, gh_ref[...]))
        mains = (v_ref[...], g_ref[...])
        val = _conv(mains[0], halos[0], wv_ref, bv_ref)
        gate = _conv(mains[1], halos[1], wg_ref, bg_ref)
        dav = da_ref[...]
        act, act_grad = _gelu_and_grad(gate)
        dups = (dav * act, (dav * val) * act_grad)
        for half in range(2):
            dup = dups[half]
            dup_ref[half] = dup
            dcb_ref[half] += jnp.sum(dup, axis=0, keepdims=True)
            dcw_ref[half, 0:1, :] += jnp.sum(dup * _shift_down(mains[half], halos[half], 2), axis=0, keepdims=True)
            dcw_ref[half, 1:2, :] += jnp.sum(dup * _shift_down(mains[half], halos[half], 1), axis=0, keepdims=True)
            dcw_ref[half, 2:3, :] += jnp.sum(dup * mains[half], axis=0, keepdims=True)

    return pl.pallas_call(
        body, name=name, grid=(nfb, s // tm),
        in_specs=_gate_in_specs(tm, tn, nfb, "ji") + [pl.BlockSpec((tm, tn), lambda j, i: (i, j))],
        out_specs=[pl.BlockSpec((2, tm, tn), lambda j, i: (0, i, j)),
                   pl.BlockSpec((2, 1, tn), lambda j, i: (0, 0, j)),
                   pl.BlockSpec((2, 3, tn), lambda j, i: (0, 0, j))],
        out_shape=[jax.ShapeDtypeStruct((2, s, f), F32), jax.ShapeDtypeStruct((2, 1, f), F32),
                   jax.ShapeDtypeStruct((2, 3, f), F32)],
        compiler_params=_params(40 * tm * tn * 4 + (4 << 20)),
    )(up0, up0, up0, up0, conv_w, conv_w, conv_b, conv_b, da)


def _conv_bwd(dup, conv_w, name):
    _, s, f = dup.shape
    tm, tn = _pick(s, 512, 16), _pick(f, 1536, LANES)
    nfb, ni, hb = f // tn, s // tm, tm // 8

    def body(d_ref, dh_ref, w_ref, o_ref):
        main = d_ref[...]
        halo = jnp.where(pl.program_id(1) == ni - 1, 0.0, dh_ref[...])
        o_ref[...] = (w_ref[2:3, :] * main + w_ref[1:2, :] * _shift_up(main, halo, 1)
                      + w_ref[0:1, :] * _shift_up(main, halo, 2)).astype(o_ref.dtype)

    return pl.pallas_call(
        body, name=name, grid=(2, ni, nfb),
        in_specs=[pl.BlockSpec((None, tm, tn), lambda h, i, j: (h, i, j)),
                  pl.BlockSpec((None, 8, tn), lambda h, i, j: (h, jnp.minimum((i + 1) * hb, s // 8 - 1), j)),
                  pl.BlockSpec((3, tn), lambda h, i, j: (0, h * nfb + j))],
        out_specs=pl.BlockSpec((tm, tn), lambda h, i, j: (i, h * nfb + j)),
        out_shape=jax.ShapeDtypeStruct((s, 2 * f), BF16),
        compiler_params=_params(12 * tm * tn * 4 + (4 << 20)),
    )(dup, dup, conv_w)


def _adamw(w, g, m, v, name):
    r, c = w.shape
    tr = _pick(r, max(8, (1 << 19) // max(c, 1) // 8 * 8), 8)
    c1, c2 = 1.0 / (1.0 - ADAM_B1 ** ADAM_STEP), 1.0 / (1.0 - ADAM_B2 ** ADAM_STEP)

    def body(w_ref, g_ref, m_ref, v_ref, d_ref, nm_ref, nv_ref):
        gv = g_ref[...]
        nm = ADAM_B1 * m_ref[...] + (1.0 - ADAM_B1) * gv
        nv = ADAM_B2 * v_ref[...] + (1.0 - ADAM_B2) * (gv * gv)
        nm_ref[...] = nm
        nv_ref[...] = nv
        d_ref[...] = -ADAM_LR * ((nm * c1) / (jnp.sqrt(nv * c2) + ADAM_EPS) + ADAM_WD * w_ref[...])

    spec = pl.BlockSpec((tr, c), lambda i: (i, 0))
    out = jax.ShapeDtypeStruct((r, c), F32)
    return pl.pallas_call(body, name=name, grid=(r // tr,), in_specs=[spec] * 4, out_specs=[spec] * 3,
                          out_shape=[out] * 3, compiler_params=_params(14 * tr * c * 4 + (4 << 20)))(w, g, m, v)


def _adamw_nd(w, g, m, v, name):
    shape = w.shape
    c = shape[-1]
    outs = _adamw(w.reshape(-1, c), g.reshape(-1, c), m.reshape(-1, c), v.reshape(-1, c), name)
    return [o.reshape(shape) for o in outs]


BIG = ("w_in", "w_glu", "w_out", "w_up", "w_down")
SMALL = ("b_ada", "g_pre_mix", "g_post_mix", "attn_sinks", "lam_re", "lam_im", "log_step", "ssm_b_re", "ssm_b_im",
         "ssm_c_re", "ssm_c_im", "ssm_d", "g_attn_out", "g_ssm_out", "g_pre_ffn", "g_post_ffn", "conv_b")
ORDER = ("w_ada", "b_ada", "g_pre_mix", "g_post_mix", "w_in", "attn_sinks", "lam_re", "lam_im", "log_step",
         "ssm_b_re", "ssm_b_im", "ssm_c_re", "ssm_c_im", "ssm_d", "w_glu", "g_attn_out", "g_ssm_out", "w_out",
         "g_pre_ffn", "g_post_ffn", "w_up", "conv_w", "conv_b", "w_down")
COL_SHARDED = ("w_in", "w_up")


def kernel(x, c, w_ada, b_ada, g_pre_mix, g_post_mix, w_in, attn_sinks, lam_re, lam_im, log_step, ssm_b_re, ssm_b_im, ssm_c_re, ssm_c_im, ssm_d, w_glu, g_attn_out, g_ssm_out, w_out, g_pre_ffn, g_post_ffn, w_up, conv_w, conv_b, w_down, loss_target, m_w_ada, m_b_ada, m_g_pre_mix, m_g_post_mix, m_w_in, m_attn_sinks, m_lam_re, m_lam_im, m_log_step, m_ssm_b_re, m_ssm_b_im, m_ssm_c_re, m_ssm_c_im, m_ssm_d, m_w_glu, m_g_attn_out, m_g_ssm_out, m_w_out, m_g_pre_ffn, m_g_post_ffn, m_w_up, m_conv_w, m_conv_b, m_w_down, v_w_ada, v_b_ada, v_g_pre_mix, v_g_post_mix, v_w_in, v_attn_sinks, v_lam_re, v_lam_im, v_log_step, v_ssm_b_re, v_ssm_b_im, v_ssm_c_re, v_ssm_c_im, v_ssm_d, v_w_glu, v_g_attn_out, v_g_ssm_out, v_w_out, v_g_pre_ffn, v_g_post_ffn, v_w_up, v_conv_w, v_conv_b, v_w_down):
    env = dict(locals())
    W = {n: env[n] for n in ORDER}
    M = {n: env["m_" + n] for n in ORDER}
    V = {n: env["v_" + n] for n in ORDER}

    depth = w_ada.shape[0]
    s, d = x.shape[1], x.shape[2]
    xs0 = x.reshape(s, d)
    tgt = loss_target.reshape(s, d)
    attn_w = d // 2
    ssm_w = d - attn_w
    in_cols = w_in.shape[2] * N_DEV
    kv_dim = (in_cols - attn_w - ssm_w) // 2
    n_q, n_kv = attn_w // HEAD_DIM, kv_dim // HEAD_DIM
    n_grp = ssm_w // SSM_GROUP
    nb = ssm_w // LANES
    f = w_down.shape[1] * N_DEV
    ucol = (attn_w + 2 * kv_dim) // LANES
    t_len = _ssm_chunk(s)
    me = 4 * lax.axis_index("x") + 2 * lax.axis_index("y") + lax.axis_index("c")

    def at_block(ref, idx):
        return ref.at[idx]

    def at_rows(n_rows):
        return lambda ref, idx: ref.at[:, pl.ds(pl.multiple_of(idx * n_rows, 8), n_rows), :]

    def at_cols(n_cols):
        return lambda ref, idx: ref.at[:, :, pl.ds(pl.multiple_of(idx * n_cols, LANES), n_cols)]

    w_in_first = w_in[:1].astype(BF16)
    first = _gather_multi([w_in_first, conv_w, c],
                          [(N_DEV,) + w_in_first.shape, (N_DEV,) + conv_w.shape, (N_DEV,) + c.shape],
                          [at_block, at_block, at_block], "ag_first")
    w_in_full = [_cols_from_blocks(first[0], "w_in_layout0")[0]]
    conv_w_full = jnp.transpose(first[1], (1, 2, 0, 3)).reshape(depth, 3, 2 * f)
    c_all = first[2].reshape(N_DEV, d)

    def at_rows2(n_rows):
        return lambda ref, idx: ref.at[pl.ds(pl.multiple_of(idx * n_rows, 8), n_rows), :]

    def at_cols2(n_cols):
        return lambda ref, idx: ref.at[:, pl.ds(pl.multiple_of(idx * n_cols, LANES), n_cols)]

    def whole(ref, idx):
        return ref

    def gather_kind(n):
        return "blk" if n == "w_in" else "cols" if n in COL_SHARDED else "rows"

    def gather_view(n):
        return {"blk": at_block, "cols": at_cols2(W[n].shape[2]), "rows": at_rows2(W[n].shape[1])}[gather_kind(n)]

    def gather_shape(n):
        _, a, b = W[n].shape
        return {"blk": (N_DEV, a, b), "cols": (a, N_DEV * b), "rows": (N_DEV * a, b)}[gather_kind(n)]

    later = [(n, l) for l in range(depth) for n in (BIG if l > 0 else BIG[1:])]
    later_srcs = [W[n][l].astype(BF16) for n, l in later]
    later_views = [gather_view(n) for n, _ in later]
    me_arr = me.astype(jnp.int32).reshape(1)
    lands = [_place_own(me_arr, src, lax.empty(gather_shape(n), BF16), gather_kind(n), f"ag_own_{n}{l}")
             for (n, l), src in zip(later, later_srcs)]
    ag_started, ag_token = _exchange_start(later_srcs, lands, [whole] * len(later), later_views, "ag_start")

    def weights_arrived(names, l, after, name):
        picks = [later.index((n, l)) for n in names]
        _, got = _exchange_wait([ag_started[i] for i in picks], [after], [whole] * len(picks),
                                [later_views[i] for i in picks], name)
        return dict(zip(names, got))

    c_pad = jnp.pad(c_all, ((0, 16 - N_DEV), (0, 0)))
    n_ada = w_ada.shape[2]
    b_shard = lax.dynamic_slice_in_dim(b_ada, me * n_ada, n_ada, axis=1).reshape(depth, 1, n_ada)
    ada_part, c_act = _ada_fwd(c_pad, w_ada, b_shard, "ada_fwd")
    ada_all = _all_gather(ada_part.reshape(depth * 16, n_ada), "ag_ada").reshape(N_DEV, depth, 16, n_ada)
    ada_me = lax.dynamic_index_in_dim(ada_all, me, axis=2, keepdims=False)
    ada = jnp.transpose(ada_me, (1, 0, 2)).reshape(depth, 6, 1, d) + ag_token[0, 0]

    gp = n_grp * STATE

    def hgp(a):
        return jnp.transpose(a, (2, 0, 1)).reshape(SSM_GROUP, gp)

    ssm = []
    for l in range(depth):
        lr, li = lam_re[l].reshape(1, gp), lam_im[l].reshape(1, gp)
        ls = jnp.repeat(log_step[l], STATE).reshape(1, gp)
        br, bi = hgp(ssm_b_re[l]), hgp(ssm_b_im[l])
        bbr, bbi, tab_r, tab_i = _ssm_params_fwd(lr, li, ls, br, bi, f"ssm_params_fwd{l}")
        bb_re = jnp.transpose(bbr.reshape(SSM_GROUP, n_grp, STATE), (1, 2, 0))
        bb_im = jnp.transpose(bbi.reshape(SSM_GROUP, n_grp, STATE), (1, 2, 0))
        bbd = jnp.concatenate([_to_blocks(bb_re), _to_blocks(bb_im)], axis=2).astype(BF16)
        c_re_t = jnp.transpose(ssm_c_re[l], (0, 2, 1))
        c_im_t = jnp.transpose(ssm_c_im[l], (0, 2, 1))
        ccat = jnp.concatenate([jnp.transpose(_to_blocks(c_re_t), (0, 2, 1)),
                                -jnp.transpose(_to_blocks(c_im_t), (0, 2, 1))], axis=1).astype(BF16)

        def tab(t):
            return t.reshape(TAB_ROWS, nb, BLOCK_STATES)

        apow = jnp.transpose(jnp.concatenate([tab(tab_r), tab(tab_i)], axis=2), (1, 0, 2))
        ssm.append(dict(lr=lr, li=li, ls=ls, br=br, bi=bi, bbd=bbd, ccat=ccat, apow=apow,
                        dskip=ssm_d[l].reshape(1, ssm_w)))

    sinks_pad = jnp.pad(attn_sinks, ((0, 0), (0, LANES - n_q)))

    def vec(a):
        return a.reshape(1, -1)

    saved = []
    fw = [dict() for _ in range(depth)]
    xin = xs0
    for l in range(depth):
        sh_m, sc_m, gt_m, sh_f, sc_f, gt_f = (ada[l, i] for i in range(6))
        p = ssm[l]
        h1 = _modnorm_fwd(xin, vec(g_pre_mix[l]), sc_m, sh_m, f"modnorm_mix_fwd{l}")
        if l > 0:
            fw[l].update(weights_arrived(BIG, l, h1, f"ag_wait_layer{l}"))
            blocks = fw[l].pop("w_in")
            w_in_full.append(_cols_from_blocks(blocks.reshape(N_DEV, 1, *blocks.shape[1:]), f"w_in_layout{l}")[0])
        proj = _matmul(h1, w_in_full[l], name=f"mm_in{l}")
        attn, lse = _attn_fwd(proj, sinks_pad[l:l + 1], n_q=n_q, n_kv=n_kv, name=f"attn_fwd{l}")
        y, z, xstart = _ssm_fwd(proj, ucol, p["bbd"], p["ccat"], p["dskip"], p["apow"], t_len, name=f"ssm_fwd{l}")
        if l == 0:
            fw[0].update(weights_arrived(("w_glu", "w_out"), 0, z, "ag_wait_mix0"))
        gl = _matmul(z, fw[l]["w_glu"], name=f"mm_glu{l}")
        merged = _merge_fwd(attn, y, gl, vec(g_attn_out[l]), vec(g_ssm_out[l]), f"merge_fwd{l}")
        mix = _matmul(merged, fw[l]["w_out"], name=f"mm_out{l}")
        x2 = _resnorm_fwd(xin, mix, vec(g_post_mix[l]), gt_m, f"resnorm_mix_fwd{l}")
        h2 = _modnorm_fwd(x2, vec(g_pre_ffn[l]), sc_f, sh_f, f"modnorm_ffn_fwd{l}")
        if l == 0:
            fw[0].update(weights_arrived(("w_up",), 0, h2, "ag_wait_up0"))
        up0 = _matmul(h2, fw[l]["w_up"], name=f"mm_up{l}")
        cw, cb = conv_w_full[l], vec(conv_b[l])
        act = _gate_fwd(up0, cw, cb, f"gate_fwd{l}")
        if l == 0:
            fw[0].update(weights_arrived(("w_down",), 0, act, "ag_wait_down0"))
        ff = _matmul(act, fw[l]["w_down"], name=f"mm_down{l}")
        x3 = _resnorm_fwd(x2, ff, vec(g_post_ffn[l]), gt_f, f"resnorm_ffn_fwd{l}")
        saved.append(dict(xin=xin, h1=h1, proj=proj, attn=attn, lse=lse, y=y, z=z, xstart=xstart, gl=gl,
                          merged=merged, mix=mix, x2=x2, h2=h2, up0=up0, act=act, ff=ff))
        xin = x3

    dxo, loss_acc = _loss_bwd(xin, tgt, "loss")
    loss = lax.psum(loss_acc[0, 0], ("x", "y", "c"))

    grads = {n: [None] * depth for n in ORDER}
    dada = [None] * depth
    big_blocks = {n: [None] * depth for n in BIG}
    seg = jnp.pad(jnp.repeat(jnp.eye(n_grp, dtype=F32), STATE, axis=0), ((0, 0), (0, (-n_grp) % LANES)))

    def part_view(n):
        shp = W[n].shape
        if n == "w_in":
            return at_block, "blk"
        if n in COL_SHARDED:
            return at_cols2(shp[2]), "cols"
        return at_rows2(shp[1]), "rows"

    rs_groups = []
    small_order = SMALL + ("conv_w",)
    small_shapes = {n: W[n].shape for n in SMALL}
    small_shapes["conv_w"] = (depth, 3, 2 * f)
    small_started = [None] * depth

    def send_partials(items, name):
        parts = [big_blocks[n][l] for n, l in items]
        lands = [lax.empty((N_DEV,) + W[n].shape[1:], BF16) for n, _ in items]
        started, token = _exchange_start(parts, lands, [part_view(n)[0] for n, _ in items],
                                         [at_block] * len(items), name)
        rs_groups.append((items, started, name))
        return token[0, 0]

    order = jnp.zeros((), F32)
    for l in reversed(range(depth)):
        sh_m, sc_m, gt_m, sh_f, sc_f, gt_f = (ada[l, i] for i in range(6))
        gt_f = gt_f + order
        a, p = saved[l], ssm[l]
        cw, cb = conv_w_full[l], vec(conv_b[l])
        dff, dg, dgt_f = _resnorm_bwd(dxo, a["ff"], vec(g_post_ffn[l]), gt_f, f"resnorm_ffn_bwd{l}")
        grads["g_post_ffn"][l] = dg
        dact = _matmul(dff, fw[l]["w_down"], tb=True, name=f"mm_down_dx{l}")
        big_blocks["w_down"][l] = _matmul(a["act"], dff, ta=True, out_dtype=BF16, name=f"mm_down_dw{l}")
        dup, dcb, dcw = _gate_bwd(a["up0"], cw, cb, dact, f"gate_bwd{l}")
        grads["conv_b"][l] = dcb.reshape(1, 2 * f)
        grads["conv_w"][l] = jnp.transpose(dcw, (1, 0, 2)).reshape(3, 2 * f)
        dup0 = _conv_bwd(dup, cw, f"conv_bwd{l}")
        dh2 = _matmul(dup0, fw[l]["w_up"], tb=True, name=f"mm_up_dx{l}")
        big_blocks["w_up"][l] = _matmul(a["h2"], dup0, ta=True, out_dtype=BF16, name=f"mm_up_dw{l}")
        if l == 0:
            sc_f = sc_f + send_partials([("w_down", 0), ("w_up", 0)], "rs_start_ffn0")
        dx2, dg, dsc_f, dsh_f = _modnorm_bwd(dh2, a["x2"], vec(g_pre_ffn[l]), sc_f, dxo, f"modnorm_ffn_bwd{l}")
        grads["g_pre_ffn"][l] = dg
        dmix, dg, dgt_m = _resnorm_bwd(dx2, a["mix"], vec(g_post_mix[l]), gt_m, f"resnorm_mix_bwd{l}")
        grads["g_post_mix"][l] = dg
        dmerged = _matmul(dmix, fw[l]["w_out"], tb=True, name=f"mm_out_dx{l}")
        big_blocks["w_out"][l] = _matmul(a["merged"], dmix, ta=True, out_dtype=BF16, name=f"mm_out_dw{l}")
        dattn, dgl, dzd, dga, dgs = _merge_bwd(dmerged, a["attn"], a["y"], a["gl"], vec(g_attn_out[l]),
                                               vec(g_ssm_out[l]), f"merge_bwd{l}")
        grads["g_attn_out"][l], grads["g_ssm_out"][l] = dga, dgs
        dz2 = _matmul(dgl, fw[l]["w_glu"], tb=True, name=f"mm_glu_dx{l}")
        big_blocks["w_glu"][l] = _matmul(a["z"], dgl, ta=True, out_dtype=BF16, name=f"mm_glu_dw{l}")
        dskip = p["dskip"]
        if l == 0:
            dskip = dskip + send_partials([("w_out", 0), ("w_glu", 0)], "rs_start_mix0")
        du, dbbd, dccat, dd, da = _ssm_bwd(a["proj"], ucol, a["y"], dzd, dz2, a["xstart"], p["bbd"], p["ccat"],
                                           dskip, p["apow"], t_len, name=f"ssm_bwd{l}")
        grads["ssm_d"][l] = dd
        hs = BLOCK_STATES
        dbb_re = _from_blocks(dbbd[:, :, :hs], STATE, SSM_GROUP)
        dbb_im = _from_blocks(dbbd[:, :, hs:], STATE, SSM_GROUP)
        dccat_t = jnp.transpose(dccat, (0, 2, 1))
        grads["ssm_c_re"][l] = jnp.transpose(_from_blocks(dccat_t[:, :, :hs], STATE, SSM_GROUP), (0, 2, 1))
        grads["ssm_c_im"][l] = -jnp.transpose(_from_blocks(dccat_t[:, :, hs:], STATE, SSM_GROUP), (0, 2, 1))
        dab_re, dab_im = da[:, 0, :hs].reshape(1, gp), da[:, 0, hs:].reshape(1, gp)
        dlr, dli, dls, dbr, dbi = _ssm_params_bwd(p["lr"], p["li"], p["ls"], p["br"], p["bi"], dab_re, dab_im,
                                                  hgp(dbb_re), hgp(dbb_im), seg, f"ssm_params_bwd{l}")
        grads["lam_re"][l], grads["lam_im"][l], grads["log_step"][l] = dlr, dli, dls[0, :n_grp]
        grads["ssm_b_re"][l] = jnp.transpose(dbr.reshape(SSM_GROUP, n_grp, STATE), (1, 2, 0))
        grads["ssm_b_im"][l] = jnp.transpose(dbi.reshape(SSM_GROUP, n_grp, STATE), (1, 2, 0))
        dq, dk, dv, dsink = _attn_bwd(a["proj"], sinks_pad[l:l + 1], a["attn"], a["lse"], dattn,
                                      n_q=n_q, n_kv=n_kv, name=f"attn_bwd{l}")
        grads["attn_sinks"][l] = dsink[0, :n_q]
        dproj = jnp.concatenate([dq, dk, dv, du], axis=1).astype(BF16)
        dh1 = _matmul(dproj, w_in_full[l], tb=True, name=f"mm_in_dx{l}")
        big_blocks["w_in"][l] = _blocks_from_cols(_matmul(a["h1"], dproj, ta=True, name=f"mm_in_dw{l}"),
                                                  f"w_in_grad_layout{l}")
        dxo, dg, dsc_m, dsh_m = _modnorm_bwd(dh1, a["xin"], vec(g_pre_mix[l]), sc_m, dx2, f"modnorm_mix_bwd{l}")
        grads["g_pre_mix"][l] = dg
        dada[l] = jnp.concatenate([dsh_m, dsc_m, dgt_m, dsh_f, dsc_f, dgt_f], axis=1)
        if l > 0:
            order = send_partials([(n, l) for n in reversed(BIG)], f"rs_start_layer{l}")
        else:
            send_partials([("w_in", 0)], "rs_start_in0")
        spack = _pack([dada[l]] + [grads[n][l] for n in small_order[1:]], F32, 1024)
        started, _ = _exchange_start([spack], [lax.empty((N_DEV,) + spack.shape, F32)], [whole], [at_block],
                                     f"small_start{l}")
        small_started[l] = started
    grad_x = dxo.reshape(x.shape)

    delta, new_m, new_v = {}, {}, {}
    stacked = {n: None for n in BIG}
    landed_layers = {n: 0 for n in BIG}
    after = [dxo]
    for items, started, name in rs_groups:
        mine, landed = _exchange_wait(started, after, [part_view(n)[0] for n, _ in items], [at_block] * len(items),
                                      name.replace("start", "wait"))
        for (n, l), part, slots in zip(items, mine, landed):
            stacked[n] = _sum_slots_own(me_arr, slots, part, part_view(n)[1], f"rs_sum_{n}{l}", layer=l,
                                        n_layers=depth, stacked=stacked[n])
            landed_layers[n] += 1
            if landed_layers[n] == depth:
                grads[n] = stacked[n]
                delta[n], new_m[n], new_v[n] = _adamw_nd(W[n], grads[n], M[n], V[n], f"adamw_{n}")
                after.append(delta[n])

    n_cw = conv_w.shape[2]
    small_sums, dada_rows = [None] * depth, [None] * depth
    for l in reversed(range(depth)):
        mine, landed = _exchange_wait(small_started[l], after, [whole], [at_block], f"small_wait{l}")
        ssum = _sum_slots_own(me_arr, landed[0], mine[0], "self", f"sum_small{l}").reshape(-1)
        small_sums[l] = _unpack(ssum, [small_shapes[n][1:] for n in small_order])
        slot = lax.broadcasted_iota(jnp.int32, (N_DEV, 6 * d), 0)
        dada_rows[l] = jnp.where(slot == me, mine[0].reshape(-1)[:6 * d][None],
                                 landed[0].reshape(N_DEV, -1)[:, :6 * d])
    for i, n in enumerate(small_order):
        grads[n] = jnp.stack([small_sums[l][i] for l in range(depth)])
    grads["conv_w"] = lax.dynamic_slice_in_dim(grads["conv_w"], me * n_cw, n_cw, axis=2)
    dada_all = jnp.stack(dada_rows, axis=1)
    dada_shard = lax.dynamic_slice_in_dim(dada_all, me * n_ada, n_ada, axis=2)
    kp = LANES
    dada_pad = jnp.pad(jnp.transpose(dada_shard, (1, 0, 2)), ((0, 0), (0, kp - N_DEV), (0, 0)))
    act_t = jnp.pad(jnp.transpose(c_act[:N_DEV]), ((0, 0), (0, kp - N_DEV)))
    grads["w_ada"] = _ada_wgrad(act_t, dada_pad, "ada_wgrad")

    for n in ("w_ada", "conv_w"):
        delta[n], new_m[n], new_v[n] = _adamw_nd(W[n], grads[n], M[n], V[n], f"adamw_{n}")
    packs = [_pack([t[n] for n in SMALL], F32, 1024) for t in (W, grads, M, V)]
    outs = _adamw(*packs, "adamw_small")
    shapes = [W[n].shape for n in SMALL]
    for tgt_d, o in zip((delta, new_m, new_v), outs):
        for n, val in zip(SMALL, _unpack(o.reshape(-1), shapes)):
            tgt_d[n] = val

    return (loss, grad_x, *[grads[n] for n in ORDER], *[delta[n] for n in ORDER],
            *[new_m[n] for n in ORDER], *[new_v[n] for n in ORDER])
```

```python
import functools
import math

import jax
import jax.numpy as jnp
from jax import lax
from jax.experimental import pallas as pl
from jax.experimental.pallas import tpu as pltpu

F32 = jnp.float32
BF16 = jnp.bfloat16

N_DEV = 8
HEAD_DIM = 64
WINDOW = 128
SSM_GROUP = 16
STATE = 64
LANES = 128
GROUPS_PER_BLOCK = LANES // SSM_GROUP
BLOCK_STATES = GROUPS_PER_BLOCK * STATE
EPS = 1e-6
NEG = -1e30
ADAM_LR, ADAM_B1, ADAM_B2, ADAM_EPS, ADAM_WD, ADAM_STEP = 0.001, 0.9, 0.999, 1e-08, 0.01, 10
VMEM_BYTES_V7X = 64 * 1024 * 1024
GELU_C = math.sqrt(2.0 / math.pi)
MESH = pl.DeviceIdType.MESH
ANY = pl.BlockSpec(memory_space=pl.ANY)


def _pick(n, pref, align):
    t = (min(pref, n) // align) * align
    while t >= align:
        if n % t == 0:
            return t
        t -= align
    return n


def _params(vmem_bytes=None):
    if vmem_bytes is None:
        return pltpu.CompilerParams()
    return pltpu.CompilerParams(vmem_limit_bytes=int(min(vmem_bytes, VMEM_BYTES_V7X - (8 << 20))))


def _gelu_and_grad(x):
    x2 = x * x
    half_x = 0.5 * x
    th = jnp.tanh((GELU_C * x) * (1.0 + 0.044715 * x2))
    one_th = 1.0 + th
    grad = 0.5 * one_th + (half_x * (1.0 - th * th)) * (GELU_C + (3.0 * 0.044715 * GELU_C) * x2)
    return half_x * one_th, grad


def _gelu(x):
    return _gelu_and_grad(x)[0]


def _gelu_grad(x):
    return _gelu_and_grad(x)[1]


def _rstd(x):
    return lax.rsqrt(jnp.mean(x * x, axis=-1, keepdims=True) + EPS)


def _norm_bwd(dhat, xhat, r):
    return r * (dhat - xhat * jnp.mean(dhat * xhat, axis=-1, keepdims=True))


def _matmul(a, b, *, ta=False, tb=False, out_dtype=F32, name):
    (kdim, m) = a.shape if ta else a.shape[::-1]
    (n, k2) = b.shape if tb else b.shape[::-1]
    assert kdim == k2, (a.shape, b.shape, ta, tb)
    tm, tn, tk = _pick(m, 1024, LANES), _pick(n, 1024, LANES), _pick(kdim, 2048, LANES)
    nk = kdim // tk
    dn = (((0 if ta else 1,), (1 if tb else 0,)), ((), ()))

    def partial_product(a_ref, b_ref):
        return lax.dot_general(a_ref[...].astype(BF16), b_ref[...].astype(BF16), dn, preferred_element_type=F32)

    def body_one(a_ref, b_ref, o_ref):
        o_ref[...] = partial_product(a_ref, b_ref).astype(o_ref.dtype)

    def body_acc(a_ref, b_ref, o_ref, acc_ref):
        k = pl.program_id(2)

        @pl.when(k == 0)
        def _():
            acc_ref[...] = partial_product(a_ref, b_ref)

        @pl.when((k > 0) & (k < nk - 1))
        def _():
            acc_ref[...] += partial_product(a_ref, b_ref)

        @pl.when(k == nk - 1)
        def _():
            o_ref[...] = (acc_ref[...] + partial_product(a_ref, b_ref)).astype(o_ref.dtype)

    body = body_one if nk == 1 else body_acc
    a_spec = pl.BlockSpec((tk, tm), lambda i, j, k: (k, i)) if ta else pl.BlockSpec((tm, tk), lambda i, j, k: (i, k))
    b_spec = pl.BlockSpec((tn, tk), lambda i, j, k: (j, k)) if tb else pl.BlockSpec((tk, tn), lambda i, j, k: (k, j))
    vmem = (2 * (tm * tk * a.dtype.itemsize + tk * tn * b.dtype.itemsize) + tm * tn * 4
            + 2 * tm * tn * jnp.dtype(out_dtype).itemsize + 3 * tm * tn * 4 + (4 << 20))
    return pl.pallas_call(
        body, name=name, grid=(m // tm, n // tn, nk),
        in_specs=[a_spec, b_spec], out_specs=pl.BlockSpec((tm, tn), lambda i, j, k: (i, j)),
        out_shape=jax.ShapeDtypeStruct((m, n), out_dtype),
        scratch_shapes=[] if nk == 1 else [pltpu.VMEM((tm, tn), F32)],
        compiler_params=_params(vmem),
    )(a, b)


def _all_gather(x, name):
    def body(x_ref, out_ref, send_sems, recv_sems, local_sem):
        x_, y_, c_ = lax.axis_index("x"), lax.axis_index("y"), lax.axis_index("c")
        me, sibling = (x_, y_, c_), (x_, y_, 1 - c_)
        chips = [(1 - x_, y_), (x_, 1 - y_), (1 - x_, 1 - y_)]

        def slot(px, py, pc):
            return out_ref.at[4 * px + 2 * py + pc]

        def copy(k, block, to, src=None):
            return pltpu.make_async_remote_copy(
                src_ref=slot(*block) if src is None else src, dst_ref=slot(*block),
                send_sem=send_sems.at[k], recv_sem=recv_sems.at[k], device_id=to, device_id_type=MESH)

        mine = pltpu.make_async_copy(x_ref, slot(*me), local_sem)
        mine.start()
        first = [copy(0, me, sibling, src=x_ref)]
        first += [copy(1 + j, me, (*chip, c_), src=x_ref) for j, chip in enumerate(chips)]
        for cp in first:
            cp.start()
        passed = [copy(4 + j, (*chip, c_), sibling) for j, chip in enumerate(chips)]
        for j, chip in enumerate(chips):
            copy(1 + j, (*chip, c_), me).wait_recv()
            passed[j].start()
        copy(0, sibling, me).wait_recv()
        for j, chip in enumerate(chips):
            copy(4 + j, (*chip, 1 - c_), me).wait_recv()
        for cp in first + passed:
            cp.wait_send()
        mine.wait()

    return pl.pallas_call(
        body, name=name, out_shape=jax.ShapeDtypeStruct((N_DEV,) + x.shape, x.dtype),
        in_specs=[ANY], out_specs=ANY,
        scratch_shapes=[pltpu.SemaphoreType.DMA((7,)), pltpu.SemaphoreType.DMA((7,)), pltpu.SemaphoreType.DMA],
    )(x)


def _gather_multi(srcs, out_shapes, views, name):
    n = len(srcs)

    def body(*refs):
        src_refs, out_refs = refs[:n], refs[n:2 * n]
        send_sems, recv_sems, local_sems = refs[2 * n:]
        x_, y_, c_ = lax.axis_index("x"), lax.axis_index("y"), lax.axis_index("c")
        me, sibling = (x_, y_, c_), (x_, y_, 1 - c_)
        chips = [(1 - x_, y_), (x_, 1 - y_), (1 - x_, 1 - y_)]

        def slot(i, px, py, pc):
            return views[i](out_refs[i], 4 * px + 2 * py + pc)

        def copy(i, k, block, to, from_src=False):
            return pltpu.make_async_remote_copy(
                src_ref=src_refs[i] if from_src else slot(i, *block), dst_ref=slot(i, *block),
                send_sem=send_sems.at[7 * i + k], recv_sem=recv_sems.at[7 * i + k], device_id=to, device_id_type=MESH)

        mine = [pltpu.make_async_copy(src_refs[i], slot(i, *me), local_sems.at[i]) for i in range(n)]
        for cp in mine:
            cp.start()
        first = []
        for i in range(n):
            first.append(copy(i, 0, me, sibling, True))
            first += [copy(i, 1 + j, me, (*chip, c_), True) for j, chip in enumerate(chips)]
        for cp in first:
            cp.start()
        passed = []
        for j, chip in enumerate(chips):
            for i in range(n):
                copy(i, 1 + j, (*chip, c_), me).wait_recv()
                fwd = copy(i, 4 + j, (*chip, c_), sibling)
                fwd.start()
                passed.append(fwd)
        for i in range(n):
            copy(i, 0, sibling, me).wait_recv()
            for j, chip in enumerate(chips):
                copy(i, 4 + j, (*chip, 1 - c_), me).wait_recv()
        for cp in first + passed:
            cp.wait_send()
        for cp in mine:
            cp.wait()

    return pl.pallas_call(
        body, name=name, out_shape=[jax.ShapeDtypeStruct(s, a.dtype) for s, a in zip(out_shapes, srcs)],
        in_specs=[ANY] * n, out_specs=[ANY] * n,
        scratch_shapes=[pltpu.SemaphoreType.DMA((7 * n,)), pltpu.SemaphoreType.DMA((7 * n,)),
                        pltpu.SemaphoreType.DMA((n,))],
    )(*srcs)


HBM_SPEC = pl.BlockSpec(memory_space=pltpu.HBM)
SEM_SPEC = pl.BlockSpec(memory_space=pltpu.SEMAPHORE)
SIDE_EFFECT = pltpu.SideEffectType.DATAFLOW_SIDE_EFFECTING
N_PEERS = N_DEV - 1


def _peer(k, x_, y_, c_):
    px = 1 - x_ if (k >> 2) & 1 else x_
    py = 1 - y_ if (k >> 1) & 1 else y_
    pc = 1 - c_ if k & 1 else c_
    return (px, py, pc), 4 * px + 2 * py + pc


def _exchange_copies(src_refs, land_refs, send_sems, recv_sems, src_views, dst_views):
    x_, y_, c_ = lax.axis_index("x"), lax.axis_index("y"), lax.axis_index("c")
    me = 4 * x_ + 2 * y_ + c_
    out = []
    for i in range(len(src_refs)):
        for k in range(1, N_DEV):
            peer, idx = _peer(k, x_, y_, c_)

            def copy(dst_slot, i=i, k=k, peer=peer, idx=idx):
                return pltpu.make_async_remote_copy(
                    src_ref=src_views[i](src_refs[i], idx), dst_ref=dst_views[i](land_refs[i], dst_slot),
                    send_sem=send_sems[i].at[k - 1], recv_sem=recv_sems[i].at[k - 1], device_id=peer,
                    device_id_type=MESH)

            out.append((copy(me), copy(idx)))
    return out


def _exchange_start(srcs, lands, src_views, dst_views, name):
    n = len(srcs)

    def body(*refs):
        src_refs, land_refs = refs[:n], refs[n:2 * n]
        send_sems, recv_sems = refs[2 * n:3 * n], refs[3 * n:4 * n]
        token = refs[-1]
        for send, _ in _exchange_copies(src_refs, land_refs, send_sems, recv_sems, src_views, dst_views):
            send.start()
        token[...] = jnp.zeros_like(token)

    sems = [pltpu.SemaphoreType.DMA((N_PEERS,))] * n
    thru = [pltpu.HBM(a.shape, a.dtype) for a in list(srcs) + list(lands)]
    outs = pl.pallas_call(
        body, name=name, out_shape=sems + sems + thru + [jax.ShapeDtypeStruct((8, LANES), F32)],
        in_specs=[HBM_SPEC] * (2 * n),
        out_specs=[SEM_SPEC] * (2 * n) + [HBM_SPEC] * (2 * n) + [pl.BlockSpec(memory_space=pltpu.VMEM)],
        input_output_aliases={j: 2 * n + j for j in range(2 * n)},
        compiler_params=pltpu.CompilerParams(has_side_effects=SIDE_EFFECT),
    )(*[pltpu.with_memory_space_constraint(a, pltpu.HBM) for a in list(srcs) + list(lands)])
    per_array = [(outs[j], outs[n + j], outs[2 * n + j], outs[3 * n + j]) for j in range(n)]
    return per_array, outs[-1]


def _exchange_wait(started, after, src_views, dst_views, name):
    send_sems, recv_sems, srcs, lands = (list(t) for t in zip(*started))
    n = len(srcs)
    after = list(after)

    def body(*refs):
        src_refs, land_refs = refs[:n], refs[n:2 * n]
        send_refs, recv_refs = refs[2 * n:3 * n], refs[3 * n:4 * n]
        copies = _exchange_copies(src_refs, land_refs, send_refs, recv_refs, src_views, dst_views)
        for send, _ in copies:
            send.wait_send()
        for _, recv in copies:
            recv.wait_recv()

    thru = [pltpu.HBM(a.shape, a.dtype) for a in list(srcs) + list(lands)]
    outs = pl.pallas_call(
        body, name=name, out_shape=thru,
        in_specs=[HBM_SPEC] * (2 * n) + [SEM_SPEC] * (2 * n) + [ANY] * len(after),
        out_specs=[HBM_SPEC] * (2 * n),
        input_output_aliases={j: j for j in range(2 * n)},
        compiler_params=pltpu.CompilerParams(has_side_effects=SIDE_EFFECT),
    )(*srcs, *lands, *send_sems, *recv_sems, *after)
    return outs[:n], outs[n:]


def _place_own(me, src, land, kind, name):
    r, c = src.shape
    tr = _pick(r, 512, 16)
    nt = r // tr
    if kind == "rows":
        out_spec = pl.BlockSpec((tr, c), lambda i, mr: (mr[0] * nt + i, 0))
    elif kind == "cols":
        out_spec = pl.BlockSpec((tr, c), lambda i, mr: (i, mr[0]))
    else:
        out_spec = pl.BlockSpec((None, tr, c), lambda i, mr: (mr[0], i, 0))

    def body(me_ref, s_ref, land_ref, o_ref):
        o_ref[...] = s_ref[...]

    return pl.pallas_call(
        body, name=name,
        grid_spec=pltpu.PrefetchScalarGridSpec(
            num_scalar_prefetch=1, grid=(nt,),
            in_specs=[pl.BlockSpec((tr, c), lambda i, mr: (i, 0)), ANY], out_specs=out_spec),
        out_shape=jax.ShapeDtypeStruct(land.shape, land.dtype),
        input_output_aliases={2: 0},
    )(me, src, land)


def _sum_slots_own(me, landed, part, kind, name, *, layer=0, n_layers=1, stacked=None):
    _, r, c = landed.shape
    tr = _pick(r, 512, 16)
    nt = r // tr
    if kind == "rows":
        part_spec = pl.BlockSpec((tr, c), lambda i, mr: (mr[0] * nt + i, 0))
    elif kind == "cols":
        part_spec = pl.BlockSpec((tr, c), lambda i, mr: (i, mr[0]))
    elif kind == "blk":
        part_spec = pl.BlockSpec((None, tr, c), lambda i, mr: (mr[0], i, 0))
    else:
        part_spec = pl.BlockSpec((tr, c), lambda i, mr: (i, 0))

    def body(me_ref, x_ref, p_ref, *rest):
        o_ref = rest[-1]
        own = p_ref[...].astype(F32)
        acc = jnp.where(me_ref[0] == 0, own, x_ref[0].astype(F32))
        for i in range(1, N_DEV):
            acc = acc + jnp.where(me_ref[0] == i, own, x_ref[i].astype(F32))
        o_ref[...] = acc

    operands = [me, landed, part] + ([] if stacked is None else [stacked])
    return pl.pallas_call(
        body, name=name,
        grid_spec=pltpu.PrefetchScalarGridSpec(
            num_scalar_prefetch=1, grid=(nt,),
            in_specs=[pl.BlockSpec((N_DEV, tr, c), lambda i, mr: (0, i, 0)), part_spec]
            + ([] if stacked is None else [ANY]),
            out_specs=pl.BlockSpec((None, tr, c), lambda i, mr: (layer, i, 0))),
        out_shape=jax.ShapeDtypeStruct((n_layers, r, c), F32),
        input_output_aliases={} if stacked is None else {3: 0},
        compiler_params=_params(2 * N_DEV * tr * c * landed.dtype.itemsize + 8 * tr * c * 4 + (4 << 20)),
    )(*operands)


def _cols_from_blocks(blk, name):
    nd, nl, k, n = blk.shape
    tk = _pick(k, 256, 16)

    def body(b_ref, o_ref, wide_ref):
        for dev in range(nd):
            wide_ref[:, dev * n:(dev + 1) * n] = b_ref[dev].astype(F32)
        o_ref[...] = wide_ref[...].astype(o_ref.dtype)

    return pl.pallas_call(
        body, name=name, grid=(nl, k // tk),
        in_specs=[pl.BlockSpec((nd, None, tk, n), lambda l, i: (0, l, i, 0))],
        out_specs=pl.BlockSpec((None, tk, nd * n), lambda l, i: (l, i, 0)),
        out_shape=jax.ShapeDtypeStruct((nl, k, nd * n), BF16),
        scratch_shapes=[pltpu.VMEM((tk, nd * n), F32)],
    )(blk)


def _blocks_from_cols(full, name):
    k, n8 = full.shape
    n = n8 // N_DEV
    tk = _pick(k, 256, 16)

    def body(f_ref, o_ref):
        for dev in range(N_DEV):
            o_ref[dev] = f_ref[:, dev * n:(dev + 1) * n].astype(o_ref.dtype)

    return pl.pallas_call(
        body, name=name, grid=(k // tk,),
        in_specs=[pl.BlockSpec((tk, n8), lambda i: (i, 0))],
        out_specs=pl.BlockSpec((N_DEV, tk, n), lambda i: (0, i, 0)),
        out_shape=jax.ShapeDtypeStruct((N_DEV, k, n), BF16),
    )(full)


def _pack(arrs, dtype, cols):
    flat = jnp.concatenate([a.astype(dtype).reshape(-1) for a in arrs])
    unit = 16 * cols
    pad = (-flat.shape[0]) % unit
    flat = jnp.pad(flat, (0, pad))
    return flat.reshape(-1, cols)


def _unpack(flat, shapes):
    out, off = [], 0
    for s in shapes:
        n = math.prod(s)
        out.append(flat[off:off + n].reshape(s))
        off += n
    return out


def _ada_fwd(c_all, w_ada, b_shard, name):
    nl, d, n = w_ada.shape
    tn = _pick(n, 512, LANES)

    def body(c_ref, w_ref, b_ref, o_ref, act_ref):
        cv = c_ref[...]
        act = cv * jax.nn.sigmoid(cv)
        act_ref[...] = act
        o_ref[...] = jnp.dot(act.astype(BF16), w_ref[...].astype(BF16), preferred_element_type=F32) + b_ref[...]

    return pl.pallas_call(
        body, name=name, grid=(nl, n // tn),
        in_specs=[pl.BlockSpec(c_all.shape, lambda l, j: (0, 0)),
                  pl.BlockSpec((None, d, tn), lambda l, j: (l, 0, j)),
                  pl.BlockSpec((None, 1, tn), lambda l, j: (l, 0, j))],
        out_specs=[pl.BlockSpec((None, c_all.shape[0], tn), lambda l, j: (l, 0, j)),
                   pl.BlockSpec(c_all.shape, lambda l, j: (0, 0))],
        out_shape=[jax.ShapeDtypeStruct((nl, c_all.shape[0], n), F32), jax.ShapeDtypeStruct(c_all.shape, F32)],
        compiler_params=_params(2 * d * tn * 4 + d * tn * 2 + (8 << 20)),
    )(c_all, w_ada, b_shard)


def _ada_wgrad(act_t, dada, name):
    d, kp = act_t.shape
    nl, _, n = dada.shape
    tm = _pick(d, 512, 8)

    def body(a_ref, g_ref, o_ref):
        o_ref[...] = jnp.dot(a_ref[...].astype(BF16), g_ref[...].astype(BF16), preferred_element_type=F32)

    return pl.pallas_call(
        body, name=name, grid=(nl, d // tm),
        in_specs=[pl.BlockSpec((tm, kp), lambda l, i: (i, 0)), pl.BlockSpec((None, kp, n), lambda l, i: (l, 0, 0))],
        out_specs=pl.BlockSpec((None, tm, n), lambda l, i: (l, i, 0)),
        out_shape=jax.ShapeDtypeStruct((nl, d, n), F32),
        compiler_params=_params(4 * tm * n * 4 + 2 * kp * n * 4 + (8 << 20)),
    )(act_t, dada)


def _row_spec(tm, d):
    return pl.BlockSpec((tm, d), lambda i: (i, 0))


def _vec_spec(d):
    return pl.BlockSpec((1, d), lambda i: (0, 0))


def _modnorm_fwd(x, g, sc, sh, name):
    s, d = x.shape
    tm = _pick(s, 256, 16)

    def body(x_ref, g_ref, sc_ref, sh_ref, o_ref):
        xv = x_ref[...]
        o_ref[...] = ((xv * _rstd(xv)) * g_ref[...] * (1.0 + sc_ref[...]) + sh_ref[...]).astype(o_ref.dtype)

    return pl.pallas_call(
        body, name=name, grid=(s // tm,),
        in_specs=[_row_spec(tm, d), _vec_spec(d), _vec_spec(d), _vec_spec(d)], out_specs=_row_spec(tm, d),
        out_shape=jax.ShapeDtypeStruct((s, d), BF16),
    )(x, g, sc, sh)


def _modnorm_bwd(dh, x, g, sc, dres, name):
    s, d = x.shape
    tm = _pick(s, 256, 8)

    def body(dh_ref, x_ref, g_ref, sc_ref, dres_ref, dx_ref, dg_ref, dsc_ref, dsh_ref):
        @pl.when(pl.program_id(0) == 0)
        def _():
            dg_ref[...] = jnp.zeros_like(dg_ref)
            dsc_ref[...] = jnp.zeros_like(dsc_ref)
            dsh_ref[...] = jnp.zeros_like(dsh_ref)

        dh_, xv, gv = dh_ref[...], x_ref[...], g_ref[...]
        r = _rstd(xv)
        xhat = xv * r
        dn = dh_ * (1.0 + sc_ref[...])
        dsh_ref[...] += jnp.sum(dh_, axis=0, keepdims=True)
        dsc_ref[...] += jnp.sum(dh_ * (xhat * gv), axis=0, keepdims=True)
        dg_ref[...] += jnp.sum(dn * xhat, axis=0, keepdims=True)
        dx_ref[...] = _norm_bwd(dn * gv, xhat, r) + dres_ref[...]

    vec = jax.ShapeDtypeStruct((1, d), F32)
    return pl.pallas_call(
        body, name=name, grid=(s // tm,),
        in_specs=[_row_spec(tm, d), _row_spec(tm, d), _vec_spec(d), _vec_spec(d), _row_spec(tm, d)],
        out_specs=[_row_spec(tm, d), _vec_spec(d), _vec_spec(d), _vec_spec(d)],
        out_shape=[jax.ShapeDtypeStruct((s, d), F32), vec, vec, vec],
    )(dh, x, g, sc, dres)


def _resnorm_fwd(x, y, g, gt, name):
    s, d = x.shape
    tm = _pick(s, 256, 8)

    def body(x_ref, y_ref, g_ref, gt_ref, o_ref):
        yv = y_ref[...]
        o_ref[...] = x_ref[...] + (1.0 + gt_ref[...]) * ((yv * _rstd(yv)) * g_ref[...])

    return pl.pallas_call(
        body, name=name, grid=(s // tm,),
        in_specs=[_row_spec(tm, d), _row_spec(tm, d), _vec_spec(d), _vec_spec(d)], out_specs=_row_spec(tm, d),
        out_shape=jax.ShapeDtypeStruct((s, d), F32),
    )(x, y, g, gt)


def _resnorm_bwd(dxo, y, g, gt, name):
    s, d = y.shape
    tm = _pick(s, 256, 16)

    def body(dxo_ref, y_ref, g_ref, gt_ref, dy_ref, dg_ref, dgt_ref):
        @pl.when(pl.program_id(0) == 0)
        def _():
            dg_ref[...] = jnp.zeros_like(dg_ref)
            dgt_ref[...] = jnp.zeros_like(dgt_ref)

        dxo_, yv, gv = dxo_ref[...], y_ref[...], g_ref[...]
        r = _rstd(yv)
        yhat = yv * r
        dn = dxo_ * (1.0 + gt_ref[...])
        dgt_ref[...] += jnp.sum(dxo_ * (yhat * gv), axis=0, keepdims=True)
        dg_ref[...] += jnp.sum(dn * yhat, axis=0, keepdims=True)
        dy_ref[...] = _norm_bwd(dn * gv, yhat, r).astype(dy_ref.dtype)

    vec = jax.ShapeDtypeStruct((1, d), F32)
    return pl.pallas_call(
        body, name=name, grid=(s // tm,),
        in_specs=[_row_spec(tm, d), _row_spec(tm, d), _vec_spec(d), _vec_spec(d)],
        out_specs=[_row_spec(tm, d), _vec_spec(d), _vec_spec(d)],
        out_shape=[jax.ShapeDtypeStruct((s, d), BF16), vec, vec],
    )(dxo, y, g, gt)


def _loss_bwd(xf, tgt, name):
    s, d = xf.shape
    tm = _pick(s, 256, 8)

    def body(x_ref, t_ref, dy_ref, l_ref):
        @pl.when(pl.program_id(0) == 0)
        def _():
            l_ref[...] = jnp.zeros_like(l_ref)

        e = x_ref[...] - t_ref[...]
        dy_ref[...] = e * (1.0 / d)
        l_ref[...] += jnp.sum(e * e) * (0.5 / d)

    return pl.pallas_call(
        body, name=name, grid=(s // tm,),
        in_specs=[_row_spec(tm, d), _row_spec(tm, d)],
        out_specs=[_row_spec(tm, d), pl.BlockSpec((8, LANES), lambda i: (0, 0))],
        out_shape=[jax.ShapeDtypeStruct((s, d), F32), jax.ShapeDtypeStruct((8, LANES), F32)],
    )(xf, tgt)


def _attn_specs(n_q, n_kv):
    aw, kvd = n_q * HEAD_DIM, n_kv * HEAD_DIM
    assert aw % kvd == 0
    kcol = aw // kvd
    q = pl.BlockSpec((WINDOW, aw), lambda n: (n, 0))
    kc = pl.BlockSpec((WINDOW, kvd), lambda n: (n, kcol))
    kp = pl.BlockSpec((WINDOW, kvd), lambda n: (jnp.maximum(n - 1, 0), kcol))
    vc = pl.BlockSpec((WINDOW, kvd), lambda n: (n, kcol + 1))
    vp = pl.BlockSpec((WINDOW, kvd), lambda n: (jnp.maximum(n - 1, 0), kcol + 1))
    return [q, kc, kp, vc, vp]


def _band_mask(n, n_heads):
    qi = lax.broadcasted_iota(jnp.int32, (n_heads * WINDOW, 2 * WINDOW), 0) & (WINDOW - 1)
    kj = lax.broadcasted_iota(jnp.int32, (n_heads * WINDOW, 2 * WINDOW), 1)
    return (kj > qi) & (kj <= qi + WINDOW) & ((kj >= WINDOW) | (n > 0))


def _stack_heads(ref, heads):
    return jnp.concatenate([ref[:, h * HEAD_DIM:(h + 1) * HEAD_DIM] for h in heads], axis=0)


def _stack_sinks(ref, heads):
    return jnp.concatenate([jnp.broadcast_to(ref[:, h:h + 1], (WINDOW, 1)) for h in heads], axis=0)


_NT = (((1,), (1,)), ((), ()))
_TN = (((0,), (0,)), ((), ()))


def _attn_fwd(proj, sinks, *, n_q, n_kv, name):
    s = proj.shape[0]
    aw, grp = n_q * HEAD_DIM, n_q // n_kv

    def body(q_ref, kc_ref, kp_ref, vc_ref, vp_ref, sink_ref, o_ref, lse_ref):
        valid = _band_mask(pl.program_id(0), grp)
        kb = jnp.concatenate([kp_ref[...], kc_ref[...]], axis=0).astype(BF16)
        vb = jnp.concatenate([vp_ref[...], vc_ref[...]], axis=0).astype(BF16)
        lse_ref[...] = jnp.zeros_like(lse_ref)
        for g in range(n_kv):
            heads = range(g * grp, (g + 1) * grp)
            gs = slice(g * HEAD_DIM, (g + 1) * HEAD_DIM)
            qg = _stack_heads(q_ref, heads).astype(BF16)
            sink = _stack_sinks(sink_ref, heads)
            sc = lax.dot_general(qg, kb[:, gs], _NT, preferred_element_type=F32)
            sc = jnp.where(valid, sc * (HEAD_DIM ** -0.5), NEG)
            m = jnp.maximum(jnp.max(sc, axis=-1, keepdims=True), sink)
            e = jnp.exp(sc - m)
            den = jnp.sum(e, axis=-1, keepdims=True) + jnp.exp(sink - m)
            p = e * (1.0 / den)
            og = jnp.dot(p.astype(BF16), vb[:, gs], preferred_element_type=F32)
            lse = m + jnp.log(den)
            for i, h in enumerate(heads):
                rows = slice(i * WINDOW, (i + 1) * WINDOW)
                o_ref[:, h * HEAD_DIM:(h + 1) * HEAD_DIM] = og[rows]
                lse_ref[:, h:h + 1] = lse[rows]

    return pl.pallas_call(
        body, name=name, grid=(s // WINDOW,),
        in_specs=_attn_specs(n_q, n_kv) + [pl.BlockSpec((1, LANES), lambda n: (0, 0))],
        out_specs=[pl.BlockSpec((WINDOW, aw), lambda n: (n, 0)), pl.BlockSpec((WINDOW, LANES), lambda n: (n, 0))],
        out_shape=[jax.ShapeDtypeStruct((s, aw), F32), jax.ShapeDtypeStruct((s, LANES), F32)],
    )(proj, proj, proj, proj, proj, sinks)


def _attn_bwd(proj, sinks, out, lse, dout, *, n_q, n_kv, name):
    s = proj.shape[0]
    aw, kvd, grp = n_q * HEAD_DIM, n_kv * HEAD_DIM, n_q // n_kv
    scale = HEAD_DIM ** -0.5

    def body(q_ref, kc_ref, kp_ref, vc_ref, vp_ref, sink_ref, o_ref, lse_ref, do_ref,
             dq_ref, dk_ref, dv_ref, dsink_ref):
        n = pl.program_id(0)

        @pl.when(n == 0)
        def _():
            dk_ref[...] = jnp.zeros_like(dk_ref)
            dv_ref[...] = jnp.zeros_like(dv_ref)
            dsink_ref[...] = jnp.zeros_like(dsink_ref)

        valid = _band_mask(n, grp)
        kb = jnp.concatenate([kp_ref[...], kc_ref[...]], axis=0).astype(BF16)
        vb = jnp.concatenate([vp_ref[...], vc_ref[...]], axis=0).astype(BF16)
        lane = lax.broadcasted_iota(jnp.int32, (8, LANES), 1)
        dsink = jnp.zeros((8, LANES), F32)
        cur = pl.ds(pl.multiple_of(n * WINDOW, WINDOW), WINDOW)
        prev = pl.ds(pl.multiple_of(jnp.maximum(n - 1, 0) * WINDOW, WINDOW), WINDOW)
        for g in range(n_kv):
            heads = range(g * grp, (g + 1) * grp)
            gs = slice(g * HEAD_DIM, (g + 1) * HEAD_DIM)
            qg = _stack_heads(q_ref, heads).astype(BF16)
            do = _stack_heads(do_ref, heads)
            dob = do.astype(BF16)
            lse = jnp.concatenate([lse_ref[:, h:h + 1] for h in heads], axis=0)
            sc = lax.dot_general(qg, kb[:, gs], _NT, preferred_element_type=F32)
            sc = jnp.where(valid, sc * scale, NEG)
            p = jnp.exp(sc - lse)
            delta = jnp.sum(do * _stack_heads(o_ref, heads), axis=-1, keepdims=True)
            dp = lax.dot_general(dob, vb[:, gs], _NT, preferred_element_type=F32)
            ds = (p * (dp - delta) * scale).astype(BF16)
            dqg = jnp.dot(ds, kb[:, gs], preferred_element_type=F32)
            dkb = lax.dot_general(ds, qg, _TN, preferred_element_type=F32)
            dvb = lax.dot_general(p.astype(BF16), dob, _TN, preferred_element_type=F32)
            sink_term = jnp.exp(_stack_sinks(sink_ref, heads) - lse) * delta
            for i, h in enumerate(heads):
                rows = slice(i * WINDOW, (i + 1) * WINDOW)
                dq_ref[:, h * HEAD_DIM:(h + 1) * HEAD_DIM] = dqg[rows]
                dsink = dsink + jnp.where(lane == h, -jnp.sum(sink_term[rows]), 0.0)
            dk_ref[cur, gs] += dkb[WINDOW:]
            dv_ref[cur, gs] += dvb[WINDOW:]

            @pl.when(n > 0)
            def _():
                dk_ref[prev, gs] += dkb[:WINDOW]
                dv_ref[prev, gs] += dvb[:WINDOW]

        dsink_ref[...] += dsink

    blk = pl.BlockSpec((WINDOW, aw), lambda n: (n, 0))
    kv_full = pl.BlockSpec((s, kvd), lambda n: (0, 0))
    return pl.pallas_call(
        body, name=name, grid=(s // WINDOW,),
        in_specs=_attn_specs(n_q, n_kv) + [pl.BlockSpec((1, LANES), lambda n: (0, 0)), blk,
                                           pl.BlockSpec((WINDOW, LANES), lambda n: (n, 0)), blk],
        out_specs=[blk, kv_full, kv_full, pl.BlockSpec((8, LANES), lambda n: (0, 0))],
        out_shape=[jax.ShapeDtypeStruct((s, aw), F32), jax.ShapeDtypeStruct((s, kvd), F32),
                   jax.ShapeDtypeStruct((s, kvd), F32), jax.ShapeDtypeStruct((8, LANES), F32)],
    )(proj, proj, proj, proj, proj, sinks, out, lse, dout)


def _disc(lr, li, ls):
    dt = jnp.exp(ls)
    mag = jnp.exp(lr * dt)
    ang = li * dt
    ab_re, ab_im = mag * jnp.cos(ang), mag * jnp.sin(ang)
    den = lr * lr + li * li
    f_re = ((ab_re - 1.0) * lr + ab_im * li) / den
    f_im = (ab_im * lr - (ab_re - 1.0) * li) / den
    return ab_re, ab_im, f_re, f_im


POW_ROWS = 8
SUB = 8
TAB_ROWS = POW_ROWS + 2 * SUB


def _ssm_params_fwd(lr, li, ls, b_re, b_im, name):
    gp = lr.shape[1]
    h = b_re.shape[0]

    def body(lr_ref, li_ref, ls_ref, br_ref, bi_ref, bbr_ref, bbi_ref, tr_ref, ti_ref):
        ab_re, ab_im, f_re, f_im = _disc(lr_ref[...], li_ref[...], ls_ref[...])
        br, bi = br_ref[...], bi_ref[...]
        bbr_ref[...] = f_re * br - f_im * bi
        bbi_ref[...] = f_re * bi + f_im * br
        pr, pi = ab_re, ab_im
        for i in range(POW_ROWS):
            tr_ref[i:i + 1, :] = pr
            ti_ref[i:i + 1, :] = pi
            pr, pi = pr * pr - pi * pi, 2.0 * pr * pi
        pr, pi = ab_re, ab_im
        for r in range(SUB):
            for row in (POW_ROWS + r, POW_ROWS + 2 * SUB - 1 - r):
                tr_ref[row:row + 1, :] = pr
                ti_ref[row:row + 1, :] = pi
            pr, pi = pr * ab_re - pi * ab_im, pr * ab_im + pi * ab_re

    mat, tab = jax.ShapeDtypeStruct((h, gp), F32), jax.ShapeDtypeStruct((TAB_ROWS, gp), F32)
    return pl.pallas_call(body, name=name, out_shape=[mat, mat, tab, tab])(lr, li, ls, b_re, b_im)


def _ssm_params_bwd(lr, li, ls, b_re, b_im, dab_re, dab_im, dbb_re, dbb_im, seg, name):
    gp = lr.shape[1]
    h = b_re.shape[0]

    def body(lr_ref, li_ref, ls_ref, br_ref, bi_ref, dar_ref, dai_ref, dbbr_ref, dbbi_ref, seg_ref,
             dlr_ref, dli_ref, dls_ref, dbr_ref, dbi_ref):
        lr_, li_, ls_ = lr_ref[...], li_ref[...], ls_ref[...]
        (ab_re, ab_im, f_re, f_im), vjp = jax.vjp(_disc, lr_, li_, ls_)
        br, bi, dbbr, dbbi = br_ref[...], bi_ref[...], dbbr_ref[...], dbbi_ref[...]
        dbr_ref[...] = dbbr * f_re + dbbi * f_im
        dbi_ref[...] = dbbi * f_re - dbbr * f_im
        df_re = jnp.sum(dbbr * br + dbbi * bi, axis=0, keepdims=True)
        df_im = jnp.sum(dbbi * br - dbbr * bi, axis=0, keepdims=True)
        dlr, dli, dls = vjp((dar_ref[...], dai_ref[...], df_re, df_im))
        dlr_ref[...] = dlr
        dli_ref[...] = dli
        dls8 = jnp.broadcast_to(dls, (8, gp))
        dls_ref[...] = jnp.dot(dls8, seg_ref[...], preferred_element_type=F32, precision=lax.Precision.HIGHEST)

    vec, mat = jax.ShapeDtypeStruct((1, gp), F32), jax.ShapeDtypeStruct((h, gp), F32)
    return pl.pallas_call(body, name=name,
                          out_shape=[vec, vec, jax.ShapeDtypeStruct((8, seg.shape[1]), F32), mat, mat],
                          compiler_params=_params(24 << 20))(
        lr, li, ls, b_re, b_im, dab_re, dab_im, dbb_re, dbb_im, seg)


def _scan_bufs(t_len):
    hs = BLOCK_STATES
    return [pltpu.VMEM((hs // LANES, t_len, LANES), F32), pltpu.VMEM((hs // LANES, t_len, LANES), F32),
            pltpu.VMEM((t_len // SUB, hs), F32), pltpu.VMEM((t_len // SUB, hs), F32)]


def _scan(xr, xi, apow_ref, bufs, t_len, reverse):
    hs = BLOCK_STATES
    n_tiles = t_len // SUB
    sr_ref, si_ref, er_ref, ei_ref = bufs

    def doubling(xr, xi, n_rows, first_pow, within):
        row = lax.broadcasted_iota(jnp.int32, xr.shape, 0) & (within - 1)
        d = 1
        while d < within:
            i = first_pow + d.bit_length() - 1
            pr, pi = apow_ref[i:i + 1, :hs], apow_ref[i:i + 1, hs:]
            if reverse:
                pi, shift, keep = -pi, n_rows - d, row < within - d
            else:
                shift, keep = d, row >= d
            sr = jnp.where(keep, pltpu.roll(xr, shift, 0), 0.0)
            si = jnp.where(keep, pltpu.roll(xi, shift, 0), 0.0)
            xr, xi = xr + pr * sr - pi * si, xi + pr * si + pi * sr
            d *= 2
        return xr, xi

    shape3 = (n_tiles, SUB, hs)
    row = lax.broadcasted_iota(jnp.int32, shape3, 1)
    xr, xi = xr.reshape(shape3), xi.reshape(shape3)
    for i, d in enumerate((1, 2, 4)):
        pr, pi = apow_ref[i:i + 1, :hs], apow_ref[i:i + 1, hs:]
        if reverse:
            pi, shift, keep = -pi, SUB - d, row < SUB - d
        else:
            shift, keep = d, row >= d
        sr = jnp.where(keep, pltpu.roll(xr, shift, 1), 0.0)
        si = jnp.where(keep, pltpu.roll(xi, shift, 1), 0.0)
        xr, xi = xr + pr * sr - pi * si, xi + pr * si + pi * sr
    xr, xi = xr.reshape(t_len, hs), xi.reshape(t_len, hs)
    chunks = [slice(c * LANES, (c + 1) * LANES) for c in range(hs // LANES)]
    for c, lanes in enumerate(chunks):
        sr_ref[c] = xr[:, lanes]
        si_ref[c] = xi[:, lanes]
    edge = pl.ds(0 if reverse else SUB - 1, n_tiles, stride=SUB)
    tr, ti = doubling(jnp.concatenate([sr_ref[c, edge, :] for c in range(len(chunks))], axis=1),
                      jnp.concatenate([si_ref[c, edge, :] for c in range(len(chunks))], axis=1), n_tiles, 3, n_tiles)
    trow = lax.broadcasted_iota(jnp.int32, tr.shape, 0)
    if reverse:
        shift, keep = n_tiles - 1, trow < n_tiles - 1
    else:
        shift, keep = 1, trow >= 1
    er_ref[...] = jnp.where(keep, pltpu.roll(tr, shift, 0), 0.0)
    ei_ref[...] = jnp.where(keep, pltpu.roll(ti, shift, 0), 0.0)
    lin = POW_ROWS + SUB if reverse else POW_ROWS
    mr, mi = apow_ref[lin:lin + SUB, :hs], apow_ref[lin:lin + SUB, hs:]
    if reverse:
        mi = -mi
    for t in range(n_tiles):
        rows = slice(t * SUB, (t + 1) * SUB)
        er, ei = er_ref[t:t + 1, :], ei_ref[t:t + 1, :]
        add_r, add_i = mr * er - mi * ei, mr * ei + mi * er
        for c, lanes in enumerate(chunks):
            sr_ref[c, rows, :] += add_r[:, lanes]
            si_ref[c, rows, :] += add_i[:, lanes]
    return (jnp.concatenate([sr_ref[c] for c in range(len(chunks))], axis=1),
            jnp.concatenate([si_ref[c] for c in range(len(chunks))], axis=1))


def _ssm_chunk(s):
    t_len = _pick(s, 256, 8)
    assert t_len & (t_len - 1) == 0 and t_len <= 1 << POW_ROWS, t_len
    return t_len


def _fold_carry(br, bi, carry_ref, apow_ref, at_row, conj):
    hs = BLOCK_STATES
    cr, ci = carry_ref[0:1, :hs], carry_ref[0:1, hs:]
    ar, ai = apow_ref[0:1, :hs], apow_ref[0:1, hs:]
    if conj:
        ai = -ai
    here = lax.broadcasted_iota(jnp.int32, br.shape, 0) == at_row
    return jnp.where(here, br + (ar * cr - ai * ci), br), jnp.where(here, bi + (ar * ci + ai * cr), bi)


def _ssm_fwd(proj, ucol, bbd, ccat, dskip, apow, t_len, *, name):
    s = proj.shape[0]
    nb = bbd.shape[0]
    nc = s // t_len
    hs = BLOCK_STATES

    def body(u_ref, bbd_ref, ccat_ref, d_ref, apow_ref, y_ref, z_ref, xs_ref, carry_ref, *bufs):
        @pl.when(pl.program_id(1) == 0)
        def _():
            carry_ref[...] = jnp.zeros_like(carry_ref)

        xs_ref[...] = carry_ref[...]
        u = u_ref[...]
        bu = jnp.dot(u.astype(BF16), bbd_ref[...], preferred_element_type=F32)
        br, bi = _fold_carry(bu[:, :hs], bu[:, hs:], carry_ref, apow_ref, 0, False)
        xr, xi = _scan(br, bi, apow_ref, bufs, t_len, False)
        xcat = jnp.concatenate([xr, xi], axis=1)
        carry_ref[...] = jnp.broadcast_to(xcat[t_len - 1:t_len, :], carry_ref.shape)
        y = jnp.dot(xcat.astype(BF16), ccat_ref[...], preferred_element_type=F32) + d_ref[...] * u
        y_ref[...] = y
        z_ref[...] = _gelu(y).astype(z_ref.dtype)

    return pl.pallas_call(
        body, name=name, grid=(nb, nc),
        in_specs=[pl.BlockSpec((t_len, LANES), lambda j, n: (n, ucol + j)),
                  pl.BlockSpec((None, LANES, 2 * hs), lambda j, n: (j, 0, 0)),
                  pl.BlockSpec((None, 2 * hs, LANES), lambda j, n: (j, 0, 0)),
                  pl.BlockSpec((1, LANES), lambda j, n: (0, j)),
                  pl.BlockSpec((None, TAB_ROWS, 2 * hs), lambda j, n: (j, 0, 0))],
        out_specs=[pl.BlockSpec((t_len, LANES), lambda j, n: (n, j)),
                   pl.BlockSpec((t_len, LANES), lambda j, n: (n, j)),
                   pl.BlockSpec((None, None, 8, 2 * hs), lambda j, n: (j, n, 0, 0))],
        out_shape=[jax.ShapeDtypeStruct((s, nb * LANES), F32), jax.ShapeDtypeStruct((s, nb * LANES), BF16),
                   jax.ShapeDtypeStruct((nb, nc, 8, 2 * hs), F32)],
        scratch_shapes=[pltpu.VMEM((8, 2 * hs), F32)] + _scan_bufs(t_len),
        compiler_params=_params(40 << 20),
    )(proj, bbd, ccat, dskip, apow)


def _ssm_bwd(proj, ucol, y, dzd, dz2, xs, bbd, ccat, dskip, apow, t_len, *, name):
    s = proj.shape[0]
    nb = bbd.shape[0]
    nc = s // t_len
    hs = BLOCK_STATES

    def body(u_ref, y_ref, dzd_ref, dz2_ref, xs_ref, bbd_ref, ccat_ref, d_ref, apow_ref,
             du_ref, dbbd_ref, dccat_ref, dd_ref, da_ref, gcarry_ref, *bufs):
        @pl.when(pl.program_id(1) == 0)
        def _():
            gcarry_ref[...] = jnp.zeros_like(gcarry_ref)
            dbbd_ref[...] = jnp.zeros_like(dbbd_ref)
            dccat_ref[...] = jnp.zeros_like(dccat_ref)
            dd_ref[...] = jnp.zeros_like(dd_ref)
            da_ref[...] = jnp.zeros_like(da_ref)

        u = u_ref[...]
        ub = u.astype(BF16)
        dy = (dzd_ref[...] + dz2_ref[...]) * _gelu_grad(y_ref[...])
        dyb = dy.astype(BF16)
        bu = jnp.dot(ub, bbd_ref[...], preferred_element_type=F32)
        br, bi = _fold_carry(bu[:, :hs], bu[:, hs:], xs_ref, apow_ref, 0, False)
        xr, xi = _scan(br, bi, apow_ref, bufs[:4], t_len, False)
        sr, si = xs_ref[0:1, :hs], xs_ref[0:1, hs:]
        dxd = lax.dot_general(dyb, ccat_ref[...], _NT, preferred_element_type=F32)
        dr, di = _fold_carry(dxd[:, :hs], dxd[:, hs:], gcarry_ref, apow_ref, t_len - 1, True)
        gr, gi = _scan(dr, di, apow_ref, bufs[4:], t_len, True)
        gcat = jnp.concatenate([gr, gi], axis=1)
        gcarry_ref[...] = jnp.broadcast_to(gcat[0:1, :], gcarry_ref.shape)
        gb = gcat.astype(BF16)
        du_ref[...] = lax.dot_general(gb, bbd_ref[...], _NT, preferred_element_type=F32) + d_ref[...] * dy
        dbbd_ref[...] += lax.dot_general(ub, gb, _TN, preferred_element_type=F32)
        xb = jnp.concatenate([xr, xi], axis=1).astype(BF16)
        dccat_ref[...] += lax.dot_general(xb, dyb, _TN, preferred_element_type=F32)
        dd_ref[...] += jnp.sum(dy * u, axis=0, keepdims=True)
        first = lax.broadcasted_iota(jnp.int32, xr.shape, 0) == 0
        xpr = jnp.where(first, sr, pltpu.roll(xr, 1, 0))
        xpi = jnp.where(first, si, pltpu.roll(xi, 1, 0))
        dar = jnp.sum(gr * xpr + gi * xpi, axis=0, keepdims=True)
        dai = jnp.sum(gi * xpr - gr * xpi, axis=0, keepdims=True)
        da_ref[...] += jnp.concatenate([dar, dai], axis=1)

    def rows(j, n):
        return nc - 1 - n

    chunk = pl.BlockSpec((t_len, LANES), lambda j, n: (rows(j, n), j))
    return pl.pallas_call(
        body, name=name, grid=(nb, nc),
        in_specs=[pl.BlockSpec((t_len, LANES), lambda j, n: (rows(j, n), ucol + j)), chunk, chunk, chunk,
                  pl.BlockSpec((None, None, 8, 2 * hs), lambda j, n: (j, rows(j, n), 0, 0)),
                  pl.BlockSpec((None, LANES, 2 * hs), lambda j, n: (j, 0, 0)),
                  pl.BlockSpec((None, 2 * hs, LANES), lambda j, n: (j, 0, 0)),
                  pl.BlockSpec((1, LANES), lambda j, n: (0, j)),
                  pl.BlockSpec((None, TAB_ROWS, 2 * hs), lambda j, n: (j, 0, 0))],
        out_specs=[chunk,
                   pl.BlockSpec((None, LANES, 2 * hs), lambda j, n: (j, 0, 0)),
                   pl.BlockSpec((None, 2 * hs, LANES), lambda j, n: (j, 0, 0)),
                   pl.BlockSpec((1, LANES), lambda j, n: (0, j)),
                   pl.BlockSpec((None, 1, 2 * hs), lambda j, n: (j, 0, 0))],
        out_shape=[jax.ShapeDtypeStruct((s, nb * LANES), F32),
                   jax.ShapeDtypeStruct((nb, LANES, 2 * hs), F32),
                   jax.ShapeDtypeStruct((nb, 2 * hs, LANES), F32),
                   jax.ShapeDtypeStruct((1, nb * LANES), F32),
                   jax.ShapeDtypeStruct((nb, 1, 2 * hs), F32)],
        scratch_shapes=[pltpu.VMEM((8, 2 * hs), F32)] + _scan_bufs(t_len) + _scan_bufs(t_len),
        compiler_params=_params(48 << 20),
    )(proj, y, dzd, dz2, xs, bbd, ccat, dskip, apow)


def _to_blocks(a):
    g, p, k = a.shape
    nb = g // GROUPS_PER_BLOCK
    eye = jnp.eye(GROUPS_PER_BLOCK, dtype=a.dtype)
    a4 = a.reshape(nb, GROUPS_PER_BLOCK, p, k)
    out = jnp.einsum("ab,jbpk->jakbp", eye, a4)
    return out.reshape(nb, GROUPS_PER_BLOCK * k, GROUPS_PER_BLOCK * p)


def _from_blocks(d, p, k):
    nb = d.shape[0]
    d5 = d.reshape(nb, GROUPS_PER_BLOCK, k, GROUPS_PER_BLOCK, p)
    eye = jnp.eye(GROUPS_PER_BLOCK, dtype=bool)[None, :, None, :, None]
    diag = jnp.sum(jnp.where(eye, d5, 0.0), axis=1)
    return jnp.transpose(diag, (0, 2, 3, 1)).reshape(nb * GROUPS_PER_BLOCK, p, k)


def _merge_fwd(attn, y, gl, g_a, g_s, name):
    s, wa = attn.shape
    ws = y.shape[1]
    tm = _pick(s, 256, 16)

    def body(a_ref, y_ref, gl_ref, ga_ref, gs_ref, o_ref):
        av = a_ref[...]
        o_ref[:, :wa] = ((av * _rstd(av)) * ga_ref[...]).astype(o_ref.dtype)
        sv = _gelu(y_ref[...]) * jax.nn.sigmoid(gl_ref[...])
        o_ref[:, wa:] = ((sv * _rstd(sv)) * gs_ref[...]).astype(o_ref.dtype)

    return pl.pallas_call(
        body, name=name, grid=(s // tm,),
        in_specs=[_row_spec(tm, wa), _row_spec(tm, ws), _row_spec(tm, ws), _vec_spec(wa), _vec_spec(ws)],
        out_specs=_row_spec(tm, wa + ws), out_shape=jax.ShapeDtypeStruct((s, wa + ws), BF16),
    )(attn, y, gl, g_a, g_s)


def _merge_bwd(dmerged, attn, y, gl, g_a, g_s, name):
    s, wa = attn.shape
    ws = y.shape[1]
    tm = _pick(s, 256, 16)

    def body(dm_ref, a_ref, y_ref, gl_ref, ga_ref, gs_ref, da_ref, dgl_ref, dzd_ref, dga_ref, dgs_ref):
        @pl.when(pl.program_id(0) == 0)
        def _():
            dga_ref[...] = jnp.zeros_like(dga_ref)
            dgs_ref[...] = jnp.zeros_like(dgs_ref)

        dan, dsn = dm_ref[:, :wa], dm_ref[:, wa:]
        av = a_ref[...]
        ra = _rstd(av)
        ahat = av * ra
        dga_ref[...] += jnp.sum(dan * ahat, axis=0, keepdims=True)
        da_ref[...] = _norm_bwd(dan * ga_ref[...], ahat, ra)
        z = _gelu(y_ref[...])
        sig = jax.nn.sigmoid(gl_ref[...])
        sv = z * sig
        rs = _rstd(sv)
        shat = sv * rs
        dgs_ref[...] += jnp.sum(dsn * shat, axis=0, keepdims=True)
        dssm = _norm_bwd(dsn * gs_ref[...], shat, rs)
        dzd_ref[...] = dssm * sig
        dgl_ref[...] = (dssm * z * sig * (1.0 - sig)).astype(dgl_ref.dtype)

    return pl.pallas_call(
        body, name=name, grid=(s // tm,),
        in_specs=[_row_spec(tm, wa + ws), _row_spec(tm, wa), _row_spec(tm, ws), _row_spec(tm, ws),
                  _vec_spec(wa), _vec_spec(ws)],
        out_specs=[_row_spec(tm, wa), _row_spec(tm, ws), _row_spec(tm, ws), _vec_spec(wa), _vec_spec(ws)],
        out_shape=[jax.ShapeDtypeStruct((s, wa), F32), jax.ShapeDtypeStruct((s, ws), BF16),
                   jax.ShapeDtypeStruct((s, ws), F32), jax.ShapeDtypeStruct((1, wa), F32),
                   jax.ShapeDtypeStruct((1, ws), F32)],
    )(dmerged, attn, y, gl, g_a, g_s)


def _shift_down(main, halo, k):
    rolled = pltpu.roll(main, k, 0)
    row = lax.broadcasted_iota(jnp.int32, main.shape, 0)
    for r in range(k):
        rolled = jnp.where(row == r, halo[8 - k + r:8 - k + r + 1, :], rolled)
    return rolled


def _shift_up(main, halo, k):
    tm = main.shape[0]
    rolled = pltpu.roll(main, tm - k, 0)
    row = lax.broadcasted_iota(jnp.int32, main.shape, 0)
    for r in range(k):
        rolled = jnp.where(row == tm - k + r, halo[r:r + 1, :], rolled)
    return rolled


def _conv(main, halo, w_ref, b_ref):
    return (b_ref[...] + w_ref[0:1, :] * _shift_down(main, halo, 2) + w_ref[1:2, :] * _shift_down(main, halo, 1)
            + w_ref[2:3, :] * main)


def _gate_tiles(s, f):
    return _pick(s, 512, 16), _pick(f, 512, LANES)


def _gate_in_specs(tm, tn, nfb, order):
    hb = tm // 8
    ij = (lambda a, b: (b, a)) if order == "ji" else (lambda a, b: (a, b))

    def main(off):
        return pl.BlockSpec((tm, tn), lambda a, b: (ij(a, b)[0], ij(a, b)[1] + off))

    def halo(off):
        return pl.BlockSpec((8, tn), lambda a, b: (jnp.maximum(ij(a, b)[0] * hb - 1, 0), ij(a, b)[1] + off))

    def vec(rows, off):
        return pl.BlockSpec((rows, tn), lambda a, b: (0, ij(a, b)[1] + off))

    return [main(0), main(nfb), halo(0), halo(nfb), vec(3, 0), vec(3, nfb), vec(1, 0), vec(1, nfb)]


def _gate_fwd(up0, conv_w, conv_b, name):
    s, f2 = up0.shape
    f = f2 // 2
    tm, tn = _gate_tiles(s, f)
    nfb = f // tn

    def body(v_ref, g_ref, vh_ref, gh_ref, wv_ref, wg_ref, bv_ref, bg_ref, o_ref):
        top = pl.program_id(0) == 0
        vh = jnp.where(top, 0.0, vh_ref[...])
        gh = jnp.where(top, 0.0, gh_ref[...])
        val = _conv(v_ref[...], vh, wv_ref, bv_ref)
        gate = _conv(g_ref[...], gh, wg_ref, bg_ref)
        o_ref[...] = (_gelu(gate) * val).astype(o_ref.dtype)

    return pl.pallas_call(
        body, name=name, grid=(s // tm, nfb),
        in_specs=_gate_in_specs(tm, tn, nfb, "ij"), out_specs=pl.BlockSpec((tm, tn), lambda i, j: (i, j)),
        out_shape=jax.ShapeDtypeStruct((s, f), BF16),
        compiler_params=_params(24 * tm * tn * 4 + (4 << 20)),
    )(up0, up0, up0, up0, conv_w, conv_w, conv_b, conv_b)


def _gate_bwd(up0, conv_w, conv_b, da, name):
    s, f2 = up0.shape
    f = f2 // 2
    tm, tn = _gate_tiles(s, f)
    nfb = f // tn

    def body(v_ref, g_ref, vh_ref, gh_ref, wv_ref, wg_ref, bv_ref, bg_ref, da_ref, dup_ref, dcb_ref, dcw_ref):
        top = pl.program_id(1) == 0

        @pl.when(top)
        def _():
            dcb_ref[...] = jnp.zeros_like(dcb_ref)
            dcw_ref[...] = jnp.zeros_like(dcw_ref)

        halos = (jnp.where(top, 0.0, vh_ref[...]), jnp.where(top, 0.0, gh_ref[...]))
        mains = (v_ref[...], g_ref[...])
        val = _conv(mains[0], halos[0], wv_ref, bv_ref)
        gate = _conv(mains[1], halos[1], wg_ref, bg_ref)
        dav = da_ref[...]
        act, act_grad = _gelu_and_grad(gate)
        dups = (dav * act, (dav * val) * act_grad)
        for half in range(2):
            dup = dups[half]
            dup_ref[half] = dup
            dcb_ref[half] += jnp.sum(dup, axis=0, keepdims=True)
            dcw_ref[half, 0:1, :] += jnp.sum(dup * _shift_down(mains[half], halos[half], 2), axis=0, keepdims=True)
            dcw_ref[half, 1:2, :] += jnp.sum(dup * _shift_down(mains[half], halos[half], 1), axis=0, keepdims=True)
            dcw_ref[half, 2:3, :] += jnp.sum(dup * mains[half], axis=0, keepdims=True)

    return pl.pallas_call(
        body, name=name, grid=(nfb, s // tm),
        in_specs=_gate_in_specs(tm, tn, nfb, "ji") + [pl.BlockSpec((tm, tn), lambda j, i: (i, j))],
        out_specs=[pl.BlockSpec((2, tm, tn), lambda j, i: (0, i, j)),
                   pl.BlockSpec((2, 1, tn), lambda j, i: (0, 0, j)),
                   pl.BlockSpec((2, 3, tn), lambda j, i: (0, 0, j))],
        out_shape=[jax.ShapeDtypeStruct((2, s, f), F32), jax.ShapeDtypeStruct((2, 1, f), F32),
                   jax.ShapeDtypeStruct((2, 3, f), F32)],
        compiler_params=_params(40 * tm * tn * 4 + (4 << 20)),
    )(up0, up0, up0, up0, conv_w, conv_w, conv_b, conv_b, da)


def _conv_bwd(dup, conv_w, name):
    _, s, f = dup.shape
    tm, tn = _pick(s, 512, 16), _pick(f, 1536, LANES)
    nfb, ni, hb = f // tn, s // tm, tm // 8

    def body(d_ref, dh_ref, w_ref, o_ref):
        main = d_ref[...]
        halo = jnp.where(pl.program_id(1) == ni - 1, 0.0, dh_ref[...])
        o_ref[...] = (w_ref[2:3, :] * main + w_ref[1:2, :] * _shift_up(main, halo, 1)
                      + w_ref[0:1, :] * _shift_up(main, halo, 2)).astype(o_ref.dtype)

    return pl.pallas_call(
        body, name=name, grid=(2, ni, nfb),
        in_specs=[pl.BlockSpec((None, tm, tn), lambda h, i, j: (h, i, j)),
                  pl.BlockSpec((None, 8, tn), lambda h, i, j: (h, jnp.minimum((i + 1) * hb, s // 8 - 1), j)),
                  pl.BlockSpec((3, tn), lambda h, i, j: (0, h * nfb + j))],
        out_specs=pl.BlockSpec((tm, tn), lambda h, i, j: (i, h * nfb + j)),
        out_shape=jax.ShapeDtypeStruct((s, 2 * f), BF16),
        compiler_params=_params(12 * tm * tn * 4 + (4 << 20)),
    )(dup, dup, conv_w)


def _adamw(w, g, m, v, name):
    r, c = w.shape
    tr = _pick(r, max(8, (1 << 19) // max(c, 1) // 8 * 8), 8)
    c1, c2 = 1.0 / (1.0 - ADAM_B1 ** ADAM_STEP), 1.0 / (1.0 - ADAM_B2 ** ADAM_STEP)

    def body(w_ref, g_ref, m_ref, v_ref, d_ref, nm_ref, nv_ref):
        gv = g_ref[...]
        nm = ADAM_B1 * m_ref[...] + (1.0 - ADAM_B1) * gv
        nv = ADAM_B2 * v_ref[...] + (1.0 - ADAM_B2) * (gv * gv)
        nm_ref[...] = nm
        nv_ref[...] = nv
        d_ref[...] = -ADAM_LR * ((nm * c1) / (jnp.sqrt(nv * c2) + ADAM_EPS) + ADAM_WD * w_ref[...])

    spec = pl.BlockSpec((tr, c), lambda i: (i, 0))
    out = jax.ShapeDtypeStruct((r, c), F32)
    return pl.pallas_call(body, name=name, grid=(r // tr,), in_specs=[spec] * 4, out_specs=[spec] * 3,
                          out_shape=[out] * 3, compiler_params=_params(14 * tr * c * 4 + (4 << 20)))(w, g, m, v)


def _adamw_nd(w, g, m, v, name):
    shape = w.shape
    c = shape[-1]
    outs = _adamw(w.reshape(-1, c), g.reshape(-1, c), m.reshape(-1, c), v.reshape(-1, c), name)
    return [o.reshape(shape) for o in outs]


BIG = ("w_in", "w_glu", "w_out", "w_up", "w_down")
SMALL = ("b_ada", "g_pre_mix", "g_post_mix", "attn_sinks", "lam_re", "lam_im", "log_step", "ssm_b_re", "ssm_b_im",
         "ssm_c_re", "ssm_c_im", "ssm_d", "g_attn_out", "g_ssm_out", "g_pre_ffn", "g_post_ffn", "conv_b")
ORDER = ("w_ada", "b_ada", "g_pre_mix", "g_post_mix", "w_in", "attn_sinks", "lam_re", "lam_im", "log_step",
         "ssm_b_re", "ssm_b_im", "ssm_c_re", "ssm_c_im", "ssm_d", "w_glu", "g_attn_out", "g_ssm_out", "w_out",
         "g_pre_ffn", "g_post_ffn", "w_up", "conv_w", "conv_b", "w_down")
COL_SHARDED = ("w_in", "w_up")


def kernel(x, c, w_ada, b_ada, g_pre_mix, g_post_mix, w_in, attn_sinks, lam_re, lam_im, log_step, ssm_b_re, ssm_b_im, ssm_c_re, ssm_c_im, ssm_d, w_glu, g_attn_out, g_ssm_out, w_out, g_pre_ffn, g_post_ffn, w_up, conv_w, conv_b, w_down, loss_target, m_w_ada, m_b_ada, m_g_pre_mix, m_g_post_mix, m_w_in, m_attn_sinks, m_lam_re, m_lam_im, m_log_step, m_ssm_b_re, m_ssm_b_im, m_ssm_c_re, m_ssm_c_im, m_ssm_d, m_w_glu, m_g_attn_out, m_g_ssm_out, m_w_out, m_g_pre_ffn, m_g_post_ffn, m_w_up, m_conv_w, m_conv_b, m_w_down, v_w_ada, v_b_ada, v_g_pre_mix, v_g_post_mix, v_w_in, v_attn_sinks, v_lam_re, v_lam_im, v_log_step, v_ssm_b_re, v_ssm_b_im, v_ssm_c_re, v_ssm_c_im, v_ssm_d, v_w_glu, v_g_attn_out, v_g_ssm_out, v_w_out, v_g_pre_ffn, v_g_post_ffn, v_w_up, v_conv_w, v_conv_b, v_w_down):
    env = dict(locals())
    W = {n: env[n] for n in ORDER}
    M = {n: env["m_" + n] for n in ORDER}
    V = {n: env["v_" + n] for n in ORDER}

    depth = w_ada.shape[0]
    s, d = x.shape[1], x.shape[2]
    xs0 = x.reshape(s, d)
    tgt = loss_target.reshape(s, d)
    attn_w = d // 2
    ssm_w = d - attn_w
    in_cols = w_in.shape[2] * N_DEV
    kv_dim = (in_cols - attn_w - ssm_w) // 2
    n_q, n_kv = attn_w // HEAD_DIM, kv_dim // HEAD_DIM
    n_grp = ssm_w // SSM_GROUP
    nb = ssm_w // LANES
    f = w_down.shape[1] * N_DEV
    ucol = (attn_w + 2 * kv_dim) // LANES
    t_len = _ssm_chunk(s)
    me = 4 * lax.axis_index("x") + 2 * lax.axis_index("y") + lax.axis_index("c")

    def at_block(ref, idx):
        return ref.at[idx]

    def at_rows(n_rows):
        return lambda ref, idx: ref.at[:, pl.ds(pl.multiple_of(idx * n_rows, 8), n_rows), :]

    def at_cols(n_cols):
        return lambda ref, idx: ref.at[:, :, pl.ds(pl.multiple_of(idx * n_cols, LANES), n_cols)]

    w_in_shard = w_in.shape[1:]
    w_in_dense = (math.prod(w_in_shard) // LANES, LANES)
    w_in_first = w_in[0].astype(BF16).reshape(w_in_dense)
    first = _gather_multi([w_in_first, conv_w, c],
                          [(N_DEV,) + w_in_dense, (N_DEV,) + conv_w.shape, (N_DEV,) + c.shape],
                          [at_block, at_block, at_block], "ag_first")
    w_in_full = [_cols_from_blocks(first[0].reshape(N_DEV, 1, *w_in_shard), "w_in_layout0")[0]]
    conv_w_full = jnp.transpose(first[1], (1, 2, 0, 3)).reshape(depth, 3, 2 * f)
    c_all = first[2].reshape(N_DEV, d)

    def at_rows2(n_rows):
        return lambda ref, idx: ref.at[pl.ds(pl.multiple_of(idx * n_rows, 8), n_rows), :]

    def at_cols2(n_cols):
        return lambda ref, idx: ref.at[:, pl.ds(pl.multiple_of(idx * n_cols, LANES), n_cols)]

    def whole(ref, idx):
        return ref

    def gather_kind(n):
        return "blk" if n == "w_in" else "cols" if n in COL_SHARDED else "rows"

    def gather_view(n):
        return {"blk": at_block, "cols": at_cols2(W[n].shape[2]), "rows": at_rows2(W[n].shape[1])}[gather_kind(n)]

    def gather_shape(n):
        _, a, b = W[n].shape
        return {"blk": (N_DEV,) + w_in_dense, "cols": (a, N_DEV * b), "rows": (N_DEV * a, b)}[gather_kind(n)]

    later = [(n, l) for l in range(depth) for n in (BIG if l > 0 else BIG[1:])]
    later_srcs = [W[n][l].astype(BF16).reshape(w_in_dense) if n == "w_in" else W[n][l].astype(BF16)
                  for n, l in later]
    later_views = [gather_view(n) for n, _ in later]
    me_arr = me.astype(jnp.int32).reshape(1)
    lands = [_place_own(me_arr, src, lax.empty(gather_shape(n), BF16), gather_kind(n), f"ag_own_{n}{l}")
             for (n, l), src in zip(later, later_srcs)]
    ag_started, ag_token = _exchange_start(later_srcs, lands, [whole] * len(later), later_views, "ag_start")

    def weights_arrived(names, l, after, name):
        picks = [later.index((n, l)) for n in names]
        _, got = _exchange_wait([ag_started[i] for i in picks], [after], [whole] * len(picks),
                                [later_views[i] for i in picks], name)
        return dict(zip(names, got))

    c_pad = jnp.pad(c_all, ((0, 16 - N_DEV), (0, 0)))
    n_ada = w_ada.shape[2]
    b_shard = lax.dynamic_slice_in_dim(b_ada, me * n_ada, n_ada, axis=1).reshape(depth, 1, n_ada)
    ada_part, c_act = _ada_fwd(c_pad, w_ada, b_shard, "ada_fwd")
    ada_all = _all_gather(ada_part.reshape(depth * 16, n_ada), "ag_ada").reshape(N_DEV, depth, 16, n_ada)
    ada_me = lax.dynamic_index_in_dim(ada_all, me, axis=2, keepdims=False)
    ada = jnp.transpose(ada_me, (1, 0, 2)).reshape(depth, 6, 1, d) + ag_token[0, 0]

    gp = n_grp * STATE

    def hgp(a):
        return jnp.transpose(a, (2, 0, 1)).reshape(SSM_GROUP, gp)

    ssm = []
    for l in range(depth):
        lr, li = lam_re[l].reshape(1, gp), lam_im[l].reshape(1, gp)
        ls = jnp.repeat(log_step[l], STATE).reshape(1, gp)
        br, bi = hgp(ssm_b_re[l]), hgp(ssm_b_im[l])
        bbr, bbi, tab_r, tab_i = _ssm_params_fwd(lr, li, ls, br, bi, f"ssm_params_fwd{l}")
        bb_re = jnp.transpose(bbr.reshape(SSM_GROUP, n_grp, STATE), (1, 2, 0))
        bb_im = jnp.transpose(bbi.reshape(SSM_GROUP, n_grp, STATE), (1, 2, 0))
        bbd = jnp.concatenate([_to_blocks(bb_re), _to_blocks(bb_im)], axis=2).astype(BF16)
        c_re_t = jnp.transpose(ssm_c_re[l], (0, 2, 1))
        c_im_t = jnp.transpose(ssm_c_im[l], (0, 2, 1))
        ccat = jnp.concatenate([jnp.transpose(_to_blocks(c_re_t), (0, 2, 1)),
                                -jnp.transpose(_to_blocks(c_im_t), (0, 2, 1))], axis=1).astype(BF16)

        def tab(t):
            return t.reshape(TAB_ROWS, nb, BLOCK_STATES)

        apow = jnp.transpose(jnp.concatenate([tab(tab_r), tab(tab_i)], axis=2), (1, 0, 2))
        ssm.append(dict(lr=lr, li=li, ls=ls, br=br, bi=bi, bbd=bbd, ccat=ccat, apow=apow,
                        dskip=ssm_d[l].reshape(1, ssm_w)))

    sinks_pad = jnp.pad(attn_sinks, ((0, 0), (0, LANES - n_q)))

    def vec(a):
        return a.reshape(1, -1)

    saved = []
    fw = [dict() for _ in range(depth)]
    xin = xs0
    for l in range(depth):
        sh_m, sc_m, gt_m, sh_f, sc_f, gt_f = (ada[l, i] for i in range(6))
        p = ssm[l]
        h1 = _modnorm_fwd(xin, vec(g_pre_mix[l]), sc_m, sh_m, f"modnorm_mix_fwd{l}")
        if l > 0:
            fw[l].update(weights_arrived(BIG, l, h1, f"ag_wait_layer{l}"))
            blocks = fw[l].pop("w_in")
            w_in_full.append(_cols_from_blocks(blocks.reshape(N_DEV, 1, *w_in_shard), f"w_in_layout{l}")[0])
        proj = _matmul(h1, w_in_full[l], name=f"mm_in{l}")
        attn, lse = _attn_fwd(proj, sinks_pad[l:l + 1], n_q=n_q, n_kv=n_kv, name=f"attn_fwd{l}")
        y, z, xstart = _ssm_fwd(proj, ucol, p["bbd"], p["ccat"], p["dskip"], p["apow"], t_len, name=f"ssm_fwd{l}")
        if l == 0:
            fw[0].update(weights_arrived(("w_glu", "w_out"), 0, z, "ag_wait_mix0"))
        gl = _matmul(z, fw[l]["w_glu"], name=f"mm_glu{l}")
        merged = _merge_fwd(attn, y, gl, vec(g_attn_out[l]), vec(g_ssm_out[l]), f"merge_fwd{l}")
        mix = _matmul(merged, fw[l]["w_out"], name=f"mm_out{l}")
        x2 = _resnorm_fwd(xin, mix, vec(g_post_mix[l]), gt_m, f"resnorm_mix_fwd{l}")
        h2 = _modnorm_fwd(x2, vec(g_pre_ffn[l]), sc_f, sh_f, f"modnorm_ffn_fwd{l}")
        if l == 0:
            fw[0].update(weights_arrived(("w_up",), 0, h2, "ag_wait_up0"))
        up0 = _matmul(h2, fw[l]["w_up"], name=f"mm_up{l}")
        cw, cb = conv_w_full[l], vec(conv_b[l])
        act = _gate_fwd(up0, cw, cb, f"gate_fwd{l}")
        if l == 0:
            fw[0].update(weights_arrived(("w_down",), 0, act, "ag_wait_down0"))
        ff = _matmul(act, fw[l]["w_down"], name=f"mm_down{l}")
        x3 = _resnorm_fwd(x2, ff, vec(g_post_ffn[l]), gt_f, f"resnorm_ffn_fwd{l}")
        saved.append(dict(xin=xin, h1=h1, proj=proj, attn=attn, lse=lse, y=y, z=z, xstart=xstart, gl=gl,
                          merged=merged, mix=mix, x2=x2, h2=h2, up0=up0, act=act, ff=ff))
        xin = x3

    dxo, loss_acc = _loss_bwd(xin, tgt, "loss")
    loss = lax.psum(loss_acc[0, 0], ("x", "y", "c"))

    grads = {n: [None] * depth for n in ORDER}
    dada = [None] * depth
    big_blocks = {n: [None] * depth for n in BIG}
    seg = jnp.pad(jnp.repeat(jnp.eye(n_grp, dtype=F32), STATE, axis=0), ((0, 0), (0, (-n_grp) % LANES)))

    def part_view(n):
        shp = W[n].shape
        if n == "w_in":
            return at_block, "blk"
        if n in COL_SHARDED:
            return at_cols2(shp[2]), "cols"
        return at_rows2(shp[1]), "rows"

    rs_groups = []
    small_order = SMALL + ("conv_w",)
    small_shapes = {n: W[n].shape for n in SMALL}
    small_shapes["conv_w"] = (depth, 3, 2 * f)
    small_started = [None] * depth

    def send_partials(items, name):
        parts = [big_blocks[n][l] for n, l in items]
        lands = [lax.empty((N_DEV,) + (w_in_dense if n == "w_in" else W[n].shape[1:]), BF16) for n, _ in items]
        started, token = _exchange_start(parts, lands, [part_view(n)[0] for n, _ in items],
                                         [at_block] * len(items), name)
        rs_groups.append((items, started, name))
        return token[0, 0]

    order = jnp.zeros((), F32)
    for l in reversed(range(depth)):
        sh_m, sc_m, gt_m, sh_f, sc_f, gt_f = (ada[l, i] for i in range(6))
        gt_f = gt_f + order
        a, p = saved[l], ssm[l]
        cw, cb = conv_w_full[l], vec(conv_b[l])
        dff, dg, dgt_f = _resnorm_bwd(dxo, a["ff"], vec(g_post_ffn[l]), gt_f, f"resnorm_ffn_bwd{l}")
        grads["g_post_ffn"][l] = dg
        dact = _matmul(dff, fw[l]["w_down"], tb=True, name=f"mm_down_dx{l}")
        big_blocks["w_down"][l] = _matmul(a["act"], dff, ta=True, out_dtype=BF16, name=f"mm_down_dw{l}")
        dup, dcb, dcw = _gate_bwd(a["up0"], cw, cb, dact, f"gate_bwd{l}")
        grads["conv_b"][l] = dcb.reshape(1, 2 * f)
        grads["conv_w"][l] = jnp.transpose(dcw, (1, 0, 2)).reshape(3, 2 * f)
        dup0 = _conv_bwd(dup, cw, f"conv_bwd{l}")
        dh2 = _matmul(dup0, fw[l]["w_up"], tb=True, name=f"mm_up_dx{l}")
        big_blocks["w_up"][l] = _matmul(a["h2"], dup0, ta=True, out_dtype=BF16, name=f"mm_up_dw{l}")
        if l == 0:
            sc_f = sc_f + send_partials([("w_down", 0), ("w_up", 0)], "rs_start_ffn0")
        dx2, dg, dsc_f, dsh_f = _modnorm_bwd(dh2, a["x2"], vec(g_pre_ffn[l]), sc_f, dxo, f"modnorm_ffn_bwd{l}")
        grads["g_pre_ffn"][l] = dg
        dmix, dg, dgt_m = _resnorm_bwd(dx2, a["mix"], vec(g_post_mix[l]), gt_m, f"resnorm_mix_bwd{l}")
        grads["g_post_mix"][l] = dg
        dmerged = _matmul(dmix, fw[l]["w_out"], tb=True, name=f"mm_out_dx{l}")
        big_blocks["w_out"][l] = _matmul(a["merged"], dmix, ta=True, out_dtype=BF16, name=f"mm_out_dw{l}")
        dattn, dgl, dzd, dga, dgs = _merge_bwd(dmerged, a["attn"], a["y"], a["gl"], vec(g_attn_out[l]),
                                               vec(g_ssm_out[l]), f"merge_bwd{l}")
        grads["g_attn_out"][l], grads["g_ssm_out"][l] = dga, dgs
        dz2 = _matmul(dgl, fw[l]["w_glu"], tb=True, name=f"mm_glu_dx{l}")
        big_blocks["w_glu"][l] = _matmul(a["z"], dgl, ta=True, out_dtype=BF16, name=f"mm_glu_dw{l}")
        dskip = p["dskip"]
        if l == 0:
            dskip = dskip + send_partials([("w_out", 0), ("w_glu", 0)], "rs_start_mix0")
        du, dbbd, dccat, dd, da = _ssm_bwd(a["proj"], ucol, a["y"], dzd, dz2, a["xstart"], p["bbd"], p["ccat"],
                                           dskip, p["apow"], t_len, name=f"ssm_bwd{l}")
        grads["ssm_d"][l] = dd
        hs = BLOCK_STATES
        dbb_re = _from_blocks(dbbd[:, :, :hs], STATE, SSM_GROUP)
        dbb_im = _from_blocks(dbbd[:, :, hs:], STATE, SSM_GROUP)
        dccat_t = jnp.transpose(dccat, (0, 2, 1))
        grads["ssm_c_re"][l] = jnp.transpose(_from_blocks(dccat_t[:, :, :hs], STATE, SSM_GROUP), (0, 2, 1))
        grads["ssm_c_im"][l] = -jnp.transpose(_from_blocks(dccat_t[:, :, hs:], STATE, SSM_GROUP), (0, 2, 1))
        dab_re, dab_im = da[:, 0, :hs].reshape(1, gp), da[:, 0, hs:].reshape(1, gp)
        dlr, dli, dls, dbr, dbi = _ssm_params_bwd(p["lr"], p["li"], p["ls"], p["br"], p["bi"], dab_re, dab_im,
                                                  hgp(dbb_re), hgp(dbb_im), seg, f"ssm_params_bwd{l}")
        grads["lam_re"][l], grads["lam_im"][l], grads["log_step"][l] = dlr, dli, dls[0, :n_grp]
        grads["ssm_b_re"][l] = jnp.transpose(dbr.reshape(SSM_GROUP, n_grp, STATE), (1, 2, 0))
        grads["ssm_b_im"][l] = jnp.transpose(dbi.reshape(SSM_GROUP, n_grp, STATE), (1, 2, 0))
        dq, dk, dv, dsink = _attn_bwd(a["proj"], sinks_pad[l:l + 1], a["attn"], a["lse"], dattn,
                                      n_q=n_q, n_kv=n_kv, name=f"attn_bwd{l}")
        grads["attn_sinks"][l] = dsink[0, :n_q]
        dproj = jnp.concatenate([dq, dk, dv, du], axis=1).astype(BF16)
        dh1 = _matmul(dproj, w_in_full[l], tb=True, name=f"mm_in_dx{l}")
        big_blocks["w_in"][l] = _blocks_from_cols(_matmul(a["h1"], dproj, ta=True, name=f"mm_in_dw{l}"),
                                                  f"w_in_grad_layout{l}").reshape((N_DEV,) + w_in_dense)
        dxo, dg, dsc_m, dsh_m = _modnorm_bwd(dh1, a["xin"], vec(g_pre_mix[l]), sc_m, dx2, f"modnorm_mix_bwd{l}")
        grads["g_pre_mix"][l] = dg
        dada[l] = jnp.concatenate([dsh_m, dsc_m, dgt_m, dsh_f, dsc_f, dgt_f], axis=1)
        if l > 0:
            order = send_partials([(n, l) for n in reversed(BIG)], f"rs_start_layer{l}")
        else:
            send_partials([("w_in", 0)], "rs_start_in0")
        spack = _pack([dada[l]] + [grads[n][l] for n in small_order[1:]], F32, 1024)
        started, _ = _exchange_start([spack], [lax.empty((N_DEV,) + spack.shape, F32)], [whole], [at_block],
                                     f"small_start{l}")
        small_started[l] = started
    grad_x = dxo.reshape(x.shape)

    delta, new_m, new_v = {}, {}, {}
    stacked = {n: None for n in BIG}
    landed_layers = {n: 0 for n in BIG}
    after = [dxo]
    for items, started, name in rs_groups:
        mine, landed = _exchange_wait(started, after, [part_view(n)[0] for n, _ in items], [at_block] * len(items),
                                      name.replace("start", "wait"))
        for (n, l), part, slots in zip(items, mine, landed):
            stacked[n] = _sum_slots_own(me_arr, slots, part, part_view(n)[1], f"rs_sum_{n}{l}", layer=l,
                                        n_layers=depth, stacked=stacked[n])
            landed_layers[n] += 1
            if landed_layers[n] == depth:
                grads[n] = stacked[n].reshape(W[n].shape)
                delta[n], new_m[n], new_v[n] = _adamw_nd(W[n], grads[n], M[n], V[n], f"adamw_{n}")
                after.append(delta[n])

    n_cw = conv_w.shape[2]
    small_sums, dada_rows = [None] * depth, [None] * depth
    for l in reversed(range(depth)):
        mine, landed = _exchange_wait(small_started[l], after, [whole], [at_block], f"small_wait{l}")
        ssum = _sum_slots_own(me_arr, landed[0], mine[0], "self", f"sum_small{l}").reshape(-1)
        small_sums[l] = _unpack(ssum, [small_shapes[n][1:] for n in small_order])
        slot = lax.broadcasted_iota(jnp.int32, (N_DEV, 6 * d), 0)
        dada_rows[l] = jnp.where(slot == me, mine[0].reshape(-1)[:6 * d][None],
                                 landed[0].reshape(N_DEV, -1)[:, :6 * d])
    for i, n in enumerate(small_order):
        grads[n] = jnp.stack([small_sums[l][i] for l in range(depth)])
    grads["conv_w"] = lax.dynamic_slice_in_dim(grads["conv_w"], me * n_cw, n_cw, axis=2)
    dada_all = jnp.stack(dada_rows, axis=1)
    dada_shard = lax.dynamic_slice_in_dim(dada_all, me * n_ada, n_ada, axis=2)
    kp = LANES
    dada_pad = jnp.pad(jnp.transpose(dada_shard, (1, 0, 2)), ((0, 0), (0, kp - N_DEV), (0, 0)))
    act_t = jnp.pad(jnp.transpose(c_act[:N_DEV]), ((0, 0), (0, kp - N_DEV)))
    grads["w_ada"] = _ada_wgrad(act_t, dada_pad, "ada_wgrad")

    for n in ("w_ada", "conv_w"):
        delta[n], new_m[n], new_v[n] = _adamw_nd(W[n], grads[n], M[n], V[n], f"adamw_{n}")
    packs = [_pack([t[n] for n in SMALL], F32, 1024) for t in (W, grads, M, V)]
    outs = _adamw(*packs, "adamw_small")
    shapes = [W[n].shape for n in SMALL]
    for tgt_d, o in zip((delta, new_m, new_v), outs):
        for n, val in zip(SMALL, _unpack(o.reshape(-1), shapes)):
            tgt_d[n] = val

    return (loss, grad_x, *[grads[n] for n in ORDER], *[delta[n] for n in ORDER],
            *[new_m[n] for n in ORDER], *[new_v[n] for n in ORDER])
```

```python
import functools
import math

import jax
import jax.numpy as jnp
from jax import lax
from jax.experimental import pallas as pl
from jax.experimental.pallas import tpu as pltpu

F32 = jnp.float32
BF16 = jnp.bfloat16

N_DEV = 8
HEAD_DIM = 64
WINDOW = 128
SSM_GROUP = 16
STATE = 64
LANES = 128
GROUPS_PER_BLOCK = LANES // SSM_GROUP
BLOCK_STATES = GROUPS_PER_BLOCK * STATE
EPS = 1e-6
NEG = -1e30
ADAM_LR, ADAM_B1, ADAM_B2, ADAM_EPS, ADAM_WD, ADAM_STEP = 0.001, 0.9, 0.999, 1e-08, 0.01, 10
VMEM_BYTES_V7X = 64 * 1024 * 1024
GELU_C = math.sqrt(2.0 / math.pi)
MESH = pl.DeviceIdType.MESH
ANY = pl.BlockSpec(memory_space=pl.ANY)


def _pick(n, pref, align):
    t = (min(pref, n) // align) * align
    while t >= align:
        if n % t == 0:
            return t
        t -= align
    return n


def _params(vmem_bytes=None):
    if vmem_bytes is None:
        return pltpu.CompilerParams()
    return pltpu.CompilerParams(vmem_limit_bytes=int(min(vmem_bytes, VMEM_BYTES_V7X - (8 << 20))))


def _gelu_and_grad(x):
    x2 = x * x
    half_x = 0.5 * x
    th = jnp.tanh((GELU_C * x) * (1.0 + 0.044715 * x2))
    one_th = 1.0 + th
    grad = 0.5 * one_th + (half_x * (1.0 - th * th)) * (GELU_C + (3.0 * 0.044715 * GELU_C) * x2)
    return half_x * one_th, grad


def _gelu(x):
    return _gelu_and_grad(x)[0]


def _gelu_grad(x):
    return _gelu_and_grad(x)[1]


def _rstd(x):
    return lax.rsqrt(jnp.mean(x * x, axis=-1, keepdims=True) + EPS)


def _norm_bwd(dhat, xhat, r):
    return r * (dhat - xhat * jnp.mean(dhat * xhat, axis=-1, keepdims=True))


def _matmul(a, b, *, ta=False, tb=False, out_dtype=F32, name):
    (kdim, m) = a.shape if ta else a.shape[::-1]
    (n, k2) = b.shape if tb else b.shape[::-1]
    assert kdim == k2, (a.shape, b.shape, ta, tb)
    tm, tn, tk = _pick(m, 1024, LANES), _pick(n, 1024, LANES), _pick(kdim, 2048, LANES)
    nk = kdim // tk
    dn = (((0 if ta else 1,), (1 if tb else 0,)), ((), ()))

    def partial_product(a_ref, b_ref):
        return lax.dot_general(a_ref[...].astype(BF16), b_ref[...].astype(BF16), dn, preferred_element_type=F32)

    def body_one(a_ref, b_ref, o_ref):
        o_ref[...] = partial_product(a_ref, b_ref).astype(o_ref.dtype)

    def body_acc(a_ref, b_ref, o_ref, acc_ref):
        k = pl.program_id(2)

        @pl.when(k == 0)
        def _():
            acc_ref[...] = partial_product(a_ref, b_ref)

        @pl.when((k > 0) & (k < nk - 1))
        def _():
            acc_ref[...] += partial_product(a_ref, b_ref)

        @pl.when(k == nk - 1)
        def _():
            o_ref[...] = (acc_ref[...] + partial_product(a_ref, b_ref)).astype(o_ref.dtype)

    body = body_one if nk == 1 else body_acc
    a_spec = pl.BlockSpec((tk, tm), lambda i, j, k: (k, i)) if ta else pl.BlockSpec((tm, tk), lambda i, j, k: (i, k))
    b_spec = pl.BlockSpec((tn, tk), lambda i, j, k: (j, k)) if tb else pl.BlockSpec((tk, tn), lambda i, j, k: (k, j))
    vmem = (2 * (tm * tk * a.dtype.itemsize + tk * tn * b.dtype.itemsize) + tm * tn * 4
            + 2 * tm * tn * jnp.dtype(out_dtype).itemsize + 3 * tm * tn * 4 + (4 << 20))
    return pl.pallas_call(
        body, name=name, grid=(m // tm, n // tn, nk),
        in_specs=[a_spec, b_spec], out_specs=pl.BlockSpec((tm, tn), lambda i, j, k: (i, j)),
        out_shape=jax.ShapeDtypeStruct((m, n), out_dtype),
        scratch_shapes=[] if nk == 1 else [pltpu.VMEM((tm, tn), F32)],
        compiler_params=_params(vmem),
    )(a, b)


def _all_gather(x, name):
    def body(x_ref, out_ref, send_sems, recv_sems, local_sem):
        x_, y_, c_ = lax.axis_index("x"), lax.axis_index("y"), lax.axis_index("c")
        me, sibling = (x_, y_, c_), (x_, y_, 1 - c_)
        chips = [(1 - x_, y_), (x_, 1 - y_), (1 - x_, 1 - y_)]

        def slot(px, py, pc):
            return out_ref.at[4 * px + 2 * py + pc]

        def copy(k, block, to, src=None):
            return pltpu.make_async_remote_copy(
                src_ref=slot(*block) if src is None else src, dst_ref=slot(*block),
                send_sem=send_sems.at[k], recv_sem=recv_sems.at[k], device_id=to, device_id_type=MESH)

        mine = pltpu.make_async_copy(x_ref, slot(*me), local_sem)
        mine.start()
        first = [copy(0, me, sibling, src=x_ref)]
        first += [copy(1 + j, me, (*chip, c_), src=x_ref) for j, chip in enumerate(chips)]
        for cp in first:
            cp.start()
        passed = [copy(4 + j, (*chip, c_), sibling) for j, chip in enumerate(chips)]
        for j, chip in enumerate(chips):
            copy(1 + j, (*chip, c_), me).wait_recv()
            passed[j].start()
        copy(0, sibling, me).wait_recv()
        for j, chip in enumerate(chips):
            copy(4 + j, (*chip, 1 - c_), me).wait_recv()
        for cp in first + passed:
            cp.wait_send()
        mine.wait()

    return pl.pallas_call(
        body, name=name, out_shape=jax.ShapeDtypeStruct((N_DEV,) + x.shape, x.dtype),
        in_specs=[ANY], out_specs=ANY,
        scratch_shapes=[pltpu.SemaphoreType.DMA((7,)), pltpu.SemaphoreType.DMA((7,)), pltpu.SemaphoreType.DMA],
    )(x)


def _gather_multi(srcs, out_shapes, views, name):
    n = len(srcs)

    def body(*refs):
        src_refs, out_refs = refs[:n], refs[n:2 * n]
        send_sems, recv_sems, local_sems = refs[2 * n:]
        x_, y_, c_ = lax.axis_index("x"), lax.axis_index("y"), lax.axis_index("c")
        me, sibling = (x_, y_, c_), (x_, y_, 1 - c_)
        chips = [(1 - x_, y_), (x_, 1 - y_), (1 - x_, 1 - y_)]

        def slot(i, px, py, pc):
            return views[i](out_refs[i], 4 * px + 2 * py + pc)

        def copy(i, k, block, to, from_src=False):
            return pltpu.make_async_remote_copy(
                src_ref=src_refs[i] if from_src else slot(i, *block), dst_ref=slot(i, *block),
                send_sem=send_sems.at[7 * i + k], recv_sem=recv_sems.at[7 * i + k], device_id=to, device_id_type=MESH)

        mine = [pltpu.make_async_copy(src_refs[i], slot(i, *me), local_sems.at[i]) for i in range(n)]
        for cp in mine:
            cp.start()
        first = []
        for i in range(n):
            first.append(copy(i, 0, me, sibling, True))
            first += [copy(i, 1 + j, me, (*chip, c_), True) for j, chip in enumerate(chips)]
        for cp in first:
            cp.start()
        passed = []
        for j, chip in enumerate(chips):
            for i in range(n):
                copy(i, 1 + j, (*chip, c_), me).wait_recv()
                fwd = copy(i, 4 + j, (*chip, c_), sibling)
                fwd.start()
                passed.append(fwd)
        for i in range(n):
            copy(i, 0, sibling, me).wait_recv()
            for j, chip in enumerate(chips):
                copy(i, 4 + j, (*chip, 1 - c_), me).wait_recv()
        for cp in first + passed:
            cp.wait_send()
        for cp in mine:
            cp.wait()

    return pl.pallas_call(
        body, name=name, out_shape=[jax.ShapeDtypeStruct(s, a.dtype) for s, a in zip(out_shapes, srcs)],
        in_specs=[ANY] * n, out_specs=[ANY] * n,
        scratch_shapes=[pltpu.SemaphoreType.DMA((7 * n,)), pltpu.SemaphoreType.DMA((7 * n,)),
                        pltpu.SemaphoreType.DMA((n,))],
    )(*srcs)


HBM_SPEC = pl.BlockSpec(memory_space=pltpu.HBM)
SEM_SPEC = pl.BlockSpec(memory_space=pltpu.SEMAPHORE)
SIDE_EFFECT = pltpu.SideEffectType.DATAFLOW_SIDE_EFFECTING
N_PEERS = N_DEV - 1


def _peer(k, x_, y_, c_):
    px = 1 - x_ if (k >> 2) & 1 else x_
    py = 1 - y_ if (k >> 1) & 1 else y_
    pc = 1 - c_ if k & 1 else c_
    return (px, py, pc), 4 * px + 2 * py + pc


def _exchange_copies(src_refs, land_refs, send_sems, recv_sems, src_views, dst_views):
    x_, y_, c_ = lax.axis_index("x"), lax.axis_index("y"), lax.axis_index("c")
    me = 4 * x_ + 2 * y_ + c_
    out = []
    for i in range(len(src_refs)):
        for k in range(1, N_DEV):
            peer, idx = _peer(k, x_, y_, c_)

            def copy(dst_slot, i=i, k=k, peer=peer, idx=idx):
                return pltpu.make_async_remote_copy(
                    src_ref=src_views[i](src_refs[i], idx), dst_ref=dst_views[i](land_refs[i], dst_slot),
                    send_sem=send_sems[i].at[k - 1], recv_sem=recv_sems[i].at[k - 1], device_id=peer,
                    device_id_type=MESH)

            out.append((copy(me), copy(idx)))
    return out


def _exchange_start(srcs, lands, src_views, dst_views, name):
    n = len(srcs)

    def body(*refs):
        src_refs, land_refs = refs[:n], refs[n:2 * n]
        send_sems, recv_sems = refs[2 * n:3 * n], refs[3 * n:4 * n]
        token = refs[-1]
        for send, _ in _exchange_copies(src_refs, land_refs, send_sems, recv_sems, src_views, dst_views):
            send.start()
        token[...] = jnp.zeros_like(token)

    sems = [pltpu.SemaphoreType.DMA((N_PEERS,))] * n
    thru = [pltpu.HBM(a.shape, a.dtype) for a in list(srcs) + list(lands)]
    outs = pl.pallas_call(
        body, name=name, out_shape=sems + sems + thru + [jax.ShapeDtypeStruct((8, LANES), F32)],
        in_specs=[HBM_SPEC] * (2 * n),
        out_specs=[SEM_SPEC] * (2 * n) + [HBM_SPEC] * (2 * n) + [pl.BlockSpec(memory_space=pltpu.VMEM)],
        input_output_aliases={j: 2 * n + j for j in range(2 * n)},
        compiler_params=pltpu.CompilerParams(has_side_effects=SIDE_EFFECT),
    )(*[pltpu.with_memory_space_constraint(a, pltpu.HBM) for a in list(srcs) + list(lands)])
    per_array = [(outs[j], outs[n + j], outs[2 * n + j], outs[3 * n + j]) for j in range(n)]
    return per_array, outs[-1]


def _exchange_wait(started, after, src_views, dst_views, name):
    send_sems, recv_sems, srcs, lands = (list(t) for t in zip(*started))
    n = len(srcs)
    after = list(after)

    def body(*refs):
        src_refs, land_refs = refs[:n], refs[n:2 * n]
        send_refs, recv_refs = refs[2 * n:3 * n], refs[3 * n:4 * n]
        copies = _exchange_copies(src_refs, land_refs, send_refs, recv_refs, src_views, dst_views)
        for send, _ in copies:
            send.wait_send()
        for _, recv in copies:
            recv.wait_recv()

    thru = [pltpu.HBM(a.shape, a.dtype) for a in list(srcs) + list(lands)]
    outs = pl.pallas_call(
        body, name=name, out_shape=thru,
        in_specs=[HBM_SPEC] * (2 * n) + [SEM_SPEC] * (2 * n) + [ANY] * len(after),
        out_specs=[HBM_SPEC] * (2 * n),
        input_output_aliases={j: j for j in range(2 * n)},
        compiler_params=pltpu.CompilerParams(has_side_effects=SIDE_EFFECT),
    )(*srcs, *lands, *send_sems, *recv_sems, *after)
    return outs[:n], outs[n:]


def _place_own(me, src, land, kind, name):
    r, c = src.shape
    tr = _pick(r, 512, 16)
    nt = r // tr
    if kind == "rows":
        out_spec = pl.BlockSpec((tr, c), lambda i, mr: (mr[0] * nt + i, 0))
    elif kind == "cols":
        out_spec = pl.BlockSpec((tr, c), lambda i, mr: (i, mr[0]))
    else:
        out_spec = pl.BlockSpec((None, tr, c), lambda i, mr: (mr[0], i, 0))

    def body(me_ref, s_ref, land_ref, o_ref):
        o_ref[...] = s_ref[...]

    return pl.pallas_call(
        body, name=name,
        grid_spec=pltpu.PrefetchScalarGridSpec(
            num_scalar_prefetch=1, grid=(nt,),
            in_specs=[pl.BlockSpec((tr, c), lambda i, mr: (i, 0)), ANY], out_specs=out_spec),
        out_shape=jax.ShapeDtypeStruct(land.shape, land.dtype),
        input_output_aliases={2: 0},
    )(me, src, land)


def _sum_slots_own(me, landed, part, kind, name, *, layer=0, n_layers=1, stacked=None):
    _, r, c = landed.shape
    tr = _pick(r, 512, 16)
    nt = r // tr
    if kind == "rows":
        part_spec = pl.BlockSpec((tr, c), lambda i, mr: (mr[0] * nt + i, 0))
    elif kind == "cols":
        part_spec = pl.BlockSpec((tr, c), lambda i, mr: (i, mr[0]))
    elif kind == "blk":
        part_spec = pl.BlockSpec((None, tr, c), lambda i, mr: (mr[0], i, 0))
    else:
        part_spec = pl.BlockSpec((tr, c), lambda i, mr: (i, 0))

    def body(me_ref, x_ref, p_ref, *rest):
        o_ref = rest[-1]
        own = p_ref[...].astype(F32)
        acc = jnp.where(me_ref[0] == 0, own, x_ref[0].astype(F32))
        for i in range(1, N_DEV):
            acc = acc + jnp.where(me_ref[0] == i, own, x_ref[i].astype(F32))
        o_ref[...] = acc

    operands = [me, landed, part] + ([] if stacked is None else [stacked])
    return pl.pallas_call(
        body, name=name,
        grid_spec=pltpu.PrefetchScalarGridSpec(
            num_scalar_prefetch=1, grid=(nt,),
            in_specs=[pl.BlockSpec((N_DEV, tr, c), lambda i, mr: (0, i, 0)), part_spec]
            + ([] if stacked is None else [ANY]),
            out_specs=pl.BlockSpec((None, tr, c), lambda i, mr: (layer, i, 0))),
        out_shape=jax.ShapeDtypeStruct((n_layers, r, c), F32),
        input_output_aliases={} if stacked is None else {3: 0},
        compiler_params=_params(2 * N_DEV * tr * c * landed.dtype.itemsize + 8 * tr * c * 4 + (4 << 20)),
    )(*operands)


def _cols_from_blocks(blk, name):
    nd, nl, k, n = blk.shape
    tk = _pick(k, 256, 16)

    def body(b_ref, o_ref, wide_ref):
        for dev in range(nd):
            wide_ref[:, dev * n:(dev + 1) * n] = b_ref[dev].astype(F32)
        o_ref[...] = wide_ref[...].astype(o_ref.dtype)

    return pl.pallas_call(
        body, name=name, grid=(nl, k // tk),
        in_specs=[pl.BlockSpec((nd, None, tk, n), lambda l, i: (0, l, i, 0))],
        out_specs=pl.BlockSpec((None, tk, nd * n), lambda l, i: (l, i, 0)),
        out_shape=jax.ShapeDtypeStruct((nl, k, nd * n), BF16),
        scratch_shapes=[pltpu.VMEM((tk, nd * n), F32)],
    )(blk)


def _blocks_from_cols(full, name):
    k, n8 = full.shape
    n = n8 // N_DEV
    tk = _pick(k, 256, 16)

    def body(f_ref, o_ref):
        for dev in range(N_DEV):
            o_ref[dev] = f_ref[:, dev * n:(dev + 1) * n].astype(o_ref.dtype)

    return pl.pallas_call(
        body, name=name, grid=(k // tk,),
        in_specs=[pl.BlockSpec((tk, n8), lambda i: (i, 0))],
        out_specs=pl.BlockSpec((N_DEV, tk, n), lambda i: (0, i, 0)),
        out_shape=jax.ShapeDtypeStruct((N_DEV, k, n), BF16),
    )(full)


def _pack(arrs, dtype, cols):
    flat = jnp.concatenate([a.astype(dtype).reshape(-1) for a in arrs])
    unit = 16 * cols
    pad = (-flat.shape[0]) % unit
    flat = jnp.pad(flat, (0, pad))
    return flat.reshape(-1, cols)


def _unpack(flat, shapes):
    out, off = [], 0
    for s in shapes:
        n = math.prod(s)
        out.append(flat[off:off + n].reshape(s))
        off += n
    return out


def _ada_fwd(c_all, w_ada, b_shard, name):
    nl, d, n = w_ada.shape
    tn = _pick(n, 512, LANES)

    def body(c_ref, w_ref, b_ref, o_ref, act_ref):
        cv = c_ref[...]
        act = cv * jax.nn.sigmoid(cv)
        act_ref[...] = act
        o_ref[...] = jnp.dot(act.astype(BF16), w_ref[...].astype(BF16), preferred_element_type=F32) + b_ref[...]

    return pl.pallas_call(
        body, name=name, grid=(nl, n // tn),
        in_specs=[pl.BlockSpec(c_all.shape, lambda l, j: (0, 0)),
                  pl.BlockSpec((None, d, tn), lambda l, j: (l, 0, j)),
                  pl.BlockSpec((None, 1, tn), lambda l, j: (l, 0, j))],
        out_specs=[pl.BlockSpec((None, c_all.shape[0], tn), lambda l, j: (l, 0, j)),
                   pl.BlockSpec(c_all.shape, lambda l, j: (0, 0))],
        out_shape=[jax.ShapeDtypeStruct((nl, c_all.shape[0], n), F32), jax.ShapeDtypeStruct(c_all.shape, F32)],
        compiler_params=_params(2 * d * tn * 4 + d * tn * 2 + (8 << 20)),
    )(c_all, w_ada, b_shard)


def _ada_wgrad(act_t, dada, name):
    d, kp = act_t.shape
    nl, _, n = dada.shape
    tm = _pick(d, 512, 8)

    def body(a_ref, g_ref, o_ref):
        o_ref[...] = jnp.dot(a_ref[...].astype(BF16), g_ref[...].astype(BF16), preferred_element_type=F32)

    return pl.pallas_call(
        body, name=name, grid=(nl, d // tm),
        in_specs=[pl.BlockSpec((tm, kp), lambda l, i: (i, 0)), pl.BlockSpec((None, kp, n), lambda l, i: (l, 0, 0))],
        out_specs=pl.BlockSpec((None, tm, n), lambda l, i: (l, i, 0)),
        out_shape=jax.ShapeDtypeStruct((nl, d, n), F32),
        compiler_params=_params(4 * tm * n * 4 + 2 * kp * n * 4 + (8 << 20)),
    )(act_t, dada)


def _row_spec(tm, d):
    return pl.BlockSpec((tm, d), lambda i: (i, 0))


def _vec_spec(d):
    return pl.BlockSpec((1, d), lambda i: (0, 0))


def _modnorm_fwd(x, g, sc, sh, name):
    s, d = x.shape
    tm = _pick(s, 256, 16)

    def body(x_ref, g_ref, sc_ref, sh_ref, o_ref):
        xv = x_ref[...]
        o_ref[...] = ((xv * _rstd(xv)) * g_ref[...] * (1.0 + sc_ref[...]) + sh_ref[...]).astype(o_ref.dtype)

    return pl.pallas_call(
        body, name=name, grid=(s // tm,),
        in_specs=[_row_spec(tm, d), _vec_spec(d), _vec_spec(d), _vec_spec(d)], out_specs=_row_spec(tm, d),
        out_shape=jax.ShapeDtypeStruct((s, d), BF16),
    )(x, g, sc, sh)


def _modnorm_bwd(dh, x, g, sc, dres, name):
    s, d = x.shape
    tm = _pick(s, 256, 8)

    def body(dh_ref, x_ref, g_ref, sc_ref, dres_ref, dx_ref, dg_ref, dsc_ref, dsh_ref):
        @pl.when(pl.program_id(0) == 0)
        def _():
            dg_ref[...] = jnp.zeros_like(dg_ref)
            dsc_ref[...] = jnp.zeros_like(dsc_ref)
            dsh_ref[...] = jnp.zeros_like(dsh_ref)

        dh_, xv, gv = dh_ref[...], x_ref[...], g_ref[...]
        r = _rstd(xv)
        xhat = xv * r
        dn = dh_ * (1.0 + sc_ref[...])
        dsh_ref[...] += jnp.sum(dh_, axis=0, keepdims=True)
        dsc_ref[...] += jnp.sum(dh_ * (xhat * gv), axis=0, keepdims=True)
        dg_ref[...] += jnp.sum(dn * xhat, axis=0, keepdims=True)
        dx_ref[...] = _norm_bwd(dn * gv, xhat, r) + dres_ref[...]

    vec = jax.ShapeDtypeStruct((1, d), F32)
    return pl.pallas_call(
        body, name=name, grid=(s // tm,),
        in_specs=[_row_spec(tm, d), _row_spec(tm, d), _vec_spec(d), _vec_spec(d), _row_spec(tm, d)],
        out_specs=[_row_spec(tm, d), _vec_spec(d), _vec_spec(d), _vec_spec(d)],
        out_shape=[jax.ShapeDtypeStruct((s, d), F32), vec, vec, vec],
    )(dh, x, g, sc, dres)


def _resnorm_fwd(x, y, g, gt, name):
    s, d = x.shape
    tm = _pick(s, 256, 8)

    def body(x_ref, y_ref, g_ref, gt_ref, o_ref):
        yv = y_ref[...]
        o_ref[...] = x_ref[...] + (1.0 + gt_ref[...]) * ((yv * _rstd(yv)) * g_ref[...])

    return pl.pallas_call(
        body, name=name, grid=(s // tm,),
        in_specs=[_row_spec(tm, d), _row_spec(tm, d), _vec_spec(d), _vec_spec(d)], out_specs=_row_spec(tm, d),
        out_shape=jax.ShapeDtypeStruct((s, d), F32),
    )(x, y, g, gt)


def _resnorm_bwd(dxo, y, g, gt, name):
    s, d = y.shape
    tm = _pick(s, 256, 16)

    def body(dxo_ref, y_ref, g_ref, gt_ref, dy_ref, dg_ref, dgt_ref):
        @pl.when(pl.program_id(0) == 0)
        def _():
            dg_ref[...] = jnp.zeros_like(dg_ref)
            dgt_ref[...] = jnp.zeros_like(dgt_ref)

        dxo_, yv, gv = dxo_ref[...], y_ref[...], g_ref[...]
        r = _rstd(yv)
        yhat = yv * r
        dn = dxo_ * (1.0 + gt_ref[...])
        dgt_ref[...] += jnp.sum(dxo_ * (yhat * gv), axis=0, keepdims=True)
        dg_ref[...] += jnp.sum(dn * yhat, axis=0, keepdims=True)
        dy_ref[...] = _norm_bwd(dn * gv, yhat, r).astype(dy_ref.dtype)

    vec = jax.ShapeDtypeStruct((1, d), F32)
    return pl.pallas_call(
        body, name=name, grid=(s // tm,),
        in_specs=[_row_spec(tm, d), _row_spec(tm, d), _vec_spec(d), _vec_spec(d)],
        out_specs=[_row_spec(tm, d), _vec_spec(d), _vec_spec(d)],
        out_shape=[jax.ShapeDtypeStruct((s, d), BF16), vec, vec],
    )(dxo, y, g, gt)


def _loss_bwd(xf, tgt, name):
    s, d = xf.shape
    tm = _pick(s, 256, 8)

    def body(x_ref, t_ref, dy_ref, l_ref):
        @pl.when(pl.program_id(0) == 0)
        def _():
            l_ref[...] = jnp.zeros_like(l_ref)

        e = x_ref[...] - t_ref[...]
        dy_ref[...] = e * (1.0 / d)
        l_ref[...] += jnp.sum(e * e) * (0.5 / d)

    return pl.pallas_call(
        body, name=name, grid=(s // tm,),
        in_specs=[_row_spec(tm, d), _row_spec(tm, d)],
        out_specs=[_row_spec(tm, d), pl.BlockSpec((8, LANES), lambda i: (0, 0))],
        out_shape=[jax.ShapeDtypeStruct((s, d), F32), jax.ShapeDtypeStruct((8, LANES), F32)],
    )(xf, tgt)


def _attn_specs(n_q, n_kv):
    aw, kvd = n_q * HEAD_DIM, n_kv * HEAD_DIM
    assert aw % kvd == 0
    kcol = aw // kvd
    q = pl.BlockSpec((WINDOW, aw), lambda n: (n, 0))
    kc = pl.BlockSpec((WINDOW, kvd), lambda n: (n, kcol))
    kp = pl.BlockSpec((WINDOW, kvd), lambda n: (jnp.maximum(n - 1, 0), kcol))
    vc = pl.BlockSpec((WINDOW, kvd), lambda n: (n, kcol + 1))
    vp = pl.BlockSpec((WINDOW, kvd), lambda n: (jnp.maximum(n - 1, 0), kcol + 1))
    return [q, kc, kp, vc, vp]


def _band_mask(n, n_heads):
    qi = lax.broadcasted_iota(jnp.int32, (n_heads * WINDOW, 2 * WINDOW), 0) & (WINDOW - 1)
    kj = lax.broadcasted_iota(jnp.int32, (n_heads * WINDOW, 2 * WINDOW), 1)
    return (kj > qi) & (kj <= qi + WINDOW) & ((kj >= WINDOW) | (n > 0))


def _stack_heads(ref, heads):
    return jnp.concatenate([ref[:, h * HEAD_DIM:(h + 1) * HEAD_DIM] for h in heads], axis=0)


def _stack_sinks(ref, heads):
    return jnp.concatenate([jnp.broadcast_to(ref[:, h:h + 1], (WINDOW, 1)) for h in heads], axis=0)


_NT = (((1,), (1,)), ((), ()))
_TN = (((0,), (0,)), ((), ()))


def _attn_fwd(proj, sinks, *, n_q, n_kv, name):
    s = proj.shape[0]
    aw, grp = n_q * HEAD_DIM, n_q // n_kv

    def body(q_ref, kc_ref, kp_ref, vc_ref, vp_ref, sink_ref, o_ref, lse_ref):
        valid = _band_mask(pl.program_id(0), grp)
        kb = jnp.concatenate([kp_ref[...], kc_ref[...]], axis=0).astype(BF16)
        vb = jnp.concatenate([vp_ref[...], vc_ref[...]], axis=0).astype(BF16)
        lse_ref[...] = jnp.zeros_like(lse_ref)
        for g in range(n_kv):
            heads = range(g * grp, (g + 1) * grp)
            gs = slice(g * HEAD_DIM, (g + 1) * HEAD_DIM)
            qg = _stack_heads(q_ref, heads).astype(BF16)
            sink = _stack_sinks(sink_ref, heads)
            sc = lax.dot_general(qg, kb[:, gs], _NT, preferred_element_type=F32)
            sc = jnp.where(valid, sc * (HEAD_DIM ** -0.5), NEG)
            m = jnp.maximum(jnp.max(sc, axis=-1, keepdims=True), sink)
            e = jnp.exp(sc - m)
            den = jnp.sum(e, axis=-1, keepdims=True) + jnp.exp(sink - m)
            p = e * (1.0 / den)
            og = jnp.dot(p.astype(BF16), vb[:, gs], preferred_element_type=F32)
            lse = m + jnp.log(den)
            for i, h in enumerate(heads):
                rows = slice(i * WINDOW, (i + 1) * WINDOW)
                o_ref[:, h * HEAD_DIM:(h + 1) * HEAD_DIM] = og[rows]
                lse_ref[:, h:h + 1] = lse[rows]

    return pl.pallas_call(
        body, name=name, grid=(s // WINDOW,),
        in_specs=_attn_specs(n_q, n_kv) + [pl.BlockSpec((1, LANES), lambda n: (0, 0))],
        out_specs=[pl.BlockSpec((WINDOW, aw), lambda n: (n, 0)), pl.BlockSpec((WINDOW, LANES), lambda n: (n, 0))],
        out_shape=[jax.ShapeDtypeStruct((s, aw), F32), jax.ShapeDtypeStruct((s, LANES), F32)],
    )(proj, proj, proj, proj, proj, sinks)


def _attn_bwd(proj, sinks, out, lse, dout, *, n_q, n_kv, name):
    s = proj.shape[0]
    aw, kvd, grp = n_q * HEAD_DIM, n_kv * HEAD_DIM, n_q // n_kv
    scale = HEAD_DIM ** -0.5

    def body(q_ref, kc_ref, kp_ref, vc_ref, vp_ref, sink_ref, o_ref, lse_ref, do_ref,
             dq_ref, dk_ref, dv_ref, dsink_ref):
        n = pl.program_id(0)

        @pl.when(n == 0)
        def _():
            dk_ref[...] = jnp.zeros_like(dk_ref)
            dv_ref[...] = jnp.zeros_like(dv_ref)
            dsink_ref[...] = jnp.zeros_like(dsink_ref)

        valid = _band_mask(n, grp)
        kb = jnp.concatenate([kp_ref[...], kc_ref[...]], axis=0).astype(BF16)
        vb = jnp.concatenate([vp_ref[...], vc_ref[...]], axis=0).astype(BF16)
        lane = lax.broadcasted_iota(jnp.int32, (8, LANES), 1)
        dsink = jnp.zeros((8, LANES), F32)
        cur = pl.ds(pl.multiple_of(n * WINDOW, WINDOW), WINDOW)
        prev = pl.ds(pl.multiple_of(jnp.maximum(n - 1, 0) * WINDOW, WINDOW), WINDOW)
        for g in range(n_kv):
            heads = range(g * grp, (g + 1) * grp)
            gs = slice(g * HEAD_DIM, (g + 1) * HEAD_DIM)
            qg = _stack_heads(q_ref, heads).astype(BF16)
            do = _stack_heads(do_ref, heads)
            dob = do.astype(BF16)
            lse = jnp.concatenate([lse_ref[:, h:h + 1] for h in heads], axis=0)
            sc = lax.dot_general(qg, kb[:, gs], _NT, preferred_element_type=F32)
            sc = jnp.where(valid, sc * scale, NEG)
            p = jnp.exp(sc - lse)
            delta = jnp.sum(do * _stack_heads(o_ref, heads), axis=-1, keepdims=True)
            dp = lax.dot_general(dob, vb[:, gs], _NT, preferred_element_type=F32)
            ds = (p * (dp - delta) * scale).astype(BF16)
            dqg = jnp.dot(ds, kb[:, gs], preferred_element_type=F32)
            dkb = lax.dot_general(ds, qg, _TN, preferred_element_type=F32)
            dvb = lax.dot_general(p.astype(BF16), dob, _TN, preferred_element_type=F32)
            sink_term = jnp.exp(_stack_sinks(sink_ref, heads) - lse) * delta
            for i, h in enumerate(heads):
                rows = slice(i * WINDOW, (i + 1) * WINDOW)
                dq_ref[:, h * HEAD_DIM:(h + 1) * HEAD_DIM] = dqg[rows]
                dsink = dsink + jnp.where(lane == h, -jnp.sum(sink_term[rows]), 0.0)
            dk_ref[cur, gs] += dkb[WINDOW:]
            dv_ref[cur, gs] += dvb[WINDOW:]

            @pl.when(n > 0)
            def _():
                dk_ref[prev, gs] += dkb[:WINDOW]
                dv_ref[prev, gs] += dvb[:WINDOW]

        dsink_ref[...] += dsink

    blk = pl.BlockSpec((WINDOW, aw), lambda n: (n, 0))
    kv_full = pl.BlockSpec((s, kvd), lambda n: (0, 0))
    return pl.pallas_call(
        body, name=name, grid=(s // WINDOW,),
        in_specs=_attn_specs(n_q, n_kv) + [pl.BlockSpec((1, LANES), lambda n: (0, 0)), blk,
                                           pl.BlockSpec((WINDOW, LANES), lambda n: (n, 0)), blk],
        out_specs=[blk, kv_full, kv_full, pl.BlockSpec((8, LANES), lambda n: (0, 0))],
        out_shape=[jax.ShapeDtypeStruct((s, aw), F32), jax.ShapeDtypeStruct((s, kvd), F32),
                   jax.ShapeDtypeStruct((s, kvd), F32), jax.ShapeDtypeStruct((8, LANES), F32)],
    )(proj, proj, proj, proj, proj, sinks, out, lse, dout)


def _disc(lr, li, ls):
    dt = jnp.exp(ls)
    mag = jnp.exp(lr * dt)
    ang = li * dt
    ab_re, ab_im = mag * jnp.cos(ang), mag * jnp.sin(ang)
    den = lr * lr + li * li
    f_re = ((ab_re - 1.0) * lr + ab_im * li) / den
    f_im = (ab_im * lr - (ab_re - 1.0) * li) / den
    return ab_re, ab_im, f_re, f_im


POW_ROWS = 8
SUB = 8
TAB_ROWS = POW_ROWS + 2 * SUB


def _ssm_params_fwd(lr, li, ls, b_re, b_im, name):
    gp = lr.shape[1]
    h = b_re.shape[0]

    def body(lr_ref, li_ref, ls_ref, br_ref, bi_ref, bbr_ref, bbi_ref, tr_ref, ti_ref):
        ab_re, ab_im, f_re, f_im = _disc(lr_ref[...], li_ref[...], ls_ref[...])
        br, bi = br_ref[...], bi_ref[...]
        bbr_ref[...] = f_re * br - f_im * bi
        bbi_ref[...] = f_re * bi + f_im * br
        pr, pi = ab_re, ab_im
        for i in range(POW_ROWS):
            tr_ref[i:i + 1, :] = pr
            ti_ref[i:i + 1, :] = pi
            pr, pi = pr * pr - pi * pi, 2.0 * pr * pi
        pr, pi = ab_re, ab_im
        for r in range(SUB):
            for row in (POW_ROWS + r, POW_ROWS + 2 * SUB - 1 - r):
                tr_ref[row:row + 1, :] = pr
                ti_ref[row:row + 1, :] = pi
            pr, pi = pr * ab_re - pi * ab_im, pr * ab_im + pi * ab_re

    mat, tab = jax.ShapeDtypeStruct((h, gp), F32), jax.ShapeDtypeStruct((TAB_ROWS, gp), F32)
    return pl.pallas_call(body, name=name, out_shape=[mat, mat, tab, tab])(lr, li, ls, b_re, b_im)


def _ssm_params_bwd(lr, li, ls, b_re, b_im, dab_re, dab_im, dbb_re, dbb_im, seg, name):
    gp = lr.shape[1]
    h = b_re.shape[0]

    def body(lr_ref, li_ref, ls_ref, br_ref, bi_ref, dar_ref, dai_ref, dbbr_ref, dbbi_ref, seg_ref,
             dlr_ref, dli_ref, dls_ref, dbr_ref, dbi_ref):
        lr_, li_, ls_ = lr_ref[...], li_ref[...], ls_ref[...]
        (ab_re, ab_im, f_re, f_im), vjp = jax.vjp(_disc, lr_, li_, ls_)
        br, bi, dbbr, dbbi = br_ref[...], bi_ref[...], dbbr_ref[...], dbbi_ref[...]
        dbr_ref[...] = dbbr * f_re + dbbi * f_im
        dbi_ref[...] = dbbi * f_re - dbbr * f_im
        df_re = jnp.sum(dbbr * br + dbbi * bi, axis=0, keepdims=True)
        df_im = jnp.sum(dbbi * br - dbbr * bi, axis=0, keepdims=True)
        dlr, dli, dls = vjp((dar_ref[...], dai_ref[...], df_re, df_im))
        dlr_ref[...] = dlr
        dli_ref[...] = dli
        dls8 = jnp.broadcast_to(dls, (8, gp))
        dls_ref[...] = jnp.dot(dls8, seg_ref[...], preferred_element_type=F32, precision=lax.Precision.HIGHEST)

    vec, mat = jax.ShapeDtypeStruct((1, gp), F32), jax.ShapeDtypeStruct((h, gp), F32)
    return pl.pallas_call(body, name=name,
                          out_shape=[vec, vec, jax.ShapeDtypeStruct((8, seg.shape[1]), F32), mat, mat],
                          compiler_params=_params(24 << 20))(
        lr, li, ls, b_re, b_im, dab_re, dab_im, dbb_re, dbb_im, seg)


def _scan_bufs(t_len):
    hs = BLOCK_STATES
    return [pltpu.VMEM((hs // LANES, t_len, LANES), F32), pltpu.VMEM((hs // LANES, t_len, LANES), F32),
            pltpu.VMEM((t_len // SUB, hs), F32), pltpu.VMEM((t_len // SUB, hs), F32)]


def _scan(xr, xi, apow_ref, bufs, t_len, reverse):
    hs = BLOCK_STATES
    n_tiles = t_len // SUB
    sr_ref, si_ref, er_ref, ei_ref = bufs

    def doubling(xr, xi, n_rows, first_pow, within):
        row = lax.broadcasted_iota(jnp.int32, xr.shape, 0) & (within - 1)
        d = 1
        while d < within:
            i = first_pow + d.bit_length() - 1
            pr, pi = apow_ref[i:i + 1, :hs], apow_ref[i:i + 1, hs:]
            if reverse:
                pi, shift, keep = -pi, n_rows - d, row < within - d
            else:
                shift, keep = d, row >= d
            sr = jnp.where(keep, pltpu.roll(xr, shift, 0), 0.0)
            si = jnp.where(keep, pltpu.roll(xi, shift, 0), 0.0)
            xr, xi = xr + pr * sr - pi * si, xi + pr * si + pi * sr
            d *= 2
        return xr, xi

    shape3 = (n_tiles, SUB, hs)
    row = lax.broadcasted_iota(jnp.int32, shape3, 1)
    xr, xi = xr.reshape(shape3), xi.reshape(shape3)
    for i, d in enumerate((1, 2, 4)):
        pr, pi = apow_ref[i:i + 1, :hs], apow_ref[i:i + 1, hs:]
        if reverse:
            pi, shift, keep = -pi, SUB - d, row < SUB - d
        else:
            shift, keep = d, row >= d
        sr = jnp.where(keep, pltpu.roll(xr, shift, 1), 0.0)
        si = jnp.where(keep, pltpu.roll(xi, shift, 1), 0.0)
        xr, xi = xr + pr * sr - pi * si, xi + pr * si + pi * sr
    xr, xi = xr.reshape(t_len, hs), xi.reshape(t_len, hs)
    chunks = [slice(c * LANES, (c + 1) * LANES) for c in range(hs // LANES)]
    for c, lanes in enumerate(chunks):
        sr_ref[c] = xr[:, lanes]
        si_ref[c] = xi[:, lanes]
    edge = pl.ds(0 if reverse else SUB - 1, n_tiles, stride=SUB)
    tr, ti = doubling(jnp.concatenate([sr_ref[c, edge, :] for c in range(len(chunks))], axis=1),
                      jnp.concatenate([si_ref[c, edge, :] for c in range(len(chunks))], axis=1), n_tiles, 3, n_tiles)
    trow = lax.broadcasted_iota(jnp.int32, tr.shape, 0)
    if reverse:
        shift, keep = n_tiles - 1, trow < n_tiles - 1
    else:
        shift, keep = 1, trow >= 1
    er_ref[...] = jnp.where(keep, pltpu.roll(tr, shift, 0), 0.0)
    ei_ref[...] = jnp.where(keep, pltpu.roll(ti, shift, 0), 0.0)
    lin = POW_ROWS + SUB if reverse else POW_ROWS
    mr, mi = apow_ref[lin:lin + SUB, :hs], apow_ref[lin:lin + SUB, hs:]
    if reverse:
        mi = -mi
    for t in range(n_tiles):
        rows = slice(t * SUB, (t + 1) * SUB)
        er, ei = er_ref[t:t + 1, :], ei_ref[t:t + 1, :]
        add_r, add_i = mr * er - mi * ei, mr * ei + mi * er
        for c, lanes in enumerate(chunks):
            sr_ref[c, rows, :] += add_r[:, lanes]
            si_ref[c, rows, :] += add_i[:, lanes]
    return (jnp.concatenate([sr_ref[c] for c in range(len(chunks))], axis=1),
            jnp.concatenate([si_ref[c] for c in range(len(chunks))], axis=1))


def _ssm_chunk(s):
    t_len = _pick(s, 256, 8)
    assert t_len & (t_len - 1) == 0 and t_len <= 1 << POW_ROWS, t_len
    return t_len


def _fold_carry(br, bi, carry_ref, apow_ref, at_row, conj):
    hs = BLOCK_STATES
    cr, ci = carry_ref[0:1, :hs], carry_ref[0:1, hs:]
    ar, ai = apow_ref[0:1, :hs], apow_ref[0:1, hs:]
    if conj:
        ai = -ai
    here = lax.broadcasted_iota(jnp.int32, br.shape, 0) == at_row
    return jnp.where(here, br + (ar * cr - ai * ci), br), jnp.where(here, bi + (ar * ci + ai * cr), bi)


def _ssm_fwd(proj, ucol, bbd, ccat, dskip, apow, t_len, *, name):
    s = proj.shape[0]
    nb = bbd.shape[0]
    nc = s // t_len
    hs = BLOCK_STATES

    def body(u_ref, bbd_ref, ccat_ref, d_ref, apow_ref, y_ref, z_ref, xs_ref, carry_ref, *bufs):
        @pl.when(pl.program_id(1) == 0)
        def _():
            carry_ref[...] = jnp.zeros_like(carry_ref)

        xs_ref[...] = carry_ref[...]
        u = u_ref[...]
        bu = jnp.dot(u.astype(BF16), bbd_ref[...], preferred_element_type=F32)
        br, bi = _fold_carry(bu[:, :hs], bu[:, hs:], carry_ref, apow_ref, 0, False)
        xr, xi = _scan(br, bi, apow_ref, bufs, t_len, False)
        xcat = jnp.concatenate([xr, xi], axis=1)
        carry_ref[...] = jnp.broadcast_to(xcat[t_len - 1:t_len, :], carry_ref.shape)
        y = jnp.dot(xcat.astype(BF16), ccat_ref[...], preferred_element_type=F32) + d_ref[...] * u
        y_ref[...] = y
        z_ref[...] = _gelu(y).astype(z_ref.dtype)

    return pl.pallas_call(
        body, name=name, grid=(nb, nc),
        in_specs=[pl.BlockSpec((t_len, LANES), lambda j, n: (n, ucol + j)),
                  pl.BlockSpec((None, LANES, 2 * hs), lambda j, n: (j, 0, 0)),
                  pl.BlockSpec((None, 2 * hs, LANES), lambda j, n: (j, 0, 0)),
                  pl.BlockSpec((1, LANES), lambda j, n: (0, j)),
                  pl.BlockSpec((None, TAB_ROWS, 2 * hs), lambda j, n: (j, 0, 0))],
        out_specs=[pl.BlockSpec((t_len, LANES), lambda j, n: (n, j)),
                   pl.BlockSpec((t_len, LANES), lambda j, n: (n, j)),
                   pl.BlockSpec((None, None, 8, 2 * hs), lambda j, n: (j, n, 0, 0))],
        out_shape=[jax.ShapeDtypeStruct((s, nb * LANES), F32), jax.ShapeDtypeStruct((s, nb * LANES), BF16),
                   jax.ShapeDtypeStruct((nb, nc, 8, 2 * hs), F32)],
        scratch_shapes=[pltpu.VMEM((8, 2 * hs), F32)] + _scan_bufs(t_len),
        compiler_params=_params(40 << 20),
    )(proj, bbd, ccat, dskip, apow)


def _ssm_bwd(proj, ucol, y, dzd, dz2, xs, bbd, ccat, dskip, apow, t_len, *, name):
    s = proj.shape[0]
    nb = bbd.shape[0]
    nc = s // t_len
    hs = BLOCK_STATES

    def body(u_ref, y_ref, dzd_ref, dz2_ref, xs_ref, bbd_ref, ccat_ref, d_ref, apow_ref,
             du_ref, dbbd_ref, dccat_ref, dd_ref, da_ref, gcarry_ref, *bufs):
        @pl.when(pl.program_id(1) == 0)
        def _():
            gcarry_ref[...] = jnp.zeros_like(gcarry_ref)
            dbbd_ref[...] = jnp.zeros_like(dbbd_ref)
            dccat_ref[...] = jnp.zeros_like(dccat_ref)
            dd_ref[...] = jnp.zeros_like(dd_ref)
            da_ref[...] = jnp.zeros_like(da_ref)

        u = u_ref[...]
        ub = u.astype(BF16)
        dy = (dzd_ref[...] + dz2_ref[...]) * _gelu_grad(y_ref[...])
        dyb = dy.astype(BF16)
        bu = jnp.dot(ub, bbd_ref[...], preferred_element_type=F32)
        br, bi = _fold_carry(bu[:, :hs], bu[:, hs:], xs_ref, apow_ref, 0, False)
        xr, xi = _scan(br, bi, apow_ref, bufs[:4], t_len, False)
        sr, si = xs_ref[0:1, :hs], xs_ref[0:1, hs:]
        dxd = lax.dot_general(dyb, ccat_ref[...], _NT, preferred_element_type=F32)
        dr, di = _fold_carry(dxd[:, :hs], dxd[:, hs:], gcarry_ref, apow_ref, t_len - 1, True)
        gr, gi = _scan(dr, di, apow_ref, bufs[4:], t_len, True)
        gcat = jnp.concatenate([gr, gi], axis=1)
        gcarry_ref[...] = jnp.broadcast_to(gcat[0:1, :], gcarry_ref.shape)
        gb = gcat.astype(BF16)
        du_ref[...] = lax.dot_general(gb, bbd_ref[...], _NT, preferred_element_type=F32) + d_ref[...] * dy
        dbbd_ref[...] += lax.dot_general(ub, gb, _TN, preferred_element_type=F32)
        xb = jnp.concatenate([xr, xi], axis=1).astype(BF16)
        dccat_ref[...] += lax.dot_general(xb, dyb, _TN, preferred_element_type=F32)
        dd_ref[...] += jnp.sum(dy * u, axis=0, keepdims=True)
        first = lax.broadcasted_iota(jnp.int32, xr.shape, 0) == 0
        xpr = jnp.where(first, sr, pltpu.roll(xr, 1, 0))
        xpi = jnp.where(first, si, pltpu.roll(xi, 1, 0))
        dar = jnp.sum(gr * xpr + gi * xpi, axis=0, keepdims=True)
        dai = jnp.sum(gi * xpr - gr * xpi, axis=0, keepdims=True)
        da_ref[...] += jnp.concatenate([dar, dai], axis=1)

    def rows(j, n):
        return nc - 1 - n

    chunk = pl.BlockSpec((t_len, LANES), lambda j, n: (rows(j, n), j))
    return pl.pallas_call(
        body, name=name, grid=(nb, nc),
        in_specs=[pl.BlockSpec((t_len, LANES), lambda j, n: (rows(j, n), ucol + j)), chunk, chunk, chunk,
                  pl.BlockSpec((None, None, 8, 2 * hs), lambda j, n: (j, rows(j, n), 0, 0)),
                  pl.BlockSpec((None, LANES, 2 * hs), lambda j, n: (j, 0, 0)),
                  pl.BlockSpec((None, 2 * hs, LANES), lambda j, n: (j, 0, 0)),
                  pl.BlockSpec((1, LANES), lambda j, n: (0, j)),
                  pl.BlockSpec((None, TAB_ROWS, 2 * hs), lambda j, n: (j, 0, 0))],
        out_specs=[chunk,
                   pl.BlockSpec((None, LANES, 2 * hs), lambda j, n: (j, 0, 0)),
                   pl.BlockSpec((None, 2 * hs, LANES), lambda j, n: (j, 0, 0)),
                   pl.BlockSpec((1, LANES), lambda j, n: (0, j)),
                   pl.BlockSpec((None, 1, 2 * hs), lambda j, n: (j, 0, 0))],
        out_shape=[jax.ShapeDtypeStruct((s, nb * LANES), F32),
                   jax.ShapeDtypeStruct((nb, LANES, 2 * hs), F32),
                   jax.ShapeDtypeStruct((nb, 2 * hs, LANES), F32),
                   jax.ShapeDtypeStruct((1, nb * LANES), F32),
                   jax.ShapeDtypeStruct((nb, 1, 2 * hs), F32)],
        scratch_shapes=[pltpu.VMEM((8, 2 * hs), F32)] + _scan_bufs(t_len) + _scan_bufs(t_len),
        compiler_params=_params(48 << 20),
    )(proj, y, dzd, dz2, xs, bbd, ccat, dskip, apow)


def _to_blocks(a):
    g, p, k = a.shape
    nb = g // GROUPS_PER_BLOCK
    eye = jnp.eye(GROUPS_PER_BLOCK, dtype=a.dtype)
    a4 = a.reshape(nb, GROUPS_PER_BLOCK, p, k)
    out = jnp.einsum("ab,jbpk->jakbp", eye, a4)
    return out.reshape(nb, GROUPS_PER_BLOCK * k, GROUPS_PER_BLOCK * p)


def _from_blocks(d, p, k):
    nb = d.shape[0]
    d5 = d.reshape(nb, GROUPS_PER_BLOCK, k, GROUPS_PER_BLOCK, p)
    eye = jnp.eye(GROUPS_PER_BLOCK, dtype=bool)[None, :, None, :, None]
    diag = jnp.sum(jnp.where(eye, d5, 0.0), axis=1)
    return jnp.transpose(diag, (0, 2, 3, 1)).reshape(nb * GROUPS_PER_BLOCK, p, k)


def _merge_fwd(attn, y, gl, g_a, g_s, name):
    s, wa = attn.shape
    ws = y.shape[1]
    tm = _pick(s, 256, 16)

    def body(a_ref, y_ref, gl_ref, ga_ref, gs_ref, o_ref):
        av = a_ref[...]
        o_ref[:, :wa] = ((av * _rstd(av)) * ga_ref[...]).astype(o_ref.dtype)
        sv = _gelu(y_ref[...]) * jax.nn.sigmoid(gl_ref[...])
        o_ref[:, wa:] = ((sv * _rstd(sv)) * gs_ref[...]).astype(o_ref.dtype)

    return pl.pallas_call(
        body, name=name, grid=(s // tm,),
        in_specs=[_row_spec(tm, wa), _row_spec(tm, ws), _row_spec(tm, ws), _vec_spec(wa), _vec_spec(ws)],
        out_specs=_row_spec(tm, wa + ws), out_shape=jax.ShapeDtypeStruct((s, wa + ws), BF16),
    )(attn, y, gl, g_a, g_s)


def _merge_bwd(dmerged, attn, y, gl, g_a, g_s, name):
    s, wa = attn.shape
    ws = y.shape[1]
    tm = _pick(s, 256, 16)

    def body(dm_ref, a_ref, y_ref, gl_ref, ga_ref, gs_ref, da_ref, dgl_ref, dzd_ref, dga_ref, dgs_ref):
        @pl.when(pl.program_id(0) == 0)
        def _():
            dga_ref[...] = jnp.zeros_like(dga_ref)
            dgs_ref[...] = jnp.zeros_like(dgs_ref)

        dan, dsn = dm_ref[:, :wa], dm_ref[:, wa:]
        av = a_ref[...]
        ra = _rstd(av)
        ahat = av * ra
        dga_ref[...] += jnp.sum(dan * ahat, axis=0, keepdims=True)
        da_ref[...] = _norm_bwd(dan * ga_ref[...], ahat, ra)
        z = _gelu(y_ref[...])
        sig = jax.nn.sigmoid(gl_ref[...])
        sv = z * sig
        rs = _rstd(sv)
        shat = sv * rs
        dgs_ref[...] += jnp.sum(dsn * shat, axis=0, keepdims=True)
        dssm = _norm_bwd(dsn * gs_ref[...], shat, rs)
        dzd_ref[...] = dssm * sig
        dgl_ref[...] = (dssm * z * sig * (1.0 - sig)).astype(dgl_ref.dtype)

    return pl.pallas_call(
        body, name=name, grid=(s // tm,),
        in_specs=[_row_spec(tm, wa + ws), _row_spec(tm, wa), _row_spec(tm, ws), _row_spec(tm, ws),
                  _vec_spec(wa), _vec_spec(ws)],
        out_specs=[_row_spec(tm, wa), _row_spec(tm, ws), _row_spec(tm, ws), _vec_spec(wa), _vec_spec(ws)],
        out_shape=[jax.ShapeDtypeStruct((s, wa), F32), jax.ShapeDtypeStruct((s, ws), BF16),
                   jax.ShapeDtypeStruct((s, ws), F32), jax.ShapeDtypeStruct((1, wa), F32),
                   jax.ShapeDtypeStruct((1, ws), F32)],
    )(dmerged, attn, y, gl, g_a, g_s)


def _shift_down(main, halo, k):
    rolled = pltpu.roll(main, k, 0)
    row = lax.broadcasted_iota(jnp.int32, main.shape, 0)
    for r in range(k):
        rolled = jnp.where(row == r, halo[8 - k + r:8 - k + r + 1, :], rolled)
    return rolled


def _shift_up(main, halo, k):
    tm = main.shape[0]
    rolled = pltpu.roll(main, tm - k, 0)
    row = lax.broadcasted_iota(jnp.int32, main.shape, 0)
    for r in range(k):
        rolled = jnp.where(row == tm - k + r, halo[r:r + 1, :], rolled)
    return rolled


def _conv(main, halo, w_ref, b_ref):
    return (b_ref[...] + w_ref[0:1, :] * _shift_down(main, halo, 2) + w_ref[1:2, :] * _shift_down(main, halo, 1)
            + w_ref[2:3, :] * main)


def _gate_tiles(s, f):
    return _pick(s, 512, 16), _pick(f, 512, LANES)


def _gate_in_specs(tm, tn, nfb, order):
    hb = tm // 8
    ij = (lambda a, b: (b, a)) if order == "ji" else (lambda a, b: (a, b))

    def main(off):
        return pl.BlockSpec((tm, tn), lambda a, b: (ij(a, b)[0], ij(a, b)[1] + off))

    def halo(off):
        return pl.BlockSpec((8, tn), lambda a, b: (jnp.maximum(ij(a, b)[0] * hb - 1, 0), ij(a, b)[1] + off))

    def vec(rows, off):
        return pl.BlockSpec((rows, tn), lambda a, b: (0, ij(a, b)[1] + off))

    return [main(0), main(nfb), halo(0), halo(nfb), vec(3, 0), vec(3, nfb), vec(1, 0), vec(1, nfb)]


def _gate_fwd(up0, conv_w, conv_b, name):
    s, f2 = up0.shape
    f = f2 // 2
    tm, tn = _gate_tiles(s, f)
    nfb = f // tn

    def body(v_ref, g_ref, vh_ref, gh_ref, wv_ref, wg_ref, bv_ref, bg_ref, o_ref):
        top = pl.program_id(0) == 0
        vh = jnp.where(top, 0.0, vh_ref[...])
        gh = jnp.where(top, 0.0, gh_ref[...])
        val = _conv(v_ref[...], vh, wv_ref, bv_ref)
        gate = _conv(g_ref[...], gh, wg_ref, bg_ref)
        o_ref[...] = (_gelu(gate) * val).astype(o_ref.dtype)

    return pl.pallas_call(
        body, name=name, grid=(s // tm, nfb),
        in_specs=_gate_in_specs(tm, tn, nfb, "ij"), out_specs=pl.BlockSpec((tm, tn), lambda i, j: (i, j)),
        out_shape=jax.ShapeDtypeStruct((s, f), BF16),
        compiler_params=_params(24 * tm * tn * 4 + (4 << 20)),
    )(up0, up0, up0, up0, conv_w, conv_w, conv_b, conv_b)


def _gate_bwd(up0, conv_w, conv_b, da, name):
    s, f2 = up0.shape
    f = f2 // 2
    tm, tn = _gate_tiles(s, f)
    nfb = f // tn

    def body(v_ref, g_ref, vh_ref, gh_ref, wv_ref, wg_ref, bv_ref, bg_ref, da_ref, dup_ref, dcb_ref, dcw_ref):
        top = pl.program_id(1) == 0

        @pl.when(top)
        def _():
            dcb_ref[...] = jnp.zeros_like(dcb_ref)
            dcw_ref[...] = jnp.zeros_like(dcw_ref)

        halos = (jnp.where(top, 0.0, vh_ref[...]), jnp.where(top, 0.0, gh_ref[...]))
        mains = (v_ref[...], g_ref[...])
        val = _conv(mains[0], halos[0], wv_ref, bv_ref)
        gate = _conv(mains[1], halos[1], wg_ref, bg_ref)
        dav = da_ref[...]
        act, act_grad = _gelu_and_grad(gate)
        dups = (dav * act, (dav * val) * act_grad)
        for half in range(2):
            dup = dups[half]
            dup_ref[half] = dup
            dcb_ref[half] += jnp.sum(dup, axis=0, keepdims=True)
            dcw_ref[half, 0:1, :] += jnp.sum(dup * _shift_down(mains[half], halos[half], 2), axis=0, keepdims=True)
            dcw_ref[half, 1:2, :] += jnp.sum(dup * _shift_down(mains[half], halos[half], 1), axis=0, keepdims=True)
            dcw_ref[half, 2:3, :] += jnp.sum(dup * mains[half], axis=0, keepdims=True)

    return pl.pallas_call(
        body, name=name, grid=(nfb, s // tm),
        in_specs=_gate_in_specs(tm, tn, nfb, "ji") + [pl.BlockSpec((tm, tn), lambda j, i: (i, j))],
        out_specs=[pl.BlockSpec((2, tm, tn), lambda j, i: (0, i, j)),
                   pl.BlockSpec((2, 1, tn), lambda j, i: (0, 0, j)),
                   pl.BlockSpec((2, 3, tn), lambda j, i: (0, 0, j))],
        out_shape=[jax.ShapeDtypeStruct((2, s, f), F32), jax.ShapeDtypeStruct((2, 1, f), F32),
                   jax.ShapeDtypeStruct((2, 3, f), F32)],
        compiler_params=_params(40 * tm * tn * 4 + (4 << 20)),
    )(up0, up0, up0, up0, conv_w, conv_w, conv_b, conv_b, da)


def _conv_bwd(dup, conv_w, name):
    _, s, f = dup.shape
    tm, tn = _pick(s, 512, 16), _pick(f, 1536, LANES)
    nfb, ni, hb = f // tn, s // tm, tm // 8

    def body(d_ref, dh_ref, w_ref, o_ref):
        main = d_ref[...]
        halo = jnp.where(pl.program_id(1) == ni - 1, 0.0, dh_ref[...])
        o_ref[...] = (w_ref[2:3, :] * main + w_ref[1:2, :] * _shift_up(main, halo, 1)
                      + w_ref[0:1, :] * _shift_up(main, halo, 2)).astype(o_ref.dtype)

    return pl.pallas_call(
        body, name=name, grid=(2, ni, nfb),
        in_specs=[pl.BlockSpec((None, tm, tn), lambda h, i, j: (h, i, j)),
                  pl.BlockSpec((None, 8, tn), lambda h, i, j: (h, jnp.minimum((i + 1) * hb, s // 8 - 1), j)),
                  pl.BlockSpec((3, tn), lambda h, i, j: (0, h * nfb + j))],
        out_specs=pl.BlockSpec((tm, tn), lambda h, i, j: (i, h * nfb + j)),
        out_shape=jax.ShapeDtypeStruct((s, 2 * f), BF16),
        compiler_params=_params(12 * tm * tn * 4 + (4 << 20)),
    )(dup, dup, conv_w)


def _adamw(w, g, m, v, name):
    r, c = w.shape
    tr = _pick(r, max(8, (1 << 19) // max(c, 1) // 8 * 8), 8)
    c1, c2 = 1.0 / (1.0 - ADAM_B1 ** ADAM_STEP), 1.0 / (1.0 - ADAM_B2 ** ADAM_STEP)

    def body(w_ref, g_ref, m_ref, v_ref, d_ref, nm_ref, nv_ref):
        gv = g_ref[...]
        nm = ADAM_B1 * m_ref[...] + (1.0 - ADAM_B1) * gv
        nv = ADAM_B2 * v_ref[...] + (1.0 - ADAM_B2) * (gv * gv)
        nm_ref[...] = nm
        nv_ref[...] = nv
        d_ref[...] = -ADAM_LR * ((nm * c1) / (jnp.sqrt(nv * c2) + ADAM_EPS) + ADAM_WD * w_ref[...])

    spec = pl.BlockSpec((tr, c), lambda i: (i, 0))
    out = jax.ShapeDtypeStruct((r, c), F32)
    return pl.pallas_call(body, name=name, grid=(r // tr,), in_specs=[spec] * 4, out_specs=[spec] * 3,
                          out_shape=[out] * 3, compiler_params=_params(14 * tr * c * 4 + (4 << 20)))(w, g, m, v)


def _adamw_many(ws, gs, ms, vs, name):
    n = len(ws)
    c1, c2 = 1.0 / (1.0 - ADAM_B1 ** ADAM_STEP), 1.0 / (1.0 - ADAM_B2 ** ADAM_STEP)

    def body(*refs):
        w_refs, g_refs, m_refs, v_refs = (refs[i * n:(i + 1) * n] for i in range(4))
        d_refs, nm_refs, nv_refs = (refs[(4 + i) * n:(5 + i) * n] for i in range(3))
        for i in range(n):
            gv = g_refs[i][...]
            nm = ADAM_B1 * m_refs[i][...] + (1.0 - ADAM_B1) * gv
            nv = ADAM_B2 * v_refs[i][...] + (1.0 - ADAM_B2) * (gv * gv)
            nm_refs[i][...] = nm
            nv_refs[i][...] = nv
            d_refs[i][...] = -ADAM_LR * ((nm * c1) / (jnp.sqrt(nv * c2) + ADAM_EPS) + ADAM_WD * w_refs[i][...])

    shapes = [jax.ShapeDtypeStruct(w.shape, F32) for w in ws]
    outs = pl.pallas_call(body, name=name, out_shape=shapes * 3, compiler_params=_params(48 << 20))(
        *ws, *gs, *ms, *vs)
    return outs[:n], outs[n:2 * n], outs[2 * n:]


def _adamw_nd(w, g, m, v, name):
    shape = w.shape
    c = shape[-1]
    outs = _adamw(w.reshape(-1, c), g.reshape(-1, c), m.reshape(-1, c), v.reshape(-1, c), name)
    return [o.reshape(shape) for o in outs]


BIG = ("w_in", "w_glu", "w_out", "w_up", "w_down")
SMALL = ("b_ada", "g_pre_mix", "g_post_mix", "attn_sinks", "lam_re", "lam_im", "log_step", "ssm_b_re", "ssm_b_im",
         "ssm_c_re", "ssm_c_im", "ssm_d", "g_attn_out", "g_ssm_out", "g_pre_ffn", "g_post_ffn", "conv_b")
ORDER = ("w_ada", "b_ada", "g_pre_mix", "g_post_mix", "w_in", "attn_sinks", "lam_re", "lam_im", "log_step",
         "ssm_b_re", "ssm_b_im", "ssm_c_re", "ssm_c_im", "ssm_d", "w_glu", "g_attn_out", "g_ssm_out", "w_out",
         "g_pre_ffn", "g_post_ffn", "w_up", "conv_w", "conv_b", "w_down")
COL_SHARDED = ("w_in", "w_up")


def kernel(x, c, w_ada, b_ada, g_pre_mix, g_post_mix, w_in, attn_sinks, lam_re, lam_im, log_step, ssm_b_re, ssm_b_im, ssm_c_re, ssm_c_im, ssm_d, w_glu, g_attn_out, g_ssm_out, w_out, g_pre_ffn, g_post_ffn, w_up, conv_w, conv_b, w_down, loss_target, m_w_ada, m_b_ada, m_g_pre_mix, m_g_post_mix, m_w_in, m_attn_sinks, m_lam_re, m_lam_im, m_log_step, m_ssm_b_re, m_ssm_b_im, m_ssm_c_re, m_ssm_c_im, m_ssm_d, m_w_glu, m_g_attn_out, m_g_ssm_out, m_w_out, m_g_pre_ffn, m_g_post_ffn, m_w_up, m_conv_w, m_conv_b, m_w_down, v_w_ada, v_b_ada, v_g_pre_mix, v_g_post_mix, v_w_in, v_attn_sinks, v_lam_re, v_lam_im, v_log_step, v_ssm_b_re, v_ssm_b_im, v_ssm_c_re, v_ssm_c_im, v_ssm_d, v_w_glu, v_g_attn_out, v_g_ssm_out, v_w_out, v_g_pre_ffn, v_g_post_ffn, v_w_up, v_conv_w, v_conv_b, v_w_down):
    env = dict(locals())
    W = {n: env[n] for n in ORDER}
    M = {n: env["m_" + n] for n in ORDER}
    V = {n: env["v_" + n] for n in ORDER}

    depth = w_ada.shape[0]
    s, d = x.shape[1], x.shape[2]
    xs0 = x.reshape(s, d)
    tgt = loss_target.reshape(s, d)
    attn_w = d // 2
    ssm_w = d - attn_w
    in_cols = w_in.shape[2] * N_DEV
    kv_dim = (in_cols - attn_w - ssm_w) // 2
    n_q, n_kv = attn_w // HEAD_DIM, kv_dim // HEAD_DIM
    n_grp = ssm_w // SSM_GROUP
    nb = ssm_w // LANES
    f = w_down.shape[1] * N_DEV
    ucol = (attn_w + 2 * kv_dim) // LANES
    t_len = _ssm_chunk(s)
    me = 4 * lax.axis_index("x") + 2 * lax.axis_index("y") + lax.axis_index("c")

    def at_block(ref, idx):
        return ref.at[idx]

    def at_rows(n_rows):
        return lambda ref, idx: ref.at[:, pl.ds(pl.multiple_of(idx * n_rows, 8), n_rows), :]

    def at_cols(n_cols):
        return lambda ref, idx: ref.at[:, :, pl.ds(pl.multiple_of(idx * n_cols, LANES), n_cols)]

    w_in_shard = w_in.shape[1:]
    w_in_dense = (math.prod(w_in_shard) // LANES, LANES)
    w_in_first = w_in.astype(BF16).reshape((depth,) + w_in_dense)
    first = _gather_multi([w_in_first, conv_w, c],
                          [(N_DEV, depth) + w_in_dense, (N_DEV,) + conv_w.shape, (N_DEV,) + c.shape],
                          [at_block, at_block, at_block], "ag_first")
    w_in_full = _cols_from_blocks(first[0].reshape(N_DEV, depth, *w_in_shard), "w_in_layout")
    conv_w_full = jnp.transpose(first[1], (1, 2, 0, 3)).reshape(depth, 3, 2 * f)
    c_all = first[2].reshape(N_DEV, d)

    def at_rows2(n_rows):
        return lambda ref, idx: ref.at[pl.ds(pl.multiple_of(idx * n_rows, 8), n_rows), :]

    def at_cols2(n_cols):
        return lambda ref, idx: ref.at[:, pl.ds(pl.multiple_of(idx * n_cols, LANES), n_cols)]

    def whole(ref, idx):
        return ref

    def gather_kind(n):
        return "blk" if n == "w_in" else "cols" if n in COL_SHARDED else "rows"

    def gather_view(n):
        return {"blk": at_block, "cols": at_cols2(W[n].shape[2]), "rows": at_rows2(W[n].shape[1])}[gather_kind(n)]

    def gather_shape(n):
        _, a, b = W[n].shape
        return {"blk": (N_DEV,) + w_in_dense, "cols": (a, N_DEV * b), "rows": (N_DEV * a, b)}[gather_kind(n)]

    later = [(n, l) for l in range(depth) for n in BIG[1:]]
    later_srcs = [W[n][l].astype(BF16) for n, l in later]
    later_views = [gather_view(n) for n, _ in later]
    me_arr = me.astype(jnp.int32).reshape(1)
    lands = [_place_own(me_arr, src, lax.empty(gather_shape(n), BF16), gather_kind(n), f"ag_own_{n}{l}")
             for (n, l), src in zip(later, later_srcs)]
    ag_started, ag_token = _exchange_start(later_srcs, lands, [whole] * len(later), later_views, "ag_start")

    def weights_arrived(names, l, after, name):
        picks = [later.index((n, l)) for n in names]
        _, got = _exchange_wait([ag_started[i] for i in picks], [after], [whole] * len(picks),
                                [later_views[i] for i in picks], name)
        return dict(zip(names, got))

    c_pad = jnp.pad(c_all, ((0, 16 - N_DEV), (0, 0)))
    n_ada = w_ada.shape[2]
    b_shard = lax.dynamic_slice_in_dim(b_ada, me * n_ada, n_ada, axis=1).reshape(depth, 1, n_ada)
    ada_part, c_act = _ada_fwd(c_pad, w_ada, b_shard, "ada_fwd")
    ada_all = _all_gather(ada_part.reshape(depth * 16, n_ada), "ag_ada").reshape(N_DEV, depth, 16, n_ada)
    ada_me = lax.dynamic_index_in_dim(ada_all, me, axis=2, keepdims=False)
    ada = jnp.transpose(ada_me, (1, 0, 2)).reshape(depth, 6, 1, d) + ag_token[0, 0]

    gp = n_grp * STATE

    def hgp(a):
        return jnp.transpose(a, (2, 0, 1)).reshape(SSM_GROUP, gp)

    ssm = []
    for l in range(depth):
        lr, li = lam_re[l].reshape(1, gp), lam_im[l].reshape(1, gp)
        ls = jnp.repeat(log_step[l], STATE).reshape(1, gp)
        br, bi = hgp(ssm_b_re[l]), hgp(ssm_b_im[l])
        bbr, bbi, tab_r, tab_i = _ssm_params_fwd(lr, li, ls, br, bi, f"ssm_params_fwd{l}")
        bb_re = jnp.transpose(bbr.reshape(SSM_GROUP, n_grp, STATE), (1, 2, 0))
        bb_im = jnp.transpose(bbi.reshape(SSM_GROUP, n_grp, STATE), (1, 2, 0))
        bbd = jnp.concatenate([_to_blocks(bb_re), _to_blocks(bb_im)], axis=2).astype(BF16)
        c_re_t = jnp.transpose(ssm_c_re[l], (0, 2, 1))
        c_im_t = jnp.transpose(ssm_c_im[l], (0, 2, 1))
        ccat = jnp.concatenate([jnp.transpose(_to_blocks(c_re_t), (0, 2, 1)),
                                -jnp.transpose(_to_blocks(c_im_t), (0, 2, 1))], axis=1).astype(BF16)

        def tab(t):
            return t.reshape(TAB_ROWS, nb, BLOCK_STATES)

        apow = jnp.transpose(jnp.concatenate([tab(tab_r), tab(tab_i)], axis=2), (1, 0, 2))
        ssm.append(dict(lr=lr, li=li, ls=ls, br=br, bi=bi, bbd=bbd, ccat=ccat, apow=apow,
                        dskip=ssm_d[l].reshape(1, ssm_w)))

    sinks_pad = jnp.pad(attn_sinks, ((0, 0), (0, LANES - n_q)))

    def vec(a):
        return a.reshape(1, -1)

    saved = []
    fw = [dict() for _ in range(depth)]
    xin = xs0
    for l in range(depth):
        sh_m, sc_m, gt_m, sh_f, sc_f, gt_f = (ada[l, i] for i in range(6))
        p = ssm[l]
        h1 = _modnorm_fwd(xin, vec(g_pre_mix[l]), sc_m, sh_m, f"modnorm_mix_fwd{l}")
        proj = _matmul(h1, w_in_full[l], name=f"mm_in{l}")
        attn, lse = _attn_fwd(proj, sinks_pad[l:l + 1], n_q=n_q, n_kv=n_kv, name=f"attn_fwd{l}")
        y, z, xstart = _ssm_fwd(proj, ucol, p["bbd"], p["ccat"], p["dskip"], p["apow"], t_len, name=f"ssm_fwd{l}")
        fw[l].update(weights_arrived(("w_glu", "w_out"), l, z, f"ag_wait_mix{l}"))
        gl = _matmul(z, fw[l]["w_glu"], name=f"mm_glu{l}")
        merged = _merge_fwd(attn, y, gl, vec(g_attn_out[l]), vec(g_ssm_out[l]), f"merge_fwd{l}")
        mix = _matmul(merged, fw[l]["w_out"], name=f"mm_out{l}")
        x2 = _resnorm_fwd(xin, mix, vec(g_post_mix[l]), gt_m, f"resnorm_mix_fwd{l}")
        h2 = _modnorm_fwd(x2, vec(g_pre_ffn[l]), sc_f, sh_f, f"modnorm_ffn_fwd{l}")
        fw[l].update(weights_arrived(("w_up",), l, h2, f"ag_wait_up{l}"))
        up0 = _matmul(h2, fw[l]["w_up"], name=f"mm_up{l}")
        cw, cb = conv_w_full[l], vec(conv_b[l])
        act = _gate_fwd(up0, cw, cb, f"gate_fwd{l}")
        fw[l].update(weights_arrived(("w_down",), l, act, f"ag_wait_down{l}"))
        ff = _matmul(act, fw[l]["w_down"], name=f"mm_down{l}")
        x3 = _resnorm_fwd(x2, ff, vec(g_post_ffn[l]), gt_f, f"resnorm_ffn_fwd{l}")
        saved.append(dict(xin=xin, h1=h1, proj=proj, attn=attn, lse=lse, y=y, z=z, xstart=xstart, gl=gl,
                          merged=merged, mix=mix, x2=x2, h2=h2, up0=up0, act=act, ff=ff))
        xin = x3

    dxo, loss_acc = _loss_bwd(xin, tgt, "loss")
    loss = lax.psum(loss_acc[0, 0], ("x", "y", "c"))

    grads = {n: [None] * depth for n in ORDER}
    dada = [None] * depth
    big_blocks = {n: [None] * depth for n in BIG}
    seg = jnp.pad(jnp.repeat(jnp.eye(n_grp, dtype=F32), STATE, axis=0), ((0, 0), (0, (-n_grp) % LANES)))

    def part_view(n):
        shp = W[n].shape
        if n == "w_in":
            return at_block, "blk"
        if n in COL_SHARDED:
            return at_cols2(shp[2]), "cols"
        return at_rows2(shp[1]), "rows"

    rs_groups, start_tokens = [], []
    small_order = SMALL + ("conv_w",)
    small_shapes = {n: W[n].shape for n in SMALL}
    small_shapes["conv_w"] = (depth, 3, 2 * f)
    small_started = [None] * depth

    def send_partials(items, name):
        parts = [big_blocks[n][l] for n, l in items]
        lands = [lax.empty((N_DEV,) + (w_in_dense if n == "w_in" else W[n].shape[1:]), BF16) for n, _ in items]
        started, token = _exchange_start(parts, lands, [part_view(n)[0] for n, _ in items],
                                         [at_block] * len(items), name)
        rs_groups.append((items, started, name))
        start_tokens.append(token)
        return token[0, 0]

    order = jnp.zeros((), F32)
    for l in reversed(range(depth)):
        sh_m, sc_m, gt_m, sh_f, sc_f, gt_f = (ada[l, i] for i in range(6))
        gt_f = gt_f + order
        a, p = saved[l], ssm[l]
        cw, cb = conv_w_full[l], vec(conv_b[l])
        dff, dg, dgt_f = _resnorm_bwd(dxo, a["ff"], vec(g_post_ffn[l]), gt_f, f"resnorm_ffn_bwd{l}")
        grads["g_post_ffn"][l] = dg
        dact = _matmul(dff, fw[l]["w_down"], tb=True, name=f"mm_down_dx{l}")
        big_blocks["w_down"][l] = _matmul(a["act"], dff, ta=True, out_dtype=BF16, name=f"mm_down_dw{l}")
        dup, dcb, dcw = _gate_bwd(a["up0"], cw, cb, dact, f"gate_bwd{l}")
        grads["conv_b"][l] = dcb.reshape(1, 2 * f)
        grads["conv_w"][l] = jnp.transpose(dcw, (1, 0, 2)).reshape(3, 2 * f)
        dup0 = _conv_bwd(dup, cw, f"conv_bwd{l}")
        dh2 = _matmul(dup0, fw[l]["w_up"], tb=True, name=f"mm_up_dx{l}")
        big_blocks["w_up"][l] = _matmul(a["h2"], dup0, ta=True, out_dtype=BF16, name=f"mm_up_dw{l}")
        if l == 0:
            sc_f = sc_f + send_partials([("w_down", 0), ("w_up", 0)], "rs_start_ffn0")
        dx2, dg, dsc_f, dsh_f = _modnorm_bwd(dh2, a["x2"], vec(g_pre_ffn[l]), sc_f, dxo, f"modnorm_ffn_bwd{l}")
        grads["g_pre_ffn"][l] = dg
        dmix, dg, dgt_m = _resnorm_bwd(dx2, a["mix"], vec(g_post_mix[l]), gt_m, f"resnorm_mix_bwd{l}")
        grads["g_post_mix"][l] = dg
        dmerged = _matmul(dmix, fw[l]["w_out"], tb=True, name=f"mm_out_dx{l}")
        big_blocks["w_out"][l] = _matmul(a["merged"], dmix, ta=True, out_dtype=BF16, name=f"mm_out_dw{l}")
        dattn, dgl, dzd, dga, dgs = _merge_bwd(dmerged, a["attn"], a["y"], a["gl"], vec(g_attn_out[l]),
                                               vec(g_ssm_out[l]), f"merge_bwd{l}")
        grads["g_attn_out"][l], grads["g_ssm_out"][l] = dga, dgs
        dz2 = _matmul(dgl, fw[l]["w_glu"], tb=True, name=f"mm_glu_dx{l}")
        big_blocks["w_glu"][l] = _matmul(a["z"], dgl, ta=True, out_dtype=BF16, name=f"mm_glu_dw{l}")
        dskip = p["dskip"]
        if l == 0:
            dskip = dskip + send_partials([("w_out", 0), ("w_glu", 0)], "rs_start_mix0")
        du, dbbd, dccat, dd, da = _ssm_bwd(a["proj"], ucol, a["y"], dzd, dz2, a["xstart"], p["bbd"], p["ccat"],
                                           dskip, p["apow"], t_len, name=f"ssm_bwd{l}")
        grads["ssm_d"][l] = dd
        hs = BLOCK_STATES
        dbb_re = _from_blocks(dbbd[:, :, :hs], STATE, SSM_GROUP)
        dbb_im = _from_blocks(dbbd[:, :, hs:], STATE, SSM_GROUP)
        dccat_t = jnp.transpose(dccat, (0, 2, 1))
        grads["ssm_c_re"][l] = jnp.transpose(_from_blocks(dccat_t[:, :, :hs], STATE, SSM_GROUP), (0, 2, 1))
        grads["ssm_c_im"][l] = -jnp.transpose(_from_blocks(dccat_t[:, :, hs:], STATE, SSM_GROUP), (0, 2, 1))
        dab_re, dab_im = da[:, 0, :hs].reshape(1, gp), da[:, 0, hs:].reshape(1, gp)
        dlr, dli, dls, dbr, dbi = _ssm_params_bwd(p["lr"], p["li"], p["ls"], p["br"], p["bi"], dab_re, dab_im,
                                                  hgp(dbb_re), hgp(dbb_im), seg, f"ssm_params_bwd{l}")
        grads["lam_re"][l], grads["lam_im"][l], grads["log_step"][l] = dlr, dli, dls[0, :n_grp]
        grads["ssm_b_re"][l] = jnp.transpose(dbr.reshape(SSM_GROUP, n_grp, STATE), (1, 2, 0))
        grads["ssm_b_im"][l] = jnp.transpose(dbi.reshape(SSM_GROUP, n_grp, STATE), (1, 2, 0))
        dq, dk, dv, dsink = _attn_bwd(a["proj"], sinks_pad[l:l + 1], a["attn"], a["lse"], dattn,
                                      n_q=n_q, n_kv=n_kv, name=f"attn_bwd{l}")
        grads["attn_sinks"][l] = dsink[0, :n_q]
        dproj = jnp.concatenate([dq, dk, dv, du], axis=1).astype(BF16)
        dh1 = _matmul(dproj, w_in_full[l], tb=True, name=f"mm_in_dx{l}")
        big_blocks["w_in"][l] = _blocks_from_cols(_matmul(a["h1"], dproj, ta=True, name=f"mm_in_dw{l}"),
                                                  f"w_in_grad_layout{l}").reshape((N_DEV,) + w_in_dense)
        dxo, dg, dsc_m, dsh_m = _modnorm_bwd(dh1, a["xin"], vec(g_pre_mix[l]), sc_m, dx2, f"modnorm_mix_bwd{l}")
        grads["g_pre_mix"][l] = dg
        dada[l] = jnp.concatenate([dsh_m, dsc_m, dgt_m, dsh_f, dsc_f, dgt_f], axis=1)
        if l > 0:
            order = send_partials([(n, l) for n in reversed(BIG)], f"rs_start_layer{l}")
        else:
            order = order + send_partials([("w_in", 0)], "rs_start_in0")
        spack = _pack([dada[l]] + [grads[n][l] for n in small_order[1:]], F32, 1024)
        started, token = _exchange_start([spack], [lax.empty((N_DEV,) + spack.shape, F32)], [whole], [at_block],
                                         f"small_start{l}")
        small_started[l] = started
        start_tokens.append(token)
        order = order + token[0, 0]
    grad_x = dxo.reshape(x.shape)

    delta, new_m, new_v = {}, {}, {}
    stacked = {n: None for n in BIG}
    landed_layers = {n: 0 for n in BIG}
    after = [dxo] + start_tokens
    for items, started, name in rs_groups:
        mine, landed = _exchange_wait(started, after, [part_view(n)[0] for n, _ in items], [at_block] * len(items),
                                      name.replace("start", "wait"))
        for (n, l), part, slots in zip(items, mine, landed):
            stacked[n] = _sum_slots_own(me_arr, slots, part, part_view(n)[1], f"rs_sum_{n}{l}", layer=l,
                                        n_layers=depth, stacked=stacked[n])
            landed_layers[n] += 1
            if landed_layers[n] == depth:
                grads[n] = stacked[n].reshape(W[n].shape)
                delta[n], new_m[n], new_v[n] = _adamw_nd(W[n], grads[n], M[n], V[n], f"adamw_{n}")
                after.append(delta[n])

    n_cw = conv_w.shape[2]
    small_sums, dada_rows = [None] * depth, [None] * depth
    for l in reversed(range(depth)):
        mine, landed = _exchange_wait(small_started[l], after, [whole], [at_block], f"small_wait{l}")
        ssum = _sum_slots_own(me_arr, landed[0], mine[0], "self", f"sum_small{l}").reshape(-1)
        small_sums[l] = _unpack(ssum, [small_shapes[n][1:] for n in small_order])
        slot = lax.broadcasted_iota(jnp.int32, (N_DEV, 6 * d), 0)
        dada_rows[l] = jnp.where(slot == me, mine[0].reshape(-1)[:6 * d][None],
                                 landed[0].reshape(N_DEV, -1)[:, :6 * d])
    for i, n in enumerate(small_order):
        grads[n] = jnp.stack([small_sums[l][i] for l in range(depth)])
    grads["conv_w"] = lax.dynamic_slice_in_dim(grads["conv_w"], me * n_cw, n_cw, axis=2)
    dada_all = jnp.stack(dada_rows, axis=1)
    dada_shard = lax.dynamic_slice_in_dim(dada_all, me * n_ada, n_ada, axis=2)
    kp = LANES
    dada_pad = jnp.pad(jnp.transpose(dada_shard, (1, 0, 2)), ((0, 0), (0, kp - N_DEV), (0, 0)))
    act_t = jnp.pad(jnp.transpose(c_act[:N_DEV]), ((0, 0), (0, kp - N_DEV)))
    grads["w_ada"] = _ada_wgrad(act_t, dada_pad, "ada_wgrad")

    delta["w_ada"], new_m["w_ada"], new_v["w_ada"] = _adamw_nd(W["w_ada"], grads["w_ada"], M["w_ada"], V["w_ada"],
                                                                "adamw_w_ada")

    def lane_friendly(a):
        return a.reshape(-1, 1024) if a.size % 1024 == 0 else a

    rest = SMALL + ("conv_w",)
    outs = _adamw_many(*[[lane_friendly(t[n]) for n in rest] for t in (W, grads, M, V)], "adamw_small")
    for tgt_d, vals in zip((delta, new_m, new_v), outs):
        for n, val in zip(rest, vals):
            tgt_d[n] = val.reshape(W[n].shape)

    return (loss, grad_x, *[grads[n] for n in ORDER], *[delta[n] for n in ORDER],
            *[new_m[n] for n in ORDER], *[new_v[n] for n in ORDER])
```

```python
import functools
import math

import jax
import jax.numpy as jnp
from jax import lax
from jax.experimental import pallas as pl
from jax.experimental.pallas import tpu as pltpu

F32 = jnp.float32
BF16 = jnp.bfloat16

N_DEV = 8
HEAD_DIM = 64
WINDOW = 128
SSM_GROUP = 16
STATE = 64
LANES = 128
GROUPS_PER_BLOCK = LANES // SSM_GROUP
BLOCK_STATES = GROUPS_PER_BLOCK * STATE
EPS = 1e-6
NEG = -1e30
ADAM_LR, ADAM_B1, ADAM_B2, ADAM_EPS, ADAM_WD, ADAM_STEP = 0.001, 0.9, 0.999, 1e-08, 0.01, 10
VMEM_BYTES_V7X = 64 * 1024 * 1024
GELU_C = math.sqrt(2.0 / math.pi)
MESH = pl.DeviceIdType.MESH
ANY = pl.BlockSpec(memory_space=pl.ANY)


def _pick(n, pref, align):
    t = (min(pref, n) // align) * align
    while t >= align:
        if n % t == 0:
            return t
        t -= align
    return n


def _params(vmem_bytes=None):
    if vmem_bytes is None:
        return pltpu.CompilerParams()
    return pltpu.CompilerParams(vmem_limit_bytes=int(min(vmem_bytes, VMEM_BYTES_V7X - (8 << 20))))


def _gelu_and_grad(x):
    x2 = x * x
    half_x = 0.5 * x
    th = jnp.tanh((GELU_C * x) * (1.0 + 0.044715 * x2))
    one_th = 1.0 + th
    grad = 0.5 * one_th + (half_x * (1.0 - th * th)) * (GELU_C + (3.0 * 0.044715 * GELU_C) * x2)
    return half_x * one_th, grad


def _gelu(x):
    return _gelu_and_grad(x)[0]


def _gelu_grad(x):
    return _gelu_and_grad(x)[1]


def _rstd(x):
    return lax.rsqrt(jnp.mean(x * x, axis=-1, keepdims=True) + EPS)


def _norm_bwd(dhat, xhat, r):
    return r * (dhat - xhat * jnp.mean(dhat * xhat, axis=-1, keepdims=True))


def _matmul(a, b, *, ta=False, tb=False, out_dtype=F32, name):
    (kdim, m) = a.shape if ta else a.shape[::-1]
    (n, k2) = b.shape if tb else b.shape[::-1]
    assert kdim == k2, (a.shape, b.shape, ta, tb)
    tm, tn, tk = _pick(m, 1024, LANES), _pick(n, 1024, LANES), _pick(kdim, 2816, LANES)
    nk = kdim // tk
    dn = (((0 if ta else 1,), (1 if tb else 0,)), ((), ()))

    def partial_product(a_ref, b_ref):
        return lax.dot_general(a_ref[...].astype(BF16), b_ref[...].astype(BF16), dn, preferred_element_type=F32)

    def body_one(a_ref, b_ref, o_ref):
        o_ref[...] = partial_product(a_ref, b_ref).astype(o_ref.dtype)

    def body_acc(a_ref, b_ref, o_ref, acc_ref):
        k = pl.program_id(2)

        @pl.when(k == 0)
        def _():
            acc_ref[...] = partial_product(a_ref, b_ref)

        @pl.when((k > 0) & (k < nk - 1))
        def _():
            acc_ref[...] += partial_product(a_ref, b_ref)

        @pl.when(k == nk - 1)
        def _():
            o_ref[...] = (acc_ref[...] + partial_product(a_ref, b_ref)).astype(o_ref.dtype)

    body = body_one if nk == 1 else body_acc
    a_spec = pl.BlockSpec((tk, tm), lambda i, j, k: (k, i)) if ta else pl.BlockSpec((tm, tk), lambda i, j, k: (i, k))
    b_spec = pl.BlockSpec((tn, tk), lambda i, j, k: (j, k)) if tb else pl.BlockSpec((tk, tn), lambda i, j, k: (k, j))
    vmem = (2 * (tm * tk * a.dtype.itemsize + tk * tn * b.dtype.itemsize) + tm * tn * 4
            + 2 * tm * tn * jnp.dtype(out_dtype).itemsize + 3 * tm * tn * 4 + (4 << 20))
    return pl.pallas_call(
        body, name=name, grid=(m // tm, n // tn, nk),
        in_specs=[a_spec, b_spec], out_specs=pl.BlockSpec((tm, tn), lambda i, j, k: (i, j)),
        out_shape=jax.ShapeDtypeStruct((m, n), out_dtype),
        scratch_shapes=[] if nk == 1 else [pltpu.VMEM((tm, tn), F32)],
        compiler_params=_params(vmem),
    )(a, b)


def _all_gather(x, name):
    def body(x_ref, out_ref, send_sems, recv_sems, local_sem):
        x_, y_, c_ = lax.axis_index("x"), lax.axis_index("y"), lax.axis_index("c")
        me, sibling = (x_, y_, c_), (x_, y_, 1 - c_)
        chips = [(1 - x_, y_), (x_, 1 - y_), (1 - x_, 1 - y_)]

        def slot(px, py, pc):
            return out_ref.at[4 * px + 2 * py + pc]

        def copy(k, block, to, src=None):
            return pltpu.make_async_remote_copy(
                src_ref=slot(*block) if src is None else src, dst_ref=slot(*block),
                send_sem=send_sems.at[k], recv_sem=recv_sems.at[k], device_id=to, device_id_type=MESH)

        mine = pltpu.make_async_copy(x_ref, slot(*me), local_sem)
        mine.start()
        first = [copy(0, me, sibling, src=x_ref)]
        first += [copy(1 + j, me, (*chip, c_), src=x_ref) for j, chip in enumerate(chips)]
        for cp in first:
            cp.start()
        passed = [copy(4 + j, (*chip, c_), sibling) for j, chip in enumerate(chips)]
        for j, chip in enumerate(chips):
            copy(1 + j, (*chip, c_), me).wait_recv()
            passed[j].start()
        copy(0, sibling, me).wait_recv()
        for j, chip in enumerate(chips):
            copy(4 + j, (*chip, 1 - c_), me).wait_recv()
        for cp in first + passed:
            cp.wait_send()
        mine.wait()

    return pl.pallas_call(
        body, name=name, out_shape=jax.ShapeDtypeStruct((N_DEV,) + x.shape, x.dtype),
        in_specs=[ANY], out_specs=ANY,
        scratch_shapes=[pltpu.SemaphoreType.DMA((7,)), pltpu.SemaphoreType.DMA((7,)), pltpu.SemaphoreType.DMA],
    )(x)


def _gather_multi(srcs, out_shapes, views, name):
    n = len(srcs)

    def body(*refs):
        src_refs, out_refs = refs[:n], refs[n:2 * n]
        send_sems, recv_sems, local_sems = refs[2 * n:]
        x_, y_, c_ = lax.axis_index("x"), lax.axis_index("y"), lax.axis_index("c")
        me, sibling = (x_, y_, c_), (x_, y_, 1 - c_)
        chips = [(1 - x_, y_), (x_, 1 - y_), (1 - x_, 1 - y_)]

        def slot(i, px, py, pc):
            return views[i](out_refs[i], 4 * px + 2 * py + pc)

        def copy(i, k, block, to, from_src=False):
            return pltpu.make_async_remote_copy(
                src_ref=src_refs[i] if from_src else slot(i, *block), dst_ref=slot(i, *block),
                send_sem=send_sems.at[7 * i + k], recv_sem=recv_sems.at[7 * i + k], device_id=to, device_id_type=MESH)

        mine = [pltpu.make_async_copy(src_refs[i], slot(i, *me), local_sems.at[i]) for i in range(n)]
        for cp in mine:
            cp.start()
        first = []
        for i in range(n):
            first.append(copy(i, 0, me, sibling, True))
            first += [copy(i, 1 + j, me, (*chip, c_), True) for j, chip in enumerate(chips)]
        for cp in first:
            cp.start()
        passed = []
        for j, chip in enumerate(chips):
            for i in range(n):
                copy(i, 1 + j, (*chip, c_), me).wait_recv()
                fwd = copy(i, 4 + j, (*chip, c_), sibling)
                fwd.start()
                passed.append(fwd)
        for i in range(n):
            copy(i, 0, sibling, me).wait_recv()
            for j, chip in enumerate(chips):
                copy(i, 4 + j, (*chip, 1 - c_), me).wait_recv()
        for cp in first + passed:
            cp.wait_send()
        for cp in mine:
            cp.wait()

    return pl.pallas_call(
        body, name=name, out_shape=[jax.ShapeDtypeStruct(s, a.dtype) for s, a in zip(out_shapes, srcs)],
        in_specs=[ANY] * n, out_specs=[ANY] * n,
        scratch_shapes=[pltpu.SemaphoreType.DMA((7 * n,)), pltpu.SemaphoreType.DMA((7 * n,)),
                        pltpu.SemaphoreType.DMA((n,))],
    )(*srcs)


HBM_SPEC = pl.BlockSpec(memory_space=pltpu.HBM)
SEM_SPEC = pl.BlockSpec(memory_space=pltpu.SEMAPHORE)
SIDE_EFFECT = pltpu.SideEffectType.DATAFLOW_SIDE_EFFECTING
N_PEERS = N_DEV - 1


def _peer(k, x_, y_, c_):
    px = 1 - x_ if (k >> 2) & 1 else x_
    py = 1 - y_ if (k >> 1) & 1 else y_
    pc = 1 - c_ if k & 1 else c_
    return (px, py, pc), 4 * px + 2 * py + pc


def _exchange_copies(src_refs, land_refs, send_sems, recv_sems, src_views, dst_views):
    x_, y_, c_ = lax.axis_index("x"), lax.axis_index("y"), lax.axis_index("c")
    me = 4 * x_ + 2 * y_ + c_
    out = []
    for i in range(len(src_refs)):
        for k in range(1, N_DEV):
            peer, idx = _peer(k, x_, y_, c_)

            def copy(dst_slot, i=i, k=k, peer=peer, idx=idx):
                return pltpu.make_async_remote_copy(
                    src_ref=src_views[i](src_refs[i], idx), dst_ref=dst_views[i](land_refs[i], dst_slot),
                    send_sem=send_sems[i].at[k - 1], recv_sem=recv_sems[i].at[k - 1], device_id=peer,
                    device_id_type=MESH)

            out.append((copy(me), copy(idx)))
    return out


def _exchange_start(srcs, lands, src_views, dst_views, name):
    n = len(srcs)

    def body(*refs):
        src_refs, land_refs = refs[:n], refs[n:2 * n]
        send_sems, recv_sems = refs[2 * n:3 * n], refs[3 * n:4 * n]
        token = refs[-1]
        for send, _ in _exchange_copies(src_refs, land_refs, send_sems, recv_sems, src_views, dst_views):
            send.start()
        token[...] = jnp.zeros_like(token)

    sems = [pltpu.SemaphoreType.DMA((N_PEERS,))] * n
    thru = [pltpu.HBM(a.shape, a.dtype) for a in list(srcs) + list(lands)]
    outs = pl.pallas_call(
        body, name=name, out_shape=sems + sems + thru + [jax.ShapeDtypeStruct((8, LANES), F32)],
        in_specs=[HBM_SPEC] * (2 * n),
        out_specs=[SEM_SPEC] * (2 * n) + [HBM_SPEC] * (2 * n) + [pl.BlockSpec(memory_space=pltpu.VMEM)],
        input_output_aliases={j: 2 * n + j for j in range(2 * n)},
        compiler_params=pltpu.CompilerParams(has_side_effects=SIDE_EFFECT),
    )(*[pltpu.with_memory_space_constraint(a, pltpu.HBM) for a in list(srcs) + list(lands)])
    per_array = [(outs[j], outs[n + j], outs[2 * n + j], outs[3 * n + j]) for j in range(n)]
    return per_array, outs[-1]


def _exchange_wait(started, after, src_views, dst_views, name):
    send_sems, recv_sems, srcs, lands = (list(t) for t in zip(*started))
    n = len(srcs)
    after = list(after)

    def body(*refs):
        src_refs, land_refs = refs[:n], refs[n:2 * n]
        send_refs, recv_refs = refs[2 * n:3 * n], refs[3 * n:4 * n]
        copies = _exchange_copies(src_refs, land_refs, send_refs, recv_refs, src_views, dst_views)
        for send, _ in copies:
            send.wait_send()
        for _, recv in copies:
            recv.wait_recv()

    thru = [pltpu.HBM(a.shape, a.dtype) for a in list(srcs) + list(lands)]
    outs = pl.pallas_call(
        body, name=name, out_shape=thru,
        in_specs=[HBM_SPEC] * (2 * n) + [SEM_SPEC] * (2 * n) + [ANY] * len(after),
        out_specs=[HBM_SPEC] * (2 * n),
        input_output_aliases={j: j for j in range(2 * n)},
        compiler_params=pltpu.CompilerParams(has_side_effects=SIDE_EFFECT),
    )(*srcs, *lands, *send_sems, *recv_sems, *after)
    return outs[:n], outs[n:]


def _place_own(me, src, land, kind, name):
    r, c = src.shape
    tr = _pick(r, 512, 16)
    nt = r // tr
    if kind == "rows":
        out_spec = pl.BlockSpec((tr, c), lambda i, mr: (mr[0] * nt + i, 0))
    elif kind == "cols":
        out_spec = pl.BlockSpec((tr, c), lambda i, mr: (i, mr[0]))
    else:
        out_spec = pl.BlockSpec((None, tr, c), lambda i, mr: (mr[0], i, 0))

    def body(me_ref, s_ref, land_ref, o_ref):
        o_ref[...] = s_ref[...]

    return pl.pallas_call(
        body, name=name,
        grid_spec=pltpu.PrefetchScalarGridSpec(
            num_scalar_prefetch=1, grid=(nt,),
            in_specs=[pl.BlockSpec((tr, c), lambda i, mr: (i, 0)), ANY], out_specs=out_spec),
        out_shape=jax.ShapeDtypeStruct(land.shape, land.dtype),
        input_output_aliases={2: 0},
    )(me, src, land)


def _sum_slots_own(me, landed, part, kind, name, *, layer=0, n_layers=1, stacked=None):
    _, r, c = landed.shape
    tr = _pick(r, 512, 16)
    nt = r // tr
    if kind == "rows":
        part_spec = pl.BlockSpec((tr, c), lambda i, mr: (mr[0] * nt + i, 0))
    elif kind == "cols":
        part_spec = pl.BlockSpec((tr, c), lambda i, mr: (i, mr[0]))
    elif kind == "blk":
        part_spec = pl.BlockSpec((None, tr, c), lambda i, mr: (mr[0], i, 0))
    else:
        part_spec = pl.BlockSpec((tr, c), lambda i, mr: (i, 0))

    def body(me_ref, x_ref, p_ref, *rest):
        o_ref = rest[-1]
        own = p_ref[...].astype(F32)
        acc = jnp.where(me_ref[0] == 0, own, x_ref[0].astype(F32))
        for i in range(1, N_DEV):
            acc = acc + jnp.where(me_ref[0] == i, own, x_ref[i].astype(F32))
        o_ref[...] = acc

    operands = [me, landed, part] + ([] if stacked is None else [stacked])
    return pl.pallas_call(
        body, name=name,
        grid_spec=pltpu.PrefetchScalarGridSpec(
            num_scalar_prefetch=1, grid=(nt,),
            in_specs=[pl.BlockSpec((N_DEV, tr, c), lambda i, mr: (0, i, 0)), part_spec]
            + ([] if stacked is None else [ANY]),
            out_specs=pl.BlockSpec((None, tr, c), lambda i, mr: (layer, i, 0))),
        out_shape=jax.ShapeDtypeStruct((n_layers, r, c), F32),
        input_output_aliases={} if stacked is None else {3: 0},
        compiler_params=_params(2 * N_DEV * tr * c * landed.dtype.itemsize + 8 * tr * c * 4 + (4 << 20)),
    )(*operands)


def _cols_from_blocks(blk, name):
    nd, nl, k, n = blk.shape
    tk = _pick(k, 256, 16)

    def body(b_ref, o_ref, wide_ref):
        for dev in range(nd):
            wide_ref[:, dev * n:(dev + 1) * n] = b_ref[dev].astype(F32)
        o_ref[...] = wide_ref[...].astype(o_ref.dtype)

    return pl.pallas_call(
        body, name=name, grid=(nl, k // tk),
        in_specs=[pl.BlockSpec((nd, None, tk, n), lambda l, i: (0, l, i, 0))],
        out_specs=pl.BlockSpec((None, tk, nd * n), lambda l, i: (l, i, 0)),
        out_shape=jax.ShapeDtypeStruct((nl, k, nd * n), BF16),
        scratch_shapes=[pltpu.VMEM((tk, nd * n), F32)],
    )(blk)


def _blocks_from_cols(full, name):
    k, n8 = full.shape
    n = n8 // N_DEV
    tk = _pick(k, 256, 16)

    def body(f_ref, o_ref):
        for dev in range(N_DEV):
            o_ref[dev] = f_ref[:, dev * n:(dev + 1) * n].astype(o_ref.dtype)

    return pl.pallas_call(
        body, name=name, grid=(k // tk,),
        in_specs=[pl.BlockSpec((tk, n8), lambda i: (i, 0))],
        out_specs=pl.BlockSpec((N_DEV, tk, n), lambda i: (0, i, 0)),
        out_shape=jax.ShapeDtypeStruct((N_DEV, k, n), BF16),
    )(full)


def _pack(arrs, dtype, cols):
    flat = jnp.concatenate([a.astype(dtype).reshape(-1) for a in arrs])
    unit = 16 * cols
    pad = (-flat.shape[0]) % unit
    flat = jnp.pad(flat, (0, pad))
    return flat.reshape(-1, cols)


def _unpack(flat, shapes):
    out, off = [], 0
    for s in shapes:
        n = math.prod(s)
        out.append(flat[off:off + n].reshape(s))
        off += n
    return out


def _ada_fwd(c_all, w_ada, b_shard, name):
    nl, d, n = w_ada.shape
    tn = _pick(n, 512, LANES)

    def body(c_ref, w_ref, b_ref, o_ref, act_ref):
        cv = c_ref[...]
        act = cv * jax.nn.sigmoid(cv)
        act_ref[...] = act
        o_ref[...] = jnp.dot(act.astype(BF16), w_ref[...].astype(BF16), preferred_element_type=F32) + b_ref[...]

    return pl.pallas_call(
        body, name=name, grid=(nl, n // tn),
        in_specs=[pl.BlockSpec(c_all.shape, lambda l, j: (0, 0)),
                  pl.BlockSpec((None, d, tn), lambda l, j: (l, 0, j)),
                  pl.BlockSpec((None, 1, tn), lambda l, j: (l, 0, j))],
        out_specs=[pl.BlockSpec((None, c_all.shape[0], tn), lambda l, j: (l, 0, j)),
                   pl.BlockSpec(c_all.shape, lambda l, j: (0, 0))],
        out_shape=[jax.ShapeDtypeStruct((nl, c_all.shape[0], n), F32), jax.ShapeDtypeStruct(c_all.shape, F32)],
        compiler_params=_params(2 * d * tn * 4 + d * tn * 2 + (8 << 20)),
    )(c_all, w_ada, b_shard)


def _ada_wgrad(act_t, dada, name):
    d, kp = act_t.shape
    nl, _, n = dada.shape
    tm = _pick(d, 512, 8)

    def body(a_ref, g_ref, o_ref):
        o_ref[...] = jnp.dot(a_ref[...].astype(BF16), g_ref[...].astype(BF16), preferred_element_type=F32)

    return pl.pallas_call(
        body, name=name, grid=(nl, d // tm),
        in_specs=[pl.BlockSpec((tm, kp), lambda l, i: (i, 0)), pl.BlockSpec((None, kp, n), lambda l, i: (l, 0, 0))],
        out_specs=pl.BlockSpec((None, tm, n), lambda l, i: (l, i, 0)),
        out_shape=jax.ShapeDtypeStruct((nl, d, n), F32),
        compiler_params=_params(4 * tm * n * 4 + 2 * kp * n * 4 + (8 << 20)),
    )(act_t, dada)


def _row_spec(tm, d):
    return pl.BlockSpec((tm, d), lambda i: (i, 0))


def _vec_spec(d):
    return pl.BlockSpec((1, d), lambda i: (0, 0))


def _modnorm_fwd(x, g, sc, sh, name):
    s, d = x.shape
    tm = _pick(s, 256, 16)

    def body(x_ref, g_ref, sc_ref, sh_ref, o_ref):
        xv = x_ref[...]
        o_ref[...] = ((xv * _rstd(xv)) * g_ref[...] * (1.0 + sc_ref[...]) + sh_ref[...]).astype(o_ref.dtype)

    return pl.pallas_call(
        body, name=name, grid=(s // tm,),
        in_specs=[_row_spec(tm, d), _vec_spec(d), _vec_spec(d), _vec_spec(d)], out_specs=_row_spec(tm, d),
        out_shape=jax.ShapeDtypeStruct((s, d), BF16),
    )(x, g, sc, sh)


def _modnorm_bwd(dh, x, g, sc, dres, name):
    s, d = x.shape
    tm = _pick(s, 256, 8)

    def body(dh_ref, x_ref, g_ref, sc_ref, dres_ref, dx_ref, dg_ref, dsc_ref, dsh_ref):
        @pl.when(pl.program_id(0) == 0)
        def _():
            dg_ref[...] = jnp.zeros_like(dg_ref)
            dsc_ref[...] = jnp.zeros_like(dsc_ref)
            dsh_ref[...] = jnp.zeros_like(dsh_ref)

        dh_, xv, gv = dh_ref[...], x_ref[...], g_ref[...]
        r = _rstd(xv)
        xhat = xv * r
        dn = dh_ * (1.0 + sc_ref[...])
        dsh_ref[...] += jnp.sum(dh_, axis=0, keepdims=True)
        dsc_ref[...] += jnp.sum(dh_ * (xhat * gv), axis=0, keepdims=True)
        dg_ref[...] += jnp.sum(dn * xhat, axis=0, keepdims=True)
        dx_ref[...] = _norm_bwd(dn * gv, xhat, r) + dres_ref[...]

    vec = jax.ShapeDtypeStruct((1, d), F32)
    return pl.pallas_call(
        body, name=name, grid=(s // tm,),
        in_specs=[_row_spec(tm, d), _row_spec(tm, d), _vec_spec(d), _vec_spec(d), _row_spec(tm, d)],
        out_specs=[_row_spec(tm, d), _vec_spec(d), _vec_spec(d), _vec_spec(d)],
        out_shape=[jax.ShapeDtypeStruct((s, d), F32), vec, vec, vec],
    )(dh, x, g, sc, dres)


def _resnorm_fwd(x, y, g, gt, name):
    s, d = x.shape
    tm = _pick(s, 256, 8)

    def body(x_ref, y_ref, g_ref, gt_ref, o_ref):
        yv = y_ref[...]
        o_ref[...] = x_ref[...] + (1.0 + gt_ref[...]) * ((yv * _rstd(yv)) * g_ref[...])

    return pl.pallas_call(
        body, name=name, grid=(s // tm,),
        in_specs=[_row_spec(tm, d), _row_spec(tm, d), _vec_spec(d), _vec_spec(d)], out_specs=_row_spec(tm, d),
        out_shape=jax.ShapeDtypeStruct((s, d), F32),
    )(x, y, g, gt)


def _resnorm_bwd(dxo, y, g, gt, name):
    s, d = y.shape
    tm = _pick(s, 256, 16)

    def body(dxo_ref, y_ref, g_ref, gt_ref, dy_ref, dg_ref, dgt_ref):
        @pl.when(pl.program_id(0) == 0)
        def _():
            dg_ref[...] = jnp.zeros_like(dg_ref)
            dgt_ref[...] = jnp.zeros_like(dgt_ref)

        dxo_, yv, gv = dxo_ref[...], y_ref[...], g_ref[...]
        r = _rstd(yv)
        yhat = yv * r
        dn = dxo_ * (1.0 + gt_ref[...])
        dgt_ref[...] += jnp.sum(dxo_ * (yhat * gv), axis=0, keepdims=True)
        dg_ref[...] += jnp.sum(dn * yhat, axis=0, keepdims=True)
        dy_ref[...] = _norm_bwd(dn * gv, yhat, r).astype(dy_ref.dtype)

    vec = jax.ShapeDtypeStruct((1, d), F32)
    return pl.pallas_call(
        body, name=name, grid=(s // tm,),
        in_specs=[_row_spec(tm, d), _row_spec(tm, d), _vec_spec(d), _vec_spec(d)],
        out_specs=[_row_spec(tm, d), _vec_spec(d), _vec_spec(d)],
        out_shape=[jax.ShapeDtypeStruct((s, d), BF16), vec, vec],
    )(dxo, y, g, gt)


def _loss_bwd(xf, tgt, name):
    s, d = xf.shape
    tm = _pick(s, 256, 8)

    def body(x_ref, t_ref, dy_ref, l_ref):
        @pl.when(pl.program_id(0) == 0)
        def _():
            l_ref[...] = jnp.zeros_like(l_ref)

        e = x_ref[...] - t_ref[...]
        dy_ref[...] = e * (1.0 / d)
        l_ref[...] += jnp.sum(e * e) * (0.5 / d)

    return pl.pallas_call(
        body, name=name, grid=(s // tm,),
        in_specs=[_row_spec(tm, d), _row_spec(tm, d)],
        out_specs=[_row_spec(tm, d), pl.BlockSpec((8, LANES), lambda i: (0, 0))],
        out_shape=[jax.ShapeDtypeStruct((s, d), F32), jax.ShapeDtypeStruct((8, LANES), F32)],
    )(xf, tgt)


def _attn_specs(n_q, n_kv):
    aw, kvd = n_q * HEAD_DIM, n_kv * HEAD_DIM
    assert aw % kvd == 0
    kcol = aw // kvd
    q = pl.BlockSpec((WINDOW, aw), lambda n: (n, 0))
    kc = pl.BlockSpec((WINDOW, kvd), lambda n: (n, kcol))
    kp = pl.BlockSpec((WINDOW, kvd), lambda n: (jnp.maximum(n - 1, 0), kcol))
    vc = pl.BlockSpec((WINDOW, kvd), lambda n: (n, kcol + 1))
    vp = pl.BlockSpec((WINDOW, kvd), lambda n: (jnp.maximum(n - 1, 0), kcol + 1))
    return [q, kc, kp, vc, vp]


def _band_mask(n, n_heads):
    qi = lax.broadcasted_iota(jnp.int32, (n_heads * WINDOW, 2 * WINDOW), 0) & (WINDOW - 1)
    kj = lax.broadcasted_iota(jnp.int32, (n_heads * WINDOW, 2 * WINDOW), 1)
    return (kj > qi) & (kj <= qi + WINDOW) & ((kj >= WINDOW) | (n > 0))


def _stack_heads(ref, heads):
    return jnp.concatenate([ref[:, h * HEAD_DIM:(h + 1) * HEAD_DIM] for h in heads], axis=0)


def _stack_sinks(ref, heads):
    return jnp.concatenate([jnp.broadcast_to(ref[:, h:h + 1], (WINDOW, 1)) for h in heads], axis=0)


_NT = (((1,), (1,)), ((), ()))
_TN = (((0,), (0,)), ((), ()))


def _attn_fwd(proj, sinks, *, n_q, n_kv, name):
    s = proj.shape[0]
    aw, grp = n_q * HEAD_DIM, n_q // n_kv

    def body(q_ref, kc_ref, kp_ref, vc_ref, vp_ref, sink_ref, o_ref, lse_ref):
        valid = _band_mask(pl.program_id(0), grp)
        kb = jnp.concatenate([kp_ref[...], kc_ref[...]], axis=0).astype(BF16)
        vb = jnp.concatenate([vp_ref[...], vc_ref[...]], axis=0).astype(BF16)
        lse_ref[...] = jnp.zeros_like(lse_ref)
        for g in range(n_kv):
            heads = range(g * grp, (g + 1) * grp)
            gs = slice(g * HEAD_DIM, (g + 1) * HEAD_DIM)
            qg = _stack_heads(q_ref, heads).astype(BF16)
            sink = _stack_sinks(sink_ref, heads)
            sc = lax.dot_general(qg, kb[:, gs], _NT, preferred_element_type=F32)
            sc = jnp.where(valid, sc * (HEAD_DIM ** -0.5), NEG)
            m = jnp.maximum(jnp.max(sc, axis=-1, keepdims=True), sink)
            e = jnp.exp(sc - m)
            den = jnp.sum(e, axis=-1, keepdims=True) + jnp.exp(sink - m)
            p = e * (1.0 / den)
            og = jnp.dot(p.astype(BF16), vb[:, gs], preferred_element_type=F32)
            lse = m + jnp.log(den)
            for i, h in enumerate(heads):
                rows = slice(i * WINDOW, (i + 1) * WINDOW)
                o_ref[:, h * HEAD_DIM:(h + 1) * HEAD_DIM] = og[rows]
                lse_ref[:, h:h + 1] = lse[rows]

    return pl.pallas_call(
        body, name=name, grid=(s // WINDOW,),
        in_specs=_attn_specs(n_q, n_kv) + [pl.BlockSpec((1, LANES), lambda n: (0, 0))],
        out_specs=[pl.BlockSpec((WINDOW, aw), lambda n: (n, 0)), pl.BlockSpec((WINDOW, LANES), lambda n: (n, 0))],
        out_shape=[jax.ShapeDtypeStruct((s, aw), F32), jax.ShapeDtypeStruct((s, LANES), F32)],
    )(proj, proj, proj, proj, proj, sinks)


def _attn_bwd(proj, sinks, out, lse, dout, *, n_q, n_kv, name):
    s = proj.shape[0]
    aw, kvd, grp = n_q * HEAD_DIM, n_kv * HEAD_DIM, n_q // n_kv
    scale = HEAD_DIM ** -0.5

    def body(q_ref, kc_ref, kp_ref, vc_ref, vp_ref, sink_ref, o_ref, lse_ref, do_ref,
             dq_ref, dk_ref, dv_ref, dsink_ref):
        n = pl.program_id(0)

        @pl.when(n == 0)
        def _():
            dk_ref[...] = jnp.zeros_like(dk_ref)
            dv_ref[...] = jnp.zeros_like(dv_ref)
            dsink_ref[...] = jnp.zeros_like(dsink_ref)

        valid = _band_mask(n, grp)
        kb = jnp.concatenate([kp_ref[...], kc_ref[...]], axis=0).astype(BF16)
        vb = jnp.concatenate([vp_ref[...], vc_ref[...]], axis=0).astype(BF16)
        lane = lax.broadcasted_iota(jnp.int32, (8, LANES), 1)
        dsink = jnp.zeros((8, LANES), F32)
        cur = pl.ds(pl.multiple_of(n * WINDOW, WINDOW), WINDOW)
        prev = pl.ds(pl.multiple_of(jnp.maximum(n - 1, 0) * WINDOW, WINDOW), WINDOW)
        for g in range(n_kv):
            heads = range(g * grp, (g + 1) * grp)
            gs = slice(g * HEAD_DIM, (g + 1) * HEAD_DIM)
            qg = _stack_heads(q_ref, heads).astype(BF16)
            do = _stack_heads(do_ref, heads)
            dob = do.astype(BF16)
            lse = jnp.concatenate([lse_ref[:, h:h + 1] for h in heads], axis=0)
            sc = lax.dot_general(qg, kb[:, gs], _NT, preferred_element_type=F32)
            sc = jnp.where(valid, sc * scale, NEG)
            p = jnp.exp(sc - lse)
            delta = jnp.sum(do * _stack_heads(o_ref, heads), axis=-1, keepdims=True)
            dp = lax.dot_general(dob, vb[:, gs], _NT, preferred_element_type=F32)
            ds = (p * (dp - delta) * scale).astype(BF16)
            dqg = jnp.dot(ds, kb[:, gs], preferred_element_type=F32)
            dkb = lax.dot_general(ds, qg, _TN, preferred_element_type=F32)
            dvb = lax.dot_general(p.astype(BF16), dob, _TN, preferred_element_type=F32)
            sink_term = jnp.exp(_stack_sinks(sink_ref, heads) - lse) * delta
            for i, h in enumerate(heads):
                rows = slice(i * WINDOW, (i + 1) * WINDOW)
                dq_ref[:, h * HEAD_DIM:(h + 1) * HEAD_DIM] = dqg[rows]
                dsink = dsink + jnp.where(lane == h, -jnp.sum(sink_term[rows]), 0.0)
            dk_ref[cur, gs] += dkb[WINDOW:]
            dv_ref[cur, gs] += dvb[WINDOW:]

            @pl.when(n > 0)
            def _():
                dk_ref[prev, gs] += dkb[:WINDOW]
                dv_ref[prev, gs] += dvb[:WINDOW]

        dsink_ref[...] += dsink

    blk = pl.BlockSpec((WINDOW, aw), lambda n: (n, 0))
    kv_full = pl.BlockSpec((s, kvd), lambda n: (0, 0))
    return pl.pallas_call(
        body, name=name, grid=(s // WINDOW,),
        in_specs=_attn_specs(n_q, n_kv) + [pl.BlockSpec((1, LANES), lambda n: (0, 0)), blk,
                                           pl.BlockSpec((WINDOW, LANES), lambda n: (n, 0)), blk],
        out_specs=[blk, kv_full, kv_full, pl.BlockSpec((8, LANES), lambda n: (0, 0))],
        out_shape=[jax.ShapeDtypeStruct((s, aw), F32), jax.ShapeDtypeStruct((s, kvd), F32),
                   jax.ShapeDtypeStruct((s, kvd), F32), jax.ShapeDtypeStruct((8, LANES), F32)],
    )(proj, proj, proj, proj, proj, sinks, out, lse, dout)


def _disc(lr, li, ls):
    dt = jnp.exp(ls)
    mag = jnp.exp(lr * dt)
    ang = li * dt
    ab_re, ab_im = mag * jnp.cos(ang), mag * jnp.sin(ang)
    den = lr * lr + li * li
    f_re = ((ab_re - 1.0) * lr + ab_im * li) / den
    f_im = (ab_im * lr - (ab_re - 1.0) * li) / den
    return ab_re, ab_im, f_re, f_im


POW_ROWS = 8
SUB = 8
TAB_ROWS = POW_ROWS + 2 * SUB


def _ssm_params_fwd(lr, li, ls, b_re, b_im, name):
    gp = lr.shape[1]
    h = b_re.shape[0]

    def body(lr_ref, li_ref, ls_ref, br_ref, bi_ref, bbr_ref, bbi_ref, tr_ref, ti_ref):
        ab_re, ab_im, f_re, f_im = _disc(lr_ref[...], li_ref[...], ls_ref[...])
        br, bi = br_ref[...], bi_ref[...]
        bbr_ref[...] = f_re * br - f_im * bi
        bbi_ref[...] = f_re * bi + f_im * br
        pr, pi = ab_re, ab_im
        for i in range(POW_ROWS):
            tr_ref[i:i + 1, :] = pr
            ti_ref[i:i + 1, :] = pi
            pr, pi = pr * pr - pi * pi, 2.0 * pr * pi
        pr, pi = ab_re, ab_im
        for r in range(SUB):
            for row in (POW_ROWS + r, POW_ROWS + 2 * SUB - 1 - r):
                tr_ref[row:row + 1, :] = pr
                ti_ref[row:row + 1, :] = pi
            pr, pi = pr * ab_re - pi * ab_im, pr * ab_im + pi * ab_re

    mat, tab = jax.ShapeDtypeStruct((h, gp), F32), jax.ShapeDtypeStruct((TAB_ROWS, gp), F32)
    return pl.pallas_call(body, name=name, out_shape=[mat, mat, tab, tab])(lr, li, ls, b_re, b_im)


def _ssm_params_bwd(lr, li, ls, b_re, b_im, dab_re, dab_im, dbb_re, dbb_im, seg, name):
    gp = lr.shape[1]
    h = b_re.shape[0]

    def body(lr_ref, li_ref, ls_ref, br_ref, bi_ref, dar_ref, dai_ref, dbbr_ref, dbbi_ref, seg_ref,
             dlr_ref, dli_ref, dls_ref, dbr_ref, dbi_ref):
        lr_, li_, ls_ = lr_ref[...], li_ref[...], ls_ref[...]
        (ab_re, ab_im, f_re, f_im), vjp = jax.vjp(_disc, lr_, li_, ls_)
        br, bi, dbbr, dbbi = br_ref[...], bi_ref[...], dbbr_ref[...], dbbi_ref[...]
        dbr_ref[...] = dbbr * f_re + dbbi * f_im
        dbi_ref[...] = dbbi * f_re - dbbr * f_im
        df_re = jnp.sum(dbbr * br + dbbi * bi, axis=0, keepdims=True)
        df_im = jnp.sum(dbbi * br - dbbr * bi, axis=0, keepdims=True)
        dlr, dli, dls = vjp((dar_ref[...], dai_ref[...], df_re, df_im))
        dlr_ref[...] = dlr
        dli_ref[...] = dli
        dls8 = jnp.broadcast_to(dls, (8, gp))
        dls_ref[...] = jnp.dot(dls8, seg_ref[...], preferred_element_type=F32, precision=lax.Precision.HIGHEST)

    vec, mat = jax.ShapeDtypeStruct((1, gp), F32), jax.ShapeDtypeStruct((h, gp), F32)
    return pl.pallas_call(body, name=name,
                          out_shape=[vec, vec, jax.ShapeDtypeStruct((8, seg.shape[1]), F32), mat, mat],
                          compiler_params=_params(24 << 20))(
        lr, li, ls, b_re, b_im, dab_re, dab_im, dbb_re, dbb_im, seg)


def _scan_bufs(t_len):
    hs = BLOCK_STATES
    return [pltpu.VMEM((hs // LANES, t_len, LANES), F32), pltpu.VMEM((hs // LANES, t_len, LANES), F32),
            pltpu.VMEM((t_len // SUB, hs), F32), pltpu.VMEM((t_len // SUB, hs), F32)]


def _scan(xr, xi, apow_ref, bufs, t_len, reverse):
    hs = BLOCK_STATES
    n_tiles = t_len // SUB
    sr_ref, si_ref, er_ref, ei_ref = bufs

    def doubling(xr, xi, n_rows, first_pow, within):
        row = lax.broadcasted_iota(jnp.int32, xr.shape, 0) & (within - 1)
        d = 1
        while d < within:
            i = first_pow + d.bit_length() - 1
            pr, pi = apow_ref[i:i + 1, :hs], apow_ref[i:i + 1, hs:]
            if reverse:
                pi, shift, keep = -pi, n_rows - d, row < within - d
            else:
                shift, keep = d, row >= d
            sr = jnp.where(keep, pltpu.roll(xr, shift, 0), 0.0)
            si = jnp.where(keep, pltpu.roll(xi, shift, 0), 0.0)
            xr, xi = xr + pr * sr - pi * si, xi + pr * si + pi * sr
            d *= 2
        return xr, xi

    shape3 = (n_tiles, SUB, hs)
    row = lax.broadcasted_iota(jnp.int32, shape3, 1)
    xr, xi = xr.reshape(shape3), xi.reshape(shape3)
    for i, d in enumerate((1, 2, 4)):
        pr, pi = apow_ref[i:i + 1, :hs], apow_ref[i:i + 1, hs:]
        if reverse:
            pi, shift, keep = -pi, SUB - d, row < SUB - d
        else:
            shift, keep = d, row >= d
        sr = jnp.where(keep, pltpu.roll(xr, shift, 1), 0.0)
        si = jnp.where(keep, pltpu.roll(xi, shift, 1), 0.0)
        xr, xi = xr + pr * sr - pi * si, xi + pr * si + pi * sr
    xr, xi = xr.reshape(t_len, hs), xi.reshape(t_len, hs)
    chunks = [slice(c * LANES, (c + 1) * LANES) for c in range(hs // LANES)]
    for c, lanes in enumerate(chunks):
        sr_ref[c] = xr[:, lanes]
        si_ref[c] = xi[:, lanes]
    edge = pl.ds(0 if reverse else SUB - 1, n_tiles, stride=SUB)
    tr, ti = doubling(jnp.concatenate([sr_ref[c, edge, :] for c in range(len(chunks))], axis=1),
                      jnp.concatenate([si_ref[c, edge, :] for c in range(len(chunks))], axis=1), n_tiles, 3, n_tiles)
    trow = lax.broadcasted_iota(jnp.int32, tr.shape, 0)
    if reverse:
        shift, keep = n_tiles - 1, trow < n_tiles - 1
    else:
        shift, keep = 1, trow >= 1
    er_ref[...] = jnp.where(keep, pltpu.roll(tr, shift, 0), 0.0)
    ei_ref[...] = jnp.where(keep, pltpu.roll(ti, shift, 0), 0.0)
    lin = POW_ROWS + SUB if reverse else POW_ROWS
    mr, mi = apow_ref[lin:lin + SUB, :hs], apow_ref[lin:lin + SUB, hs:]
    if reverse:
        mi = -mi
    for t in range(n_tiles):
        rows = slice(t * SUB, (t + 1) * SUB)
        er, ei = er_ref[t:t + 1, :], ei_ref[t:t + 1, :]
        add_r, add_i = mr * er - mi * ei, mr * ei + mi * er
        for c, lanes in enumerate(chunks):
            sr_ref[c, rows, :] += add_r[:, lanes]
            si_ref[c, rows, :] += add_i[:, lanes]
    return (jnp.concatenate([sr_ref[c] for c in range(len(chunks))], axis=1),
            jnp.concatenate([si_ref[c] for c in range(len(chunks))], axis=1))


def _ssm_chunk(s):
    t_len = _pick(s, 256, 8)
    assert t_len & (t_len - 1) == 0 and t_len <= 1 << POW_ROWS, t_len
    return t_len


def _fold_carry(br, bi, carry_ref, apow_ref, at_row, conj):
    hs = BLOCK_STATES
    cr, ci = carry_ref[0:1, :hs], carry_ref[0:1, hs:]
    ar, ai = apow_ref[0:1, :hs], apow_ref[0:1, hs:]
    if conj:
        ai = -ai
    here = lax.broadcasted_iota(jnp.int32, br.shape, 0) == at_row
    return jnp.where(here, br + (ar * cr - ai * ci), br), jnp.where(here, bi + (ar * ci + ai * cr), bi)


def _ssm_fwd(proj, ucol, bbd, ccat, dskip, apow, t_len, *, name):
    s = proj.shape[0]
    nb = bbd.shape[0]
    nc = s // t_len
    hs = BLOCK_STATES

    def body(u_ref, bbd_ref, ccat_ref, d_ref, apow_ref, y_ref, z_ref, xs_ref, carry_ref, *bufs):
        @pl.when(pl.program_id(1) == 0)
        def _():
            carry_ref[...] = jnp.zeros_like(carry_ref)

        xs_ref[...] = carry_ref[...]
        u = u_ref[...]
        bu = jnp.dot(u.astype(BF16), bbd_ref[...], preferred_element_type=F32)
        br, bi = _fold_carry(bu[:, :hs], bu[:, hs:], carry_ref, apow_ref, 0, False)
        xr, xi = _scan(br, bi, apow_ref, bufs, t_len, False)
        xcat = jnp.concatenate([xr, xi], axis=1)
        carry_ref[...] = jnp.broadcast_to(xcat[t_len - 1:t_len, :], carry_ref.shape)
        y = jnp.dot(xcat.astype(BF16), ccat_ref[...], preferred_element_type=F32) + d_ref[...] * u
        y_ref[...] = y
        z_ref[...] = _gelu(y).astype(z_ref.dtype)

    return pl.pallas_call(
        body, name=name, grid=(nb, nc),
        in_specs=[pl.BlockSpec((t_len, LANES), lambda j, n: (n, ucol + j)),
                  pl.BlockSpec((None, LANES, 2 * hs), lambda j, n: (j, 0, 0)),
                  pl.BlockSpec((None, 2 * hs, LANES), lambda j, n: (j, 0, 0)),
                  pl.BlockSpec((1, LANES), lambda j, n: (0, j)),
                  pl.BlockSpec((None, TAB_ROWS, 2 * hs), lambda j, n: (j, 0, 0))],
        out_specs=[pl.BlockSpec((t_len, LANES), lambda j, n: (n, j)),
                   pl.BlockSpec((t_len, LANES), lambda j, n: (n, j)),
                   pl.BlockSpec((None, None, 8, 2 * hs), lambda j, n: (j, n, 0, 0))],
        out_shape=[jax.ShapeDtypeStruct((s, nb * LANES), F32), jax.ShapeDtypeStruct((s, nb * LANES), BF16),
                   jax.ShapeDtypeStruct((nb, nc, 8, 2 * hs), F32)],
        scratch_shapes=[pltpu.VMEM((8, 2 * hs), F32)] + _scan_bufs(t_len),
        compiler_params=_params(40 << 20),
    )(proj, bbd, ccat, dskip, apow)


def _ssm_bwd(proj, ucol, y, dzd, dz2, xs, bbd, ccat, dskip, apow, t_len, *, name):
    s = proj.shape[0]
    nb = bbd.shape[0]
    nc = s // t_len
    hs = BLOCK_STATES

    def body(u_ref, y_ref, dzd_ref, dz2_ref, xs_ref, bbd_ref, ccat_ref, d_ref, apow_ref,
             du_ref, dbbd_ref, dccat_ref, dd_ref, da_ref, gcarry_ref, *bufs):
        @pl.when(pl.program_id(1) == 0)
        def _():
            gcarry_ref[...] = jnp.zeros_like(gcarry_ref)
            dbbd_ref[...] = jnp.zeros_like(dbbd_ref)
            dccat_ref[...] = jnp.zeros_like(dccat_ref)
            dd_ref[...] = jnp.zeros_like(dd_ref)
            da_ref[...] = jnp.zeros_like(da_ref)

        u = u_ref[...]
        ub = u.astype(BF16)
        dy = (dzd_ref[...] + dz2_ref[...]) * _gelu_grad(y_ref[...])
        dyb = dy.astype(BF16)
        bu = jnp.dot(ub, bbd_ref[...], preferred_element_type=F32)
        br, bi = _fold_carry(bu[:, :hs], bu[:, hs:], xs_ref, apow_ref, 0, False)
        xr, xi = _scan(br, bi, apow_ref, bufs[:4], t_len, False)
        sr, si = xs_ref[0:1, :hs], xs_ref[0:1, hs:]
        dxd = lax.dot_general(dyb, ccat_ref[...], _NT, preferred_element_type=F32)
        dr, di = _fold_carry(dxd[:, :hs], dxd[:, hs:], gcarry_ref, apow_ref, t_len - 1, True)
        gr, gi = _scan(dr, di, apow_ref, bufs[4:], t_len, True)
        gcat = jnp.concatenate([gr, gi], axis=1)
        gcarry_ref[...] = jnp.broadcast_to(gcat[0:1, :], gcarry_ref.shape)
        gb = gcat.astype(BF16)
        du_ref[...] = lax.dot_general(gb, bbd_ref[...], _NT, preferred_element_type=F32) + d_ref[...] * dy
        dbbd_ref[...] += lax.dot_general(ub, gb, _TN, preferred_element_type=F32)
        xb = jnp.concatenate([xr, xi], axis=1).astype(BF16)
        dccat_ref[...] += lax.dot_general(xb, dyb, _TN, preferred_element_type=F32)
        dd_ref[...] += jnp.sum(dy * u, axis=0, keepdims=True)
        first = lax.broadcasted_iota(jnp.int32, xr.shape, 0) == 0
        xpr = jnp.where(first, sr, pltpu.roll(xr, 1, 0))
        xpi = jnp.where(first, si, pltpu.roll(xi, 1, 0))
        dar = jnp.sum(gr * xpr + gi * xpi, axis=0, keepdims=True)
        dai = jnp.sum(gi * xpr - gr * xpi, axis=0, keepdims=True)
        da_ref[...] += jnp.concatenate([dar, dai], axis=1)

    def rows(j, n):
        return nc - 1 - n

    chunk = pl.BlockSpec((t_len, LANES), lambda j, n: (rows(j, n), j))
    return pl.pallas_call(
        body, name=name, grid=(nb, nc),
        in_specs=[pl.BlockSpec((t_len, LANES), lambda j, n: (rows(j, n), ucol + j)), chunk, chunk, chunk,
                  pl.BlockSpec((None, None, 8, 2 * hs), lambda j, n: (j, rows(j, n), 0, 0)),
                  pl.BlockSpec((None, LANES, 2 * hs), lambda j, n: (j, 0, 0)),
                  pl.BlockSpec((None, 2 * hs, LANES), lambda j, n: (j, 0, 0)),
                  pl.BlockSpec((1, LANES), lambda j, n: (0, j)),
                  pl.BlockSpec((None, TAB_ROWS, 2 * hs), lambda j, n: (j, 0, 0))],
        out_specs=[chunk,
                   pl.BlockSpec((None, LANES, 2 * hs), lambda j, n: (j, 0, 0)),
                   pl.BlockSpec((None, 2 * hs, LANES), lambda j, n: (j, 0, 0)),
                   pl.BlockSpec((1, LANES), lambda j, n: (0, j)),
                   pl.BlockSpec((None, 1, 2 * hs), lambda j, n: (j, 0, 0))],
        out_shape=[jax.ShapeDtypeStruct((s, nb * LANES), F32),
                   jax.ShapeDtypeStruct((nb, LANES, 2 * hs), F32),
                   jax.ShapeDtypeStruct((nb, 2 * hs, LANES), F32),
                   jax.ShapeDtypeStruct((1, nb * LANES), F32),
                   jax.ShapeDtypeStruct((nb, 1, 2 * hs), F32)],
        scratch_shapes=[pltpu.VMEM((8, 2 * hs), F32)] + _scan_bufs(t_len) + _scan_bufs(t_len),
        compiler_params=_params(48 << 20),
    )(proj, y, dzd, dz2, xs, bbd, ccat, dskip, apow)


def _to_blocks(a):
    g, p, k = a.shape
    nb = g // GROUPS_PER_BLOCK
    eye = jnp.eye(GROUPS_PER_BLOCK, dtype=a.dtype)
    a4 = a.reshape(nb, GROUPS_PER_BLOCK, p, k)
    out = jnp.einsum("ab,jbpk->jakbp", eye, a4)
    return out.reshape(nb, GROUPS_PER_BLOCK * k, GROUPS_PER_BLOCK * p)


def _from_blocks(d, p, k):
    nb = d.shape[0]
    d5 = d.reshape(nb, GROUPS_PER_BLOCK, k, GROUPS_PER_BLOCK, p)
    eye = jnp.eye(GROUPS_PER_BLOCK, dtype=bool)[None, :, None, :, None]
    diag = jnp.sum(jnp.where(eye, d5, 0.0), axis=1)
    return jnp.transpose(diag, (0, 2, 3, 1)).reshape(nb * GROUPS_PER_BLOCK, p, k)


def _merge_fwd(attn, y, gl, g_a, g_s, name):
    s, wa = attn.shape
    ws = y.shape[1]
    tm = _pick(s, 256, 16)

    def body(a_ref, y_ref, gl_ref, ga_ref, gs_ref, o_ref):
        av = a_ref[...]
        o_ref[:, :wa] = ((av * _rstd(av)) * ga_ref[...]).astype(o_ref.dtype)
        sv = _gelu(y_ref[...]) * jax.nn.sigmoid(gl_ref[...])
        o_ref[:, wa:] = ((sv * _rstd(sv)) * gs_ref[...]).astype(o_ref.dtype)

    return pl.pallas_call(
        body, name=name, grid=(s // tm,),
        in_specs=[_row_spec(tm, wa), _row_spec(tm, ws), _row_spec(tm, ws), _vec_spec(wa), _vec_spec(ws)],
        out_specs=_row_spec(tm, wa + ws), out_shape=jax.ShapeDtypeStruct((s, wa + ws), BF16),
    )(attn, y, gl, g_a, g_s)


def _merge_bwd(dmerged, attn, y, gl, g_a, g_s, name):
    s, wa = attn.shape
    ws = y.shape[1]
    tm = _pick(s, 256, 16)

    def body(dm_ref, a_ref, y_ref, gl_ref, ga_ref, gs_ref, da_ref, dgl_ref, dzd_ref, dga_ref, dgs_ref):
        @pl.when(pl.program_id(0) == 0)
        def _():
            dga_ref[...] = jnp.zeros_like(dga_ref)
            dgs_ref[...] = jnp.zeros_like(dgs_ref)

        dan, dsn = dm_ref[:, :wa], dm_ref[:, wa:]
        av = a_ref[...]
        ra = _rstd(av)
        ahat = av * ra
        dga_ref[...] += jnp.sum(dan * ahat, axis=0, keepdims=True)
        da_ref[...] = _norm_bwd(dan * ga_ref[...], ahat, ra)
        z = _gelu(y_ref[...])
        sig = jax.nn.sigmoid(gl_ref[...])
        sv = z * sig
        rs = _rstd(sv)
        shat = sv * rs
        dgs_ref[...] += jnp.sum(dsn * shat, axis=0, keepdims=True)
        dssm = _norm_bwd(dsn * gs_ref[...], shat, rs)
        dzd_ref[...] = dssm * sig
        dgl_ref[...] = (dssm * z * sig * (1.0 - sig)).astype(dgl_ref.dtype)

    return pl.pallas_call(
        body, name=name, grid=(s // tm,),
        in_specs=[_row_spec(tm, wa + ws), _row_spec(tm, wa), _row_spec(tm, ws), _row_spec(tm, ws),
                  _vec_spec(wa), _vec_spec(ws)],
        out_specs=[_row_spec(tm, wa), _row_spec(tm, ws), _row_spec(tm, ws), _vec_spec(wa), _vec_spec(ws)],
        out_shape=[jax.ShapeDtypeStruct((s, wa), F32), jax.ShapeDtypeStruct((s, ws), BF16),
                   jax.ShapeDtypeStruct((s, ws), F32), jax.ShapeDtypeStruct((1, wa), F32),
                   jax.ShapeDtypeStruct((1, ws), F32)],
    )(dmerged, attn, y, gl, g_a, g_s)


def _shift_down(main, halo, k):
    rolled = pltpu.roll(main, k, 0)
    row = lax.broadcasted_iota(jnp.int32, main.shape, 0)
    for r in range(k):
        rolled = jnp.where(row == r, halo[8 - k + r:8 - k + r + 1, :], rolled)
    return rolled


def _shift_up(main, halo, k):
    tm = main.shape[0]
    rolled = pltpu.roll(main, tm - k, 0)
    row = lax.broadcasted_iota(jnp.int32, main.shape, 0)
    for r in range(k):
        rolled = jnp.where(row == tm - k + r, halo[r:r + 1, :], rolled)
    return rolled


def _conv(main, halo, w_ref, b_ref):
    return (b_ref[...] + w_ref[0:1, :] * _shift_down(main, halo, 2) + w_ref[1:2, :] * _shift_down(main, halo, 1)
            + w_ref[2:3, :] * main)


def _gate_tiles(s, f):
    return _pick(s, 512, 16), _pick(f, 512, LANES)


def _gate_in_specs(tm, tn, nfb, order):
    hb = tm // 8
    ij = (lambda a, b: (b, a)) if order == "ji" else (lambda a, b: (a, b))

    def main(off):
        return pl.BlockSpec((tm, tn), lambda a, b: (ij(a, b)[0], ij(a, b)[1] + off))

    def halo(off):
        return pl.BlockSpec((8, tn), lambda a, b: (jnp.maximum(ij(a, b)[0] * hb - 1, 0), ij(a, b)[1] + off))

    def vec(rows, off):
        return pl.BlockSpec((rows, tn), lambda a, b: (0, ij(a, b)[1] + off))

    return [main(0), main(nfb), halo(0), halo(nfb), vec(3, 0), vec(3, nfb), vec(1, 0), vec(1, nfb)]


def _gate_fwd(up0, conv_w, conv_b, name):
    s, f2 = up0.shape
    f = f2 // 2
    tm, tn = _gate_tiles(s, f)
    nfb = f // tn

    def body(v_ref, g_ref, vh_ref, gh_ref, wv_ref, wg_ref, bv_ref, bg_ref, o_ref):
        top = pl.program_id(0) == 0
        vh = jnp.where(top, 0.0, vh_ref[...])
        gh = jnp.where(top, 0.0, gh_ref[...])
        val = _conv(v_ref[...], vh, wv_ref, bv_ref)
        gate = _conv(g_ref[...], gh, wg_ref, bg_ref)
        o_ref[...] = (_gelu(gate) * val).astype(o_ref.dtype)

    return pl.pallas_call(
        body, name=name, grid=(s // tm, nfb),
        in_specs=_gate_in_specs(tm, tn, nfb, "ij"), out_specs=pl.BlockSpec((tm, tn), lambda i, j: (i, j)),
        out_shape=jax.ShapeDtypeStruct((s, f), BF16),
        compiler_params=_params(24 * tm * tn * 4 + (4 << 20)),
    )(up0, up0, up0, up0, conv_w, conv_w, conv_b, conv_b)


def _gate_bwd(up0, conv_w, conv_b, da, name):
    s, f2 = up0.shape
    f = f2 // 2
    tm, tn = _gate_tiles(s, f)
    nfb = f // tn

    def body(v_ref, g_ref, vh_ref, gh_ref, wv_ref, wg_ref, bv_ref, bg_ref, da_ref, dup_ref, dcb_ref, dcw_ref):
        top = pl.program_id(1) == 0

        @pl.when(top)
        def _():
            dcb_ref[...] = jnp.zeros_like(dcb_ref)
            dcw_ref[...] = jnp.zeros_like(dcw_ref)

        halos = (jnp.where(top, 0.0, vh_ref[...]), jnp.where(top, 0.0, gh_ref[...]))
        mains = (v_ref[...], g_ref[...])
        val = _conv(mains[0], halos[0], wv_ref, bv_ref)
        gate = _conv(mains[1], halos[1], wg_ref, bg_ref)
        dav = da_ref[...]
        act, act_grad = _gelu_and_grad(gate)
        dups = (dav * act, (dav * val) * act_grad)
        for half in range(2):
            dup = dups[half]
            dup_ref[half] = dup
            dcb_ref[half] += jnp.sum(dup, axis=0, keepdims=True)
            dcw_ref[half, 0:1, :] += jnp.sum(dup * _shift_down(mains[half], halos[half], 2), axis=0, keepdims=True)
            dcw_ref[half, 1:2, :] += jnp.sum(dup * _shift_down(mains[half], halos[half], 1), axis=0, keepdims=True)
            dcw_ref[half, 2:3, :] += jnp.sum(dup * mains[half], axis=0, keepdims=True)

    return pl.pallas_call(
        body, name=name, grid=(nfb, s // tm),
        in_specs=_gate_in_specs(tm, tn, nfb, "ji") + [pl.BlockSpec((tm, tn), lambda j, i: (i, j))],
        out_specs=[pl.BlockSpec((2, tm, tn), lambda j, i: (0, i, j)),
                   pl.BlockSpec((2, 1, tn), lambda j, i: (0, 0, j)),
                   pl.BlockSpec((2, 3, tn), lambda j, i: (0, 0, j))],
        out_shape=[jax.ShapeDtypeStruct((2, s, f), F32), jax.ShapeDtypeStruct((2, 1, f), F32),
                   jax.ShapeDtypeStruct((2, 3, f), F32)],
        compiler_params=_params(40 * tm * tn * 4 + (4 << 20)),
    )(up0, up0, up0, up0, conv_w, conv_w, conv_b, conv_b, da)


def _conv_bwd(dup, conv_w, name):
    _, s, f = dup.shape
    tm, tn = _pick(s, 512, 16), _pick(f, 1536, LANES)
    nfb, ni, hb = f // tn, s // tm, tm // 8

    def body(d_ref, dh_ref, w_ref, o_ref):
        main = d_ref[...]
        halo = jnp.where(pl.program_id(1) == ni - 1, 0.0, dh_ref[...])
        o_ref[...] = (w_ref[2:3, :] * main + w_ref[1:2, :] * _shift_up(main, halo, 1)
                      + w_ref[0:1, :] * _shift_up(main, halo, 2)).astype(o_ref.dtype)

    return pl.pallas_call(
        body, name=name, grid=(2, ni, nfb),
        in_specs=[pl.BlockSpec((None, tm, tn), lambda h, i, j: (h, i, j)),
                  pl.BlockSpec((None, 8, tn), lambda h, i, j: (h, jnp.minimum((i + 1) * hb, s // 8 - 1), j)),
                  pl.BlockSpec((3, tn), lambda h, i, j: (0, h * nfb + j))],
        out_specs=pl.BlockSpec((tm, tn), lambda h, i, j: (i, h * nfb + j)),
        out_shape=jax.ShapeDtypeStruct((s, 2 * f), BF16),
        compiler_params=_params(12 * tm * tn * 4 + (4 << 20)),
    )(dup, dup, conv_w)


def _adamw(w, g, m, v, name):
    r, c = w.shape
    tr = _pick(r, max(8, (1 << 19) // max(c, 1) // 8 * 8), 8)
    c1, c2 = 1.0 / (1.0 - ADAM_B1 ** ADAM_STEP), 1.0 / (1.0 - ADAM_B2 ** ADAM_STEP)

    def body(w_ref, g_ref, m_ref, v_ref, d_ref, nm_ref, nv_ref):
        gv = g_ref[...]
        nm = ADAM_B1 * m_ref[...] + (1.0 - ADAM_B1) * gv
        nv = ADAM_B2 * v_ref[...] + (1.0 - ADAM_B2) * (gv * gv)
        nm_ref[...] = nm
        nv_ref[...] = nv
        d_ref[...] = -ADAM_LR * ((nm * c1) / (jnp.sqrt(nv * c2) + ADAM_EPS) + ADAM_WD * w_ref[...])

    spec = pl.BlockSpec((tr, c), lambda i: (i, 0))
    out = jax.ShapeDtypeStruct((r, c), F32)
    return pl.pallas_call(body, name=name, grid=(r // tr,), in_specs=[spec] * 4, out_specs=[spec] * 3,
                          out_shape=[out] * 3, compiler_params=_params(14 * tr * c * 4 + (4 << 20)))(w, g, m, v)


def _adamw_many(ws, gs, ms, vs, name):
    n = len(ws)
    c1, c2 = 1.0 / (1.0 - ADAM_B1 ** ADAM_STEP), 1.0 / (1.0 - ADAM_B2 ** ADAM_STEP)

    def body(*refs):
        w_refs, g_refs, m_refs, v_refs = (refs[i * n:(i + 1) * n] for i in range(4))
        d_refs, nm_refs, nv_refs = (refs[(4 + i) * n:(5 + i) * n] for i in range(3))
        for i in range(n):
            gv = g_refs[i][...]
            nm = ADAM_B1 * m_refs[i][...] + (1.0 - ADAM_B1) * gv
            nv = ADAM_B2 * v_refs[i][...] + (1.0 - ADAM_B2) * (gv * gv)
            nm_refs[i][...] = nm
            nv_refs[i][...] = nv
            d_refs[i][...] = -ADAM_LR * ((nm * c1) / (jnp.sqrt(nv * c2) + ADAM_EPS) + ADAM_WD * w_refs[i][...])

    shapes = [jax.ShapeDtypeStruct(w.shape, F32) for w in ws]
    outs = pl.pallas_call(body, name=name, out_shape=shapes * 3, compiler_params=_params(48 << 20))(
        *ws, *gs, *ms, *vs)
    return outs[:n], outs[n:2 * n], outs[2 * n:]


def _adamw_nd(w, g, m, v, name):
    shape = w.shape
    c = shape[-1]
    outs = _adamw(w.reshape(-1, c), g.reshape(-1, c), m.reshape(-1, c), v.reshape(-1, c), name)
    return [o.reshape(shape) for o in outs]


BIG = ("w_in", "w_glu", "w_out", "w_up", "w_down")
SMALL = ("b_ada", "g_pre_mix", "g_post_mix", "attn_sinks", "lam_re", "lam_im", "log_step", "ssm_b_re", "ssm_b_im",
         "ssm_c_re", "ssm_c_im", "ssm_d", "g_attn_out", "g_ssm_out", "g_pre_ffn", "g_post_ffn", "conv_b")
ORDER = ("w_ada", "b_ada", "g_pre_mix", "g_post_mix", "w_in", "attn_sinks", "lam_re", "lam_im", "log_step",
         "ssm_b_re", "ssm_b_im", "ssm_c_re", "ssm_c_im", "ssm_d", "w_glu", "g_attn_out", "g_ssm_out", "w_out",
         "g_pre_ffn", "g_post_ffn", "w_up", "conv_w", "conv_b", "w_down")
COL_SHARDED = ("w_in", "w_up")


def kernel(x, c, w_ada, b_ada, g_pre_mix, g_post_mix, w_in, attn_sinks, lam_re, lam_im, log_step, ssm_b_re, ssm_b_im, ssm_c_re, ssm_c_im, ssm_d, w_glu, g_attn_out, g_ssm_out, w_out, g_pre_ffn, g_post_ffn, w_up, conv_w, conv_b, w_down, loss_target, m_w_ada, m_b_ada, m_g_pre_mix, m_g_post_mix, m_w_in, m_attn_sinks, m_lam_re, m_lam_im, m_log_step, m_ssm_b_re, m_ssm_b_im, m_ssm_c_re, m_ssm_c_im, m_ssm_d, m_w_glu, m_g_attn_out, m_g_ssm_out, m_w_out, m_g_pre_ffn, m_g_post_ffn, m_w_up, m_conv_w, m_conv_b, m_w_down, v_w_ada, v_b_ada, v_g_pre_mix, v_g_post_mix, v_w_in, v_attn_sinks, v_lam_re, v_lam_im, v_log_step, v_ssm_b_re, v_ssm_b_im, v_ssm_c_re, v_ssm_c_im, v_ssm_d, v_w_glu, v_g_attn_out, v_g_ssm_out, v_w_out, v_g_pre_ffn, v_g_post_ffn, v_w_up, v_conv_w, v_conv_b, v_w_down):
    env = dict(locals())
    W = {n: env[n] for n in ORDER}
    M = {n: env["m_" + n] for n in ORDER}
    V = {n: env["v_" + n] for n in ORDER}

    depth = w_ada.shape[0]
    s, d = x.shape[1], x.shape[2]
    xs0 = x.reshape(s, d)
    tgt = loss_target.reshape(s, d)
    attn_w = d // 2
    ssm_w = d - attn_w
    in_cols = w_in.shape[2] * N_DEV
    kv_dim = (in_cols - attn_w - ssm_w) // 2
    n_q, n_kv = attn_w // HEAD_DIM, kv_dim // HEAD_DIM
    n_grp = ssm_w // SSM_GROUP
    nb = ssm_w // LANES
    f = w_down.shape[1] * N_DEV
    ucol = (attn_w + 2 * kv_dim) // LANES
    t_len = _ssm_chunk(s)
    me = 4 * lax.axis_index("x") + 2 * lax.axis_index("y") + lax.axis_index("c")

    def at_block(ref, idx):
        return ref.at[idx]

    def at_rows(n_rows):
        return lambda ref, idx: ref.at[:, pl.ds(pl.multiple_of(idx * n_rows, 8), n_rows), :]

    def at_cols(n_cols):
        return lambda ref, idx: ref.at[:, :, pl.ds(pl.multiple_of(idx * n_cols, LANES), n_cols)]

    first = _gather_multi([w_in.astype(BF16), conv_w, c],
                          [(N_DEV,) + w_in.shape, (N_DEV,) + conv_w.shape, (N_DEV,) + c.shape],
                          [at_block, at_block, at_block], "ag_first")
    w_in_full = _cols_from_blocks(first[0], "w_in_layout")
    conv_w_full = jnp.transpose(first[1], (1, 2, 0, 3)).reshape(depth, 3, 2 * f)
    c_all = first[2].reshape(N_DEV, d)

    def at_rows2(n_rows):
        return lambda ref, idx: ref.at[pl.ds(pl.multiple_of(idx * n_rows, 8), n_rows), :]

    def at_cols2(n_cols):
        return lambda ref, idx: ref.at[:, pl.ds(pl.multiple_of(idx * n_cols, LANES), n_cols)]

    def whole(ref, idx):
        return ref

    def gather_kind(n):
        return "blk" if n == "w_in" else "cols" if n in COL_SHARDED else "rows"

    def gather_view(n):
        return {"blk": at_block, "cols": at_cols2(W[n].shape[2]), "rows": at_rows2(W[n].shape[1])}[gather_kind(n)]

    def gather_shape(n):
        _, a, b = W[n].shape
        return {"blk": (N_DEV, a, b), "cols": (a, N_DEV * b), "rows": (N_DEV * a, b)}[gather_kind(n)]

    later = [(n, l) for l in range(depth) for n in BIG[1:]]
    later_srcs = [W[n][l].astype(BF16) for n, l in later]
    later_views = [gather_view(n) for n, _ in later]
    me_arr = me.astype(jnp.int32).reshape(1)
    lands = [_place_own(me_arr, src, lax.empty(gather_shape(n), BF16), gather_kind(n), f"ag_own_{n}{l}")
             for (n, l), src in zip(later, later_srcs)]
    ag_started, ag_token = _exchange_start(later_srcs, lands, [whole] * len(later), later_views, "ag_start")

    def weights_arrived(names, l, after, name):
        picks = [later.index((n, l)) for n in names]
        _, got = _exchange_wait([ag_started[i] for i in picks], [after], [whole] * len(picks),
                                [later_views[i] for i in picks], name)
        return dict(zip(names, got))

    c_pad = jnp.pad(c_all, ((0, 16 - N_DEV), (0, 0)))
    n_ada = w_ada.shape[2]
    b_shard = lax.dynamic_slice_in_dim(b_ada, me * n_ada, n_ada, axis=1).reshape(depth, 1, n_ada)
    ada_part, c_act = _ada_fwd(c_pad, w_ada, b_shard, "ada_fwd")
    ada_all = _all_gather(ada_part.reshape(depth * 16, n_ada), "ag_ada").reshape(N_DEV, depth, 16, n_ada)
    ada_me = lax.dynamic_index_in_dim(ada_all, me, axis=2, keepdims=False)
    ada = jnp.transpose(ada_me, (1, 0, 2)).reshape(depth, 6, 1, d) + ag_token[0, 0]

    gp = n_grp * STATE

    def hgp(a):
        return jnp.transpose(a, (2, 0, 1)).reshape(SSM_GROUP, gp)

    ssm = []
    for l in range(depth):
        lr, li = lam_re[l].reshape(1, gp), lam_im[l].reshape(1, gp)
        ls = jnp.repeat(log_step[l], STATE).reshape(1, gp)
        br, bi = hgp(ssm_b_re[l]), hgp(ssm_b_im[l])
        bbr, bbi, tab_r, tab_i = _ssm_params_fwd(lr, li, ls, br, bi, f"ssm_params_fwd{l}")
        bb_re = jnp.transpose(bbr.reshape(SSM_GROUP, n_grp, STATE), (1, 2, 0))
        bb_im = jnp.transpose(bbi.reshape(SSM_GROUP, n_grp, STATE), (1, 2, 0))
        bbd = jnp.concatenate([_to_blocks(bb_re), _to_blocks(bb_im)], axis=2).astype(BF16)
        c_re_t = jnp.transpose(ssm_c_re[l], (0, 2, 1))
        c_im_t = jnp.transpose(ssm_c_im[l], (0, 2, 1))
        ccat = jnp.concatenate([jnp.transpose(_to_blocks(c_re_t), (0, 2, 1)),
                                -jnp.transpose(_to_blocks(c_im_t), (0, 2, 1))], axis=1).astype(BF16)

        def tab(t):
            return t.reshape(TAB_ROWS, nb, BLOCK_STATES)

        apow = jnp.transpose(jnp.concatenate([tab(tab_r), tab(tab_i)], axis=2), (1, 0, 2))
        ssm.append(dict(lr=lr, li=li, ls=ls, br=br, bi=bi, bbd=bbd, ccat=ccat, apow=apow,
                        dskip=ssm_d[l].reshape(1, ssm_w)))

    sinks_pad = jnp.pad(attn_sinks, ((0, 0), (0, LANES - n_q)))

    def vec(a):
        return a.reshape(1, -1)

    saved = []
    fw = [dict() for _ in range(depth)]
    xin = xs0
    for l in range(depth):
        sh_m, sc_m, gt_m, sh_f, sc_f, gt_f = (ada[l, i] for i in range(6))
        p = ssm[l]
        h1 = _modnorm_fwd(xin, vec(g_pre_mix[l]), sc_m, sh_m, f"modnorm_mix_fwd{l}")
        proj = _matmul(h1, w_in_full[l], name=f"mm_in{l}")
        attn, lse = _attn_fwd(proj, sinks_pad[l:l + 1], n_q=n_q, n_kv=n_kv, name=f"attn_fwd{l}")
        y, z, xstart = _ssm_fwd(proj, ucol, p["bbd"], p["ccat"], p["dskip"], p["apow"], t_len, name=f"ssm_fwd{l}")
        fw[l].update(weights_arrived(("w_glu", "w_out"), l, z, f"ag_wait_mix{l}"))
        gl = _matmul(z, fw[l]["w_glu"], name=f"mm_glu{l}")
        merged = _merge_fwd(attn, y, gl, vec(g_attn_out[l]), vec(g_ssm_out[l]), f"merge_fwd{l}")
        mix = _matmul(merged, fw[l]["w_out"], name=f"mm_out{l}")
        x2 = _resnorm_fwd(xin, mix, vec(g_post_mix[l]), gt_m, f"resnorm_mix_fwd{l}")
        h2 = _modnorm_fwd(x2, vec(g_pre_ffn[l]), sc_f, sh_f, f"modnorm_ffn_fwd{l}")
        fw[l].update(weights_arrived(("w_up",), l, h2, f"ag_wait_up{l}"))
        up0 = _matmul(h2, fw[l]["w_up"], name=f"mm_up{l}")
        cw, cb = conv_w_full[l], vec(conv_b[l])
        act = _gate_fwd(up0, cw, cb, f"gate_fwd{l}")
        fw[l].update(weights_arrived(("w_down",), l, act, f"ag_wait_down{l}"))
        ff = _matmul(act, fw[l]["w_down"], name=f"mm_down{l}")
        x3 = _resnorm_fwd(x2, ff, vec(g_post_ffn[l]), gt_f, f"resnorm_ffn_fwd{l}")
        saved.append(dict(xin=xin, h1=h1, proj=proj, attn=attn, lse=lse, y=y, z=z, xstart=xstart, gl=gl,
                          merged=merged, mix=mix, x2=x2, h2=h2, up0=up0, act=act, ff=ff))
        xin = x3

    dxo, loss_acc = _loss_bwd(xin, tgt, "loss")
    loss = lax.psum(loss_acc[0, 0], ("x", "y", "c"))

    grads = {n: [None] * depth for n in ORDER}
    dada = [None] * depth
    big_blocks = {n: [None] * depth for n in BIG}
    seg = jnp.pad(jnp.repeat(jnp.eye(n_grp, dtype=F32), STATE, axis=0), ((0, 0), (0, (-n_grp) % LANES)))

    def part_view(n):
        shp = W[n].shape
        if n == "w_in":
            return at_block, "blk"
        if n in COL_SHARDED:
            return at_cols2(shp[2]), "cols"
        return at_rows2(shp[1]), "rows"

    rs_groups, start_tokens = [], []
    small_order = SMALL + ("conv_w",)
    small_shapes = {n: W[n].shape for n in SMALL}
    small_shapes["conv_w"] = (depth, 3, 2 * f)
    small_started = [None] * depth

    def send_partials(items, name):
        parts = [big_blocks[n][l] for n, l in items]
        lands = [lax.empty((N_DEV,) + W[n].shape[1:], BF16) for n, _ in items]
        started, token = _exchange_start(parts, lands, [part_view(n)[0] for n, _ in items],
                                         [at_block] * len(items), name)
        rs_groups.append((items, started, name))
        start_tokens.append(token)
        return token[0, 0]

    order = jnp.zeros((), F32)
    for l in reversed(range(depth)):
        sh_m, sc_m, gt_m, sh_f, sc_f, gt_f = (ada[l, i] for i in range(6))
        gt_f = gt_f + order
        a, p = saved[l], ssm[l]
        cw, cb = conv_w_full[l], vec(conv_b[l])
        dff, dg, dgt_f = _resnorm_bwd(dxo, a["ff"], vec(g_post_ffn[l]), gt_f, f"resnorm_ffn_bwd{l}")
        grads["g_post_ffn"][l] = dg
        dact = _matmul(dff, fw[l]["w_down"], tb=True, name=f"mm_down_dx{l}")
        big_blocks["w_down"][l] = _matmul(a["act"], dff, ta=True, out_dtype=BF16, name=f"mm_down_dw{l}")
        dup, dcb, dcw = _gate_bwd(a["up0"], cw, cb, dact, f"gate_bwd{l}")
        grads["conv_b"][l] = dcb.reshape(1, 2 * f)
        grads["conv_w"][l] = jnp.transpose(dcw, (1, 0, 2)).reshape(3, 2 * f)
        dup0 = _conv_bwd(dup, cw, f"conv_bwd{l}")
        dh2 = _matmul(dup0, fw[l]["w_up"], tb=True, name=f"mm_up_dx{l}")
        big_blocks["w_up"][l] = _matmul(a["h2"], dup0, ta=True, out_dtype=BF16, name=f"mm_up_dw{l}")
        if l == 0:
            sc_f = sc_f + send_partials([("w_down", 0), ("w_up", 0)], "rs_start_ffn0")
        dx2, dg, dsc_f, dsh_f = _modnorm_bwd(dh2, a["x2"], vec(g_pre_ffn[l]), sc_f, dxo, f"modnorm_ffn_bwd{l}")
        grads["g_pre_ffn"][l] = dg
        dmix, dg, dgt_m = _resnorm_bwd(dx2, a["mix"], vec(g_post_mix[l]), gt_m, f"resnorm_mix_bwd{l}")
        grads["g_post_mix"][l] = dg
        dmerged = _matmul(dmix, fw[l]["w_out"], tb=True, name=f"mm_out_dx{l}")
        big_blocks["w_out"][l] = _matmul(a["merged"], dmix, ta=True, out_dtype=BF16, name=f"mm_out_dw{l}")
        dattn, dgl, dzd, dga, dgs = _merge_bwd(dmerged, a["attn"], a["y"], a["gl"], vec(g_attn_out[l]),
                                               vec(g_ssm_out[l]), f"merge_bwd{l}")
        grads["g_attn_out"][l], grads["g_ssm_out"][l] = dga, dgs
        dz2 = _matmul(dgl, fw[l]["w_glu"], tb=True, name=f"mm_glu_dx{l}")
        big_blocks["w_glu"][l] = _matmul(a["z"], dgl, ta=True, out_dtype=BF16, name=f"mm_glu_dw{l}")
        dskip = p["dskip"]
        if l == 0:
            dskip = dskip + send_partials([("w_out", 0), ("w_glu", 0)], "rs_start_mix0")
        du, dbbd, dccat, dd, da = _ssm_bwd(a["proj"], ucol, a["y"], dzd, dz2, a["xstart"], p["bbd"], p["ccat"],
                                           dskip, p["apow"], t_len, name=f"ssm_bwd{l}")
        grads["ssm_d"][l] = dd
        hs = BLOCK_STATES
        dbb_re = _from_blocks(dbbd[:, :, :hs], STATE, SSM_GROUP)
        dbb_im = _from_blocks(dbbd[:, :, hs:], STATE, SSM_GROUP)
        dccat_t = jnp.transpose(dccat, (0, 2, 1))
        grads["ssm_c_re"][l] = jnp.transpose(_from_blocks(dccat_t[:, :, :hs], STATE, SSM_GROUP), (0, 2, 1))
        grads["ssm_c_im"][l] = -jnp.transpose(_from_blocks(dccat_t[:, :, hs:], STATE, SSM_GROUP), (0, 2, 1))
        dab_re, dab_im = da[:, 0, :hs].reshape(1, gp), da[:, 0, hs:].reshape(1, gp)
        dlr, dli, dls, dbr, dbi = _ssm_params_bwd(p["lr"], p["li"], p["ls"], p["br"], p["bi"], dab_re, dab_im,
                                                  hgp(dbb_re), hgp(dbb_im), seg, f"ssm_params_bwd{l}")
        grads["lam_re"][l], grads["lam_im"][l], grads["log_step"][l] = dlr, dli, dls[0, :n_grp]
        grads["ssm_b_re"][l] = jnp.transpose(dbr.reshape(SSM_GROUP, n_grp, STATE), (1, 2, 0))
        grads["ssm_b_im"][l] = jnp.transpose(dbi.reshape(SSM_GROUP, n_grp, STATE), (1, 2, 0))
        dq, dk, dv, dsink = _attn_bwd(a["proj"], sinks_pad[l:l + 1], a["attn"], a["lse"], dattn,
                                      n_q=n_q, n_kv=n_kv, name=f"attn_bwd{l}")
        grads["attn_sinks"][l] = dsink[0, :n_q]
        dproj = jnp.concatenate([dq, dk, dv, du], axis=1).astype(BF16)
        dh1 = _matmul(dproj, w_in_full[l], tb=True, name=f"mm_in_dx{l}")
        big_blocks["w_in"][l] = _blocks_from_cols(_matmul(a["h1"], dproj, ta=True, name=f"mm_in_dw{l}"),
                                                  f"w_in_grad_layout{l}")
        dxo, dg, dsc_m, dsh_m = _modnorm_bwd(dh1, a["xin"], vec(g_pre_mix[l]), sc_m, dx2, f"modnorm_mix_bwd{l}")
        grads["g_pre_mix"][l] = dg
        dada[l] = jnp.concatenate([dsh_m, dsc_m, dgt_m, dsh_f, dsc_f, dgt_f], axis=1)
        if l > 0:
            order = send_partials([(n, l) for n in reversed(BIG)], f"rs_start_layer{l}")
        else:
            order = order + send_partials([("w_in", 0)], "rs_start_in0")
        spack = _pack([dada[l]] + [grads[n][l] for n in small_order[1:]], F32, 1024)
        started, token = _exchange_start([spack], [lax.empty((N_DEV,) + spack.shape, F32)], [whole], [at_block],
                                         f"small_start{l}")
        small_started[l] = started
        start_tokens.append(token)
        order = order + token[0, 0]
    grad_x = dxo.reshape(x.shape)

    delta, new_m, new_v = {}, {}, {}
    stacked = {n: None for n in BIG}
    landed_layers = {n: 0 for n in BIG}
    after = [dxo] + start_tokens
    for items, started, name in rs_groups:
        mine, landed = _exchange_wait(started, after, [part_view(n)[0] for n, _ in items], [at_block] * len(items),
                                      name.replace("start", "wait"))
        for (n, l), part, slots in zip(items, mine, landed):
            stacked[n] = _sum_slots_own(me_arr, slots, part, part_view(n)[1], f"rs_sum_{n}{l}", layer=l,
                                        n_layers=depth, stacked=stacked[n])
            landed_layers[n] += 1
            if landed_layers[n] == depth:
                grads[n] = stacked[n]
                delta[n], new_m[n], new_v[n] = _adamw_nd(W[n], grads[n], M[n], V[n], f"adamw_{n}")
                after.append(delta[n])

    n_cw = conv_w.shape[2]
    small_sums, dada_rows = [None] * depth, [None] * depth
    for l in reversed(range(depth)):
        mine, landed = _exchange_wait(small_started[l], after, [whole], [at_block], f"small_wait{l}")
        ssum = _sum_slots_own(me_arr, landed[0], mine[0], "self", f"sum_small{l}").reshape(-1)
        small_sums[l] = _unpack(ssum, [small_shapes[n][1:] for n in small_order])
        slot = lax.broadcasted_iota(jnp.int32, (N_DEV, 6 * d), 0)
        dada_rows[l] = jnp.where(slot == me, mine[0].reshape(-1)[:6 * d][None],
                                 landed[0].reshape(N_DEV, -1)[:, :6 * d])
    for i, n in enumerate(small_order):
        grads[n] = jnp.stack([small_sums[l][i] for l in range(depth)])
    grads["conv_w"] = lax.dynamic_slice_in_dim(grads["conv_w"], me * n_cw, n_cw, axis=2)
    dada_all = jnp.stack(dada_rows, axis=1)
    dada_shard = lax.dynamic_slice_in_dim(dada_all, me * n_ada, n_ada, axis=2)
    kp = LANES
    dada_pad = jnp.pad(jnp.transpose(dada_shard, (1, 0, 2)), ((0, 0), (0, kp - N_DEV), (0, 0)))
    act_t = jnp.pad(jnp.transpose(c_act[:N_DEV]), ((0, 0), (0, kp - N_DEV)))
    grads["w_ada"] = _ada_wgrad(act_t, dada_pad, "ada_wgrad")

    delta["w_ada"], new_m["w_ada"], new_v["w_ada"] = _adamw_nd(W["w_ada"], grads["w_ada"], M["w_ada"], V["w_ada"],
                                                                "adamw_w_ada")

    def lane_friendly(a):
        return a.reshape(-1, 1024) if a.ndim > 2 and a.shape[-1] < LANES and a.size % 1024 == 0 else a

    rest = SMALL + ("conv_w",)
    outs = _adamw_many(*[[lane_friendly(t[n]) for n in rest] for t in (W, grads, M, V)], "adamw_small")
    for tgt_d, vals in zip((delta, new_m, new_v), outs):
        for n, val in zip(rest, vals):
            tgt_d[n] = val.reshape(W[n].shape)

    return (loss, grad_x, *[grads[n] for n in ORDER], *[delta[n] for n in ORDER],
            *[new_m[n] for n in ORDER], *[new_v[n] for n in ORDER])
```

```python
import functools
import math

import jax
import jax.numpy as jnp
from jax import lax
from jax.experimental import pallas as pl
from jax.experimental.pallas import tpu as pltpu

F32 = jnp.float32
BF16 = jnp.bfloat16

N_DEV = 8
HEAD_DIM = 64
WINDOW = 128
SSM_GROUP = 16
STATE = 64
LANES = 128
GROUPS_PER_BLOCK = LANES // SSM_GROUP
BLOCK_STATES = GROUPS_PER_BLOCK * STATE
EPS = 1e-6
NEG = -1e30
ADAM_LR, ADAM_B1, ADAM_B2, ADAM_EPS, ADAM_WD, ADAM_STEP = 0.001, 0.9, 0.999, 1e-08, 0.01, 10
VMEM_BYTES_V7X = 64 * 1024 * 1024
GELU_C = math.sqrt(2.0 / math.pi)
MESH = pl.DeviceIdType.MESH
ANY = pl.BlockSpec(memory_space=pl.ANY)


def _pick(n, pref, align):
    t = (min(pref, n) // align) * align
    while t >= align:
        if n % t == 0:
            return t
        t -= align
    return n


def _params(vmem_bytes=None):
    if vmem_bytes is None:
        return pltpu.CompilerParams()
    return pltpu.CompilerParams(vmem_limit_bytes=int(min(vmem_bytes, VMEM_BYTES_V7X - (8 << 20))))


def _gelu_and_grad(x):
    x2 = x * x
    half_x = 0.5 * x
    th = jnp.tanh((GELU_C * x) * (1.0 + 0.044715 * x2))
    one_th = 1.0 + th
    grad = 0.5 * one_th + (half_x * (1.0 - th * th)) * (GELU_C + (3.0 * 0.044715 * GELU_C) * x2)
    return half_x * one_th, grad


def _gelu(x):
    return _gelu_and_grad(x)[0]


def _gelu_grad(x):
    return _gelu_and_grad(x)[1]


def _rstd(x):
    return lax.rsqrt(jnp.mean(x * x, axis=-1, keepdims=True) + EPS)


def _norm_bwd(dhat, xhat, r):
    return r * (dhat - xhat * jnp.mean(dhat * xhat, axis=-1, keepdims=True))


def _matmul(a, b, *, ta=False, tb=False, a_halves=False, b_halves=False, out_dtype=F32, name):
    assert not (a_halves and ta) and not (b_halves and tb)
    if a_halves:
        m, kdim = a.shape[1], 2 * a.shape[2]
    else:
        (kdim, m) = a.shape if ta else a.shape[::-1]
    if b_halves:
        k2, n = b.shape[1], 2 * b.shape[2]
    else:
        (n, k2) = b.shape if tb else b.shape[::-1]
    assert kdim == k2, (a.shape, b.shape, ta, tb)
    tm = _pick(m, 1024, LANES)
    tn = _pick(n // 2, 1536, LANES) if b_halves else _pick(n, 1024, LANES)
    tk = _pick(kdim // 2, 2816, LANES) if a_halves else _pick(kdim, 2816, LANES)
    nk = kdim // tk
    dn = (((0 if ta else 1,), (1 if tb else 0,)), ((), ()))

    def partial_product(a_ref, b_ref):
        return lax.dot_general(a_ref[...].astype(BF16), b_ref[...].astype(BF16), dn, preferred_element_type=F32)

    def body_one(a_ref, b_ref, o_ref):
        o_ref[...] = partial_product(a_ref, b_ref).astype(o_ref.dtype)

    def body_acc(a_ref, b_ref, o_ref, acc_ref):
        k = pl.program_id(2)

        @pl.when(k == 0)
        def _():
            acc_ref[...] = partial_product(a_ref, b_ref)

        @pl.when((k > 0) & (k < nk - 1))
        def _():
            acc_ref[...] += partial_product(a_ref, b_ref)

        @pl.when(k == nk - 1)
        def _():
            o_ref[...] = (acc_ref[...] + partial_product(a_ref, b_ref)).astype(o_ref.dtype)

    body = body_one if nk == 1 else body_acc
    a_spec = pl.BlockSpec((tk, tm), lambda i, j, k: (k, i)) if ta else pl.BlockSpec((tm, tk), lambda i, j, k: (i, k))
    b_spec = pl.BlockSpec((tn, tk), lambda i, j, k: (j, k)) if tb else pl.BlockSpec((tk, tn), lambda i, j, k: (k, j))
    if a_halves:
        nkh = nk // 2
        a_spec = pl.BlockSpec((None, tm, tk), lambda i, j, k: (k // nkh, i, k % nkh))
    if b_halves:
        njh = n // tn // 2
        b_spec = pl.BlockSpec((None, tk, tn), lambda i, j, k: (j // njh, k, j % njh))
    vmem = (2 * (tm * tk * a.dtype.itemsize + tk * tn * b.dtype.itemsize) + tm * tn * 4
            + 2 * tm * tn * jnp.dtype(out_dtype).itemsize + 3 * tm * tn * 4 + (4 << 20))
    return pl.pallas_call(
        body, name=name, grid=(m // tm, n // tn, nk),
        in_specs=[a_spec, b_spec], out_specs=pl.BlockSpec((tm, tn), lambda i, j, k: (i, j)),
        out_shape=jax.ShapeDtypeStruct((m, n), out_dtype),
        scratch_shapes=[] if nk == 1 else [pltpu.VMEM((tm, tn), F32)],
        compiler_params=_params(vmem),
    )(a, b)


def _all_gather(x, name):
    def body(x_ref, out_ref, send_sems, recv_sems, local_sem):
        x_, y_, c_ = lax.axis_index("x"), lax.axis_index("y"), lax.axis_index("c")
        me, sibling = (x_, y_, c_), (x_, y_, 1 - c_)
        chips = [(1 - x_, y_), (x_, 1 - y_), (1 - x_, 1 - y_)]

        def slot(px, py, pc):
            return out_ref.at[4 * px + 2 * py + pc]

        def copy(k, block, to, src=None):
            return pltpu.make_async_remote_copy(
                src_ref=slot(*block) if src is None else src, dst_ref=slot(*block),
                send_sem=send_sems.at[k], recv_sem=recv_sems.at[k], device_id=to, device_id_type=MESH)

        mine = pltpu.make_async_copy(x_ref, slot(*me), local_sem)
        mine.start()
        first = [copy(0, me, sibling, src=x_ref)]
        first += [copy(1 + j, me, (*chip, c_), src=x_ref) for j, chip in enumerate(chips)]
        for cp in first:
            cp.start()
        passed = [copy(4 + j, (*chip, c_), sibling) for j, chip in enumerate(chips)]
        for j, chip in enumerate(chips):
            copy(1 + j, (*chip, c_), me).wait_recv()
            passed[j].start()
        copy(0, sibling, me).wait_recv()
        for j, chip in enumerate(chips):
            copy(4 + j, (*chip, 1 - c_), me).wait_recv()
        for cp in first + passed:
            cp.wait_send()
        mine.wait()

    return pl.pallas_call(
        body, name=name, out_shape=jax.ShapeDtypeStruct((N_DEV,) + x.shape, x.dtype),
        in_specs=[ANY], out_specs=ANY,
        scratch_shapes=[pltpu.SemaphoreType.DMA((7,)), pltpu.SemaphoreType.DMA((7,)), pltpu.SemaphoreType.DMA],
    )(x)


def _gather_multi(srcs, out_shapes, views, name):
    n = len(srcs)

    def body(*refs):
        src_refs, out_refs = refs[:n], refs[n:2 * n]
        send_sems, recv_sems, local_sems = refs[2 * n:]
        x_, y_, c_ = lax.axis_index("x"), lax.axis_index("y"), lax.axis_index("c")
        me, sibling = (x_, y_, c_), (x_, y_, 1 - c_)
        chips = [(1 - x_, y_), (x_, 1 - y_), (1 - x_, 1 - y_)]

        def slot(i, px, py, pc):
            return views[i](out_refs[i], 4 * px + 2 * py + pc)

        def copy(i, k, block, to, from_src=False):
            return pltpu.make_async_remote_copy(
                src_ref=src_refs[i] if from_src else slot(i, *block), dst_ref=slot(i, *block),
                send_sem=send_sems.at[7 * i + k], recv_sem=recv_sems.at[7 * i + k], device_id=to, device_id_type=MESH)

        mine = [pltpu.make_async_copy(src_refs[i], slot(i, *me), local_sems.at[i]) for i in range(n)]
        for cp in mine:
            cp.start()
        first = []
        for i in range(n):
            first.append(copy(i, 0, me, sibling, True))
            first += [copy(i, 1 + j, me, (*chip, c_), True) for j, chip in enumerate(chips)]
        for cp in first:
            cp.start()
        passed = []
        for j, chip in enumerate(chips):
            for i in range(n):
                copy(i, 1 + j, (*chip, c_), me).wait_recv()
                fwd = copy(i, 4 + j, (*chip, c_), sibling)
                fwd.start()
                passed.append(fwd)
        for i in range(n):
            copy(i, 0, sibling, me).wait_recv()
            for j, chip in enumerate(chips):
                copy(i, 4 + j, (*chip, 1 - c_), me).wait_recv()
        for cp in first + passed:
            cp.wait_send()
        for cp in mine:
            cp.wait()

    return pl.pallas_call(
        body, name=name, out_shape=[jax.ShapeDtypeStruct(s, a.dtype) for s, a in zip(out_shapes, srcs)],
        in_specs=[ANY] * n, out_specs=[ANY] * n,
        scratch_shapes=[pltpu.SemaphoreType.DMA((7 * n,)), pltpu.SemaphoreType.DMA((7 * n,)),
                        pltpu.SemaphoreType.DMA((n,))],
    )(*srcs)


HBM_SPEC = pl.BlockSpec(memory_space=pltpu.HBM)
SEM_SPEC = pl.BlockSpec(memory_space=pltpu.SEMAPHORE)
SIDE_EFFECT = pltpu.SideEffectType.DATAFLOW_SIDE_EFFECTING
N_PEERS = N_DEV - 1


def _peer(k, x_, y_, c_):
    px = 1 - x_ if (k >> 2) & 1 else x_
    py = 1 - y_ if (k >> 1) & 1 else y_
    pc = 1 - c_ if k & 1 else c_
    return (px, py, pc), 4 * px + 2 * py + pc


def _exchange_copies(src_refs, land_refs, send_sems, recv_sems, src_views, dst_views):
    x_, y_, c_ = lax.axis_index("x"), lax.axis_index("y"), lax.axis_index("c")
    me = 4 * x_ + 2 * y_ + c_
    out = []
    for i in range(len(src_refs)):
        for k in range(1, N_DEV):
            peer, idx = _peer(k, x_, y_, c_)

            def copy(dst_slot, i=i, k=k, peer=peer, idx=idx):
                return pltpu.make_async_remote_copy(
                    src_ref=src_views[i](src_refs[i], idx), dst_ref=dst_views[i](land_refs[i], dst_slot),
                    send_sem=send_sems[i].at[k - 1], recv_sem=recv_sems[i].at[k - 1], device_id=peer,
                    device_id_type=MESH)

            out.append((copy(me), copy(idx)))
    return out


def _exchange_start(srcs, lands, src_views, dst_views, name):
    n = len(srcs)

    def body(*refs):
        src_refs, land_refs = refs[:n], refs[n:2 * n]
        send_sems, recv_sems = refs[2 * n:3 * n], refs[3 * n:4 * n]
        token = refs[-1]
        for send, _ in _exchange_copies(src_refs, land_refs, send_sems, recv_sems, src_views, dst_views):
            send.start()
        token[...] = jnp.zeros_like(token)

    sems = [pltpu.SemaphoreType.DMA((N_PEERS,))] * n
    thru = [pltpu.HBM(a.shape, a.dtype) for a in list(srcs) + list(lands)]
    outs = pl.pallas_call(
        body, name=name, out_shape=sems + sems + thru + [jax.ShapeDtypeStruct((8, LANES), F32)],
        in_specs=[HBM_SPEC] * (2 * n),
        out_specs=[SEM_SPEC] * (2 * n) + [HBM_SPEC] * (2 * n) + [pl.BlockSpec(memory_space=pltpu.VMEM)],
        input_output_aliases={j: 2 * n + j for j in range(2 * n)},
        compiler_params=pltpu.CompilerParams(has_side_effects=SIDE_EFFECT),
    )(*[pltpu.with_memory_space_constraint(a, pltpu.HBM) for a in list(srcs) + list(lands)])
    per_array = [(outs[j], outs[n + j], outs[2 * n + j], outs[3 * n + j]) for j in range(n)]
    return per_array, outs[-1]


def _exchange_wait(started, after, src_views, dst_views, name):
    send_sems, recv_sems, srcs, lands = (list(t) for t in zip(*started))
    n = len(srcs)
    after = list(after)

    def body(*refs):
        src_refs, land_refs = refs[:n], refs[n:2 * n]
        send_refs, recv_refs = refs[2 * n:3 * n], refs[3 * n:4 * n]
        copies = _exchange_copies(src_refs, land_refs, send_refs, recv_refs, src_views, dst_views)
        for send, _ in copies:
            send.wait_send()
        for _, recv in copies:
            recv.wait_recv()

    thru = [pltpu.HBM(a.shape, a.dtype) for a in list(srcs) + list(lands)]
    outs = pl.pallas_call(
        body, name=name, out_shape=thru,
        in_specs=[HBM_SPEC] * (2 * n) + [SEM_SPEC] * (2 * n) + [ANY] * len(after),
        out_specs=[HBM_SPEC] * (2 * n),
        input_output_aliases={j: j for j in range(2 * n)},
        compiler_params=pltpu.CompilerParams(has_side_effects=SIDE_EFFECT),
    )(*srcs, *lands, *send_sems, *recv_sems, *after)
    return outs[:n], outs[n:]


def _place_own(me, src, land, kind, name):
    r, c = src.shape
    tr = _pick(r, 512, 16)
    nt = r // tr
    if kind == "rows":
        out_spec = pl.BlockSpec((tr, c), lambda i, mr: (mr[0] * nt + i, 0))
    elif kind == "cols":
        out_spec = pl.BlockSpec((tr, c), lambda i, mr: (i, mr[0]))
    else:
        out_spec = pl.BlockSpec((None, tr, c), lambda i, mr: (mr[0], i, 0))

    def body(me_ref, s_ref, land_ref, o_ref):
        o_ref[...] = s_ref[...]

    return pl.pallas_call(
        body, name=name,
        grid_spec=pltpu.PrefetchScalarGridSpec(
            num_scalar_prefetch=1, grid=(nt,),
            in_specs=[pl.BlockSpec((tr, c), lambda i, mr: (i, 0)), ANY], out_specs=out_spec),
        out_shape=jax.ShapeDtypeStruct(land.shape, land.dtype),
        input_output_aliases={2: 0},
    )(me, src, land)


def _sum_slots_own(me, landed, part, kind, name, *, layer=0, n_layers=1, stacked=None):
    _, r, c = landed.shape
    tr = _pick(r, 512, 16)
    nt = r // tr
    if kind == "rows":
        part_spec = pl.BlockSpec((tr, c), lambda i, mr: (mr[0] * nt + i, 0))
    elif kind == "cols":
        part_spec = pl.BlockSpec((tr, c), lambda i, mr: (i, mr[0]))
    elif kind == "blk":
        part_spec = pl.BlockSpec((None, tr, c), lambda i, mr: (mr[0], i, 0))
    else:
        part_spec = pl.BlockSpec((tr, c), lambda i, mr: (i, 0))

    def body(me_ref, x_ref, p_ref, *rest):
        o_ref = rest[-1]
        own = p_ref[...].astype(F32)
        acc = jnp.where(me_ref[0] == 0, own, x_ref[0].astype(F32))
        for i in range(1, N_DEV):
            acc = acc + jnp.where(me_ref[0] == i, own, x_ref[i].astype(F32))
        o_ref[...] = acc

    operands = [me, landed, part] + ([] if stacked is None else [stacked])
    return pl.pallas_call(
        body, name=name,
        grid_spec=pltpu.PrefetchScalarGridSpec(
            num_scalar_prefetch=1, grid=(nt,),
            in_specs=[pl.BlockSpec((N_DEV, tr, c), lambda i, mr: (0, i, 0)), part_spec]
            + ([] if stacked is None else [ANY]),
            out_specs=pl.BlockSpec((None, tr, c), lambda i, mr: (layer, i, 0))),
        out_shape=jax.ShapeDtypeStruct((n_layers, r, c), F32),
        input_output_aliases={} if stacked is None else {3: 0},
        compiler_params=_params(2 * N_DEV * tr * c * landed.dtype.itemsize + 8 * tr * c * 4 + (4 << 20)),
    )(*operands)


def _cols_from_blocks(blk, name):
    nd, nl, k, n = blk.shape
    tk = _pick(k, 256, 16)

    def body(b_ref, o_ref, wide_ref):
        for dev in range(nd):
            wide_ref[:, dev * n:(dev + 1) * n] = b_ref[dev].astype(F32)
        o_ref[...] = wide_ref[...].astype(o_ref.dtype)

    return pl.pallas_call(
        body, name=name, grid=(nl, k // tk),
        in_specs=[pl.BlockSpec((nd, None, tk, n), lambda l, i: (0, l, i, 0))],
        out_specs=pl.BlockSpec((None, tk, nd * n), lambda l, i: (l, i, 0)),
        out_shape=jax.ShapeDtypeStruct((nl, k, nd * n), BF16),
        scratch_shapes=[pltpu.VMEM((tk, nd * n), F32)],
    )(blk)


def _blocks_from_cols(full, name):
    k, n8 = full.shape
    n = n8 // N_DEV
    tk = _pick(k, 256, 16)

    def body(f_ref, o_ref):
        for dev in range(N_DEV):
            o_ref[dev] = f_ref[:, dev * n:(dev + 1) * n].astype(o_ref.dtype)

    return pl.pallas_call(
        body, name=name, grid=(k // tk,),
        in_specs=[pl.BlockSpec((tk, n8), lambda i: (i, 0))],
        out_specs=pl.BlockSpec((N_DEV, tk, n), lambda i: (0, i, 0)),
        out_shape=jax.ShapeDtypeStruct((N_DEV, k, n), BF16),
    )(full)


def _pack(arrs, dtype, cols):
    flat = jnp.concatenate([a.astype(dtype).reshape(-1) for a in arrs])
    unit = 16 * cols
    pad = (-flat.shape[0]) % unit
    flat = jnp.pad(flat, (0, pad))
    return flat.reshape(-1, cols)


def _unpack(flat, shapes):
    out, off = [], 0
    for s in shapes:
        n = math.prod(s)
        out.append(flat[off:off + n].reshape(s))
        off += n
    return out


def _ada_fwd(c_all, w_ada, b_shard, name):
    nl, d, n = w_ada.shape
    tn = _pick(n, 512, LANES)

    def body(c_ref, w_ref, b_ref, o_ref, act_ref):
        cv = c_ref[...]
        act = cv * jax.nn.sigmoid(cv)
        act_ref[...] = act
        o_ref[...] = jnp.dot(act.astype(BF16), w_ref[...].astype(BF16), preferred_element_type=F32) + b_ref[...]

    return pl.pallas_call(
        body, name=name, grid=(nl, n // tn),
        in_specs=[pl.BlockSpec(c_all.shape, lambda l, j: (0, 0)),
                  pl.BlockSpec((None, d, tn), lambda l, j: (l, 0, j)),
                  pl.BlockSpec((None, 1, tn), lambda l, j: (l, 0, j))],
        out_specs=[pl.BlockSpec((None, c_all.shape[0], tn), lambda l, j: (l, 0, j)),
                   pl.BlockSpec(c_all.shape, lambda l, j: (0, 0))],
        out_shape=[jax.ShapeDtypeStruct((nl, c_all.shape[0], n), F32), jax.ShapeDtypeStruct(c_all.shape, F32)],
        compiler_params=_params(2 * d * tn * 4 + d * tn * 2 + (8 << 20)),
    )(c_all, w_ada, b_shard)


def _ada_wgrad(act_t, dada, name):
    d, kp = act_t.shape
    nl, _, n = dada.shape
    tm = _pick(d, 512, 8)

    def body(a_ref, g_ref, o_ref):
        o_ref[...] = jnp.dot(a_ref[...].astype(BF16), g_ref[...].astype(BF16), preferred_element_type=F32)

    return pl.pallas_call(
        body, name=name, grid=(nl, d // tm),
        in_specs=[pl.BlockSpec((tm, kp), lambda l, i: (i, 0)), pl.BlockSpec((None, kp, n), lambda l, i: (l, 0, 0))],
        out_specs=pl.BlockSpec((None, tm, n), lambda l, i: (l, i, 0)),
        out_shape=jax.ShapeDtypeStruct((nl, d, n), F32),
        compiler_params=_params(4 * tm * n * 4 + 2 * kp * n * 4 + (8 << 20)),
    )(act_t, dada)


def _row_spec(tm, d):
    return pl.BlockSpec((tm, d), lambda i: (i, 0))


def _vec_spec(d):
    return pl.BlockSpec((1, d), lambda i: (0, 0))


def _modnorm_fwd(x, g, sc, sh, name):
    s, d = x.shape
    tm = _pick(s, 256, 16)

    def body(x_ref, g_ref, sc_ref, sh_ref, o_ref):
        xv = x_ref[...]
        o_ref[...] = ((xv * _rstd(xv)) * g_ref[...] * (1.0 + sc_ref[...]) + sh_ref[...]).astype(o_ref.dtype)

    return pl.pallas_call(
        body, name=name, grid=(s // tm,),
        in_specs=[_row_spec(tm, d), _vec_spec(d), _vec_spec(d), _vec_spec(d)], out_specs=_row_spec(tm, d),
        out_shape=jax.ShapeDtypeStruct((s, d), BF16),
    )(x, g, sc, sh)


def _modnorm_bwd(dh, x, g, sc, dres, name):
    s, d = x.shape
    tm = _pick(s, 256, 8)

    def body(dh_ref, x_ref, g_ref, sc_ref, dres_ref, dx_ref, dg_ref, dsc_ref, dsh_ref):
        @pl.when(pl.program_id(0) == 0)
        def _():
            dg_ref[...] = jnp.zeros_like(dg_ref)
            dsc_ref[...] = jnp.zeros_like(dsc_ref)
            dsh_ref[...] = jnp.zeros_like(dsh_ref)

        dh_, xv, gv = dh_ref[...], x_ref[...], g_ref[...]
        r = _rstd(xv)
        xhat = xv * r
        dn = dh_ * (1.0 + sc_ref[...])
        dsh_ref[...] += jnp.sum(dh_, axis=0, keepdims=True)
        dsc_ref[...] += jnp.sum(dh_ * (xhat * gv), axis=0, keepdims=True)
        dg_ref[...] += jnp.sum(dn * xhat, axis=0, keepdims=True)
        dx_ref[...] = _norm_bwd(dn * gv, xhat, r) + dres_ref[...]

    vec = jax.ShapeDtypeStruct((1, d), F32)
    return pl.pallas_call(
        body, name=name, grid=(s // tm,),
        in_specs=[_row_spec(tm, d), _row_spec(tm, d), _vec_spec(d), _vec_spec(d), _row_spec(tm, d)],
        out_specs=[_row_spec(tm, d), _vec_spec(d), _vec_spec(d), _vec_spec(d)],
        out_shape=[jax.ShapeDtypeStruct((s, d), F32), vec, vec, vec],
    )(dh, x, g, sc, dres)


def _resnorm_fwd(x, y, g, gt, name):
    s, d = x.shape
    tm = _pick(s, 256, 8)

    def body(x_ref, y_ref, g_ref, gt_ref, o_ref):
        yv = y_ref[...]
        o_ref[...] = x_ref[...] + (1.0 + gt_ref[...]) * ((yv * _rstd(yv)) * g_ref[...])

    return pl.pallas_call(
        body, name=name, grid=(s // tm,),
        in_specs=[_row_spec(tm, d), _row_spec(tm, d), _vec_spec(d), _vec_spec(d)], out_specs=_row_spec(tm, d),
        out_shape=jax.ShapeDtypeStruct((s, d), F32),
    )(x, y, g, gt)


def _resnorm_bwd(dxo, y, g, gt, name):
    s, d = y.shape
    tm = _pick(s, 256, 16)

    def body(dxo_ref, y_ref, g_ref, gt_ref, dy_ref, dg_ref, dgt_ref):
        @pl.when(pl.program_id(0) == 0)
        def _():
            dg_ref[...] = jnp.zeros_like(dg_ref)
            dgt_ref[...] = jnp.zeros_like(dgt_ref)

        dxo_, yv, gv = dxo_ref[...], y_ref[...], g_ref[...]
        r = _rstd(yv)
        yhat = yv * r
        dn = dxo_ * (1.0 + gt_ref[...])
        dgt_ref[...] += jnp.sum(dxo_ * (yhat * gv), axis=0, keepdims=True)
        dg_ref[...] += jnp.sum(dn * yhat, axis=0, keepdims=True)
        dy_ref[...] = _norm_bwd(dn * gv, yhat, r).astype(dy_ref.dtype)

    vec = jax.ShapeDtypeStruct((1, d), F32)
    return pl.pallas_call(
        body, name=name, grid=(s // tm,),
        in_specs=[_row_spec(tm, d), _row_spec(tm, d), _vec_spec(d), _vec_spec(d)],
        out_specs=[_row_spec(tm, d), _vec_spec(d), _vec_spec(d)],
        out_shape=[jax.ShapeDtypeStruct((s, d), BF16), vec, vec],
    )(dxo, y, g, gt)


def _loss_bwd(xf, tgt, name):
    s, d = xf.shape
    tm = _pick(s, 256, 8)

    def body(x_ref, t_ref, dy_ref, l_ref):
        @pl.when(pl.program_id(0) == 0)
        def _():
            l_ref[...] = jnp.zeros_like(l_ref)

        e = x_ref[...] - t_ref[...]
        dy_ref[...] = e * (1.0 / d)
        l_ref[...] += jnp.sum(e * e) * (0.5 / d)

    return pl.pallas_call(
        body, name=name, grid=(s // tm,),
        in_specs=[_row_spec(tm, d), _row_spec(tm, d)],
        out_specs=[_row_spec(tm, d), pl.BlockSpec((8, LANES), lambda i: (0, 0))],
        out_shape=[jax.ShapeDtypeStruct((s, d), F32), jax.ShapeDtypeStruct((8, LANES), F32)],
    )(xf, tgt)


def _attn_specs(n_q, n_kv):
    aw, kvd = n_q * HEAD_DIM, n_kv * HEAD_DIM
    assert aw % kvd == 0
    kcol = aw // kvd
    q = pl.BlockSpec((WINDOW, aw), lambda n: (n, 0))
    kc = pl.BlockSpec((WINDOW, kvd), lambda n: (n, kcol))
    kp = pl.BlockSpec((WINDOW, kvd), lambda n: (jnp.maximum(n - 1, 0), kcol))
    vc = pl.BlockSpec((WINDOW, kvd), lambda n: (n, kcol + 1))
    vp = pl.BlockSpec((WINDOW, kvd), lambda n: (jnp.maximum(n - 1, 0), kcol + 1))
    return [q, kc, kp, vc, vp]


def _band_mask(n, n_heads):
    qi = lax.broadcasted_iota(jnp.int32, (n_heads * WINDOW, 2 * WINDOW), 0) & (WINDOW - 1)
    kj = lax.broadcasted_iota(jnp.int32, (n_heads * WINDOW, 2 * WINDOW), 1)
    return (kj > qi) & (kj <= qi + WINDOW) & ((kj >= WINDOW) | (n > 0))


def _stack_heads(ref, heads):
    return jnp.concatenate([ref[:, h * HEAD_DIM:(h + 1) * HEAD_DIM] for h in heads], axis=0)


def _stack_sinks(ref, heads):
    return jnp.concatenate([jnp.broadcast_to(ref[:, h:h + 1], (WINDOW, 1)) for h in heads], axis=0)


_NT = (((1,), (1,)), ((), ()))
_TN = (((0,), (0,)), ((), ()))


def _attn_fwd(proj, sinks, *, n_q, n_kv, name):
    s = proj.shape[0]
    aw, grp = n_q * HEAD_DIM, n_q // n_kv

    def body(q_ref, kc_ref, kp_ref, vc_ref, vp_ref, sink_ref, o_ref, lse_ref):
        valid = _band_mask(pl.program_id(0), grp)
        kb = jnp.concatenate([kp_ref[...], kc_ref[...]], axis=0).astype(BF16)
        vb = jnp.concatenate([vp_ref[...], vc_ref[...]], axis=0).astype(BF16)
        lse_ref[...] = jnp.zeros_like(lse_ref)
        for g in range(n_kv):
            heads = range(g * grp, (g + 1) * grp)
            gs = slice(g * HEAD_DIM, (g + 1) * HEAD_DIM)
            qg = _stack_heads(q_ref, heads).astype(BF16)
            sink = _stack_sinks(sink_ref, heads)
            sc = lax.dot_general(qg, kb[:, gs], _NT, preferred_element_type=F32)
            sc = jnp.where(valid, sc * (HEAD_DIM ** -0.5), NEG)
            m = jnp.maximum(jnp.max(sc, axis=-1, keepdims=True), sink)
            e = jnp.exp(sc - m)
            den = jnp.sum(e, axis=-1, keepdims=True) + jnp.exp(sink - m)
            p = e * (1.0 / den)
            og = jnp.dot(p.astype(BF16), vb[:, gs], preferred_element_type=F32)
            lse = m + jnp.log(den)
            for i, h in enumerate(heads):
                rows = slice(i * WINDOW, (i + 1) * WINDOW)
                o_ref[:, h * HEAD_DIM:(h + 1) * HEAD_DIM] = og[rows]
                lse_ref[:, h:h + 1] = lse[rows]

    return pl.pallas_call(
        body, name=name, grid=(s // WINDOW,),
        in_specs=_attn_specs(n_q, n_kv) + [pl.BlockSpec((1, LANES), lambda n: (0, 0))],
        out_specs=[pl.BlockSpec((WINDOW, aw), lambda n: (n, 0)), pl.BlockSpec((WINDOW, LANES), lambda n: (n, 0))],
        out_shape=[jax.ShapeDtypeStruct((s, aw), F32), jax.ShapeDtypeStruct((s, LANES), F32)],
    )(proj, proj, proj, proj, proj, sinks)


def _attn_bwd(proj, sinks, out, lse, dout, *, n_q, n_kv, name):
    s = proj.shape[0]
    aw, kvd, grp = n_q * HEAD_DIM, n_kv * HEAD_DIM, n_q // n_kv
    scale = HEAD_DIM ** -0.5

    def body(q_ref, kc_ref, kp_ref, vc_ref, vp_ref, sink_ref, o_ref, lse_ref, do_ref,
             dq_ref, dk_ref, dv_ref, dsink_ref):
        n = pl.program_id(0)

        @pl.when(n == 0)
        def _():
            dk_ref[...] = jnp.zeros_like(dk_ref)
            dv_ref[...] = jnp.zeros_like(dv_ref)
            dsink_ref[...] = jnp.zeros_like(dsink_ref)

        valid = _band_mask(n, grp)
        kb = jnp.concatenate([kp_ref[...], kc_ref[...]], axis=0).astype(BF16)
        vb = jnp.concatenate([vp_ref[...], vc_ref[...]], axis=0).astype(BF16)
        lane = lax.broadcasted_iota(jnp.int32, (8, LANES), 1)
        dsink = jnp.zeros((8, LANES), F32)
        cur = pl.ds(pl.multiple_of(n * WINDOW, WINDOW), WINDOW)
        prev = pl.ds(pl.multiple_of(jnp.maximum(n - 1, 0) * WINDOW, WINDOW), WINDOW)
        for g in range(n_kv):
            heads = range(g * grp, (g + 1) * grp)
            gs = slice(g * HEAD_DIM, (g + 1) * HEAD_DIM)
            qg = _stack_heads(q_ref, heads).astype(BF16)
            do = _stack_heads(do_ref, heads)
            dob = do.astype(BF16)
            lse = jnp.concatenate([lse_ref[:, h:h + 1] for h in heads], axis=0)
            sc = lax.dot_general(qg, kb[:, gs], _NT, preferred_element_type=F32)
            sc = jnp.where(valid, sc * scale, NEG)
            p = jnp.exp(sc - lse)
            delta = jnp.sum(do * _stack_heads(o_ref, heads), axis=-1, keepdims=True)
            dp = lax.dot_general(dob, vb[:, gs], _NT, preferred_element_type=F32)
            ds = (p * (dp - delta) * scale).astype(BF16)
            dqg = jnp.dot(ds, kb[:, gs], preferred_element_type=F32)
            dkb = lax.dot_general(ds, qg, _TN, preferred_element_type=F32)
            dvb = lax.dot_general(p.astype(BF16), dob, _TN, preferred_element_type=F32)
            sink_term = jnp.exp(_stack_sinks(sink_ref, heads) - lse) * delta
            for i, h in enumerate(heads):
                rows = slice(i * WINDOW, (i + 1) * WINDOW)
                dq_ref[:, h * HEAD_DIM:(h + 1) * HEAD_DIM] = dqg[rows]
                dsink = dsink + jnp.where(lane == h, -jnp.sum(sink_term[rows]), 0.0)
            dk_ref[cur, gs] += dkb[WINDOW:]
            dv_ref[cur, gs] += dvb[WINDOW:]

            @pl.when(n > 0)
            def _():
                dk_ref[prev, gs] += dkb[:WINDOW]
                dv_ref[prev, gs] += dvb[:WINDOW]

        dsink_ref[...] += dsink

    blk = pl.BlockSpec((WINDOW, aw), lambda n: (n, 0))
    kv_full = pl.BlockSpec((s, kvd), lambda n: (0, 0))
    return pl.pallas_call(
        body, name=name, grid=(s // WINDOW,),
        in_specs=_attn_specs(n_q, n_kv) + [pl.BlockSpec((1, LANES), lambda n: (0, 0)), blk,
                                           pl.BlockSpec((WINDOW, LANES), lambda n: (n, 0)), blk],
        out_specs=[blk, kv_full, kv_full, pl.BlockSpec((8, LANES), lambda n: (0, 0))],
        out_shape=[jax.ShapeDtypeStruct((s, aw), F32), jax.ShapeDtypeStruct((s, kvd), F32),
                   jax.ShapeDtypeStruct((s, kvd), F32), jax.ShapeDtypeStruct((8, LANES), F32)],
    )(proj, proj, proj, proj, proj, sinks, out, lse, dout)


def _disc(lr, li, ls):
    dt = jnp.exp(ls)
    mag = jnp.exp(lr * dt)
    ang = li * dt
    ab_re, ab_im = mag * jnp.cos(ang), mag * jnp.sin(ang)
    den = lr * lr + li * li
    f_re = ((ab_re - 1.0) * lr + ab_im * li) / den
    f_im = (ab_im * lr - (ab_re - 1.0) * li) / den
    return ab_re, ab_im, f_re, f_im


POW_ROWS = 8
SUB = 8
TAB_ROWS = POW_ROWS + 2 * SUB


def _ssm_params_fwd(lr, li, ls, b_re, b_im, name):
    gp = lr.shape[1]
    h = b_re.shape[0]

    def body(lr_ref, li_ref, ls_ref, br_ref, bi_ref, bbr_ref, bbi_ref, tr_ref, ti_ref):
        ab_re, ab_im, f_re, f_im = _disc(lr_ref[...], li_ref[...], ls_ref[...])
        br, bi = br_ref[...], bi_ref[...]
        bbr_ref[...] = f_re * br - f_im * bi
        bbi_ref[...] = f_re * bi + f_im * br
        pr, pi = ab_re, ab_im
        for i in range(POW_ROWS):
            tr_ref[i:i + 1, :] = pr
            ti_ref[i:i + 1, :] = pi
            pr, pi = pr * pr - pi * pi, 2.0 * pr * pi
        pr, pi = ab_re, ab_im
        for r in range(SUB):
            for row in (POW_ROWS + r, POW_ROWS + 2 * SUB - 1 - r):
                tr_ref[row:row + 1, :] = pr
                ti_ref[row:row + 1, :] = pi
            pr, pi = pr * ab_re - pi * ab_im, pr * ab_im + pi * ab_re

    mat, tab = jax.ShapeDtypeStruct((h, gp), F32), jax.ShapeDtypeStruct((TAB_ROWS, gp), F32)
    return pl.pallas_call(body, name=name, out_shape=[mat, mat, tab, tab])(lr, li, ls, b_re, b_im)


def _ssm_params_bwd(lr, li, ls, b_re, b_im, dab_re, dab_im, dbb_re, dbb_im, seg, name):
    gp = lr.shape[1]
    h = b_re.shape[0]

    def body(lr_ref, li_ref, ls_ref, br_ref, bi_ref, dar_ref, dai_ref, dbbr_ref, dbbi_ref, seg_ref,
             dlr_ref, dli_ref, dls_ref, dbr_ref, dbi_ref):
        lr_, li_, ls_ = lr_ref[...], li_ref[...], ls_ref[...]
        (ab_re, ab_im, f_re, f_im), vjp = jax.vjp(_disc, lr_, li_, ls_)
        br, bi, dbbr, dbbi = br_ref[...], bi_ref[...], dbbr_ref[...], dbbi_ref[...]
        dbr_ref[...] = dbbr * f_re + dbbi * f_im
        dbi_ref[...] = dbbi * f_re - dbbr * f_im
        df_re = jnp.sum(dbbr * br + dbbi * bi, axis=0, keepdims=True)
        df_im = jnp.sum(dbbi * br - dbbr * bi, axis=0, keepdims=True)
        dlr, dli, dls = vjp((dar_ref[...], dai_ref[...], df_re, df_im))
        dlr_ref[...] = dlr
        dli_ref[...] = dli
        dls8 = jnp.broadcast_to(dls, (8, gp))
        dls_ref[...] = jnp.dot(dls8, seg_ref[...], preferred_element_type=F32, precision=lax.Precision.HIGHEST)

    vec, mat = jax.ShapeDtypeStruct((1, gp), F32), jax.ShapeDtypeStruct((h, gp), F32)
    return pl.pallas_call(body, name=name,
                          out_shape=[vec, vec, jax.ShapeDtypeStruct((8, seg.shape[1]), F32), mat, mat],
                          compiler_params=_params(24 << 20))(
        lr, li, ls, b_re, b_im, dab_re, dab_im, dbb_re, dbb_im, seg)


def _scan_bufs(t_len):
    hs = BLOCK_STATES
    return [pltpu.VMEM((hs // LANES, t_len, LANES), F32), pltpu.VMEM((hs // LANES, t_len, LANES), F32),
            pltpu.VMEM((t_len // SUB, hs), F32), pltpu.VMEM((t_len // SUB, hs), F32)]


def _scan(xr, xi, apow_ref, bufs, t_len, reverse):
    hs = BLOCK_STATES
    n_tiles = t_len // SUB
    sr_ref, si_ref, er_ref, ei_ref = bufs

    def doubling(xr, xi, n_rows, first_pow, within):
        row = lax.broadcasted_iota(jnp.int32, xr.shape, 0) & (within - 1)
        d = 1
        while d < within:
            i = first_pow + d.bit_length() - 1
            pr, pi = apow_ref[i:i + 1, :hs], apow_ref[i:i + 1, hs:]
            if reverse:
                pi, shift, keep = -pi, n_rows - d, row < within - d
            else:
                shift, keep = d, row >= d
            sr = jnp.where(keep, pltpu.roll(xr, shift, 0), 0.0)
            si = jnp.where(keep, pltpu.roll(xi, shift, 0), 0.0)
            xr, xi = xr + pr * sr - pi * si, xi + pr * si + pi * sr
            d *= 2
        return xr, xi

    shape3 = (n_tiles, SUB, hs)
    row = lax.broadcasted_iota(jnp.int32, shape3, 1)
    xr, xi = xr.reshape(shape3), xi.reshape(shape3)
    for i, d in enumerate((1, 2, 4)):
        pr, pi = apow_ref[i:i + 1, :hs], apow_ref[i:i + 1, hs:]
        if reverse:
            pi, shift, keep = -pi, SUB - d, row < SUB - d
        else:
            shift, keep = d, row >= d
        sr = jnp.where(keep, pltpu.roll(xr, shift, 1), 0.0)
        si = jnp.where(keep, pltpu.roll(xi, shift, 1), 0.0)
        xr, xi = xr + pr * sr - pi * si, xi + pr * si + pi * sr
    xr, xi = xr.reshape(t_len, hs), xi.reshape(t_len, hs)
    chunks = [slice(c * LANES, (c + 1) * LANES) for c in range(hs // LANES)]
    for c, lanes in enumerate(chunks):
        sr_ref[c] = xr[:, lanes]
        si_ref[c] = xi[:, lanes]
    edge = pl.ds(0 if reverse else SUB - 1, n_tiles, stride=SUB)
    tr, ti = doubling(jnp.concatenate([sr_ref[c, edge, :] for c in range(len(chunks))], axis=1),
                      jnp.concatenate([si_ref[c, edge, :] for c in range(len(chunks))], axis=1), n_tiles, 3, n_tiles)
    trow = lax.broadcasted_iota(jnp.int32, tr.shape, 0)
    if reverse:
        shift, keep = n_tiles - 1, trow < n_tiles - 1
    else:
        shift, keep = 1, trow >= 1
    er_ref[...] = jnp.where(keep, pltpu.roll(tr, shift, 0), 0.0)
    ei_ref[...] = jnp.where(keep, pltpu.roll(ti, shift, 0), 0.0)
    lin = POW_ROWS + SUB if reverse else POW_ROWS
    mr, mi = apow_ref[lin:lin + SUB, :hs], apow_ref[lin:lin + SUB, hs:]
    if reverse:
        mi = -mi
    for t in range(n_tiles):
        rows = slice(t * SUB, (t + 1) * SUB)
        er, ei = er_ref[t:t + 1, :], ei_ref[t:t + 1, :]
        add_r, add_i = mr * er - mi * ei, mr * ei + mi * er
        for c, lanes in enumerate(chunks):
            sr_ref[c, rows, :] += add_r[:, lanes]
            si_ref[c, rows, :] += add_i[:, lanes]
    return (jnp.concatenate([sr_ref[c] for c in range(len(chunks))], axis=1),
            jnp.concatenate([si_ref[c] for c in range(len(chunks))], axis=1))


def _ssm_chunk(s):
    t_len = _pick(s, 256, 8)
    assert t_len & (t_len - 1) == 0 and t_len <= 1 << POW_ROWS, t_len
    return t_len


def _fold_carry(br, bi, carry_ref, apow_ref, at_row, conj):
    hs = BLOCK_STATES
    cr, ci = carry_ref[0:1, :hs], carry_ref[0:1, hs:]
    ar, ai = apow_ref[0:1, :hs], apow_ref[0:1, hs:]
    if conj:
        ai = -ai
    here = lax.broadcasted_iota(jnp.int32, br.shape, 0) == at_row
    return jnp.where(here, br + (ar * cr - ai * ci), br), jnp.where(here, bi + (ar * ci + ai * cr), bi)


def _ssm_fwd(proj, ucol, bbd, ccat, dskip, apow, t_len, *, name):
    s = proj.shape[0]
    nb = bbd.shape[0]
    nc = s // t_len
    hs = BLOCK_STATES

    def body(u_ref, bbd_ref, ccat_ref, d_ref, apow_ref, y_ref, z_ref, xs_ref, carry_ref, *bufs):
        @pl.when(pl.program_id(1) == 0)
        def _():
            carry_ref[...] = jnp.zeros_like(carry_ref)

        xs_ref[...] = carry_ref[...]
        u = u_ref[...]
        bu = jnp.dot(u.astype(BF16), bbd_ref[...], preferred_element_type=F32)
        br, bi = _fold_carry(bu[:, :hs], bu[:, hs:], carry_ref, apow_ref, 0, False)
        xr, xi = _scan(br, bi, apow_ref, bufs, t_len, False)
        xcat = jnp.concatenate([xr, xi], axis=1)
        carry_ref[...] = jnp.broadcast_to(xcat[t_len - 1:t_len, :], carry_ref.shape)
        y = jnp.dot(xcat.astype(BF16), ccat_ref[...], preferred_element_type=F32) + d_ref[...] * u
        y_ref[...] = y
        z_ref[...] = _gelu(y).astype(z_ref.dtype)

    return pl.pallas_call(
        body, name=name, grid=(nb, nc),
        in_specs=[pl.BlockSpec((t_len, LANES), lambda j, n: (n, ucol + j)),
                  pl.BlockSpec((None, LANES, 2 * hs), lambda j, n: (j, 0, 0)),
                  pl.BlockSpec((None, 2 * hs, LANES), lambda j, n: (j, 0, 0)),
                  pl.BlockSpec((1, LANES), lambda j, n: (0, j)),
                  pl.BlockSpec((None, TAB_ROWS, 2 * hs), lambda j, n: (j, 0, 0))],
        out_specs=[pl.BlockSpec((t_len, LANES), lambda j, n: (n, j)),
                   pl.BlockSpec((t_len, LANES), lambda j, n: (n, j)),
                   pl.BlockSpec((None, None, 8, 2 * hs), lambda j, n: (j, n, 0, 0))],
        out_shape=[jax.ShapeDtypeStruct((s, nb * LANES), F32), jax.ShapeDtypeStruct((s, nb * LANES), BF16),
                   jax.ShapeDtypeStruct((nb, nc, 8, 2 * hs), F32)],
        scratch_shapes=[pltpu.VMEM((8, 2 * hs), F32)] + _scan_bufs(t_len),
        compiler_params=_params(40 << 20),
    )(proj, bbd, ccat, dskip, apow)


def _ssm_bwd(proj, ucol, y, dzd, dz2, xs, bbd, ccat, dskip, apow, t_len, *, name):
    s = proj.shape[0]
    nb = bbd.shape[0]
    nc = s // t_len
    hs = BLOCK_STATES

    def body(u_ref, y_ref, dzd_ref, dz2_ref, xs_ref, bbd_ref, ccat_ref, d_ref, apow_ref,
             du_ref, dbbd_ref, dccat_ref, dd_ref, da_ref, gcarry_ref, *bufs):
        @pl.when(pl.program_id(1) == 0)
        def _():
            gcarry_ref[...] = jnp.zeros_like(gcarry_ref)
            dbbd_ref[...] = jnp.zeros_like(dbbd_ref)
            dccat_ref[...] = jnp.zeros_like(dccat_ref)
            dd_ref[...] = jnp.zeros_like(dd_ref)
            da_ref[...] = jnp.zeros_like(da_ref)

        u = u_ref[...]
        ub = u.astype(BF16)
        dy = (dzd_ref[...] + dz2_ref[...]) * _gelu_grad(y_ref[...])
        dyb = dy.astype(BF16)
        bu = jnp.dot(ub, bbd_ref[...], preferred_element_type=F32)
        br, bi = _fold_carry(bu[:, :hs], bu[:, hs:], xs_ref, apow_ref, 0, False)
        xr, xi = _scan(br, bi, apow_ref, bufs[:4], t_len, False)
        sr, si = xs_ref[0:1, :hs], xs_ref[0:1, hs:]
        dxd = lax.dot_general(dyb, ccat_ref[...], _NT, preferred_element_type=F32)
        dr, di = _fold_carry(dxd[:, :hs], dxd[:, hs:], gcarry_ref, apow_ref, t_len - 1, True)
        gr, gi = _scan(dr, di, apow_ref, bufs[4:], t_len, True)
        gcat = jnp.concatenate([gr, gi], axis=1)
        gcarry_ref[...] = jnp.broadcast_to(gcat[0:1, :], gcarry_ref.shape)
        gb = gcat.astype(BF16)
        du_ref[...] = lax.dot_general(gb, bbd_ref[...], _NT, preferred_element_type=F32) + d_ref[...] * dy
        dbbd_ref[...] += lax.dot_general(ub, gb, _TN, preferred_element_type=F32)
        xb = jnp.concatenate([xr, xi], axis=1).astype(BF16)
        dccat_ref[...] += lax.dot_general(xb, dyb, _TN, preferred_element_type=F32)
        dd_ref[...] += jnp.sum(dy * u, axis=0, keepdims=True)
        first = lax.broadcasted_iota(jnp.int32, xr.shape, 0) == 0
        xpr = jnp.where(first, sr, pltpu.roll(xr, 1, 0))
        xpi = jnp.where(first, si, pltpu.roll(xi, 1, 0))
        dar = jnp.sum(gr * xpr + gi * xpi, axis=0, keepdims=True)
        dai = jnp.sum(gi * xpr - gr * xpi, axis=0, keepdims=True)
        da_ref[...] += jnp.concatenate([dar, dai], axis=1)

    def rows(j, n):
        return nc - 1 - n

    chunk = pl.BlockSpec((t_len, LANES), lambda j, n: (rows(j, n), j))
    return pl.pallas_call(
        body, name=name, grid=(nb, nc),
        in_specs=[pl.BlockSpec((t_len, LANES), lambda j, n: (rows(j, n), ucol + j)), chunk, chunk, chunk,
                  pl.BlockSpec((None, None, 8, 2 * hs), lambda j, n: (j, rows(j, n), 0, 0)),
                  pl.BlockSpec((None, LANES, 2 * hs), lambda j, n: (j, 0, 0)),
                  pl.BlockSpec((None, 2 * hs, LANES), lambda j, n: (j, 0, 0)),
                  pl.BlockSpec((1, LANES), lambda j, n: (0, j)),
                  pl.BlockSpec((None, TAB_ROWS, 2 * hs), lambda j, n: (j, 0, 0))],
        out_specs=[chunk,
                   pl.BlockSpec((None, LANES, 2 * hs), lambda j, n: (j, 0, 0)),
                   pl.BlockSpec((None, 2 * hs, LANES), lambda j, n: (j, 0, 0)),
                   pl.BlockSpec((1, LANES), lambda j, n: (0, j)),
                   pl.BlockSpec((None, 1, 2 * hs), lambda j, n: (j, 0, 0))],
        out_shape=[jax.ShapeDtypeStruct((s, nb * LANES), F32),
                   jax.ShapeDtypeStruct((nb, LANES, 2 * hs), F32),
                   jax.ShapeDtypeStruct((nb, 2 * hs, LANES), F32),
                   jax.ShapeDtypeStruct((1, nb * LANES), F32),
                   jax.ShapeDtypeStruct((nb, 1, 2 * hs), F32)],
        scratch_shapes=[pltpu.VMEM((8, 2 * hs), F32)] + _scan_bufs(t_len) + _scan_bufs(t_len),
        compiler_params=_params(48 << 20),
    )(proj, y, dzd, dz2, xs, bbd, ccat, dskip, apow)


def _to_blocks(a):
    g, p, k = a.shape
    nb = g // GROUPS_PER_BLOCK
    eye = jnp.eye(GROUPS_PER_BLOCK, dtype=a.dtype)
    a4 = a.reshape(nb, GROUPS_PER_BLOCK, p, k)
    out = jnp.einsum("ab,jbpk->jakbp", eye, a4)
    return out.reshape(nb, GROUPS_PER_BLOCK * k, GROUPS_PER_BLOCK * p)


def _from_blocks(d, p, k):
    nb = d.shape[0]
    d5 = d.reshape(nb, GROUPS_PER_BLOCK, k, GROUPS_PER_BLOCK, p)
    eye = jnp.eye(GROUPS_PER_BLOCK, dtype=bool)[None, :, None, :, None]
    diag = jnp.sum(jnp.where(eye, d5, 0.0), axis=1)
    return jnp.transpose(diag, (0, 2, 3, 1)).reshape(nb * GROUPS_PER_BLOCK, p, k)


def _merge_fwd(attn, y, gl, g_a, g_s, name):
    s, wa = attn.shape
    ws = y.shape[1]
    tm = _pick(s, 256, 16)

    def body(a_ref, y_ref, gl_ref, ga_ref, gs_ref, o_ref):
        av = a_ref[...]
        o_ref[:, :wa] = ((av * _rstd(av)) * ga_ref[...]).astype(o_ref.dtype)
        sv = _gelu(y_ref[...]) * jax.nn.sigmoid(gl_ref[...])
        o_ref[:, wa:] = ((sv * _rstd(sv)) * gs_ref[...]).astype(o_ref.dtype)

    return pl.pallas_call(
        body, name=name, grid=(s // tm,),
        in_specs=[_row_spec(tm, wa), _row_spec(tm, ws), _row_spec(tm, ws), _vec_spec(wa), _vec_spec(ws)],
        out_specs=_row_spec(tm, wa + ws), out_shape=jax.ShapeDtypeStruct((s, wa + ws), BF16),
    )(attn, y, gl, g_a, g_s)


def _merge_bwd(dmerged, attn, y, gl, g_a, g_s, name):
    s, wa = attn.shape
    ws = y.shape[1]
    tm = _pick(s, 256, 16)

    def body(dm_ref, a_ref, y_ref, gl_ref, ga_ref, gs_ref, da_ref, dgl_ref, dzd_ref, dga_ref, dgs_ref):
        @pl.when(pl.program_id(0) == 0)
        def _():
            dga_ref[...] = jnp.zeros_like(dga_ref)
            dgs_ref[...] = jnp.zeros_like(dgs_ref)

        dan, dsn = dm_ref[:, :wa], dm_ref[:, wa:]
        av = a_ref[...]
        ra = _rstd(av)
        ahat = av * ra
        dga_ref[...] += jnp.sum(dan * ahat, axis=0, keepdims=True)
        da_ref[...] = _norm_bwd(dan * ga_ref[...], ahat, ra)
        z = _gelu(y_ref[...])
        sig = jax.nn.sigmoid(gl_ref[...])
        sv = z * sig
        rs = _rstd(sv)
        shat = sv * rs
        dgs_ref[...] += jnp.sum(dsn * shat, axis=0, keepdims=True)
        dssm = _norm_bwd(dsn * gs_ref[...], shat, rs)
        dzd_ref[...] = dssm * sig
        dgl_ref[...] = (dssm * z * sig * (1.0 - sig)).astype(dgl_ref.dtype)

    return pl.pallas_call(
        body, name=name, grid=(s // tm,),
        in_specs=[_row_spec(tm, wa + ws), _row_spec(tm, wa), _row_spec(tm, ws), _row_spec(tm, ws),
                  _vec_spec(wa), _vec_spec(ws)],
        out_specs=[_row_spec(tm, wa), _row_spec(tm, ws), _row_spec(tm, ws), _vec_spec(wa), _vec_spec(ws)],
        out_shape=[jax.ShapeDtypeStruct((s, wa), F32), jax.ShapeDtypeStruct((s, ws), BF16),
                   jax.ShapeDtypeStruct((s, ws), F32), jax.ShapeDtypeStruct((1, wa), F32),
                   jax.ShapeDtypeStruct((1, ws), F32)],
    )(dmerged, attn, y, gl, g_a, g_s)


def _shift_down(main, halo, k):
    rolled = pltpu.roll(main, k, 0)
    row = lax.broadcasted_iota(jnp.int32, main.shape, 0)
    for r in range(k):
        rolled = jnp.where(row == r, halo[8 - k + r:8 - k + r + 1, :], rolled)
    return rolled


def _shift_up(main, halo, k):
    tm = main.shape[0]
    rolled = pltpu.roll(main, tm - k, 0)
    row = lax.broadcasted_iota(jnp.int32, main.shape, 0)
    for r in range(k):
        rolled = jnp.where(row == tm - k + r, halo[r:r + 1, :], rolled)
    return rolled


def _conv(main, halo, w_ref, b_ref):
    return (b_ref[...] + w_ref[0:1, :] * _shift_down(main, halo, 2) + w_ref[1:2, :] * _shift_down(main, halo, 1)
            + w_ref[2:3, :] * main)


def _gate_tiles(s, f):
    return _pick(s, 512, 16), _pick(f, 512, LANES)


def _gate_in_specs(tm, tn, nfb, order, last_row_tile=None):
    hb = tm // 8
    if order == "ij":
        ij = lambda a, b: (a, b)
    elif last_row_tile is None:
        ij = lambda a, b: (b, a)
    else:
        ij = lambda a, b: (last_row_tile - b, a)

    def main(off):
        return pl.BlockSpec((tm, tn), lambda a, b: (ij(a, b)[0], ij(a, b)[1] + off))

    def halo(off):
        return pl.BlockSpec((8, tn), lambda a, b: (jnp.maximum(ij(a, b)[0] * hb - 1, 0), ij(a, b)[1] + off))

    def vec(rows, off):
        return pl.BlockSpec((rows, tn), lambda a, b: (0, ij(a, b)[1] + off))

    return [main(0), main(nfb), halo(0), halo(nfb), vec(3, 0), vec(3, nfb), vec(1, 0), vec(1, nfb)]


def _gate_fwd(up0, conv_w, conv_b, name):
    s, f2 = up0.shape
    f = f2 // 2
    tm, tn = _gate_tiles(s, f)
    nfb = f // tn

    def body(v_ref, g_ref, vh_ref, gh_ref, wv_ref, wg_ref, bv_ref, bg_ref, o_ref):
        top = pl.program_id(0) == 0
        vh = jnp.where(top, 0.0, vh_ref[...])
        gh = jnp.where(top, 0.0, gh_ref[...])
        val = _conv(v_ref[...], vh, wv_ref, bv_ref)
        gate = _conv(g_ref[...], gh, wg_ref, bg_ref)
        o_ref[...] = (_gelu(gate) * val).astype(o_ref.dtype)

    return pl.pallas_call(
        body, name=name, grid=(s // tm, nfb),
        in_specs=_gate_in_specs(tm, tn, nfb, "ij"), out_specs=pl.BlockSpec((tm, tn), lambda i, j: (i, j)),
        out_shape=jax.ShapeDtypeStruct((s, f), BF16),
        compiler_params=_params(24 * tm * tn * 4 + (4 << 20)),
    )(up0, up0, up0, up0, conv_w, conv_w, conv_b, conv_b)


def _gate_bwd(up0, conv_w, conv_b, da, name):
    s, f2 = up0.shape
    f = f2 // 2
    tm, tn = _gate_tiles(s, f)
    nfb, ni = f // tn, s // tm

    def body(v_ref, g_ref, vh_ref, gh_ref, wv_ref, wg_ref, bv_ref, bg_ref, da_ref, dup0_ref, dcb_ref, dcw_ref,
             below_ref):
        step = pl.program_id(1)
        top = step == ni - 1

        @pl.when(step == 0)
        def _():
            dcb_ref[...] = jnp.zeros_like(dcb_ref)
            dcw_ref[...] = jnp.zeros_like(dcw_ref)
            below_ref[...] = jnp.zeros_like(below_ref)

        halos = (jnp.where(top, 0.0, vh_ref[...]), jnp.where(top, 0.0, gh_ref[...]))
        mains = (v_ref[...], g_ref[...])
        w_refs = (wv_ref, wg_ref)
        val = _conv(mains[0], halos[0], wv_ref, bv_ref)
        gate = _conv(mains[1], halos[1], wg_ref, bg_ref)
        dav = da_ref[...]
        act, act_grad = _gelu_and_grad(gate)
        dups = (dav * act, (dav * val) * act_grad)
        for half in range(2):
            dup, w_ref = dups[half], w_refs[half]
            below = below_ref[half]
            dup0_ref[half] = (w_ref[2:3, :] * dup + w_ref[1:2, :] * _shift_up(dup, below, 1)
                              + w_ref[0:1, :] * _shift_up(dup, below, 2)).astype(dup0_ref.dtype)
            below_ref[half] = dup[0:8, :]
            dcb_ref[half] += jnp.sum(dup, axis=0, keepdims=True)
            dcw_ref[half, 0:1, :] += jnp.sum(dup * _shift_down(mains[half], halos[half], 2), axis=0, keepdims=True)
            dcw_ref[half, 1:2, :] += jnp.sum(dup * _shift_down(mains[half], halos[half], 1), axis=0, keepdims=True)
            dcw_ref[half, 2:3, :] += jnp.sum(dup * mains[half], axis=0, keepdims=True)

    return pl.pallas_call(
        body, name=name, grid=(nfb, ni),
        in_specs=_gate_in_specs(tm, tn, nfb, "ji", ni - 1) + [pl.BlockSpec((tm, tn), lambda j, i: (ni - 1 - i, j))],
        out_specs=[pl.BlockSpec((2, tm, tn), lambda j, i: (0, ni - 1 - i, j)),
                   pl.BlockSpec((2, 1, tn), lambda j, i: (0, 0, j)),
                   pl.BlockSpec((2, 3, tn), lambda j, i: (0, 0, j))],
        out_shape=[jax.ShapeDtypeStruct((2, s, f), BF16), jax.ShapeDtypeStruct((2, 1, f), F32),
                   jax.ShapeDtypeStruct((2, 3, f), F32)],
        scratch_shapes=[pltpu.VMEM((2, 8, tn), F32)],
        compiler_params=_params(40 * tm * tn * 4 + (4 << 20)),
    )(up0, up0, up0, up0, conv_w, conv_w, conv_b, conv_b, da)


def _adamw(w, g, m, v, name):
    r, c = w.shape
    tr = _pick(r, max(8, (1 << 19) // max(c, 1) // 8 * 8), 8)
    c1, c2 = 1.0 / (1.0 - ADAM_B1 ** ADAM_STEP), 1.0 / (1.0 - ADAM_B2 ** ADAM_STEP)

    def body(w_ref, g_ref, m_ref, v_ref, d_ref, nm_ref, nv_ref):
        gv = g_ref[...]
        nm = ADAM_B1 * m_ref[...] + (1.0 - ADAM_B1) * gv
        nv = ADAM_B2 * v_ref[...] + (1.0 - ADAM_B2) * (gv * gv)
        nm_ref[...] = nm
        nv_ref[...] = nv
        d_ref[...] = -ADAM_LR * ((nm * c1) / (jnp.sqrt(nv * c2) + ADAM_EPS) + ADAM_WD * w_ref[...])

    spec = pl.BlockSpec((tr, c), lambda i: (i, 0))
    out = jax.ShapeDtypeStruct((r, c), F32)
    return pl.pallas_call(body, name=name, grid=(r // tr,), in_specs=[spec] * 4, out_specs=[spec] * 3,
                          out_shape=[out] * 3, compiler_params=_params(14 * tr * c * 4 + (4 << 20)))(w, g, m, v)


def _adamw_many(ws, gs, ms, vs, name):
    n = len(ws)
    c1, c2 = 1.0 / (1.0 - ADAM_B1 ** ADAM_STEP), 1.0 / (1.0 - ADAM_B2 ** ADAM_STEP)

    def body(*refs):
        w_refs, g_refs, m_refs, v_refs = (refs[i * n:(i + 1) * n] for i in range(4))
        d_refs, nm_refs, nv_refs = (refs[(4 + i) * n:(5 + i) * n] for i in range(3))
        for i in range(n):
            gv = g_refs[i][...]
            nm = ADAM_B1 * m_refs[i][...] + (1.0 - ADAM_B1) * gv
            nv = ADAM_B2 * v_refs[i][...] + (1.0 - ADAM_B2) * (gv * gv)
            nm_refs[i][...] = nm
            nv_refs[i][...] = nv
            d_refs[i][...] = -ADAM_LR * ((nm * c1) / (jnp.sqrt(nv * c2) + ADAM_EPS) + ADAM_WD * w_refs[i][...])

    shapes = [jax.ShapeDtypeStruct(w.shape, F32) for w in ws]
    outs = pl.pallas_call(body, name=name, out_shape=shapes * 3, compiler_params=_params(48 << 20))(
        *ws, *gs, *ms, *vs)
    return outs[:n], outs[n:2 * n], outs[2 * n:]


def _adamw_nd(w, g, m, v, name):
    shape = w.shape
    c = shape[-1]
    outs = _adamw(w.reshape(-1, c), g.reshape(-1, c), m.reshape(-1, c), v.reshape(-1, c), name)
    return [o.reshape(shape) for o in outs]


BIG = ("w_in", "w_glu", "w_out", "w_up", "w_down")
SMALL = ("b_ada", "g_pre_mix", "g_post_mix", "attn_sinks", "lam_re", "lam_im", "log_step", "ssm_b_re", "ssm_b_im",
         "ssm_c_re", "ssm_c_im", "ssm_d", "g_attn_out", "g_ssm_out", "g_pre_ffn", "g_post_ffn", "conv_b")
ORDER = ("w_ada", "b_ada", "g_pre_mix", "g_post_mix", "w_in", "attn_sinks", "lam_re", "lam_im", "log_step",
         "ssm_b_re", "ssm_b_im", "ssm_c_re", "ssm_c_im", "ssm_d", "w_glu", "g_attn_out", "g_ssm_out", "w_out",
         "g_pre_ffn", "g_post_ffn", "w_up", "conv_w", "conv_b", "w_down")
COL_SHARDED = ("w_in", "w_up")


def kernel(x, c, w_ada, b_ada, g_pre_mix, g_post_mix, w_in, attn_sinks, lam_re, lam_im, log_step, ssm_b_re, ssm_b_im, ssm_c_re, ssm_c_im, ssm_d, w_glu, g_attn_out, g_ssm_out, w_out, g_pre_ffn, g_post_ffn, w_up, conv_w, conv_b, w_down, loss_target, m_w_ada, m_b_ada, m_g_pre_mix, m_g_post_mix, m_w_in, m_attn_sinks, m_lam_re, m_lam_im, m_log_step, m_ssm_b_re, m_ssm_b_im, m_ssm_c_re, m_ssm_c_im, m_ssm_d, m_w_glu, m_g_attn_out, m_g_ssm_out, m_w_out, m_g_pre_ffn, m_g_post_ffn, m_w_up, m_conv_w, m_conv_b, m_w_down, v_w_ada, v_b_ada, v_g_pre_mix, v_g_post_mix, v_w_in, v_attn_sinks, v_lam_re, v_lam_im, v_log_step, v_ssm_b_re, v_ssm_b_im, v_ssm_c_re, v_ssm_c_im, v_ssm_d, v_w_glu, v_g_attn_out, v_g_ssm_out, v_w_out, v_g_pre_ffn, v_g_post_ffn, v_w_up, v_conv_w, v_conv_b, v_w_down):
    env = dict(locals())
    W = {n: env[n] for n in ORDER}
    M = {n: env["m_" + n] for n in ORDER}
    V = {n: env["v_" + n] for n in ORDER}

    depth = w_ada.shape[0]
    s, d = x.shape[1], x.shape[2]
    xs0 = x.reshape(s, d)
    tgt = loss_target.reshape(s, d)
    attn_w = d // 2
    ssm_w = d - attn_w
    in_cols = w_in.shape[2] * N_DEV
    kv_dim = (in_cols - attn_w - ssm_w) // 2
    n_q, n_kv = attn_w // HEAD_DIM, kv_dim // HEAD_DIM
    n_grp = ssm_w // SSM_GROUP
    nb = ssm_w // LANES
    f = w_down.shape[1] * N_DEV
    ucol = (attn_w + 2 * kv_dim) // LANES
    t_len = _ssm_chunk(s)
    me = 4 * lax.axis_index("x") + 2 * lax.axis_index("y") + lax.axis_index("c")

    def at_block(ref, idx):
        return ref.at[idx]

    def at_rows(n_rows):
        return lambda ref, idx: ref.at[:, pl.ds(pl.multiple_of(idx * n_rows, 8), n_rows), :]

    def at_cols(n_cols):
        return lambda ref, idx: ref.at[:, :, pl.ds(pl.multiple_of(idx * n_cols, LANES), n_cols)]

    first = _gather_multi([w_in.astype(BF16), conv_w, c],
                          [(N_DEV,) + w_in.shape, (N_DEV,) + conv_w.shape, (N_DEV,) + c.shape],
                          [at_block, at_block, at_block], "ag_first")
    w_in_full = _cols_from_blocks(first[0], "w_in_layout")
    conv_w_full = jnp.transpose(first[1], (1, 2, 0, 3)).reshape(depth, 3, 2 * f)
    c_all = first[2].reshape(N_DEV, d)

    def at_rows2(n_rows):
        return lambda ref, idx: ref.at[pl.ds(pl.multiple_of(idx * n_rows, 8), n_rows), :]

    def at_cols2(n_cols):
        return lambda ref, idx: ref.at[:, pl.ds(pl.multiple_of(idx * n_cols, LANES), n_cols)]

    def whole(ref, idx):
        return ref

    def gather_kind(n):
        return "blk" if n == "w_in" else "cols" if n in COL_SHARDED else "rows"

    def gather_view(n):
        return {"blk": at_block, "cols": at_cols2(W[n].shape[2]), "rows": at_rows2(W[n].shape[1])}[gather_kind(n)]

    def gather_shape(n):
        _, a, b = W[n].shape
        return {"blk": (N_DEV, a, b), "cols": (a, N_DEV * b), "rows": (N_DEV * a, b)}[gather_kind(n)]

    later = [(n, l) for l in range(depth) for n in BIG[1:]]
    later_srcs = [W[n][l].astype(BF16) for n, l in later]
    later_views = [gather_view(n) for n, _ in later]
    me_arr = me.astype(jnp.int32).reshape(1)
    lands = [_place_own(me_arr, src, lax.empty(gather_shape(n), BF16), gather_kind(n), f"ag_own_{n}{l}")
             for (n, l), src in zip(later, later_srcs)]
    ag_started, ag_token = _exchange_start(later_srcs, lands, [whole] * len(later), later_views, "ag_start")

    def weights_arrived(names, l, after, name):
        picks = [later.index((n, l)) for n in names]
        _, got = _exchange_wait([ag_started[i] for i in picks], [after], [whole] * len(picks),
                                [later_views[i] for i in picks], name)
        return dict(zip(names, got))

    c_pad = jnp.pad(c_all, ((0, 16 - N_DEV), (0, 0)))
    n_ada = w_ada.shape[2]
    b_shard = lax.dynamic_slice_in_dim(b_ada, me * n_ada, n_ada, axis=1).reshape(depth, 1, n_ada)
    ada_part, c_act = _ada_fwd(c_pad, w_ada, b_shard, "ada_fwd")
    ada_all = _all_gather(ada_part.reshape(depth * 16, n_ada), "ag_ada").reshape(N_DEV, depth, 16, n_ada)
    ada_me = lax.dynamic_index_in_dim(ada_all, me, axis=2, keepdims=False)
    ada = jnp.transpose(ada_me, (1, 0, 2)).reshape(depth, 6, 1, d) + ag_token[0, 0]

    gp = n_grp * STATE

    def hgp(a):
        return jnp.transpose(a, (2, 0, 1)).reshape(SSM_GROUP, gp)

    ssm = []
    for l in range(depth):
        lr, li = lam_re[l].reshape(1, gp), lam_im[l].reshape(1, gp)
        ls = jnp.repeat(log_step[l], STATE).reshape(1, gp)
        br, bi = hgp(ssm_b_re[l]), hgp(ssm_b_im[l])
        bbr, bbi, tab_r, tab_i = _ssm_params_fwd(lr, li, ls, br, bi, f"ssm_params_fwd{l}")
        bb_re = jnp.transpose(bbr.reshape(SSM_GROUP, n_grp, STATE), (1, 2, 0))
        bb_im = jnp.transpose(bbi.reshape(SSM_GROUP, n_grp, STATE), (1, 2, 0))
        bbd = jnp.concatenate([_to_blocks(bb_re), _to_blocks(bb_im)], axis=2).astype(BF16)
        c_re_t = jnp.transpose(ssm_c_re[l], (0, 2, 1))
        c_im_t = jnp.transpose(ssm_c_im[l], (0, 2, 1))
        ccat = jnp.concatenate([jnp.transpose(_to_blocks(c_re_t), (0, 2, 1)),
                                -jnp.transpose(_to_blocks(c_im_t), (0, 2, 1))], axis=1).astype(BF16)

        def tab(t):
            return t.reshape(TAB_ROWS, nb, BLOCK_STATES)

        apow = jnp.transpose(jnp.concatenate([tab(tab_r), tab(tab_i)], axis=2), (1, 0, 2))
        ssm.append(dict(lr=lr, li=li, ls=ls, br=br, bi=bi, bbd=bbd, ccat=ccat, apow=apow,
                        dskip=ssm_d[l].reshape(1, ssm_w)))

    sinks_pad = jnp.pad(attn_sinks, ((0, 0), (0, LANES - n_q)))

    def vec(a):
        return a.reshape(1, -1)

    saved = []
    fw = [dict() for _ in range(depth)]
    xin = xs0
    for l in range(depth):
        sh_m, sc_m, gt_m, sh_f, sc_f, gt_f = (ada[l, i] for i in range(6))
        p = ssm[l]
        h1 = _modnorm_fwd(xin, vec(g_pre_mix[l]), sc_m, sh_m, f"modnorm_mix_fwd{l}")
        proj = _matmul(h1, w_in_full[l], name=f"mm_in{l}")
        attn, lse = _attn_fwd(proj, sinks_pad[l:l + 1], n_q=n_q, n_kv=n_kv, name=f"attn_fwd{l}")
        y, z, xstart = _ssm_fwd(proj, ucol, p["bbd"], p["ccat"], p["dskip"], p["apow"], t_len, name=f"ssm_fwd{l}")
        fw[l].update(weights_arrived(("w_glu", "w_out"), l, z, f"ag_wait_mix{l}"))
        gl = _matmul(z, fw[l]["w_glu"], name=f"mm_glu{l}")
        merged = _merge_fwd(attn, y, gl, vec(g_attn_out[l]), vec(g_ssm_out[l]), f"merge_fwd{l}")
        mix = _matmul(merged, fw[l]["w_out"], name=f"mm_out{l}")
        x2 = _resnorm_fwd(xin, mix, vec(g_post_mix[l]), gt_m, f"resnorm_mix_fwd{l}")
        h2 = _modnorm_fwd(x2, vec(g_pre_ffn[l]), sc_f, sh_f, f"modnorm_ffn_fwd{l}")
        fw[l].update(weights_arrived(("w_up",), l, h2, f"ag_wait_up{l}"))
        up0 = _matmul(h2, fw[l]["w_up"], name=f"mm_up{l}")
        cw, cb = conv_w_full[l], vec(conv_b[l])
        act = _gate_fwd(up0, cw, cb, f"gate_fwd{l}")
        fw[l].update(weights_arrived(("w_down",), l, act, f"ag_wait_down{l}"))
        ff = _matmul(act, fw[l]["w_down"], name=f"mm_down{l}")
        x3 = _resnorm_fwd(x2, ff, vec(g_post_ffn[l]), gt_f, f"resnorm_ffn_fwd{l}")
        saved.append(dict(xin=xin, h1=h1, proj=proj, attn=attn, lse=lse, y=y, z=z, xstart=xstart, gl=gl,
                          merged=merged, mix=mix, x2=x2, h2=h2, up0=up0, act=act, ff=ff))
        xin = x3

    dxo, loss_acc = _loss_bwd(xin, tgt, "loss")
    loss = lax.psum(loss_acc[0, 0], ("x", "y", "c"))

    grads = {n: [None] * depth for n in ORDER}
    dada = [None] * depth
    big_blocks = {n: [None] * depth for n in BIG}
    seg = jnp.pad(jnp.repeat(jnp.eye(n_grp, dtype=F32), STATE, axis=0), ((0, 0), (0, (-n_grp) % LANES)))

    def part_view(n):
        shp = W[n].shape
        if n == "w_in":
            return at_block, "blk"
        if n in COL_SHARDED:
            return at_cols2(shp[2]), "cols"
        return at_rows2(shp[1]), "rows"

    rs_groups, start_tokens = [], []
    small_order = SMALL + ("conv_w",)
    small_shapes = {n: W[n].shape for n in SMALL}
    small_shapes["conv_w"] = (depth, 3, 2 * f)
    small_started = [None] * depth

    def send_partials(items, name):
        parts = [big_blocks[n][l] for n, l in items]
        lands = [lax.empty((N_DEV,) + W[n].shape[1:], BF16) for n, _ in items]
        started, token = _exchange_start(parts, lands, [part_view(n)[0] for n, _ in items],
                                         [at_block] * len(items), name)
        rs_groups.append((items, started, name))
        start_tokens.append(token)
        return token[0, 0]

    order = jnp.zeros((), F32)
    for l in reversed(range(depth)):
        sh_m, sc_m, gt_m, sh_f, sc_f, gt_f = (ada[l, i] for i in range(6))
        gt_f = gt_f + order
        a, p = saved[l], ssm[l]
        cw, cb = conv_w_full[l], vec(conv_b[l])
        dff, dg, dgt_f = _resnorm_bwd(dxo, a["ff"], vec(g_post_ffn[l]), gt_f, f"resnorm_ffn_bwd{l}")
        grads["g_post_ffn"][l] = dg
        dact = _matmul(dff, fw[l]["w_down"], tb=True, name=f"mm_down_dx{l}")
        big_blocks["w_down"][l] = _matmul(a["act"], dff, ta=True, out_dtype=BF16, name=f"mm_down_dw{l}")
        dup0, dcb, dcw = _gate_bwd(a["up0"], cw, cb, dact, f"gate_bwd{l}")
        grads["conv_b"][l] = dcb.reshape(1, 2 * f)
        grads["conv_w"][l] = jnp.transpose(dcw, (1, 0, 2)).reshape(3, 2 * f)
        dh2 = _matmul(dup0, fw[l]["w_up"], tb=True, a_halves=True, name=f"mm_up_dx{l}")
        big_blocks["w_up"][l] = _matmul(a["h2"], dup0, ta=True, b_halves=True, out_dtype=BF16,
                                        name=f"mm_up_dw{l}")
        if l == 0:
            sc_f = sc_f + send_partials([("w_down", 0), ("w_up", 0)], "rs_start_ffn0")
        dx2, dg, dsc_f, dsh_f = _modnorm_bwd(dh2, a["x2"], vec(g_pre_ffn[l]), sc_f, dxo, f"modnorm_ffn_bwd{l}")
        grads["g_pre_ffn"][l] = dg
        dmix, dg, dgt_m = _resnorm_bwd(dx2, a["mix"], vec(g_post_mix[l]), gt_m, f"resnorm_mix_bwd{l}")
        grads["g_post_mix"][l] = dg
        dmerged = _matmul(dmix, fw[l]["w_out"], tb=True, name=f"mm_out_dx{l}")
        big_blocks["w_out"][l] = _matmul(a["merged"], dmix, ta=True, out_dtype=BF16, name=f"mm_out_dw{l}")
        dattn, dgl, dzd, dga, dgs = _merge_bwd(dmerged, a["attn"], a["y"], a["gl"], vec(g_attn_out[l]),
                                               vec(g_ssm_out[l]), f"merge_bwd{l}")
        grads["g_attn_out"][l], grads["g_ssm_out"][l] = dga, dgs
        dz2 = _matmul(dgl, fw[l]["w_glu"], tb=True, name=f"mm_glu_dx{l}")
        big_blocks["w_glu"][l] = _matmul(a["z"], dgl, ta=True, out_dtype=BF16, name=f"mm_glu_dw{l}")
        dskip = p["dskip"]
        if l == 0:
            dskip = dskip + send_partials([("w_out", 0), ("w_glu", 0)], "rs_start_mix0")
        du, dbbd, dccat, dd, da = _ssm_bwd(a["proj"], ucol, a["y"], dzd, dz2, a["xstart"], p["bbd"], p["ccat"],
                                           dskip, p["apow"], t_len, name=f"ssm_bwd{l}")
        grads["ssm_d"][l] = dd
        hs = BLOCK_STATES
        dbb_re = _from_blocks(dbbd[:, :, :hs], STATE, SSM_GROUP)
        dbb_im = _from_blocks(dbbd[:, :, hs:], STATE, SSM_GROUP)
        dccat_t = jnp.transpose(dccat, (0, 2, 1))
        grads["ssm_c_re"][l] = jnp.transpose(_from_blocks(dccat_t[:, :, :hs], STATE, SSM_GROUP), (0, 2, 1))
        grads["ssm_c_im"][l] = -jnp.transpose(_from_blocks(dccat_t[:, :, hs:], STATE, SSM_GROUP), (0, 2, 1))
        dab_re, dab_im = da[:, 0, :hs].reshape(1, gp), da[:, 0, hs:].reshape(1, gp)
        dlr, dli, dls, dbr, dbi = _ssm_params_bwd(p["lr"], p["li"], p["ls"], p["br"], p["bi"], dab_re, dab_im,
                                                  hgp(dbb_re), hgp(dbb_im), seg, f"ssm_params_bwd{l}")
        grads["lam_re"][l], grads["lam_im"][l], grads["log_step"][l] = dlr, dli, dls[0, :n_grp]
        grads["ssm_b_re"][l] = jnp.transpose(dbr.reshape(SSM_GROUP, n_grp, STATE), (1, 2, 0))
        grads["ssm_b_im"][l] = jnp.transpose(dbi.reshape(SSM_GROUP, n_grp, STATE), (1, 2, 0))
        dq, dk, dv, dsink = _attn_bwd(a["proj"], sinks_pad[l:l + 1], a["attn"], a["lse"], dattn,
                                      n_q=n_q, n_kv=n_kv, name=f"attn_bwd{l}")
        grads["attn_sinks"][l] = dsink[0, :n_q]
        dproj = jnp.concatenate([dq, dk, dv, du], axis=1).astype(BF16)
        dh1 = _matmul(dproj, w_in_full[l], tb=True, name=f"mm_in_dx{l}")
        big_blocks["w_in"][l] = _blocks_from_cols(_matmul(a["h1"], dproj, ta=True, name=f"mm_in_dw{l}"),
                                                  f"w_in_grad_layout{l}")
        dxo, dg, dsc_m, dsh_m = _modnorm_bwd(dh1, a["xin"], vec(g_pre_mix[l]), sc_m, dx2, f"modnorm_mix_bwd{l}")
        grads["g_pre_mix"][l] = dg
        dada[l] = jnp.concatenate([dsh_m, dsc_m, dgt_m, dsh_f, dsc_f, dgt_f], axis=1)
        if l > 0:
            order = send_partials([(n, l) for n in reversed(BIG)], f"rs_start_layer{l}")
        else:
            order = order + send_partials([("w_in", 0)], "rs_start_in0")
        spack = _pack([dada[l]] + [grads[n][l] for n in small_order[1:]], F32, 1024)
        started, token = _exchange_start([spack], [lax.empty((N_DEV,) + spack.shape, F32)], [whole], [at_block],
                                         f"small_start{l}")
        small_started[l] = started
        start_tokens.append(token)
        order = order + token[0, 0]
    grad_x = dxo.reshape(x.shape)

    delta, new_m, new_v = {}, {}, {}
    stacked = {n: None for n in BIG}
    landed_layers = {n: 0 for n in BIG}
    after = [dxo] + start_tokens
    for items, started, name in rs_groups:
        mine, landed = _exchange_wait(started, after, [part_view(n)[0] for n, _ in items], [at_block] * len(items),
                                      name.replace("start", "wait"))
        for (n, l), part, slots in zip(items, mine, landed):
            stacked[n] = _sum_slots_own(me_arr, slots, part, part_view(n)[1], f"rs_sum_{n}{l}", layer=l,
                                        n_layers=depth, stacked=stacked[n])
            landed_layers[n] += 1
            if landed_layers[n] == depth:
                grads[n] = stacked[n]
                delta[n], new_m[n], new_v[n] = _adamw_nd(W[n], grads[n], M[n], V[n], f"adamw_{n}")
                after.append(delta[n])

    n_cw = conv_w.shape[2]
    small_sums, dada_rows = [None] * depth, [None] * depth
    for l in reversed(range(depth)):
        mine, landed = _exchange_wait(small_started[l], after, [whole], [at_block], f"small_wait{l}")
        ssum = _sum_slots_own(me_arr, landed[0], mine[0], "self", f"sum_small{l}").reshape(-1)
        small_sums[l] = _unpack(ssum, [small_shapes[n][1:] for n in small_order])
        slot = lax.broadcasted_iota(jnp.int32, (N_DEV, 6 * d), 0)
        dada_rows[l] = jnp.where(slot == me, mine[0].reshape(-1)[:6 * d][None],
                                 landed[0].reshape(N_DEV, -1)[:, :6 * d])
    for i, n in enumerate(small_order):
        grads[n] = jnp.stack([small_sums[l][i] for l in range(depth)])
    grads["conv_w"] = lax.dynamic_slice_in_dim(grads["conv_w"], me * n_cw, n_cw, axis=2)
    dada_all = jnp.stack(dada_rows, axis=1)
    dada_shard = lax.dynamic_slice_in_dim(dada_all, me * n_ada, n_ada, axis=2)
    kp = LANES
    dada_pad = jnp.pad(jnp.transpose(dada_shard, (1, 0, 2)), ((0, 0), (0, kp - N_DEV), (0, 0)))
    act_t = jnp.pad(jnp.transpose(c_act[:N_DEV]), ((0, 0), (0, kp - N_DEV)))
    grads["w_ada"] = _ada_wgrad(act_t, dada_pad, "ada_wgrad")

    delta["w_ada"], new_m["w_ada"], new_v["w_ada"] = _adamw_nd(W["w_ada"], grads["w_ada"], M["w_ada"], V["w_ada"],
                                                                "adamw_w_ada")

    def lane_friendly(a):
        return a.reshape(-1, 1024) if a.ndim > 2 and a.shape[-1] < LANES and a.size % 1024 == 0 else a

    rest = SMALL + ("conv_w",)
    outs = _adamw_many(*[[lane_friendly(t[n]) for n in rest] for t in (W, grads, M, V)], "adamw_small")
    for tgt_d, vals in zip((delta, new_m, new_v), outs):
        for n, val in zip(rest, vals):
            tgt_d[n] = val.reshape(W[n].shape)

    return (loss, grad_x, *[grads[n] for n in ORDER], *[delta[n] for n in ORDER],
            *[new_m[n] for n in ORDER], *[new_v[n] for n in ORDER])
```

```python
import functools
import math

import jax
import jax.numpy as jnp
from jax import lax
from jax.experimental import pallas as pl
from jax.experimental.pallas import tpu as pltpu

F32 = jnp.float32
BF16 = jnp.bfloat16

N_DEV = 8
HEAD_DIM = 64
WINDOW = 128
SSM_GROUP = 16
STATE = 64
LANES = 128
GROUPS_PER_BLOCK = LANES // SSM_GROUP
BLOCK_STATES = GROUPS_PER_BLOCK * STATE
EPS = 1e-6
NEG = -1e30
ADAM_LR, ADAM_B1, ADAM_B2, ADAM_EPS, ADAM_WD, ADAM_STEP = 0.001, 0.9, 0.999, 1e-08, 0.01, 10
VMEM_BYTES_V7X = 64 * 1024 * 1024
GELU_C = math.sqrt(2.0 / math.pi)
MESH = pl.DeviceIdType.MESH
ANY = pl.BlockSpec(memory_space=pl.ANY)


def _pick(n, pref, align):
    t = (min(pref, n) // align) * align
    while t >= align:
        if n % t == 0:
            return t
        t -= align
    return n


def _params(vmem_bytes=None):
    if vmem_bytes is None:
        return pltpu.CompilerParams()
    return pltpu.CompilerParams(vmem_limit_bytes=int(min(vmem_bytes, VMEM_BYTES_V7X - (8 << 20))))


def _gelu_and_grad(x):
    x2 = x * x
    half_x = 0.5 * x
    th = jnp.tanh((GELU_C * x) * (1.0 + 0.044715 * x2))
    one_th = 1.0 + th
    grad = 0.5 * one_th + (half_x * (1.0 - th * th)) * (GELU_C + (3.0 * 0.044715 * GELU_C) * x2)
    return half_x * one_th, grad


def _gelu(x):
    return _gelu_and_grad(x)[0]


def _gelu_grad(x):
    return _gelu_and_grad(x)[1]


def _rstd(x):
    return lax.rsqrt(jnp.mean(x * x, axis=-1, keepdims=True) + EPS)


def _norm_bwd(dhat, xhat, r):
    return r * (dhat - xhat * jnp.mean(dhat * xhat, axis=-1, keepdims=True))


def _matmul(a, b, *, ta=False, tb=False, a_halves=False, b_halves=False, out_dtype=F32, name):
    assert not (a_halves and ta) and not (b_halves and tb)
    if a_halves:
        m, kdim = a.shape[1], 2 * a.shape[2]
    else:
        (kdim, m) = a.shape if ta else a.shape[::-1]
    if b_halves:
        k2, n = b.shape[1], 2 * b.shape[2]
    else:
        (n, k2) = b.shape if tb else b.shape[::-1]
    assert kdim == k2, (a.shape, b.shape, ta, tb)
    tm = _pick(m, 1024, LANES)
    tn = _pick(n // 2, 1536, LANES) if b_halves else _pick(n, 1024, LANES)
    tk = _pick(kdim // 2, 2816, LANES) if a_halves else _pick(kdim, 2816, LANES)
    nk = kdim // tk
    dn = (((0 if ta else 1,), (1 if tb else 0,)), ((), ()))

    def partial_product(a_ref, b_ref):
        return lax.dot_general(a_ref[...].astype(BF16), b_ref[...].astype(BF16), dn, preferred_element_type=F32)

    def body_one(a_ref, b_ref, o_ref):
        o_ref[...] = partial_product(a_ref, b_ref).astype(o_ref.dtype)

    def body_acc(a_ref, b_ref, o_ref, acc_ref):
        k = pl.program_id(2)

        @pl.when(k == 0)
        def _():
            acc_ref[...] = partial_product(a_ref, b_ref)

        @pl.when((k > 0) & (k < nk - 1))
        def _():
            acc_ref[...] += partial_product(a_ref, b_ref)

        @pl.when(k == nk - 1)
        def _():
            o_ref[...] = (acc_ref[...] + partial_product(a_ref, b_ref)).astype(o_ref.dtype)

    body = body_one if nk == 1 else body_acc
    a_spec = pl.BlockSpec((tk, tm), lambda i, j, k: (k, i)) if ta else pl.BlockSpec((tm, tk), lambda i, j, k: (i, k))
    b_spec = pl.BlockSpec((tn, tk), lambda i, j, k: (j, k)) if tb else pl.BlockSpec((tk, tn), lambda i, j, k: (k, j))
    if a_halves:
        nkh = nk // 2
        a_spec = pl.BlockSpec((None, tm, tk), lambda i, j, k: (k // nkh, i, k % nkh))
    if b_halves:
        njh = n // tn // 2
        b_spec = pl.BlockSpec((None, tk, tn), lambda i, j, k: (j // njh, k, j % njh))
    vmem = (2 * (tm * tk * a.dtype.itemsize + tk * tn * b.dtype.itemsize) + tm * tn * 4
            + 2 * tm * tn * jnp.dtype(out_dtype).itemsize + 3 * tm * tn * 4 + (4 << 20))
    return pl.pallas_call(
        body, name=name, grid=(m // tm, n // tn, nk),
        in_specs=[a_spec, b_spec], out_specs=pl.BlockSpec((tm, tn), lambda i, j, k: (i, j)),
        out_shape=jax.ShapeDtypeStruct((m, n), out_dtype),
        scratch_shapes=[] if nk == 1 else [pltpu.VMEM((tm, tn), F32)],
        compiler_params=_params(vmem),
    )(a, b)


def _all_gather(x, name):
    def body(x_ref, out_ref, send_sems, recv_sems, local_sem):
        x_, y_, c_ = lax.axis_index("x"), lax.axis_index("y"), lax.axis_index("c")
        me, sibling = (x_, y_, c_), (x_, y_, 1 - c_)
        chips = [(1 - x_, y_), (x_, 1 - y_), (1 - x_, 1 - y_)]

        def slot(px, py, pc):
            return out_ref.at[4 * px + 2 * py + pc]

        def copy(k, block, to, src=None):
            return pltpu.make_async_remote_copy(
                src_ref=slot(*block) if src is None else src, dst_ref=slot(*block),
                send_sem=send_sems.at[k], recv_sem=recv_sems.at[k], device_id=to, device_id_type=MESH)

        mine = pltpu.make_async_copy(x_ref, slot(*me), local_sem)
        mine.start()
        first = [copy(0, me, sibling, src=x_ref)]
        first += [copy(1 + j, me, (*chip, c_), src=x_ref) for j, chip in enumerate(chips)]
        for cp in first:
            cp.start()
        passed = [copy(4 + j, (*chip, c_), sibling) for j, chip in enumerate(chips)]
        for j, chip in enumerate(chips):
            copy(1 + j, (*chip, c_), me).wait_recv()
            passed[j].start()
        copy(0, sibling, me).wait_recv()
        for j, chip in enumerate(chips):
            copy(4 + j, (*chip, 1 - c_), me).wait_recv()
        for cp in first + passed:
            cp.wait_send()
        mine.wait()

    return pl.pallas_call(
        body, name=name, out_shape=jax.ShapeDtypeStruct((N_DEV,) + x.shape, x.dtype),
        in_specs=[ANY], out_specs=ANY,
        scratch_shapes=[pltpu.SemaphoreType.DMA((7,)), pltpu.SemaphoreType.DMA((7,)), pltpu.SemaphoreType.DMA],
    )(x)


def _gather_multi(srcs, out_shapes, views, name):
    n = len(srcs)

    def body(*refs):
        src_refs, out_refs = refs[:n], refs[n:2 * n]
        send_sems, recv_sems, local_sems = refs[2 * n:]
        x_, y_, c_ = lax.axis_index("x"), lax.axis_index("y"), lax.axis_index("c")
        me, sibling = (x_, y_, c_), (x_, y_, 1 - c_)
        chips = [(1 - x_, y_), (x_, 1 - y_), (1 - x_, 1 - y_)]

        def slot(i, px, py, pc):
            return views[i](out_refs[i], 4 * px + 2 * py + pc)

        def copy(i, k, block, to, from_src=False):
            return pltpu.make_async_remote_copy(
                src_ref=src_refs[i] if from_src else slot(i, *block), dst_ref=slot(i, *block),
                send_sem=send_sems.at[7 * i + k], recv_sem=recv_sems.at[7 * i + k], device_id=to, device_id_type=MESH)

        mine = [pltpu.make_async_copy(src_refs[i], slot(i, *me), local_sems.at[i]) for i in range(n)]
        for cp in mine:
            cp.start()
        first = []
        for i in range(n):
            first.append(copy(i, 0, me, sibling, True))
            first += [copy(i, 1 + j, me, (*chip, c_), True) for j, chip in enumerate(chips)]
        for cp in first:
            cp.start()
        passed = []
        for j, chip in enumerate(chips):
            for i in range(n):
                copy(i, 1 + j, (*chip, c_), me).wait_recv()
                fwd = copy(i, 4 + j, (*chip, c_), sibling)
                fwd.start()
                passed.append(fwd)
        for i in range(n):
            copy(i, 0, sibling, me).wait_recv()
            for j, chip in enumerate(chips):
                copy(i, 4 + j, (*chip, 1 - c_), me).wait_recv()
        for cp in first + passed:
            cp.wait_send()
        for cp in mine:
            cp.wait()

    return pl.pallas_call(
        body, name=name, out_shape=[jax.ShapeDtypeStruct(s, a.dtype) for s, a in zip(out_shapes, srcs)],
        in_specs=[ANY] * n, out_specs=[ANY] * n,
        scratch_shapes=[pltpu.SemaphoreType.DMA((7 * n,)), pltpu.SemaphoreType.DMA((7 * n,)),
                        pltpu.SemaphoreType.DMA((n,))],
    )(*srcs)


HBM_SPEC = pl.BlockSpec(memory_space=pltpu.HBM)
SEM_SPEC = pl.BlockSpec(memory_space=pltpu.SEMAPHORE)
SIDE_EFFECT = pltpu.SideEffectType.DATAFLOW_SIDE_EFFECTING
N_PEERS = N_DEV - 1


def _peer(k, x_, y_, c_):
    px = 1 - x_ if (k >> 2) & 1 else x_
    py = 1 - y_ if (k >> 1) & 1 else y_
    pc = 1 - c_ if k & 1 else c_
    return (px, py, pc), 4 * px + 2 * py + pc


def _exchange_copies(src_refs, land_refs, send_sems, recv_sems, src_views, dst_views):
    x_, y_, c_ = lax.axis_index("x"), lax.axis_index("y"), lax.axis_index("c")
    me = 4 * x_ + 2 * y_ + c_
    out = []
    for i in range(len(src_refs)):
        for k in range(1, N_DEV):
            peer, idx = _peer(k, x_, y_, c_)

            def copy(dst_slot, i=i, k=k, peer=peer, idx=idx):
                return pltpu.make_async_remote_copy(
                    src_ref=src_views[i](src_refs[i], idx), dst_ref=dst_views[i](land_refs[i], dst_slot),
                    send_sem=send_sems[i].at[k - 1], recv_sem=recv_sems[i].at[k - 1], device_id=peer,
                    device_id_type=MESH)

            out.append((copy(me), copy(idx)))
    return out


def _exchange_start(srcs, lands, src_views, dst_views, name):
    n = len(srcs)

    def body(*refs):
        src_refs, land_refs = refs[:n], refs[n:2 * n]
        send_sems, recv_sems = refs[2 * n:3 * n], refs[3 * n:4 * n]
        token = refs[-1]
        for send, _ in _exchange_copies(src_refs, land_refs, send_sems, recv_sems, src_views, dst_views):
            send.start()
        token[...] = jnp.zeros_like(token)

    sems = [pltpu.SemaphoreType.DMA((N_PEERS,))] * n
    thru = [pltpu.HBM(a.shape, a.dtype) for a in list(srcs) + list(lands)]
    outs = pl.pallas_call(
        body, name=name, out_shape=sems + sems + thru + [jax.ShapeDtypeStruct((8, LANES), F32)],
        in_specs=[HBM_SPEC] * (2 * n),
        out_specs=[SEM_SPEC] * (2 * n) + [HBM_SPEC] * (2 * n) + [pl.BlockSpec(memory_space=pltpu.VMEM)],
        input_output_aliases={j: 2 * n + j for j in range(2 * n)},
        compiler_params=pltpu.CompilerParams(has_side_effects=SIDE_EFFECT),
    )(*[pltpu.with_memory_space_constraint(a, pltpu.HBM) for a in list(srcs) + list(lands)])
    per_array = [(outs[j], outs[n + j], outs[2 * n + j], outs[3 * n + j]) for j in range(n)]
    return per_array, outs[-1]


def _exchange_wait(started, after, src_views, dst_views, name):
    send_sems, recv_sems, srcs, lands = (list(t) for t in zip(*started))
    n = len(srcs)
    after = list(after)

    def body(*refs):
        src_refs, land_refs = refs[:n], refs[n:2 * n]
        send_refs, recv_refs = refs[2 * n:3 * n], refs[3 * n:4 * n]
        copies = _exchange_copies(src_refs, land_refs, send_refs, recv_refs, src_views, dst_views)
        for send, _ in copies:
            send.wait_send()
        for _, recv in copies:
            recv.wait_recv()

    thru = [pltpu.HBM(a.shape, a.dtype) for a in list(srcs) + list(lands)]
    outs = pl.pallas_call(
        body, name=name, out_shape=thru,
        in_specs=[HBM_SPEC] * (2 * n) + [SEM_SPEC] * (2 * n) + [ANY] * len(after),
        out_specs=[HBM_SPEC] * (2 * n),
        input_output_aliases={j: j for j in range(2 * n)},
        compiler_params=pltpu.CompilerParams(has_side_effects=SIDE_EFFECT),
    )(*srcs, *lands, *send_sems, *recv_sems, *after)
    return outs[:n], outs[n:]


def _place_own(me, src, land, kind, name):
    r, c = src.shape
    tr = _pick(r, 512, 16)
    nt = r // tr
    if kind == "rows":
        out_spec = pl.BlockSpec((tr, c), lambda i, mr: (mr[0] * nt + i, 0))
    elif kind == "cols":
        out_spec = pl.BlockSpec((tr, c), lambda i, mr: (i, mr[0]))
    else:
        out_spec = pl.BlockSpec((None, tr, c), lambda i, mr: (mr[0], i, 0))

    def body(me_ref, s_ref, land_ref, o_ref):
        o_ref[...] = s_ref[...]

    return pl.pallas_call(
        body, name=name,
        grid_spec=pltpu.PrefetchScalarGridSpec(
            num_scalar_prefetch=1, grid=(nt,),
            in_specs=[pl.BlockSpec((tr, c), lambda i, mr: (i, 0)), ANY], out_specs=out_spec),
        out_shape=jax.ShapeDtypeStruct(land.shape, land.dtype),
        input_output_aliases={2: 0},
    )(me, src, land)


def _sum_slots_own(me, landed, part, kind, name, *, layer=0, n_layers=1, stacked=None):
    _, r, c = landed.shape
    tr = _pick(r, 512, 16)
    nt = r // tr
    if kind == "rows":
        part_spec = pl.BlockSpec((tr, c), lambda i, mr: (mr[0] * nt + i, 0))
    elif kind == "cols":
        part_spec = pl.BlockSpec((tr, c), lambda i, mr: (i, mr[0]))
    elif kind == "blk":
        part_spec = pl.BlockSpec((None, tr, c), lambda i, mr: (mr[0], i, 0))
    else:
        part_spec = pl.BlockSpec((tr, c), lambda i, mr: (i, 0))

    def body(me_ref, x_ref, p_ref, *rest):
        o_ref = rest[-1]
        own = p_ref[...].astype(F32)
        acc = jnp.where(me_ref[0] == 0, own, x_ref[0].astype(F32))
        for i in range(1, N_DEV):
            acc = acc + jnp.where(me_ref[0] == i, own, x_ref[i].astype(F32))
        o_ref[...] = acc

    operands = [me, landed, part] + ([] if stacked is None else [stacked])
    return pl.pallas_call(
        body, name=name,
        grid_spec=pltpu.PrefetchScalarGridSpec(
            num_scalar_prefetch=1, grid=(nt,),
            in_specs=[pl.BlockSpec((N_DEV, tr, c), lambda i, mr: (0, i, 0)), part_spec]
            + ([] if stacked is None else [ANY]),
            out_specs=pl.BlockSpec((None, tr, c), lambda i, mr: (layer, i, 0))),
        out_shape=jax.ShapeDtypeStruct((n_layers, r, c), F32),
        input_output_aliases={} if stacked is None else {3: 0},
        compiler_params=_params(2 * N_DEV * tr * c * landed.dtype.itemsize + 8 * tr * c * 4 + (4 << 20)),
    )(*operands)


def _cols_from_blocks(blk, name):
    nd, nl, k, n = blk.shape
    tk = _pick(k, 256, 16)

    def body(b_ref, o_ref, wide_ref):
        for dev in range(nd):
            wide_ref[:, dev * n:(dev + 1) * n] = b_ref[dev].astype(F32)
        o_ref[...] = wide_ref[...].astype(o_ref.dtype)

    return pl.pallas_call(
        body, name=name, grid=(nl, k // tk),
        in_specs=[pl.BlockSpec((nd, None, tk, n), lambda l, i: (0, l, i, 0))],
        out_specs=pl.BlockSpec((None, tk, nd * n), lambda l, i: (l, i, 0)),
        out_shape=jax.ShapeDtypeStruct((nl, k, nd * n), BF16),
        scratch_shapes=[pltpu.VMEM((tk, nd * n), F32)],
    )(blk)


def _blocks_from_cols(full, name):
    k, n8 = full.shape
    n = n8 // N_DEV
    tk = _pick(k, 256, 16)

    def body(f_ref, o_ref):
        for dev in range(N_DEV):
            o_ref[dev] = f_ref[:, dev * n:(dev + 1) * n].astype(o_ref.dtype)

    return pl.pallas_call(
        body, name=name, grid=(k // tk,),
        in_specs=[pl.BlockSpec((tk, n8), lambda i: (i, 0))],
        out_specs=pl.BlockSpec((N_DEV, tk, n), lambda i: (0, i, 0)),
        out_shape=jax.ShapeDtypeStruct((N_DEV, k, n), BF16),
    )(full)


def _pack(arrs, dtype, cols):
    flat = jnp.concatenate([a.astype(dtype).reshape(-1) for a in arrs])
    unit = 16 * cols
    pad = (-flat.shape[0]) % unit
    flat = jnp.pad(flat, (0, pad))
    return flat.reshape(-1, cols)


def _unpack(flat, shapes):
    out, off = [], 0
    for s in shapes:
        n = math.prod(s)
        out.append(flat[off:off + n].reshape(s))
        off += n
    return out


def _ada_fwd(c_all, w_ada, b_shard, name):
    nl, d, n = w_ada.shape
    tn = _pick(n, 512, LANES)

    def body(c_ref, w_ref, b_ref, o_ref, act_ref):
        cv = c_ref[...]
        act = cv * jax.nn.sigmoid(cv)
        act_ref[...] = act
        o_ref[...] = jnp.dot(act.astype(BF16), w_ref[...].astype(BF16), preferred_element_type=F32) + b_ref[...]

    return pl.pallas_call(
        body, name=name, grid=(nl, n // tn),
        in_specs=[pl.BlockSpec(c_all.shape, lambda l, j: (0, 0)),
                  pl.BlockSpec((None, d, tn), lambda l, j: (l, 0, j)),
                  pl.BlockSpec((None, 1, tn), lambda l, j: (l, 0, j))],
        out_specs=[pl.BlockSpec((None, c_all.shape[0], tn), lambda l, j: (l, 0, j)),
                   pl.BlockSpec(c_all.shape, lambda l, j: (0, 0))],
        out_shape=[jax.ShapeDtypeStruct((nl, c_all.shape[0], n), F32), jax.ShapeDtypeStruct(c_all.shape, F32)],
        compiler_params=_params(2 * d * tn * 4 + d * tn * 2 + (8 << 20)),
    )(c_all, w_ada, b_shard)


def _ada_wgrad(act_t, dada, name):
    d, kp = act_t.shape
    nl, _, n = dada.shape
    tm = _pick(d, 512, 8)

    def body(a_ref, g_ref, o_ref):
        o_ref[...] = jnp.dot(a_ref[...].astype(BF16), g_ref[...].astype(BF16), preferred_element_type=F32)

    return pl.pallas_call(
        body, name=name, grid=(nl, d // tm),
        in_specs=[pl.BlockSpec((tm, kp), lambda l, i: (i, 0)), pl.BlockSpec((None, kp, n), lambda l, i: (l, 0, 0))],
        out_specs=pl.BlockSpec((None, tm, n), lambda l, i: (l, i, 0)),
        out_shape=jax.ShapeDtypeStruct((nl, d, n), F32),
        compiler_params=_params(4 * tm * n * 4 + 2 * kp * n * 4 + (8 << 20)),
    )(act_t, dada)


def _row_spec(tm, d):
    return pl.BlockSpec((tm, d), lambda i: (i, 0))


def _row_params(tm, d):
    return _params(11 * tm * d * 4 + (4 << 20))


def _vec_spec(d):
    return pl.BlockSpec((1, d), lambda i: (0, 0))


def _modnorm_fwd(x, g, sc, sh, name):
    s, d = x.shape
    tm = _pick(s, 512, 16)

    def body(x_ref, g_ref, sc_ref, sh_ref, o_ref):
        xv = x_ref[...]
        o_ref[...] = ((xv * _rstd(xv)) * g_ref[...] * (1.0 + sc_ref[...]) + sh_ref[...]).astype(o_ref.dtype)

    return pl.pallas_call(
        body, name=name, grid=(s // tm,),
        in_specs=[_row_spec(tm, d), _vec_spec(d), _vec_spec(d), _vec_spec(d)], out_specs=_row_spec(tm, d),
        out_shape=jax.ShapeDtypeStruct((s, d), BF16), compiler_params=_row_params(tm, d),
    )(x, g, sc, sh)


def _modnorm_bwd(dh, x, g, sc, dres, name):
    s, d = x.shape
    tm = _pick(s, 256, 8)

    def body(dh_ref, x_ref, g_ref, sc_ref, dres_ref, dx_ref, dg_ref, dsc_ref, dsh_ref):
        @pl.when(pl.program_id(0) == 0)
        def _():
            dg_ref[...] = jnp.zeros_like(dg_ref)
            dsc_ref[...] = jnp.zeros_like(dsc_ref)
            dsh_ref[...] = jnp.zeros_like(dsh_ref)

        dh_, xv, gv = dh_ref[...], x_ref[...], g_ref[...]
        r = _rstd(xv)
        xhat = xv * r
        dn = dh_ * (1.0 + sc_ref[...])
        dsh_ref[...] += jnp.sum(dh_, axis=0, keepdims=True)
        dsc_ref[...] += jnp.sum(dh_ * (xhat * gv), axis=0, keepdims=True)
        dg_ref[...] += jnp.sum(dn * xhat, axis=0, keepdims=True)
        dx_ref[...] = _norm_bwd(dn * gv, xhat, r) + dres_ref[...]

    vec = jax.ShapeDtypeStruct((1, d), F32)
    return pl.pallas_call(
        body, name=name, grid=(s // tm,),
        in_specs=[_row_spec(tm, d), _row_spec(tm, d), _vec_spec(d), _vec_spec(d), _row_spec(tm, d)],
        out_specs=[_row_spec(tm, d), _vec_spec(d), _vec_spec(d), _vec_spec(d)],
        out_shape=[jax.ShapeDtypeStruct((s, d), F32), vec, vec, vec],
    )(dh, x, g, sc, dres)


def _resnorm_fwd(x, y, g, gt, name):
    s, d = x.shape
    tm = _pick(s, 512, 8)

    def body(x_ref, y_ref, g_ref, gt_ref, o_ref):
        yv = y_ref[...]
        o_ref[...] = x_ref[...] + (1.0 + gt_ref[...]) * ((yv * _rstd(yv)) * g_ref[...])

    return pl.pallas_call(
        body, name=name, grid=(s // tm,),
        in_specs=[_row_spec(tm, d), _row_spec(tm, d), _vec_spec(d), _vec_spec(d)], out_specs=_row_spec(tm, d),
        out_shape=jax.ShapeDtypeStruct((s, d), F32), compiler_params=_row_params(tm, d),
    )(x, y, g, gt)


def _resnorm_bwd(dxo, y, g, gt, name):
    s, d = y.shape
    tm = _pick(s, 512, 16)

    def body(dxo_ref, y_ref, g_ref, gt_ref, dy_ref, dg_ref, dgt_ref):
        @pl.when(pl.program_id(0) == 0)
        def _():
            dg_ref[...] = jnp.zeros_like(dg_ref)
            dgt_ref[...] = jnp.zeros_like(dgt_ref)

        dxo_, yv, gv = dxo_ref[...], y_ref[...], g_ref[...]
        r = _rstd(yv)
        yhat = yv * r
        dn = dxo_ * (1.0 + gt_ref[...])
        dgt_ref[...] += jnp.sum(dxo_ * (yhat * gv), axis=0, keepdims=True)
        dg_ref[...] += jnp.sum(dn * yhat, axis=0, keepdims=True)
        dy_ref[...] = _norm_bwd(dn * gv, yhat, r).astype(dy_ref.dtype)

    vec = jax.ShapeDtypeStruct((1, d), F32)
    return pl.pallas_call(
        body, name=name, grid=(s // tm,),
        in_specs=[_row_spec(tm, d), _row_spec(tm, d), _vec_spec(d), _vec_spec(d)],
        out_specs=[_row_spec(tm, d), _vec_spec(d), _vec_spec(d)],
        out_shape=[jax.ShapeDtypeStruct((s, d), BF16), vec, vec], compiler_params=_row_params(tm, d),
    )(dxo, y, g, gt)


def _loss_bwd(xf, tgt, name):
    s, d = xf.shape
    tm = _pick(s, 512, 8)

    def body(x_ref, t_ref, dy_ref, l_ref):
        @pl.when(pl.program_id(0) == 0)
        def _():
            l_ref[...] = jnp.zeros_like(l_ref)

        e = x_ref[...] - t_ref[...]
        dy_ref[...] = e * (1.0 / d)
        l_ref[...] += jnp.sum(e * e) * (0.5 / d)

    return pl.pallas_call(
        body, name=name, grid=(s // tm,),
        in_specs=[_row_spec(tm, d), _row_spec(tm, d)],
        out_specs=[_row_spec(tm, d), pl.BlockSpec((8, LANES), lambda i: (0, 0))],
        out_shape=[jax.ShapeDtypeStruct((s, d), F32), jax.ShapeDtypeStruct((8, LANES), F32)],
        compiler_params=_row_params(tm, d),
    )(xf, tgt)


def _attn_specs(n_q, n_kv):
    aw, kvd = n_q * HEAD_DIM, n_kv * HEAD_DIM
    assert aw % kvd == 0
    kcol = aw // kvd
    q = pl.BlockSpec((WINDOW, aw), lambda n: (n, 0))
    kc = pl.BlockSpec((WINDOW, kvd), lambda n: (n, kcol))
    kp = pl.BlockSpec((WINDOW, kvd), lambda n: (jnp.maximum(n - 1, 0), kcol))
    vc = pl.BlockSpec((WINDOW, kvd), lambda n: (n, kcol + 1))
    vp = pl.BlockSpec((WINDOW, kvd), lambda n: (jnp.maximum(n - 1, 0), kcol + 1))
    return [q, kc, kp, vc, vp]


def _band_mask(n, n_heads):
    qi = lax.broadcasted_iota(jnp.int32, (n_heads * WINDOW, 2 * WINDOW), 0) & (WINDOW - 1)
    kj = lax.broadcasted_iota(jnp.int32, (n_heads * WINDOW, 2 * WINDOW), 1)
    return (kj > qi) & (kj <= qi + WINDOW) & ((kj >= WINDOW) | (n > 0))


def _stack_heads(ref, heads):
    return jnp.concatenate([ref[:, h * HEAD_DIM:(h + 1) * HEAD_DIM] for h in heads], axis=0)


def _stack_sinks(ref, heads):
    return jnp.concatenate([jnp.broadcast_to(ref[:, h:h + 1], (WINDOW, 1)) for h in heads], axis=0)


_NT = (((1,), (1,)), ((), ()))
_TN = (((0,), (0,)), ((), ()))


def _attn_fwd(proj, sinks, *, n_q, n_kv, name):
    s = proj.shape[0]
    aw, grp = n_q * HEAD_DIM, n_q // n_kv

    def body(q_ref, kc_ref, kp_ref, vc_ref, vp_ref, sink_ref, o_ref, lse_ref):
        valid = _band_mask(pl.program_id(0), grp)
        kb = jnp.concatenate([kp_ref[...], kc_ref[...]], axis=0).astype(BF16)
        vb = jnp.concatenate([vp_ref[...], vc_ref[...]], axis=0).astype(BF16)
        lse_ref[...] = jnp.zeros_like(lse_ref)
        for g in range(n_kv):
            heads = range(g * grp, (g + 1) * grp)
            gs = slice(g * HEAD_DIM, (g + 1) * HEAD_DIM)
            qg = _stack_heads(q_ref, heads).astype(BF16)
            sink = _stack_sinks(sink_ref, heads)
            sc = lax.dot_general(qg, kb[:, gs], _NT, preferred_element_type=F32)
            sc = jnp.where(valid, sc * (HEAD_DIM ** -0.5), NEG)
            m = jnp.maximum(jnp.max(sc, axis=-1, keepdims=True), sink)
            e = jnp.exp(sc - m)
            den = jnp.sum(e, axis=-1, keepdims=True) + jnp.exp(sink - m)
            p = e * (1.0 / den)
            og = jnp.dot(p.astype(BF16), vb[:, gs], preferred_element_type=F32)
            lse = m + jnp.log(den)
            for i, h in enumerate(heads):
                rows = slice(i * WINDOW, (i + 1) * WINDOW)
                o_ref[:, h * HEAD_DIM:(h + 1) * HEAD_DIM] = og[rows]
                lse_ref[:, h:h + 1] = lse[rows]

    return pl.pallas_call(
        body, name=name, grid=(s // WINDOW,),
        in_specs=_attn_specs(n_q, n_kv) + [pl.BlockSpec((1, LANES), lambda n: (0, 0))],
        out_specs=[pl.BlockSpec((WINDOW, aw), lambda n: (n, 0)), pl.BlockSpec((WINDOW, LANES), lambda n: (n, 0))],
        out_shape=[jax.ShapeDtypeStruct((s, aw), F32), jax.ShapeDtypeStruct((s, LANES), F32)],
    )(proj, proj, proj, proj, proj, sinks)


def _attn_bwd(proj, sinks, out, lse, dout, *, n_q, n_kv, name):
    s = proj.shape[0]
    aw, kvd, grp = n_q * HEAD_DIM, n_kv * HEAD_DIM, n_q // n_kv
    scale = HEAD_DIM ** -0.5

    def body(q_ref, kc_ref, kp_ref, vc_ref, vp_ref, sink_ref, o_ref, lse_ref, do_ref,
             dq_ref, dk_ref, dv_ref, dsink_ref):
        n = pl.program_id(0)

        @pl.when(n == 0)
        def _():
            dk_ref[...] = jnp.zeros_like(dk_ref)
            dv_ref[...] = jnp.zeros_like(dv_ref)
            dsink_ref[...] = jnp.zeros_like(dsink_ref)

        valid = _band_mask(n, grp)
        kb = jnp.concatenate([kp_ref[...], kc_ref[...]], axis=0).astype(BF16)
        vb = jnp.concatenate([vp_ref[...], vc_ref[...]], axis=0).astype(BF16)
        lane = lax.broadcasted_iota(jnp.int32, (8, LANES), 1)
        dsink = jnp.zeros((8, LANES), F32)
        cur = pl.ds(pl.multiple_of(n * WINDOW, WINDOW), WINDOW)
        prev = pl.ds(pl.multiple_of(jnp.maximum(n - 1, 0) * WINDOW, WINDOW), WINDOW)
        for g in range(n_kv):
            heads = range(g * grp, (g + 1) * grp)
            gs = slice(g * HEAD_DIM, (g + 1) * HEAD_DIM)
            qg = _stack_heads(q_ref, heads).astype(BF16)
            do = _stack_heads(do_ref, heads)
            dob = do.astype(BF16)
            lse = jnp.concatenate([lse_ref[:, h:h + 1] for h in heads], axis=0)
            sc = lax.dot_general(qg, kb[:, gs], _NT, preferred_element_type=F32)
            sc = jnp.where(valid, sc * scale, NEG)
            p = jnp.exp(sc - lse)
            delta = jnp.sum(do * _stack_heads(o_ref, heads), axis=-1, keepdims=True)
            dp = lax.dot_general(dob, vb[:, gs], _NT, preferred_element_type=F32)
            ds = (p * (dp - delta) * scale).astype(BF16)
            dqg = jnp.dot(ds, kb[:, gs], preferred_element_type=F32)
            dkb = lax.dot_general(ds, qg, _TN, preferred_element_type=F32)
            dvb = lax.dot_general(p.astype(BF16), dob, _TN, preferred_element_type=F32)
            sink_term = jnp.exp(_stack_sinks(sink_ref, heads) - lse) * delta
            for i, h in enumerate(heads):
                rows = slice(i * WINDOW, (i + 1) * WINDOW)
                dq_ref[:, h * HEAD_DIM:(h + 1) * HEAD_DIM] = dqg[rows]
                dsink = dsink + jnp.where(lane == h, -jnp.sum(sink_term[rows]), 0.0)
            dk_ref[cur, gs] += dkb[WINDOW:]
            dv_ref[cur, gs] += dvb[WINDOW:]

            @pl.when(n > 0)
            def _():
                dk_ref[prev, gs] += dkb[:WINDOW]
                dv_ref[prev, gs] += dvb[:WINDOW]

        dsink_ref[...] += dsink

    blk = pl.BlockSpec((WINDOW, aw), lambda n: (n, 0))
    kv_full = pl.BlockSpec((s, kvd), lambda n: (0, 0))
    return pl.pallas_call(
        body, name=name, grid=(s // WINDOW,),
        in_specs=_attn_specs(n_q, n_kv) + [pl.BlockSpec((1, LANES), lambda n: (0, 0)), blk,
                                           pl.BlockSpec((WINDOW, LANES), lambda n: (n, 0)), blk],
        out_specs=[blk, kv_full, kv_full, pl.BlockSpec((8, LANES), lambda n: (0, 0))],
        out_shape=[jax.ShapeDtypeStruct((s, aw), F32), jax.ShapeDtypeStruct((s, kvd), F32),
                   jax.ShapeDtypeStruct((s, kvd), F32), jax.ShapeDtypeStruct((8, LANES), F32)],
    )(proj, proj, proj, proj, proj, sinks, out, lse, dout)


def _disc(lr, li, ls):
    dt = jnp.exp(ls)
    mag = jnp.exp(lr * dt)
    ang = li * dt
    ab_re, ab_im = mag * jnp.cos(ang), mag * jnp.sin(ang)
    den = lr * lr + li * li
    f_re = ((ab_re - 1.0) * lr + ab_im * li) / den
    f_im = (ab_im * lr - (ab_re - 1.0) * li) / den
    return ab_re, ab_im, f_re, f_im


POW_ROWS = 8
SUB = 8
TAB_ROWS = POW_ROWS + 2 * SUB


def _ssm_params_fwd(lr, li, ls, b_re, b_im, name):
    gp = lr.shape[1]
    h = b_re.shape[0]

    def body(lr_ref, li_ref, ls_ref, br_ref, bi_ref, bbr_ref, bbi_ref, tr_ref, ti_ref):
        ab_re, ab_im, f_re, f_im = _disc(lr_ref[...], li_ref[...], ls_ref[...])
        br, bi = br_ref[...], bi_ref[...]
        bbr_ref[...] = f_re * br - f_im * bi
        bbi_ref[...] = f_re * bi + f_im * br
        pr, pi = ab_re, ab_im
        for i in range(POW_ROWS):
            tr_ref[i:i + 1, :] = pr
            ti_ref[i:i + 1, :] = pi
            pr, pi = pr * pr - pi * pi, 2.0 * pr * pi
        pr, pi = ab_re, ab_im
        for r in range(SUB):
            for row in (POW_ROWS + r, POW_ROWS + 2 * SUB - 1 - r):
                tr_ref[row:row + 1, :] = pr
                ti_ref[row:row + 1, :] = pi
            pr, pi = pr * ab_re - pi * ab_im, pr * ab_im + pi * ab_re

    mat, tab = jax.ShapeDtypeStruct((h, gp), F32), jax.ShapeDtypeStruct((TAB_ROWS, gp), F32)
    return pl.pallas_call(body, name=name, out_shape=[mat, mat, tab, tab])(lr, li, ls, b_re, b_im)


def _ssm_params_bwd(lr, li, ls, b_re, b_im, dab_re, dab_im, dbb_re, dbb_im, seg, name):
    gp = lr.shape[1]
    h = b_re.shape[0]

    def body(lr_ref, li_ref, ls_ref, br_ref, bi_ref, dar_ref, dai_ref, dbbr_ref, dbbi_ref, seg_ref,
             dlr_ref, dli_ref, dls_ref, dbr_ref, dbi_ref):
        lr_, li_, ls_ = lr_ref[...], li_ref[...], ls_ref[...]
        (ab_re, ab_im, f_re, f_im), vjp = jax.vjp(_disc, lr_, li_, ls_)
        br, bi, dbbr, dbbi = br_ref[...], bi_ref[...], dbbr_ref[...], dbbi_ref[...]
        dbr_ref[...] = dbbr * f_re + dbbi * f_im
        dbi_ref[...] = dbbi * f_re - dbbr * f_im
        df_re = jnp.sum(dbbr * br + dbbi * bi, axis=0, keepdims=True)
        df_im = jnp.sum(dbbi * br - dbbr * bi, axis=0, keepdims=True)
        dlr, dli, dls = vjp((dar_ref[...], dai_ref[...], df_re, df_im))
        dlr_ref[...] = dlr
        dli_ref[...] = dli
        dls8 = jnp.broadcast_to(dls, (8, gp))
        dls_ref[...] = jnp.dot(dls8, seg_ref[...], preferred_element_type=F32, precision=lax.Precision.HIGHEST)

    vec, mat = jax.ShapeDtypeStruct((1, gp), F32), jax.ShapeDtypeStruct((h, gp), F32)
    return pl.pallas_call(body, name=name,
                          out_shape=[vec, vec, jax.ShapeDtypeStruct((8, seg.shape[1]), F32), mat, mat],
                          compiler_params=_params(24 << 20))(
        lr, li, ls, b_re, b_im, dab_re, dab_im, dbb_re, dbb_im, seg)


def _scan_bufs(t_len):
    hs = BLOCK_STATES
    return [pltpu.VMEM((hs // LANES, t_len, LANES), F32), pltpu.VMEM((hs // LANES, t_len, LANES), F32),
            pltpu.VMEM((t_len // SUB, hs), F32), pltpu.VMEM((t_len // SUB, hs), F32)]


def _scan(xr, xi, apow_ref, bufs, t_len, reverse):
    hs = BLOCK_STATES
    n_tiles = t_len // SUB
    sr_ref, si_ref, er_ref, ei_ref = bufs

    def doubling(xr, xi, n_rows, first_pow, within):
        row = lax.broadcasted_iota(jnp.int32, xr.shape, 0) & (within - 1)
        d = 1
        while d < within:
            i = first_pow + d.bit_length() - 1
            pr, pi = apow_ref[i:i + 1, :hs], apow_ref[i:i + 1, hs:]
            if reverse:
                pi, shift, keep = -pi, n_rows - d, row < within - d
            else:
                shift, keep = d, row >= d
            sr = jnp.where(keep, pltpu.roll(xr, shift, 0), 0.0)
            si = jnp.where(keep, pltpu.roll(xi, shift, 0), 0.0)
            xr, xi = xr + pr * sr - pi * si, xi + pr * si + pi * sr
            d *= 2
        return xr, xi

    shape3 = (n_tiles, SUB, hs)
    row = lax.broadcasted_iota(jnp.int32, shape3, 1)
    xr, xi = xr.reshape(shape3), xi.reshape(shape3)
    for i, d in enumerate((1, 2, 4)):
        pr, pi = apow_ref[i:i + 1, :hs], apow_ref[i:i + 1, hs:]
        if reverse:
            pi, shift, keep = -pi, SUB - d, row < SUB - d
        else:
            shift, keep = d, row >= d
        sr = jnp.where(keep, pltpu.roll(xr, shift, 1), 0.0)
        si = jnp.where(keep, pltpu.roll(xi, shift, 1), 0.0)
        xr, xi = xr + pr * sr - pi * si, xi + pr * si + pi * sr
    xr, xi = xr.reshape(t_len, hs), xi.reshape(t_len, hs)
    chunks = [slice(c * LANES, (c + 1) * LANES) for c in range(hs // LANES)]
    for c, lanes in enumerate(chunks):
        sr_ref[c] = xr[:, lanes]
        si_ref[c] = xi[:, lanes]
    edge = pl.ds(0 if reverse else SUB - 1, n_tiles, stride=SUB)
    tr, ti = doubling(jnp.concatenate([sr_ref[c, edge, :] for c in range(len(chunks))], axis=1),
                      jnp.concatenate([si_ref[c, edge, :] for c in range(len(chunks))], axis=1), n_tiles, 3, n_tiles)
    trow = lax.broadcasted_iota(jnp.int32, tr.shape, 0)
    if reverse:
        shift, keep = n_tiles - 1, trow < n_tiles - 1
    else:
        shift, keep = 1, trow >= 1
    er_ref[...] = jnp.where(keep, pltpu.roll(tr, shift, 0), 0.0)
    ei_ref[...] = jnp.where(keep, pltpu.roll(ti, shift, 0), 0.0)
    lin = POW_ROWS + SUB if reverse else POW_ROWS
    mr, mi = apow_ref[lin:lin + SUB, :hs], apow_ref[lin:lin + SUB, hs:]
    if reverse:
        mi = -mi
    for t in range(n_tiles):
        rows = slice(t * SUB, (t + 1) * SUB)
        er, ei = er_ref[t:t + 1, :], ei_ref[t:t + 1, :]
        add_r, add_i = mr * er - mi * ei, mr * ei + mi * er
        for c, lanes in enumerate(chunks):
            sr_ref[c, rows, :] += add_r[:, lanes]
            si_ref[c, rows, :] += add_i[:, lanes]
    return (jnp.concatenate([sr_ref[c] for c in range(len(chunks))], axis=1),
            jnp.concatenate([si_ref[c] for c in range(len(chunks))], axis=1))


def _ssm_chunk(s):
    t_len = _pick(s, 256, 8)
    assert t_len & (t_len - 1) == 0 and t_len <= 1 << POW_ROWS, t_len
    return t_len


def _fold_carry(br, bi, carry_ref, apow_ref, at_row, conj):
    hs = BLOCK_STATES
    cr, ci = carry_ref[0:1, :hs], carry_ref[0:1, hs:]
    ar, ai = apow_ref[0:1, :hs], apow_ref[0:1, hs:]
    if conj:
        ai = -ai
    here = lax.broadcasted_iota(jnp.int32, br.shape, 0) == at_row
    return jnp.where(here, br + (ar * cr - ai * ci), br), jnp.where(here, bi + (ar * ci + ai * cr), bi)


def _ssm_fwd(proj, ucol, bbd, ccat, dskip, apow, t_len, *, name):
    s = proj.shape[0]
    nb = bbd.shape[0]
    nc = s // t_len
    hs = BLOCK_STATES

    def body(u_ref, bbd_ref, ccat_ref, d_ref, apow_ref, y_ref, z_ref, xs_ref, carry_ref, *bufs):
        @pl.when(pl.program_id(1) == 0)
        def _():
            carry_ref[...] = jnp.zeros_like(carry_ref)

        xs_ref[...] = carry_ref[...]
        u = u_ref[...]
        bu = jnp.dot(u.astype(BF16), bbd_ref[...], preferred_element_type=F32)
        br, bi = _fold_carry(bu[:, :hs], bu[:, hs:], carry_ref, apow_ref, 0, False)
        xr, xi = _scan(br, bi, apow_ref, bufs, t_len, False)
        xcat = jnp.concatenate([xr, xi], axis=1)
        carry_ref[...] = jnp.broadcast_to(xcat[t_len - 1:t_len, :], carry_ref.shape)
        y = jnp.dot(xcat.astype(BF16), ccat_ref[...], preferred_element_type=F32) + d_ref[...] * u
        y_ref[...] = y
        z_ref[...] = _gelu(y).astype(z_ref.dtype)

    return pl.pallas_call(
        body, name=name, grid=(nb, nc),
        in_specs=[pl.BlockSpec((t_len, LANES), lambda j, n: (n, ucol + j)),
                  pl.BlockSpec((None, LANES, 2 * hs), lambda j, n: (j, 0, 0)),
                  pl.BlockSpec((None, 2 * hs, LANES), lambda j, n: (j, 0, 0)),
                  pl.BlockSpec((1, LANES), lambda j, n: (0, j)),
                  pl.BlockSpec((None, TAB_ROWS, 2 * hs), lambda j, n: (j, 0, 0))],
        out_specs=[pl.BlockSpec((t_len, LANES), lambda j, n: (n, j)),
                   pl.BlockSpec((t_len, LANES), lambda j, n: (n, j)),
                   pl.BlockSpec((None, None, 8, 2 * hs), lambda j, n: (j, n, 0, 0))],
        out_shape=[jax.ShapeDtypeStruct((s, nb * LANES), F32), jax.ShapeDtypeStruct((s, nb * LANES), BF16),
                   jax.ShapeDtypeStruct((nb, nc, 8, 2 * hs), F32)],
        scratch_shapes=[pltpu.VMEM((8, 2 * hs), F32)] + _scan_bufs(t_len),
        compiler_params=_params(40 << 20),
    )(proj, bbd, ccat, dskip, apow)


def _ssm_bwd(proj, ucol, y, dzd, dz2, xs, bbd, ccat, dskip, apow, t_len, *, name):
    s = proj.shape[0]
    nb = bbd.shape[0]
    nc = s // t_len
    hs = BLOCK_STATES

    def body(u_ref, y_ref, dzd_ref, dz2_ref, xs_ref, bbd_ref, ccat_ref, d_ref, apow_ref,
             du_ref, dbbd_ref, dccat_ref, dd_ref, da_ref, gcarry_ref, *bufs):
        @pl.when(pl.program_id(1) == 0)
        def _():
            gcarry_ref[...] = jnp.zeros_like(gcarry_ref)
            dbbd_ref[...] = jnp.zeros_like(dbbd_ref)
            dccat_ref[...] = jnp.zeros_like(dccat_ref)
            dd_ref[...] = jnp.zeros_like(dd_ref)
            da_ref[...] = jnp.zeros_like(da_ref)

        u = u_ref[...]
        ub = u.astype(BF16)
        dy = (dzd_ref[...] + dz2_ref[...]) * _gelu_grad(y_ref[...])
        dyb = dy.astype(BF16)
        bu = jnp.dot(ub, bbd_ref[...], preferred_element_type=F32)
        br, bi = _fold_carry(bu[:, :hs], bu[:, hs:], xs_ref, apow_ref, 0, False)
        xr, xi = _scan(br, bi, apow_ref, bufs[:4], t_len, False)
        sr, si = xs_ref[0:1, :hs], xs_ref[0:1, hs:]
        dxd = lax.dot_general(dyb, ccat_ref[...], _NT, preferred_element_type=F32)
        dr, di = _fold_carry(dxd[:, :hs], dxd[:, hs:], gcarry_ref, apow_ref, t_len - 1, True)
        gr, gi = _scan(dr, di, apow_ref, bufs[4:], t_len, True)
        gcat = jnp.concatenate([gr, gi], axis=1)
        gcarry_ref[...] = jnp.broadcast_to(gcat[0:1, :], gcarry_ref.shape)
        gb = gcat.astype(BF16)
        du_ref[...] = lax.dot_general(gb, bbd_ref[...], _NT, preferred_element_type=F32) + d_ref[...] * dy
        dbbd_ref[...] += lax.dot_general(ub, gb, _TN, preferred_element_type=F32)
        xb = jnp.concatenate([xr, xi], axis=1).astype(BF16)
        dccat_ref[...] += lax.dot_general(xb, dyb, _TN, preferred_element_type=F32)
        dd_ref[...] += jnp.sum(dy * u, axis=0, keepdims=True)
        first = lax.broadcasted_iota(jnp.int32, xr.shape, 0) == 0
        xpr = jnp.where(first, sr, pltpu.roll(xr, 1, 0))
        xpi = jnp.where(first, si, pltpu.roll(xi, 1, 0))
        dar = jnp.sum(gr * xpr + gi * xpi, axis=0, keepdims=True)
        dai = jnp.sum(gi * xpr - gr * xpi, axis=0, keepdims=True)
        da_ref[...] += jnp.concatenate([dar, dai], axis=1)

    def rows(j, n):
        return nc - 1 - n

    chunk = pl.BlockSpec((t_len, LANES), lambda j, n: (rows(j, n), j))
    return pl.pallas_call(
        body, name=name, grid=(nb, nc),
        in_specs=[pl.BlockSpec((t_len, LANES), lambda j, n: (rows(j, n), ucol + j)), chunk, chunk, chunk,
                  pl.BlockSpec((None, None, 8, 2 * hs), lambda j, n: (j, rows(j, n), 0, 0)),
                  pl.BlockSpec((None, LANES, 2 * hs), lambda j, n: (j, 0, 0)),
                  pl.BlockSpec((None, 2 * hs, LANES), lambda j, n: (j, 0, 0)),
                  pl.BlockSpec((1, LANES), lambda j, n: (0, j)),
                  pl.BlockSpec((None, TAB_ROWS, 2 * hs), lambda j, n: (j, 0, 0))],
        out_specs=[chunk,
                   pl.BlockSpec((None, LANES, 2 * hs), lambda j, n: (j, 0, 0)),
                   pl.BlockSpec((None, 2 * hs, LANES), lambda j, n: (j, 0, 0)),
                   pl.BlockSpec((1, LANES), lambda j, n: (0, j)),
                   pl.BlockSpec((None, 1, 2 * hs), lambda j, n: (j, 0, 0))],
        out_shape=[jax.ShapeDtypeStruct((s, nb * LANES), F32),
                   jax.ShapeDtypeStruct((nb, LANES, 2 * hs), F32),
                   jax.ShapeDtypeStruct((nb, 2 * hs, LANES), F32),
                   jax.ShapeDtypeStruct((1, nb * LANES), F32),
                   jax.ShapeDtypeStruct((nb, 1, 2 * hs), F32)],
        scratch_shapes=[pltpu.VMEM((8, 2 * hs), F32)] + _scan_bufs(t_len) + _scan_bufs(t_len),
        compiler_params=_params(48 << 20),
    )(proj, y, dzd, dz2, xs, bbd, ccat, dskip, apow)


def _to_blocks(a):
    g, p, k = a.shape
    nb = g // GROUPS_PER_BLOCK
    eye = jnp.eye(GROUPS_PER_BLOCK, dtype=a.dtype)
    a4 = a.reshape(nb, GROUPS_PER_BLOCK, p, k)
    out = jnp.einsum("ab,jbpk->jakbp", eye, a4)
    return out.reshape(nb, GROUPS_PER_BLOCK * k, GROUPS_PER_BLOCK * p)


def _from_blocks(d, p, k):
    nb = d.shape[0]
    d5 = d.reshape(nb, GROUPS_PER_BLOCK, k, GROUPS_PER_BLOCK, p)
    eye = jnp.eye(GROUPS_PER_BLOCK, dtype=bool)[None, :, None, :, None]
    diag = jnp.sum(jnp.where(eye, d5, 0.0), axis=1)
    return jnp.transpose(diag, (0, 2, 3, 1)).reshape(nb * GROUPS_PER_BLOCK, p, k)


def _merge_fwd(attn, y, gl, g_a, g_s, name):
    s, wa = attn.shape
    ws = y.shape[1]
    tm = _pick(s, 256, 16)

    def body(a_ref, y_ref, gl_ref, ga_ref, gs_ref, o_ref):
        av = a_ref[...]
        o_ref[:, :wa] = ((av * _rstd(av)) * ga_ref[...]).astype(o_ref.dtype)
        sv = _gelu(y_ref[...]) * jax.nn.sigmoid(gl_ref[...])
        o_ref[:, wa:] = ((sv * _rstd(sv)) * gs_ref[...]).astype(o_ref.dtype)

    return pl.pallas_call(
        body, name=name, grid=(s // tm,),
        in_specs=[_row_spec(tm, wa), _row_spec(tm, ws), _row_spec(tm, ws), _vec_spec(wa), _vec_spec(ws)],
        out_specs=_row_spec(tm, wa + ws), out_shape=jax.ShapeDtypeStruct((s, wa + ws), BF16),
    )(attn, y, gl, g_a, g_s)


def _merge_bwd(dmerged, attn, y, gl, g_a, g_s, name):
    s, wa = attn.shape
    ws = y.shape[1]
    tm = _pick(s, 256, 16)

    def body(dm_ref, a_ref, y_ref, gl_ref, ga_ref, gs_ref, da_ref, dgl_ref, dzd_ref, dga_ref, dgs_ref):
        @pl.when(pl.program_id(0) == 0)
        def _():
            dga_ref[...] = jnp.zeros_like(dga_ref)
            dgs_ref[...] = jnp.zeros_like(dgs_ref)

        dan, dsn = dm_ref[:, :wa], dm_ref[:, wa:]
        av = a_ref[...]
        ra = _rstd(av)
        ahat = av * ra
        dga_ref[...] += jnp.sum(dan * ahat, axis=0, keepdims=True)
        da_ref[...] = _norm_bwd(dan * ga_ref[...], ahat, ra)
        z = _gelu(y_ref[...])
        sig = jax.nn.sigmoid(gl_ref[...])
        sv = z * sig
        rs = _rstd(sv)
        shat = sv * rs
        dgs_ref[...] += jnp.sum(dsn * shat, axis=0, keepdims=True)
        dssm = _norm_bwd(dsn * gs_ref[...], shat, rs)
        dzd_ref[...] = dssm * sig
        dgl_ref[...] = (dssm * z * sig * (1.0 - sig)).astype(dgl_ref.dtype)

    return pl.pallas_call(
        body, name=name, grid=(s // tm,),
        in_specs=[_row_spec(tm, wa + ws), _row_spec(tm, wa), _row_spec(tm, ws), _row_spec(tm, ws),
                  _vec_spec(wa), _vec_spec(ws)],
        out_specs=[_row_spec(tm, wa), _row_spec(tm, ws), _row_spec(tm, ws), _vec_spec(wa), _vec_spec(ws)],
        out_shape=[jax.ShapeDtypeStruct((s, wa), F32), jax.ShapeDtypeStruct((s, ws), BF16),
                   jax.ShapeDtypeStruct((s, ws), F32), jax.ShapeDtypeStruct((1, wa), F32),
                   jax.ShapeDtypeStruct((1, ws), F32)],
    )(dmerged, attn, y, gl, g_a, g_s)


def _shift_down(main, halo, k):
    rolled = pltpu.roll(main, k, 0)
    row = lax.broadcasted_iota(jnp.int32, main.shape, 0)
    for r in range(k):
        rolled = jnp.where(row == r, halo[8 - k + r:8 - k + r + 1, :], rolled)
    return rolled


def _shift_up(main, halo, k):
    tm = main.shape[0]
    rolled = pltpu.roll(main, tm - k, 0)
    row = lax.broadcasted_iota(jnp.int32, main.shape, 0)
    for r in range(k):
        rolled = jnp.where(row == tm - k + r, halo[r:r + 1, :], rolled)
    return rolled


def _conv(main, halo, w_ref, b_ref):
    return (b_ref[...] + w_ref[0:1, :] * _shift_down(main, halo, 2) + w_ref[1:2, :] * _shift_down(main, halo, 1)
            + w_ref[2:3, :] * main)


def _gate_tiles(s, f):
    return _pick(s, 512, 16), _pick(f, 512, LANES)


def _gate_in_specs(tm, tn, nfb, order, last_row_tile=None):
    hb = tm // 8
    if order == "ij":
        ij = lambda a, b: (a, b)
    elif last_row_tile is None:
        ij = lambda a, b: (b, a)
    else:
        ij = lambda a, b: (last_row_tile - b, a)

    def main(off):
        return pl.BlockSpec((tm, tn), lambda a, b: (ij(a, b)[0], ij(a, b)[1] + off))

    def halo(off):
        return pl.BlockSpec((8, tn), lambda a, b: (jnp.maximum(ij(a, b)[0] * hb - 1, 0), ij(a, b)[1] + off))

    def vec(rows, off):
        return pl.BlockSpec((rows, tn), lambda a, b: (0, ij(a, b)[1] + off))

    return [main(0), main(nfb), halo(0), halo(nfb), vec(3, 0), vec(3, nfb), vec(1, 0), vec(1, nfb)]


def _gate_fwd(up0, conv_w, conv_b, name):
    s, f2 = up0.shape
    f = f2 // 2
    tm, tn = _gate_tiles(s, f)
    nfb = f // tn

    def body(v_ref, g_ref, vh_ref, gh_ref, wv_ref, wg_ref, bv_ref, bg_ref, o_ref):
        top = pl.program_id(0) == 0
        vh = jnp.where(top, 0.0, vh_ref[...])
        gh = jnp.where(top, 0.0, gh_ref[...])
        val = _conv(v_ref[...], vh, wv_ref, bv_ref)
        gate = _conv(g_ref[...], gh, wg_ref, bg_ref)
        o_ref[...] = (_gelu(gate) * val).astype(o_ref.dtype)

    return pl.pallas_call(
        body, name=name, grid=(s // tm, nfb),
        in_specs=_gate_in_specs(tm, tn, nfb, "ij"), out_specs=pl.BlockSpec((tm, tn), lambda i, j: (i, j)),
        out_shape=jax.ShapeDtypeStruct((s, f), BF16),
        compiler_params=_params(24 * tm * tn * 4 + (4 << 20)),
    )(up0, up0, up0, up0, conv_w, conv_w, conv_b, conv_b)


def _gate_bwd(up0, conv_w, conv_b, da, name):
    s, f2 = up0.shape
    f = f2 // 2
    tm, tn = _gate_tiles(s, f)
    nfb, ni = f // tn, s // tm

    def body(v_ref, g_ref, vh_ref, gh_ref, wv_ref, wg_ref, bv_ref, bg_ref, da_ref, dup0_ref, dcb_ref, dcw_ref,
             below_ref):
        step = pl.program_id(1)
        top = step == ni - 1

        @pl.when(step == 0)
        def _():
            dcb_ref[...] = jnp.zeros_like(dcb_ref)
            dcw_ref[...] = jnp.zeros_like(dcw_ref)
            below_ref[...] = jnp.zeros_like(below_ref)

        halos = (jnp.where(top, 0.0, vh_ref[...]), jnp.where(top, 0.0, gh_ref[...]))
        mains = (v_ref[...], g_ref[...])
        w_refs = (wv_ref, wg_ref)
        val = _conv(mains[0], halos[0], wv_ref, bv_ref)
        gate = _conv(mains[1], halos[1], wg_ref, bg_ref)
        dav = da_ref[...]
        act, act_grad = _gelu_and_grad(gate)
        dups = (dav * act, (dav * val) * act_grad)
        for half in range(2):
            dup, w_ref = dups[half], w_refs[half]
            below = below_ref[half]
            dup0_ref[half] = (w_ref[2:3, :] * dup + w_ref[1:2, :] * _shift_up(dup, below, 1)
                              + w_ref[0:1, :] * _shift_up(dup, below, 2)).astype(dup0_ref.dtype)
            below_ref[half] = dup[0:8, :]
            dcb_ref[half] += jnp.sum(dup, axis=0, keepdims=True)
            dcw_ref[half, 0:1, :] += jnp.sum(dup * _shift_down(mains[half], halos[half], 2), axis=0, keepdims=True)
            dcw_ref[half, 1:2, :] += jnp.sum(dup * _shift_down(mains[half], halos[half], 1), axis=0, keepdims=True)
            dcw_ref[half, 2:3, :] += jnp.sum(dup * mains[half], axis=0, keepdims=True)

    return pl.pallas_call(
        body, name=name, grid=(nfb, ni),
        in_specs=_gate_in_specs(tm, tn, nfb, "ji", ni - 1) + [pl.BlockSpec((tm, tn), lambda j, i: (ni - 1 - i, j))],
        out_specs=[pl.BlockSpec((2, tm, tn), lambda j, i: (0, ni - 1 - i, j)),
                   pl.BlockSpec((2, 1, tn), lambda j, i: (0, 0, j)),
                   pl.BlockSpec((2, 3, tn), lambda j, i: (0, 0, j))],
        out_shape=[jax.ShapeDtypeStruct((2, s, f), BF16), jax.ShapeDtypeStruct((2, 1, f), F32),
                   jax.ShapeDtypeStruct((2, 3, f), F32)],
        scratch_shapes=[pltpu.VMEM((2, 8, tn), F32)],
        compiler_params=_params(40 * tm * tn * 4 + (4 << 20)),
    )(up0, up0, up0, up0, conv_w, conv_w, conv_b, conv_b, da)


def _adamw(w, g, m, v, name):
    r, c = w.shape
    tr = _pick(r, max(8, (1 << 19) // max(c, 1) // 8 * 8), 8)
    c1, c2 = 1.0 / (1.0 - ADAM_B1 ** ADAM_STEP), 1.0 / (1.0 - ADAM_B2 ** ADAM_STEP)

    def body(w_ref, g_ref, m_ref, v_ref, d_ref, nm_ref, nv_ref):
        gv = g_ref[...]
        nm = ADAM_B1 * m_ref[...] + (1.0 - ADAM_B1) * gv
        nv = ADAM_B2 * v_ref[...] + (1.0 - ADAM_B2) * (gv * gv)
        nm_ref[...] = nm
        nv_ref[...] = nv
        d_ref[...] = -ADAM_LR * ((nm * c1) / (jnp.sqrt(nv * c2) + ADAM_EPS) + ADAM_WD * w_ref[...])

    spec = pl.BlockSpec((tr, c), lambda i: (i, 0))
    out = jax.ShapeDtypeStruct((r, c), F32)
    return pl.pallas_call(body, name=name, grid=(r // tr,), in_specs=[spec] * 4, out_specs=[spec] * 3,
                          out_shape=[out] * 3, compiler_params=_params(14 * tr * c * 4 + (4 << 20)))(w, g, m, v)


def _adamw_many(ws, gs, ms, vs, name):
    n = len(ws)
    c1, c2 = 1.0 / (1.0 - ADAM_B1 ** ADAM_STEP), 1.0 / (1.0 - ADAM_B2 ** ADAM_STEP)

    def body(*refs):
        w_refs, g_refs, m_refs, v_refs = (refs[i * n:(i + 1) * n] for i in range(4))
        d_refs, nm_refs, nv_refs = (refs[(4 + i) * n:(5 + i) * n] for i in range(3))
        for i in range(n):
            gv = g_refs[i][...]
            nm = ADAM_B1 * m_refs[i][...] + (1.0 - ADAM_B1) * gv
            nv = ADAM_B2 * v_refs[i][...] + (1.0 - ADAM_B2) * (gv * gv)
            nm_refs[i][...] = nm
            nv_refs[i][...] = nv
            d_refs[i][...] = -ADAM_LR * ((nm * c1) / (jnp.sqrt(nv * c2) + ADAM_EPS) + ADAM_WD * w_refs[i][...])

    shapes = [jax.ShapeDtypeStruct(w.shape, F32) for w in ws]
    outs = pl.pallas_call(body, name=name, out_shape=shapes * 3, compiler_params=_params(48 << 20))(
        *ws, *gs, *ms, *vs)
    return outs[:n], outs[n:2 * n], outs[2 * n:]


def _adamw_nd(w, g, m, v, name):
    shape = w.shape
    c = shape[-1]
    outs = _adamw(w.reshape(-1, c), g.reshape(-1, c), m.reshape(-1, c), v.reshape(-1, c), name)
    return [o.reshape(shape) for o in outs]


BIG = ("w_in", "w_glu", "w_out", "w_up", "w_down")
SMALL = ("b_ada", "g_pre_mix", "g_post_mix", "attn_sinks", "lam_re", "lam_im", "log_step", "ssm_b_re", "ssm_b_im",
         "ssm_c_re", "ssm_c_im", "ssm_d", "g_attn_out", "g_ssm_out", "g_pre_ffn", "g_post_ffn", "conv_b")
ORDER = ("w_ada", "b_ada", "g_pre_mix", "g_post_mix", "w_in", "attn_sinks", "lam_re", "lam_im", "log_step",
         "ssm_b_re", "ssm_b_im", "ssm_c_re", "ssm_c_im", "ssm_d", "w_glu", "g_attn_out", "g_ssm_out", "w_out",
         "g_pre_ffn", "g_post_ffn", "w_up", "conv_w", "conv_b", "w_down")
COL_SHARDED = ("w_in", "w_up")


def kernel(x, c, w_ada, b_ada, g_pre_mix, g_post_mix, w_in, attn_sinks, lam_re, lam_im, log_step, ssm_b_re, ssm_b_im, ssm_c_re, ssm_c_im, ssm_d, w_glu, g_attn_out, g_ssm_out, w_out, g_pre_ffn, g_post_ffn, w_up, conv_w, conv_b, w_down, loss_target, m_w_ada, m_b_ada, m_g_pre_mix, m_g_post_mix, m_w_in, m_attn_sinks, m_lam_re, m_lam_im, m_log_step, m_ssm_b_re, m_ssm_b_im, m_ssm_c_re, m_ssm_c_im, m_ssm_d, m_w_glu, m_g_attn_out, m_g_ssm_out, m_w_out, m_g_pre_ffn, m_g_post_ffn, m_w_up, m_conv_w, m_conv_b, m_w_down, v_w_ada, v_b_ada, v_g_pre_mix, v_g_post_mix, v_w_in, v_attn_sinks, v_lam_re, v_lam_im, v_log_step, v_ssm_b_re, v_ssm_b_im, v_ssm_c_re, v_ssm_c_im, v_ssm_d, v_w_glu, v_g_attn_out, v_g_ssm_out, v_w_out, v_g_pre_ffn, v_g_post_ffn, v_w_up, v_conv_w, v_conv_b, v_w_down):
    env = dict(locals())
    W = {n: env[n] for n in ORDER}
    M = {n: env["m_" + n] for n in ORDER}
    V = {n: env["v_" + n] for n in ORDER}

    depth = w_ada.shape[0]
    s, d = x.shape[1], x.shape[2]
    xs0 = x.reshape(s, d)
    tgt = loss_target.reshape(s, d)
    attn_w = d // 2
    ssm_w = d - attn_w
    in_cols = w_in.shape[2] * N_DEV
    kv_dim = (in_cols - attn_w - ssm_w) // 2
    n_q, n_kv = attn_w // HEAD_DIM, kv_dim // HEAD_DIM
    n_grp = ssm_w // SSM_GROUP
    nb = ssm_w // LANES
    f = w_down.shape[1] * N_DEV
    ucol = (attn_w + 2 * kv_dim) // LANES
    t_len = _ssm_chunk(s)
    me = 4 * lax.axis_index("x") + 2 * lax.axis_index("y") + lax.axis_index("c")

    def at_block(ref, idx):
        return ref.at[idx]

    def at_rows(n_rows):
        return lambda ref, idx: ref.at[:, pl.ds(pl.multiple_of(idx * n_rows, 8), n_rows), :]

    def at_cols(n_cols):
        return lambda ref, idx: ref.at[:, :, pl.ds(pl.multiple_of(idx * n_cols, LANES), n_cols)]

    first = _gather_multi([w_in.astype(BF16), conv_w, c],
                          [(N_DEV,) + w_in.shape, (N_DEV,) + conv_w.shape, (N_DEV,) + c.shape],
                          [at_block, at_block, at_block], "ag_first")
    w_in_full = _cols_from_blocks(first[0], "w_in_layout")
    conv_w_full = jnp.transpose(first[1], (1, 2, 0, 3)).reshape(depth, 3, 2 * f)
    c_all = first[2].reshape(N_DEV, d)

    def at_rows2(n_rows):
        return lambda ref, idx: ref.at[pl.ds(pl.multiple_of(idx * n_rows, 8), n_rows), :]

    def at_cols2(n_cols):
        return lambda ref, idx: ref.at[:, pl.ds(pl.multiple_of(idx * n_cols, LANES), n_cols)]

    def whole(ref, idx):
        return ref

    def gather_kind(n):
        return "blk" if n == "w_in" else "cols" if n in COL_SHARDED else "rows"

    def gather_view(n):
        return {"blk": at_block, "cols": at_cols2(W[n].shape[2]), "rows": at_rows2(W[n].shape[1])}[gather_kind(n)]

    def gather_shape(n):
        _, a, b = W[n].shape
        return {"blk": (N_DEV, a, b), "cols": (a, N_DEV * b), "rows": (N_DEV * a, b)}[gather_kind(n)]

    later = [(n, l) for l in range(depth) for n in BIG[1:]]
    later_srcs = [W[n][l].astype(BF16) for n, l in later]
    later_views = [gather_view(n) for n, _ in later]
    me_arr = me.astype(jnp.int32).reshape(1)
    lands = [_place_own(me_arr, src, lax.empty(gather_shape(n), BF16), gather_kind(n), f"ag_own_{n}{l}")
             for (n, l), src in zip(later, later_srcs)]
    ag_started, ag_token = _exchange_start(later_srcs, lands, [whole] * len(later), later_views, "ag_start")

    def weights_arrived(names, l, after, name):
        picks = [later.index((n, l)) for n in names]
        _, got = _exchange_wait([ag_started[i] for i in picks], [after], [whole] * len(picks),
                                [later_views[i] for i in picks], name)
        return dict(zip(names, got))

    c_pad = jnp.pad(c_all, ((0, 16 - N_DEV), (0, 0)))
    n_ada = w_ada.shape[2]
    b_shard = lax.dynamic_slice_in_dim(b_ada, me * n_ada, n_ada, axis=1).reshape(depth, 1, n_ada)
    ada_part, c_act = _ada_fwd(c_pad, w_ada, b_shard, "ada_fwd")
    ada_all = _all_gather(ada_part.reshape(depth * 16, n_ada), "ag_ada").reshape(N_DEV, depth, 16, n_ada)
    ada_me = lax.dynamic_index_in_dim(ada_all, me, axis=2, keepdims=False)
    ada = jnp.transpose(ada_me, (1, 0, 2)).reshape(depth, 6, 1, d) + ag_token[0, 0]

    gp = n_grp * STATE

    def hgp(a):
        return jnp.transpose(a, (2, 0, 1)).reshape(SSM_GROUP, gp)

    ssm = []
    for l in range(depth):
        lr, li = lam_re[l].reshape(1, gp), lam_im[l].reshape(1, gp)
        ls = jnp.repeat(log_step[l], STATE).reshape(1, gp)
        br, bi = hgp(ssm_b_re[l]), hgp(ssm_b_im[l])
        bbr, bbi, tab_r, tab_i = _ssm_params_fwd(lr, li, ls, br, bi, f"ssm_params_fwd{l}")
        bb_re = jnp.transpose(bbr.reshape(SSM_GROUP, n_grp, STATE), (1, 2, 0))
        bb_im = jnp.transpose(bbi.reshape(SSM_GROUP, n_grp, STATE), (1, 2, 0))
        bbd = jnp.concatenate([_to_blocks(bb_re), _to_blocks(bb_im)], axis=2).astype(BF16)
        c_re_t = jnp.transpose(ssm_c_re[l], (0, 2, 1))
        c_im_t = jnp.transpose(ssm_c_im[l], (0, 2, 1))
        ccat = jnp.concatenate([jnp.transpose(_to_blocks(c_re_t), (0, 2, 1)),
                                -jnp.transpose(_to_blocks(c_im_t), (0, 2, 1))], axis=1).astype(BF16)

        def tab(t):
            return t.reshape(TAB_ROWS, nb, BLOCK_STATES)

        apow = jnp.transpose(jnp.concatenate([tab(tab_r), tab(tab_i)], axis=2), (1, 0, 2))
        ssm.append(dict(lr=lr, li=li, ls=ls, br=br, bi=bi, bbd=bbd, ccat=ccat, apow=apow,
                        dskip=ssm_d[l].reshape(1, ssm_w)))

    sinks_pad = jnp.pad(attn_sinks, ((0, 0), (0, LANES - n_q)))

    def vec(a):
        return a.reshape(1, -1)

    saved = []
    fw = [dict() for _ in range(depth)]
    xin = xs0
    for l in range(depth):
        sh_m, sc_m, gt_m, sh_f, sc_f, gt_f = (ada[l, i] for i in range(6))
        p = ssm[l]
        h1 = _modnorm_fwd(xin, vec(g_pre_mix[l]), sc_m, sh_m, f"modnorm_mix_fwd{l}")
        proj = _matmul(h1, w_in_full[l], name=f"mm_in{l}")
        attn, lse = _attn_fwd(proj, sinks_pad[l:l + 1], n_q=n_q, n_kv=n_kv, name=f"attn_fwd{l}")
        y, z, xstart = _ssm_fwd(proj, ucol, p["bbd"], p["ccat"], p["dskip"], p["apow"], t_len, name=f"ssm_fwd{l}")
        fw[l].update(weights_arrived(("w_glu", "w_out"), l, z, f"ag_wait_mix{l}"))
        gl = _matmul(z, fw[l]["w_glu"], name=f"mm_glu{l}")
        merged = _merge_fwd(attn, y, gl, vec(g_attn_out[l]), vec(g_ssm_out[l]), f"merge_fwd{l}")
        mix = _matmul(merged, fw[l]["w_out"], name=f"mm_out{l}")
        x2 = _resnorm_fwd(xin, mix, vec(g_post_mix[l]), gt_m, f"resnorm_mix_fwd{l}")
        h2 = _modnorm_fwd(x2, vec(g_pre_ffn[l]), sc_f, sh_f, f"modnorm_ffn_fwd{l}")
        fw[l].update(weights_arrived(("w_up",), l, h2, f"ag_wait_up{l}"))
        up0 = _matmul(h2, fw[l]["w_up"], name=f"mm_up{l}")
        cw, cb = conv_w_full[l], vec(conv_b[l])
        act = _gate_fwd(up0, cw, cb, f"gate_fwd{l}")
        fw[l].update(weights_arrived(("w_down",), l, act, f"ag_wait_down{l}"))
        ff = _matmul(act, fw[l]["w_down"], name=f"mm_down{l}")
        x3 = _resnorm_fwd(x2, ff, vec(g_post_ffn[l]), gt_f, f"resnorm_ffn_fwd{l}")
        saved.append(dict(xin=xin, h1=h1, proj=proj, attn=attn, lse=lse, y=y, z=z, xstart=xstart, gl=gl,
                          merged=merged, mix=mix, x2=x2, h2=h2, up0=up0, act=act, ff=ff))
        xin = x3

    dxo, loss_acc = _loss_bwd(xin, tgt, "loss")
    loss = lax.psum(loss_acc[0, 0], ("x", "y", "c"))

    grads = {n: [None] * depth for n in ORDER}
    dada = [None] * depth
    big_blocks = {n: [None] * depth for n in BIG}
    seg = jnp.pad(jnp.repeat(jnp.eye(n_grp, dtype=F32), STATE, axis=0), ((0, 0), (0, (-n_grp) % LANES)))

    def part_view(n):
        shp = W[n].shape
        if n == "w_in":
            return at_block, "blk"
        if n in COL_SHARDED:
            return at_cols2(shp[2]), "cols"
        return at_rows2(shp[1]), "rows"

    rs_groups, start_tokens = [], []
    small_order = SMALL + ("conv_w",)
    small_shapes = {n: W[n].shape for n in SMALL}
    small_shapes["conv_w"] = (depth, 3, 2 * f)
    small_started = [None] * depth

    def send_partials(items, name):
        parts = [big_blocks[n][l] for n, l in items]
        lands = [lax.empty((N_DEV,) + W[n].shape[1:], BF16) for n, _ in items]
        started, token = _exchange_start(parts, lands, [part_view(n)[0] for n, _ in items],
                                         [at_block] * len(items), name)
        rs_groups.append((items, started, name))
        start_tokens.append(token)
        return token[0, 0]

    order = jnp.zeros((), F32)
    for l in reversed(range(depth)):
        sh_m, sc_m, gt_m, sh_f, sc_f, gt_f = (ada[l, i] for i in range(6))
        gt_f = gt_f + order
        a, p = saved[l], ssm[l]
        cw, cb = conv_w_full[l], vec(conv_b[l])
        dff, dg, dgt_f = _resnorm_bwd(dxo, a["ff"], vec(g_post_ffn[l]), gt_f, f"resnorm_ffn_bwd{l}")
        grads["g_post_ffn"][l] = dg
        dact = _matmul(dff, fw[l]["w_down"], tb=True, name=f"mm_down_dx{l}")
        big_blocks["w_down"][l] = _matmul(a["act"], dff, ta=True, out_dtype=BF16, name=f"mm_down_dw{l}")
        dup0, dcb, dcw = _gate_bwd(a["up0"], cw, cb, dact, f"gate_bwd{l}")
        grads["conv_b"][l] = dcb.reshape(1, 2 * f)
        grads["conv_w"][l] = jnp.transpose(dcw, (1, 0, 2)).reshape(3, 2 * f)
        dh2 = _matmul(dup0, fw[l]["w_up"], tb=True, a_halves=True, name=f"mm_up_dx{l}")
        big_blocks["w_up"][l] = _matmul(a["h2"], dup0, ta=True, b_halves=True, out_dtype=BF16,
                                        name=f"mm_up_dw{l}")
        if l == 0:
            sc_f = sc_f + send_partials([("w_down", 0), ("w_up", 0)], "rs_start_ffn0")
        dx2, dg, dsc_f, dsh_f = _modnorm_bwd(dh2, a["x2"], vec(g_pre_ffn[l]), sc_f, dxo, f"modnorm_ffn_bwd{l}")
        grads["g_pre_ffn"][l] = dg
        dmix, dg, dgt_m = _resnorm_bwd(dx2, a["mix"], vec(g_post_mix[l]), gt_m, f"resnorm_mix_bwd{l}")
        grads["g_post_mix"][l] = dg
        dmerged = _matmul(dmix, fw[l]["w_out"], tb=True, name=f"mm_out_dx{l}")
        big_blocks["w_out"][l] = _matmul(a["merged"], dmix, ta=True, out_dtype=BF16, name=f"mm_out_dw{l}")
        dattn, dgl, dzd, dga, dgs = _merge_bwd(dmerged, a["attn"], a["y"], a["gl"], vec(g_attn_out[l]),
                                               vec(g_ssm_out[l]), f"merge_bwd{l}")
        grads["g_attn_out"][l], grads["g_ssm_out"][l] = dga, dgs
        dz2 = _matmul(dgl, fw[l]["w_glu"], tb=True, name=f"mm_glu_dx{l}")
        big_blocks["w_glu"][l] = _matmul(a["z"], dgl, ta=True, out_dtype=BF16, name=f"mm_glu_dw{l}")
        dskip = p["dskip"]
        if l == 0:
            dskip = dskip + send_partials([("w_out", 0), ("w_glu", 0)], "rs_start_mix0")
        du, dbbd, dccat, dd, da = _ssm_bwd(a["proj"], ucol, a["y"], dzd, dz2, a["xstart"], p["bbd"], p["ccat"],
                                           dskip, p["apow"], t_len, name=f"ssm_bwd{l}")
        grads["ssm_d"][l] = dd
        hs = BLOCK_STATES
        dbb_re = _from_blocks(dbbd[:, :, :hs], STATE, SSM_GROUP)
        dbb_im = _from_blocks(dbbd[:, :, hs:], STATE, SSM_GROUP)
        dccat_t = jnp.transpose(dccat, (0, 2, 1))
        grads["ssm_c_re"][l] = jnp.transpose(_from_blocks(dccat_t[:, :, :hs], STATE, SSM_GROUP), (0, 2, 1))
        grads["ssm_c_im"][l] = -jnp.transpose(_from_blocks(dccat_t[:, :, hs:], STATE, SSM_GROUP), (0, 2, 1))
        dab_re, dab_im = da[:, 0, :hs].reshape(1, gp), da[:, 0, hs:].reshape(1, gp)
        dlr, dli, dls, dbr, dbi = _ssm_params_bwd(p["lr"], p["li"], p["ls"], p["br"], p["bi"], dab_re, dab_im,
                                                  hgp(dbb_re), hgp(dbb_im), seg, f"ssm_params_bwd{l}")
        grads["lam_re"][l], grads["lam_im"][l], grads["log_step"][l] = dlr, dli, dls[0, :n_grp]
        grads["ssm_b_re"][l] = jnp.transpose(dbr.reshape(SSM_GROUP, n_grp, STATE), (1, 2, 0))
        grads["ssm_b_im"][l] = jnp.transpose(dbi.reshape(SSM_GROUP, n_grp, STATE), (1, 2, 0))
        dq, dk, dv, dsink = _attn_bwd(a["proj"], sinks_pad[l:l + 1], a["attn"], a["lse"], dattn,
                                      n_q=n_q, n_kv=n_kv, name=f"attn_bwd{l}")
        grads["attn_sinks"][l] = dsink[0, :n_q]
        dproj = jnp.concatenate([dq, dk, dv, du], axis=1).astype(BF16)
        dh1 = _matmul(dproj, w_in_full[l], tb=True, name=f"mm_in_dx{l}")
        big_blocks["w_in"][l] = _blocks_from_cols(_matmul(a["h1"], dproj, ta=True, name=f"mm_in_dw{l}"),
                                                  f"w_in_grad_layout{l}")
        dxo, dg, dsc_m, dsh_m = _modnorm_bwd(dh1, a["xin"], vec(g_pre_mix[l]), sc_m, dx2, f"modnorm_mix_bwd{l}")
        grads["g_pre_mix"][l] = dg
        dada[l] = jnp.concatenate([dsh_m, dsc_m, dgt_m, dsh_f, dsc_f, dgt_f], axis=1)
        if l > 0:
            order = send_partials([(n, l) for n in reversed(BIG)], f"rs_start_layer{l}")
        else:
            order = order + send_partials([("w_in", 0)], "rs_start_in0")
        spack = _pack([dada[l]] + [grads[n][l] for n in small_order[1:]], F32, 1024)
        started, token = _exchange_start([spack], [lax.empty((N_DEV,) + spack.shape, F32)], [whole], [at_block],
                                         f"small_start{l}")
        small_started[l] = started
        start_tokens.append(token)
        order = order + token[0, 0]
    grad_x = dxo.reshape(x.shape)

    delta, new_m, new_v = {}, {}, {}
    stacked = {n: None for n in BIG}
    landed_layers = {n: 0 for n in BIG}
    after = [dxo] + start_tokens
    for items, started, name in rs_groups:
        mine, landed = _exchange_wait(started, after, [part_view(n)[0] for n, _ in items], [at_block] * len(items),
                                      name.replace("start", "wait"))
        for (n, l), part, slots in zip(items, mine, landed):
            stacked[n] = _sum_slots_own(me_arr, slots, part, part_view(n)[1], f"rs_sum_{n}{l}", layer=l,
                                        n_layers=depth, stacked=stacked[n])
            landed_layers[n] += 1
            if landed_layers[n] == depth:
                grads[n] = stacked[n]
                delta[n], new_m[n], new_v[n] = _adamw_nd(W[n], grads[n], M[n], V[n], f"adamw_{n}")
                after.append(delta[n])

    n_cw = conv_w.shape[2]
    small_sums, dada_rows = [None] * depth, [None] * depth
    for l in reversed(range(depth)):
        mine, landed = _exchange_wait(small_started[l], after, [whole], [at_block], f"small_wait{l}")
        ssum = _sum_slots_own(me_arr, landed[0], mine[0], "self", f"sum_small{l}").reshape(-1)
        small_sums[l] = _unpack(ssum, [small_shapes[n][1:] for n in small_order])
        slot = lax.broadcasted_iota(jnp.int32, (N_DEV, 6 * d), 0)
        dada_rows[l] = jnp.where(slot == me, mine[0].reshape(-1)[:6 * d][None],
                                 landed[0].reshape(N_DEV, -1)[:, :6 * d])
    for i, n in enumerate(small_order):
        grads[n] = jnp.stack([small_sums[l][i] for l in range(depth)])
    grads["conv_w"] = lax.dynamic_slice_in_dim(grads["conv_w"], me * n_cw, n_cw, axis=2)
    dada_all = jnp.stack(dada_rows, axis=1)
    dada_shard = lax.dynamic_slice_in_dim(dada_all, me * n_ada, n_ada, axis=2)
    kp = LANES
    dada_pad = jnp.pad(jnp.transpose(dada_shard, (1, 0, 2)), ((0, 0), (0, kp - N_DEV), (0, 0)))
    act_t = jnp.pad(jnp.transpose(c_act[:N_DEV]), ((0, 0), (0, kp - N_DEV)))
    grads["w_ada"] = _ada_wgrad(act_t, dada_pad, "ada_wgrad")

    delta["w_ada"], new_m["w_ada"], new_v["w_ada"] = _adamw_nd(W["w_ada"], grads["w_ada"], M["w_ada"], V["w_ada"],
                                                                "adamw_w_ada")

    def lane_friendly(a):
        return a.reshape(-1, 1024) if a.ndim > 2 and a.shape[-1] < LANES and a.size % 1024 == 0 else a

    rest = SMALL + ("conv_w",)
    outs = _adamw_many(*[[lane_friendly(t[n]) for n in rest] for t in (W, grads, M, V)], "adamw_small")
    for tgt_d, vals in zip((delta, new_m, new_v), outs):
        for n, val in zip(rest, vals):
            tgt_d[n] = val.reshape(W[n].shape)

    return (loss, grad_x, *[grads[n] for n in ORDER], *[delta[n] for n in ORDER],
            *[new_m[n] for n in ORDER], *[new_v[n] for n in ORDER])
```

```python
import math

import jax
import jax.numpy as jnp
from jax import lax
from jax.experimental import pallas as pl
from jax.experimental.pallas import tpu as pltpu

F32 = jnp.float32
BF16 = jnp.bfloat16

N_DEV = 8
HEAD_DIM = 64
WINDOW = 128
SSM_GROUP = 16
STATE = 64
LANES = 128
GROUPS_PER_BLOCK = LANES // SSM_GROUP
BLOCK_STATES = GROUPS_PER_BLOCK * STATE
EPS = 1e-6
NEG = -1e30
ADAM_LR, ADAM_B1, ADAM_B2, ADAM_EPS, ADAM_WD, ADAM_STEP = 0.001, 0.9, 0.999, 1e-08, 0.01, 10
VMEM_BYTES_V7X = 64 * 1024 * 1024
GELU_C = math.sqrt(2.0 / math.pi)
MESH = pl.DeviceIdType.MESH
ANY = pl.BlockSpec(memory_space=pl.ANY)


def _pick(n, pref, align):
    t = (min(pref, n) // align) * align
    while t >= align:
        if n % t == 0:
            return t
        t -= align
    return n


def _params(vmem_bytes=None):
    if vmem_bytes is None:
        return pltpu.CompilerParams()
    return pltpu.CompilerParams(vmem_limit_bytes=int(min(vmem_bytes, VMEM_BYTES_V7X - (8 << 20))))


def _gelu_and_grad(x):
    x2 = x * x
    half_x = 0.5 * x
    th = jnp.tanh((GELU_C * x) * (1.0 + 0.044715 * x2))
    one_th = 1.0 + th
    grad = 0.5 * one_th + (half_x * (1.0 - th * th)) * (GELU_C + (3.0 * 0.044715 * GELU_C) * x2)
    return half_x * one_th, grad


def _gelu(x):
    return _gelu_and_grad(x)[0]


def _gelu_grad(x):
    return _gelu_and_grad(x)[1]


def _rstd(x):
    return lax.rsqrt(jnp.mean(x * x, axis=-1, keepdims=True) + EPS)


def _norm_bwd(dhat, xhat, r):
    return r * (dhat - xhat * jnp.mean(dhat * xhat, axis=-1, keepdims=True))


def _matmul(a, b, *, ta=False, tb=False, a_halves=False, b_halves=False, out_dtype=F32, name):
    assert not (a_halves and ta) and not (b_halves and tb)
    if a_halves:
        m, kdim = a.shape[1], 2 * a.shape[2]
    else:
        (kdim, m) = a.shape if ta else a.shape[::-1]
    if b_halves:
        k2, n = b.shape[1], 2 * b.shape[2]
    else:
        (n, k2) = b.shape if tb else b.shape[::-1]
    assert kdim == k2, (a.shape, b.shape, ta, tb)
    tm = _pick(m, 1024, LANES)
    tn = _pick(n // 2, 1536, LANES) if b_halves else _pick(n, 1024, LANES)
    tk = _pick(kdim // 2, 2816, LANES) if a_halves else _pick(kdim, 2816, LANES)
    nk = kdim // tk
    dn = (((0 if ta else 1,), (1 if tb else 0,)), ((), ()))

    def partial_product(a_ref, b_ref):
        return lax.dot_general(a_ref[...].astype(BF16), b_ref[...].astype(BF16), dn, preferred_element_type=F32)

    def body_one(a_ref, b_ref, o_ref):
        o_ref[...] = partial_product(a_ref, b_ref).astype(o_ref.dtype)

    def body_acc(a_ref, b_ref, o_ref, acc_ref):
        k = pl.program_id(2)

        @pl.when(k == 0)
        def _():
            acc_ref[...] = partial_product(a_ref, b_ref)

        @pl.when((k > 0) & (k < nk - 1))
        def _():
            acc_ref[...] += partial_product(a_ref, b_ref)

        @pl.when(k == nk - 1)
        def _():
            o_ref[...] = (acc_ref[...] + partial_product(a_ref, b_ref)).astype(o_ref.dtype)

    body = body_one if nk == 1 else body_acc
    a_spec = pl.BlockSpec((tk, tm), lambda i, j, k: (k, i)) if ta else pl.BlockSpec((tm, tk), lambda i, j, k: (i, k))
    b_spec = pl.BlockSpec((tn, tk), lambda i, j, k: (j, k)) if tb else pl.BlockSpec((tk, tn), lambda i, j, k: (k, j))
    if a_halves:
        nkh = nk // 2
        a_spec = pl.BlockSpec((None, tm, tk), lambda i, j, k: (k // nkh, i, k % nkh))
    if b_halves:
        njh = n // tn // 2
        b_spec = pl.BlockSpec((None, tk, tn), lambda i, j, k: (j // njh, k, j % njh))
    vmem = (2 * (tm * tk * a.dtype.itemsize + tk * tn * b.dtype.itemsize) + tm * tn * 4
            + 2 * tm * tn * jnp.dtype(out_dtype).itemsize + 3 * tm * tn * 4 + (4 << 20))
    return pl.pallas_call(
        body, name=name, grid=(m // tm, n // tn, nk),
        in_specs=[a_spec, b_spec], out_specs=pl.BlockSpec((tm, tn), lambda i, j, k: (i, j)),
        out_shape=jax.ShapeDtypeStruct((m, n), out_dtype),
        scratch_shapes=[] if nk == 1 else [pltpu.VMEM((tm, tn), F32)],
        compiler_params=_params(vmem),
    )(a, b)


def _all_gather(x, name):
    def body(x_ref, out_ref, send_sems, recv_sems, local_sem):
        x_, y_, c_ = lax.axis_index("x"), lax.axis_index("y"), lax.axis_index("c")
        me, sibling = (x_, y_, c_), (x_, y_, 1 - c_)
        chips = [(1 - x_, y_), (x_, 1 - y_), (1 - x_, 1 - y_)]

        def slot(px, py, pc):
            return out_ref.at[4 * px + 2 * py + pc]

        def copy(k, block, to, src=None):
            return pltpu.make_async_remote_copy(
                src_ref=slot(*block) if src is None else src, dst_ref=slot(*block),
                send_sem=send_sems.at[k], recv_sem=recv_sems.at[k], device_id=to, device_id_type=MESH)

        mine = pltpu.make_async_copy(x_ref, slot(*me), local_sem)
        mine.start()
        first = [copy(0, me, sibling, src=x_ref)]
        first += [copy(1 + j, me, (*chip, c_), src=x_ref) for j, chip in enumerate(chips)]
        for cp in first:
            cp.start()
        passed = [copy(4 + j, (*chip, c_), sibling) for j, chip in enumerate(chips)]
        for j, chip in enumerate(chips):
            copy(1 + j, (*chip, c_), me).wait_recv()
            passed[j].start()
        copy(0, sibling, me).wait_recv()
        for j, chip in enumerate(chips):
            copy(4 + j, (*chip, 1 - c_), me).wait_recv()
        for cp in first + passed:
            cp.wait_send()
        mine.wait()

    return pl.pallas_call(
        body, name=name, out_shape=jax.ShapeDtypeStruct((N_DEV,) + x.shape, x.dtype),
        in_specs=[ANY], out_specs=ANY,
        scratch_shapes=[pltpu.SemaphoreType.DMA((7,)), pltpu.SemaphoreType.DMA((7,)), pltpu.SemaphoreType.DMA],
    )(x)


def _gather_multi(srcs, out_shapes, views, name):
    n = len(srcs)

    def body(*refs):
        src_refs, out_refs = refs[:n], refs[n:2 * n]
        send_sems, recv_sems, local_sems = refs[2 * n:]
        x_, y_, c_ = lax.axis_index("x"), lax.axis_index("y"), lax.axis_index("c")
        me, sibling = (x_, y_, c_), (x_, y_, 1 - c_)
        chips = [(1 - x_, y_), (x_, 1 - y_), (1 - x_, 1 - y_)]

        def slot(i, px, py, pc):
            return views[i](out_refs[i], 4 * px + 2 * py + pc)

        def copy(i, k, block, to, from_src=False):
            return pltpu.make_async_remote_copy(
                src_ref=src_refs[i] if from_src else slot(i, *block), dst_ref=slot(i, *block),
                send_sem=send_sems.at[7 * i + k], recv_sem=recv_sems.at[7 * i + k], device_id=to, device_id_type=MESH)

        mine = [pltpu.make_async_copy(src_refs[i], slot(i, *me), local_sems.at[i]) for i in range(n)]
        for cp in mine:
            cp.start()
        first = []
        for i in range(n):
            first.append(copy(i, 0, me, sibling, True))
            first += [copy(i, 1 + j, me, (*chip, c_), True) for j, chip in enumerate(chips)]
        for cp in first:
            cp.start()
        passed = []
        for j, chip in enumerate(chips):
            for i in range(n):
                copy(i, 1 + j, (*chip, c_), me).wait_recv()
                fwd = copy(i, 4 + j, (*chip, c_), sibling)
                fwd.start()
                passed.append(fwd)
        for i in range(n):
            copy(i, 0, sibling, me).wait_recv()
            for j, chip in enumerate(chips):
                copy(i, 4 + j, (*chip, 1 - c_), me).wait_recv()
        for cp in first + passed:
            cp.wait_send()
        for cp in mine:
            cp.wait()

    return pl.pallas_call(
        body, name=name, out_shape=[jax.ShapeDtypeStruct(s, a.dtype) for s, a in zip(out_shapes, srcs)],
        in_specs=[ANY] * n, out_specs=[ANY] * n,
        scratch_shapes=[pltpu.SemaphoreType.DMA((7 * n,)), pltpu.SemaphoreType.DMA((7 * n,)),
                        pltpu.SemaphoreType.DMA((n,))],
    )(*srcs)


HBM_SPEC = pl.BlockSpec(memory_space=pltpu.HBM)
SEM_SPEC = pl.BlockSpec(memory_space=pltpu.SEMAPHORE)
SIDE_EFFECT = pltpu.SideEffectType.DATAFLOW_SIDE_EFFECTING
N_PEERS = N_DEV - 1


def _peer(k, x_, y_, c_):
    px = 1 - x_ if (k >> 2) & 1 else x_
    py = 1 - y_ if (k >> 1) & 1 else y_
    pc = 1 - c_ if k & 1 else c_
    return (px, py, pc), 4 * px + 2 * py + pc


def _exchange_copies(src_refs, land_refs, send_sems, recv_sems, src_views, dst_views):
    x_, y_, c_ = lax.axis_index("x"), lax.axis_index("y"), lax.axis_index("c")
    me = 4 * x_ + 2 * y_ + c_
    out = []
    for i in range(len(src_refs)):
        for k in range(1, N_DEV):
            peer, idx = _peer(k, x_, y_, c_)

            def copy(dst_slot, i=i, k=k, peer=peer, idx=idx):
                return pltpu.make_async_remote_copy(
                    src_ref=src_views[i](src_refs[i], idx), dst_ref=dst_views[i](land_refs[i], dst_slot),
                    send_sem=send_sems[i].at[k - 1], recv_sem=recv_sems[i].at[k - 1], device_id=peer,
                    device_id_type=MESH)

            out.append((copy(me), copy(idx)))
    return out


def _exchange_start(srcs, lands, src_views, dst_views, name):
    n = len(srcs)

    def body(*refs):
        src_refs, land_refs = refs[:n], refs[n:2 * n]
        send_sems, recv_sems = refs[2 * n:3 * n], refs[3 * n:4 * n]
        token = refs[-1]
        for send, _ in _exchange_copies(src_refs, land_refs, send_sems, recv_sems, src_views, dst_views):
            send.start()
        token[...] = jnp.zeros_like(token)

    sems = [pltpu.SemaphoreType.DMA((N_PEERS,))] * n
    thru = [pltpu.HBM(a.shape, a.dtype) for a in list(srcs) + list(lands)]
    outs = pl.pallas_call(
        body, name=name, out_shape=sems + sems + thru + [jax.ShapeDtypeStruct((8, LANES), F32)],
        in_specs=[HBM_SPEC] * (2 * n),
        out_specs=[SEM_SPEC] * (2 * n) + [HBM_SPEC] * (2 * n) + [pl.BlockSpec(memory_space=pltpu.VMEM)],
        input_output_aliases={j: 2 * n + j for j in range(2 * n)},
        compiler_params=pltpu.CompilerParams(has_side_effects=SIDE_EFFECT),
    )(*[pltpu.with_memory_space_constraint(a, pltpu.HBM) for a in list(srcs) + list(lands)])
    per_array = [(outs[j], outs[n + j], outs[2 * n + j], outs[3 * n + j]) for j in range(n)]
    return per_array, outs[-1]


def _exchange_wait(started, after, src_views, dst_views, name):
    send_sems, recv_sems, srcs, lands = (list(t) for t in zip(*started))
    n = len(srcs)
    after = list(after)

    def body(*refs):
        src_refs, land_refs = refs[:n], refs[n:2 * n]
        send_refs, recv_refs = refs[2 * n:3 * n], refs[3 * n:4 * n]
        copies = _exchange_copies(src_refs, land_refs, send_refs, recv_refs, src_views, dst_views)
        for send, _ in copies:
            send.wait_send()
        for _, recv in copies:
            recv.wait_recv()

    thru = [pltpu.HBM(a.shape, a.dtype) for a in list(srcs) + list(lands)]
    outs = pl.pallas_call(
        body, name=name, out_shape=thru,
        in_specs=[HBM_SPEC] * (2 * n) + [SEM_SPEC] * (2 * n) + [ANY] * len(after),
        out_specs=[HBM_SPEC] * (2 * n),
        input_output_aliases={j: j for j in range(2 * n)},
        compiler_params=pltpu.CompilerParams(has_side_effects=SIDE_EFFECT),
    )(*srcs, *lands, *send_sems, *recv_sems, *after)
    return outs[:n], outs[n:]


def _place_own(me, src, land, kind, name):
    r, c = src.shape
    tr = _pick(r, 512, 16)
    nt = r // tr
    if kind == "rows":
        out_spec = pl.BlockSpec((tr, c), lambda i, mr: (mr[0] * nt + i, 0))
    elif kind == "cols":
        out_spec = pl.BlockSpec((tr, c), lambda i, mr: (i, mr[0]))
    else:
        out_spec = pl.BlockSpec((None, tr, c), lambda i, mr: (mr[0], i, 0))

    def body(me_ref, s_ref, land_ref, o_ref):
        o_ref[...] = s_ref[...]

    return pl.pallas_call(
        body, name=name,
        grid_spec=pltpu.PrefetchScalarGridSpec(
            num_scalar_prefetch=1, grid=(nt,),
            in_specs=[pl.BlockSpec((tr, c), lambda i, mr: (i, 0)), ANY], out_specs=out_spec),
        out_shape=jax.ShapeDtypeStruct(land.shape, land.dtype),
        input_output_aliases={2: 0},
    )(me, src, land)


def _sum_slots_own(me, landed, part, kind, name, *, layer=0, n_layers=1, stacked=None):
    _, r, c = landed.shape
    tr = _pick(r, 512, 16)
    nt = r // tr
    if kind == "rows":
        part_spec = pl.BlockSpec((tr, c), lambda i, mr: (mr[0] * nt + i, 0))
    elif kind == "cols":
        part_spec = pl.BlockSpec((tr, c), lambda i, mr: (i, mr[0]))
    elif kind == "blk":
        part_spec = pl.BlockSpec((None, tr, c), lambda i, mr: (mr[0], i, 0))
    else:
        part_spec = pl.BlockSpec((tr, c), lambda i, mr: (i, 0))

    def body(me_ref, x_ref, p_ref, *rest):
        o_ref = rest[-1]
        own = p_ref[...].astype(F32)
        acc = jnp.where(me_ref[0] == 0, own, x_ref[0].astype(F32))
        for i in range(1, N_DEV):
            acc = acc + jnp.where(me_ref[0] == i, own, x_ref[i].astype(F32))
        o_ref[...] = acc

    operands = [me, landed, part] + ([] if stacked is None else [stacked])
    return pl.pallas_call(
        body, name=name,
        grid_spec=pltpu.PrefetchScalarGridSpec(
            num_scalar_prefetch=1, grid=(nt,),
            in_specs=[pl.BlockSpec((N_DEV, tr, c), lambda i, mr: (0, i, 0)), part_spec]
            + ([] if stacked is None else [ANY]),
            out_specs=pl.BlockSpec((None, tr, c), lambda i, mr: (layer, i, 0))),
        out_shape=jax.ShapeDtypeStruct((n_layers, r, c), F32),
        input_output_aliases={} if stacked is None else {3: 0},
        compiler_params=_params(2 * N_DEV * tr * c * landed.dtype.itemsize + 8 * tr * c * 4 + (4 << 20)),
    )(*operands)


def _cols_from_blocks(blk, name):
    nd, nl, k, n = blk.shape
    tk = _pick(k, 256, 16)

    def body(b_ref, o_ref, wide_ref):
        for dev in range(nd):
            wide_ref[:, dev * n:(dev + 1) * n] = b_ref[dev].astype(F32)
        o_ref[...] = wide_ref[...].astype(o_ref.dtype)

    return pl.pallas_call(
        body, name=name, grid=(nl, k // tk),
        in_specs=[pl.BlockSpec((nd, None, tk, n), lambda l, i: (0, l, i, 0))],
        out_specs=pl.BlockSpec((None, tk, nd * n), lambda l, i: (l, i, 0)),
        out_shape=jax.ShapeDtypeStruct((nl, k, nd * n), BF16),
        scratch_shapes=[pltpu.VMEM((tk, nd * n), F32)],
    )(blk)


def _blocks_from_cols(full, name):
    k, n8 = full.shape
    n = n8 // N_DEV
    tk = _pick(k, 256, 16)

    def body(f_ref, o_ref):
        for dev in range(N_DEV):
            o_ref[dev] = f_ref[:, dev * n:(dev + 1) * n].astype(o_ref.dtype)

    return pl.pallas_call(
        body, name=name, grid=(k // tk,),
        in_specs=[pl.BlockSpec((tk, n8), lambda i: (i, 0))],
        out_specs=pl.BlockSpec((N_DEV, tk, n), lambda i: (0, i, 0)),
        out_shape=jax.ShapeDtypeStruct((N_DEV, k, n), BF16),
    )(full)


def _pack(arrs, dtype, cols):
    flat = jnp.concatenate([a.astype(dtype).reshape(-1) for a in arrs])
    unit = 16 * cols
    pad = (-flat.shape[0]) % unit
    flat = jnp.pad(flat, (0, pad))
    return flat.reshape(-1, cols)


def _unpack(flat, shapes):
    out, off = [], 0
    for s in shapes:
        n = math.prod(s)
        out.append(flat[off:off + n].reshape(s))
        off += n
    return out


def _ada_fwd(c_all, w_ada, b_shard, name):
    nl, d, n = w_ada.shape
    tn = _pick(n, 512, LANES)

    def body(c_ref, w_ref, b_ref, o_ref, act_ref):
        cv = c_ref[...]
        act = cv * jax.nn.sigmoid(cv)
        act_ref[...] = act
        o_ref[...] = jnp.dot(act.astype(BF16), w_ref[...].astype(BF16), preferred_element_type=F32) + b_ref[...]

    return pl.pallas_call(
        body, name=name, grid=(nl, n // tn),
        in_specs=[pl.BlockSpec(c_all.shape, lambda l, j: (0, 0)),
                  pl.BlockSpec((None, d, tn), lambda l, j: (l, 0, j)),
                  pl.BlockSpec((None, 1, tn), lambda l, j: (l, 0, j))],
        out_specs=[pl.BlockSpec((None, c_all.shape[0], tn), lambda l, j: (l, 0, j)),
                   pl.BlockSpec(c_all.shape, lambda l, j: (0, 0))],
        out_shape=[jax.ShapeDtypeStruct((nl, c_all.shape[0], n), F32), jax.ShapeDtypeStruct(c_all.shape, F32)],
        compiler_params=_params(2 * d * tn * 4 + d * tn * 2 + (8 << 20)),
    )(c_all, w_ada, b_shard)


def _ada_wgrad(act_t, dada, name):
    d, kp = act_t.shape
    nl, _, n = dada.shape
    tm = _pick(d, 512, 8)

    def body(a_ref, g_ref, o_ref):
        o_ref[...] = jnp.dot(a_ref[...].astype(BF16), g_ref[...].astype(BF16), preferred_element_type=F32)

    return pl.pallas_call(
        body, name=name, grid=(nl, d // tm),
        in_specs=[pl.BlockSpec((tm, kp), lambda l, i: (i, 0)), pl.BlockSpec((None, kp, n), lambda l, i: (l, 0, 0))],
        out_specs=pl.BlockSpec((None, tm, n), lambda l, i: (l, i, 0)),
        out_shape=jax.ShapeDtypeStruct((nl, d, n), F32),
        compiler_params=_params(4 * tm * n * 4 + 2 * kp * n * 4 + (8 << 20)),
    )(act_t, dada)


def _row_spec(tm, d):
    return pl.BlockSpec((tm, d), lambda i: (i, 0))


def _row_params(tm, d):
    return _params(11 * tm * d * 4 + (4 << 20))


def _vec_spec(d):
    return pl.BlockSpec((1, d), lambda i: (0, 0))


def _modnorm_fwd(x, g, sc, sh, name):
    s, d = x.shape
    tm = _pick(s, 512, 16)

    def body(x_ref, g_ref, sc_ref, sh_ref, o_ref):
        xv = x_ref[...]
        o_ref[...] = ((xv * _rstd(xv)) * g_ref[...] * (1.0 + sc_ref[...]) + sh_ref[...]).astype(o_ref.dtype)

    return pl.pallas_call(
        body, name=name, grid=(s // tm,),
        in_specs=[_row_spec(tm, d), _vec_spec(d), _vec_spec(d), _vec_spec(d)], out_specs=_row_spec(tm, d),
        out_shape=jax.ShapeDtypeStruct((s, d), BF16), compiler_params=_row_params(tm, d),
    )(x, g, sc, sh)


def _modnorm_bwd(dh, x, g, sc, dres, name):
    s, d = x.shape
    tm = _pick(s, 256, 8)

    def body(dh_ref, x_ref, g_ref, sc_ref, dres_ref, dx_ref, dg_ref, dsc_ref, dsh_ref):
        @pl.when(pl.program_id(0) == 0)
        def _():
            dg_ref[...] = jnp.zeros_like(dg_ref)
            dsc_ref[...] = jnp.zeros_like(dsc_ref)
            dsh_ref[...] = jnp.zeros_like(dsh_ref)

        dh_, xv, gv = dh_ref[...], x_ref[...], g_ref[...]
        r = _rstd(xv)
        xhat = xv * r
        dn = dh_ * (1.0 + sc_ref[...])
        dsh_ref[...] += jnp.sum(dh_, axis=0, keepdims=True)
        dsc_ref[...] += jnp.sum(dh_ * (xhat * gv), axis=0, keepdims=True)
        dg_ref[...] += jnp.sum(dn * xhat, axis=0, keepdims=True)
        dx_ref[...] = _norm_bwd(dn * gv, xhat, r) + dres_ref[...]

    vec = jax.ShapeDtypeStruct((1, d), F32)
    return pl.pallas_call(
        body, name=name, grid=(s // tm,),
        in_specs=[_row_spec(tm, d), _row_spec(tm, d), _vec_spec(d), _vec_spec(d), _row_spec(tm, d)],
        out_specs=[_row_spec(tm, d), _vec_spec(d), _vec_spec(d), _vec_spec(d)],
        out_shape=[jax.ShapeDtypeStruct((s, d), F32), vec, vec, vec],
    )(dh, x, g, sc, dres)


def _resnorm_modnorm_fwd(x, y, g, gt, g2, sc2, sh2, name):
    s, d = x.shape
    tm = _pick(s, 512, 16)

    def body(x_ref, y_ref, g_ref, gt_ref, g2_ref, sc_ref, sh_ref, o_ref, h_ref):
        yv = y_ref[...]
        xo = x_ref[...] + (1.0 + gt_ref[...]) * ((yv * _rstd(yv)) * g_ref[...])
        o_ref[...] = xo
        h_ref[...] = ((xo * _rstd(xo)) * g2_ref[...] * (1.0 + sc_ref[...]) + sh_ref[...]).astype(h_ref.dtype)

    return pl.pallas_call(
        body, name=name, grid=(s // tm,),
        in_specs=[_row_spec(tm, d), _row_spec(tm, d)] + [_vec_spec(d)] * 5,
        out_specs=[_row_spec(tm, d), _row_spec(tm, d)],
        out_shape=[jax.ShapeDtypeStruct((s, d), F32), jax.ShapeDtypeStruct((s, d), BF16)],
        compiler_params=_row_params(tm, d),
    )(x, y, g, gt, g2, sc2, sh2)


def _resnorm_loss(x, y, g, gt, tgt, name):
    s, d = x.shape
    tm = _pick(s, 512, 8)

    def body(x_ref, y_ref, g_ref, gt_ref, t_ref, dy_ref, l_ref):
        @pl.when(pl.program_id(0) == 0)
        def _():
            l_ref[...] = jnp.zeros_like(l_ref)

        yv = y_ref[...]
        e = x_ref[...] + (1.0 + gt_ref[...]) * ((yv * _rstd(yv)) * g_ref[...]) - t_ref[...]
        dy_ref[...] = e * (1.0 / d)
        l_ref[...] += jnp.sum(e * e) * (0.5 / d)

    return pl.pallas_call(
        body, name=name, grid=(s // tm,),
        in_specs=[_row_spec(tm, d), _row_spec(tm, d), _vec_spec(d), _vec_spec(d), _row_spec(tm, d)],
        out_specs=[_row_spec(tm, d), pl.BlockSpec((8, LANES), lambda i: (0, 0))],
        out_shape=[jax.ShapeDtypeStruct((s, d), F32), jax.ShapeDtypeStruct((8, LANES), F32)],
        compiler_params=_row_params(tm, d),
    )(x, y, g, gt, tgt)


def _resnorm_bwd(dxo, y, g, gt, name):
    s, d = y.shape
    tm = _pick(s, 512, 16)

    def body(dxo_ref, y_ref, g_ref, gt_ref, dy_ref, dg_ref, dgt_ref):
        @pl.when(pl.program_id(0) == 0)
        def _():
            dg_ref[...] = jnp.zeros_like(dg_ref)
            dgt_ref[...] = jnp.zeros_like(dgt_ref)

        dxo_, yv, gv = dxo_ref[...], y_ref[...], g_ref[...]
        r = _rstd(yv)
        yhat = yv * r
        dn = dxo_ * (1.0 + gt_ref[...])
        dgt_ref[...] += jnp.sum(dxo_ * (yhat * gv), axis=0, keepdims=True)
        dg_ref[...] += jnp.sum(dn * yhat, axis=0, keepdims=True)
        dy_ref[...] = _norm_bwd(dn * gv, yhat, r).astype(dy_ref.dtype)

    vec = jax.ShapeDtypeStruct((1, d), F32)
    return pl.pallas_call(
        body, name=name, grid=(s // tm,),
        in_specs=[_row_spec(tm, d), _row_spec(tm, d), _vec_spec(d), _vec_spec(d)],
        out_specs=[_row_spec(tm, d), _vec_spec(d), _vec_spec(d)],
        out_shape=[jax.ShapeDtypeStruct((s, d), BF16), vec, vec], compiler_params=_row_params(tm, d),
    )(dxo, y, g, gt)


def _attn_specs(n_q, n_kv):
    aw, kvd = n_q * HEAD_DIM, n_kv * HEAD_DIM
    assert aw % kvd == 0
    kcol = aw // kvd
    q = pl.BlockSpec((WINDOW, aw), lambda n: (n, 0))
    kc = pl.BlockSpec((WINDOW, kvd), lambda n: (n, kcol))
    kp = pl.BlockSpec((WINDOW, kvd), lambda n: (jnp.maximum(n - 1, 0), kcol))
    vc = pl.BlockSpec((WINDOW, kvd), lambda n: (n, kcol + 1))
    vp = pl.BlockSpec((WINDOW, kvd), lambda n: (jnp.maximum(n - 1, 0), kcol + 1))
    return [q, kc, kp, vc, vp]


def _band_mask(n, n_heads):
    qi = lax.broadcasted_iota(jnp.int32, (n_heads * WINDOW, 2 * WINDOW), 0) & (WINDOW - 1)
    kj = lax.broadcasted_iota(jnp.int32, (n_heads * WINDOW, 2 * WINDOW), 1)
    return (kj > qi) & (kj <= qi + WINDOW) & ((kj >= WINDOW) | (n > 0))


def _stack_heads(ref, heads):
    return jnp.concatenate([ref[:, h * HEAD_DIM:(h + 1) * HEAD_DIM] for h in heads], axis=0)


def _stack_sinks(ref, heads):
    return jnp.concatenate([jnp.broadcast_to(ref[:, h:h + 1], (WINDOW, 1)) for h in heads], axis=0)


_NT = (((1,), (1,)), ((), ()))
_TN = (((0,), (0,)), ((), ()))


def _attn_fwd(proj, sinks, *, n_q, n_kv, name):
    s = proj.shape[0]
    aw, grp = n_q * HEAD_DIM, n_q // n_kv

    def body(q_ref, kc_ref, kp_ref, vc_ref, vp_ref, sink_ref, o_ref, lse_ref):
        valid = _band_mask(pl.program_id(0), grp)
        kb = jnp.concatenate([kp_ref[...], kc_ref[...]], axis=0).astype(BF16)
        vb = jnp.concatenate([vp_ref[...], vc_ref[...]], axis=0).astype(BF16)
        lse_ref[...] = jnp.zeros_like(lse_ref)
        for g in range(n_kv):
            heads = range(g * grp, (g + 1) * grp)
            gs = slice(g * HEAD_DIM, (g + 1) * HEAD_DIM)
            qg = _stack_heads(q_ref, heads).astype(BF16)
            sink = _stack_sinks(sink_ref, heads)
            sc = lax.dot_general(qg, kb[:, gs], _NT, preferred_element_type=F32)
            sc = jnp.where(valid, sc * (HEAD_DIM ** -0.5), NEG)
            m = jnp.maximum(jnp.max(sc, axis=-1, keepdims=True), sink)
            e = jnp.exp(sc - m)
            den = jnp.sum(e, axis=-1, keepdims=True) + jnp.exp(sink - m)
            p = e * (1.0 / den)
            og = jnp.dot(p.astype(BF16), vb[:, gs], preferred_element_type=F32)
            lse = m + jnp.log(den)
            for i, h in enumerate(heads):
                rows = slice(i * WINDOW, (i + 1) * WINDOW)
                o_ref[:, h * HEAD_DIM:(h + 1) * HEAD_DIM] = og[rows]
                lse_ref[:, h:h + 1] = lse[rows]

    return pl.pallas_call(
        body, name=name, grid=(s // WINDOW,),
        in_specs=_attn_specs(n_q, n_kv) + [pl.BlockSpec((1, LANES), lambda n: (0, 0))],
        out_specs=[pl.BlockSpec((WINDOW, aw), lambda n: (n, 0)), pl.BlockSpec((WINDOW, LANES), lambda n: (n, 0))],
        out_shape=[jax.ShapeDtypeStruct((s, aw), F32), jax.ShapeDtypeStruct((s, LANES), F32)],
    )(proj, proj, proj, proj, proj, sinks)


def _attn_bwd(proj, sinks, out, lse, dout, *, n_q, n_kv, name):
    s = proj.shape[0]
    aw, kvd, grp = n_q * HEAD_DIM, n_kv * HEAD_DIM, n_q // n_kv
    scale = HEAD_DIM ** -0.5

    def body(q_ref, kc_ref, kp_ref, vc_ref, vp_ref, sink_ref, o_ref, lse_ref, do_ref,
             dq_ref, dk_ref, dv_ref, dsink_ref):
        n = pl.program_id(0)

        @pl.when(n == 0)
        def _():
            dk_ref[...] = jnp.zeros_like(dk_ref)
            dv_ref[...] = jnp.zeros_like(dv_ref)
            dsink_ref[...] = jnp.zeros_like(dsink_ref)

        valid = _band_mask(n, grp)
        kb = jnp.concatenate([kp_ref[...], kc_ref[...]], axis=0).astype(BF16)
        vb = jnp.concatenate([vp_ref[...], vc_ref[...]], axis=0).astype(BF16)
        lane = lax.broadcasted_iota(jnp.int32, (8, LANES), 1)
        dsink = jnp.zeros((8, LANES), F32)
        cur = pl.ds(pl.multiple_of(n * WINDOW, WINDOW), WINDOW)
        prev = pl.ds(pl.multiple_of(jnp.maximum(n - 1, 0) * WINDOW, WINDOW), WINDOW)
        for g in range(n_kv):
            heads = range(g * grp, (g + 1) * grp)
            gs = slice(g * HEAD_DIM, (g + 1) * HEAD_DIM)
            qg = _stack_heads(q_ref, heads).astype(BF16)
            do = _stack_heads(do_ref, heads)
            dob = do.astype(BF16)
            lse = jnp.concatenate([lse_ref[:, h:h + 1] for h in heads], axis=0)
            sc = lax.dot_general(qg, kb[:, gs], _NT, preferred_element_type=F32)
            sc = jnp.where(valid, sc * scale, NEG)
            p = jnp.exp(sc - lse)
            delta = jnp.sum(do * _stack_heads(o_ref, heads), axis=-1, keepdims=True)
            dp = lax.dot_general(dob, vb[:, gs], _NT, preferred_element_type=F32)
            ds = (p * (dp - delta) * scale).astype(BF16)
            dqg = jnp.dot(ds, kb[:, gs], preferred_element_type=F32)
            dkb = lax.dot_general(ds, qg, _TN, preferred_element_type=F32)
            dvb = lax.dot_general(p.astype(BF16), dob, _TN, preferred_element_type=F32)
            sink_term = jnp.exp(_stack_sinks(sink_ref, heads) - lse) * delta
            for i, h in enumerate(heads):
                rows = slice(i * WINDOW, (i + 1) * WINDOW)
                dq_ref[:, h * HEAD_DIM:(h + 1) * HEAD_DIM] = dqg[rows]
                dsink = dsink + jnp.where(lane == h, -jnp.sum(sink_term[rows]), 0.0)
            dk_ref[cur, gs] += dkb[WINDOW:]
            dv_ref[cur, gs] += dvb[WINDOW:]

            @pl.when(n > 0)
            def _():
                dk_ref[prev, gs] += dkb[:WINDOW]
                dv_ref[prev, gs] += dvb[:WINDOW]

        dsink_ref[...] += dsink

    blk = pl.BlockSpec((WINDOW, aw), lambda n: (n, 0))
    kv_full = pl.BlockSpec((s, kvd), lambda n: (0, 0))
    return pl.pallas_call(
        body, name=name, grid=(s // WINDOW,),
        in_specs=_attn_specs(n_q, n_kv) + [pl.BlockSpec((1, LANES), lambda n: (0, 0)), blk,
                                           pl.BlockSpec((WINDOW, LANES), lambda n: (n, 0)), blk],
        out_specs=[blk, kv_full, kv_full, pl.BlockSpec((8, LANES), lambda n: (0, 0))],
        out_shape=[jax.ShapeDtypeStruct((s, aw), F32), jax.ShapeDtypeStruct((s, kvd), F32),
                   jax.ShapeDtypeStruct((s, kvd), F32), jax.ShapeDtypeStruct((8, LANES), F32)],
    )(proj, proj, proj, proj, proj, sinks, out, lse, dout)


def _disc(lr, li, ls):
    dt = jnp.exp(ls)
    mag = jnp.exp(lr * dt)
    ang = li * dt
    ab_re, ab_im = mag * jnp.cos(ang), mag * jnp.sin(ang)
    den = lr * lr + li * li
    f_re = ((ab_re - 1.0) * lr + ab_im * li) / den
    f_im = (ab_im * lr - (ab_re - 1.0) * li) / den
    return ab_re, ab_im, f_re, f_im


POW_ROWS = 8
SUB = 8
TAB_ROWS = POW_ROWS + 2 * SUB


def _ssm_params_fwd(lr, li, ls, b_re, b_im, name):
    gp = lr.shape[1]
    h = b_re.shape[0]

    def body(lr_ref, li_ref, ls_ref, br_ref, bi_ref, bbr_ref, bbi_ref, tr_ref, ti_ref):
        ab_re, ab_im, f_re, f_im = _disc(lr_ref[...], li_ref[...], ls_ref[...])
        br, bi = br_ref[...], bi_ref[...]
        bbr_ref[...] = f_re * br - f_im * bi
        bbi_ref[...] = f_re * bi + f_im * br
        pr, pi = ab_re, ab_im
        for i in range(POW_ROWS):
            tr_ref[i:i + 1, :] = pr
            ti_ref[i:i + 1, :] = pi
            pr, pi = pr * pr - pi * pi, 2.0 * pr * pi
        pr, pi = ab_re, ab_im
        for r in range(SUB):
            for row in (POW_ROWS + r, POW_ROWS + 2 * SUB - 1 - r):
                tr_ref[row:row + 1, :] = pr
                ti_ref[row:row + 1, :] = pi
            pr, pi = pr * ab_re - pi * ab_im, pr * ab_im + pi * ab_re

    mat, tab = jax.ShapeDtypeStruct((h, gp), F32), jax.ShapeDtypeStruct((TAB_ROWS, gp), F32)
    return pl.pallas_call(body, name=name, out_shape=[mat, mat, tab, tab])(lr, li, ls, b_re, b_im)


def _ssm_params_bwd(lr, li, ls, b_re, b_im, dab_re, dab_im, dbb_re, dbb_im, seg, name):
    gp = lr.shape[1]
    h = b_re.shape[0]

    def body(lr_ref, li_ref, ls_ref, br_ref, bi_ref, dar_ref, dai_ref, dbbr_ref, dbbi_ref, seg_ref,
             dlr_ref, dli_ref, dls_ref, dbr_ref, dbi_ref):
        lr_, li_, ls_ = lr_ref[...], li_ref[...], ls_ref[...]
        (ab_re, ab_im, f_re, f_im), vjp = jax.vjp(_disc, lr_, li_, ls_)
        br, bi, dbbr, dbbi = br_ref[...], bi_ref[...], dbbr_ref[...], dbbi_ref[...]
        dbr_ref[...] = dbbr * f_re + dbbi * f_im
        dbi_ref[...] = dbbi * f_re - dbbr * f_im
        df_re = jnp.sum(dbbr * br + dbbi * bi, axis=0, keepdims=True)
        df_im = jnp.sum(dbbi * br - dbbr * bi, axis=0, keepdims=True)
        dlr, dli, dls = vjp((dar_ref[...], dai_ref[...], df_re, df_im))
        dlr_ref[...] = dlr
        dli_ref[...] = dli
        dls8 = jnp.broadcast_to(dls, (8, gp))
        dls_ref[...] = jnp.dot(dls8, seg_ref[...], preferred_element_type=F32, precision=lax.Precision.HIGHEST)

    vec, mat = jax.ShapeDtypeStruct((1, gp), F32), jax.ShapeDtypeStruct((h, gp), F32)
    return pl.pallas_call(body, name=name,
                          out_shape=[vec, vec, jax.ShapeDtypeStruct((8, seg.shape[1]), F32), mat, mat],
                          compiler_params=_params(24 << 20))(
        lr, li, ls, b_re, b_im, dab_re, dab_im, dbb_re, dbb_im, seg)


def _scan_bufs(t_len):
    hs = BLOCK_STATES
    return [pltpu.VMEM((hs // LANES, t_len, LANES), F32), pltpu.VMEM((hs // LANES, t_len, LANES), F32),
            pltpu.VMEM((t_len // SUB, hs), F32), pltpu.VMEM((t_len // SUB, hs), F32)]


def _scan(xr, xi, apow_ref, bufs, t_len, reverse):
    hs = BLOCK_STATES
    n_tiles = t_len // SUB
    sr_ref, si_ref, er_ref, ei_ref = bufs

    def doubling(xr, xi, n_rows, first_pow, within):
        row = lax.broadcasted_iota(jnp.int32, xr.shape, 0) & (within - 1)
        d = 1
        while d < within:
            i = first_pow + d.bit_length() - 1
            pr, pi = apow_ref[i:i + 1, :hs], apow_ref[i:i + 1, hs:]
            if reverse:
                pi, shift, keep = -pi, n_rows - d, row < within - d
            else:
                shift, keep = d, row >= d
            sr = jnp.where(keep, pltpu.roll(xr, shift, 0), 0.0)
            si = jnp.where(keep, pltpu.roll(xi, shift, 0), 0.0)
            xr, xi = xr + pr * sr - pi * si, xi + pr * si + pi * sr
            d *= 2
        return xr, xi

    shape3 = (n_tiles, SUB, hs)
    row = lax.broadcasted_iota(jnp.int32, shape3, 1)
    xr, xi = xr.reshape(shape3), xi.reshape(shape3)
    for i, d in enumerate((1, 2, 4)):
        pr, pi = apow_ref[i:i + 1, :hs], apow_ref[i:i + 1, hs:]
        if reverse:
            pi, shift, keep = -pi, SUB - d, row < SUB - d
        else:
            shift, keep = d, row >= d
        sr = jnp.where(keep, pltpu.roll(xr, shift, 1), 0.0)
        si = jnp.where(keep, pltpu.roll(xi, shift, 1), 0.0)
        xr, xi = xr + pr * sr - pi * si, xi + pr * si + pi * sr
    xr, xi = xr.reshape(t_len, hs), xi.reshape(t_len, hs)
    chunks = [slice(c * LANES, (c + 1) * LANES) for c in range(hs // LANES)]
    for c, lanes in enumerate(chunks):
        sr_ref[c] = xr[:, lanes]
        si_ref[c] = xi[:, lanes]
    edge = pl.ds(0 if reverse else SUB - 1, n_tiles, stride=SUB)
    tr, ti = doubling(jnp.concatenate([sr_ref[c, edge, :] for c in range(len(chunks))], axis=1),
                      jnp.concatenate([si_ref[c, edge, :] for c in range(len(chunks))], axis=1), n_tiles, 3, n_tiles)
    trow = lax.broadcasted_iota(jnp.int32, tr.shape, 0)
    if reverse:
        shift, keep = n_tiles - 1, trow < n_tiles - 1
    else:
        shift, keep = 1, trow >= 1
    er_ref[...] = jnp.where(keep, pltpu.roll(tr, shift, 0), 0.0)
    ei_ref[...] = jnp.where(keep, pltpu.roll(ti, shift, 0), 0.0)
    lin = POW_ROWS + SUB if reverse else POW_ROWS
    mr, mi = apow_ref[lin:lin + SUB, :hs], apow_ref[lin:lin + SUB, hs:]
    if reverse:
        mi = -mi
    for t in range(n_tiles):
        rows = slice(t * SUB, (t + 1) * SUB)
        er, ei = er_ref[t:t + 1, :], ei_ref[t:t + 1, :]
        add_r, add_i = mr * er - mi * ei, mr * ei + mi * er
        for c, lanes in enumerate(chunks):
            sr_ref[c, rows, :] += add_r[:, lanes]
            si_ref[c, rows, :] += add_i[:, lanes]
    return (jnp.concatenate([sr_ref[c] for c in range(len(chunks))], axis=1),
            jnp.concatenate([si_ref[c] for c in range(len(chunks))], axis=1))


def _ssm_chunk(s):
    t_len = _pick(s, 256, 8)
    assert t_len & (t_len - 1) == 0 and t_len <= 1 << POW_ROWS, t_len
    return t_len


def _fold_carry(br, bi, carry_ref, apow_ref, at_row, conj):
    hs = BLOCK_STATES
    cr, ci = carry_ref[0:1, :hs], carry_ref[0:1, hs:]
    ar, ai = apow_ref[0:1, :hs], apow_ref[0:1, hs:]
    if conj:
        ai = -ai
    here = lax.broadcasted_iota(jnp.int32, br.shape, 0) == at_row
    return jnp.where(here, br + (ar * cr - ai * ci), br), jnp.where(here, bi + (ar * ci + ai * cr), bi)


def _ssm_fwd(proj, ucol, bbd, ccat, dskip, apow, t_len, *, name):
    s = proj.shape[0]
    nb = bbd.shape[0]
    nc = s // t_len
    hs = BLOCK_STATES

    def body(u_ref, bbd_ref, ccat_ref, d_ref, apow_ref, y_ref, z_ref, xs_ref, carry_ref, *bufs):
        @pl.when(pl.program_id(1) == 0)
        def _():
            carry_ref[...] = jnp.zeros_like(carry_ref)

        xs_ref[...] = carry_ref[...]
        u = u_ref[...]
        bu = jnp.dot(u.astype(BF16), bbd_ref[...], preferred_element_type=F32)
        br, bi = _fold_carry(bu[:, :hs], bu[:, hs:], carry_ref, apow_ref, 0, False)
        xr, xi = _scan(br, bi, apow_ref, bufs, t_len, False)
        xcat = jnp.concatenate([xr, xi], axis=1)
        carry_ref[...] = jnp.broadcast_to(xcat[t_len - 1:t_len, :], carry_ref.shape)
        y = jnp.dot(xcat.astype(BF16), ccat_ref[...], preferred_element_type=F32) + d_ref[...] * u
        y_ref[...] = y
        z_ref[...] = _gelu(y).astype(z_ref.dtype)

    return pl.pallas_call(
        body, name=name, grid=(nb, nc),
        in_specs=[pl.BlockSpec((t_len, LANES), lambda j, n: (n, ucol + j)),
                  pl.BlockSpec((None, LANES, 2 * hs), lambda j, n: (j, 0, 0)),
                  pl.BlockSpec((None, 2 * hs, LANES), lambda j, n: (j, 0, 0)),
                  pl.BlockSpec((1, LANES), lambda j, n: (0, j)),
                  pl.BlockSpec((None, TAB_ROWS, 2 * hs), lambda j, n: (j, 0, 0))],
        out_specs=[pl.BlockSpec((t_len, LANES), lambda j, n: (n, j)),
                   pl.BlockSpec((t_len, LANES), lambda j, n: (n, j)),
                   pl.BlockSpec((None, None, 8, 2 * hs), lambda j, n: (j, n, 0, 0))],
        out_shape=[jax.ShapeDtypeStruct((s, nb * LANES), F32), jax.ShapeDtypeStruct((s, nb * LANES), BF16),
                   jax.ShapeDtypeStruct((nb, nc, 8, 2 * hs), F32)],
        scratch_shapes=[pltpu.VMEM((8, 2 * hs), F32)] + _scan_bufs(t_len),
        compiler_params=_params(40 << 20),
    )(proj, bbd, ccat, dskip, apow)


def _ssm_bwd(proj, ucol, y, dzd, dz2, xs, bbd, ccat, dskip, apow, t_len, *, name):
    s = proj.shape[0]
    nb = bbd.shape[0]
    nc = s // t_len
    hs = BLOCK_STATES

    def body(u_ref, y_ref, dzd_ref, dz2_ref, xs_ref, bbd_ref, ccat_ref, d_ref, apow_ref,
             du_ref, dbbd_ref, dccat_ref, dd_ref, da_ref, gcarry_ref, *bufs):
        @pl.when(pl.program_id(1) == 0)
        def _():
            gcarry_ref[...] = jnp.zeros_like(gcarry_ref)
            dbbd_ref[...] = jnp.zeros_like(dbbd_ref)
            dccat_ref[...] = jnp.zeros_like(dccat_ref)
            dd_ref[...] = jnp.zeros_like(dd_ref)
            da_ref[...] = jnp.zeros_like(da_ref)

        u = u_ref[...]
        ub = u.astype(BF16)
        dy = (dzd_ref[...] + dz2_ref[...]) * _gelu_grad(y_ref[...])
        dyb = dy.astype(BF16)
        bu = jnp.dot(ub, bbd_ref[...], preferred_element_type=F32)
        br, bi = _fold_carry(bu[:, :hs], bu[:, hs:], xs_ref, apow_ref, 0, False)
        xr, xi = _scan(br, bi, apow_ref, bufs[:4], t_len, False)
        sr, si = xs_ref[0:1, :hs], xs_ref[0:1, hs:]
        dxd = lax.dot_general(dyb, ccat_ref[...], _NT, preferred_element_type=F32)
        dr, di = _fold_carry(dxd[:, :hs], dxd[:, hs:], gcarry_ref, apow_ref, t_len - 1, True)
        gr, gi = _scan(dr, di, apow_ref, bufs[4:], t_len, True)
        gcat = jnp.concatenate([gr, gi], axis=1)
        gcarry_ref[...] = jnp.broadcast_to(gcat[0:1, :], gcarry_ref.shape)
        gb = gcat.astype(BF16)
        du_ref[...] = lax.dot_general(gb, bbd_ref[...], _NT, preferred_element_type=F32) + d_ref[...] * dy
        dbbd_ref[...] += lax.dot_general(ub, gb, _TN, preferred_element_type=F32)
        xb = jnp.concatenate([xr, xi], axis=1).astype(BF16)
        dccat_ref[...] += lax.dot_general(xb, dyb, _TN, preferred_element_type=F32)
        dd_ref[...] += jnp.sum(dy * u, axis=0, keepdims=True)
        first = lax.broadcasted_iota(jnp.int32, xr.shape, 0) == 0
        xpr = jnp.where(first, sr, pltpu.roll(xr, 1, 0))
        xpi = jnp.where(first, si, pltpu.roll(xi, 1, 0))
        dar = jnp.sum(gr * xpr + gi * xpi, axis=0, keepdims=True)
        dai = jnp.sum(gi * xpr - gr * xpi, axis=0, keepdims=True)
        da_ref[...] += jnp.concatenate([dar, dai], axis=1)

    def rows(j, n):
        return nc - 1 - n

    chunk = pl.BlockSpec((t_len, LANES), lambda j, n: (rows(j, n), j))
    return pl.pallas_call(
        body, name=name, grid=(nb, nc),
        in_specs=[pl.BlockSpec((t_len, LANES), lambda j, n: (rows(j, n), ucol + j)), chunk, chunk, chunk,
                  pl.BlockSpec((None, None, 8, 2 * hs), lambda j, n: (j, rows(j, n), 0, 0)),
                  pl.BlockSpec((None, LANES, 2 * hs), lambda j, n: (j, 0, 0)),
                  pl.BlockSpec((None, 2 * hs, LANES), lambda j, n: (j, 0, 0)),
                  pl.BlockSpec((1, LANES), lambda j, n: (0, j)),
                  pl.BlockSpec((None, TAB_ROWS, 2 * hs), lambda j, n: (j, 0, 0))],
        out_specs=[chunk,
                   pl.BlockSpec((None, LANES, 2 * hs), lambda j, n: (j, 0, 0)),
                   pl.BlockSpec((None, 2 * hs, LANES), lambda j, n: (j, 0, 0)),
                   pl.BlockSpec((1, LANES), lambda j, n: (0, j)),
                   pl.BlockSpec((None, 1, 2 * hs), lambda j, n: (j, 0, 0))],
        out_shape=[jax.ShapeDtypeStruct((s, nb * LANES), F32),
                   jax.ShapeDtypeStruct((nb, LANES, 2 * hs), F32),
                   jax.ShapeDtypeStruct((nb, 2 * hs, LANES), F32),
                   jax.ShapeDtypeStruct((1, nb * LANES), F32),
                   jax.ShapeDtypeStruct((nb, 1, 2 * hs), F32)],
        scratch_shapes=[pltpu.VMEM((8, 2 * hs), F32)] + _scan_bufs(t_len) + _scan_bufs(t_len),
        compiler_params=_params(48 << 20),
    )(proj, y, dzd, dz2, xs, bbd, ccat, dskip, apow)


def _to_blocks(a):
    g, p, k = a.shape
    nb = g // GROUPS_PER_BLOCK
    eye = jnp.eye(GROUPS_PER_BLOCK, dtype=a.dtype)
    a4 = a.reshape(nb, GROUPS_PER_BLOCK, p, k)
    out = jnp.einsum("ab,jbpk->jakbp", eye, a4)
    return out.reshape(nb, GROUPS_PER_BLOCK * k, GROUPS_PER_BLOCK * p)


def _from_blocks(d, p, k):
    nb = d.shape[0]
    d5 = d.reshape(nb, GROUPS_PER_BLOCK, k, GROUPS_PER_BLOCK, p)
    eye = jnp.eye(GROUPS_PER_BLOCK, dtype=bool)[None, :, None, :, None]
    diag = jnp.sum(jnp.where(eye, d5, 0.0), axis=1)
    return jnp.transpose(diag, (0, 2, 3, 1)).reshape(nb * GROUPS_PER_BLOCK, p, k)


def _merge_fwd(attn, y, gl, g_a, g_s, name):
    s, wa = attn.shape
    ws = y.shape[1]
    tm = _pick(s, 256, 16)

    def body(a_ref, y_ref, gl_ref, ga_ref, gs_ref, o_ref):
        av = a_ref[...]
        o_ref[:, :wa] = ((av * _rstd(av)) * ga_ref[...]).astype(o_ref.dtype)
        sv = _gelu(y_ref[...]) * jax.nn.sigmoid(gl_ref[...])
        o_ref[:, wa:] = ((sv * _rstd(sv)) * gs_ref[...]).astype(o_ref.dtype)

    return pl.pallas_call(
        body, name=name, grid=(s // tm,),
        in_specs=[_row_spec(tm, wa), _row_spec(tm, ws), _row_spec(tm, ws), _vec_spec(wa), _vec_spec(ws)],
        out_specs=_row_spec(tm, wa + ws), out_shape=jax.ShapeDtypeStruct((s, wa + ws), BF16),
    )(attn, y, gl, g_a, g_s)


def _merge_bwd(dmerged, attn, y, gl, g_a, g_s, name):
    s, wa = attn.shape
    ws = y.shape[1]
    tm = _pick(s, 256, 16)

    def body(dm_ref, a_ref, y_ref, gl_ref, ga_ref, gs_ref, da_ref, dgl_ref, dzd_ref, dga_ref, dgs_ref):
        @pl.when(pl.program_id(0) == 0)
        def _():
            dga_ref[...] = jnp.zeros_like(dga_ref)
            dgs_ref[...] = jnp.zeros_like(dgs_ref)

        dan, dsn = dm_ref[:, :wa], dm_ref[:, wa:]
        av = a_ref[...]
        ra = _rstd(av)
        ahat = av * ra
        dga_ref[...] += jnp.sum(dan * ahat, axis=0, keepdims=True)
        da_ref[...] = _norm_bwd(dan * ga_ref[...], ahat, ra)
        z = _gelu(y_ref[...])
        sig = jax.nn.sigmoid(gl_ref[...])
        sv = z * sig
        rs = _rstd(sv)
        shat = sv * rs
        dgs_ref[...] += jnp.sum(dsn * shat, axis=0, keepdims=True)
        dssm = _norm_bwd(dsn * gs_ref[...], shat, rs)
        dzd_ref[...] = dssm * sig
        dgl_ref[...] = (dssm * z * sig * (1.0 - sig)).astype(dgl_ref.dtype)

    return pl.pallas_call(
        body, name=name, grid=(s // tm,),
        in_specs=[_row_spec(tm, wa + ws), _row_spec(tm, wa), _row_spec(tm, ws), _row_spec(tm, ws),
                  _vec_spec(wa), _vec_spec(ws)],
        out_specs=[_row_spec(tm, wa), _row_spec(tm, ws), _row_spec(tm, ws), _vec_spec(wa), _vec_spec(ws)],
        out_shape=[jax.ShapeDtypeStruct((s, wa), F32), jax.ShapeDtypeStruct((s, ws), BF16),
                   jax.ShapeDtypeStruct((s, ws), F32), jax.ShapeDtypeStruct((1, wa), F32),
                   jax.ShapeDtypeStruct((1, ws), F32)],
    )(dmerged, attn, y, gl, g_a, g_s)


def _shift_down(main, halo, k):
    rolled = pltpu.roll(main, k, 0)
    row = lax.broadcasted_iota(jnp.int32, main.shape, 0)
    for r in range(k):
        rolled = jnp.where(row == r, halo[8 - k + r:8 - k + r + 1, :], rolled)
    return rolled


def _shift_up(main, halo, k):
    tm = main.shape[0]
    rolled = pltpu.roll(main, tm - k, 0)
    row = lax.broadcasted_iota(jnp.int32, main.shape, 0)
    for r in range(k):
        rolled = jnp.where(row == tm - k + r, halo[r:r + 1, :], rolled)
    return rolled


def _conv(main, halo, w_ref, b_ref):
    return (b_ref[...] + w_ref[0:1, :] * _shift_down(main, halo, 2) + w_ref[1:2, :] * _shift_down(main, halo, 1)
            + w_ref[2:3, :] * main)


def _gate_tiles(s, f):
    return _pick(s, 512, 16), _pick(f, 512, LANES)


def _gate_in_specs(tm, tn, nfb, order, last_row_tile=None):
    hb = tm // 8
    if order == "ij":
        ij = lambda a, b: (a, b)
    elif last_row_tile is None:
        ij = lambda a, b: (b, a)
    else:
        ij = lambda a, b: (last_row_tile - b, a)

    def main(off):
        return pl.BlockSpec((tm, tn), lambda a, b: (ij(a, b)[0], ij(a, b)[1] + off))

    def halo(off):
        return pl.BlockSpec((8, tn), lambda a, b: (jnp.maximum(ij(a, b)[0] * hb - 1, 0), ij(a, b)[1] + off))

    def vec(rows, off):
        return pl.BlockSpec((rows, tn), lambda a, b: (0, ij(a, b)[1] + off))

    return [main(0), main(nfb), halo(0), halo(nfb), vec(3, 0), vec(3, nfb), vec(1, 0), vec(1, nfb)]


def _gate_fwd(up0, conv_w, conv_b, name):
    s, f2 = up0.shape
    f = f2 // 2
    tm, tn = _gate_tiles(s, f)
    nfb = f // tn

    def body(v_ref, g_ref, vh_ref, gh_ref, wv_ref, wg_ref, bv_ref, bg_ref, o_ref):
        top = pl.program_id(0) == 0
        vh = jnp.where(top, 0.0, vh_ref[...])
        gh = jnp.where(top, 0.0, gh_ref[...])
        val = _conv(v_ref[...], vh, wv_ref, bv_ref)
        gate = _conv(g_ref[...], gh, wg_ref, bg_ref)
        o_ref[...] = (_gelu(gate) * val).astype(o_ref.dtype)

    return pl.pallas_call(
        body, name=name, grid=(s // tm, nfb),
        in_specs=_gate_in_specs(tm, tn, nfb, "ij"), out_specs=pl.BlockSpec((tm, tn), lambda i, j: (i, j)),
        out_shape=jax.ShapeDtypeStruct((s, f), BF16),
        compiler_params=_params(24 * tm * tn * 4 + (4 << 20)),
    )(up0, up0, up0, up0, conv_w, conv_w, conv_b, conv_b)


def _gate_bwd(up0, conv_w, conv_b, da, name):
    s, f2 = up0.shape
    f = f2 // 2
    tm, tn = _gate_tiles(s, f)
    nfb, ni = f // tn, s // tm

    def body(v_ref, g_ref, vh_ref, gh_ref, wv_ref, wg_ref, bv_ref, bg_ref, da_ref, dup0_ref, dcb_ref, dcw_ref,
             below_ref):
        step = pl.program_id(1)
        top = step == ni - 1

        @pl.when(step == 0)
        def _():
            dcb_ref[...] = jnp.zeros_like(dcb_ref)
            dcw_ref[...] = jnp.zeros_like(dcw_ref)
            below_ref[...] = jnp.zeros_like(below_ref)

        halos = (jnp.where(top, 0.0, vh_ref[...]), jnp.where(top, 0.0, gh_ref[...]))
        mains = (v_ref[...], g_ref[...])
        w_refs = (wv_ref, wg_ref)
        val = _conv(mains[0], halos[0], wv_ref, bv_ref)
        gate = _conv(mains[1], halos[1], wg_ref, bg_ref)
        dav = da_ref[...]
        act, act_grad = _gelu_and_grad(gate)
        dups = (dav * act, (dav * val) * act_grad)
        for half in range(2):
            dup, w_ref = dups[half], w_refs[half]
            below = below_ref[half]
            dup0_ref[half] = (w_ref[2:3, :] * dup + w_ref[1:2, :] * _shift_up(dup, below, 1)
                              + w_ref[0:1, :] * _shift_up(dup, below, 2)).astype(dup0_ref.dtype)
            below_ref[half] = dup[0:8, :]
            dcb_ref[half] += jnp.sum(dup, axis=0, keepdims=True)
            dcw_ref[half, 0:1, :] += jnp.sum(dup * _shift_down(mains[half], halos[half], 2), axis=0, keepdims=True)
            dcw_ref[half, 1:2, :] += jnp.sum(dup * _shift_down(mains[half], halos[half], 1), axis=0, keepdims=True)
            dcw_ref[half, 2:3, :] += jnp.sum(dup * mains[half], axis=0, keepdims=True)

    return pl.pallas_call(
        body, name=name, grid=(nfb, ni),
        in_specs=_gate_in_specs(tm, tn, nfb, "ji", ni - 1) + [pl.BlockSpec((tm, tn), lambda j, i: (ni - 1 - i, j))],
        out_specs=[pl.BlockSpec((2, tm, tn), lambda j, i: (0, ni - 1 - i, j)),
                   pl.BlockSpec((2, 1, tn), lambda j, i: (0, 0, j)),
                   pl.BlockSpec((2, 3, tn), lambda j, i: (0, 0, j))],
        out_shape=[jax.ShapeDtypeStruct((2, s, f), BF16), jax.ShapeDtypeStruct((2, 1, f), F32),
                   jax.ShapeDtypeStruct((2, 3, f), F32)],
        scratch_shapes=[pltpu.VMEM((2, 8, tn), F32)],
        compiler_params=_params(40 * tm * tn * 4 + (4 << 20)),
    )(up0, up0, up0, up0, conv_w, conv_w, conv_b, conv_b, da)


def _adamw(w, g, m, v, name):
    r, c = w.shape
    tr = _pick(r, max(8, (1 << 19) // max(c, 1) // 8 * 8), 8)
    c1, c2 = 1.0 / (1.0 - ADAM_B1 ** ADAM_STEP), 1.0 / (1.0 - ADAM_B2 ** ADAM_STEP)

    def body(w_ref, g_ref, m_ref, v_ref, d_ref, nm_ref, nv_ref):
        gv = g_ref[...]
        nm = ADAM_B1 * m_ref[...] + (1.0 - ADAM_B1) * gv
        nv = ADAM_B2 * v_ref[...] + (1.0 - ADAM_B2) * (gv * gv)
        nm_ref[...] = nm
        nv_ref[...] = nv
        d_ref[...] = -ADAM_LR * ((nm * c1) / (jnp.sqrt(nv * c2) + ADAM_EPS) + ADAM_WD * w_ref[...])

    spec = pl.BlockSpec((tr, c), lambda i: (i, 0))
    out = jax.ShapeDtypeStruct((r, c), F32)
    return pl.pallas_call(body, name=name, grid=(r // tr,), in_specs=[spec] * 4, out_specs=[spec] * 3,
                          out_shape=[out] * 3, compiler_params=_params(14 * tr * c * 4 + (4 << 20)))(w, g, m, v)


def _adamw_many(ws, gs, ms, vs, name):
    n = len(ws)
    c1, c2 = 1.0 / (1.0 - ADAM_B1 ** ADAM_STEP), 1.0 / (1.0 - ADAM_B2 ** ADAM_STEP)

    def body(*refs):
        w_refs, g_refs, m_refs, v_refs = (refs[i * n:(i + 1) * n] for i in range(4))
        d_refs, nm_refs, nv_refs = (refs[(4 + i) * n:(5 + i) * n] for i in range(3))
        for i in range(n):
            gv = g_refs[i][...]
            nm = ADAM_B1 * m_refs[i][...] + (1.0 - ADAM_B1) * gv
            nv = ADAM_B2 * v_refs[i][...] + (1.0 - ADAM_B2) * (gv * gv)
            nm_refs[i][...] = nm
            nv_refs[i][...] = nv
            d_refs[i][...] = -ADAM_LR * ((nm * c1) / (jnp.sqrt(nv * c2) + ADAM_EPS) + ADAM_WD * w_refs[i][...])

    shapes = [jax.ShapeDtypeStruct(w.shape, F32) for w in ws]
    outs = pl.pallas_call(body, name=name, out_shape=shapes * 3, compiler_params=_params(48 << 20))(
        *ws, *gs, *ms, *vs)
    return outs[:n], outs[n:2 * n], outs[2 * n:]


def _adamw_nd(w, g, m, v, name):
    shape = w.shape
    c = shape[-1]
    outs = _adamw(w.reshape(-1, c), g.reshape(-1, c), m.reshape(-1, c), v.reshape(-1, c), name)
    return [o.reshape(shape) for o in outs]


BIG = ("w_in", "w_glu", "w_out", "w_up", "w_down")
SMALL = ("b_ada", "g_pre_mix", "g_post_mix", "attn_sinks", "lam_re", "lam_im", "log_step", "ssm_b_re", "ssm_b_im",
         "ssm_c_re", "ssm_c_im", "ssm_d", "g_attn_out", "g_ssm_out", "g_pre_ffn", "g_post_ffn", "conv_b")
ORDER = ("w_ada", "b_ada", "g_pre_mix", "g_post_mix", "w_in", "attn_sinks", "lam_re", "lam_im", "log_step",
         "ssm_b_re", "ssm_b_im", "ssm_c_re", "ssm_c_im", "ssm_d", "w_glu", "g_attn_out", "g_ssm_out", "w_out",
         "g_pre_ffn", "g_post_ffn", "w_up", "conv_w", "conv_b", "w_down")
COL_SHARDED = ("w_in", "w_up")


def kernel(x, c, w_ada, b_ada, g_pre_mix, g_post_mix, w_in, attn_sinks, lam_re, lam_im, log_step, ssm_b_re, ssm_b_im, ssm_c_re, ssm_c_im, ssm_d, w_glu, g_attn_out, g_ssm_out, w_out, g_pre_ffn, g_post_ffn, w_up, conv_w, conv_b, w_down, loss_target, m_w_ada, m_b_ada, m_g_pre_mix, m_g_post_mix, m_w_in, m_attn_sinks, m_lam_re, m_lam_im, m_log_step, m_ssm_b_re, m_ssm_b_im, m_ssm_c_re, m_ssm_c_im, m_ssm_d, m_w_glu, m_g_attn_out, m_g_ssm_out, m_w_out, m_g_pre_ffn, m_g_post_ffn, m_w_up, m_conv_w, m_conv_b, m_w_down, v_w_ada, v_b_ada, v_g_pre_mix, v_g_post_mix, v_w_in, v_attn_sinks, v_lam_re, v_lam_im, v_log_step, v_ssm_b_re, v_ssm_b_im, v_ssm_c_re, v_ssm_c_im, v_ssm_d, v_w_glu, v_g_attn_out, v_g_ssm_out, v_w_out, v_g_pre_ffn, v_g_post_ffn, v_w_up, v_conv_w, v_conv_b, v_w_down):
    env = dict(locals())
    W = {n: env[n] for n in ORDER}
    M = {n: env["m_" + n] for n in ORDER}
    V = {n: env["v_" + n] for n in ORDER}

    depth = w_ada.shape[0]
    s, d = x.shape[1], x.shape[2]
    xs0 = x.reshape(s, d)
    tgt = loss_target.reshape(s, d)
    attn_w = d // 2
    ssm_w = d - attn_w
    in_cols = w_in.shape[2] * N_DEV
    kv_dim = (in_cols - attn_w - ssm_w) // 2
    n_q, n_kv = attn_w // HEAD_DIM, kv_dim // HEAD_DIM
    n_grp = ssm_w // SSM_GROUP
    nb = ssm_w // LANES
    f = w_down.shape[1] * N_DEV
    ucol = (attn_w + 2 * kv_dim) // LANES
    t_len = _ssm_chunk(s)
    me = 4 * lax.axis_index("x") + 2 * lax.axis_index("y") + lax.axis_index("c")

    def at_block(ref, idx):
        return ref.at[idx]

    def at_rows(n_rows):
        return lambda ref, idx: ref.at[:, pl.ds(pl.multiple_of(idx * n_rows, 8), n_rows), :]

    def at_cols(n_cols):
        return lambda ref, idx: ref.at[:, :, pl.ds(pl.multiple_of(idx * n_cols, LANES), n_cols)]

    first = _gather_multi([w_in.astype(BF16), conv_w, c],
                          [(N_DEV,) + w_in.shape, (N_DEV,) + conv_w.shape, (N_DEV,) + c.shape],
                          [at_block, at_block, at_block], "ag_first")
    w_in_full = _cols_from_blocks(first[0], "w_in_layout")
    conv_w_full = jnp.transpose(first[1], (1, 2, 0, 3)).reshape(depth, 3, 2 * f)
    c_all = first[2].reshape(N_DEV, d)

    def at_rows2(n_rows):
        return lambda ref, idx: ref.at[pl.ds(pl.multiple_of(idx * n_rows, 8), n_rows), :]

    def at_cols2(n_cols):
        return lambda ref, idx: ref.at[:, pl.ds(pl.multiple_of(idx * n_cols, LANES), n_cols)]

    def whole(ref, idx):
        return ref

    def gather_kind(n):
        return "blk" if n == "w_in" else "cols" if n in COL_SHARDED else "rows"

    def gather_view(n):
        return {"blk": at_block, "cols": at_cols2(W[n].shape[2]), "rows": at_rows2(W[n].shape[1])}[gather_kind(n)]

    def gather_shape(n):
        _, a, b = W[n].shape
        return {"blk": (N_DEV, a, b), "cols": (a, N_DEV * b), "rows": (N_DEV * a, b)}[gather_kind(n)]

    later = [(n, l) for l in range(depth) for n in BIG[1:]]
    later_srcs = [W[n][l].astype(BF16) for n, l in later]
    later_views = [gather_view(n) for n, _ in later]
    me_arr = me.astype(jnp.int32).reshape(1)
    lands = [_place_own(me_arr, src, lax.empty(gather_shape(n), BF16), gather_kind(n), f"ag_own_{n}{l}")
             for (n, l), src in zip(later, later_srcs)]
    ag_started, ag_token = _exchange_start(later_srcs, lands, [whole] * len(later), later_views, "ag_start")

    def weights_arrived(names, l, after, name):
        picks = [later.index((n, l)) for n in names]
        _, got = _exchange_wait([ag_started[i] for i in picks], [after], [whole] * len(picks),
                                [later_views[i] for i in picks], name)
        return dict(zip(names, got))

    c_pad = jnp.pad(c_all, ((0, 16 - N_DEV), (0, 0)))
    n_ada = w_ada.shape[2]
    b_shard = lax.dynamic_slice_in_dim(b_ada, me * n_ada, n_ada, axis=1).reshape(depth, 1, n_ada)
    ada_part, c_act = _ada_fwd(c_pad, w_ada, b_shard, "ada_fwd")
    ada_all = _all_gather(ada_part.reshape(depth * 16, n_ada), "ag_ada").reshape(N_DEV, depth, 16, n_ada)
    ada_me = lax.dynamic_index_in_dim(ada_all, me, axis=2, keepdims=False)
    ada = jnp.transpose(ada_me, (1, 0, 2)).reshape(depth, 6, 1, d) + ag_token[0, 0]

    gp = n_grp * STATE

    def hgp(a):
        return jnp.transpose(a, (2, 0, 1)).reshape(SSM_GROUP, gp)

    ssm = []
    for l in range(depth):
        lr, li = lam_re[l].reshape(1, gp), lam_im[l].reshape(1, gp)
        ls = jnp.repeat(log_step[l], STATE).reshape(1, gp)
        br, bi = hgp(ssm_b_re[l]), hgp(ssm_b_im[l])
        bbr, bbi, tab_r, tab_i = _ssm_params_fwd(lr, li, ls, br, bi, f"ssm_params_fwd{l}")
        bb_re = jnp.transpose(bbr.reshape(SSM_GROUP, n_grp, STATE), (1, 2, 0))
        bb_im = jnp.transpose(bbi.reshape(SSM_GROUP, n_grp, STATE), (1, 2, 0))
        bbd = jnp.concatenate([_to_blocks(bb_re), _to_blocks(bb_im)], axis=2).astype(BF16)
        c_re_t = jnp.transpose(ssm_c_re[l], (0, 2, 1))
        c_im_t = jnp.transpose(ssm_c_im[l], (0, 2, 1))
        ccat = jnp.concatenate([jnp.transpose(_to_blocks(c_re_t), (0, 2, 1)),
                                -jnp.transpose(_to_blocks(c_im_t), (0, 2, 1))], axis=1).astype(BF16)

        def tab(t):
            return t.reshape(TAB_ROWS, nb, BLOCK_STATES)

        apow = jnp.transpose(jnp.concatenate([tab(tab_r), tab(tab_i)], axis=2), (1, 0, 2))
        ssm.append(dict(lr=lr, li=li, ls=ls, br=br, bi=bi, bbd=bbd, ccat=ccat, apow=apow,
                        dskip=ssm_d[l].reshape(1, ssm_w)))

    sinks_pad = jnp.pad(attn_sinks, ((0, 0), (0, LANES - n_q)))

    def vec(a):
        return a.reshape(1, -1)

    saved = []
    fw = [dict() for _ in range(depth)]
    xin = xs0
    for l in range(depth):
        sh_m, sc_m, gt_m, sh_f, sc_f, gt_f = (ada[l, i] for i in range(6))
        p = ssm[l]
        if l == 0:
            h1 = _modnorm_fwd(xin, vec(g_pre_mix[l]), sc_m, sh_m, f"modnorm_mix_fwd{l}")
        proj = _matmul(h1, w_in_full[l], name=f"mm_in{l}")
        attn, lse = _attn_fwd(proj, sinks_pad[l:l + 1], n_q=n_q, n_kv=n_kv, name=f"attn_fwd{l}")
        y, z, xstart = _ssm_fwd(proj, ucol, p["bbd"], p["ccat"], p["dskip"], p["apow"], t_len, name=f"ssm_fwd{l}")
        fw[l].update(weights_arrived(("w_glu", "w_out"), l, z, f"ag_wait_mix{l}"))
        gl = _matmul(z, fw[l]["w_glu"], name=f"mm_glu{l}")
        merged = _merge_fwd(attn, y, gl, vec(g_attn_out[l]), vec(g_ssm_out[l]), f"merge_fwd{l}")
        mix = _matmul(merged, fw[l]["w_out"], name=f"mm_out{l}")
        x2, h2 = _resnorm_modnorm_fwd(xin, mix, vec(g_post_mix[l]), gt_m, vec(g_pre_ffn[l]), sc_f, sh_f,
                                      f"resnorm_mix_fwd{l}")
        fw[l].update(weights_arrived(("w_up",), l, h2, f"ag_wait_up{l}"))
        up0 = _matmul(h2, fw[l]["w_up"], name=f"mm_up{l}")
        cw, cb = conv_w_full[l], vec(conv_b[l])
        act = _gate_fwd(up0, cw, cb, f"gate_fwd{l}")
        fw[l].update(weights_arrived(("w_down",), l, act, f"ag_wait_down{l}"))
        ff = _matmul(act, fw[l]["w_down"], name=f"mm_down{l}")
        saved.append(dict(xin=xin, h1=h1, proj=proj, attn=attn, lse=lse, y=y, z=z, xstart=xstart, gl=gl,
                          merged=merged, mix=mix, x2=x2, h2=h2, up0=up0, act=act, ff=ff))
        if l + 1 < depth:
            xin, h1 = _resnorm_modnorm_fwd(x2, ff, vec(g_post_ffn[l]), gt_f, vec(g_pre_mix[l + 1]), ada[l + 1, 1],
                                           ada[l + 1, 0], f"resnorm_ffn_fwd{l}")
        else:
            dxo, loss_acc = _resnorm_loss(x2, ff, vec(g_post_ffn[l]), gt_f, tgt, "resnorm_ffn_loss")
    loss = lax.psum(loss_acc[0, 0], ("x", "y", "c"))

    grads = {n: [None] * depth for n in ORDER}
    dada = [None] * depth
    big_blocks = {n: [None] * depth for n in BIG}
    seg = jnp.pad(jnp.repeat(jnp.eye(n_grp, dtype=F32), STATE, axis=0), ((0, 0), (0, (-n_grp) % LANES)))

    def part_view(n):
        shp = W[n].shape
        if n == "w_in":
            return at_block, "blk"
        if n in COL_SHARDED:
            return at_cols2(shp[2]), "cols"
        return at_rows2(shp[1]), "rows"

    rs_groups, start_tokens = [], []
    small_order = SMALL + ("conv_w",)
    small_shapes = {n: W[n].shape for n in SMALL}
    small_shapes["conv_w"] = (depth, 3, 2 * f)
    small_started = [None] * depth

    def send_partials(items, name):
        parts = [big_blocks[n][l] for n, l in items]
        lands = [lax.empty((N_DEV,) + W[n].shape[1:], BF16) for n, _ in items]
        started, token = _exchange_start(parts, lands, [part_view(n)[0] for n, _ in items],
                                         [at_block] * len(items), name)
        rs_groups.append((items, started, name))
        start_tokens.append(token)
        return token[0, 0]

    order = jnp.zeros((), F32)
    for l in reversed(range(depth)):
        sh_m, sc_m, gt_m, sh_f, sc_f, gt_f = (ada[l, i] for i in range(6))
        gt_f = gt_f + order
        a, p = saved[l], ssm[l]
        cw, cb = conv_w_full[l], vec(conv_b[l])
        dff, dg, dgt_f = _resnorm_bwd(dxo, a["ff"], vec(g_post_ffn[l]), gt_f, f"resnorm_ffn_bwd{l}")
        grads["g_post_ffn"][l] = dg
        dact = _matmul(dff, fw[l]["w_down"], tb=True, name=f"mm_down_dx{l}")
        big_blocks["w_down"][l] = _matmul(a["act"], dff, ta=True, out_dtype=BF16, name=f"mm_down_dw{l}")
        dup0, dcb, dcw = _gate_bwd(a["up0"], cw, cb, dact, f"gate_bwd{l}")
        grads["conv_b"][l] = dcb.reshape(1, 2 * f)
        grads["conv_w"][l] = jnp.transpose(dcw, (1, 0, 2)).reshape(3, 2 * f)
        dh2 = _matmul(dup0, fw[l]["w_up"], tb=True, a_halves=True, name=f"mm_up_dx{l}")
        big_blocks["w_up"][l] = _matmul(a["h2"], dup0, ta=True, b_halves=True, out_dtype=BF16,
                                        name=f"mm_up_dw{l}")
        if l == 0:
            sc_f = sc_f + send_partials([("w_down", 0), ("w_up", 0)], "rs_start_ffn0")
        dx2, dg, dsc_f, dsh_f = _modnorm_bwd(dh2, a["x2"], vec(g_pre_ffn[l]), sc_f, dxo, f"modnorm_ffn_bwd{l}")
        grads["g_pre_ffn"][l] = dg
        dmix, dg, dgt_m = _resnorm_bwd(dx2, a["mix"], vec(g_post_mix[l]), gt_m, f"resnorm_mix_bwd{l}")
        grads["g_post_mix"][l] = dg
        dmerged = _matmul(dmix, fw[l]["w_out"], tb=True, name=f"mm_out_dx{l}")
        big_blocks["w_out"][l] = _matmul(a["merged"], dmix, ta=True, out_dtype=BF16, name=f"mm_out_dw{l}")
        dattn, dgl, dzd, dga, dgs = _merge_bwd(dmerged, a["attn"], a["y"], a["gl"], vec(g_attn_out[l]),
                                               vec(g_ssm_out[l]), f"merge_bwd{l}")
        grads["g_attn_out"][l], grads["g_ssm_out"][l] = dga, dgs
        dz2 = _matmul(dgl, fw[l]["w_glu"], tb=True, name=f"mm_glu_dx{l}")
        big_blocks["w_glu"][l] = _matmul(a["z"], dgl, ta=True, out_dtype=BF16, name=f"mm_glu_dw{l}")
        dskip = p["dskip"]
        if l == 0:
            dskip = dskip + send_partials([("w_out", 0), ("w_glu", 0)], "rs_start_mix0")
        du, dbbd, dccat, dd, da = _ssm_bwd(a["proj"], ucol, a["y"], dzd, dz2, a["xstart"], p["bbd"], p["ccat"],
                                           dskip, p["apow"], t_len, name=f"ssm_bwd{l}")
        grads["ssm_d"][l] = dd
        hs = BLOCK_STATES
        dbb_re = _from_blocks(dbbd[:, :, :hs], STATE, SSM_GROUP)
        dbb_im = _from_blocks(dbbd[:, :, hs:], STATE, SSM_GROUP)
        dccat_t = jnp.transpose(dccat, (0, 2, 1))
        grads["ssm_c_re"][l] = jnp.transpose(_from_blocks(dccat_t[:, :, :hs], STATE, SSM_GROUP), (0, 2, 1))
        grads["ssm_c_im"][l] = -jnp.transpose(_from_blocks(dccat_t[:, :, hs:], STATE, SSM_GROUP), (0, 2, 1))
        dab_re, dab_im = da[:, 0, :hs].reshape(1, gp), da[:, 0, hs:].reshape(1, gp)
        dlr, dli, dls, dbr, dbi = _ssm_params_bwd(p["lr"], p["li"], p["ls"], p["br"], p["bi"], dab_re, dab_im,
                                                  hgp(dbb_re), hgp(dbb_im), seg, f"ssm_params_bwd{l}")
        grads["lam_re"][l], grads["lam_im"][l], grads["log_step"][l] = dlr, dli, dls[0, :n_grp]
        grads["ssm_b_re"][l] = jnp.transpose(dbr.reshape(SSM_GROUP, n_grp, STATE), (1, 2, 0))
        grads["ssm_b_im"][l] = jnp.transpose(dbi.reshape(SSM_GROUP, n_grp, STATE), (1, 2, 0))
        dq, dk, dv, dsink = _attn_bwd(a["proj"], sinks_pad[l:l + 1], a["attn"], a["lse"], dattn,
                                      n_q=n_q, n_kv=n_kv, name=f"attn_bwd{l}")
        grads["attn_sinks"][l] = dsink[0, :n_q]
        dproj = jnp.concatenate([dq, dk, dv, du], axis=1).astype(BF16)
        dh1 = _matmul(dproj, w_in_full[l], tb=True, name=f"mm_in_dx{l}")
        big_blocks["w_in"][l] = _blocks_from_cols(_matmul(a["h1"], dproj, ta=True, name=f"mm_in_dw{l}"),
                                                  f"w_in_grad_layout{l}")
        dxo, dg, dsc_m, dsh_m = _modnorm_bwd(dh1, a["xin"], vec(g_pre_mix[l]), sc_m, dx2, f"modnorm_mix_bwd{l}")
        grads["g_pre_mix"][l] = dg
        dada[l] = jnp.concatenate([dsh_m, dsc_m, dgt_m, dsh_f, dsc_f, dgt_f], axis=1)
        if l > 0:
            order = send_partials([(n, l) for n in reversed(BIG)], f"rs_start_layer{l}")
        else:
            order = order + send_partials([("w_in", 0)], "rs_start_in0")
        spack = _pack([dada[l]] + [grads[n][l] for n in small_order[1:]], F32, 1024)
        started, token = _exchange_start([spack], [lax.empty((N_DEV,) + spack.shape, F32)], [whole], [at_block],
                                         f"small_start{l}")
        small_started[l] = started
        start_tokens.append(token)
        order = order + token[0, 0]
    grad_x = dxo.reshape(x.shape)

    delta, new_m, new_v = {}, {}, {}
    stacked = {n: None for n in BIG}
    landed_layers = {n: 0 for n in BIG}
    after = [dxo] + start_tokens
    for items, started, name in rs_groups:
        mine, landed = _exchange_wait(started, after, [part_view(n)[0] for n, _ in items], [at_block] * len(items),
                                      name.replace("start", "wait"))
        for (n, l), part, slots in zip(items, mine, landed):
            stacked[n] = _sum_slots_own(me_arr, slots, part, part_view(n)[1], f"rs_sum_{n}{l}", layer=l,
                                        n_layers=depth, stacked=stacked[n])
            landed_layers[n] += 1
            if landed_layers[n] == depth:
                grads[n] = stacked[n]
                delta[n], new_m[n], new_v[n] = _adamw_nd(W[n], grads[n], M[n], V[n], f"adamw_{n}")
                after.append(delta[n])

    n_cw = conv_w.shape[2]
    small_sums, dada_rows = [None] * depth, [None] * depth
    for l in reversed(range(depth)):
        mine, landed = _exchange_wait(small_started[l], after, [whole], [at_block], f"small_wait{l}")
        ssum = _sum_slots_own(me_arr, landed[0], mine[0], "self", f"sum_small{l}").reshape(-1)
        small_sums[l] = _unpack(ssum, [small_shapes[n][1:] for n in small_order])
        slot = lax.broadcasted_iota(jnp.int32, (N_DEV, 6 * d), 0)
        dada_rows[l] = jnp.where(slot == me, mine[0].reshape(-1)[:6 * d][None],
                                 landed[0].reshape(N_DEV, -1)[:, :6 * d])
    for i, n in enumerate(small_order):
        grads[n] = jnp.stack([small_sums[l][i] for l in range(depth)])
    grads["conv_w"] = lax.dynamic_slice_in_dim(grads["conv_w"], me * n_cw, n_cw, axis=2)
    dada_all = jnp.stack(dada_rows, axis=1)
    dada_shard = lax.dynamic_slice_in_dim(dada_all, me * n_ada, n_ada, axis=2)
    kp = LANES
    dada_pad = jnp.pad(jnp.transpose(dada_shard, (1, 0, 2)), ((0, 0), (0, kp - N_DEV), (0, 0)))
    act_t = jnp.pad(jnp.transpose(c_act[:N_DEV]), ((0, 0), (0, kp - N_DEV)))
    grads["w_ada"] = _ada_wgrad(act_t, dada_pad, "ada_wgrad")

    delta["w_ada"], new_m["w_ada"], new_v["w_ada"] = _adamw_nd(W["w_ada"], grads["w_ada"], M["w_ada"], V["w_ada"],
                                                                "adamw_w_ada")

    def lane_friendly(a):
        return a.reshape(-1, 1024) if a.ndim > 2 and a.shape[-1] < LANES and a.size % 1024 == 0 else a

    rest = SMALL + ("conv_w",)
    outs = _adamw_many(*[[lane_friendly(t[n]) for n in rest] for t in (W, grads, M, V)], "adamw_small")
    for tgt_d, vals in zip((delta, new_m, new_v), outs):
        for n, val in zip(rest, vals):
            tgt_d[n] = val.reshape(W[n].shape)

    return (loss, grad_x, *[grads[n] for n in ORDER], *[delta[n] for n in ORDER],
            *[new_m[n] for n in ORDER], *[new_v[n] for n in ORDER])
```

```python
import math

import jax
import jax.numpy as jnp
from jax import lax
from jax.experimental import pallas as pl
from jax.experimental.pallas import tpu as pltpu

F32 = jnp.float32
BF16 = jnp.bfloat16

N_DEV = 8
HEAD_DIM = 64
WINDOW = 128
SSM_GROUP = 16
STATE = 64
LANES = 128
GROUPS_PER_BLOCK = LANES // SSM_GROUP
BLOCK_STATES = GROUPS_PER_BLOCK * STATE
EPS = 1e-6
NEG = -1e30
ADAM_LR, ADAM_B1, ADAM_B2, ADAM_EPS, ADAM_WD, ADAM_STEP = 0.001, 0.9, 0.999, 1e-08, 0.01, 10
VMEM_BYTES_V7X = 64 * 1024 * 1024
GELU_C = math.sqrt(2.0 / math.pi)
MESH = pl.DeviceIdType.MESH
ANY = pl.BlockSpec(memory_space=pl.ANY)


def _pick(n, pref, align):
    t = (min(pref, n) // align) * align
    while t >= align:
        if n % t == 0:
            return t
        t -= align
    return n


def _params(vmem_bytes=None):
    if vmem_bytes is None:
        return pltpu.CompilerParams()
    return pltpu.CompilerParams(vmem_limit_bytes=int(min(vmem_bytes, VMEM_BYTES_V7X - (8 << 20))))


def _gelu_and_grad(x):
    x2 = x * x
    half_x = 0.5 * x
    th = jnp.tanh((GELU_C * x) * (1.0 + 0.044715 * x2))
    one_th = 1.0 + th
    grad = 0.5 * one_th + (half_x * (1.0 - th * th)) * (GELU_C + (3.0 * 0.044715 * GELU_C) * x2)
    return half_x * one_th, grad


def _gelu(x):
    return _gelu_and_grad(x)[0]


def _gelu_grad(x):
    return _gelu_and_grad(x)[1]


def _rstd(x):
    return lax.rsqrt(jnp.mean(x * x, axis=-1, keepdims=True) + EPS)


def _norm_bwd(dhat, xhat, r):
    return r * (dhat - xhat * jnp.mean(dhat * xhat, axis=-1, keepdims=True))


def _matmul(a, b, *, ta=False, tb=False, a_halves=False, b_halves=False, out_dtype=F32, name):
    assert not (a_halves and ta) and not (b_halves and tb)
    if a_halves:
        m, kdim = a.shape[1], 2 * a.shape[2]
    else:
        (kdim, m) = a.shape if ta else a.shape[::-1]
    if b_halves:
        k2, n = b.shape[1], 2 * b.shape[2]
    else:
        (n, k2) = b.shape if tb else b.shape[::-1]
    assert kdim == k2, (a.shape, b.shape, ta, tb)
    tm = _pick(m, 1024, LANES)
    tn = _pick(n // 2, 1536, LANES) if b_halves else _pick(n, 1024, LANES)
    tk = _pick(kdim // 2, 2816, LANES) if a_halves else _pick(kdim, 2816, LANES)
    nk = kdim // tk
    dn = (((0 if ta else 1,), (1 if tb else 0,)), ((), ()))

    def partial_product(a_ref, b_ref):
        return lax.dot_general(a_ref[...].astype(BF16), b_ref[...].astype(BF16), dn, preferred_element_type=F32)

    def body_one(a_ref, b_ref, o_ref):
        o_ref[...] = partial_product(a_ref, b_ref).astype(o_ref.dtype)

    def body_acc(a_ref, b_ref, o_ref, acc_ref):
        k = pl.program_id(2)

        @pl.when(k == 0)
        def _():
            acc_ref[...] = partial_product(a_ref, b_ref)

        @pl.when((k > 0) & (k < nk - 1))
        def _():
            acc_ref[...] += partial_product(a_ref, b_ref)

        @pl.when(k == nk - 1)
        def _():
            o_ref[...] = (acc_ref[...] + partial_product(a_ref, b_ref)).astype(o_ref.dtype)

    body = body_one if nk == 1 else body_acc
    a_spec = pl.BlockSpec((tk, tm), lambda i, j, k: (k, i)) if ta else pl.BlockSpec((tm, tk), lambda i, j, k: (i, k))
    b_spec = pl.BlockSpec((tn, tk), lambda i, j, k: (j, k)) if tb else pl.BlockSpec((tk, tn), lambda i, j, k: (k, j))
    if a_halves:
        nkh = nk // 2
        a_spec = pl.BlockSpec((None, tm, tk), lambda i, j, k: (k // nkh, i, k % nkh))
    if b_halves:
        njh = n // tn // 2
        b_spec = pl.BlockSpec((None, tk, tn), lambda i, j, k: (j // njh, k, j % njh))
    vmem = (2 * (tm * tk * a.dtype.itemsize + tk * tn * b.dtype.itemsize) + tm * tn * 4
            + 2 * tm * tn * jnp.dtype(out_dtype).itemsize + 3 * tm * tn * 4 + (4 << 20))
    return pl.pallas_call(
        body, name=name, grid=(m // tm, n // tn, nk),
        in_specs=[a_spec, b_spec], out_specs=pl.BlockSpec((tm, tn), lambda i, j, k: (i, j)),
        out_shape=jax.ShapeDtypeStruct((m, n), out_dtype),
        scratch_shapes=[] if nk == 1 else [pltpu.VMEM((tm, tn), F32)],
        compiler_params=_params(vmem),
    )(a, b)


def _all_gather(x, name):
    def body(x_ref, out_ref, send_sems, recv_sems, local_sem):
        x_, y_, c_ = lax.axis_index("x"), lax.axis_index("y"), lax.axis_index("c")
        me, sibling = (x_, y_, c_), (x_, y_, 1 - c_)
        chips = [(1 - x_, y_), (x_, 1 - y_), (1 - x_, 1 - y_)]

        def slot(px, py, pc):
            return out_ref.at[4 * px + 2 * py + pc]

        def copy(k, block, to, src=None):
            return pltpu.make_async_remote_copy(
                src_ref=slot(*block) if src is None else src, dst_ref=slot(*block),
                send_sem=send_sems.at[k], recv_sem=recv_sems.at[k], device_id=to, device_id_type=MESH)

        mine = pltpu.make_async_copy(x_ref, slot(*me), local_sem)
        mine.start()
        first = [copy(0, me, sibling, src=x_ref)]
        first += [copy(1 + j, me, (*chip, c_), src=x_ref) for j, chip in enumerate(chips)]
        for cp in first:
            cp.start()
        passed = [copy(4 + j, (*chip, c_), sibling) for j, chip in enumerate(chips)]
        for j, chip in enumerate(chips):
            copy(1 + j, (*chip, c_), me).wait_recv()
            passed[j].start()
        copy(0, sibling, me).wait_recv()
        for j, chip in enumerate(chips):
            copy(4 + j, (*chip, 1 - c_), me).wait_recv()
        for cp in first + passed:
            cp.wait_send()
        mine.wait()

    return pl.pallas_call(
        body, name=name, out_shape=jax.ShapeDtypeStruct((N_DEV,) + x.shape, x.dtype),
        in_specs=[ANY], out_specs=ANY,
        scratch_shapes=[pltpu.SemaphoreType.DMA((7,)), pltpu.SemaphoreType.DMA((7,)), pltpu.SemaphoreType.DMA],
    )(x)


def _gather_multi(srcs, out_shapes, views, name):
    n = len(srcs)

    def body(*refs):
        src_refs, out_refs = refs[:n], refs[n:2 * n]
        send_sems, recv_sems, local_sems = refs[2 * n:]
        x_, y_, c_ = lax.axis_index("x"), lax.axis_index("y"), lax.axis_index("c")
        me, sibling = (x_, y_, c_), (x_, y_, 1 - c_)
        chips = [(1 - x_, y_), (x_, 1 - y_), (1 - x_, 1 - y_)]

        def slot(i, px, py, pc):
            return views[i](out_refs[i], 4 * px + 2 * py + pc)

        def copy(i, k, block, to, from_src=False):
            return pltpu.make_async_remote_copy(
                src_ref=src_refs[i] if from_src else slot(i, *block), dst_ref=slot(i, *block),
                send_sem=send_sems.at[7 * i + k], recv_sem=recv_sems.at[7 * i + k], device_id=to, device_id_type=MESH)

        mine = [pltpu.make_async_copy(src_refs[i], slot(i, *me), local_sems.at[i]) for i in range(n)]
        for cp in mine:
            cp.start()
        first = []
        for i in range(n):
            first.append(copy(i, 0, me, sibling, True))
            first += [copy(i, 1 + j, me, (*chip, c_), True) for j, chip in enumerate(chips)]
        for cp in first:
            cp.start()
        passed = []
        for j, chip in enumerate(chips):
            for i in range(n):
                copy(i, 1 + j, (*chip, c_), me).wait_recv()
                fwd = copy(i, 4 + j, (*chip, c_), sibling)
                fwd.start()
                passed.append(fwd)
        for i in range(n):
            copy(i, 0, sibling, me).wait_recv()
            for j, chip in enumerate(chips):
                copy(i, 4 + j, (*chip, 1 - c_), me).wait_recv()
        for cp in first + passed:
            cp.wait_send()
        for cp in mine:
            cp.wait()

    return pl.pallas_call(
        body, name=name, out_shape=[jax.ShapeDtypeStruct(s, a.dtype) for s, a in zip(out_shapes, srcs)],
        in_specs=[ANY] * n, out_specs=[ANY] * n,
        scratch_shapes=[pltpu.SemaphoreType.DMA((7 * n,)), pltpu.SemaphoreType.DMA((7 * n,)),
                        pltpu.SemaphoreType.DMA((n,))],
    )(*srcs)


HBM_SPEC = pl.BlockSpec(memory_space=pltpu.HBM)
SEM_SPEC = pl.BlockSpec(memory_space=pltpu.SEMAPHORE)
SIDE_EFFECT = pltpu.SideEffectType.DATAFLOW_SIDE_EFFECTING
N_PEERS = N_DEV - 1


def _peer(k, x_, y_, c_):
    px = 1 - x_ if (k >> 2) & 1 else x_
    py = 1 - y_ if (k >> 1) & 1 else y_
    pc = 1 - c_ if k & 1 else c_
    return (px, py, pc), 4 * px + 2 * py + pc


def _exchange_copies(src_refs, land_refs, send_sems, recv_sems, src_views, dst_views):
    x_, y_, c_ = lax.axis_index("x"), lax.axis_index("y"), lax.axis_index("c")
    me = 4 * x_ + 2 * y_ + c_
    out = []
    for i in range(len(src_refs)):
        for k in range(1, N_DEV):
            peer, idx = _peer(k, x_, y_, c_)

            def copy(dst_slot, i=i, k=k, peer=peer, idx=idx):
                return pltpu.make_async_remote_copy(
                    src_ref=src_views[i](src_refs[i], idx), dst_ref=dst_views[i](land_refs[i], dst_slot),
                    send_sem=send_sems[i].at[k - 1], recv_sem=recv_sems[i].at[k - 1], device_id=peer,
                    device_id_type=MESH)

            out.append((copy(me), copy(idx)))
    return out


def _exchange_start(srcs, lands, src_views, dst_views, name):
    n = len(srcs)

    def body(*refs):
        src_refs, land_refs = refs[:n], refs[n:2 * n]
        send_sems, recv_sems = refs[2 * n:3 * n], refs[3 * n:4 * n]
        token = refs[-1]
        for send, _ in _exchange_copies(src_refs, land_refs, send_sems, recv_sems, src_views, dst_views):
            send.start()
        token[...] = jnp.zeros_like(token)

    sems = [pltpu.SemaphoreType.DMA((N_PEERS,))] * n
    thru = [pltpu.HBM(a.shape, a.dtype) for a in list(srcs) + list(lands)]
    outs = pl.pallas_call(
        body, name=name, out_shape=sems + sems + thru + [jax.ShapeDtypeStruct((8, LANES), F32)],
        in_specs=[HBM_SPEC] * (2 * n),
        out_specs=[SEM_SPEC] * (2 * n) + [HBM_SPEC] * (2 * n) + [pl.BlockSpec(memory_space=pltpu.VMEM)],
        input_output_aliases={j: 2 * n + j for j in range(2 * n)},
        compiler_params=pltpu.CompilerParams(has_side_effects=SIDE_EFFECT),
    )(*[pltpu.with_memory_space_constraint(a, pltpu.HBM) for a in list(srcs) + list(lands)])
    per_array = [(outs[j], outs[n + j], outs[2 * n + j], outs[3 * n + j]) for j in range(n)]
    return per_array, outs[-1]


def _exchange_wait(started, after, src_views, dst_views, name):
    send_sems, recv_sems, srcs, lands = (list(t) for t in zip(*started))
    n = len(srcs)
    after = list(after)

    def body(*refs):
        src_refs, land_refs = refs[:n], refs[n:2 * n]
        send_refs, recv_refs = refs[2 * n:3 * n], refs[3 * n:4 * n]
        copies = _exchange_copies(src_refs, land_refs, send_refs, recv_refs, src_views, dst_views)
        for send, _ in copies:
            send.wait_send()
        for _, recv in copies:
            recv.wait_recv()

    thru = [pltpu.HBM(a.shape, a.dtype) for a in list(srcs) + list(lands)]
    outs = pl.pallas_call(
        body, name=name, out_shape=thru,
        in_specs=[HBM_SPEC] * (2 * n) + [SEM_SPEC] * (2 * n) + [ANY] * len(after),
        out_specs=[HBM_SPEC] * (2 * n),
        input_output_aliases={j: j for j in range(2 * n)},
        compiler_params=pltpu.CompilerParams(has_side_effects=SIDE_EFFECT),
    )(*srcs, *lands, *send_sems, *recv_sems, *after)
    return outs[:n], outs[n:]


def _place_own(me, src, land, kind, name):
    r, c = src.shape
    tr = _pick(r, 512, 16)
    nt = r // tr
    if kind == "rows":
        out_spec = pl.BlockSpec((tr, c), lambda i, mr: (mr[0] * nt + i, 0))
    elif kind == "cols":
        out_spec = pl.BlockSpec((tr, c), lambda i, mr: (i, mr[0]))
    else:
        out_spec = pl.BlockSpec((None, tr, c), lambda i, mr: (mr[0], i, 0))

    def body(me_ref, s_ref, land_ref, o_ref):
        o_ref[...] = s_ref[...]

    return pl.pallas_call(
        body, name=name,
        grid_spec=pltpu.PrefetchScalarGridSpec(
            num_scalar_prefetch=1, grid=(nt,),
            in_specs=[pl.BlockSpec((tr, c), lambda i, mr: (i, 0)), ANY], out_specs=out_spec),
        out_shape=jax.ShapeDtypeStruct(land.shape, land.dtype),
        input_output_aliases={2: 0},
    )(me, src, land)


def _sum_slots_own(me, landed, part, kind, name, *, layer=0, n_layers=1, stacked=None):
    _, r, c = landed.shape
    tr = _pick(r, 512, 16)
    nt = r // tr
    if kind == "rows":
        part_spec = pl.BlockSpec((tr, c), lambda i, mr: (mr[0] * nt + i, 0))
    elif kind == "cols":
        part_spec = pl.BlockSpec((tr, c), lambda i, mr: (i, mr[0]))
    elif kind == "blk":
        part_spec = pl.BlockSpec((None, tr, c), lambda i, mr: (mr[0], i, 0))
    else:
        part_spec = pl.BlockSpec((tr, c), lambda i, mr: (i, 0))

    def body(me_ref, x_ref, p_ref, *rest):
        o_ref = rest[-1]
        own = p_ref[...].astype(F32)
        acc = jnp.where(me_ref[0] == 0, own, x_ref[0].astype(F32))
        for i in range(1, N_DEV):
            acc = acc + jnp.where(me_ref[0] == i, own, x_ref[i].astype(F32))
        o_ref[...] = acc

    operands = [me, landed, part] + ([] if stacked is None else [stacked])
    return pl.pallas_call(
        body, name=name,
        grid_spec=pltpu.PrefetchScalarGridSpec(
            num_scalar_prefetch=1, grid=(nt,),
            in_specs=[pl.BlockSpec((N_DEV, tr, c), lambda i, mr: (0, i, 0)), part_spec]
            + ([] if stacked is None else [ANY]),
            out_specs=pl.BlockSpec((None, tr, c), lambda i, mr: (layer, i, 0))),
        out_shape=jax.ShapeDtypeStruct((n_layers, r, c), F32),
        input_output_aliases={} if stacked is None else {3: 0},
        compiler_params=_params(2 * N_DEV * tr * c * landed.dtype.itemsize + 8 * tr * c * 4 + (4 << 20)),
    )(*operands)


def _cols_from_blocks(blk, name):
    nd, nl, k, n = blk.shape
    tk = _pick(k, 256, 16)

    def body(b_ref, o_ref, wide_ref):
        for dev in range(nd):
            wide_ref[:, dev * n:(dev + 1) * n] = b_ref[dev].astype(F32)
        o_ref[...] = wide_ref[...].astype(o_ref.dtype)

    return pl.pallas_call(
        body, name=name, grid=(nl, k // tk),
        in_specs=[pl.BlockSpec((nd, None, tk, n), lambda l, i: (0, l, i, 0))],
        out_specs=pl.BlockSpec((None, tk, nd * n), lambda l, i: (l, i, 0)),
        out_shape=jax.ShapeDtypeStruct((nl, k, nd * n), BF16),
        scratch_shapes=[pltpu.VMEM((tk, nd * n), F32)],
    )(blk)


def _blocks_from_cols(full, name):
    k, n8 = full.shape
    n = n8 // N_DEV
    tk = _pick(k, 256, 16)

    def body(f_ref, o_ref):
        for dev in range(N_DEV):
            o_ref[dev] = f_ref[:, dev * n:(dev + 1) * n].astype(o_ref.dtype)

    return pl.pallas_call(
        body, name=name, grid=(k // tk,),
        in_specs=[pl.BlockSpec((tk, n8), lambda i: (i, 0))],
        out_specs=pl.BlockSpec((N_DEV, tk, n), lambda i: (0, i, 0)),
        out_shape=jax.ShapeDtypeStruct((N_DEV, k, n), BF16),
    )(full)


def _pack(arrs, dtype, cols):
    flat = jnp.concatenate([a.astype(dtype).reshape(-1) for a in arrs])
    unit = 16 * cols
    pad = (-flat.shape[0]) % unit
    flat = jnp.pad(flat, (0, pad))
    return flat.reshape(-1, cols)


def _unpack(flat, shapes):
    out, off = [], 0
    for s in shapes:
        n = math.prod(s)
        out.append(flat[off:off + n].reshape(s))
        off += n
    return out


def _ada_fwd(c_all, w_ada, b_shard, name):
    nl, d, n = w_ada.shape
    tn = _pick(n, 512, LANES)

    def body(c_ref, w_ref, b_ref, o_ref, act_ref):
        cv = c_ref[...]
        act = cv * jax.nn.sigmoid(cv)
        act_ref[...] = act
        o_ref[...] = jnp.dot(act.astype(BF16), w_ref[...].astype(BF16), preferred_element_type=F32) + b_ref[...]

    return pl.pallas_call(
        body, name=name, grid=(nl, n // tn),
        in_specs=[pl.BlockSpec(c_all.shape, lambda l, j: (0, 0)),
                  pl.BlockSpec((None, d, tn), lambda l, j: (l, 0, j)),
                  pl.BlockSpec((None, 1, tn), lambda l, j: (l, 0, j))],
        out_specs=[pl.BlockSpec((None, c_all.shape[0], tn), lambda l, j: (l, 0, j)),
                   pl.BlockSpec(c_all.shape, lambda l, j: (0, 0))],
        out_shape=[jax.ShapeDtypeStruct((nl, c_all.shape[0], n), F32), jax.ShapeDtypeStruct(c_all.shape, F32)],
        compiler_params=_params(2 * d * tn * 4 + d * tn * 2 + (8 << 20)),
    )(c_all, w_ada, b_shard)


def _ada_wgrad(act_t, dada, name):
    d, kp = act_t.shape
    nl, _, n = dada.shape
    tm = _pick(d, 512, 8)

    def body(a_ref, g_ref, o_ref):
        o_ref[...] = jnp.dot(a_ref[...].astype(BF16), g_ref[...].astype(BF16), preferred_element_type=F32)

    return pl.pallas_call(
        body, name=name, grid=(nl, d // tm),
        in_specs=[pl.BlockSpec((tm, kp), lambda l, i: (i, 0)), pl.BlockSpec((None, kp, n), lambda l, i: (l, 0, 0))],
        out_specs=pl.BlockSpec((None, tm, n), lambda l, i: (l, i, 0)),
        out_shape=jax.ShapeDtypeStruct((nl, d, n), F32),
        compiler_params=_params(4 * tm * n * 4 + 2 * kp * n * 4 + (8 << 20)),
    )(act_t, dada)


def _row_spec(tm, d):
    return pl.BlockSpec((tm, d), lambda i: (i, 0))


def _row_params(tm, d):
    return _params(11 * tm * d * 4 + (4 << 20))


def _vec_spec(d):
    return pl.BlockSpec((1, d), lambda i: (0, 0))


def _modnorm_fwd(x, g, sc, sh, name):
    s, d = x.shape
    tm = _pick(s, 512, 16)

    def body(x_ref, g_ref, sc_ref, sh_ref, o_ref):
        xv = x_ref[...]
        o_ref[...] = ((xv * _rstd(xv)) * g_ref[...] * (1.0 + sc_ref[...]) + sh_ref[...]).astype(o_ref.dtype)

    return pl.pallas_call(
        body, name=name, grid=(s // tm,),
        in_specs=[_row_spec(tm, d), _vec_spec(d), _vec_spec(d), _vec_spec(d)], out_specs=_row_spec(tm, d),
        out_shape=jax.ShapeDtypeStruct((s, d), BF16), compiler_params=_row_params(tm, d),
    )(x, g, sc, sh)


def _modnorm_bwd(dh, x, g, sc, dres, name):
    s, d = x.shape
    tm = _pick(s, 256, 8)

    def body(dh_ref, x_ref, g_ref, sc_ref, dres_ref, dx_ref, dg_ref, dsc_ref, dsh_ref):
        @pl.when(pl.program_id(0) == 0)
        def _():
            dg_ref[...] = jnp.zeros_like(dg_ref)
            dsc_ref[...] = jnp.zeros_like(dsc_ref)
            dsh_ref[...] = jnp.zeros_like(dsh_ref)

        dh_, xv, gv = dh_ref[...], x_ref[...], g_ref[...]
        r = _rstd(xv)
        xhat = xv * r
        dn = dh_ * (1.0 + sc_ref[...])
        dsh_ref[...] += jnp.sum(dh_, axis=0, keepdims=True)
        dsc_ref[...] += jnp.sum(dh_ * (xhat * gv), axis=0, keepdims=True)
        dg_ref[...] += jnp.sum(dn * xhat, axis=0, keepdims=True)
        dx_ref[...] = _norm_bwd(dn * gv, xhat, r) + dres_ref[...]

    vec = jax.ShapeDtypeStruct((1, d), F32)
    return pl.pallas_call(
        body, name=name, grid=(s // tm,),
        in_specs=[_row_spec(tm, d), _row_spec(tm, d), _vec_spec(d), _vec_spec(d), _row_spec(tm, d)],
        out_specs=[_row_spec(tm, d), _vec_spec(d), _vec_spec(d), _vec_spec(d)],
        out_shape=[jax.ShapeDtypeStruct((s, d), F32), vec, vec, vec],
    )(dh, x, g, sc, dres)


def _modnorm_resnorm_bwd(dh, x, g, sc, dres, y, g2, gt2, name):
    s, d = x.shape
    tm = _pick(s, 256, 16)

    def body(dh_ref, x_ref, g_ref, sc_ref, dres_ref, y_ref, g2_ref, gt2_ref,
             dx_ref, dg_ref, dsc_ref, dsh_ref, dy_ref, dg2_ref, dgt2_ref):
        @pl.when(pl.program_id(0) == 0)
        def _():
            for ref in (dg_ref, dsc_ref, dsh_ref, dg2_ref, dgt2_ref):
                ref[...] = jnp.zeros_like(ref)

        dh_, xv, gv = dh_ref[...], x_ref[...], g_ref[...]
        r = _rstd(xv)
        xhat = xv * r
        dn = dh_ * (1.0 + sc_ref[...])
        dsh_ref[...] += jnp.sum(dh_, axis=0, keepdims=True)
        dsc_ref[...] += jnp.sum(dh_ * (xhat * gv), axis=0, keepdims=True)
        dg_ref[...] += jnp.sum(dn * xhat, axis=0, keepdims=True)
        dx = _norm_bwd(dn * gv, xhat, r) + dres_ref[...]
        dx_ref[...] = dx
        yv, g2v = y_ref[...], g2_ref[...]
        r2 = _rstd(yv)
        yhat = yv * r2
        dn2 = dx * (1.0 + gt2_ref[...])
        dgt2_ref[...] += jnp.sum(dx * (yhat * g2v), axis=0, keepdims=True)
        dg2_ref[...] += jnp.sum(dn2 * yhat, axis=0, keepdims=True)
        dy_ref[...] = _norm_bwd(dn2 * g2v, yhat, r2).astype(dy_ref.dtype)

    vec = jax.ShapeDtypeStruct((1, d), F32)
    row, vspec = _row_spec(tm, d), _vec_spec(d)
    return pl.pallas_call(
        body, name=name, grid=(s // tm,),
        in_specs=[row, row, vspec, vspec, row, row, vspec, vspec],
        out_specs=[row, vspec, vspec, vspec, row, vspec, vspec],
        out_shape=[jax.ShapeDtypeStruct((s, d), F32), vec, vec, vec, jax.ShapeDtypeStruct((s, d), BF16), vec, vec],
        compiler_params=_params(22 * tm * d * 4 + (4 << 20)),
    )(dh, x, g, sc, dres, y, g2, gt2)


def _resnorm_modnorm_fwd(x, y, g, gt, g2, sc2, sh2, name):
    s, d = x.shape
    tm = _pick(s, 512, 16)

    def body(x_ref, y_ref, g_ref, gt_ref, g2_ref, sc_ref, sh_ref, o_ref, h_ref):
        yv = y_ref[...]
        xo = x_ref[...] + (1.0 + gt_ref[...]) * ((yv * _rstd(yv)) * g_ref[...])
        o_ref[...] = xo
        h_ref[...] = ((xo * _rstd(xo)) * g2_ref[...] * (1.0 + sc_ref[...]) + sh_ref[...]).astype(h_ref.dtype)

    return pl.pallas_call(
        body, name=name, grid=(s // tm,),
        in_specs=[_row_spec(tm, d), _row_spec(tm, d)] + [_vec_spec(d)] * 5,
        out_specs=[_row_spec(tm, d), _row_spec(tm, d)],
        out_shape=[jax.ShapeDtypeStruct((s, d), F32), jax.ShapeDtypeStruct((s, d), BF16)],
        compiler_params=_row_params(tm, d),
    )(x, y, g, gt, g2, sc2, sh2)


def _resnorm_loss(x, y, g, gt, tgt, name):
    s, d = x.shape
    tm = _pick(s, 512, 8)

    def body(x_ref, y_ref, g_ref, gt_ref, t_ref, dy_ref, l_ref):
        @pl.when(pl.program_id(0) == 0)
        def _():
            l_ref[...] = jnp.zeros_like(l_ref)

        yv = y_ref[...]
        e = x_ref[...] + (1.0 + gt_ref[...]) * ((yv * _rstd(yv)) * g_ref[...]) - t_ref[...]
        dy_ref[...] = e * (1.0 / d)
        l_ref[...] += jnp.sum(e * e) * (0.5 / d)

    return pl.pallas_call(
        body, name=name, grid=(s // tm,),
        in_specs=[_row_spec(tm, d), _row_spec(tm, d), _vec_spec(d), _vec_spec(d), _row_spec(tm, d)],
        out_specs=[_row_spec(tm, d), pl.BlockSpec((8, LANES), lambda i: (0, 0))],
        out_shape=[jax.ShapeDtypeStruct((s, d), F32), jax.ShapeDtypeStruct((8, LANES), F32)],
        compiler_params=_row_params(tm, d),
    )(x, y, g, gt, tgt)


def _resnorm_bwd(dxo, y, g, gt, name):
    s, d = y.shape
    tm = _pick(s, 512, 16)

    def body(dxo_ref, y_ref, g_ref, gt_ref, dy_ref, dg_ref, dgt_ref):
        @pl.when(pl.program_id(0) == 0)
        def _():
            dg_ref[...] = jnp.zeros_like(dg_ref)
            dgt_ref[...] = jnp.zeros_like(dgt_ref)

        dxo_, yv, gv = dxo_ref[...], y_ref[...], g_ref[...]
        r = _rstd(yv)
        yhat = yv * r
        dn = dxo_ * (1.0 + gt_ref[...])
        dgt_ref[...] += jnp.sum(dxo_ * (yhat * gv), axis=0, keepdims=True)
        dg_ref[...] += jnp.sum(dn * yhat, axis=0, keepdims=True)
        dy_ref[...] = _norm_bwd(dn * gv, yhat, r).astype(dy_ref.dtype)

    vec = jax.ShapeDtypeStruct((1, d), F32)
    return pl.pallas_call(
        body, name=name, grid=(s // tm,),
        in_specs=[_row_spec(tm, d), _row_spec(tm, d), _vec_spec(d), _vec_spec(d)],
        out_specs=[_row_spec(tm, d), _vec_spec(d), _vec_spec(d)],
        out_shape=[jax.ShapeDtypeStruct((s, d), BF16), vec, vec], compiler_params=_row_params(tm, d),
    )(dxo, y, g, gt)


def _attn_specs(n_q, n_kv):
    aw, kvd = n_q * HEAD_DIM, n_kv * HEAD_DIM
    assert aw % kvd == 0
    kcol = aw // kvd
    q = pl.BlockSpec((WINDOW, aw), lambda n: (n, 0))
    kc = pl.BlockSpec((WINDOW, kvd), lambda n: (n, kcol))
    kp = pl.BlockSpec((WINDOW, kvd), lambda n: (jnp.maximum(n - 1, 0), kcol))
    vc = pl.BlockSpec((WINDOW, kvd), lambda n: (n, kcol + 1))
    vp = pl.BlockSpec((WINDOW, kvd), lambda n: (jnp.maximum(n - 1, 0), kcol + 1))
    return [q, kc, kp, vc, vp]


def _band_mask(n, n_heads):
    qi = lax.broadcasted_iota(jnp.int32, (n_heads * WINDOW, 2 * WINDOW), 0) & (WINDOW - 1)
    kj = lax.broadcasted_iota(jnp.int32, (n_heads * WINDOW, 2 * WINDOW), 1)
    return (kj > qi) & (kj <= qi + WINDOW) & ((kj >= WINDOW) | (n > 0))


def _stack_heads(ref, heads):
    return jnp.concatenate([ref[:, h * HEAD_DIM:(h + 1) * HEAD_DIM] for h in heads], axis=0)


def _stack_sinks(ref, heads):
    return jnp.concatenate([jnp.broadcast_to(ref[:, h:h + 1], (WINDOW, 1)) for h in heads], axis=0)


_NT = (((1,), (1,)), ((), ()))
_TN = (((0,), (0,)), ((), ()))


def _attn_fwd(proj, sinks, *, n_q, n_kv, name):
    s = proj.shape[0]
    aw, grp = n_q * HEAD_DIM, n_q // n_kv

    def body(q_ref, kc_ref, kp_ref, vc_ref, vp_ref, sink_ref, o_ref, lse_ref):
        valid = _band_mask(pl.program_id(0), grp)
        kb = jnp.concatenate([kp_ref[...], kc_ref[...]], axis=0).astype(BF16)
        vb = jnp.concatenate([vp_ref[...], vc_ref[...]], axis=0).astype(BF16)
        lse_ref[...] = jnp.zeros_like(lse_ref)
        for g in range(n_kv):
            heads = range(g * grp, (g + 1) * grp)
            gs = slice(g * HEAD_DIM, (g + 1) * HEAD_DIM)
            qg = _stack_heads(q_ref, heads).astype(BF16)
            sink = _stack_sinks(sink_ref, heads)
            sc = lax.dot_general(qg, kb[:, gs], _NT, preferred_element_type=F32)
            sc = jnp.where(valid, sc * (HEAD_DIM ** -0.5), NEG)
            m = jnp.maximum(jnp.max(sc, axis=-1, keepdims=True), sink)
            e = jnp.exp(sc - m)
            den = jnp.sum(e, axis=-1, keepdims=True) + jnp.exp(sink - m)
            p = e * (1.0 / den)
            og = jnp.dot(p.astype(BF16), vb[:, gs], preferred_element_type=F32)
            lse = m + jnp.log(den)
            for i, h in enumerate(heads):
                rows = slice(i * WINDOW, (i + 1) * WINDOW)
                o_ref[:, h * HEAD_DIM:(h + 1) * HEAD_DIM] = og[rows]
                lse_ref[:, h:h + 1] = lse[rows]

    return pl.pallas_call(
        body, name=name, grid=(s // WINDOW,),
        in_specs=_attn_specs(n_q, n_kv) + [pl.BlockSpec((1, LANES), lambda n: (0, 0))],
        out_specs=[pl.BlockSpec((WINDOW, aw), lambda n: (n, 0)), pl.BlockSpec((WINDOW, LANES), lambda n: (n, 0))],
        out_shape=[jax.ShapeDtypeStruct((s, aw), F32), jax.ShapeDtypeStruct((s, LANES), F32)],
    )(proj, proj, proj, proj, proj, sinks)


def _attn_bwd(proj, sinks, out, lse, dout, *, n_q, n_kv, name):
    s = proj.shape[0]
    aw, kvd, grp = n_q * HEAD_DIM, n_kv * HEAD_DIM, n_q // n_kv
    scale = HEAD_DIM ** -0.5

    def body(q_ref, kc_ref, kp_ref, vc_ref, vp_ref, sink_ref, o_ref, lse_ref, do_ref,
             dq_ref, dk_ref, dv_ref, dsink_ref):
        n = pl.program_id(0)

        @pl.when(n == 0)
        def _():
            dk_ref[...] = jnp.zeros_like(dk_ref)
            dv_ref[...] = jnp.zeros_like(dv_ref)
            dsink_ref[...] = jnp.zeros_like(dsink_ref)

        valid = _band_mask(n, grp)
        kb = jnp.concatenate([kp_ref[...], kc_ref[...]], axis=0).astype(BF16)
        vb = jnp.concatenate([vp_ref[...], vc_ref[...]], axis=0).astype(BF16)
        lane = lax.broadcasted_iota(jnp.int32, (8, LANES), 1)
        dsink = jnp.zeros((8, LANES), F32)
        cur = pl.ds(pl.multiple_of(n * WINDOW, WINDOW), WINDOW)
        prev = pl.ds(pl.multiple_of(jnp.maximum(n - 1, 0) * WINDOW, WINDOW), WINDOW)
        for g in range(n_kv):
            heads = range(g * grp, (g + 1) * grp)
            gs = slice(g * HEAD_DIM, (g + 1) * HEAD_DIM)
            qg = _stack_heads(q_ref, heads).astype(BF16)
            do = _stack_heads(do_ref, heads)
            dob = do.astype(BF16)
            lse = jnp.concatenate([lse_ref[:, h:h + 1] for h in heads], axis=0)
            sc = lax.dot_general(qg, kb[:, gs], _NT, preferred_element_type=F32)
            sc = jnp.where(valid, sc * scale, NEG)
            p = jnp.exp(sc - lse)
            delta = jnp.sum(do * _stack_heads(o_ref, heads), axis=-1, keepdims=True)
            dp = lax.dot_general(dob, vb[:, gs], _NT, preferred_element_type=F32)
            ds = (p * (dp - delta) * scale).astype(BF16)
            dqg = jnp.dot(ds, kb[:, gs], preferred_element_type=F32)
            dkb = lax.dot_general(ds, qg, _TN, preferred_element_type=F32)
            dvb = lax.dot_general(p.astype(BF16), dob, _TN, preferred_element_type=F32)
            sink_term = jnp.exp(_stack_sinks(sink_ref, heads) - lse) * delta
            for i, h in enumerate(heads):
                rows = slice(i * WINDOW, (i + 1) * WINDOW)
                dq_ref[:, h * HEAD_DIM:(h + 1) * HEAD_DIM] = dqg[rows]
                dsink = dsink + jnp.where(lane == h, -jnp.sum(sink_term[rows]), 0.0)
            dk_ref[cur, gs] += dkb[WINDOW:]
            dv_ref[cur, gs] += dvb[WINDOW:]

            @pl.when(n > 0)
            def _():
                dk_ref[prev, gs] += dkb[:WINDOW]
                dv_ref[prev, gs] += dvb[:WINDOW]

        dsink_ref[...] += dsink

    blk = pl.BlockSpec((WINDOW, aw), lambda n: (n, 0))
    kv_full = pl.BlockSpec((s, kvd), lambda n: (0, 0))
    return pl.pallas_call(
        body, name=name, grid=(s // WINDOW,),
        in_specs=_attn_specs(n_q, n_kv) + [pl.BlockSpec((1, LANES), lambda n: (0, 0)), blk,
                                           pl.BlockSpec((WINDOW, LANES), lambda n: (n, 0)), blk],
        out_specs=[blk, kv_full, kv_full, pl.BlockSpec((8, LANES), lambda n: (0, 0))],
        out_shape=[jax.ShapeDtypeStruct((s, aw), F32), jax.ShapeDtypeStruct((s, kvd), F32),
                   jax.ShapeDtypeStruct((s, kvd), F32), jax.ShapeDtypeStruct((8, LANES), F32)],
    )(proj, proj, proj, proj, proj, sinks, out, lse, dout)


def _disc(lr, li, ls):
    dt = jnp.exp(ls)
    mag = jnp.exp(lr * dt)
    ang = li * dt
    ab_re, ab_im = mag * jnp.cos(ang), mag * jnp.sin(ang)
    den = lr * lr + li * li
    f_re = ((ab_re - 1.0) * lr + ab_im * li) / den
    f_im = (ab_im * lr - (ab_re - 1.0) * li) / den
    return ab_re, ab_im, f_re, f_im


POW_ROWS = 8
SUB = 8
TAB_ROWS = POW_ROWS + 2 * SUB


def _ssm_params_fwd(lr, li, ls, b_re, b_im, name):
    gp = lr.shape[1]
    h = b_re.shape[0]

    def body(lr_ref, li_ref, ls_ref, br_ref, bi_ref, bbr_ref, bbi_ref, tr_ref, ti_ref):
        ab_re, ab_im, f_re, f_im = _disc(lr_ref[...], li_ref[...], ls_ref[...])
        br, bi = br_ref[...], bi_ref[...]
        bbr_ref[...] = f_re * br - f_im * bi
        bbi_ref[...] = f_re * bi + f_im * br
        pr, pi = ab_re, ab_im
        for i in range(POW_ROWS):
            tr_ref[i:i + 1, :] = pr
            ti_ref[i:i + 1, :] = pi
            pr, pi = pr * pr - pi * pi, 2.0 * pr * pi
        pr, pi = ab_re, ab_im
        for r in range(SUB):
            for row in (POW_ROWS + r, POW_ROWS + 2 * SUB - 1 - r):
                tr_ref[row:row + 1, :] = pr
                ti_ref[row:row + 1, :] = pi
            pr, pi = pr * ab_re - pi * ab_im, pr * ab_im + pi * ab_re

    mat, tab = jax.ShapeDtypeStruct((h, gp), F32), jax.ShapeDtypeStruct((TAB_ROWS, gp), F32)
    return pl.pallas_call(body, name=name, out_shape=[mat, mat, tab, tab])(lr, li, ls, b_re, b_im)


def _ssm_params_bwd(lr, li, ls, b_re, b_im, dab_re, dab_im, dbb_re, dbb_im, seg, name):
    gp = lr.shape[1]
    h = b_re.shape[0]

    def body(lr_ref, li_ref, ls_ref, br_ref, bi_ref, dar_ref, dai_ref, dbbr_ref, dbbi_ref, seg_ref,
             dlr_ref, dli_ref, dls_ref, dbr_ref, dbi_ref):
        lr_, li_, ls_ = lr_ref[...], li_ref[...], ls_ref[...]
        (ab_re, ab_im, f_re, f_im), vjp = jax.vjp(_disc, lr_, li_, ls_)
        br, bi, dbbr, dbbi = br_ref[...], bi_ref[...], dbbr_ref[...], dbbi_ref[...]
        dbr_ref[...] = dbbr * f_re + dbbi * f_im
        dbi_ref[...] = dbbi * f_re - dbbr * f_im
        df_re = jnp.sum(dbbr * br + dbbi * bi, axis=0, keepdims=True)
        df_im = jnp.sum(dbbi * br - dbbr * bi, axis=0, keepdims=True)
        dlr, dli, dls = vjp((dar_ref[...], dai_ref[...], df_re, df_im))
        dlr_ref[...] = dlr
        dli_ref[...] = dli
        dls8 = jnp.broadcast_to(dls, (8, gp))
        dls_ref[...] = jnp.dot(dls8, seg_ref[...], preferred_element_type=F32, precision=lax.Precision.HIGHEST)

    vec, mat = jax.ShapeDtypeStruct((1, gp), F32), jax.ShapeDtypeStruct((h, gp), F32)
    return pl.pallas_call(body, name=name,
                          out_shape=[vec, vec, jax.ShapeDtypeStruct((8, seg.shape[1]), F32), mat, mat],
                          compiler_params=_params(24 << 20))(
        lr, li, ls, b_re, b_im, dab_re, dab_im, dbb_re, dbb_im, seg)


def _scan_bufs(t_len):
    hs = BLOCK_STATES
    return [pltpu.VMEM((hs // LANES, t_len, LANES), F32), pltpu.VMEM((hs // LANES, t_len, LANES), F32),
            pltpu.VMEM((t_len // SUB, hs), F32), pltpu.VMEM((t_len // SUB, hs), F32)]


def _scan(xr, xi, apow_ref, bufs, t_len, reverse):
    hs = BLOCK_STATES
    n_tiles = t_len // SUB
    sr_ref, si_ref, er_ref, ei_ref = bufs

    def doubling(xr, xi, n_rows, first_pow, within):
        row = lax.broadcasted_iota(jnp.int32, xr.shape, 0) & (within - 1)
        d = 1
        while d < within:
            i = first_pow + d.bit_length() - 1
            pr, pi = apow_ref[i:i + 1, :hs], apow_ref[i:i + 1, hs:]
            if reverse:
                pi, shift, keep = -pi, n_rows - d, row < within - d
            else:
                shift, keep = d, row >= d
            sr = jnp.where(keep, pltpu.roll(xr, shift, 0), 0.0)
            si = jnp.where(keep, pltpu.roll(xi, shift, 0), 0.0)
            xr, xi = xr + pr * sr - pi * si, xi + pr * si + pi * sr
            d *= 2
        return xr, xi

    shape3 = (n_tiles, SUB, hs)
    row = lax.broadcasted_iota(jnp.int32, shape3, 1)
    xr, xi = xr.reshape(shape3), xi.reshape(shape3)
    for i, d in enumerate((1, 2, 4)):
        pr, pi = apow_ref[i:i + 1, :hs], apow_ref[i:i + 1, hs:]
        if reverse:
            pi, shift, keep = -pi, SUB - d, row < SUB - d
        else:
            shift, keep = d, row >= d
        sr = jnp.where(keep, pltpu.roll(xr, shift, 1), 0.0)
        si = jnp.where(keep, pltpu.roll(xi, shift, 1), 0.0)
        xr, xi = xr + pr * sr - pi * si, xi + pr * si + pi * sr
    xr, xi = xr.reshape(t_len, hs), xi.reshape(t_len, hs)
    chunks = [slice(c * LANES, (c + 1) * LANES) for c in range(hs // LANES)]
    for c, lanes in enumerate(chunks):
        sr_ref[c] = xr[:, lanes]
        si_ref[c] = xi[:, lanes]
    edge = pl.ds(0 if reverse else SUB - 1, n_tiles, stride=SUB)
    tr, ti = doubling(jnp.concatenate([sr_ref[c, edge, :] for c in range(len(chunks))], axis=1),
                      jnp.concatenate([si_ref[c, edge, :] for c in range(len(chunks))], axis=1), n_tiles, 3, n_tiles)
    trow = lax.broadcasted_iota(jnp.int32, tr.shape, 0)
    if reverse:
        shift, keep = n_tiles - 1, trow < n_tiles - 1
    else:
        shift, keep = 1, trow >= 1
    er_ref[...] = jnp.where(keep, pltpu.roll(tr, shift, 0), 0.0)
    ei_ref[...] = jnp.where(keep, pltpu.roll(ti, shift, 0), 0.0)
    lin = POW_ROWS + SUB if reverse else POW_ROWS
    mr, mi = apow_ref[lin:lin + SUB, :hs], apow_ref[lin:lin + SUB, hs:]
    if reverse:
        mi = -mi
    for t in range(n_tiles):
        rows = slice(t * SUB, (t + 1) * SUB)
        er, ei = er_ref[t:t + 1, :], ei_ref[t:t + 1, :]
        add_r, add_i = mr * er - mi * ei, mr * ei + mi * er
        for c, lanes in enumerate(chunks):
            sr_ref[c, rows, :] += add_r[:, lanes]
            si_ref[c, rows, :] += add_i[:, lanes]
    return (jnp.concatenate([sr_ref[c] for c in range(len(chunks))], axis=1),
            jnp.concatenate([si_ref[c] for c in range(len(chunks))], axis=1))


def _ssm_chunk(s):
    t_len = _pick(s, 256, 8)
    assert t_len & (t_len - 1) == 0 and t_len <= 1 << POW_ROWS, t_len
    return t_len


def _fold_carry(br, bi, carry_ref, apow_ref, at_row, conj):
    hs = BLOCK_STATES
    cr, ci = carry_ref[0:1, :hs], carry_ref[0:1, hs:]
    ar, ai = apow_ref[0:1, :hs], apow_ref[0:1, hs:]
    if conj:
        ai = -ai
    here = lax.broadcasted_iota(jnp.int32, br.shape, 0) == at_row
    return jnp.where(here, br + (ar * cr - ai * ci), br), jnp.where(here, bi + (ar * ci + ai * cr), bi)


def _ssm_fwd(proj, ucol, bbd, ccat, dskip, apow, t_len, *, name):
    s = proj.shape[0]
    nb = bbd.shape[0]
    nc = s // t_len
    hs = BLOCK_STATES

    def body(u_ref, bbd_ref, ccat_ref, d_ref, apow_ref, y_ref, z_ref, xs_ref, carry_ref, *bufs):
        @pl.when(pl.program_id(1) == 0)
        def _():
            carry_ref[...] = jnp.zeros_like(carry_ref)

        xs_ref[...] = carry_ref[...]
        u = u_ref[...]
        bu = jnp.dot(u.astype(BF16), bbd_ref[...], preferred_element_type=F32)
        br, bi = _fold_carry(bu[:, :hs], bu[:, hs:], carry_ref, apow_ref, 0, False)
        xr, xi = _scan(br, bi, apow_ref, bufs, t_len, False)
        xcat = jnp.concatenate([xr, xi], axis=1)
        carry_ref[...] = jnp.broadcast_to(xcat[t_len - 1:t_len, :], carry_ref.shape)
        y = jnp.dot(xcat.astype(BF16), ccat_ref[...], preferred_element_type=F32) + d_ref[...] * u
        y_ref[...] = y
        z_ref[...] = _gelu(y).astype(z_ref.dtype)

    return pl.pallas_call(
        body, name=name, grid=(nb, nc),
        in_specs=[pl.BlockSpec((t_len, LANES), lambda j, n: (n, ucol + j)),
                  pl.BlockSpec((None, LANES, 2 * hs), lambda j, n: (j, 0, 0)),
                  pl.BlockSpec((None, 2 * hs, LANES), lambda j, n: (j, 0, 0)),
                  pl.BlockSpec((1, LANES), lambda j, n: (0, j)),
                  pl.BlockSpec((None, TAB_ROWS, 2 * hs), lambda j, n: (j, 0, 0))],
        out_specs=[pl.BlockSpec((t_len, LANES), lambda j, n: (n, j)),
                   pl.BlockSpec((t_len, LANES), lambda j, n: (n, j)),
                   pl.BlockSpec((None, None, 8, 2 * hs), lambda j, n: (j, n, 0, 0))],
        out_shape=[jax.ShapeDtypeStruct((s, nb * LANES), F32), jax.ShapeDtypeStruct((s, nb * LANES), BF16),
                   jax.ShapeDtypeStruct((nb, nc, 8, 2 * hs), F32)],
        scratch_shapes=[pltpu.VMEM((8, 2 * hs), F32)] + _scan_bufs(t_len),
        compiler_params=_params(40 << 20),
    )(proj, bbd, ccat, dskip, apow)


def _ssm_bwd(proj, ucol, y, dzd, dz2, xs, bbd, ccat, dskip, apow, t_len, *, name):
    s = proj.shape[0]
    nb = bbd.shape[0]
    nc = s // t_len
    hs = BLOCK_STATES

    def body(u_ref, y_ref, dzd_ref, dz2_ref, xs_ref, bbd_ref, ccat_ref, d_ref, apow_ref,
             du_ref, dbbd_ref, dccat_ref, dd_ref, da_ref, gcarry_ref, *bufs):
        @pl.when(pl.program_id(1) == 0)
        def _():
            gcarry_ref[...] = jnp.zeros_like(gcarry_ref)
            dbbd_ref[...] = jnp.zeros_like(dbbd_ref)
            dccat_ref[...] = jnp.zeros_like(dccat_ref)
            dd_ref[...] = jnp.zeros_like(dd_ref)
            da_ref[...] = jnp.zeros_like(da_ref)

        u = u_ref[...]
        ub = u.astype(BF16)
        dy = (dzd_ref[...] + dz2_ref[...]) * _gelu_grad(y_ref[...])
        dyb = dy.astype(BF16)
        bu = jnp.dot(ub, bbd_ref[...], preferred_element_type=F32)
        br, bi = _fold_carry(bu[:, :hs], bu[:, hs:], xs_ref, apow_ref, 0, False)
        xr, xi = _scan(br, bi, apow_ref, bufs[:4], t_len, False)
        sr, si = xs_ref[0:1, :hs], xs_ref[0:1, hs:]
        dxd = lax.dot_general(dyb, ccat_ref[...], _NT, preferred_element_type=F32)
        dr, di = _fold_carry(dxd[:, :hs], dxd[:, hs:], gcarry_ref, apow_ref, t_len - 1, True)
        gr, gi = _scan(dr, di, apow_ref, bufs[4:], t_len, True)
        gcat = jnp.concatenate([gr, gi], axis=1)
        gcarry_ref[...] = jnp.broadcast_to(gcat[0:1, :], gcarry_ref.shape)
        gb = gcat.astype(BF16)
        du_ref[...] = lax.dot_general(gb, bbd_ref[...], _NT, preferred_element_type=F32) + d_ref[...] * dy
        dbbd_ref[...] += lax.dot_general(ub, gb, _TN, preferred_element_type=F32)
        xb = jnp.concatenate([xr, xi], axis=1).astype(BF16)
        dccat_ref[...] += lax.dot_general(xb, dyb, _TN, preferred_element_type=F32)
        dd_ref[...] += jnp.sum(dy * u, axis=0, keepdims=True)
        first = lax.broadcasted_iota(jnp.int32, xr.shape, 0) == 0
        xpr = jnp.where(first, sr, pltpu.roll(xr, 1, 0))
        xpi = jnp.where(first, si, pltpu.roll(xi, 1, 0))
        dar = jnp.sum(gr * xpr + gi * xpi, axis=0, keepdims=True)
        dai = jnp.sum(gi * xpr - gr * xpi, axis=0, keepdims=True)
        da_ref[...] += jnp.concatenate([dar, dai], axis=1)

    def rows(j, n):
        return nc - 1 - n

    chunk = pl.BlockSpec((t_len, LANES), lambda j, n: (rows(j, n), j))
    return pl.pallas_call(
        body, name=name, grid=(nb, nc),
        in_specs=[pl.BlockSpec((t_len, LANES), lambda j, n: (rows(j, n), ucol + j)), chunk, chunk, chunk,
                  pl.BlockSpec((None, None, 8, 2 * hs), lambda j, n: (j, rows(j, n), 0, 0)),
                  pl.BlockSpec((None, LANES, 2 * hs), lambda j, n: (j, 0, 0)),
                  pl.BlockSpec((None, 2 * hs, LANES), lambda j, n: (j, 0, 0)),
                  pl.BlockSpec((1, LANES), lambda j, n: (0, j)),
                  pl.BlockSpec((None, TAB_ROWS, 2 * hs), lambda j, n: (j, 0, 0))],
        out_specs=[chunk,
                   pl.BlockSpec((None, LANES, 2 * hs), lambda j, n: (j, 0, 0)),
                   pl.BlockSpec((None, 2 * hs, LANES), lambda j, n: (j, 0, 0)),
                   pl.BlockSpec((1, LANES), lambda j, n: (0, j)),
                   pl.BlockSpec((None, 1, 2 * hs), lambda j, n: (j, 0, 0))],
        out_shape=[jax.ShapeDtypeStruct((s, nb * LANES), F32),
                   jax.ShapeDtypeStruct((nb, LANES, 2 * hs), F32),
                   jax.ShapeDtypeStruct((nb, 2 * hs, LANES), F32),
                   jax.ShapeDtypeStruct((1, nb * LANES), F32),
                   jax.ShapeDtypeStruct((nb, 1, 2 * hs), F32)],
        scratch_shapes=[pltpu.VMEM((8, 2 * hs), F32)] + _scan_bufs(t_len) + _scan_bufs(t_len),
        compiler_params=_params(48 << 20),
    )(proj, y, dzd, dz2, xs, bbd, ccat, dskip, apow)


def _to_blocks(a):
    g, p, k = a.shape
    nb = g // GROUPS_PER_BLOCK
    eye = jnp.eye(GROUPS_PER_BLOCK, dtype=a.dtype)
    a4 = a.reshape(nb, GROUPS_PER_BLOCK, p, k)
    out = jnp.einsum("ab,jbpk->jakbp", eye, a4)
    return out.reshape(nb, GROUPS_PER_BLOCK * k, GROUPS_PER_BLOCK * p)


def _from_blocks(d, p, k):
    nb = d.shape[0]
    d5 = d.reshape(nb, GROUPS_PER_BLOCK, k, GROUPS_PER_BLOCK, p)
    eye = jnp.eye(GROUPS_PER_BLOCK, dtype=bool)[None, :, None, :, None]
    diag = jnp.sum(jnp.where(eye, d5, 0.0), axis=1)
    return jnp.transpose(diag, (0, 2, 3, 1)).reshape(nb * GROUPS_PER_BLOCK, p, k)


def _merge_fwd(attn, y, gl, g_a, g_s, name):
    s, wa = attn.shape
    ws = y.shape[1]
    tm = _pick(s, 256, 16)

    def body(a_ref, y_ref, gl_ref, ga_ref, gs_ref, o_ref):
        av = a_ref[...]
        o_ref[:, :wa] = ((av * _rstd(av)) * ga_ref[...]).astype(o_ref.dtype)
        sv = _gelu(y_ref[...]) * jax.nn.sigmoid(gl_ref[...])
        o_ref[:, wa:] = ((sv * _rstd(sv)) * gs_ref[...]).astype(o_ref.dtype)

    return pl.pallas_call(
        body, name=name, grid=(s // tm,),
        in_specs=[_row_spec(tm, wa), _row_spec(tm, ws), _row_spec(tm, ws), _vec_spec(wa), _vec_spec(ws)],
        out_specs=_row_spec(tm, wa + ws), out_shape=jax.ShapeDtypeStruct((s, wa + ws), BF16),
    )(attn, y, gl, g_a, g_s)


def _merge_bwd(dmerged, attn, y, gl, g_a, g_s, name):
    s, wa = attn.shape
    ws = y.shape[1]
    tm = _pick(s, 256, 16)

    def body(dm_ref, a_ref, y_ref, gl_ref, ga_ref, gs_ref, da_ref, dgl_ref, dzd_ref, dga_ref, dgs_ref):
        @pl.when(pl.program_id(0) == 0)
        def _():
            dga_ref[...] = jnp.zeros_like(dga_ref)
            dgs_ref[...] = jnp.zeros_like(dgs_ref)

        dan, dsn = dm_ref[:, :wa], dm_ref[:, wa:]
        av = a_ref[...]
        ra = _rstd(av)
        ahat = av * ra
        dga_ref[...] += jnp.sum(dan * ahat, axis=0, keepdims=True)
        da_ref[...] = _norm_bwd(dan * ga_ref[...], ahat, ra)
        z = _gelu(y_ref[...])
        sig = jax.nn.sigmoid(gl_ref[...])
        sv = z * sig
        rs = _rstd(sv)
        shat = sv * rs
        dgs_ref[...] += jnp.sum(dsn * shat, axis=0, keepdims=True)
        dssm = _norm_bwd(dsn * gs_ref[...], shat, rs)
        dzd_ref[...] = dssm * sig
        dgl_ref[...] = (dssm * z * sig * (1.0 - sig)).astype(dgl_ref.dtype)

    return pl.pallas_call(
        body, name=name, grid=(s // tm,),
        in_specs=[_row_spec(tm, wa + ws), _row_spec(tm, wa), _row_spec(tm, ws), _row_spec(tm, ws),
                  _vec_spec(wa), _vec_spec(ws)],
        out_specs=[_row_spec(tm, wa), _row_spec(tm, ws), _row_spec(tm, ws), _vec_spec(wa), _vec_spec(ws)],
        out_shape=[jax.ShapeDtypeStruct((s, wa), F32), jax.ShapeDtypeStruct((s, ws), BF16),
                   jax.ShapeDtypeStruct((s, ws), F32), jax.ShapeDtypeStruct((1, wa), F32),
                   jax.ShapeDtypeStruct((1, ws), F32)],
    )(dmerged, attn, y, gl, g_a, g_s)


def _shift_down(main, halo, k):
    rolled = pltpu.roll(main, k, 0)
    row = lax.broadcasted_iota(jnp.int32, main.shape, 0)
    for r in range(k):
        rolled = jnp.where(row == r, halo[8 - k + r:8 - k + r + 1, :], rolled)
    return rolled


def _shift_up(main, halo, k):
    tm = main.shape[0]
    rolled = pltpu.roll(main, tm - k, 0)
    row = lax.broadcasted_iota(jnp.int32, main.shape, 0)
    for r in range(k):
        rolled = jnp.where(row == tm - k + r, halo[r:r + 1, :], rolled)
    return rolled


def _conv(main, halo, w_ref, b_ref):
    return (b_ref[...] + w_ref[0:1, :] * _shift_down(main, halo, 2) + w_ref[1:2, :] * _shift_down(main, halo, 1)
            + w_ref[2:3, :] * main)


def _gate_tiles(s, f):
    return _pick(s, 512, 16), _pick(f, 512, LANES)


def _gate_in_specs(tm, tn, nfb, order, last_row_tile=None):
    hb = tm // 8
    if order == "ij":
        ij = lambda a, b: (a, b)
    elif last_row_tile is None:
        ij = lambda a, b: (b, a)
    else:
        ij = lambda a, b: (last_row_tile - b, a)

    def main(off):
        return pl.BlockSpec((tm, tn), lambda a, b: (ij(a, b)[0], ij(a, b)[1] + off))

    def halo(off):
        return pl.BlockSpec((8, tn), lambda a, b: (jnp.maximum(ij(a, b)[0] * hb - 1, 0), ij(a, b)[1] + off))

    def vec(rows, off):
        return pl.BlockSpec((rows, tn), lambda a, b: (0, ij(a, b)[1] + off))

    return [main(0), main(nfb), halo(0), halo(nfb), vec(3, 0), vec(3, nfb), vec(1, 0), vec(1, nfb)]


def _gate_fwd(up0, conv_w, conv_b, name):
    s, f2 = up0.shape
    f = f2 // 2
    tm, tn = _gate_tiles(s, f)
    nfb = f // tn

    def body(v_ref, g_ref, vh_ref, gh_ref, wv_ref, wg_ref, bv_ref, bg_ref, o_ref):
        top = pl.program_id(0) == 0
        vh = jnp.where(top, 0.0, vh_ref[...])
        gh = jnp.where(top, 0.0, gh_ref[...])
        val = _conv(v_ref[...], vh, wv_ref, bv_ref)
        gate = _conv(g_ref[...], gh, wg_ref, bg_ref)
        o_ref[...] = (_gelu(gate) * val).astype(o_ref.dtype)

    return pl.pallas_call(
        body, name=name, grid=(s // tm, nfb),
        in_specs=_gate_in_specs(tm, tn, nfb, "ij"), out_specs=pl.BlockSpec((tm, tn), lambda i, j: (i, j)),
        out_shape=jax.ShapeDtypeStruct((s, f), BF16),
        compiler_params=_params(24 * tm * tn * 4 + (4 << 20)),
    )(up0, up0, up0, up0, conv_w, conv_w, conv_b, conv_b)


def _gate_bwd(up0, conv_w, conv_b, da, name):
    s, f2 = up0.shape
    f = f2 // 2
    tm, tn = _gate_tiles(s, f)
    nfb, ni = f // tn, s // tm

    def body(v_ref, g_ref, vh_ref, gh_ref, wv_ref, wg_ref, bv_ref, bg_ref, da_ref, dup0_ref, dcb_ref, dcw_ref,
             below_ref):
        step = pl.program_id(1)
        top = step == ni - 1

        @pl.when(step == 0)
        def _():
            dcb_ref[...] = jnp.zeros_like(dcb_ref)
            dcw_ref[...] = jnp.zeros_like(dcw_ref)
            below_ref[...] = jnp.zeros_like(below_ref)

        halos = (jnp.where(top, 0.0, vh_ref[...]), jnp.where(top, 0.0, gh_ref[...]))
        mains = (v_ref[...], g_ref[...])
        w_refs = (wv_ref, wg_ref)
        val = _conv(mains[0], halos[0], wv_ref, bv_ref)
        gate = _conv(mains[1], halos[1], wg_ref, bg_ref)
        dav = da_ref[...]
        act, act_grad = _gelu_and_grad(gate)
        dups = (dav * act, (dav * val) * act_grad)
        for half in range(2):
            dup, w_ref = dups[half], w_refs[half]
            below = below_ref[half]
            dup0_ref[half] = (w_ref[2:3, :] * dup + w_ref[1:2, :] * _shift_up(dup, below, 1)
                              + w_ref[0:1, :] * _shift_up(dup, below, 2)).astype(dup0_ref.dtype)
            below_ref[half] = dup[0:8, :]
            dcb_ref[half] += jnp.sum(dup, axis=0, keepdims=True)
            dcw_ref[half, 0:1, :] += jnp.sum(dup * _shift_down(mains[half], halos[half], 2), axis=0, keepdims=True)
            dcw_ref[half, 1:2, :] += jnp.sum(dup * _shift_down(mains[half], halos[half], 1), axis=0, keepdims=True)
            dcw_ref[half, 2:3, :] += jnp.sum(dup * mains[half], axis=0, keepdims=True)

    return pl.pallas_call(
        body, name=name, grid=(nfb, ni),
        in_specs=_gate_in_specs(tm, tn, nfb, "ji", ni - 1) + [pl.BlockSpec((tm, tn), lambda j, i: (ni - 1 - i, j))],
        out_specs=[pl.BlockSpec((2, tm, tn), lambda j, i: (0, ni - 1 - i, j)),
                   pl.BlockSpec((2, 1, tn), lambda j, i: (0, 0, j)),
                   pl.BlockSpec((2, 3, tn), lambda j, i: (0, 0, j))],
        out_shape=[jax.ShapeDtypeStruct((2, s, f), BF16), jax.ShapeDtypeStruct((2, 1, f), F32),
                   jax.ShapeDtypeStruct((2, 3, f), F32)],
        scratch_shapes=[pltpu.VMEM((2, 8, tn), F32)],
        compiler_params=_params(40 * tm * tn * 4 + (4 << 20)),
    )(up0, up0, up0, up0, conv_w, conv_w, conv_b, conv_b, da)


def _adamw(w, g, m, v, name):
    r, c = w.shape
    tr = _pick(r, max(8, (1 << 19) // max(c, 1) // 8 * 8), 8)
    c1, c2 = 1.0 / (1.0 - ADAM_B1 ** ADAM_STEP), 1.0 / (1.0 - ADAM_B2 ** ADAM_STEP)

    def body(w_ref, g_ref, m_ref, v_ref, d_ref, nm_ref, nv_ref):
        gv = g_ref[...]
        nm = ADAM_B1 * m_ref[...] + (1.0 - ADAM_B1) * gv
        nv = ADAM_B2 * v_ref[...] + (1.0 - ADAM_B2) * (gv * gv)
        nm_ref[...] = nm
        nv_ref[...] = nv
        d_ref[...] = -ADAM_LR * ((nm * c1) / (jnp.sqrt(nv * c2) + ADAM_EPS) + ADAM_WD * w_ref[...])

    spec = pl.BlockSpec((tr, c), lambda i: (i, 0))
    out = jax.ShapeDtypeStruct((r, c), F32)
    return pl.pallas_call(body, name=name, grid=(r // tr,), in_specs=[spec] * 4, out_specs=[spec] * 3,
                          out_shape=[out] * 3, compiler_params=_params(14 * tr * c * 4 + (4 << 20)))(w, g, m, v)


def _adamw_many(ws, gs, ms, vs, name):
    n = len(ws)
    c1, c2 = 1.0 / (1.0 - ADAM_B1 ** ADAM_STEP), 1.0 / (1.0 - ADAM_B2 ** ADAM_STEP)

    def body(*refs):
        w_refs, g_refs, m_refs, v_refs = (refs[i * n:(i + 1) * n] for i in range(4))
        d_refs, nm_refs, nv_refs = (refs[(4 + i) * n:(5 + i) * n] for i in range(3))
        for i in range(n):
            gv = g_refs[i][...]
            nm = ADAM_B1 * m_refs[i][...] + (1.0 - ADAM_B1) * gv
            nv = ADAM_B2 * v_refs[i][...] + (1.0 - ADAM_B2) * (gv * gv)
            nm_refs[i][...] = nm
            nv_refs[i][...] = nv
            d_refs[i][...] = -ADAM_LR * ((nm * c1) / (jnp.sqrt(nv * c2) + ADAM_EPS) + ADAM_WD * w_refs[i][...])

    shapes = [jax.ShapeDtypeStruct(w.shape, F32) for w in ws]
    outs = pl.pallas_call(body, name=name, out_shape=shapes * 3, compiler_params=_params(48 << 20))(
        *ws, *gs, *ms, *vs)
    return outs[:n], outs[n:2 * n], outs[2 * n:]


def _adamw_nd(w, g, m, v, name):
    shape = w.shape
    c = shape[-1]
    outs = _adamw(w.reshape(-1, c), g.reshape(-1, c), m.reshape(-1, c), v.reshape(-1, c), name)
    return [o.reshape(shape) for o in outs]


BIG = ("w_in", "w_glu", "w_out", "w_up", "w_down")
SMALL = ("b_ada", "g_pre_mix", "g_post_mix", "attn_sinks", "lam_re", "lam_im", "log_step", "ssm_b_re", "ssm_b_im",
         "ssm_c_re", "ssm_c_im", "ssm_d", "g_attn_out", "g_ssm_out", "g_pre_ffn", "g_post_ffn", "conv_b")
ORDER = ("w_ada", "b_ada", "g_pre_mix", "g_post_mix", "w_in", "attn_sinks", "lam_re", "lam_im", "log_step",
         "ssm_b_re", "ssm_b_im", "ssm_c_re", "ssm_c_im", "ssm_d", "w_glu", "g_attn_out", "g_ssm_out", "w_out",
         "g_pre_ffn", "g_post_ffn", "w_up", "conv_w", "conv_b", "w_down")
COL_SHARDED = ("w_in", "w_up")


def kernel(x, c, w_ada, b_ada, g_pre_mix, g_post_mix, w_in, attn_sinks, lam_re, lam_im, log_step, ssm_b_re, ssm_b_im, ssm_c_re, ssm_c_im, ssm_d, w_glu, g_attn_out, g_ssm_out, w_out, g_pre_ffn, g_post_ffn, w_up, conv_w, conv_b, w_down, loss_target, m_w_ada, m_b_ada, m_g_pre_mix, m_g_post_mix, m_w_in, m_attn_sinks, m_lam_re, m_lam_im, m_log_step, m_ssm_b_re, m_ssm_b_im, m_ssm_c_re, m_ssm_c_im, m_ssm_d, m_w_glu, m_g_attn_out, m_g_ssm_out, m_w_out, m_g_pre_ffn, m_g_post_ffn, m_w_up, m_conv_w, m_conv_b, m_w_down, v_w_ada, v_b_ada, v_g_pre_mix, v_g_post_mix, v_w_in, v_attn_sinks, v_lam_re, v_lam_im, v_log_step, v_ssm_b_re, v_ssm_b_im, v_ssm_c_re, v_ssm_c_im, v_ssm_d, v_w_glu, v_g_attn_out, v_g_ssm_out, v_w_out, v_g_pre_ffn, v_g_post_ffn, v_w_up, v_conv_w, v_conv_b, v_w_down):
    env = dict(locals())
    W = {n: env[n] for n in ORDER}
    M = {n: env["m_" + n] for n in ORDER}
    V = {n: env["v_" + n] for n in ORDER}

    depth = w_ada.shape[0]
    s, d = x.shape[1], x.shape[2]
    xs0 = x.reshape(s, d)
    tgt = loss_target.reshape(s, d)
    attn_w = d // 2
    ssm_w = d - attn_w
    in_cols = w_in.shape[2] * N_DEV
    kv_dim = (in_cols - attn_w - ssm_w) // 2
    n_q, n_kv = attn_w // HEAD_DIM, kv_dim // HEAD_DIM
    n_grp = ssm_w // SSM_GROUP
    nb = ssm_w // LANES
    f = w_down.shape[1] * N_DEV
    ucol = (attn_w + 2 * kv_dim) // LANES
    t_len = _ssm_chunk(s)
    me = 4 * lax.axis_index("x") + 2 * lax.axis_index("y") + lax.axis_index("c")

    def at_block(ref, idx):
        return ref.at[idx]

    def at_rows(n_rows):
        return lambda ref, idx: ref.at[:, pl.ds(pl.multiple_of(idx * n_rows, 8), n_rows), :]

    def at_cols(n_cols):
        return lambda ref, idx: ref.at[:, :, pl.ds(pl.multiple_of(idx * n_cols, LANES), n_cols)]

    first = _gather_multi([w_in.astype(BF16), conv_w, c],
                          [(N_DEV,) + w_in.shape, (N_DEV,) + conv_w.shape, (N_DEV,) + c.shape],
                          [at_block, at_block, at_block], "ag_first")
    w_in_full = _cols_from_blocks(first[0], "w_in_layout")
    conv_w_full = jnp.transpose(first[1], (1, 2, 0, 3)).reshape(depth, 3, 2 * f)
    c_all = first[2].reshape(N_DEV, d)

    def at_rows2(n_rows):
        return lambda ref, idx: ref.at[pl.ds(pl.multiple_of(idx * n_rows, 8), n_rows), :]

    def at_cols2(n_cols):
        return lambda ref, idx: ref.at[:, pl.ds(pl.multiple_of(idx * n_cols, LANES), n_cols)]

    def whole(ref, idx):
        return ref

    def gather_kind(n):
        return "blk" if n == "w_in" else "cols" if n in COL_SHARDED else "rows"

    def gather_view(n):
        return {"blk": at_block, "cols": at_cols2(W[n].shape[2]), "rows": at_rows2(W[n].shape[1])}[gather_kind(n)]

    def gather_shape(n):
        _, a, b = W[n].shape
        return {"blk": (N_DEV, a, b), "cols": (a, N_DEV * b), "rows": (N_DEV * a, b)}[gather_kind(n)]

    later = [(n, l) for l in range(depth) for n in BIG[1:]]
    later_srcs = [W[n][l].astype(BF16) for n, l in later]
    later_views = [gather_view(n) for n, _ in later]
    me_arr = me.astype(jnp.int32).reshape(1)
    lands = [_place_own(me_arr, src, lax.empty(gather_shape(n), BF16), gather_kind(n), f"ag_own_{n}{l}")
             for (n, l), src in zip(later, later_srcs)]
    ag_started, ag_token = _exchange_start(later_srcs, lands, [whole] * len(later), later_views, "ag_start")

    def weights_arrived(names, l, after, name):
        picks = [later.index((n, l)) for n in names]
        _, got = _exchange_wait([ag_started[i] for i in picks], [after], [whole] * len(picks),
                                [later_views[i] for i in picks], name)
        return dict(zip(names, got))

    c_pad = jnp.pad(c_all, ((0, 16 - N_DEV), (0, 0)))
    n_ada = w_ada.shape[2]
    b_shard = lax.dynamic_slice_in_dim(b_ada, me * n_ada, n_ada, axis=1).reshape(depth, 1, n_ada)
    ada_part, c_act = _ada_fwd(c_pad, w_ada, b_shard, "ada_fwd")
    ada_all = _all_gather(ada_part.reshape(depth * 16, n_ada), "ag_ada").reshape(N_DEV, depth, 16, n_ada)
    ada_me = lax.dynamic_index_in_dim(ada_all, me, axis=2, keepdims=False)
    ada = jnp.transpose(ada_me, (1, 0, 2)).reshape(depth, 6, 1, d) + ag_token[0, 0]

    gp = n_grp * STATE

    def hgp(a):
        return jnp.transpose(a, (2, 0, 1)).reshape(SSM_GROUP, gp)

    ssm = []
    for l in range(depth):
        lr, li = lam_re[l].reshape(1, gp), lam_im[l].reshape(1, gp)
        ls = jnp.repeat(log_step[l], STATE).reshape(1, gp)
        br, bi = hgp(ssm_b_re[l]), hgp(ssm_b_im[l])
        bbr, bbi, tab_r, tab_i = _ssm_params_fwd(lr, li, ls, br, bi, f"ssm_params_fwd{l}")
        bb_re = jnp.transpose(bbr.reshape(SSM_GROUP, n_grp, STATE), (1, 2, 0))
        bb_im = jnp.transpose(bbi.reshape(SSM_GROUP, n_grp, STATE), (1, 2, 0))
        bbd = jnp.concatenate([_to_blocks(bb_re), _to_blocks(bb_im)], axis=2).astype(BF16)
        c_re_t = jnp.transpose(ssm_c_re[l], (0, 2, 1))
        c_im_t = jnp.transpose(ssm_c_im[l], (0, 2, 1))
        ccat = jnp.concatenate([jnp.transpose(_to_blocks(c_re_t), (0, 2, 1)),
                                -jnp.transpose(_to_blocks(c_im_t), (0, 2, 1))], axis=1).astype(BF16)

        def tab(t):
            return t.reshape(TAB_ROWS, nb, BLOCK_STATES)

        apow = jnp.transpose(jnp.concatenate([tab(tab_r), tab(tab_i)], axis=2), (1, 0, 2))
        ssm.append(dict(lr=lr, li=li, ls=ls, br=br, bi=bi, bbd=bbd, ccat=ccat, apow=apow,
                        dskip=ssm_d[l].reshape(1, ssm_w)))

    sinks_pad = jnp.pad(attn_sinks, ((0, 0), (0, LANES - n_q)))

    def vec(a):
        return a.reshape(1, -1)

    saved = []
    fw = [dict() for _ in range(depth)]
    xin = xs0
    for l in range(depth):
        sh_m, sc_m, gt_m, sh_f, sc_f, gt_f = (ada[l, i] for i in range(6))
        p = ssm[l]
        if l == 0:
            h1 = _modnorm_fwd(xin, vec(g_pre_mix[l]), sc_m, sh_m, f"modnorm_mix_fwd{l}")
        proj = _matmul(h1, w_in_full[l], name=f"mm_in{l}")
        attn, lse = _attn_fwd(proj, sinks_pad[l:l + 1], n_q=n_q, n_kv=n_kv, name=f"attn_fwd{l}")
        y, z, xstart = _ssm_fwd(proj, ucol, p["bbd"], p["ccat"], p["dskip"], p["apow"], t_len, name=f"ssm_fwd{l}")
        fw[l].update(weights_arrived(("w_glu", "w_out"), l, z, f"ag_wait_mix{l}"))
        gl = _matmul(z, fw[l]["w_glu"], name=f"mm_glu{l}")
        merged = _merge_fwd(attn, y, gl, vec(g_attn_out[l]), vec(g_ssm_out[l]), f"merge_fwd{l}")
        mix = _matmul(merged, fw[l]["w_out"], name=f"mm_out{l}")
        x2, h2 = _resnorm_modnorm_fwd(xin, mix, vec(g_post_mix[l]), gt_m, vec(g_pre_ffn[l]), sc_f, sh_f,
                                      f"resnorm_mix_fwd{l}")
        fw[l].update(weights_arrived(("w_up",), l, h2, f"ag_wait_up{l}"))
        up0 = _matmul(h2, fw[l]["w_up"], name=f"mm_up{l}")
        cw, cb = conv_w_full[l], vec(conv_b[l])
        act = _gate_fwd(up0, cw, cb, f"gate_fwd{l}")
        fw[l].update(weights_arrived(("w_down",), l, act, f"ag_wait_down{l}"))
        ff = _matmul(act, fw[l]["w_down"], name=f"mm_down{l}")
        saved.append(dict(xin=xin, h1=h1, proj=proj, attn=attn, lse=lse, y=y, z=z, xstart=xstart, gl=gl,
                          merged=merged, mix=mix, x2=x2, h2=h2, up0=up0, act=act, ff=ff))
        if l + 1 < depth:
            xin, h1 = _resnorm_modnorm_fwd(x2, ff, vec(g_post_ffn[l]), gt_f, vec(g_pre_mix[l + 1]), ada[l + 1, 1],
                                           ada[l + 1, 0], f"resnorm_ffn_fwd{l}")
        else:
            dxo, loss_acc = _resnorm_loss(x2, ff, vec(g_post_ffn[l]), gt_f, tgt, "resnorm_ffn_loss")
    loss = lax.psum(loss_acc[0, 0], ("x", "y", "c"))

    grads = {n: [None] * depth for n in ORDER}
    dada = [None] * depth
    big_blocks = {n: [None] * depth for n in BIG}
    seg = jnp.pad(jnp.repeat(jnp.eye(n_grp, dtype=F32), STATE, axis=0), ((0, 0), (0, (-n_grp) % LANES)))

    def part_view(n):
        shp = W[n].shape
        if n == "w_in":
            return at_block, "blk"
        if n in COL_SHARDED:
            return at_cols2(shp[2]), "cols"
        return at_rows2(shp[1]), "rows"

    rs_groups, start_tokens = [], []
    small_order = SMALL + ("conv_w",)
    small_shapes = {n: W[n].shape for n in SMALL}
    small_shapes["conv_w"] = (depth, 3, 2 * f)
    small_started = [None] * depth

    def send_partials(items, name):
        parts = [big_blocks[n][l] for n, l in items]
        lands = [lax.empty((N_DEV,) + W[n].shape[1:], BF16) for n, _ in items]
        started, token = _exchange_start(parts, lands, [part_view(n)[0] for n, _ in items],
                                         [at_block] * len(items), name)
        rs_groups.append((items, started, name))
        start_tokens.append(token)
        return token[0, 0]

    order = jnp.zeros((), F32)
    for l in reversed(range(depth)):
        sh_m, sc_m, gt_m, sh_f, sc_f, gt_f = (ada[l, i] for i in range(6))
        gt_f = gt_f + order
        a, p = saved[l], ssm[l]
        cw, cb = conv_w_full[l], vec(conv_b[l])
        dff, dg, dgt_f = _resnorm_bwd(dxo, a["ff"], vec(g_post_ffn[l]), gt_f, f"resnorm_ffn_bwd{l}")
        grads["g_post_ffn"][l] = dg
        dact = _matmul(dff, fw[l]["w_down"], tb=True, name=f"mm_down_dx{l}")
        big_blocks["w_down"][l] = _matmul(a["act"], dff, ta=True, out_dtype=BF16, name=f"mm_down_dw{l}")
        dup0, dcb, dcw = _gate_bwd(a["up0"], cw, cb, dact, f"gate_bwd{l}")
        grads["conv_b"][l] = dcb.reshape(1, 2 * f)
        grads["conv_w"][l] = jnp.transpose(dcw, (1, 0, 2)).reshape(3, 2 * f)
        dh2 = _matmul(dup0, fw[l]["w_up"], tb=True, a_halves=True, name=f"mm_up_dx{l}")
        big_blocks["w_up"][l] = _matmul(a["h2"], dup0, ta=True, b_halves=True, out_dtype=BF16,
                                        name=f"mm_up_dw{l}")
        if l == 0:
            sc_f = sc_f + send_partials([("w_down", 0), ("w_up", 0)], "rs_start_ffn0")
        dx2, dg, dsc_f, dsh_f, dmix, dg2, dgt_m = _modnorm_resnorm_bwd(
            dh2, a["x2"], vec(g_pre_ffn[l]), sc_f, dxo, a["mix"], vec(g_post_mix[l]), gt_m, f"modnorm_ffn_bwd{l}")
        grads["g_pre_ffn"][l] = dg
        grads["g_post_mix"][l] = dg2
        dmerged = _matmul(dmix, fw[l]["w_out"], tb=True, name=f"mm_out_dx{l}")
        big_blocks["w_out"][l] = _matmul(a["merged"], dmix, ta=True, out_dtype=BF16, name=f"mm_out_dw{l}")
        dattn, dgl, dzd, dga, dgs = _merge_bwd(dmerged, a["attn"], a["y"], a["gl"], vec(g_attn_out[l]),
                                               vec(g_ssm_out[l]), f"merge_bwd{l}")
        grads["g_attn_out"][l], grads["g_ssm_out"][l] = dga, dgs
        dz2 = _matmul(dgl, fw[l]["w_glu"], tb=True, name=f"mm_glu_dx{l}")
        big_blocks["w_glu"][l] = _matmul(a["z"], dgl, ta=True, out_dtype=BF16, name=f"mm_glu_dw{l}")
        dskip = p["dskip"]
        if l == 0:
            dskip = dskip + send_partials([("w_out", 0), ("w_glu", 0)], "rs_start_mix0")
        du, dbbd, dccat, dd, da = _ssm_bwd(a["proj"], ucol, a["y"], dzd, dz2, a["xstart"], p["bbd"], p["ccat"],
                                           dskip, p["apow"], t_len, name=f"ssm_bwd{l}")
        grads["ssm_d"][l] = dd
        hs = BLOCK_STATES
        dbb_re = _from_blocks(dbbd[:, :, :hs], STATE, SSM_GROUP)
        dbb_im = _from_blocks(dbbd[:, :, hs:], STATE, SSM_GROUP)
        dccat_t = jnp.transpose(dccat, (0, 2, 1))
        grads["ssm_c_re"][l] = jnp.transpose(_from_blocks(dccat_t[:, :, :hs], STATE, SSM_GROUP), (0, 2, 1))
        grads["ssm_c_im"][l] = -jnp.transpose(_from_blocks(dccat_t[:, :, hs:], STATE, SSM_GROUP), (0, 2, 1))
        dab_re, dab_im = da[:, 0, :hs].reshape(1, gp), da[:, 0, hs:].reshape(1, gp)
        dlr, dli, dls, dbr, dbi = _ssm_params_bwd(p["lr"], p["li"], p["ls"], p["br"], p["bi"], dab_re, dab_im,
                                                  hgp(dbb_re), hgp(dbb_im), seg, f"ssm_params_bwd{l}")
        grads["lam_re"][l], grads["lam_im"][l], grads["log_step"][l] = dlr, dli, dls[0, :n_grp]
        grads["ssm_b_re"][l] = jnp.transpose(dbr.reshape(SSM_GROUP, n_grp, STATE), (1, 2, 0))
        grads["ssm_b_im"][l] = jnp.transpose(dbi.reshape(SSM_GROUP, n_grp, STATE), (1, 2, 0))
        dq, dk, dv, dsink = _attn_bwd(a["proj"], sinks_pad[l:l + 1], a["attn"], a["lse"], dattn,
                                      n_q=n_q, n_kv=n_kv, name=f"attn_bwd{l}")
        grads["attn_sinks"][l] = dsink[0, :n_q]
        dproj = jnp.concatenate([dq, dk, dv, du], axis=1).astype(BF16)
        dh1 = _matmul(dproj, w_in_full[l], tb=True, name=f"mm_in_dx{l}")
        big_blocks["w_in"][l] = _blocks_from_cols(_matmul(a["h1"], dproj, ta=True, name=f"mm_in_dw{l}"),
                                                  f"w_in_grad_layout{l}")
        dxo, dg, dsc_m, dsh_m = _modnorm_bwd(dh1, a["xin"], vec(g_pre_mix[l]), sc_m, dx2, f"modnorm_mix_bwd{l}")
        grads["g_pre_mix"][l] = dg
        dada[l] = jnp.concatenate([dsh_m, dsc_m, dgt_m, dsh_f, dsc_f, dgt_f], axis=1)
        if l > 0:
            order = send_partials([(n, l) for n in reversed(BIG)], f"rs_start_layer{l}")
        else:
            order = order + send_partials([("w_in", 0)], "rs_start_in0")
        spack = _pack([dada[l]] + [grads[n][l] for n in small_order[1:]], F32, 1024)
        started, token = _exchange_start([spack], [lax.empty((N_DEV,) + spack.shape, F32)], [whole], [at_block],
                                         f"small_start{l}")
        small_started[l] = started
        start_tokens.append(token)
        order = order + token[0, 0]
    grad_x = dxo.reshape(x.shape)

    delta, new_m, new_v = {}, {}, {}
    stacked = {n: None for n in BIG}
    landed_layers = {n: 0 for n in BIG}
    after = [dxo] + start_tokens
    for items, started, name in rs_groups:
        mine, landed = _exchange_wait(started, after, [part_view(n)[0] for n, _ in items], [at_block] * len(items),
                                      name.replace("start", "wait"))
        for (n, l), part, slots in zip(items, mine, landed):
            stacked[n] = _sum_slots_own(me_arr, slots, part, part_view(n)[1], f"rs_sum_{n}{l}", layer=l,
                                        n_layers=depth, stacked=stacked[n])
            landed_layers[n] += 1
            if landed_layers[n] == depth:
                grads[n] = stacked[n]
                delta[n], new_m[n], new_v[n] = _adamw_nd(W[n], grads[n], M[n], V[n], f"adamw_{n}")
                after.append(delta[n])

    n_cw = conv_w.shape[2]
    small_sums, dada_rows = [None] * depth, [None] * depth
    for l in reversed(range(depth)):
        mine, landed = _exchange_wait(small_started[l], after, [whole], [at_block], f"small_wait{l}")
        ssum = _sum_slots_own(me_arr, landed[0], mine[0], "self", f"sum_small{l}").reshape(-1)
        small_sums[l] = _unpack(ssum, [small_shapes[n][1:] for n in small_order])
        slot = lax.broadcasted_iota(jnp.int32, (N_DEV, 6 * d), 0)
        dada_rows[l] = jnp.where(slot == me, mine[0].reshape(-1)[:6 * d][None],
                                 landed[0].reshape(N_DEV, -1)[:, :6 * d])
    for i, n in enumerate(small_order):
        grads[n] = jnp.stack([small_sums[l][i] for l in range(depth)])
    grads["conv_w"] = lax.dynamic_slice_in_dim(grads["conv_w"], me * n_cw, n_cw, axis=2)
    dada_all = jnp.stack(dada_rows, axis=1)
    dada_shard = lax.dynamic_slice_in_dim(dada_all, me * n_ada, n_ada, axis=2)
    kp = LANES
    dada_pad = jnp.pad(jnp.transpose(dada_shard, (1, 0, 2)), ((0, 0), (0, kp - N_DEV), (0, 0)))
    act_t = jnp.pad(jnp.transpose(c_act[:N_DEV]), ((0, 0), (0, kp - N_DEV)))
    grads["w_ada"] = _ada_wgrad(act_t, dada_pad, "ada_wgrad")

    delta["w_ada"], new_m["w_ada"], new_v["w_ada"] = _adamw_nd(W["w_ada"], grads["w_ada"], M["w_ada"], V["w_ada"],
                                                                "adamw_w_ada")

    def lane_friendly(a):
        return a.reshape(-1, 1024) if a.ndim > 2 and a.shape[-1] < LANES and a.size % 1024 == 0 else a

    rest = SMALL + ("conv_w",)
    outs = _adamw_many(*[[lane_friendly(t[n]) for n in rest] for t in (W, grads, M, V)], "adamw_small")
    for tgt_d, vals in zip((delta, new_m, new_v), outs):
        for n, val in zip(rest, vals):
            tgt_d[n] = val.reshape(W[n].shape)

    return (loss, grad_x, *[grads[n] for n in ORDER], *[delta[n] for n in ORDER],
            *[new_m[n] for n in ORDER], *[new_v[n] for n in ORDER])
```
